```python
import math
import jax, jax.numpy as jnp
from jax import lax
import numpy as np

D_MODEL = 1024
BATCH = 2
SEQ = 8192
DEPTH = 2
DEC_BATCH = 128
DEC_SEQ = 1
PAST_LEN = 16384
PAGE_SIZE = 128

N_A_LAYERS = DEPTH // 2
N_B_LAYERS = DEPTH - N_A_LAYERS
MIX_WIDTH = D_MODEL
MEM_LEN = 256
MEM_HEADS = 4
MEM_HEAD_DIM = MIX_WIDTH // 4 // MEM_HEADS
MEM_Q_WIDTH = MEM_HEADS * MEM_HEAD_DIM
GLA_HEADS = 4
GLA_V_WIDTH = MIX_WIDTH - MEM_Q_WIDTH
GLA_K_WIDTH = GLA_V_WIDTH // 2
GLA_DK = GLA_K_WIDTH // GLA_HEADS
GLA_DV = GLA_V_WIDTH // GLA_HEADS
GLA_GATE_RANK = 16
GLA_GATE_NORM = 16.0
GLA_CHUNK = 64
SWA_HEAD_DIM = 64
SWA_HEADS = GLA_V_WIDTH // SWA_HEAD_DIM
SWA_KV_HEADS = 4
SWA_GROUP = SWA_HEADS // SWA_KV_HEADS
SWA_Q_WIDTH = SWA_HEADS * SWA_HEAD_DIM
SWA_KV_WIDTH = SWA_KV_HEADS * SWA_HEAD_DIM
WINDOW = 128
N_BUCKETS = 32
MAX_DISTANCE = 128
D_FF = 4 * D_MODEL
EPS = 1e-6
IN_A_SIZES = (GLA_K_WIDTH, GLA_K_WIDTH, GLA_V_WIDTH, GLA_V_WIDTH, GLA_GATE_RANK, MEM_Q_WIDTH)
IN_A = sum(IN_A_SIZES)
IN_B_SIZES = (SWA_Q_WIDTH, MEM_Q_WIDTH)
IN_B = sum(IN_B_SIZES)

kernel_name = "yoco_gla_swa_sink_memory_decoder_step"


def rmsnorm(x, g):
    xf = x.astype(jnp.float32)
    y = xf * lax.rsqrt(jnp.mean(xf * xf, axis=-1, keepdims=True) + EPS)
    return (y * g.astype(jnp.float32)).astype(x.dtype)


def split_cols(x, sizes):
    idx = np.cumsum(np.array(sizes))[:-1].tolist()
    return jnp.split(x, idx, axis=-1)


def squared_relu_mlp(h, w_up, w_down):
    return jnp.square(jax.nn.relu(h @ w_up)) @ w_down


def mem_kv(mem, g, w):
    B = mem.shape[0]
    k, v = jnp.split(rmsnorm(mem, g) @ w, 2, axis=-1)
    shp = (B, MEM_LEN, MEM_HEADS, MEM_HEAD_DIM)
    return k.reshape(shp), v.reshape(shp)


def mem_attend(qm, mem_k, mem_v):
    B, L, _ = qm.shape
    q = qm.reshape(B, L, MEM_HEADS, MEM_HEAD_DIM)
    s = jnp.einsum('blhd,bmhd->bhlm', q, mem_k).astype(jnp.float32) * (MEM_HEAD_DIM ** -0.5)
    p = jax.nn.softmax(s, axis=-1).astype(mem_v.dtype)
    return jnp.einsum('bhlm,bmhd->blhd', p, mem_v).reshape(B, L, MEM_Q_WIDTH)


def gla_chunk_step(S, inp):
    q, k, v, g = inp
    C = q.shape[1]
    b = jnp.cumsum(g, axis=1)
    causal = jnp.tril(jnp.ones((C, C), dtype=bool))
    diff = b[:, :, None] - b[:, None, :]
    decay = jnp.exp(jnp.where(causal[None, :, :, None, None], diff, -jnp.inf))
    attn = jnp.einsum('bthd,bshd,btshd->bhts', q, k, decay)
    o = jnp.einsum('bhts,bshv->bthv', attn, v) + jnp.einsum('bthd,bhdv->bthv', q * jnp.exp(b), S)
    b_last = b[:, -1]
    S_new = jnp.exp(b_last)[..., None] * S + jnp.einsum(
        'bshd,bshv->bhdv', k * jnp.exp(b_last[:, None] - b), v)
    return S_new, o


def gla_recurrence(q, k, v, g, S0):
    B, L, H, _ = q.shape
    C = math.gcd(L, GLA_CHUNK)
    n = L // C

    def to_chunks(t):
        return jnp.moveaxis(t.astype(jnp.float32).reshape((B, n, C) + t.shape[2:]), 1, 0)

    S, o = lax.scan(gla_chunk_step, S0.astype(jnp.float32),
                    (to_chunks(q), to_chunks(k), to_chunks(v), to_chunks(g)))
    o = jnp.moveaxis(o, 0, 1).reshape(B, L, H, GLA_DV)
    return o, S


def rel_bucket(dist):
    max_exact = N_BUCKETS // 2
    n = jnp.maximum(dist, 0)
    nf = jnp.maximum(n, 1).astype(jnp.float32)
    large = max_exact + (jnp.log(nf / max_exact) / math.log(MAX_DISTANCE / max_exact)
                         * (N_BUCKETS - max_exact)).astype(jnp.int32)
    large = jnp.minimum(large, N_BUCKETS - 1)
    return jnp.where(n < max_exact, n, large)


def t5_bias(dist, rel_bias):
    b = jnp.transpose(rel_bias[rel_bucket(dist)], (2, 0, 1)).astype(jnp.float32)
    return b.reshape((SWA_KV_HEADS, SWA_GROUP) + dist.shape)


def sink_attention(q, k, v, bias, valid, sinks):
    s = jnp.einsum('...qhgd,...khd->...hgqk', q, k).astype(jnp.float32) * (SWA_HEAD_DIM ** -0.5) + bias
    s = jnp.where(valid, s, -jnp.inf)
    sink = sinks.astype(jnp.float32).reshape(SWA_KV_HEADS, SWA_GROUP)[:, :, None, None]
    m = jnp.maximum(jnp.max(s, axis=-1, keepdims=True), sink)
    p = jnp.exp(s - m)
    p = (p / (jnp.sum(p, axis=-1, keepdims=True) + jnp.exp(sink - m))).astype(v.dtype)
    return jnp.einsum('...hgqk,...khd->...qhgd', p, v)


def swa_prompt(q, k, v, rel_bias, sinks):
    B, L = q.shape[:2]
    nb = L // WINDOW
    qb = q.reshape(B, nb, WINDOW, SWA_KV_HEADS, SWA_GROUP, SWA_HEAD_DIM)

    def band_keys(t):
        tb = t.reshape(B, nb, WINDOW, SWA_KV_HEADS, SWA_HEAD_DIM)
        prev = jnp.pad(tb, ((0, 0), (1, 0), (0, 0), (0, 0), (0, 0)))[:, :-1]
        return jnp.concatenate([prev, tb], axis=2)

    kk, vv = band_keys(k), band_keys(v)
    qi = jnp.arange(WINDOW)[:, None] + WINDOW
    kj = jnp.arange(2 * WINDOW)[None, :]
    dist = qi - kj
    band = (dist >= 0) & (dist < WINDOW)
    blk = jnp.arange(nb)[:, None, None]
    valid = band[None] & ((blk > 0) | (kj >= WINDOW)[None])
    o = sink_attention(qb, kk, vv, t5_bias(dist, rel_bias), valid[:, None, None], sinks)
    return o.reshape(B, L, SWA_Q_WIDTH)


def swa_sample(q, k_all, v_all, rel_bias, sinks):
    B, Lq = q.shape[:2]
    qq = q.reshape(B, Lq, SWA_KV_HEADS, SWA_GROUP, SWA_HEAD_DIM)
    dist = (jnp.arange(Lq)[:, None] + WINDOW) - jnp.arange(WINDOW + Lq)[None, :]
    valid = (dist >= 0) & (dist < WINDOW)
    o = sink_attention(qq, k_all, v_all, t5_bias(dist, rel_bias), valid, sinks)
    return o.reshape(B, Lq, SWA_Q_WIDTH)


def run_trunk(x, mem_k, mem_v, gla_state, buf_k, buf_v,
              norm_mix_pre, norm_mix_post, norm_ffn_pre, norm_ffn_post,
              w_in_a, w_gate_up, b_gate, gla_norm, w_in_b, sinks,
              norm_kv, w_kv, rel_bias, w_out, w_ffn_up, w_ffn_down):
    B, L, _ = x.shape
    new_gla = []
    k_all = v_all = new_k = new_v = None
    for l in range(DEPTH):
        if l == N_A_LAYERS:
            ks, vs = jnp.split(rmsnorm(x, norm_kv) @ w_kv, 2, axis=-1)
            ks = ks.reshape(B, L, SWA_KV_HEADS, SWA_HEAD_DIM)
            vs = vs.reshape(B, L, SWA_KV_HEADS, SWA_HEAD_DIM)
            if buf_k is None:
                k_all, v_all = ks, vs
            else:
                k_all = jnp.concatenate([buf_k, ks], axis=1)
                v_all = jnp.concatenate([buf_v, vs], axis=1)
            new_k, new_v = k_all[:, -WINDOW:], v_all[:, -WINDOW:]
        h = rmsnorm(x, norm_mix_pre[l])
        if l < N_A_LAYERS:
            a = l
            q, k, v, r, glr, qm = split_cols(h @ w_in_a[a], IN_A_SIZES)
            q = q.reshape(B, L, GLA_HEADS, GLA_DK) * (GLA_DK ** -0.5)
            k = k.reshape(B, L, GLA_HEADS, GLA_DK)
            v = v.reshape(B, L, GLA_HEADS, GLA_DV)
            g = jax.nn.log_sigmoid((glr @ w_gate_up[a] + b_gate[a]).astype(jnp.float32)) / GLA_GATE_NORM
            g = g.reshape(B, L, GLA_HEADS, GLA_DK)
            o, S = gla_recurrence(q, k, v, g, gla_state[a])
            new_gla.append(S.astype(gla_state.dtype))
            o_main = rmsnorm(o.astype(h.dtype), gla_norm[a]).reshape(B, L, GLA_V_WIDTH) * jax.nn.silu(r)
        else:
            bl = l - N_A_LAYERS
            q, qm = split_cols(h @ w_in_b[bl], IN_B_SIZES)
            if buf_k is None:
                o_main = swa_prompt(q, k_all, v_all, rel_bias, sinks[bl])
            else:
                o_main = swa_sample(q, k_all, v_all, rel_bias, sinks[bl])
        o_mem = mem_attend(qm, mem_k[l], mem_v[l])
        mix = jnp.concatenate([o_main, o_mem], axis=-1) @ w_out[l]
        x = x + rmsnorm(mix, norm_mix_post[l])
        f = squared_relu_mlp(rmsnorm(x, norm_ffn_pre[l]), w_ffn_up[l], w_ffn_down[l])
        x = x + rmsnorm(f, norm_ffn_post[l])
    return x, jnp.stack(new_gla), new_k, new_v


def setup_inputs(seed: int = 0) -> dict:
    key = jax.random.key(seed)
    ks = iter(jax.random.split(key, 40))

    def nrm(shape, scale):
        return scale * jax.random.normal(next(ks), shape, jnp.float32)

    def gain(shape):
        return 1.0 + 0.05 * jax.random.normal(next(ks), shape, jnp.float32)

    return {
        "x_prompt": nrm((BATCH, SEQ, D_MODEL), 1.0),
        "x_sample": nrm((DEC_BATCH, DEC_SEQ, D_MODEL), 1.0),
        "state_gla": nrm((N_A_LAYERS, DEC_BATCH, GLA_HEADS, GLA_DK, GLA_DV), 0.5),
        "cache_swa_k": nrm((DEC_BATCH, WINDOW, SWA_KV_HEADS, SWA_HEAD_DIM), 1.0),
        "cache_swa_v": nrm((DEC_BATCH, WINDOW, SWA_KV_HEADS, SWA_HEAD_DIM), 1.0),
        "cache_mem_k": nrm((DEPTH, DEC_BATCH, MEM_LEN, MEM_HEADS, MEM_HEAD_DIM), 1.0),
        "cache_mem_v": nrm((DEPTH, DEC_BATCH, MEM_LEN, MEM_HEADS, MEM_HEAD_DIM), 1.0),
        "mem_prompt": nrm((BATCH, MEM_LEN, D_MODEL), 1.0),
        "norm_mix_pre": gain((DEPTH, D_MODEL)),
        "norm_mix_post": gain((DEPTH, D_MODEL)),
        "norm_ffn_pre": gain((DEPTH, D_MODEL)),
        "norm_ffn_post": gain((DEPTH, D_MODEL)),
        "norm_mem": gain((DEPTH, D_MODEL)),
        "w_mem_kv": nrm((DEPTH, D_MODEL, 2 * MEM_Q_WIDTH), D_MODEL ** -0.5),
        "w_in_a": nrm((N_A_LAYERS, D_MODEL, IN_A), D_MODEL ** -0.5),
        "w_gate_up": nrm((N_A_LAYERS, GLA_GATE_RANK, GLA_K_WIDTH), GLA_GATE_RANK ** -0.5),
        "b_gate": nrm((N_A_LAYERS, GLA_K_WIDTH), 0.1),
        "gla_norm": gain((N_A_LAYERS, GLA_DV)),
        "w_in_b": nrm((N_B_LAYERS, D_MODEL, IN_B), D_MODEL ** -0.5),
        "sinks": nrm((N_B_LAYERS, SWA_HEADS), 1.0),
        "norm_kv": gain((D_MODEL,)),
        "w_kv": nrm((D_MODEL, 2 * SWA_KV_WIDTH), D_MODEL ** -0.5),
        "rel_bias": nrm((N_BUCKETS, SWA_HEADS), 0.5),
        "w_out": nrm((DEPTH, MIX_WIDTH, D_MODEL), MIX_WIDTH ** -0.5),
        "w_ffn_up": nrm((DEPTH, D_MODEL, D_FF), D_MODEL ** -0.5),
        "w_ffn_down": nrm((DEPTH, D_FF, D_MODEL), D_FF ** -0.5),
    }


def reference(x_prompt, x_sample, state_gla, cache_swa_k, cache_swa_v, cache_mem_k, cache_mem_v,
              mem_prompt, norm_mix_pre, norm_mix_post, norm_ffn_pre, norm_ffn_post, norm_mem,
              w_mem_kv, w_in_a, w_gate_up, b_gate, gla_norm, w_in_b, sinks, norm_kv, w_kv,
              rel_bias, w_out, w_ffn_up, w_ffn_down):
    weights = (norm_mix_pre, norm_mix_post, norm_ffn_pre, norm_ffn_post,
               w_in_a, w_gate_up, b_gate, gla_norm, w_in_b, sinks,
               norm_kv, w_kv, rel_bias, w_out, w_ffn_up, w_ffn_down)
    mk, mv = [], []
    for l in range(DEPTH):
        k_l, v_l = mem_kv(mem_prompt, norm_mem[l], w_mem_kv[l])
        mk.append(k_l)
        mv.append(v_l)
    cache_mem_k_prompt = jnp.stack(mk)
    cache_mem_v_prompt = jnp.stack(mv)
    gla0 = jnp.zeros((N_A_LAYERS, BATCH, GLA_HEADS, GLA_DK, GLA_DV), x_prompt.dtype)
    y_prompt, state_gla_prompt, cache_swa_k_prompt, cache_swa_v_prompt = run_trunk(
        x_prompt, cache_mem_k_prompt, cache_mem_v_prompt, gla0, None, None, *weights)
    y_sample, state_gla_sample, cache_swa_k_sample, cache_swa_v_sample = run_trunk(
        x_sample, cache_mem_k, cache_mem_v, state_gla, cache_swa_k, cache_swa_v, *weights)
    return (y_prompt, y_sample, state_gla_prompt, state_gla_sample,
            cache_swa_k_prompt, cache_swa_v_prompt, cache_swa_k_sample, cache_swa_v_sample,
            cache_mem_k_prompt, cache_mem_v_prompt)
```

```python
import functools
import math

import numpy as np
import jax
import jax.numpy as jnp
from jax import lax
from jax.experimental import pallas as pl
from jax.experimental.pallas import tpu as pltpu

F32 = jnp.float32
BF16 = jnp.bfloat16

D = 1024
D_FF = 4 * D
N_MEM = 256
MEM_H = 4
MEM_HD = 64
MEM_W = MEM_H * MEM_HD
GLA_H = 4
GLA_DK = 96
GLA_DV = 192
GLA_KW = GLA_H * GLA_DK
GLA_VW = GLA_H * GLA_DV
GATE_RANK = 16
GATE_NORM = 16.0
SWA_HD = 64
SWA_KVH = 4
SWA_G = 3
SWA_QW = SWA_KVH * SWA_G * SWA_HD
SWA_KVW = SWA_KVH * SWA_HD
WINDOW = 128
N_BUCKETS = 32
MAX_DISTANCE = 128
EPS = 1e-6

GLA_CHUNK = 64
GLA_SUB = 16
TM_PROMPT = 512
FF_CHUNK = 512
ROWS_PER_STEP = 8
V7X_VMEM_LIMIT = 56 * 1024 * 1024
NEG_INF = float("-inf")


def _bf(x):
    return x.astype(BF16)


def _dot(a, b):
    return jnp.dot(a, b, preferred_element_type=F32)


def _dot_nt(a, b):
    return lax.dot_general(a, b, (((1,), (1,)), ((), ())), preferred_element_type=F32)


def _dot_tn(a, b):
    return lax.dot_general(a, b, (((0,), (0,)), ((), ())), preferred_element_type=F32)


def _rms(x, g):
    return x * lax.rsqrt(jnp.mean(x * x, axis=-1, keepdims=True) + EPS) * g


def _split3(x):
    x1 = _bf(x)
    r1 = x - x1.astype(F32)
    x2 = _bf(r1)
    x3 = _bf(r1 - x2.astype(F32))
    return x1, x2, x3


def _exact_dot(sel, x):
    x1, x2, x3 = _split3(x)
    return _dot(sel, x1) + _dot(sel, x2) + _dot(sel, x3)


def _log_sigmoid(z):
    return jnp.minimum(z, 0.0) - jnp.log1p(jnp.exp(-jnp.abs(z)))


def _silu(z):
    return z * (1.0 / (1.0 + jnp.exp(-z)))


def _iota(shape, dim):
    return lax.broadcasted_iota(jnp.int32, shape, dim)


def _gla_k_head(lane):
    one = jnp.int32(1)
    zero = jnp.int32(0)
    return (jnp.where(lane >= GLA_DK, one, zero) + jnp.where(lane >= 2 * GLA_DK, one, zero)
            + jnp.where(lane >= 3 * GLA_DK, one, zero))


def _gla_v_head(lane):
    one = jnp.int32(1)
    zero = jnp.int32(0)
    return (jnp.where(lane >= GLA_DV, one, zero) + jnp.where(lane >= 2 * GLA_DV, one, zero)
            + jnp.where(lane >= 3 * GLA_DV, one, zero))


def _full_spec(a):
    nd = a.ndim
    return pl.BlockSpec(a.shape, lambda *_: (0,) * nd)


def _params(sem):
    return pltpu.CompilerParams(dimension_semantics=sem, vmem_limit_bytes=V7X_VMEM_LIMIT)


def _mem_softmax_pv(qm, kbd_ref, vbd_ref):
    s = _dot(_bf(qm), kbd_ref[...]) * (MEM_HD ** -0.5)
    out = None
    for h in range(MEM_H):
        sh = s[:, h * N_MEM:(h + 1) * N_MEM]
        mx = jnp.max(sh, axis=-1, keepdims=True)
        e = jnp.exp(sh - mx)
        p = e / jnp.sum(e, axis=-1, keepdims=True)
        t = _dot(_bf(p), vbd_ref[h * N_MEM:(h + 1) * N_MEM, :])
        out = t if out is None else out + t
    return out


def _memkv_kernel(mem_ref, g_ref, w_ref, k_ref, v_ref, kbd_ref, vbd_ref):
    h = _bf(_rms(mem_ref[0], g_ref[0]))
    kv = _dot(h, w_ref[0])
    k = kv[:, :MEM_W]
    v = kv[:, MEM_W:]
    k_ref[0, 0] = k
    v_ref[0, 0] = v
    kt = k.T
    kt4 = jnp.concatenate([kt, kt, kt, kt], axis=1)
    keep_k = (_iota((MEM_W, MEM_H * N_MEM), 0) >> 6) == (_iota((MEM_W, MEM_H * N_MEM), 1) >> 8)
    kbd_ref[0, 0] = _bf(jnp.where(keep_k, kt4, 0.0))
    v4 = jnp.concatenate([v, v, v, v], axis=0)
    keep_v = (_iota((MEM_H * N_MEM, MEM_W), 0) >> 8) == (_iota((MEM_H * N_MEM, MEM_W), 1) >> 6)
    vbd_ref[0, 0] = _bf(jnp.where(keep_v, v4, 0.0))


def _memkv_call(mem, norm_mem, w_mem_kv):
    nb = mem.shape[0]
    nl = w_mem_kv.shape[0]
    g = norm_mem.reshape(nl, 1, D)
    w = _bf(w_mem_kv)
    return pl.pallas_call(
        _memkv_kernel,
        grid=(nl, nb),
        in_specs=[
            pl.BlockSpec((1, N_MEM, D), lambda l, b: (b, 0, 0)),
            pl.BlockSpec((1, 1, D), lambda l, b: (l, 0, 0)),
            pl.BlockSpec((1, D, 2 * MEM_W), lambda l, b: (l, 0, 0)),
        ],
        out_specs=[
            pl.BlockSpec((1, 1, N_MEM, MEM_W), lambda l, b: (l, b, 0, 0)),
            pl.BlockSpec((1, 1, N_MEM, MEM_W), lambda l, b: (l, b, 0, 0)),
            pl.BlockSpec((1, 1, MEM_W, MEM_H * N_MEM), lambda l, b: (l, b, 0, 0)),
            pl.BlockSpec((1, 1, MEM_H * N_MEM, MEM_W), lambda l, b: (l, b, 0, 0)),
        ],
        out_shape=[
            jax.ShapeDtypeStruct((nl, nb, N_MEM, MEM_W), F32),
            jax.ShapeDtypeStruct((nl, nb, N_MEM, MEM_W), F32),
            jax.ShapeDtypeStruct((nl, nb, MEM_W, MEM_H * N_MEM), BF16),
            jax.ShapeDtypeStruct((nl, nb, MEM_H * N_MEM, MEM_W), BF16),
        ],
        compiler_params=_params(("arbitrary", "arbitrary")),
        name="mem_kv",
    )(mem, g, w)


def _proj_a(x, gpre_ref, wqk_ref, wv_ref, wr_ref, wmisc_ref, wg_ref, bg_ref):
    h = _bf(_rms(x, gpre_ref[...]))
    qk = _dot(h, wqk_ref[...])
    q = qk[:, :GLA_KW] * (GLA_DK ** -0.5)
    k = qk[:, GLA_KW:]
    v = _dot(h, wv_ref[...])
    r = _dot(h, wr_ref[...])
    misc = _dot(h, wmisc_ref[...])
    glr = misc[:, :128]
    qm = misc[:, 128:]
    g = _log_sigmoid(_dot(_bf(glr), wg_ref[...]) + bg_ref[...]) * (1.0 / GATE_NORM)
    return q, k, g, v, r, qm


def _gla_out_gate(o, r, glan_ref):
    vh = _gla_v_head(_iota(o.shape, 1))
    o2 = o * o
    scale = jnp.zeros_like(o)
    for h in range(GLA_H):
        ss = jnp.sum(jnp.where(vh == h, o2, 0.0), axis=-1, keepdims=True) * (1.0 / GLA_DV)
        scale = jnp.where(vh == h, lax.rsqrt(ss + EPS), scale)
    return o * scale * glan_ref[...] * _silu(r)


def _mix_residual(x, o_main, o_mem, wo1_ref, wo2_ref, gpost_ref):
    mix = _dot(_bf(o_main), wo1_ref[...]) + _dot(_bf(o_mem), wo2_ref[...])
    return x + _rms(mix, gpost_ref[...])


def _mixer_a_kernel(x_ref, gpre_ref, wqk_ref, wv_ref, wr_ref, wmisc_ref, wg_ref, bg_ref, glan_ref,
                    kbd_ref, vbd_ref, wo1_ref, wo2_ref, gpost_ref,
                    xo_ref, st_ref,
                    q_s, k_s, g_s, v_s, o_s, state_s, p_s):
    C = GLA_CHUNK
    tm = x_ref.shape[1]
    t = pl.program_id(1)

    @pl.when(t == 0)
    def _():
        state_s[...] = jnp.zeros_like(state_s)

    x = x_ref[0]
    q, k, g, v, r, qm = _proj_a(x, gpre_ref, wqk_ref, wv_ref, wr_ref, wmisc_ref, wg_ref, bg_ref)
    q_s[...] = q
    k_s[...] = k
    g_s[...] = g
    v_s[...] = v

    ri = _iota((C, GLA_KW), 0)
    kh = _gla_k_head(_iota((C, GLA_KW), 1))
    khcat = jnp.concatenate([kh, kh, kh], axis=1)
    vh = _gla_v_head(_iota((C, GLA_VW), 1))
    tri = _bf(jnp.where(_iota((C, C), 0) >= _iota((C, C), 1), 1.0, 0.0))
    d_rs = _iota((C, 4 * C), 0) - (_iota((C, 4 * C), 1) & (C - 1))
    band = jnp.where((d_rs >= 0) & (d_rs <= (_iota((C, 4 * C), 0) & (GLA_SUB - 1))), d_rs, -1)
    ebc = _bf(jnp.where(_gla_k_head(_iota((GLA_KW, 4 * C), 0)) == (_iota((GLA_KW, 4 * C), 1) >> 6),
                        1.0, 0.0))
    blockmask = _gla_v_head(_iota((GLA_VW, GLA_KW), 0)) == _gla_k_head(_iota((GLA_VW, GLA_KW), 1))

    def chunk(c, carry):
        r0 = pl.multiple_of(c * C, C)
        qc = q_s[pl.ds(r0, C), :]
        kc = k_s[pl.ds(r0, C), :]
        gc = g_s[pl.ds(r0, C), :]
        vc = v_s[pl.ds(r0, C), :]
        b = _exact_dot(tri, gc)

        ref1 = jnp.broadcast_to(b[31:32, :], b.shape)
        ref2 = jnp.where(ri < 32, jnp.broadcast_to(b[15:16, :], b.shape),
                         jnp.broadcast_to(b[47:48, :], b.shape))
        q1 = qc * jnp.exp(jnp.minimum(b - ref1, 0.0))
        k1 = kc * jnp.exp(jnp.minimum(ref1 - b, 0.0))
        q2 = qc * jnp.exp(jnp.minimum(b - ref2, 0.0))
        k2 = kc * jnp.exp(jnp.minimum(ref2 - b, 0.0))
        qcat = jnp.concatenate([
            jnp.where(ri >= 32, q1, 0.0),
            jnp.where((ri >= 16) & (ri < 32), q2, 0.0),
            jnp.where(ri >= 48, q2, 0.0)], axis=1)
        kcat = jnp.concatenate([
            jnp.where(ri < 32, k1, 0.0),
            jnp.where(ri < 16, k2, 0.0),
            jnp.where((ri >= 32) & (ri < 48), k2, 0.0)], axis=1)
        kst = _bf(jnp.concatenate([jnp.where(khcat == h, kcat, 0.0) for h in range(GLA_H)], axis=0))
        a_off = _dot_nt(_bf(qcat), kst)

        for dlt in range(GLA_SUB):
            if dlt == 0:
                pr = qc * kc
            else:
                kd = pltpu.roll(kc, dlt, 0)
                bd = pltpu.roll(b, dlt, 0)
                pr = qc * kd * jnp.exp(jnp.minimum(b - bd, 0.0))
            p_s[dlt * C:(dlt + 1) * C, :] = _bf(pr)
        rsum = _dot(p_s[...], ebc)
        a_diag = jnp.zeros((C, 4 * C), F32)
        for dlt in range(GLA_SUB):
            a_diag = jnp.where(band == dlt, rsum[dlt * C:(dlt + 1) * C, :], a_diag)

        vst = _bf(jnp.concatenate([jnp.where(vh == h, vc, 0.0) for h in range(GLA_H)], axis=0))
        o_intra = _dot(_bf(a_off + a_diag), vst)
        st = state_s[...]
        o_inter = _dot_nt(_bf(qc * jnp.exp(b)), _bf(st))
        o_s[pl.ds(r0, C), :] = o_intra + o_inter

        blast = jnp.broadcast_to(b[C - 1:C, :], b.shape)
        kdec = kc * jnp.exp(blast - b)
        kvt = _dot_tn(_bf(vc), _bf(kdec))
        decay = jnp.exp(b[C - 1:C, :])
        state_s[...] = st * decay + jnp.where(blockmask, kvt, 0.0)
        return carry

    lax.fori_loop(0, tm // C, chunk, 0)
    st_ref[0] = state_s[...]

    o_main = _gla_out_gate(o_s[...], r, glan_ref)
    o_mem = _mem_softmax_pv(qm, kbd_ref.at[0], vbd_ref.at[0])
    xo_ref[0] = _mix_residual(x, o_main, o_mem, wo1_ref, wo2_ref, gpost_ref)


def _mixer_a_call(x, wa, kbd, vbd):
    nb, seq, _ = x.shape
    tm = TM_PROMPT
    weights = [wa["gpre"], wa["wqk"], wa["wv"], wa["wr"], wa["wmisc"], wa["wg"], wa["bg"], wa["glan"]]
    tail = [wa["wo1"], wa["wo2"], wa["gpost"]]
    return pl.pallas_call(
        _mixer_a_kernel,
        grid=(nb, seq // tm),
        in_specs=([pl.BlockSpec((1, tm, D), lambda b, t: (b, t, 0))]
                  + [_full_spec(w) for w in weights]
                  + [pl.BlockSpec((1, MEM_W, MEM_H * N_MEM), lambda b, t: (b, 0, 0)),
                     pl.BlockSpec((1, MEM_H * N_MEM, MEM_W), lambda b, t: (b, 0, 0))]
                  + [_full_spec(w) for w in tail]),
        out_specs=[pl.BlockSpec((1, tm, D), lambda b, t: (b, t, 0)),
                   pl.BlockSpec((1, GLA_VW, GLA_KW), lambda b, t: (b, 0, 0))],
        out_shape=[jax.ShapeDtypeStruct((nb, seq, D), F32),
                   jax.ShapeDtypeStruct((nb, GLA_VW, GLA_KW), F32)],
        scratch_shapes=[
            pltpu.VMEM((tm, GLA_KW), F32), pltpu.VMEM((tm, GLA_KW), F32), pltpu.VMEM((tm, GLA_KW), F32),
            pltpu.VMEM((tm, GLA_VW), F32), pltpu.VMEM((tm, GLA_VW), F32),
            pltpu.VMEM((GLA_VW, GLA_KW), F32),
            pltpu.VMEM((GLA_SUB * GLA_CHUNK, GLA_KW), BF16),
        ],
        compiler_params=_params(("arbitrary", "arbitrary")),
        name="mixer_a_prompt",
    )(x, *weights, kbd, vbd, *tail)


def _ffn_kernel(x_ref, gpre_ref, wup_ref, wdn_ref, gpost_ref, o_ref, acc_s):
    x = x_ref[...]
    h = _bf(_rms(x, gpre_ref[...]))
    for j in range(D_FF // FF_CHUNK):
        u = _dot(h, wup_ref[:, j * FF_CHUNK:(j + 1) * FF_CHUNK])
        u = jnp.maximum(u, 0.0)
        d = _dot(_bf(u * u), wdn_ref[j * FF_CHUNK:(j + 1) * FF_CHUNK, :])
        if j == 0:
            acc_s[...] = d
        else:
            acc_s[...] += d
    o_ref[...] = x + _rms(acc_s[...], gpost_ref[...])


def _ffn_call(x2d, wf, tm):
    n = x2d.shape[0]
    ws = [wf["gpre"], wf["wup"], wf["wdn"], wf["gpost"]]
    return pl.pallas_call(
        _ffn_kernel,
        grid=(n // tm,),
        in_specs=[pl.BlockSpec((tm, D), lambda i: (i, 0))] + [_full_spec(w) for w in ws],
        out_specs=pl.BlockSpec((tm, D), lambda i: (i, 0)),
        out_shape=jax.ShapeDtypeStruct((n, D), F32),
        scratch_shapes=[pltpu.VMEM((tm, D), F32)],
        compiler_params=_params(("arbitrary",)),
        name="ffn",
    )(x2d, *ws)


def _swa_bias_tables():
    qi = np.arange(WINDOW)[:, None] + WINDOW
    kj = np.arange(2 * WINDOW)[None, :]
    dist = qi - kj
    valid = (dist >= 0) & (dist < WINDOW)
    return np.where(valid, _t5_bucket(dist), -1).astype(np.int32)


def _t5_bucket(dist):
    max_exact = N_BUCKETS // 2
    n = np.maximum(dist, 0)
    nf = np.maximum(n, 1).astype(np.float32)
    large = max_exact + (np.log(nf / np.float32(max_exact)) / np.float32(math.log(MAX_DISTANCE / max_exact))
                         * np.float32(N_BUCKETS - max_exact)).astype(np.int32)
    large = np.minimum(large, N_BUCKETS - 1)
    return np.where(n < max_exact, n, large)


def _mixer_b_kernel(rb_ref, sink_ref,
                    x_ref, bkt_ref, gkv_ref, wkv_ref, gpre_ref, wq_ref, wqm_ref,
                    kbd_ref, vbd_ref, wo1_ref, wo2_ref, gpost_ref,
                    xo_ref, kc_ref, vc_ref,
                    kbuf, vbuf, o_s, bias_s):
    W = WINDOW
    tm = x_ref.shape[1]
    bb = pl.program_id(0)
    t = pl.program_id(1)
    nheads = SWA_G * SWA_KVH

    @pl.when((bb == 0) & (t == 0))
    def _():
        bkt = bkt_ref[...]
        for i in range(nheads):
            def add_bucket(n, acc):
                return jnp.where(bkt == n, rb_ref[n, i], acc)
            tab = lax.fori_loop(0, N_BUCKETS, add_bucket, jnp.zeros((W, 2 * W), F32))
            bias_s[i] = jnp.where(bkt < 0, NEG_INF, tab)

    @pl.when(t == 0)
    def _():
        kbuf[0:W, :] = jnp.zeros((W, SWA_KVW), F32)
        vbuf[0:W, :] = jnp.zeros((W, SWA_KVW), F32)

    x = x_ref[0]
    kv = _dot(_bf(_rms(x, gkv_ref[...])), wkv_ref[...])
    kbuf[W:W + tm, :] = kv[:, :SWA_KVW]
    vbuf[W:W + tm, :] = kv[:, SWA_KVW:]
    kc_ref[0] = kv[tm - W:, :SWA_KVW]
    vc_ref[0] = kv[tm - W:, SWA_KVW:]

    h = _bf(_rms(x, gpre_ref[...]))
    q = _dot(h, wq_ref[...])
    qm = _dot(h, wqm_ref[...])

    lane_head = _iota((W, SWA_KVW), 1) >> 6
    key_head = _iota((2 * W, SWA_KVW), 1) >> 6
    own_block = _iota((W, 2 * W), 1) >= W
    for j in range(tm // W):
        kb = kbuf[j * W:j * W + 2 * W, :]
        vb = vbuf[j * W:j * W + 2 * W, :]
        qrows = []
        for gi in range(SWA_G):
            qg = q[j * W:(j + 1) * W, gi * SWA_KVW:(gi + 1) * SWA_KVW]
            for hh in range(SWA_KVH):
                qrows.append(jnp.where(lane_head == hh, qg, 0.0))
        s_all = _dot_nt(_bf(jnp.concatenate(qrows, axis=0)), _bf(kb))
        vst = _bf(jnp.concatenate([jnp.where(key_head == hh, vb, 0.0) for hh in range(SWA_KVH)], axis=0))
        for gi in range(SWA_G):
            ps = []
            for hh in range(SWA_KVH):
                i = gi * SWA_KVH + hh
                s = s_all[i * W:(i + 1) * W, :] * (SWA_HD ** -0.5) + bias_s[i]
                if j == 0:
                    s = jnp.where(own_block | (t > 0), s, NEG_INF)
                sink = sink_ref[i]
                mx = jnp.maximum(jnp.max(s, axis=-1, keepdims=True), sink)
                p = jnp.exp(s - mx)
                p = p / (jnp.sum(p, axis=-1, keepdims=True) + jnp.exp(sink - mx))
                ps.append(_bf(p))
            o_s[j * W:(j + 1) * W, gi * SWA_KVW:(gi + 1) * SWA_KVW] = _dot(jnp.concatenate(ps, axis=1), vst)

    kbuf[0:W, :] = kbuf[tm:tm + W, :]
    vbuf[0:W, :] = vbuf[tm:tm + W, :]

    o_mem = _mem_softmax_pv(qm, kbd_ref.at[0], vbd_ref.at[0])
    xo_ref[0] = _mix_residual(x, o_s[...], o_mem, wo1_ref, wo2_ref, gpost_ref)


def _mixer_b_call(x, wb, kbd, vbd):
    nb, seq, _ = x.shape
    tm = TM_PROMPT
    bkt = jnp.asarray(_swa_bias_tables())
    head = [bkt, wb["gkv"], wb["wkv"], wb["gpre"], wb["wq"], wb["wqm"]]
    tail = [wb["wo1"], wb["wo2"], wb["gpost"]]
    smem = pl.BlockSpec(memory_space=pltpu.SMEM)
    return pl.pallas_call(
        _mixer_b_kernel,
        grid=(nb, seq // tm),
        in_specs=([smem, smem, pl.BlockSpec((1, tm, D), lambda b, t: (b, t, 0))]
                  + [_full_spec(w) for w in head]
                  + [pl.BlockSpec((1, MEM_W, MEM_H * N_MEM), lambda b, t: (b, 0, 0)),
                     pl.BlockSpec((1, MEM_H * N_MEM, MEM_W), lambda b, t: (b, 0, 0))]
                  + [_full_spec(w) for w in tail]),
        out_specs=[pl.BlockSpec((1, tm, D), lambda b, t: (b, t, 0)),
                   pl.BlockSpec((1, WINDOW, SWA_KVW), lambda b, t: (b, 0, 0)),
                   pl.BlockSpec((1, WINDOW, SWA_KVW), lambda b, t: (b, 0, 0))],
        out_shape=[jax.ShapeDtypeStruct((nb, seq, D), F32),
                   jax.ShapeDtypeStruct((nb, WINDOW, SWA_KVW), F32),
                   jax.ShapeDtypeStruct((nb, WINDOW, SWA_KVW), F32)],
        scratch_shapes=[
            pltpu.VMEM((tm + WINDOW, SWA_KVW), F32), pltpu.VMEM((tm + WINDOW, SWA_KVW), F32),
            pltpu.VMEM((tm, SWA_QW), F32),
            pltpu.VMEM((SWA_G * SWA_KVH, WINDOW, 2 * WINDOW), F32),
        ],
        compiler_params=_params(("arbitrary", "arbitrary")),
        name="mixer_b_prompt",
    )(wb["rb"], wb["sinks"], x, *head, kbd, vbd, *tail)


def _pre_a_kernel(x_ref, gpre_ref, wqk_ref, wv_ref, wr_ref, wmisc_ref, wg_ref, bg_ref,
                  q_ref, k_ref, g_ref, v_ref, r_ref, qm_ref):
    q, k, g, v, r, qm = _proj_a(x_ref[...], gpre_ref, wqk_ref, wv_ref, wr_ref, wmisc_ref, wg_ref, bg_ref)
    q_ref[...] = q
    k_ref[...] = k
    g_ref[...] = g
    v_ref[...] = v
    r_ref[...] = r
    qm_ref[...] = qm


def _pre_a_call(x2d, wa):
    n = x2d.shape[0]
    ws = [wa["gpre"], wa["wqk"], wa["wv"], wa["wr"], wa["wmisc"], wa["wg"], wa["bg"]]
    widths = [GLA_KW, GLA_KW, GLA_KW, GLA_VW, GLA_VW, MEM_W]
    return pl.pallas_call(
        _pre_a_kernel,
        grid=(1,),
        in_specs=[_full_spec(x2d)] + [_full_spec(w) for w in ws],
        out_specs=[pl.BlockSpec((n, w), lambda i: (0, 0)) for w in widths],
        out_shape=[jax.ShapeDtypeStruct((n, w), F32) for w in widths],
        compiler_params=_params(("arbitrary",)),
        name="pre_a_sample",
    )(x2d, *ws)


def _row_to_col(row, eye):
    return jnp.sum(jnp.where(eye, jnp.broadcast_to(row, eye.shape), 0.0), axis=1, keepdims=True)


def _head_sum_mats(n_lanes):
    eseg = _bf(jnp.where((_iota((n_lanes, 128), 0) >> 6) == _iota((n_lanes, 128), 1), 1.0, 0.0))
    esegt = _bf(jnp.where(_iota((128, n_lanes), 0) == (_iota((128, n_lanes), 1) >> 6), 1.0, 0.0))
    return eseg, esegt


def _seg_scores(kb, qrow, eseg):
    prod = kb * qrow
    p1 = _bf(prod)
    p2 = _bf(prod - p1.astype(F32))
    return _dot(p1, eseg) + _dot(p2, eseg)


def _mem_attend_row(qrow, kb, vb, eseg, esegt):
    s = _seg_scores(kb, qrow, eseg) * (MEM_HD ** -0.5)
    mx = jnp.max(s, axis=0, keepdims=True)
    e = jnp.exp(s - mx)
    p = e / jnp.sum(e, axis=0, keepdims=True)
    pe = _dot(_bf(p), esegt)
    return jnp.sum(pe * vb, axis=0, keepdims=True)


def _step_a_kernel(q_ref, k_ref, g_ref, v_ref, qm_ref, s_ref, mk_ref, mv_ref,
                   o_ref, so_ref, om_ref):
    eye = _iota((GLA_KW, GLA_KW), 0) == _iota((GLA_KW, GLA_KW), 1)
    eseg, esegt = _head_sum_mats(MEM_W)

    def row(i, carry):
        a_col = _row_to_col(jnp.exp(g_ref[i]), eye)
        k_col = _row_to_col(k_ref[i], eye)
        q_col = _row_to_col(q_ref[i], eye)
        vx = jnp.concatenate(
            [jnp.broadcast_to(v_ref[i, h:h + 1, :], (GLA_DK, GLA_DV)) for h in range(GLA_H)], axis=0)
        s_new = a_col * s_ref[i] + k_col * vx
        so_ref[i] = s_new
        w = q_col * s_new
        for h in range(GLA_H):
            o_ref[i, h:h + 1, :] = jnp.sum(w[h * GLA_DK:(h + 1) * GLA_DK, :], axis=0, keepdims=True)
        om_ref[i] = _mem_attend_row(qm_ref[i], mk_ref[i], mv_ref[i], eseg, esegt)
        return carry

    lax.fori_loop(0, q_ref.shape[0], row, 0)


def _step_a_call(q, k, g, v, qm, state, mem_k, mem_v):
    n = q.shape[0]
    rb = ROWS_PER_STEP
    row3 = lambda w: pl.BlockSpec((rb, 1, w), lambda i: (i, 0, 0))
    return pl.pallas_call(
        _step_a_kernel,
        grid=(n // rb,),
        in_specs=[row3(GLA_KW), row3(GLA_KW), row3(GLA_KW),
                  pl.BlockSpec((rb, GLA_H, GLA_DV), lambda i: (i, 0, 0)),
                  row3(MEM_W),
                  pl.BlockSpec((rb, GLA_KW, GLA_DV), lambda i: (i, 0, 0)),
                  pl.BlockSpec((rb, N_MEM, MEM_W), lambda i: (i, 0, 0)),
                  pl.BlockSpec((rb, N_MEM, MEM_W), lambda i: (i, 0, 0))],
        out_specs=[pl.BlockSpec((rb, GLA_H, GLA_DV), lambda i: (i, 0, 0)),
                   pl.BlockSpec((rb, GLA_KW, GLA_DV), lambda i: (i, 0, 0)),
                   row3(MEM_W)],
        out_shape=[jax.ShapeDtypeStruct((n, GLA_H, GLA_DV), F32),
                   jax.ShapeDtypeStruct((n, GLA_KW, GLA_DV), F32),
                   jax.ShapeDtypeStruct((n, 1, MEM_W), F32)],
        compiler_params=_params(("arbitrary",)),
        name="gla_step_sample",
    )(q.reshape(n, 1, GLA_KW), k.reshape(n, 1, GLA_KW), g.reshape(n, 1, GLA_KW),
      v.reshape(n, GLA_H, GLA_DV), qm.reshape(n, 1, MEM_W),
      state.reshape(n, GLA_KW, GLA_DV), mem_k, mem_v)


def _post_a_kernel(x_ref, o_ref, r_ref, om_ref, glan_ref, wo1_ref, wo2_ref, gpost_ref, xo_ref):
    o_main = _gla_out_gate(o_ref[...], r_ref[...], glan_ref)
    xo_ref[...] = _mix_residual(x_ref[...], o_main, om_ref[...], wo1_ref, wo2_ref, gpost_ref)


def _post_a_call(x2d, o, r, om, wa):
    args = [x2d, o, r, om, wa["glan"], wa["wo1"], wa["wo2"], wa["gpost"]]
    return pl.pallas_call(
        _post_a_kernel,
        grid=(1,),
        in_specs=[_full_spec(a) for a in args],
        out_specs=_full_spec(x2d),
        out_shape=jax.ShapeDtypeStruct(x2d.shape, F32),
        compiler_params=_params(("arbitrary",)),
        name="post_a_sample",
    )(*args)


def _pre_b_kernel(x_ref, gkv_ref, wkv_ref, gpre_ref, wq_ref, wqm_ref, ks_ref, vs_ref, q_ref, qm_ref):
    x = x_ref[...]
    kv = _dot(_bf(_rms(x, gkv_ref[...])), wkv_ref[...])
    ks_ref[...] = kv[:, :SWA_KVW]
    vs_ref[...] = kv[:, SWA_KVW:]
    h = _bf(_rms(x, gpre_ref[...]))
    q_ref[...] = _dot(h, wq_ref[...])
    qm_ref[...] = _dot(h, wqm_ref[...])


def _pre_b_call(x2d, wb):
    n = x2d.shape[0]
    ws = [wb["gkv"], wb["wkv"], wb["gpre"], wb["wq"], wb["wqm"]]
    widths = [SWA_KVW, SWA_KVW, SWA_QW, MEM_W]
    return pl.pallas_call(
        _pre_b_kernel,
        grid=(1,),
        in_specs=[_full_spec(x2d)] + [_full_spec(w) for w in ws],
        out_specs=[pl.BlockSpec((n, w), lambda i: (0, 0)) for w in widths],
        out_shape=[jax.ShapeDtypeStruct((n, w), F32) for w in widths],
        compiler_params=_params(("arbitrary",)),
        name="pre_b_sample",
    )(x2d, *ws)


def _sample_bucket_onehot():
    dist = (WINDOW - 1) - np.arange(WINDOW)
    oh = np.zeros((WINDOW, 128), np.float32)
    oh[np.arange(WINDOW), _t5_bucket(dist)] = 1.0
    return oh


def _step_b_kernel(oh_ref, rb_ref, sink_ref, kc_ref, vc_ref, ks_ref, vs_ref, q_ref, qm_ref, mk_ref, mv_ref,
                   kn_ref, vn_ref, o_ref, om_ref, bias_s):
    W = WINDOW
    eseg, esegt = _head_sum_mats(SWA_KVW)

    @pl.when(pl.program_id(0) == 0)
    def _():
        bias_s[...] = _exact_dot(oh_ref[...], rb_ref[...])

    def row(i, carry):
        kn_ref[i, 0:W - 1, :] = kc_ref[i, 1:W, :]
        kn_ref[i, W - 1:W, :] = ks_ref[i]
        vn_ref[i, 0:W - 1, :] = vc_ref[i, 1:W, :]
        vn_ref[i, W - 1:W, :] = vs_ref[i]
        kn = kn_ref[i]
        vn = vn_ref[i]
        q = q_ref[i]
        outs = []
        for gi in range(SWA_G):
            s = (_seg_scores(kn, q[:, gi * SWA_KVW:(gi + 1) * SWA_KVW], eseg) * (SWA_HD ** -0.5)
                 + bias_s[:, gi * 128:(gi + 1) * 128])
            sink = sink_ref[:, gi * 128:(gi + 1) * 128]
            mx = jnp.maximum(jnp.max(s, axis=0, keepdims=True), sink)
            p = jnp.exp(s - mx)
            p = p / (jnp.sum(p, axis=0, keepdims=True) + jnp.exp(sink - mx))
            pe = _dot(_bf(p), esegt)
            outs.append(jnp.sum(pe * vn, axis=0, keepdims=True))
        o_ref[i] = jnp.concatenate(outs, axis=1)
        om_ref[i] = _mem_attend_row(qm_ref[i], mk_ref[i], mv_ref[i], eseg, esegt)
        return carry

    lax.fori_loop(0, q_ref.shape[0], row, 0)


def _step_b_call(cache_k, cache_v, ks, vs, q, qm, mem_k, mem_v, wb):
    n = q.shape[0]
    rb = ROWS_PER_STEP
    oh = jnp.asarray(_sample_bucket_onehot(), BF16)
    row3 = lambda w: pl.BlockSpec((rb, 1, w), lambda i: (i, 0, 0))
    cache = pl.BlockSpec((rb, WINDOW, SWA_KVW), lambda i: (i, 0, 0))
    mem = pl.BlockSpec((rb, N_MEM, MEM_W), lambda i: (i, 0, 0))
    consts = [oh, wb["rb_sample"], wb["sinks_sample"]]
    return pl.pallas_call(
        _step_b_kernel,
        grid=(n // rb,),
        in_specs=[_full_spec(c) for c in consts] + [cache, cache, row3(SWA_KVW), row3(SWA_KVW),
                                                    row3(SWA_QW), row3(MEM_W), mem, mem],
        out_specs=[cache, cache, row3(SWA_QW), row3(MEM_W)],
        out_shape=[jax.ShapeDtypeStruct((n, WINDOW, SWA_KVW), F32),
                   jax.ShapeDtypeStruct((n, WINDOW, SWA_KVW), F32),
                   jax.ShapeDtypeStruct((n, 1, SWA_QW), F32),
                   jax.ShapeDtypeStruct((n, 1, MEM_W), F32)],
        scratch_shapes=[pltpu.VMEM((WINDOW, SWA_G * 128), F32)],
        compiler_params=_params(("arbitrary",)),
        name="swa_step_sample",
    )(*consts, cache_k, cache_v, ks.reshape(n, 1, SWA_KVW), vs.reshape(n, 1, SWA_KVW),
      q.reshape(n, 1, SWA_QW), qm.reshape(n, 1, MEM_W), mem_k, mem_v)


def _post_b_kernel(x_ref, o_ref, om_ref, wo1_ref, wo2_ref, gpost_ref, xo_ref):
    xo_ref[...] = _mix_residual(x_ref[...], o_ref[...], om_ref[...], wo1_ref, wo2_ref, gpost_ref)


def _post_b_call(x2d, o, om, wb):
    args = [x2d, o, om, wb["wo1"], wb["wo2"], wb["gpost"]]
    return pl.pallas_call(
        _post_b_kernel,
        grid=(1,),
        in_specs=[_full_spec(a) for a in args],
        out_specs=_full_spec(x2d),
        out_shape=jax.ShapeDtypeStruct(x2d.shape, F32),
        compiler_params=_params(("arbitrary",)),
        name="post_b_sample",
    )(*args)


def _prep_weights(norm_mix_pre, norm_mix_post, norm_ffn_pre, norm_ffn_post, w_in_a, w_gate_up, b_gate,
                  gla_norm, w_in_b, sinks, norm_kv, w_kv, rel_bias, w_out, w_ffn_up, w_ffn_down):
    row = lambda g: g.reshape(1, -1)
    wa_in = w_in_a[0]
    c_v = 2 * GLA_KW
    c_r = c_v + GLA_VW
    c_g = c_r + GLA_VW
    c_m = c_g + GATE_RANK
    wmisc = jnp.concatenate([wa_in[:, c_g:c_m], jnp.zeros((D, 128 - GATE_RANK), F32), wa_in[:, c_m:]], axis=1)
    wg = jnp.zeros((128, GLA_KW), F32).at[:GATE_RANK].set(w_gate_up[0])
    wa = dict(
        gpre=row(norm_mix_pre[0]), wqk=_bf(wa_in[:, :c_v]), wv=_bf(wa_in[:, c_v:c_r]), wr=_bf(wa_in[:, c_r:c_g]),
        wmisc=_bf(wmisc), wg=_bf(wg), bg=row(b_gate[0]), glan=row(jnp.tile(gla_norm[0], GLA_H)),
        wo1=_bf(w_out[0][:GLA_VW]), wo2=_bf(w_out[0][GLA_VW:]), gpost=row(norm_mix_post[0]))
    wb_in = w_in_b[0]
    wq = wb_in[:, :SWA_QW].reshape(D, SWA_KVH, SWA_G, SWA_HD).transpose(0, 2, 1, 3).reshape(D, SWA_QW)
    wo1 = w_out[1][:SWA_QW].reshape(SWA_KVH, SWA_G, SWA_HD, D).transpose(1, 0, 2, 3).reshape(SWA_QW, D)
    rb = rel_bias.reshape(N_BUCKETS, SWA_KVH, SWA_G).transpose(0, 2, 1)
    sk = sinks[0].reshape(SWA_KVH, SWA_G).T
    rb_sample = jnp.zeros((128, SWA_G, 128), F32).at[:N_BUCKETS, :, :SWA_KVH].set(rb).reshape(128, SWA_G * 128)
    sinks_sample = jnp.zeros((SWA_G, 128), F32).at[:, :SWA_KVH].set(sk).reshape(1, SWA_G * 128)
    wb = dict(
        gkv=row(norm_kv), wkv=_bf(w_kv), gpre=row(norm_mix_pre[1]), wq=_bf(wq), wqm=_bf(wb_in[:, SWA_QW:]),
        wo1=_bf(wo1), wo2=_bf(w_out[1][SWA_QW:]), gpost=row(norm_mix_post[1]),
        rb=rb.reshape(N_BUCKETS, SWA_G * SWA_KVH), sinks=sk.reshape(SWA_G * SWA_KVH),
        rb_sample=rb_sample, sinks_sample=sinks_sample)
    ffn = [dict(gpre=row(norm_ffn_pre[l]), wup=_bf(w_ffn_up[l]), wdn=_bf(w_ffn_down[l]),
                gpost=row(norm_ffn_post[l])) for l in range(2)]
    return wa, wb, ffn


def kernel(x_prompt, x_sample, state_gla, cache_swa_k, cache_swa_v, cache_mem_k, cache_mem_v, mem_prompt,
           norm_mix_pre, norm_mix_post, norm_ffn_pre, norm_ffn_post, norm_mem, w_mem_kv, w_in_a, w_gate_up,
           b_gate, gla_norm, w_in_b, sinks, norm_kv, w_kv, rel_bias, w_out, w_ffn_up, w_ffn_down):
    wa, wb, ffn = _prep_weights(norm_mix_pre, norm_mix_post, norm_ffn_pre, norm_ffn_post, w_in_a, w_gate_up,
                                b_gate, gla_norm, w_in_b, sinks, norm_kv, w_kv, rel_bias, w_out, w_ffn_up,
                                w_ffn_down)
    nb, seq, _ = x_prompt.shape
    ns = x_sample.shape[0]

    mk, mv, kbd, vbd = _memkv_call(mem_prompt, norm_mem, w_mem_kv)
    x1, st = _mixer_a_call(x_prompt, wa, kbd[0], vbd[0])
    x2 = _ffn_call(x1.reshape(nb * seq, D), ffn[0], TM_PROMPT).reshape(nb, seq, D)
    x3, kc, vc = _mixer_b_call(x2, wb, kbd[1], vbd[1])
    y_prompt = _ffn_call(x3.reshape(nb * seq, D), ffn[1], TM_PROMPT).reshape(nb, seq, D)
    st4 = st.reshape(nb, GLA_H, GLA_DV, GLA_H, GLA_DK)
    state_prompt = jnp.stack([st4[:, h, :, h, :] for h in range(GLA_H)], axis=1).transpose(0, 1, 3, 2)[None]
    mem_shape = (2, nb, N_MEM, MEM_H, MEM_HD)
    swa_shape = (nb, WINDOW, SWA_KVH, SWA_HD)

    xs = x_sample.reshape(ns, D)
    q, k, g, v, r, qm = _pre_a_call(xs, wa)
    o, s_new, om = _step_a_call(q, k, g, v, qm, state_gla[0], cache_mem_k[0].reshape(ns, N_MEM, MEM_W),
                                cache_mem_v[0].reshape(ns, N_MEM, MEM_W))
    xs1 = _post_a_call(xs, o.reshape(ns, GLA_VW), r, om.reshape(ns, MEM_W), wa)
    xs2 = _ffn_call(xs1, ffn[0], ns)
    ks, vs, qb, qmb = _pre_b_call(xs2, wb)
    kn, vn, ob, omb = _step_b_call(cache_swa_k.reshape(ns, WINDOW, SWA_KVW), cache_swa_v.reshape(ns, WINDOW, SWA_KVW),
                                   ks, vs, qb, qmb, cache_mem_k[1].reshape(ns, N_MEM, MEM_W),
                                   cache_mem_v[1].reshape(ns, N_MEM, MEM_W), wb)
    xs3 = _post_b_call(xs2, ob.reshape(ns, SWA_QW), omb.reshape(ns, MEM_W), wb)
    y_sample = _ffn_call(xs3, ffn[1], ns).reshape(ns, 1, D)

    return (y_prompt, y_sample, state_prompt,
            s_new.reshape(1, ns, GLA_H, GLA_DK, GLA_DV),
            kc.reshape(swa_shape), vc.reshape(swa_shape),
            kn.reshape(ns, WINDOW, SWA_KVH, SWA_HD), vn.reshape(ns, WINDOW, SWA_KVH, SWA_HD),
            mk.reshape(mem_shape), mv.reshape(mem_shape))
```

```python
import functools
import math

import numpy as np
import jax
import jax.numpy as jnp
from jax import lax
from jax.experimental import pallas as pl
from jax.experimental.pallas import tpu as pltpu

F32 = jnp.float32
BF16 = jnp.bfloat16

D = 1024
D_FF = 4 * D
N_MEM = 256
MEM_H = 4
MEM_HD = 64
MEM_W = MEM_H * MEM_HD
GLA_H = 4
GLA_DK = 96
GLA_DV = 192
GLA_KW = GLA_H * GLA_DK
GLA_VW = GLA_H * GLA_DV
GATE_RANK = 16
GATE_NORM = 16.0
SWA_HD = 64
SWA_KVH = 4
SWA_G = 3
SWA_QW = SWA_KVH * SWA_G * SWA_HD
SWA_KVW = SWA_KVH * SWA_HD
WINDOW = 128
N_BUCKETS = 32
MAX_DISTANCE = 128
EPS = 1e-6

GLA_CHUNK = 64
GLA_SUB = 16
TM_PROMPT = 512
FF_CHUNK = 512
GLA_DK_BLOCK = 32
MEM_ROWS_PER_STEP = 16
SWA_ROWS_PER_STEP = 16
V7X_VMEM_LIMIT = 56 * 1024 * 1024
NEG_INF = float("-inf")


def _bf(x):
    return x.astype(BF16)


def _dot(a, b):
    return jnp.dot(a, b, preferred_element_type=F32)


def _dot_nt(a, b):
    return lax.dot_general(a, b, (((1,), (1,)), ((), ())), preferred_element_type=F32)


def _dot_tn(a, b):
    return lax.dot_general(a, b, (((0,), (0,)), ((), ())), preferred_element_type=F32)


def _rms(x, g):
    return x * lax.rsqrt(jnp.mean(x * x, axis=-1, keepdims=True) + EPS) * g


def _split3(x):
    x1 = _bf(x)
    r1 = x - x1.astype(F32)
    x2 = _bf(r1)
    x3 = _bf(r1 - x2.astype(F32))
    return x1, x2, x3


def _exact_dot(sel, x):
    x1, x2, x3 = _split3(x)
    return _dot(sel, x1) + _dot(sel, x2) + _dot(sel, x3)


def _log_sigmoid(z):
    return jnp.minimum(z, 0.0) - jnp.log1p(jnp.exp(-jnp.abs(z)))


def _silu(z):
    return z * (1.0 / (1.0 + jnp.exp(-z)))


def _iota(shape, dim):
    return lax.broadcasted_iota(jnp.int32, shape, dim)


def _gla_k_head(lane):
    one = jnp.int32(1)
    zero = jnp.int32(0)
    return (jnp.where(lane >= GLA_DK, one, zero) + jnp.where(lane >= 2 * GLA_DK, one, zero)
            + jnp.where(lane >= 3 * GLA_DK, one, zero))


def _gla_v_head(lane):
    one = jnp.int32(1)
    zero = jnp.int32(0)
    return (jnp.where(lane >= GLA_DV, one, zero) + jnp.where(lane >= 2 * GLA_DV, one, zero)
            + jnp.where(lane >= 3 * GLA_DV, one, zero))


def _full_spec(a):
    nd = a.ndim
    return pl.BlockSpec(a.shape, lambda *_: (0,) * nd)


def _params(sem):
    return pltpu.CompilerParams(dimension_semantics=sem, vmem_limit_bytes=V7X_VMEM_LIMIT)


def _mem_softmax_pv(qm, kbd_ref, vbd_ref):
    s = _dot(_bf(qm), kbd_ref[...]) * (MEM_HD ** -0.5)
    out = None
    for h in range(MEM_H):
        sh = s[:, h * N_MEM:(h + 1) * N_MEM]
        mx = jnp.max(sh, axis=-1, keepdims=True)
        e = jnp.exp(sh - mx)
        p = e / jnp.sum(e, axis=-1, keepdims=True)
        t = _dot(_bf(p), vbd_ref[h * N_MEM:(h + 1) * N_MEM, :])
        out = t if out is None else out + t
    return out


def _memkv_kernel(mem_ref, g_ref, w_ref, k_ref, v_ref, kbd_ref, vbd_ref):
    h = _bf(_rms(mem_ref[0], g_ref[0]))
    kv = _dot(h, w_ref[0])
    k = kv[:, :MEM_W]
    v = kv[:, MEM_W:]
    kt = k.T
    k_ref[0, 0] = kt
    v_ref[0, 0] = v.T
    kt4 = jnp.concatenate([kt, kt, kt, kt], axis=1)
    keep_k = (_iota((MEM_W, MEM_H * N_MEM), 0) >> 6) == (_iota((MEM_W, MEM_H * N_MEM), 1) >> 8)
    kbd_ref[0, 0] = _bf(jnp.where(keep_k, kt4, 0.0))
    v4 = jnp.concatenate([v, v, v, v], axis=0)
    keep_v = (_iota((MEM_H * N_MEM, MEM_W), 0) >> 8) == (_iota((MEM_H * N_MEM, MEM_W), 1) >> 6)
    vbd_ref[0, 0] = _bf(jnp.where(keep_v, v4, 0.0))


def _memkv_call(mem, norm_mem, w_mem_kv):
    nb = mem.shape[0]
    nl = w_mem_kv.shape[0]
    g = norm_mem.reshape(nl, 1, D)
    w = _bf(w_mem_kv)
    return pl.pallas_call(
        _memkv_kernel,
        grid=(nl, nb),
        in_specs=[
            pl.BlockSpec((1, N_MEM, D), lambda l, b: (b, 0, 0)),
            pl.BlockSpec((1, 1, D), lambda l, b: (l, 0, 0)),
            pl.BlockSpec((1, D, 2 * MEM_W), lambda l, b: (l, 0, 0)),
        ],
        out_specs=[
            pl.BlockSpec((1, 1, MEM_W, N_MEM), lambda l, b: (l, b, 0, 0)),
            pl.BlockSpec((1, 1, MEM_W, N_MEM), lambda l, b: (l, b, 0, 0)),
            pl.BlockSpec((1, 1, MEM_W, MEM_H * N_MEM), lambda l, b: (l, b, 0, 0)),
            pl.BlockSpec((1, 1, MEM_H * N_MEM, MEM_W), lambda l, b: (l, b, 0, 0)),
        ],
        out_shape=[
            jax.ShapeDtypeStruct((nl, nb, MEM_W, N_MEM), F32),
            jax.ShapeDtypeStruct((nl, nb, MEM_W, N_MEM), F32),
            jax.ShapeDtypeStruct((nl, nb, MEM_W, MEM_H * N_MEM), BF16),
            jax.ShapeDtypeStruct((nl, nb, MEM_H * N_MEM, MEM_W), BF16),
        ],
        compiler_params=_params(("arbitrary", "arbitrary")),
        name="mem_kv",
    )(mem, g, w)


def _proj_a(x, gpre_ref, wqk_ref, wv_ref, wr_ref, wmisc_ref, wg_ref, bg_ref):
    h = _bf(_rms(x, gpre_ref[...]))
    qk = _dot(h, wqk_ref[...])
    q = qk[:, :GLA_KW] * (GLA_DK ** -0.5)
    k = qk[:, GLA_KW:]
    v = _dot(h, wv_ref[...])
    r = _dot(h, wr_ref[...])
    misc = _dot(h, wmisc_ref[...])
    glr = misc[:, :128]
    qm = misc[:, 128:]
    g = _log_sigmoid(_dot(_bf(glr), wg_ref[...]) + bg_ref[...]) * (1.0 / GATE_NORM)
    return q, k, g, v, r, qm


def _gla_out_gate(o, r, glan_ref):
    vh = _gla_v_head(_iota(o.shape, 1))
    o2 = o * o
    scale = jnp.zeros_like(o)
    for h in range(GLA_H):
        ss = jnp.sum(jnp.where(vh == h, o2, 0.0), axis=-1, keepdims=True) * (1.0 / GLA_DV)
        scale = jnp.where(vh == h, lax.rsqrt(ss + EPS), scale)
    return o * scale * glan_ref[...] * _silu(r)


def _mix_residual(x, o_main, o_mem, wo1_ref, wo2_ref, gpost_ref):
    mix = _dot(_bf(o_main), wo1_ref[...]) + _dot(_bf(o_mem), wo2_ref[...])
    return x + _rms(mix, gpost_ref[...])


def _mixer_a_kernel(x_ref, gpre_ref, wqk_ref, wv_ref, wr_ref, wmisc_ref, wg_ref, bg_ref, glan_ref,
                    kbd_ref, vbd_ref, wo1_ref, wo2_ref, gpost_ref,
                    xo_ref, st_ref,
                    q_s, k_s, g_s, v_s, o_s, state_s, p_s):
    C = GLA_CHUNK
    tm = x_ref.shape[1]
    t = pl.program_id(1)

    @pl.when(t == 0)
    def _():
        state_s[...] = jnp.zeros_like(state_s)

    x = x_ref[0]
    q, k, g, v, r, qm = _proj_a(x, gpre_ref, wqk_ref, wv_ref, wr_ref, wmisc_ref, wg_ref, bg_ref)
    q_s[...] = q
    k_s[...] = k
    g_s[...] = g
    v_s[...] = v

    ri = _iota((C, GLA_KW), 0)
    kh = _gla_k_head(_iota((C, GLA_KW), 1))
    khcat = jnp.concatenate([kh, kh, kh], axis=1)
    vh = _gla_v_head(_iota((C, GLA_VW), 1))
    tri = _bf(jnp.where(_iota((C, C), 0) >= _iota((C, C), 1), 1.0, 0.0))
    d_rs = _iota((C, 4 * C), 0) - (_iota((C, 4 * C), 1) & (C - 1))
    band = jnp.where((d_rs >= 0) & (d_rs <= (_iota((C, 4 * C), 0) & (GLA_SUB - 1))), d_rs, -1)
    ebc = _bf(jnp.where(_gla_k_head(_iota((GLA_KW, 4 * C), 0)) == (_iota((GLA_KW, 4 * C), 1) >> 6),
                        1.0, 0.0))
    blockmask = _gla_v_head(_iota((GLA_VW, GLA_KW), 0)) == _gla_k_head(_iota((GLA_VW, GLA_KW), 1))

    def chunk(c, carry):
        r0 = pl.multiple_of(c * C, C)
        qc = q_s[pl.ds(r0, C), :]
        kc = k_s[pl.ds(r0, C), :]
        gc = g_s[pl.ds(r0, C), :]
        vc = v_s[pl.ds(r0, C), :]
        b = _exact_dot(tri, gc)

        ref1 = jnp.broadcast_to(b[31:32, :], b.shape)
        ref2 = jnp.where(ri < 32, jnp.broadcast_to(b[15:16, :], b.shape),
                         jnp.broadcast_to(b[47:48, :], b.shape))
        q1 = qc * jnp.exp(jnp.minimum(b - ref1, 0.0))
        k1 = kc * jnp.exp(jnp.minimum(ref1 - b, 0.0))
        q2 = qc * jnp.exp(jnp.minimum(b - ref2, 0.0))
        k2 = kc * jnp.exp(jnp.minimum(ref2 - b, 0.0))
        qcat = jnp.concatenate([
            jnp.where(ri >= 32, q1, 0.0),
            jnp.where((ri >= 16) & (ri < 32), q2, 0.0),
            jnp.where(ri >= 48, q2, 0.0)], axis=1)
        kcat = jnp.concatenate([
            jnp.where(ri < 32, k1, 0.0),
            jnp.where(ri < 16, k2, 0.0),
            jnp.where((ri >= 32) & (ri < 48), k2, 0.0)], axis=1)
        kst = _bf(jnp.concatenate([jnp.where(khcat == h, kcat, 0.0) for h in range(GLA_H)], axis=0))
        a_off = _dot_nt(_bf(qcat), kst)

        for dlt in range(GLA_SUB):
            if dlt == 0:
                pr = qc * kc
            else:
                kd = pltpu.roll(kc, dlt, 0)
                bd = pltpu.roll(b, dlt, 0)
                pr = qc * kd * jnp.exp(jnp.minimum(b - bd, 0.0))
            p_s[dlt * C:(dlt + 1) * C, :] = _bf(pr)
        rsum = _dot(p_s[...], ebc)
        a_diag = jnp.zeros((C, 4 * C), F32)
        for dlt in range(GLA_SUB):
            a_diag = jnp.where(band == dlt, rsum[dlt * C:(dlt + 1) * C, :], a_diag)

        vst = _bf(jnp.concatenate([jnp.where(vh == h, vc, 0.0) for h in range(GLA_H)], axis=0))
        o_intra = _dot(_bf(a_off + a_diag), vst)
        st = state_s[...]
        o_inter = _dot_nt(_bf(qc * jnp.exp(b)), _bf(st))
        o_s[pl.ds(r0, C), :] = o_intra + o_inter

        blast = jnp.broadcast_to(b[C - 1:C, :], b.shape)
        kdec = kc * jnp.exp(blast - b)
        kvt = _dot_tn(_bf(vc), _bf(kdec))
        decay = jnp.exp(b[C - 1:C, :])
        state_s[...] = st * decay + jnp.where(blockmask, kvt, 0.0)
        return carry

    lax.fori_loop(0, tm // C, chunk, 0)
    st_ref[0] = state_s[...]

    o_main = _gla_out_gate(o_s[...], r, glan_ref)
    o_mem = _mem_softmax_pv(qm, kbd_ref.at[0], vbd_ref.at[0])
    xo_ref[0] = _mix_residual(x, o_main, o_mem, wo1_ref, wo2_ref, gpost_ref)


def _mixer_a_call(x, wa, kbd, vbd):
    nb, seq, _ = x.shape
    tm = TM_PROMPT
    weights = [wa["gpre"], wa["wqk"], wa["wv"], wa["wr"], wa["wmisc"], wa["wg"], wa["bg"], wa["glan"]]
    tail = [wa["wo1"], wa["wo2"], wa["gpost"]]
    return pl.pallas_call(
        _mixer_a_kernel,
        grid=(nb, seq // tm),
        in_specs=([pl.BlockSpec((1, tm, D), lambda b, t: (b, t, 0))]
                  + [_full_spec(w) for w in weights]
                  + [pl.BlockSpec((1, MEM_W, MEM_H * N_MEM), lambda b, t: (b, 0, 0)),
                     pl.BlockSpec((1, MEM_H * N_MEM, MEM_W), lambda b, t: (b, 0, 0))]
                  + [_full_spec(w) for w in tail]),
        out_specs=[pl.BlockSpec((1, tm, D), lambda b, t: (b, t, 0)),
                   pl.BlockSpec((1, GLA_VW, GLA_KW), lambda b, t: (b, 0, 0))],
        out_shape=[jax.ShapeDtypeStruct((nb, seq, D), F32),
                   jax.ShapeDtypeStruct((nb, GLA_VW, GLA_KW), F32)],
        scratch_shapes=[
            pltpu.VMEM((tm, GLA_KW), F32), pltpu.VMEM((tm, GLA_KW), F32), pltpu.VMEM((tm, GLA_KW), F32),
            pltpu.VMEM((tm, GLA_VW), F32), pltpu.VMEM((tm, GLA_VW), F32),
            pltpu.VMEM((GLA_VW, GLA_KW), F32),
            pltpu.VMEM((GLA_SUB * GLA_CHUNK, GLA_KW), BF16),
        ],
        compiler_params=_params(("arbitrary", "arbitrary")),
        name="mixer_a_prompt",
    )(x, *weights, kbd, vbd, *tail)


def _ffn_kernel(x_ref, gpre_ref, wup_ref, wdn_ref, gpost_ref, o_ref, acc_s):
    x = x_ref[...]
    h = _bf(_rms(x, gpre_ref[...]))
    for j in range(D_FF // FF_CHUNK):
        u = _dot(h, wup_ref[:, j * FF_CHUNK:(j + 1) * FF_CHUNK])
        u = jnp.maximum(u, 0.0)
        d = _dot(_bf(u * u), wdn_ref[j * FF_CHUNK:(j + 1) * FF_CHUNK, :])
        if j == 0:
            acc_s[...] = d
        else:
            acc_s[...] += d
    o_ref[...] = x + _rms(acc_s[...], gpost_ref[...])


def _ffn_call(x2d, wf, tm):
    n = x2d.shape[0]
    ws = [wf["gpre"], wf["wup"], wf["wdn"], wf["gpost"]]
    return pl.pallas_call(
        _ffn_kernel,
        grid=(n // tm,),
        in_specs=[pl.BlockSpec((tm, D), lambda i: (i, 0))] + [_full_spec(w) for w in ws],
        out_specs=pl.BlockSpec((tm, D), lambda i: (i, 0)),
        out_shape=jax.ShapeDtypeStruct((n, D), F32),
        scratch_shapes=[pltpu.VMEM((tm, D), F32)],
        compiler_params=_params(("arbitrary",)),
        name="ffn",
    )(x2d, *ws)


def _t5_bucket(dist):
    max_exact = N_BUCKETS // 2
    n = np.maximum(dist, 0)
    nf = np.maximum(n, 1).astype(np.float32)
    large = max_exact + (np.log(nf / np.float32(max_exact)) / np.float32(math.log(MAX_DISTANCE / max_exact))
                         * np.float32(N_BUCKETS - max_exact)).astype(np.int32)
    large = np.minimum(large, N_BUCKETS - 1)
    return np.where(n < max_exact, n, large)


def _swa_bias_tables():
    qi = np.arange(WINDOW)[:, None] + WINDOW
    kj = np.arange(2 * WINDOW)[None, :]
    dist = qi - kj
    valid = (dist >= 0) & (dist < WINDOW)
    return np.where(valid, _t5_bucket(dist), -1).astype(np.int32)


def _mixer_b_kernel(rb_ref, sink_ref,
                    x_ref, bkt_ref, gkv_ref, wkv_ref, gpre_ref, wq_ref, wqm_ref,
                    kbd_ref, vbd_ref, wo1_ref, wo2_ref, gpost_ref,
                    xo_ref, kc_ref, vc_ref,
                    kbuf, vbuf, o_s, bias_s):
    W = WINDOW
    tm = x_ref.shape[1]
    bb = pl.program_id(0)
    t = pl.program_id(1)
    nheads = SWA_G * SWA_KVH

    @pl.when((bb == 0) & (t == 0))
    def _():
        bkt = bkt_ref[...]
        for i in range(nheads):
            def add_bucket(n, acc):
                return jnp.where(bkt == n, rb_ref[n, i], acc)
            tab = lax.fori_loop(0, N_BUCKETS, add_bucket, jnp.zeros((W, 2 * W), F32))
            bias_s[i] = jnp.where(bkt < 0, NEG_INF, tab)

    @pl.when(t == 0)
    def _():
        kbuf[0:W, :] = jnp.zeros((W, SWA_KVW), F32)
        vbuf[0:W, :] = jnp.zeros((W, SWA_KVW), F32)

    x = x_ref[0]
    kv = _dot(_bf(_rms(x, gkv_ref[...])), wkv_ref[...])
    kbuf[W:W + tm, :] = kv[:, :SWA_KVW]
    vbuf[W:W + tm, :] = kv[:, SWA_KVW:]
    kc_ref[0] = kv[tm - W:, :SWA_KVW]
    vc_ref[0] = kv[tm - W:, SWA_KVW:]

    h = _bf(_rms(x, gpre_ref[...]))
    q = _dot(h, wq_ref[...])
    qm = _dot(h, wqm_ref[...])

    lane_head = _iota((W, SWA_KVW), 1) >> 6
    key_head = _iota((2 * W, SWA_KVW), 1) >> 6
    own_block = _iota((W, 2 * W), 1) >= W
    for j in range(tm // W):
        kb = kbuf[j * W:j * W + 2 * W, :]
        vb = vbuf[j * W:j * W + 2 * W, :]
        qrows = []
        for gi in range(SWA_G):
            qg = q[j * W:(j + 1) * W, gi * SWA_KVW:(gi + 1) * SWA_KVW]
            for hh in range(SWA_KVH):
                qrows.append(jnp.where(lane_head == hh, qg, 0.0))
        s_all = _dot_nt(_bf(jnp.concatenate(qrows, axis=0)), _bf(kb))
        vst = _bf(jnp.concatenate([jnp.where(key_head == hh, vb, 0.0) for hh in range(SWA_KVH)], axis=0))
        for gi in range(SWA_G):
            ps = []
            for hh in range(SWA_KVH):
                i = gi * SWA_KVH + hh
                s = s_all[i * W:(i + 1) * W, :] * (SWA_HD ** -0.5) + bias_s[i]
                if j == 0:
                    s = jnp.where(own_block | (t > 0), s, NEG_INF)
                sink = sink_ref[i]
                mx = jnp.maximum(jnp.max(s, axis=-1, keepdims=True), sink)
                p = jnp.exp(s - mx)
                p = p / (jnp.sum(p, axis=-1, keepdims=True) + jnp.exp(sink - mx))
                ps.append(_bf(p))
            o_s[j * W:(j + 1) * W, gi * SWA_KVW:(gi + 1) * SWA_KVW] = _dot(jnp.concatenate(ps, axis=1), vst)

    kbuf[0:W, :] = kbuf[tm:tm + W, :]
    vbuf[0:W, :] = vbuf[tm:tm + W, :]

    o_mem = _mem_softmax_pv(qm, kbd_ref.at[0], vbd_ref.at[0])
    xo_ref[0] = _mix_residual(x, o_s[...], o_mem, wo1_ref, wo2_ref, gpost_ref)


def _mixer_b_call(x, wb, kbd, vbd):
    nb, seq, _ = x.shape
    tm = TM_PROMPT
    bkt = jnp.asarray(_swa_bias_tables())
    head = [bkt, wb["gkv"], wb["wkv"], wb["gpre"], wb["wq"], wb["wqm"]]
    tail = [wb["wo1"], wb["wo2"], wb["gpost"]]
    smem = pl.BlockSpec(memory_space=pltpu.SMEM)
    return pl.pallas_call(
        _mixer_b_kernel,
        grid=(nb, seq // tm),
        in_specs=([smem, smem, pl.BlockSpec((1, tm, D), lambda b, t: (b, t, 0))]
                  + [_full_spec(w) for w in head]
                  + [pl.BlockSpec((1, MEM_W, MEM_H * N_MEM), lambda b, t: (b, 0, 0)),
                     pl.BlockSpec((1, MEM_H * N_MEM, MEM_W), lambda b, t: (b, 0, 0))]
                  + [_full_spec(w) for w in tail]),
        out_specs=[pl.BlockSpec((1, tm, D), lambda b, t: (b, t, 0)),
                   pl.BlockSpec((1, WINDOW, SWA_KVW), lambda b, t: (b, 0, 0)),
                   pl.BlockSpec((1, WINDOW, SWA_KVW), lambda b, t: (b, 0, 0))],
        out_shape=[jax.ShapeDtypeStruct((nb, seq, D), F32),
                   jax.ShapeDtypeStruct((nb, WINDOW, SWA_KVW), F32),
                   jax.ShapeDtypeStruct((nb, WINDOW, SWA_KVW), F32)],
        scratch_shapes=[
            pltpu.VMEM((tm + WINDOW, SWA_KVW), F32), pltpu.VMEM((tm + WINDOW, SWA_KVW), F32),
            pltpu.VMEM((tm, SWA_QW), F32),
            pltpu.VMEM((SWA_G * SWA_KVH, WINDOW, 2 * WINDOW), F32),
        ],
        compiler_params=_params(("arbitrary", "arbitrary")),
        name="mixer_b_prompt",
    )(wb["rb"], wb["sinks"], x, *head, kbd, vbd, *tail)


def _row_to_col(row, eye):
    return jnp.sum(jnp.where(eye, jnp.broadcast_to(row, eye.shape), 0.0), axis=1, keepdims=True)


def _eye(n):
    return _iota((n, n), 0) == _iota((n, n), 1)


def _pre_a_kernel(x_ref, gpre_ref, wqk_ref, wv_ref, wr_ref, wmisc_ref, wg_ref, bg_ref,
                  q_ref, k_ref, g_ref, v_ref, r_ref, qm_ref):
    q, k, g, v, r, qm = _proj_a(x_ref[...], gpre_ref, wqk_ref, wv_ref, wr_ref, wmisc_ref, wg_ref, bg_ref)
    q_ref[...] = q
    k_ref[...] = k
    g_ref[...] = g
    v_ref[...] = v
    r_ref[...] = r
    qm_ref[...] = qm


def _pre_a_call(x2d, wa):
    n = x2d.shape[0]
    ws = [wa["gpre"], wa["wqk"], wa["wv"], wa["wr"], wa["wmisc"], wa["wg"], wa["bg"]]
    shapes = [(n, GLA_KW), (n, GLA_KW), (n, GLA_KW), (n, GLA_VW), (n, GLA_VW), (n, MEM_W)]
    return pl.pallas_call(
        _pre_a_kernel,
        grid=(1,),
        in_specs=[_full_spec(x2d)] + [_full_spec(w) for w in ws],
        out_specs=[pl.BlockSpec(s, lambda i: (0, 0)) for s in shapes],
        out_shape=[jax.ShapeDtypeStruct(s, F32) for s in shapes],
        compiler_params=_params(("arbitrary",)),
        name="pre_a_sample",
    )(x2d, *ws)


def _gla_step_kernel(qt_ref, kt_ref, gt_ref, vt_ref, s_ref, so_ref, ot_ref):
    c = pl.program_id(1)
    vt = vt_ref[...]
    acc = jnp.zeros_like(vt)
    for d in range(GLA_DK_BLOCK):
        s_new = jnp.exp(gt_ref[d:d + 1, :]) * s_ref[0, 0, d] + kt_ref[d:d + 1, :] * vt
        so_ref[0, 0, d] = s_new
        acc = acc + qt_ref[d:d + 1, :] * s_new

    @pl.when(c == 0)
    def _():
        ot_ref[...] = acc

    @pl.when(c > 0)
    def _():
        ot_ref[...] += acc


def _gla_step_call(qt, kt, gt, vt, state5):
    n = qt.shape[1]
    nblk = GLA_DK // GLA_DK_BLOCK
    rows = pl.BlockSpec((GLA_DK_BLOCK, n), lambda h, c: (h * nblk + c, 0))
    head = pl.BlockSpec((GLA_DV, n), lambda h, c: (h, 0))
    st = pl.BlockSpec((1, 1, GLA_DK_BLOCK, GLA_DV, n), lambda h, c: (0, h, c, 0, 0))
    return pl.pallas_call(
        _gla_step_kernel,
        grid=(GLA_H, nblk),
        in_specs=[rows, rows, rows, head, st],
        out_specs=[st, head],
        out_shape=[jax.ShapeDtypeStruct(state5.shape, F32), jax.ShapeDtypeStruct((GLA_VW, n), F32)],
        compiler_params=_params(("arbitrary", "arbitrary")),
        name="gla_step_sample",
    )(qt, kt, gt, vt, state5)


def _mem_step_kernel(qm_ref, mk_ref, mv_ref, ot_ref):
    step = pl.program_id(0)
    rb = qm_ref.shape[0]

    @pl.when(step == 0)
    def _():
        ot_ref[...] = jnp.zeros_like(ot_ref)

    eye = _eye(MEM_W)
    lane = _iota(ot_ref.shape, 1)

    def row(i, carry):
        q_col = _row_to_col(qm_ref[i], eye)
        cols = []
        for h in range(MEM_H):
            qh = q_col[h * MEM_HD:(h + 1) * MEM_HD]
            s = jnp.sum(mk_ref[0, i, h] * qh, axis=0, keepdims=True) * (MEM_HD ** -0.5)
            e = jnp.exp(s - jnp.max(s, axis=1, keepdims=True))
            p = e / jnp.sum(e, axis=1, keepdims=True)
            cols.append(jnp.sum(mv_ref[0, i, h] * p, axis=1, keepdims=True))
        col = jnp.concatenate(cols, axis=0)
        ot_ref[...] = jnp.where(lane == step * rb + i, col, ot_ref[...])
        return carry

    lax.fori_loop(0, rb, row, 0)


def _mem_step_call(qm, mk5, mv5, layer):
    n = qm.shape[0]
    rb = MEM_ROWS_PER_STEP
    blk = pl.BlockSpec((1, rb, MEM_H, MEM_HD, N_MEM), lambda i: (layer, i, 0, 0, 0))
    return pl.pallas_call(
        _mem_step_kernel,
        grid=(n // rb,),
        in_specs=[pl.BlockSpec((rb, 1, MEM_W), lambda i: (i, 0, 0)), blk, blk],
        out_specs=pl.BlockSpec((MEM_W, n), lambda i: (0, 0)),
        out_shape=jax.ShapeDtypeStruct((MEM_W, n), F32),
        compiler_params=_params(("arbitrary",)),
        name="mem_step_sample",
    )(qm.reshape(n, 1, MEM_W), mk5, mv5)


def _post_a_kernel(x_ref, o_ref, r_ref, om_ref, glan_ref, wo1_ref, wo2_ref, gpost_ref, xo_ref):
    o_main = _gla_out_gate(o_ref[...], r_ref[...], glan_ref)
    xo_ref[...] = _mix_residual(x_ref[...], o_main, om_ref[...], wo1_ref, wo2_ref, gpost_ref)


def _post_a_call(x2d, o, r, om, wa):
    args = [x2d, o, r, om, wa["glan"], wa["wo1"], wa["wo2"], wa["gpost"]]
    return pl.pallas_call(
        _post_a_kernel,
        grid=(1,),
        in_specs=[_full_spec(a) for a in args],
        out_specs=_full_spec(x2d),
        out_shape=jax.ShapeDtypeStruct(x2d.shape, F32),
        compiler_params=_params(("arbitrary",)),
        name="post_a_sample",
    )(*args)


def _pre_b_kernel(x_ref, gkv_ref, wkv_ref, gpre_ref, wq_ref, wqm_ref, ks_ref, vs_ref, q_ref, qm_ref):
    x = x_ref[...]
    kv = _dot(_bf(_rms(x, gkv_ref[...])), wkv_ref[...])
    ks_ref[...] = kv[:, :SWA_KVW]
    vs_ref[...] = kv[:, SWA_KVW:]
    h = _bf(_rms(x, gpre_ref[...]))
    q_ref[...] = _dot(h, wq_ref[...])
    qm_ref[...] = _dot(h, wqm_ref[...])


def _pre_b_call(x2d, wb):
    n = x2d.shape[0]
    ws = [wb["gkv"], wb["wkv"], wb["gpre"], wb["wq"], wb["wqm"]]
    widths = [SWA_KVW, SWA_KVW, SWA_QW, MEM_W]
    return pl.pallas_call(
        _pre_b_kernel,
        grid=(1,),
        in_specs=[_full_spec(x2d)] + [_full_spec(w) for w in ws],
        out_specs=[pl.BlockSpec((n, w), lambda i: (0, 0)) for w in widths],
        out_shape=[jax.ShapeDtypeStruct((n, w), F32) for w in widths],
        compiler_params=_params(("arbitrary",)),
        name="pre_b_sample",
    )(x2d, *ws)


def _sample_buckets():
    dist = (WINDOW - 1) - np.arange(WINDOW)
    return _t5_bucket(dist).astype(np.int32).reshape(1, WINDOW)


def _swa_step_kernel(sink_ref, rb_ref, bkt_ref, kc_ref, vc_ref, ks_ref, vs_ref, q_ref,
                     kn_ref, vn_ref, ot_ref, bias_s):
    W = WINDOW
    step = pl.program_id(0)
    rb = q_ref.shape[0]

    @pl.when(step == 0)
    def _():
        bkt = bkt_ref[...]
        for idx in range(SWA_G * SWA_KVH):
            def add_bucket(n, acc):
                return jnp.where(bkt == n, rb_ref[n, idx], acc)
            bias_s[idx:idx + 1, :] = lax.fori_loop(0, N_BUCKETS, add_bucket, jnp.zeros((1, W), F32))
        ot_ref[...] = jnp.zeros_like(ot_ref)

    eye = _eye(SWA_KVW)
    last = _iota((SWA_HD, W), 1) == W - 1
    lane = _iota(ot_ref.shape, 1)

    def row(i, carry):
        k_col = _row_to_col(ks_ref[i], eye)
        v_col = _row_to_col(vs_ref[i], eye)
        qrow = q_ref[i]
        q_cols = [_row_to_col(qrow[:, g * SWA_KVW:(g + 1) * SWA_KVW], eye) for g in range(SWA_G)]
        outs = {}
        for h in range(SWA_KVH):
            hs = slice(h * SWA_HD, (h + 1) * SWA_HD)
            kn = jnp.where(last, k_col[hs], pltpu.roll(kc_ref[i, h], W - 1, 1))
            vn = jnp.where(last, v_col[hs], pltpu.roll(vc_ref[i, h], W - 1, 1))
            kn_ref[i, h] = kn
            vn_ref[i, h] = vn
            for g in range(SWA_G):
                idx = g * SWA_KVH + h
                s = (jnp.sum(kn * q_cols[g][hs], axis=0, keepdims=True) * (SWA_HD ** -0.5)
                     + bias_s[idx:idx + 1, :])
                sink = sink_ref[idx]
                mx = jnp.maximum(jnp.max(s, axis=1, keepdims=True), sink)
                p = jnp.exp(s - mx)
                p = p / (jnp.sum(p, axis=1, keepdims=True) + jnp.exp(sink - mx))
                outs[idx] = jnp.sum(vn * p, axis=1, keepdims=True)
        col = jnp.concatenate([outs[idx] for idx in range(SWA_G * SWA_KVH)], axis=0)
        ot_ref[...] = jnp.where(lane == step * rb + i, col, ot_ref[...])
        return carry

    lax.fori_loop(0, rb, row, 0)


def _swa_step_call(kc4, vc4, ks, vs, q, wb):
    n = q.shape[0]
    rb = SWA_ROWS_PER_STEP
    bkt = jnp.asarray(_sample_buckets())
    row3 = lambda w: pl.BlockSpec((rb, 1, w), lambda i: (i, 0, 0))
    cache = pl.BlockSpec((rb, SWA_KVH, SWA_HD, WINDOW), lambda i: (i, 0, 0, 0))
    smem = pl.BlockSpec(memory_space=pltpu.SMEM)
    return pl.pallas_call(
        _swa_step_kernel,
        grid=(n // rb,),
        in_specs=[smem, smem, _full_spec(bkt),
                  cache, cache, row3(SWA_KVW), row3(SWA_KVW), row3(SWA_QW)],
        out_specs=[cache, cache, pl.BlockSpec((SWA_QW, n), lambda i: (0, 0))],
        out_shape=[jax.ShapeDtypeStruct(kc4.shape, F32), jax.ShapeDtypeStruct(vc4.shape, F32),
                   jax.ShapeDtypeStruct((SWA_QW, n), F32)],
        scratch_shapes=[pltpu.VMEM((16, WINDOW), F32)],
        compiler_params=_params(("arbitrary",)),
        name="swa_step_sample",
    )(wb["sinks"], wb["rb"], bkt, kc4, vc4, ks.reshape(n, 1, SWA_KVW), vs.reshape(n, 1, SWA_KVW),
      q.reshape(n, 1, SWA_QW))


def _post_b_kernel(x_ref, o_ref, om_ref, wo1_ref, wo2_ref, gpost_ref, xo_ref):
    xo_ref[...] = _mix_residual(x_ref[...], o_ref[...], om_ref[...], wo1_ref, wo2_ref, gpost_ref)


def _post_b_call(x2d, o, om, wb):
    args = [x2d, o, om, wb["wo1"], wb["wo2"], wb["gpost"]]
    return pl.pallas_call(
        _post_b_kernel,
        grid=(1,),
        in_specs=[_full_spec(a) for a in args],
        out_specs=_full_spec(x2d),
        out_shape=jax.ShapeDtypeStruct(x2d.shape, F32),
        compiler_params=_params(("arbitrary",)),
        name="post_b_sample",
    )(*args)


def _prep_weights(norm_mix_pre, norm_mix_post, norm_ffn_pre, norm_ffn_post, w_in_a, w_gate_up, b_gate,
                  gla_norm, w_in_b, sinks, norm_kv, w_kv, rel_bias, w_out, w_ffn_up, w_ffn_down):
    row = lambda g: g.reshape(1, -1)
    wa_in = w_in_a[0]
    c_v = 2 * GLA_KW
    c_r = c_v + GLA_VW
    c_g = c_r + GLA_VW
    c_m = c_g + GATE_RANK
    wmisc = jnp.concatenate([wa_in[:, c_g:c_m], jnp.zeros((D, 128 - GATE_RANK), F32), wa_in[:, c_m:]], axis=1)
    wg = jnp.zeros((128, GLA_KW), F32).at[:GATE_RANK].set(w_gate_up[0])
    wa = dict(
        gpre=row(norm_mix_pre[0]), wqk=_bf(wa_in[:, :c_v]), wv=_bf(wa_in[:, c_v:c_r]), wr=_bf(wa_in[:, c_r:c_g]),
        wmisc=_bf(wmisc), wg=_bf(wg), bg=row(b_gate[0]), glan=row(jnp.tile(gla_norm[0], GLA_H)),
        wo1=_bf(w_out[0][:GLA_VW]), wo2=_bf(w_out[0][GLA_VW:]), gpost=row(norm_mix_post[0]))
    wb_in = w_in_b[0]
    wq = wb_in[:, :SWA_QW].reshape(D, SWA_KVH, SWA_G, SWA_HD).transpose(0, 2, 1, 3).reshape(D, SWA_QW)
    wo1 = w_out[1][:SWA_QW].reshape(SWA_KVH, SWA_G, SWA_HD, D).transpose(1, 0, 2, 3).reshape(SWA_QW, D)
    rb = rel_bias.reshape(N_BUCKETS, SWA_KVH, SWA_G).transpose(0, 2, 1).reshape(N_BUCKETS, SWA_G * SWA_KVH)
    sk = sinks[0].reshape(SWA_KVH, SWA_G).T.reshape(SWA_G * SWA_KVH)
    wb = dict(
        gkv=row(norm_kv), wkv=_bf(w_kv), gpre=row(norm_mix_pre[1]), wq=_bf(wq), wqm=_bf(wb_in[:, SWA_QW:]),
        wo1=_bf(wo1), wo2=_bf(w_out[1][SWA_QW:]), gpost=row(norm_mix_post[1]), rb=rb, sinks=sk)
    ffn = [dict(gpre=row(norm_ffn_pre[l]), wup=_bf(w_ffn_up[l]), wdn=_bf(w_ffn_down[l]),
                gpost=row(norm_ffn_post[l])) for l in range(2)]
    return wa, wb, ffn


def kernel(x_prompt, x_sample, state_gla, cache_swa_k, cache_swa_v, cache_mem_k, cache_mem_v, mem_prompt,
           norm_mix_pre, norm_mix_post, norm_ffn_pre, norm_ffn_post, norm_mem, w_mem_kv, w_in_a, w_gate_up,
           b_gate, gla_norm, w_in_b, sinks, norm_kv, w_kv, rel_bias, w_out, w_ffn_up, w_ffn_down):
    wa, wb, ffn = _prep_weights(norm_mix_pre, norm_mix_post, norm_ffn_pre, norm_ffn_post, w_in_a, w_gate_up,
                                b_gate, gla_norm, w_in_b, sinks, norm_kv, w_kv, rel_bias, w_out, w_ffn_up,
                                w_ffn_down)
    nb, seq, _ = x_prompt.shape
    ns = x_sample.shape[0]

    mkt, mvt, kbd, vbd = _memkv_call(mem_prompt, norm_mem, w_mem_kv)
    x1, st = _mixer_a_call(x_prompt, wa, kbd[0], vbd[0])
    x2 = _ffn_call(x1.reshape(nb * seq, D), ffn[0], TM_PROMPT).reshape(nb, seq, D)
    x3, kc, vc = _mixer_b_call(x2, wb, kbd[1], vbd[1])
    y_prompt = _ffn_call(x3.reshape(nb * seq, D), ffn[1], TM_PROMPT).reshape(nb, seq, D)
    st4 = st.reshape(nb, GLA_H, GLA_DV, GLA_H, GLA_DK)
    state_prompt = jnp.stack([st4[:, h, :, h, :] for h in range(GLA_H)], axis=1).transpose(0, 1, 3, 2)[None]
    to_mem = lambda t: t.reshape(2, nb, MEM_H, MEM_HD, N_MEM).transpose(0, 1, 4, 2, 3)
    swa_shape = (nb, WINDOW, SWA_KVH, SWA_HD)

    xs = x_sample.reshape(ns, D)
    state5 = jnp.transpose(state_gla, (0, 2, 3, 4, 1))
    mk5 = jnp.transpose(cache_mem_k, (0, 1, 3, 4, 2))
    mv5 = jnp.transpose(cache_mem_v, (0, 1, 3, 4, 2))
    kc4 = jnp.transpose(cache_swa_k, (0, 2, 3, 1))
    vc4 = jnp.transpose(cache_swa_v, (0, 2, 3, 1))

    q, k, g, v, r, qm = _pre_a_call(xs, wa)
    state5_new, ot = _gla_step_call(q.T, k.T, g.T, v.T, state5)
    omt = _mem_step_call(qm, mk5, mv5, 0)
    xs1 = _post_a_call(xs, ot.T, r, omt.T, wa)
    xs2 = _ffn_call(xs1, ffn[0], ns)
    ks, vs, qb, qmb = _pre_b_call(xs2, wb)
    kn4, vn4, otb = _swa_step_call(kc4, vc4, ks, vs, qb, wb)
    omtb = _mem_step_call(qmb, mk5, mv5, 1)
    xs3 = _post_b_call(xs2, otb.T, omtb.T, wb)
    y_sample = _ffn_call(xs3, ffn[1], ns).reshape(ns, 1, D)

    return (y_prompt, y_sample, state_prompt,
            jnp.transpose(state5_new, (0, 4, 1, 2, 3)),
            kc.reshape(swa_shape), vc.reshape(swa_shape),
            jnp.transpose(kn4, (0, 3, 1, 2)), jnp.transpose(vn4, (0, 3, 1, 2)),
            to_mem(mkt), to_mem(mvt))
```

```python
import functools
import math

import numpy as np
import jax
import jax.numpy as jnp
from jax import lax
from jax.experimental import pallas as pl
from jax.experimental.pallas import tpu as pltpu

F32 = jnp.float32
BF16 = jnp.bfloat16

D = 1024
D_FF = 4 * D
N_MEM = 256
MEM_H = 4
MEM_HD = 64
MEM_W = MEM_H * MEM_HD
GLA_H = 4
GLA_DK = 96
GLA_DV = 192
GLA_KW = GLA_H * GLA_DK
GLA_VW = GLA_H * GLA_DV
GATE_RANK = 16
GATE_NORM = 16.0
SWA_HD = 64
SWA_KVH = 4
SWA_G = 3
SWA_QW = SWA_KVH * SWA_G * SWA_HD
SWA_KVW = SWA_KVH * SWA_HD
WINDOW = 128
N_BUCKETS = 32
MAX_DISTANCE = 128
EPS = 1e-6

GLA_CHUNK = 64
GLA_SUB = 16
TM_PROMPT = 512
FF_CHUNK = 512
GLA_DK_BLOCK = 32
MEM_ROWS_PER_STEP = 16
SWA_ROWS_PER_STEP = 16
V7X_VMEM_LIMIT = 56 * 1024 * 1024
NEG_INF = float("-inf")


def _bf(x):
    return x.astype(BF16)


def _dot(a, b):
    return jnp.dot(a, b, preferred_element_type=F32)


def _dot_nt(a, b):
    return lax.dot_general(a, b, (((1,), (1,)), ((), ())), preferred_element_type=F32)


def _dot_tn(a, b):
    return lax.dot_general(a, b, (((0,), (0,)), ((), ())), preferred_element_type=F32)


def _rms(x, g):
    return x * lax.rsqrt(jnp.mean(x * x, axis=-1, keepdims=True) + EPS) * g


def _split3(x):
    x1 = _bf(x)
    r1 = x - x1.astype(F32)
    x2 = _bf(r1)
    x3 = _bf(r1 - x2.astype(F32))
    return x1, x2, x3


def _exact_dot(sel, x):
    x1, x2, x3 = _split3(x)
    return _dot(sel, x1) + _dot(sel, x2) + _dot(sel, x3)


def _log_sigmoid(z):
    return jnp.minimum(z, 0.0) - jnp.log1p(jnp.exp(-jnp.abs(z)))


def _silu(z):
    return z * (1.0 / (1.0 + jnp.exp(-z)))


def _iota(shape, dim):
    return lax.broadcasted_iota(jnp.int32, shape, dim)


def _gla_k_head(lane):
    one = jnp.int32(1)
    zero = jnp.int32(0)
    return (jnp.where(lane >= GLA_DK, one, zero) + jnp.where(lane >= 2 * GLA_DK, one, zero)
            + jnp.where(lane >= 3 * GLA_DK, one, zero))


def _gla_v_head(lane):
    one = jnp.int32(1)
    zero = jnp.int32(0)
    return (jnp.where(lane >= GLA_DV, one, zero) + jnp.where(lane >= 2 * GLA_DV, one, zero)
            + jnp.where(lane >= 3 * GLA_DV, one, zero))


def _full_spec(a):
    nd = a.ndim
    return pl.BlockSpec(a.shape, lambda *_: (0,) * nd)


def _params(sem):
    return pltpu.CompilerParams(dimension_semantics=sem, vmem_limit_bytes=V7X_VMEM_LIMIT)


def _mem_softmax_pv(qm, kbd_ref, vbd_ref):
    s = _dot(_bf(qm), kbd_ref[...]) * (MEM_HD ** -0.5)
    out = None
    for h in range(MEM_H):
        sh = s[:, h * N_MEM:(h + 1) * N_MEM]
        mx = jnp.max(sh, axis=-1, keepdims=True)
        e = jnp.exp(sh - mx)
        p = e / jnp.sum(e, axis=-1, keepdims=True)
        t = _dot(_bf(p), vbd_ref[h * N_MEM:(h + 1) * N_MEM, :])
        out = t if out is None else out + t
    return out


def _memkv_kernel(mem_ref, g_ref, w_ref, k_ref, v_ref, kbd_ref, vbd_ref):
    h = _bf(_rms(mem_ref[0], g_ref[0]))
    kv = _dot(h, w_ref[0])
    k = kv[:, :MEM_W]
    v = kv[:, MEM_W:]
    kt = k.T
    k_ref[0, 0] = kt
    v_ref[0, 0] = v.T
    kt4 = jnp.concatenate([kt, kt, kt, kt], axis=1)
    keep_k = (_iota((MEM_W, MEM_H * N_MEM), 0) >> 6) == (_iota((MEM_W, MEM_H * N_MEM), 1) >> 8)
    kbd_ref[0, 0] = _bf(jnp.where(keep_k, kt4, 0.0))
    v4 = jnp.concatenate([v, v, v, v], axis=0)
    keep_v = (_iota((MEM_H * N_MEM, MEM_W), 0) >> 8) == (_iota((MEM_H * N_MEM, MEM_W), 1) >> 6)
    vbd_ref[0, 0] = _bf(jnp.where(keep_v, v4, 0.0))


def _memkv_call(mem, norm_mem, w_mem_kv):
    nb = mem.shape[0]
    nl = w_mem_kv.shape[0]
    g = norm_mem.reshape(nl, 1, D)
    w = _bf(w_mem_kv)
    return pl.pallas_call(
        _memkv_kernel,
        grid=(nl, nb),
        in_specs=[
            pl.BlockSpec((1, N_MEM, D), lambda l, b: (b, 0, 0)),
            pl.BlockSpec((1, 1, D), lambda l, b: (l, 0, 0)),
            pl.BlockSpec((1, D, 2 * MEM_W), lambda l, b: (l, 0, 0)),
        ],
        out_specs=[
            pl.BlockSpec((1, 1, MEM_W, N_MEM), lambda l, b: (l, b, 0, 0)),
            pl.BlockSpec((1, 1, MEM_W, N_MEM), lambda l, b: (l, b, 0, 0)),
            pl.BlockSpec((1, 1, MEM_W, MEM_H * N_MEM), lambda l, b: (l, b, 0, 0)),
            pl.BlockSpec((1, 1, MEM_H * N_MEM, MEM_W), lambda l, b: (l, b, 0, 0)),
        ],
        out_shape=[
            jax.ShapeDtypeStruct((nl, nb, MEM_W, N_MEM), F32),
            jax.ShapeDtypeStruct((nl, nb, MEM_W, N_MEM), F32),
            jax.ShapeDtypeStruct((nl, nb, MEM_W, MEM_H * N_MEM), BF16),
            jax.ShapeDtypeStruct((nl, nb, MEM_H * N_MEM, MEM_W), BF16),
        ],
        compiler_params=_params(("arbitrary", "arbitrary")),
        name="mem_kv",
    )(mem, g, w)


def _proj_a(x, gpre_ref, wqk_ref, wv_ref, wr_ref, wmisc_ref, wg_ref, bg_ref):
    h = _bf(_rms(x, gpre_ref[...]))
    qk = _dot(h, wqk_ref[...])
    q = qk[:, :GLA_KW] * (GLA_DK ** -0.5)
    k = qk[:, GLA_KW:]
    v = _dot(h, wv_ref[...])
    r = _dot(h, wr_ref[...])
    misc = _dot(h, wmisc_ref[...])
    glr = misc[:, :128]
    qm = misc[:, 128:]
    g = _log_sigmoid(_dot(_bf(glr), wg_ref[...]) + bg_ref[...]) * (1.0 / GATE_NORM)
    return q, k, g, v, r, qm


def _gla_out_gate(o, r, glan_ref):
    vh = _gla_v_head(_iota(o.shape, 1))
    o2 = o * o
    scale = jnp.zeros_like(o)
    for h in range(GLA_H):
        ss = jnp.sum(jnp.where(vh == h, o2, 0.0), axis=-1, keepdims=True) * (1.0 / GLA_DV)
        scale = jnp.where(vh == h, lax.rsqrt(ss + EPS), scale)
    return o * scale * glan_ref[...] * _silu(r)


def _mix_residual(x, o_main, o_mem, wo1_ref, wo2_ref, gpost_ref):
    mix = _dot(_bf(o_main), wo1_ref[...]) + _dot(_bf(o_mem), wo2_ref[...])
    return x + _rms(mix, gpost_ref[...])


def _mixer_a_kernel(x_ref, gpre_ref, wqk_ref, wv_ref, wr_ref, wmisc_ref, wg_ref, bg_ref, glan_ref,
                    kbd_ref, vbd_ref, wo1_ref, wo2_ref, gpost_ref,
                    xo_ref, st_ref,
                    q_s, k_s, g_s, v_s, o_s, state_s, p_s):
    C = GLA_CHUNK
    tm = x_ref.shape[1]
    t = pl.program_id(1)

    @pl.when(t == 0)
    def _():
        state_s[...] = jnp.zeros_like(state_s)

    x = x_ref[0]
    q, k, g, v, r, qm = _proj_a(x, gpre_ref, wqk_ref, wv_ref, wr_ref, wmisc_ref, wg_ref, bg_ref)
    q_s[...] = q
    k_s[...] = k
    g_s[...] = g
    v_s[...] = v

    ri = _iota((C, GLA_KW), 0)
    kh = _gla_k_head(_iota((C, GLA_KW), 1))
    khcat = jnp.concatenate([kh, kh, kh], axis=1)
    vh = _gla_v_head(_iota((C, GLA_VW), 1))
    tri = _bf(jnp.where(_iota((C, C), 0) >= _iota((C, C), 1), 1.0, 0.0))
    d_rs = _iota((C, 4 * C), 0) - (_iota((C, 4 * C), 1) & (C - 1))
    band = jnp.where((d_rs >= 0) & (d_rs <= (_iota((C, 4 * C), 0) & (GLA_SUB - 1))), d_rs, -1)
    ebc = _bf(jnp.where(_gla_k_head(_iota((GLA_KW, 4 * C), 0)) == (_iota((GLA_KW, 4 * C), 1) >> 6),
                        1.0, 0.0))
    blockmask = _gla_v_head(_iota((GLA_VW, GLA_KW), 0)) == _gla_k_head(_iota((GLA_VW, GLA_KW), 1))

    def chunk(c, carry):
        r0 = pl.multiple_of(c * C, C)
        qc = q_s[pl.ds(r0, C), :]
        kc = k_s[pl.ds(r0, C), :]
        gc = g_s[pl.ds(r0, C), :]
        vc = v_s[pl.ds(r0, C), :]
        b = _exact_dot(tri, gc)

        ref1 = jnp.broadcast_to(b[31:32, :], b.shape)
        ref2 = jnp.where(ri < 32, jnp.broadcast_to(b[15:16, :], b.shape),
                         jnp.broadcast_to(b[47:48, :], b.shape))
        q1 = qc * jnp.exp(jnp.minimum(b - ref1, 0.0))
        k1 = kc * jnp.exp(jnp.minimum(ref1 - b, 0.0))
        q2 = qc * jnp.exp(jnp.minimum(b - ref2, 0.0))
        k2 = kc * jnp.exp(jnp.minimum(ref2 - b, 0.0))
        qcat = jnp.concatenate([
            jnp.where(ri >= 32, q1, 0.0),
            jnp.where((ri >= 16) & (ri < 32), q2, 0.0),
            jnp.where(ri >= 48, q2, 0.0)], axis=1)
        kcat = jnp.concatenate([
            jnp.where(ri < 32, k1, 0.0),
            jnp.where(ri < 16, k2, 0.0),
            jnp.where((ri >= 32) & (ri < 48), k2, 0.0)], axis=1)
        kst = _bf(jnp.concatenate([jnp.where(khcat == h, kcat, 0.0) for h in range(GLA_H)], axis=0))
        a_off = _dot_nt(_bf(qcat), kst)

        for dlt in range(GLA_SUB):
            if dlt == 0:
                pr = qc * kc
            else:
                kd = pltpu.roll(kc, dlt, 0)
                bd = pltpu.roll(b, dlt, 0)
                pr = qc * kd * jnp.exp(jnp.minimum(b - bd, 0.0))
            p_s[dlt * C:(dlt + 1) * C, :] = _bf(pr)
        rsum = _dot(p_s[...], ebc)
        a_diag = jnp.zeros((C, 4 * C), F32)
        for dlt in range(GLA_SUB):
            a_diag = jnp.where(band == dlt, rsum[dlt * C:(dlt + 1) * C, :], a_diag)

        vst = _bf(jnp.concatenate([jnp.where(vh == h, vc, 0.0) for h in range(GLA_H)], axis=0))
        o_intra = _dot(_bf(a_off + a_diag), vst)
        st = state_s[...]
        o_inter = _dot_nt(_bf(qc * jnp.exp(b)), _bf(st))
        o_s[pl.ds(r0, C), :] = o_intra + o_inter

        blast = jnp.broadcast_to(b[C - 1:C, :], b.shape)
        kdec = kc * jnp.exp(blast - b)
        kvt = _dot_tn(_bf(vc), _bf(kdec))
        decay = jnp.exp(b[C - 1:C, :])
        state_s[...] = st * decay + jnp.where(blockmask, kvt, 0.0)
        return carry

    lax.fori_loop(0, tm // C, chunk, 0)
    st_ref[0] = state_s[...]

    o_main = _gla_out_gate(o_s[...], r, glan_ref)
    o_mem = _mem_softmax_pv(qm, kbd_ref.at[0], vbd_ref.at[0])
    xo_ref[0] = _mix_residual(x, o_main, o_mem, wo1_ref, wo2_ref, gpost_ref)


def _mixer_a_call(x, wa, kbd, vbd):
    nb, seq, _ = x.shape
    tm = TM_PROMPT
    weights = [wa["gpre"], wa["wqk"], wa["wv"], wa["wr"], wa["wmisc"], wa["wg"], wa["bg"], wa["glan"]]
    tail = [wa["wo1"], wa["wo2"], wa["gpost"]]
    return pl.pallas_call(
        _mixer_a_kernel,
        grid=(nb, seq // tm),
        in_specs=([pl.BlockSpec((1, tm, D), lambda b, t: (b, t, 0))]
                  + [_full_spec(w) for w in weights]
                  + [pl.BlockSpec((1, MEM_W, MEM_H * N_MEM), lambda b, t: (b, 0, 0)),
                     pl.BlockSpec((1, MEM_H * N_MEM, MEM_W), lambda b, t: (b, 0, 0))]
                  + [_full_spec(w) for w in tail]),
        out_specs=[pl.BlockSpec((1, tm, D), lambda b, t: (b, t, 0)),
                   pl.BlockSpec((1, GLA_VW, GLA_KW), lambda b, t: (b, 0, 0))],
        out_shape=[jax.ShapeDtypeStruct((nb, seq, D), F32),
                   jax.ShapeDtypeStruct((nb, GLA_VW, GLA_KW), F32)],
        scratch_shapes=[
            pltpu.VMEM((tm, GLA_KW), F32), pltpu.VMEM((tm, GLA_KW), F32), pltpu.VMEM((tm, GLA_KW), F32),
            pltpu.VMEM((tm, GLA_VW), F32), pltpu.VMEM((tm, GLA_VW), F32),
            pltpu.VMEM((GLA_VW, GLA_KW), F32),
            pltpu.VMEM((GLA_SUB * GLA_CHUNK, GLA_KW), BF16),
        ],
        compiler_params=_params(("arbitrary", "arbitrary")),
        name="mixer_a_prompt",
    )(x, *weights, kbd, vbd, *tail)


def _ffn_kernel(x_ref, gpre_ref, wup_ref, wdn_ref, gpost_ref, o_ref, acc_s):
    x = x_ref[...]
    h = _bf(_rms(x, gpre_ref[...]))
    for j in range(D_FF // FF_CHUNK):
        u = _dot(h, wup_ref[:, j * FF_CHUNK:(j + 1) * FF_CHUNK])
        u = jnp.maximum(u, 0.0)
        d = _dot(_bf(u * u), wdn_ref[j * FF_CHUNK:(j + 1) * FF_CHUNK, :])
        if j == 0:
            acc_s[...] = d
        else:
            acc_s[...] += d
    o_ref[...] = x + _rms(acc_s[...], gpost_ref[...])


def _ffn_call(x2d, wf, tm):
    n = x2d.shape[0]
    ws = [wf["gpre"], wf["wup"], wf["wdn"], wf["gpost"]]
    return pl.pallas_call(
        _ffn_kernel,
        grid=(n // tm,),
        in_specs=[pl.BlockSpec((tm, D), lambda i: (i, 0))] + [_full_spec(w) for w in ws],
        out_specs=pl.BlockSpec((tm, D), lambda i: (i, 0)),
        out_shape=jax.ShapeDtypeStruct((n, D), F32),
        scratch_shapes=[pltpu.VMEM((tm, D), F32)],
        compiler_params=_params(("arbitrary",)),
        name="ffn",
    )(x2d, *ws)


def _t5_bucket(dist):
    max_exact = N_BUCKETS // 2
    n = np.maximum(dist, 0)
    nf = np.maximum(n, 1).astype(np.float32)
    large = max_exact + (np.log(nf / np.float32(max_exact)) / np.float32(math.log(MAX_DISTANCE / max_exact))
                         * np.float32(N_BUCKETS - max_exact)).astype(np.int32)
    large = np.minimum(large, N_BUCKETS - 1)
    return np.where(n < max_exact, n, large)


def _swa_bias_tables():
    qi = np.arange(WINDOW)[:, None] + WINDOW
    kj = np.arange(2 * WINDOW)[None, :]
    dist = qi - kj
    valid = (dist >= 0) & (dist < WINDOW)
    return np.where(valid, _t5_bucket(dist), -1).astype(np.int32)


def _mixer_b_kernel(rb_ref, sink_ref,
                    x_ref, bkt_ref, gkv_ref, wkv_ref, gpre_ref, wq_ref, wqm_ref,
                    kbd_ref, vbd_ref, wo1_ref, wo2_ref, gpost_ref,
                    xo_ref, kc_ref, vc_ref,
                    kbuf, vbuf, o_s, bias_s):
    W = WINDOW
    tm = x_ref.shape[1]
    bb = pl.program_id(0)
    t = pl.program_id(1)
    nheads = SWA_G * SWA_KVH

    @pl.when((bb == 0) & (t == 0))
    def _():
        bkt = bkt_ref[...]
        for i in range(nheads):
            def add_bucket(n, acc):
                return jnp.where(bkt == n, rb_ref[n, i], acc)
            tab = lax.fori_loop(0, N_BUCKETS, add_bucket, jnp.zeros((W, 2 * W), F32))
            bias_s[i] = jnp.where(bkt < 0, NEG_INF, tab)

    @pl.when(t == 0)
    def _():
        kbuf[0:W, :] = jnp.zeros((W, SWA_KVW), F32)
        vbuf[0:W, :] = jnp.zeros((W, SWA_KVW), F32)

    x = x_ref[0]
    kv = _dot(_bf(_rms(x, gkv_ref[...])), wkv_ref[...])
    kbuf[W:W + tm, :] = kv[:, :SWA_KVW]
    vbuf[W:W + tm, :] = kv[:, SWA_KVW:]
    kc_ref[0] = kv[tm - W:, :SWA_KVW]
    vc_ref[0] = kv[tm - W:, SWA_KVW:]

    h = _bf(_rms(x, gpre_ref[...]))
    q = _dot(h, wq_ref[...])
    qm = _dot(h, wqm_ref[...])

    lane_head = _iota((W, SWA_KVW), 1) >> 6
    key_head = _iota((2 * W, SWA_KVW), 1) >> 6
    own_block = _iota((W, 2 * W), 1) >= W
    for j in range(tm // W):
        kb = kbuf[j * W:j * W + 2 * W, :]
        vb = vbuf[j * W:j * W + 2 * W, :]
        qrows = []
        for gi in range(SWA_G):
            qg = q[j * W:(j + 1) * W, gi * SWA_KVW:(gi + 1) * SWA_KVW]
            for hh in range(SWA_KVH):
                qrows.append(jnp.where(lane_head == hh, qg, 0.0))
        s_all = _dot_nt(_bf(jnp.concatenate(qrows, axis=0)), _bf(kb))
        vst = _bf(jnp.concatenate([jnp.where(key_head == hh, vb, 0.0) for hh in range(SWA_KVH)], axis=0))
        for gi in range(SWA_G):
            ps = []
            for hh in range(SWA_KVH):
                i = gi * SWA_KVH + hh
                s = s_all[i * W:(i + 1) * W, :] * (SWA_HD ** -0.5) + bias_s[i]
                if j == 0:
                    s = jnp.where(own_block | (t > 0), s, NEG_INF)
                sink = sink_ref[i]
                mx = jnp.maximum(jnp.max(s, axis=-1, keepdims=True), sink)
                p = jnp.exp(s - mx)
                p = p / (jnp.sum(p, axis=-1, keepdims=True) + jnp.exp(sink - mx))
                ps.append(_bf(p))
            o_s[j * W:(j + 1) * W, gi * SWA_KVW:(gi + 1) * SWA_KVW] = _dot(jnp.concatenate(ps, axis=1), vst)

    kbuf[0:W, :] = kbuf[tm:tm + W, :]
    vbuf[0:W, :] = vbuf[tm:tm + W, :]

    o_mem = _mem_softmax_pv(qm, kbd_ref.at[0], vbd_ref.at[0])
    xo_ref[0] = _mix_residual(x, o_s[...], o_mem, wo1_ref, wo2_ref, gpost_ref)


def _mixer_b_call(x, wb, kbd, vbd):
    nb, seq, _ = x.shape
    tm = TM_PROMPT
    bkt = jnp.asarray(_swa_bias_tables())
    head = [bkt, wb["gkv"], wb["wkv"], wb["gpre"], wb["wq"], wb["wqm"]]
    tail = [wb["wo1"], wb["wo2"], wb["gpost"]]
    smem = pl.BlockSpec(memory_space=pltpu.SMEM)
    return pl.pallas_call(
        _mixer_b_kernel,
        grid=(nb, seq // tm),
        in_specs=([smem, smem, pl.BlockSpec((1, tm, D), lambda b, t: (b, t, 0))]
                  + [_full_spec(w) for w in head]
                  + [pl.BlockSpec((1, MEM_W, MEM_H * N_MEM), lambda b, t: (b, 0, 0)),
                     pl.BlockSpec((1, MEM_H * N_MEM, MEM_W), lambda b, t: (b, 0, 0))]
                  + [_full_spec(w) for w in tail]),
        out_specs=[pl.BlockSpec((1, tm, D), lambda b, t: (b, t, 0)),
                   pl.BlockSpec((1, WINDOW, SWA_KVW), lambda b, t: (b, 0, 0)),
                   pl.BlockSpec((1, WINDOW, SWA_KVW), lambda b, t: (b, 0, 0))],
        out_shape=[jax.ShapeDtypeStruct((nb, seq, D), F32),
                   jax.ShapeDtypeStruct((nb, WINDOW, SWA_KVW), F32),
                   jax.ShapeDtypeStruct((nb, WINDOW, SWA_KVW), F32)],
        scratch_shapes=[
            pltpu.VMEM((tm + WINDOW, SWA_KVW), F32), pltpu.VMEM((tm + WINDOW, SWA_KVW), F32),
            pltpu.VMEM((tm, SWA_QW), F32),
            pltpu.VMEM((SWA_G * SWA_KVH, WINDOW, 2 * WINDOW), F32),
        ],
        compiler_params=_params(("arbitrary", "arbitrary")),
        name="mixer_b_prompt",
    )(wb["rb"], wb["sinks"], x, *head, kbd, vbd, *tail)


def _row_to_col(row, eye):
    return jnp.sum(jnp.where(eye, jnp.broadcast_to(row, eye.shape), 0.0), axis=1, keepdims=True)


def _eye(n):
    return _iota((n, n), 0) == _iota((n, n), 1)


def _pre_a_kernel(x_ref, gpre_ref, wqk_ref, wv_ref, wr_ref, wmisc_ref, wg_ref, bg_ref,
                  q_ref, k_ref, g_ref, v_ref, r_ref, qm_ref):
    q, k, g, v, r, qm = _proj_a(x_ref[...], gpre_ref, wqk_ref, wv_ref, wr_ref, wmisc_ref, wg_ref, bg_ref)
    q_ref[...] = q
    k_ref[...] = k
    g_ref[...] = g
    v_ref[...] = v
    r_ref[...] = r
    qm_ref[...] = qm


def _pre_a_call(x2d, wa):
    n = x2d.shape[0]
    ws = [wa["gpre"], wa["wqk"], wa["wv"], wa["wr"], wa["wmisc"], wa["wg"], wa["bg"]]
    shapes = [(n, GLA_KW), (n, GLA_KW), (n, GLA_KW), (n, GLA_VW), (n, GLA_VW), (n, MEM_W)]
    return pl.pallas_call(
        _pre_a_kernel,
        grid=(1,),
        in_specs=[_full_spec(x2d)] + [_full_spec(w) for w in ws],
        out_specs=[pl.BlockSpec(s, lambda i: (0, 0)) for s in shapes],
        out_shape=[jax.ShapeDtypeStruct(s, F32) for s in shapes],
        compiler_params=_params(("arbitrary",)),
        name="pre_a_sample",
    )(x2d, *ws)


def _gla_step_kernel(qt_ref, kt_ref, gt_ref, vt_ref, s_ref, so_ref, ot_ref):
    c = pl.program_id(1)
    vt = vt_ref[...]
    acc = jnp.zeros_like(vt)
    for d in range(GLA_DK_BLOCK):
        s_new = jnp.exp(gt_ref[d:d + 1, :]) * s_ref[0, 0, d] + kt_ref[d:d + 1, :] * vt
        so_ref[0, 0, d] = s_new
        acc = acc + qt_ref[d:d + 1, :] * s_new

    @pl.when(c == 0)
    def _():
        ot_ref[...] = acc

    @pl.when(c > 0)
    def _():
        ot_ref[...] += acc


def _gla_step_call(qt, kt, gt, vt, state5):
    n = qt.shape[1]
    nblk = GLA_DK // GLA_DK_BLOCK
    rows = pl.BlockSpec((GLA_DK_BLOCK, n), lambda h, c: (h * nblk + c, 0))
    head = pl.BlockSpec((GLA_DV, n), lambda h, c: (h, 0))
    st = pl.BlockSpec((1, 1, GLA_DK_BLOCK, GLA_DV, n), lambda h, c: (0, h, c, 0, 0))
    return pl.pallas_call(
        _gla_step_kernel,
        grid=(GLA_H, nblk),
        in_specs=[rows, rows, rows, head, st],
        out_specs=[st, head],
        out_shape=[jax.ShapeDtypeStruct(state5.shape, F32), jax.ShapeDtypeStruct((GLA_VW, n), F32)],
        compiler_params=_params(("arbitrary", "arbitrary")),
        name="gla_step_sample",
    )(qt, kt, gt, vt, state5)


def _mem_step_kernel(qm_ref, mk_ref, mv_ref, o_ref):
    own = (_iota((8, MEM_W), 1) >> 6) == _iota((8, MEM_W), 0)
    for i in range(qm_ref.shape[0]):
        q8 = jnp.where(own, jnp.broadcast_to(qm_ref[i], (8, MEM_W)), 0.0)
        s = _dot(_bf(q8), _bf(mk_ref[0, i].reshape(MEM_W, N_MEM))) * (MEM_HD ** -0.5)
        e = jnp.exp(s - jnp.max(s, axis=1, keepdims=True))
        p = e / jnp.sum(e, axis=1, keepdims=True)
        res = _dot_nt(_bf(p), _bf(mv_ref[0, i].reshape(MEM_W, N_MEM)))
        o_ref[i] = jnp.sum(jnp.where(own, res, 0.0), axis=0, keepdims=True)


def _mem_step_call(qm, mk5, mv5, layer):
    n = qm.shape[0]
    rb = MEM_ROWS_PER_STEP
    blk = pl.BlockSpec((1, rb, MEM_H, MEM_HD, N_MEM), lambda i: (layer, i, 0, 0, 0))
    rows = pl.BlockSpec((rb, 1, MEM_W), lambda i: (i, 0, 0))
    return pl.pallas_call(
        _mem_step_kernel,
        grid=(n // rb,),
        in_specs=[rows, blk, blk],
        out_specs=rows,
        out_shape=jax.ShapeDtypeStruct((n, 1, MEM_W), F32),
        compiler_params=_params(("arbitrary",)),
        name="mem_step_sample",
    )(qm.reshape(n, 1, MEM_W), mk5, mv5).reshape(n, MEM_W)


def _post_a_kernel(x_ref, o_ref, r_ref, om_ref, glan_ref, wo1_ref, wo2_ref, gpost_ref, xo_ref):
    o_main = _gla_out_gate(o_ref[...], r_ref[...], glan_ref)
    xo_ref[...] = _mix_residual(x_ref[...], o_main, om_ref[...], wo1_ref, wo2_ref, gpost_ref)


def _post_a_call(x2d, o, r, om, wa):
    args = [x2d, o, r, om, wa["glan"], wa["wo1"], wa["wo2"], wa["gpost"]]
    return pl.pallas_call(
        _post_a_kernel,
        grid=(1,),
        in_specs=[_full_spec(a) for a in args],
        out_specs=_full_spec(x2d),
        out_shape=jax.ShapeDtypeStruct(x2d.shape, F32),
        compiler_params=_params(("arbitrary",)),
        name="post_a_sample",
    )(*args)


def _pre_b_kernel(x_ref, gkv_ref, wkv_ref, gpre_ref, wq_ref, wqm_ref, ks_ref, vs_ref, q_ref, qm_ref):
    x = x_ref[...]
    kv = _dot(_bf(_rms(x, gkv_ref[...])), wkv_ref[...])
    ks_ref[...] = kv[:, :SWA_KVW]
    vs_ref[...] = kv[:, SWA_KVW:]
    h = _bf(_rms(x, gpre_ref[...]))
    q_ref[...] = _dot(h, wq_ref[...])
    qm_ref[...] = _dot(h, wqm_ref[...])


def _pre_b_call(x2d, wb):
    n = x2d.shape[0]
    ws = [wb["gkv"], wb["wkv"], wb["gpre"], wb["wq"], wb["wqm"]]
    widths = [SWA_KVW, SWA_KVW, SWA_QW, MEM_W]
    return pl.pallas_call(
        _pre_b_kernel,
        grid=(1,),
        in_specs=[_full_spec(x2d)] + [_full_spec(w) for w in ws],
        out_specs=[pl.BlockSpec((n, w), lambda i: (0, 0)) for w in widths],
        out_shape=[jax.ShapeDtypeStruct((n, w), F32) for w in widths],
        compiler_params=_params(("arbitrary",)),
        name="pre_b_sample",
    )(x2d, *ws)


def _sample_buckets():
    dist = (WINDOW - 1) - np.arange(WINDOW)
    return _t5_bucket(dist).astype(np.int32).reshape(1, WINDOW)


def _swa_step_kernel(sink_ref, rb_ref, bkt_ref, kc_ref, vc_ref, kst_ref, vst_ref, q_ref,
                     kn_ref, vn_ref, o_ref, bias_s, sink_s, s_s, p_s):
    W = WINDOW
    R = SWA_KVH * 8

    @pl.when(pl.program_id(0) == 0)
    def _():
        bkt = bkt_ref[...]
        rid = _iota((R, W), 0)
        bias = jnp.zeros((R, W), F32)
        sink = jnp.zeros((R, W), F32)
        for h in range(SWA_KVH):
            for g in range(SWA_G):
                idx = g * SWA_KVH + h
                def add_bucket(n, acc):
                    return jnp.where(bkt == n, rb_ref[n, idx], acc)
                brow = lax.fori_loop(0, N_BUCKETS, add_bucket, jnp.zeros((1, W), F32))
                bias = jnp.where(rid == h * 8 + g, brow, bias)
                sink = jnp.where(rid == h * 8 + g, sink_ref[idx], sink)
        bias_s[...] = bias
        sink_s[...] = sink

    rb = q_ref.shape[0]
    base = pl.program_id(0) * rb
    last = _iota((SWA_KVW, W), 1) == W - 1
    own = (_iota((R, SWA_KVW), 1) >> 6) == (_iota((R, SWA_KVW), 0) >> 3)
    kst = kst_ref[...]
    vst = vst_ref[...]
    for i in range(rb):
        shift = W - 1 - (base + i)
        kn = jnp.where(last, pltpu.roll(kst, shift, 1), pltpu.roll(kc_ref[i].reshape(SWA_KVW, W), W - 1, 1))
        vn = jnp.where(last, pltpu.roll(vst, shift, 1), pltpu.roll(vc_ref[i].reshape(SWA_KVW, W), W - 1, 1))
        kn_ref[i] = kn.reshape(SWA_KVH, SWA_HD, W)
        vn_ref[i] = vn.reshape(SWA_KVH, SWA_HD, W)
        q32 = jnp.where(own, jnp.concatenate([q_ref[i]] * SWA_KVH, axis=0), 0.0)
        s_s[i * R:(i + 1) * R, :] = _dot(_bf(q32), _bf(kn))
    s = s_s[...] * (SWA_HD ** -0.5) + jnp.concatenate([bias_s[...]] * rb, axis=0)
    sink = jnp.concatenate([sink_s[...]] * rb, axis=0)
    mx = jnp.maximum(jnp.max(s, axis=1, keepdims=True), sink)
    p = jnp.exp(s - mx)
    p_s[...] = _bf(p / (jnp.sum(p, axis=1, keepdims=True) + jnp.exp(sink - mx)))
    for i in range(rb):
        vn = vn_ref[i].reshape(SWA_KVW, W)
        res = jnp.where(own, _dot_nt(p_s[i * R:(i + 1) * R, :], _bf(vn)), 0.0)
        o_ref[i] = res[0:8] + res[8:16] + res[16:24] + res[24:32]


def _swa_step_call(kc4, vc4, ks, vs, q, wb):
    n = q.shape[0]
    rb = SWA_ROWS_PER_STEP
    bkt = jnp.asarray(_sample_buckets())
    q8 = jnp.pad(q.reshape(n, SWA_G, SWA_KVW), ((0, 0), (0, 8 - SWA_G), (0, 0)))
    row3 = lambda r, w: pl.BlockSpec((rb, r, w), lambda i: (i, 0, 0))
    cache = pl.BlockSpec((rb, SWA_KVH, SWA_HD, WINDOW), lambda i: (i, 0, 0, 0))
    smem = pl.BlockSpec(memory_space=pltpu.SMEM)
    kn4, vn4, o8 = pl.pallas_call(
        _swa_step_kernel,
        grid=(n // rb,),
        in_specs=[smem, smem, _full_spec(bkt),
                  cache, cache, pl.BlockSpec((SWA_KVW, n), lambda i: (0, 0)),
                  pl.BlockSpec((SWA_KVW, n), lambda i: (0, 0)), row3(8, SWA_KVW)],
        out_specs=[cache, cache, row3(8, SWA_KVW)],
        out_shape=[jax.ShapeDtypeStruct(kc4.shape, F32), jax.ShapeDtypeStruct(vc4.shape, F32),
                   jax.ShapeDtypeStruct((n, 8, SWA_KVW), F32)],
        scratch_shapes=[pltpu.VMEM((SWA_KVH * 8, WINDOW), F32), pltpu.VMEM((SWA_KVH * 8, WINDOW), F32),
                        pltpu.VMEM((rb * SWA_KVH * 8, WINDOW), F32), pltpu.VMEM((rb * SWA_KVH * 8, WINDOW), BF16)],
        compiler_params=_params(("arbitrary",)),
        name="swa_step_sample",
    )(wb["sinks"], wb["rb"], bkt, kc4, vc4, ks.T, vs.T, q8)
    return kn4, vn4, o8[:, :SWA_G].reshape(n, SWA_QW)


def _post_b_kernel(x_ref, o_ref, om_ref, wo1_ref, wo2_ref, gpost_ref, xo_ref):
    xo_ref[...] = _mix_residual(x_ref[...], o_ref[...], om_ref[...], wo1_ref, wo2_ref, gpost_ref)


def _post_b_call(x2d, o, om, wb):
    args = [x2d, o, om, wb["wo1"], wb["wo2"], wb["gpost"]]
    return pl.pallas_call(
        _post_b_kernel,
        grid=(1,),
        in_specs=[_full_spec(a) for a in args],
        out_specs=_full_spec(x2d),
        out_shape=jax.ShapeDtypeStruct(x2d.shape, F32),
        compiler_params=_params(("arbitrary",)),
        name="post_b_sample",
    )(*args)


def _prep_weights(norm_mix_pre, norm_mix_post, norm_ffn_pre, norm_ffn_post, w_in_a, w_gate_up, b_gate,
                  gla_norm, w_in_b, sinks, norm_kv, w_kv, rel_bias, w_out, w_ffn_up, w_ffn_down):
    row = lambda g: g.reshape(1, -1)
    wa_in = w_in_a[0]
    c_v = 2 * GLA_KW
    c_r = c_v + GLA_VW
    c_g = c_r + GLA_VW
    c_m = c_g + GATE_RANK
    wmisc = jnp.concatenate([wa_in[:, c_g:c_m], jnp.zeros((D, 128 - GATE_RANK), F32), wa_in[:, c_m:]], axis=1)
    wg = jnp.zeros((128, GLA_KW), F32).at[:GATE_RANK].set(w_gate_up[0])
    wa = dict(
        gpre=row(norm_mix_pre[0]), wqk=_bf(wa_in[:, :c_v]), wv=_bf(wa_in[:, c_v:c_r]), wr=_bf(wa_in[:, c_r:c_g]),
        wmisc=_bf(wmisc), wg=_bf(wg), bg=row(b_gate[0]), glan=row(jnp.tile(gla_norm[0], GLA_H)),
        wo1=_bf(w_out[0][:GLA_VW]), wo2=_bf(w_out[0][GLA_VW:]), gpost=row(norm_mix_post[0]))
    wb_in = w_in_b[0]
    wq = wb_in[:, :SWA_QW].reshape(D, SWA_KVH, SWA_G, SWA_HD).transpose(0, 2, 1, 3).reshape(D, SWA_QW)
    wo1 = w_out[1][:SWA_QW].reshape(SWA_KVH, SWA_G, SWA_HD, D).transpose(1, 0, 2, 3).reshape(SWA_QW, D)
    rb = rel_bias.reshape(N_BUCKETS, SWA_KVH, SWA_G).transpose(0, 2, 1).reshape(N_BUCKETS, SWA_G * SWA_KVH)
    sk = sinks[0].reshape(SWA_KVH, SWA_G).T.reshape(SWA_G * SWA_KVH)
    wb = dict(
        gkv=row(norm_kv), wkv=_bf(w_kv), gpre=row(norm_mix_pre[1]), wq=_bf(wq), wqm=_bf(wb_in[:, SWA_QW:]),
        wo1=_bf(wo1), wo2=_bf(w_out[1][SWA_QW:]), gpost=row(norm_mix_post[1]), rb=rb, sinks=sk)
    ffn = [dict(gpre=row(norm_ffn_pre[l]), wup=_bf(w_ffn_up[l]), wdn=_bf(w_ffn_down[l]),
                gpost=row(norm_ffn_post[l])) for l in range(2)]
    return wa, wb, ffn


def kernel(x_prompt, x_sample, state_gla, cache_swa_k, cache_swa_v, cache_mem_k, cache_mem_v, mem_prompt,
           norm_mix_pre, norm_mix_post, norm_ffn_pre, norm_ffn_post, norm_mem, w_mem_kv, w_in_a, w_gate_up,
           b_gate, gla_norm, w_in_b, sinks, norm_kv, w_kv, rel_bias, w_out, w_ffn_up, w_ffn_down):
    wa, wb, ffn = _prep_weights(norm_mix_pre, norm_mix_post, norm_ffn_pre, norm_ffn_post, w_in_a, w_gate_up,
                                b_gate, gla_norm, w_in_b, sinks, norm_kv, w_kv, rel_bias, w_out, w_ffn_up,
                                w_ffn_down)
    nb, seq, _ = x_prompt.shape
    ns = x_sample.shape[0]

    mkt, mvt, kbd, vbd = _memkv_call(mem_prompt, norm_mem, w_mem_kv)
    x1, st = _mixer_a_call(x_prompt, wa, kbd[0], vbd[0])
    x2 = _ffn_call(x1.reshape(nb * seq, D), ffn[0], TM_PROMPT).reshape(nb, seq, D)
    x3, kc, vc = _mixer_b_call(x2, wb, kbd[1], vbd[1])
    y_prompt = _ffn_call(x3.reshape(nb * seq, D), ffn[1], TM_PROMPT).reshape(nb, seq, D)
    st4 = st.reshape(nb, GLA_H, GLA_DV, GLA_H, GLA_DK)
    state_prompt = jnp.stack([st4[:, h, :, h, :] for h in range(GLA_H)], axis=1).transpose(0, 1, 3, 2)[None]
    to_mem = lambda t: t.reshape(2, nb, MEM_H, MEM_HD, N_MEM).transpose(0, 1, 4, 2, 3)
    swa_shape = (nb, WINDOW, SWA_KVH, SWA_HD)

    xs = x_sample.reshape(ns, D)
    state5 = jnp.transpose(state_gla, (0, 2, 3, 4, 1))
    mk5 = jnp.transpose(cache_mem_k, (0, 1, 3, 4, 2))
    mv5 = jnp.transpose(cache_mem_v, (0, 1, 3, 4, 2))
    kc4 = jnp.transpose(cache_swa_k, (0, 2, 3, 1))
    vc4 = jnp.transpose(cache_swa_v, (0, 2, 3, 1))

    q, k, g, v, r, qm = _pre_a_call(xs, wa)
    state5_new, ot = _gla_step_call(q.T, k.T, g.T, v.T, state5)
    om = _mem_step_call(qm, mk5, mv5, 0)
    xs1 = _post_a_call(xs, ot.T, r, om, wa)
    xs2 = _ffn_call(xs1, ffn[0], ns)
    ks, vs, qb, qmb = _pre_b_call(xs2, wb)
    kn4, vn4, ob = _swa_step_call(kc4, vc4, ks, vs, qb, wb)
    omb = _mem_step_call(qmb, mk5, mv5, 1)
    xs3 = _post_b_call(xs2, ob, omb, wb)
    y_sample = _ffn_call(xs3, ffn[1], ns).reshape(ns, 1, D)

    return (y_prompt, y_sample, state_prompt,
            jnp.transpose(state5_new, (0, 4, 1, 2, 3)),
            kc.reshape(swa_shape), vc.reshape(swa_shape),
            jnp.transpose(kn4, (0, 3, 1, 2)), jnp.transpose(vn4, (0, 3, 1, 2)),
            to_mem(mkt), to_mem(mvt))
```

```python
import functools
import math

import numpy as np
import jax
import jax.numpy as jnp
from jax import lax
from jax.experimental import pallas as pl
from jax.experimental.pallas import tpu as pltpu

F32 = jnp.float32
BF16 = jnp.bfloat16

D = 1024
D_FF = 4 * D
N_MEM = 256
MEM_H = 4
MEM_HD = 64
MEM_W = MEM_H * MEM_HD
GLA_H = 4
GLA_DK = 96
GLA_DV = 192
GLA_KW = GLA_H * GLA_DK
GLA_VW = GLA_H * GLA_DV
GATE_RANK = 16
GATE_NORM = 16.0
SWA_HD = 64
SWA_KVH = 4
SWA_G = 3
SWA_QW = SWA_KVH * SWA_G * SWA_HD
SWA_KVW = SWA_KVH * SWA_HD
WINDOW = 128
N_BUCKETS = 32
MAX_DISTANCE = 128
EPS = 1e-6

GLA_CHUNK = 64
GLA_SUB = 16
TM_PROMPT = 512
FF_CHUNK = 512
GLA_DK_BLOCK = 32
MEM_ROWS_PER_STEP = 16
SWA_ROWS_PER_STEP = 16
V7X_VMEM_LIMIT = 56 * 1024 * 1024
NEG_INF = float("-inf")


def _bf(x):
    return x.astype(BF16)


def _dot(a, b):
    return jnp.dot(a, b, preferred_element_type=F32)


def _dot_nt(a, b):
    return lax.dot_general(a, b, (((1,), (1,)), ((), ())), preferred_element_type=F32)


def _dot_tn(a, b):
    return lax.dot_general(a, b, (((0,), (0,)), ((), ())), preferred_element_type=F32)


def _rms(x, g):
    return x * lax.rsqrt(jnp.mean(x * x, axis=-1, keepdims=True) + EPS) * g


def _split3(x):
    x1 = _bf(x)
    r1 = x - x1.astype(F32)
    x2 = _bf(r1)
    x3 = _bf(r1 - x2.astype(F32))
    return x1, x2, x3


def _exact_dot(sel, x):
    x1, x2, x3 = _split3(x)
    return _dot(sel, x1) + _dot(sel, x2) + _dot(sel, x3)


def _log_sigmoid(z):
    return jnp.minimum(z, 0.0) - jnp.log1p(jnp.exp(-jnp.abs(z)))


def _silu(z):
    return z * (1.0 / (1.0 + jnp.exp(-z)))


def _iota(shape, dim):
    return lax.broadcasted_iota(jnp.int32, shape, dim)


def _gla_k_head(lane):
    one = jnp.int32(1)
    zero = jnp.int32(0)
    return (jnp.where(lane >= GLA_DK, one, zero) + jnp.where(lane >= 2 * GLA_DK, one, zero)
            + jnp.where(lane >= 3 * GLA_DK, one, zero))


def _gla_v_head(lane):
    one = jnp.int32(1)
    zero = jnp.int32(0)
    return (jnp.where(lane >= GLA_DV, one, zero) + jnp.where(lane >= 2 * GLA_DV, one, zero)
            + jnp.where(lane >= 3 * GLA_DV, one, zero))


def _full_spec(a):
    nd = a.ndim
    return pl.BlockSpec(a.shape, lambda *_: (0,) * nd)


def _params(sem):
    return pltpu.CompilerParams(dimension_semantics=sem, vmem_limit_bytes=V7X_VMEM_LIMIT)


def _mem_softmax_pv(qm, kbd_ref, vbd_ref):
    s = _dot(_bf(qm), kbd_ref[...]) * (MEM_HD ** -0.5)
    out = None
    for h in range(MEM_H):
        sh = s[:, h * N_MEM:(h + 1) * N_MEM]
        mx = jnp.max(sh, axis=-1, keepdims=True)
        e = jnp.exp(sh - mx)
        p = e / jnp.sum(e, axis=-1, keepdims=True)
        t = _dot(_bf(p), vbd_ref[h * N_MEM:(h + 1) * N_MEM, :])
        out = t if out is None else out + t
    return out


def _memkv_kernel(mem_ref, g_ref, w_ref, k_ref, v_ref, kbd_ref, vbd_ref):
    h = _bf(_rms(mem_ref[0], g_ref[0]))
    kv = _dot(h, w_ref[0])
    k = kv[:, :MEM_W]
    v = kv[:, MEM_W:]
    kt = k.T
    k_ref[0, 0] = kt
    v_ref[0, 0] = v.T
    kt4 = jnp.concatenate([kt, kt, kt, kt], axis=1)
    keep_k = (_iota((MEM_W, MEM_H * N_MEM), 0) >> 6) == (_iota((MEM_W, MEM_H * N_MEM), 1) >> 8)
    kbd_ref[0, 0] = _bf(jnp.where(keep_k, kt4, 0.0))
    v4 = jnp.concatenate([v, v, v, v], axis=0)
    keep_v = (_iota((MEM_H * N_MEM, MEM_W), 0) >> 8) == (_iota((MEM_H * N_MEM, MEM_W), 1) >> 6)
    vbd_ref[0, 0] = _bf(jnp.where(keep_v, v4, 0.0))


def _memkv_call(mem, norm_mem, w_mem_kv):
    nb = mem.shape[0]
    nl = w_mem_kv.shape[0]
    g = norm_mem.reshape(nl, 1, D)
    w = _bf(w_mem_kv)
    return pl.pallas_call(
        _memkv_kernel,
        grid=(nl, nb),
        in_specs=[
            pl.BlockSpec((1, N_MEM, D), lambda l, b: (b, 0, 0)),
            pl.BlockSpec((1, 1, D), lambda l, b: (l, 0, 0)),
            pl.BlockSpec((1, D, 2 * MEM_W), lambda l, b: (l, 0, 0)),
        ],
        out_specs=[
            pl.BlockSpec((1, 1, MEM_W, N_MEM), lambda l, b: (l, b, 0, 0)),
            pl.BlockSpec((1, 1, MEM_W, N_MEM), lambda l, b: (l, b, 0, 0)),
            pl.BlockSpec((1, 1, MEM_W, MEM_H * N_MEM), lambda l, b: (l, b, 0, 0)),
            pl.BlockSpec((1, 1, MEM_H * N_MEM, MEM_W), lambda l, b: (l, b, 0, 0)),
        ],
        out_shape=[
            jax.ShapeDtypeStruct((nl, nb, MEM_W, N_MEM), F32),
            jax.ShapeDtypeStruct((nl, nb, MEM_W, N_MEM), F32),
            jax.ShapeDtypeStruct((nl, nb, MEM_W, MEM_H * N_MEM), BF16),
            jax.ShapeDtypeStruct((nl, nb, MEM_H * N_MEM, MEM_W), BF16),
        ],
        compiler_params=_params(("arbitrary", "arbitrary")),
        name="mem_kv",
    )(mem, g, w)


def _proj_a(x, gpre_ref, wqk_ref, wv_ref, wr_ref, wmisc_ref, wg_ref, bg_ref):
    h = _bf(_rms(x, gpre_ref[...]))
    qk = _dot(h, wqk_ref[...])
    q = qk[:, :GLA_KW] * (GLA_DK ** -0.5)
    k = qk[:, GLA_KW:]
    v = _dot(h, wv_ref[...])
    r = _dot(h, wr_ref[...])
    misc = _dot(h, wmisc_ref[...])
    glr = misc[:, :128]
    qm = misc[:, 128:]
    g = _log_sigmoid(_dot(_bf(glr), wg_ref[...]) + bg_ref[...]) * (1.0 / GATE_NORM)
    return q, k, g, v, r, qm


def _gla_out_gate(o, r, glan_ref):
    vh = _gla_v_head(_iota(o.shape, 1))
    o2 = o * o
    scale = jnp.zeros_like(o)
    for h in range(GLA_H):
        ss = jnp.sum(jnp.where(vh == h, o2, 0.0), axis=-1, keepdims=True) * (1.0 / GLA_DV)
        scale = jnp.where(vh == h, lax.rsqrt(ss + EPS), scale)
    return o * scale * glan_ref[...] * _silu(r)


def _mix_residual(x, o_main, o_mem, wo1_ref, wo2_ref, gpost_ref):
    mix = _dot(_bf(o_main), wo1_ref[...]) + _dot(_bf(o_mem), wo2_ref[...])
    return x + _rms(mix, gpost_ref[...])


def _mixer_a_kernel(x_ref, gpre_ref, wqk_ref, wv_ref, wr_ref, wmisc_ref, wg_ref, bg_ref, glan_ref,
                    kbd_ref, vbd_ref, wo1_ref, wo2_ref, gpost_ref,
                    xo_ref, st_ref,
                    q_s, k_s, g_s, v_s, o_s, state_s, p_s):
    C = GLA_CHUNK
    tm = x_ref.shape[1]
    t = pl.program_id(1)

    @pl.when(t == 0)
    def _():
        state_s[...] = jnp.zeros_like(state_s)

    x = x_ref[0]
    q, k, g, v, r, qm = _proj_a(x, gpre_ref, wqk_ref, wv_ref, wr_ref, wmisc_ref, wg_ref, bg_ref)
    q_s[...] = q
    k_s[...] = k
    g_s[...] = g
    v_s[...] = v

    ri = _iota((C, GLA_KW), 0)
    kh = _gla_k_head(_iota((C, GLA_KW), 1))
    khcat = jnp.concatenate([kh, kh, kh], axis=1)
    vh = _gla_v_head(_iota((C, GLA_VW), 1))
    tri = _bf(jnp.where(_iota((C, C), 0) >= _iota((C, C), 1), 1.0, 0.0))
    d_rs = _iota((C, 4 * C), 0) - (_iota((C, 4 * C), 1) & (C - 1))
    band = jnp.where((d_rs >= 0) & (d_rs <= (_iota((C, 4 * C), 0) & (GLA_SUB - 1))), d_rs, -1)
    ebc = _bf(jnp.where(_gla_k_head(_iota((GLA_KW, 4 * C), 0)) == (_iota((GLA_KW, 4 * C), 1) >> 6),
                        1.0, 0.0))
    blockmask = _gla_v_head(_iota((GLA_VW, GLA_KW), 0)) == _gla_k_head(_iota((GLA_VW, GLA_KW), 1))

    def chunk(c, carry):
        r0 = pl.multiple_of(c * C, C)
        qc = q_s[pl.ds(r0, C), :]
        kc = k_s[pl.ds(r0, C), :]
        gc = g_s[pl.ds(r0, C), :]
        vc = v_s[pl.ds(r0, C), :]
        b = _exact_dot(tri, gc)

        ref1 = jnp.broadcast_to(b[31:32, :], b.shape)
        ref2 = jnp.where(ri < 32, jnp.broadcast_to(b[15:16, :], b.shape),
                         jnp.broadcast_to(b[47:48, :], b.shape))
        q1 = qc * jnp.exp(jnp.minimum(b - ref1, 0.0))
        k1 = kc * jnp.exp(jnp.minimum(ref1 - b, 0.0))
        q2 = qc * jnp.exp(jnp.minimum(b - ref2, 0.0))
        k2 = kc * jnp.exp(jnp.minimum(ref2 - b, 0.0))
        qcat = jnp.concatenate([
            jnp.where(ri >= 32, q1, 0.0),
            jnp.where((ri >= 16) & (ri < 32), q2, 0.0),
            jnp.where(ri >= 48, q2, 0.0)], axis=1)
        kcat = jnp.concatenate([
            jnp.where(ri < 32, k1, 0.0),
            jnp.where(ri < 16, k2, 0.0),
            jnp.where((ri >= 32) & (ri < 48), k2, 0.0)], axis=1)
        kst = _bf(jnp.concatenate([jnp.where(khcat == h, kcat, 0.0) for h in range(GLA_H)], axis=0))
        a_off = _dot_nt(_bf(qcat), kst)

        for dlt in range(GLA_SUB):
            if dlt == 0:
                pr = qc * kc
            else:
                kd = pltpu.roll(kc, dlt, 0)
                bd = pltpu.roll(b, dlt, 0)
                pr = qc * kd * jnp.exp(jnp.minimum(b - bd, 0.0))
            p_s[dlt * C:(dlt + 1) * C, :] = _bf(pr)
        rsum = _dot(p_s[...], ebc)
        a_diag = jnp.zeros((C, 4 * C), F32)
        for dlt in range(GLA_SUB):
            a_diag = jnp.where(band == dlt, rsum[dlt * C:(dlt + 1) * C, :], a_diag)

        vst = _bf(jnp.concatenate([jnp.where(vh == h, vc, 0.0) for h in range(GLA_H)], axis=0))
        o_intra = _dot(_bf(a_off + a_diag), vst)
        st = state_s[...]
        o_inter = _dot_nt(_bf(qc * jnp.exp(b)), _bf(st))
        o_s[pl.ds(r0, C), :] = o_intra + o_inter

        blast = jnp.broadcast_to(b[C - 1:C, :], b.shape)
        kdec = kc * jnp.exp(blast - b)
        kvt = _dot_tn(_bf(vc), _bf(kdec))
        decay = jnp.exp(b[C - 1:C, :])
        state_s[...] = st * decay + jnp.where(blockmask, kvt, 0.0)
        return carry

    lax.fori_loop(0, tm // C, chunk, 0)
    st_ref[0] = state_s[...]

    o_main = _gla_out_gate(o_s[...], r, glan_ref)
    o_mem = _mem_softmax_pv(qm, kbd_ref.at[0], vbd_ref.at[0])
    xo_ref[0] = _mix_residual(x, o_main, o_mem, wo1_ref, wo2_ref, gpost_ref)


def _mixer_a_call(x, wa, kbd, vbd):
    nb, seq, _ = x.shape
    tm = TM_PROMPT
    weights = [wa["gpre"], wa["wqk"], wa["wv"], wa["wr"], wa["wmisc"], wa["wg"], wa["bg"], wa["glan"]]
    tail = [wa["wo1"], wa["wo2"], wa["gpost"]]
    return pl.pallas_call(
        _mixer_a_kernel,
        grid=(nb, seq // tm),
        in_specs=([pl.BlockSpec((1, tm, D), lambda b, t: (b, t, 0))]
                  + [_full_spec(w) for w in weights]
                  + [pl.BlockSpec((1, MEM_W, MEM_H * N_MEM), lambda b, t: (b, 0, 0)),
                     pl.BlockSpec((1, MEM_H * N_MEM, MEM_W), lambda b, t: (b, 0, 0))]
                  + [_full_spec(w) for w in tail]),
        out_specs=[pl.BlockSpec((1, tm, D), lambda b, t: (b, t, 0)),
                   pl.BlockSpec((1, GLA_VW, GLA_KW), lambda b, t: (b, 0, 0))],
        out_shape=[jax.ShapeDtypeStruct((nb, seq, D), F32),
                   jax.ShapeDtypeStruct((nb, GLA_VW, GLA_KW), F32)],
        scratch_shapes=[
            pltpu.VMEM((tm, GLA_KW), F32), pltpu.VMEM((tm, GLA_KW), F32), pltpu.VMEM((tm, GLA_KW), F32),
            pltpu.VMEM((tm, GLA_VW), F32), pltpu.VMEM((tm, GLA_VW), F32),
            pltpu.VMEM((GLA_VW, GLA_KW), F32),
            pltpu.VMEM((GLA_SUB * GLA_CHUNK, GLA_KW), BF16),
        ],
        compiler_params=_params(("arbitrary", "arbitrary")),
        name="mixer_a_prompt",
    )(x, *weights, kbd, vbd, *tail)


def _ffn_kernel(x_ref, gpre_ref, wup_ref, wdn_ref, gpost_ref, o_ref, acc_s):
    x = x_ref[...]
    h = _bf(_rms(x, gpre_ref[...]))
    for j in range(D_FF // FF_CHUNK):
        u = _dot(h, wup_ref[:, j * FF_CHUNK:(j + 1) * FF_CHUNK])
        u = jnp.maximum(u, 0.0)
        d = _dot(_bf(u * u), wdn_ref[j * FF_CHUNK:(j + 1) * FF_CHUNK, :])
        if j == 0:
            acc_s[...] = d
        else:
            acc_s[...] += d
    o_ref[...] = x + _rms(acc_s[...], gpost_ref[...])


def _ffn_call(x2d, wf, tm):
    n = x2d.shape[0]
    ws = [wf["gpre"], wf["wup"], wf["wdn"], wf["gpost"]]
    return pl.pallas_call(
        _ffn_kernel,
        grid=(n // tm,),
        in_specs=[pl.BlockSpec((tm, D), lambda i: (i, 0))] + [_full_spec(w) for w in ws],
        out_specs=pl.BlockSpec((tm, D), lambda i: (i, 0)),
        out_shape=jax.ShapeDtypeStruct((n, D), F32),
        scratch_shapes=[pltpu.VMEM((tm, D), F32)],
        compiler_params=_params(("arbitrary",)),
        name="ffn",
    )(x2d, *ws)


def _t5_bucket(dist):
    max_exact = N_BUCKETS // 2
    n = np.maximum(dist, 0)
    nf = np.maximum(n, 1).astype(np.float32)
    large = max_exact + (np.log(nf / np.float32(max_exact)) / np.float32(math.log(MAX_DISTANCE / max_exact))
                         * np.float32(N_BUCKETS - max_exact)).astype(np.int32)
    large = np.minimum(large, N_BUCKETS - 1)
    return np.where(n < max_exact, n, large)


def _swa_bias_tables():
    qi = np.arange(WINDOW)[:, None] + WINDOW
    kj = np.arange(2 * WINDOW)[None, :]
    dist = qi - kj
    valid = (dist >= 0) & (dist < WINDOW)
    return np.where(valid, _t5_bucket(dist), -1).astype(np.int32)


def _mixer_b_kernel(rb_ref, sink_ref,
                    x_ref, bkt_ref, gkv_ref, wkv_ref, gpre_ref, wq_ref, wqm_ref,
                    kbd_ref, vbd_ref, wo1_ref, wo2_ref, gpost_ref,
                    xo_ref, kc_ref, vc_ref,
                    kbuf, vbuf, q_s, o_s, bias_s):
    W = WINDOW
    tm = x_ref.shape[1]
    bb = pl.program_id(0)
    t = pl.program_id(1)
    nheads = SWA_G * SWA_KVH

    @pl.when((bb == 0) & (t == 0))
    def _():
        bkt = bkt_ref[...]
        own = _iota((W, 2 * W), 1) >= W
        for i in range(nheads):
            def add_bucket(n, acc):
                return jnp.where(bkt == n, rb_ref[n, i], acc)
            tab = lax.fori_loop(0, N_BUCKETS, add_bucket, jnp.zeros((W, 2 * W), F32))
            tab = jnp.where(bkt < 0, NEG_INF, tab)
            bias_s[0, i] = tab
            bias_s[1, i] = jnp.where(own, tab, NEG_INF)

    @pl.when(t == 0)
    def _():
        kbuf[0:W, :] = jnp.zeros((W, SWA_KVW), F32)
        vbuf[0:W, :] = jnp.zeros((W, SWA_KVW), F32)

    x = x_ref[0]
    xn = x * lax.rsqrt(jnp.mean(x * x, axis=-1, keepdims=True) + EPS)
    kv = _dot(_bf(xn * gkv_ref[...]), wkv_ref[...])
    kbuf[W:W + tm, :] = kv[:, :SWA_KVW]
    vbuf[W:W + tm, :] = kv[:, SWA_KVW:]
    kc_ref[0] = kv[tm - W:, :SWA_KVW]
    vc_ref[0] = kv[tm - W:, SWA_KVW:]

    h = _bf(xn * gpre_ref[...])
    q_s[...] = _dot(h, wq_ref[...]) * (SWA_HD ** -0.5)
    qm = _dot(h, wqm_ref[...])

    lane_head = _iota((W, SWA_KVW), 1) >> 6
    key_head = _iota((2 * W, SWA_KVW), 1) >> 6

    def block(j, carry):
        r0 = pl.multiple_of(j * W, W)
        kb = kbuf[pl.ds(r0, 2 * W), :]
        vb = vbuf[pl.ds(r0, 2 * W), :]
        tab = jnp.where((j == 0) & (t == 0), 1, 0)
        qrows = []
        for gi in range(SWA_G):
            qg = q_s[pl.ds(r0, W), gi * SWA_KVW:(gi + 1) * SWA_KVW]
            for hh in range(SWA_KVH):
                qrows.append(jnp.where(lane_head == hh, qg, 0.0))
        s_all = _dot_nt(_bf(jnp.concatenate(qrows, axis=0)), _bf(kb))
        vst = _bf(jnp.concatenate([jnp.where(key_head == hh, vb, 0.0) for hh in range(SWA_KVH)], axis=0))
        for gi in range(SWA_G):
            ps = []
            for hh in range(SWA_KVH):
                i = gi * SWA_KVH + hh
                s = s_all[i * W:(i + 1) * W, :] + bias_s[tab, i]
                sink = sink_ref[i]
                mx = jnp.maximum(jnp.max(s, axis=-1, keepdims=True), sink)
                p = jnp.exp(s - mx)
                p = p / (jnp.sum(p, axis=-1, keepdims=True) + jnp.exp(sink - mx))
                ps.append(_bf(p))
            o_s[pl.ds(r0, W), gi * SWA_KVW:(gi + 1) * SWA_KVW] = _dot(jnp.concatenate(ps, axis=1), vst)
        return carry

    lax.fori_loop(0, tm // W, block, 0, unroll=2)

    kbuf[0:W, :] = kbuf[tm:tm + W, :]
    vbuf[0:W, :] = vbuf[tm:tm + W, :]

    o_mem = _mem_softmax_pv(qm, kbd_ref.at[0], vbd_ref.at[0])
    xo_ref[0] = _mix_residual(x, o_s[...], o_mem, wo1_ref, wo2_ref, gpost_ref)


def _mixer_b_call(x, wb, kbd, vbd):
    nb, seq, _ = x.shape
    tm = TM_PROMPT
    bkt = jnp.asarray(_swa_bias_tables())
    head = [bkt, wb["gkv"], wb["wkv"], wb["gpre"], wb["wq"], wb["wqm"]]
    tail = [wb["wo1"], wb["wo2"], wb["gpost"]]
    smem = pl.BlockSpec(memory_space=pltpu.SMEM)
    return pl.pallas_call(
        _mixer_b_kernel,
        grid=(nb, seq // tm),
        in_specs=([smem, smem, pl.BlockSpec((1, tm, D), lambda b, t: (b, t, 0))]
                  + [_full_spec(w) for w in head]
                  + [pl.BlockSpec((1, MEM_W, MEM_H * N_MEM), lambda b, t: (b, 0, 0)),
                     pl.BlockSpec((1, MEM_H * N_MEM, MEM_W), lambda b, t: (b, 0, 0))]
                  + [_full_spec(w) for w in tail]),
        out_specs=[pl.BlockSpec((1, tm, D), lambda b, t: (b, t, 0)),
                   pl.BlockSpec((1, WINDOW, SWA_KVW), lambda b, t: (b, 0, 0)),
                   pl.BlockSpec((1, WINDOW, SWA_KVW), lambda b, t: (b, 0, 0))],
        out_shape=[jax.ShapeDtypeStruct((nb, seq, D), F32),
                   jax.ShapeDtypeStruct((nb, WINDOW, SWA_KVW), F32),
                   jax.ShapeDtypeStruct((nb, WINDOW, SWA_KVW), F32)],
        scratch_shapes=[
            pltpu.VMEM((tm + WINDOW, SWA_KVW), F32), pltpu.VMEM((tm + WINDOW, SWA_KVW), F32),
            pltpu.VMEM((tm, SWA_QW), F32), pltpu.VMEM((tm, SWA_QW), F32),
            pltpu.VMEM((2, SWA_G * SWA_KVH, WINDOW, 2 * WINDOW), F32),
        ],
        compiler_params=_params(("arbitrary", "arbitrary")),
        name="mixer_b_prompt",
    )(wb["rb"], wb["sinks"], x, *head, kbd, vbd, *tail)


def _row_to_col(row, eye):
    return jnp.sum(jnp.where(eye, jnp.broadcast_to(row, eye.shape), 0.0), axis=1, keepdims=True)


def _eye(n):
    return _iota((n, n), 0) == _iota((n, n), 1)


def _pre_a_kernel(x_ref, gpre_ref, wqk_ref, wv_ref, wr_ref, wmisc_ref, wg_ref, bg_ref,
                  q_ref, k_ref, g_ref, v_ref, r_ref, qm_ref):
    q, k, g, v, r, qm = _proj_a(x_ref[...], gpre_ref, wqk_ref, wv_ref, wr_ref, wmisc_ref, wg_ref, bg_ref)
    q_ref[...] = q
    k_ref[...] = k
    g_ref[...] = g
    v_ref[...] = v
    r_ref[...] = r
    qm_ref[...] = qm


def _pre_a_call(x2d, wa):
    n = x2d.shape[0]
    ws = [wa["gpre"], wa["wqk"], wa["wv"], wa["wr"], wa["wmisc"], wa["wg"], wa["bg"]]
    shapes = [(n, GLA_KW), (n, GLA_KW), (n, GLA_KW), (n, GLA_VW), (n, GLA_VW), (n, MEM_W)]
    return pl.pallas_call(
        _pre_a_kernel,
        grid=(1,),
        in_specs=[_full_spec(x2d)] + [_full_spec(w) for w in ws],
        out_specs=[pl.BlockSpec(s, lambda i: (0, 0)) for s in shapes],
        out_shape=[jax.ShapeDtypeStruct(s, F32) for s in shapes],
        compiler_params=_params(("arbitrary",)),
        name="pre_a_sample",
    )(x2d, *ws)


def _gla_step_kernel(qt_ref, kt_ref, gt_ref, vt_ref, s_ref, so_ref, ot_ref):
    c = pl.program_id(1)
    vt = vt_ref[...]
    acc = jnp.zeros_like(vt)
    for d in range(GLA_DK_BLOCK):
        s_new = jnp.exp(gt_ref[d:d + 1, :]) * s_ref[0, 0, d] + kt_ref[d:d + 1, :] * vt
        so_ref[0, 0, d] = s_new
        acc = acc + qt_ref[d:d + 1, :] * s_new

    @pl.when(c == 0)
    def _():
        ot_ref[...] = acc

    @pl.when(c > 0)
    def _():
        ot_ref[...] += acc


def _gla_step_call(qt, kt, gt, vt, state5):
    n = qt.shape[1]
    nblk = GLA_DK // GLA_DK_BLOCK
    rows = pl.BlockSpec((GLA_DK_BLOCK, n), lambda h, c: (h * nblk + c, 0))
    head = pl.BlockSpec((GLA_DV, n), lambda h, c: (h, 0))
    st = pl.BlockSpec((1, 1, GLA_DK_BLOCK, GLA_DV, n), lambda h, c: (0, h, c, 0, 0))
    return pl.pallas_call(
        _gla_step_kernel,
        grid=(GLA_H, nblk),
        in_specs=[rows, rows, rows, head, st],
        out_specs=[st, head],
        out_shape=[jax.ShapeDtypeStruct(state5.shape, F32), jax.ShapeDtypeStruct((GLA_VW, n), F32)],
        compiler_params=_params(("arbitrary", "arbitrary")),
        name="gla_step_sample",
    )(qt, kt, gt, vt, state5)


def _mem_step_kernel(qm_ref, mk_ref, mv_ref, o_ref):
    own = (_iota((8, MEM_W), 1) >> 6) == _iota((8, MEM_W), 0)
    for i in range(qm_ref.shape[0]):
        q8 = jnp.where(own, jnp.broadcast_to(qm_ref[i], (8, MEM_W)), 0.0)
        s = _dot(_bf(q8), _bf(mk_ref[0, i].reshape(MEM_W, N_MEM))) * (MEM_HD ** -0.5)
        e = jnp.exp(s - jnp.max(s, axis=1, keepdims=True))
        p = e / jnp.sum(e, axis=1, keepdims=True)
        res = _dot_nt(_bf(p), _bf(mv_ref[0, i].reshape(MEM_W, N_MEM)))
        o_ref[i] = jnp.sum(jnp.where(own, res, 0.0), axis=0, keepdims=True)


def _mem_step_call(qm, mk5, mv5, layer):
    n = qm.shape[0]
    rb = MEM_ROWS_PER_STEP
    blk = pl.BlockSpec((1, rb, MEM_H, MEM_HD, N_MEM), lambda i: (layer, i, 0, 0, 0))
    rows = pl.BlockSpec((rb, 1, MEM_W), lambda i: (i, 0, 0))
    return pl.pallas_call(
        _mem_step_kernel,
        grid=(n // rb,),
        in_specs=[rows, blk, blk],
        out_specs=rows,
        out_shape=jax.ShapeDtypeStruct((n, 1, MEM_W), F32),
        compiler_params=_params(("arbitrary",)),
        name="mem_step_sample",
    )(qm.reshape(n, 1, MEM_W), mk5, mv5).reshape(n, MEM_W)


def _post_a_kernel(x_ref, o_ref, r_ref, om_ref, glan_ref, wo1_ref, wo2_ref, gpost_ref, xo_ref):
    o_main = _gla_out_gate(o_ref[...], r_ref[...], glan_ref)
    xo_ref[...] = _mix_residual(x_ref[...], o_main, om_ref[...], wo1_ref, wo2_ref, gpost_ref)


def _post_a_call(x2d, o, r, om, wa):
    args = [x2d, o, r, om, wa["glan"], wa["wo1"], wa["wo2"], wa["gpost"]]
    return pl.pallas_call(
        _post_a_kernel,
        grid=(1,),
        in_specs=[_full_spec(a) for a in args],
        out_specs=_full_spec(x2d),
        out_shape=jax.ShapeDtypeStruct(x2d.shape, F32),
        compiler_params=_params(("arbitrary",)),
        name="post_a_sample",
    )(*args)


def _pre_b_kernel(x_ref, gkv_ref, wkv_ref, gpre_ref, wq_ref, wqm_ref, ks_ref, vs_ref, q_ref, qm_ref):
    x = x_ref[...]
    kv = _dot(_bf(_rms(x, gkv_ref[...])), wkv_ref[...])
    ks_ref[...] = kv[:, :SWA_KVW]
    vs_ref[...] = kv[:, SWA_KVW:]
    h = _bf(_rms(x, gpre_ref[...]))
    q_ref[...] = _dot(h, wq_ref[...])
    qm_ref[...] = _dot(h, wqm_ref[...])


def _pre_b_call(x2d, wb):
    n = x2d.shape[0]
    ws = [wb["gkv"], wb["wkv"], wb["gpre"], wb["wq"], wb["wqm"]]
    widths = [SWA_KVW, SWA_KVW, SWA_QW, MEM_W]
    return pl.pallas_call(
        _pre_b_kernel,
        grid=(1,),
        in_specs=[_full_spec(x2d)] + [_full_spec(w) for w in ws],
        out_specs=[pl.BlockSpec((n, w), lambda i: (0, 0)) for w in widths],
        out_shape=[jax.ShapeDtypeStruct((n, w), F32) for w in widths],
        compiler_params=_params(("arbitrary",)),
        name="pre_b_sample",
    )(x2d, *ws)


def _sample_buckets():
    dist = (WINDOW - 1) - np.arange(WINDOW)
    return _t5_bucket(dist).astype(np.int32).reshape(1, WINDOW)


def _swa_step_kernel(sink_ref, rb_ref, bkt_ref, kc_ref, vc_ref, kst_ref, vst_ref, q_ref,
                     kn_ref, vn_ref, o_ref, bias_s, sink_s, s_s, p_s):
    W = WINDOW
    R = SWA_KVH * 8

    @pl.when(pl.program_id(0) == 0)
    def _():
        bkt = bkt_ref[...]
        rid = _iota((R, W), 0)
        bias = jnp.zeros((R, W), F32)
        sink = jnp.zeros((R, W), F32)
        for h in range(SWA_KVH):
            for g in range(SWA_G):
                idx = g * SWA_KVH + h
                def add_bucket(n, acc):
                    return jnp.where(bkt == n, rb_ref[n, idx], acc)
                brow = lax.fori_loop(0, N_BUCKETS, add_bucket, jnp.zeros((1, W), F32))
                bias = jnp.where(rid == h * 8 + g, brow, bias)
                sink = jnp.where(rid == h * 8 + g, sink_ref[idx], sink)
        bias_s[...] = bias
        sink_s[...] = sink

    rb = q_ref.shape[0]
    base = pl.program_id(0) * rb
    last = _iota((SWA_KVW, W), 1) == W - 1
    own = (_iota((R, SWA_KVW), 1) >> 6) == (_iota((R, SWA_KVW), 0) >> 3)
    kst = kst_ref[...]
    vst = vst_ref[...]
    for i in range(rb):
        shift = W - 1 - (base + i)
        kn = jnp.where(last, pltpu.roll(kst, shift, 1), pltpu.roll(kc_ref[i].reshape(SWA_KVW, W), W - 1, 1))
        vn = jnp.where(last, pltpu.roll(vst, shift, 1), pltpu.roll(vc_ref[i].reshape(SWA_KVW, W), W - 1, 1))
        kn_ref[i] = kn.reshape(SWA_KVH, SWA_HD, W)
        vn_ref[i] = vn.reshape(SWA_KVH, SWA_HD, W)
        q32 = jnp.where(own, jnp.concatenate([q_ref[i]] * SWA_KVH, axis=0), 0.0)
        s_s[i * R:(i + 1) * R, :] = _dot(_bf(q32), _bf(kn))
    s = s_s[...] * (SWA_HD ** -0.5) + jnp.concatenate([bias_s[...]] * rb, axis=0)
    sink = jnp.concatenate([sink_s[...]] * rb, axis=0)
    mx = jnp.maximum(jnp.max(s, axis=1, keepdims=True), sink)
    p = jnp.exp(s - mx)
    p_s[...] = _bf(p / (jnp.sum(p, axis=1, keepdims=True) + jnp.exp(sink - mx)))
    for i in range(rb):
        vn = vn_ref[i].reshape(SWA_KVW, W)
        res = jnp.where(own, _dot_nt(p_s[i * R:(i + 1) * R, :], _bf(vn)), 0.0)
        o_ref[i] = res[0:8] + res[8:16] + res[16:24] + res[24:32]


def _swa_step_call(kc4, vc4, ks, vs, q, wb):
    n = q.shape[0]
    rb = SWA_ROWS_PER_STEP
    bkt = jnp.asarray(_sample_buckets())
    q8 = jnp.pad(q.reshape(n, SWA_G, SWA_KVW), ((0, 0), (0, 8 - SWA_G), (0, 0)))
    row3 = lambda r, w: pl.BlockSpec((rb, r, w), lambda i: (i, 0, 0))
    cache = pl.BlockSpec((rb, SWA_KVH, SWA_HD, WINDOW), lambda i: (i, 0, 0, 0))
    smem = pl.BlockSpec(memory_space=pltpu.SMEM)
    kn4, vn4, o8 = pl.pallas_call(
        _swa_step_kernel,
        grid=(n // rb,),
        in_specs=[smem, smem, _full_spec(bkt),
                  cache, cache, pl.BlockSpec((SWA_KVW, n), lambda i: (0, 0)),
                  pl.BlockSpec((SWA_KVW, n), lambda i: (0, 0)), row3(8, SWA_KVW)],
        out_specs=[cache, cache, row3(8, SWA_KVW)],
        out_shape=[jax.ShapeDtypeStruct(kc4.shape, F32), jax.ShapeDtypeStruct(vc4.shape, F32),
                   jax.ShapeDtypeStruct((n, 8, SWA_KVW), F32)],
        scratch_shapes=[pltpu.VMEM((SWA_KVH * 8, WINDOW), F32), pltpu.VMEM((SWA_KVH * 8, WINDOW), F32),
                        pltpu.VMEM((rb * SWA_KVH * 8, WINDOW), F32), pltpu.VMEM((rb * SWA_KVH * 8, WINDOW), BF16)],
        compiler_params=_params(("arbitrary",)),
        name="swa_step_sample",
    )(wb["sinks"], wb["rb"], bkt, kc4, vc4, ks.T, vs.T, q8)
    return kn4, vn4, o8[:, :SWA_G].reshape(n, SWA_QW)


def _post_b_kernel(x_ref, o_ref, om_ref, wo1_ref, wo2_ref, gpost_ref, xo_ref):
    xo_ref[...] = _mix_residual(x_ref[...], o_ref[...], om_ref[...], wo1_ref, wo2_ref, gpost_ref)


def _post_b_call(x2d, o, om, wb):
    args = [x2d, o, om, wb["wo1"], wb["wo2"], wb["gpost"]]
    return pl.pallas_call(
        _post_b_kernel,
        grid=(1,),
        in_specs=[_full_spec(a) for a in args],
        out_specs=_full_spec(x2d),
        out_shape=jax.ShapeDtypeStruct(x2d.shape, F32),
        compiler_params=_params(("arbitrary",)),
        name="post_b_sample",
    )(*args)


def _prep_weights(norm_mix_pre, norm_mix_post, norm_ffn_pre, norm_ffn_post, w_in_a, w_gate_up, b_gate,
                  gla_norm, w_in_b, sinks, norm_kv, w_kv, rel_bias, w_out, w_ffn_up, w_ffn_down):
    row = lambda g: g.reshape(1, -1)
    wa_in = w_in_a[0]
    c_v = 2 * GLA_KW
    c_r = c_v + GLA_VW
    c_g = c_r + GLA_VW
    c_m = c_g + GATE_RANK
    wmisc = jnp.concatenate([wa_in[:, c_g:c_m], jnp.zeros((D, 128 - GATE_RANK), F32), wa_in[:, c_m:]], axis=1)
    wg = jnp.zeros((128, GLA_KW), F32).at[:GATE_RANK].set(w_gate_up[0])
    wa = dict(
        gpre=row(norm_mix_pre[0]), wqk=_bf(wa_in[:, :c_v]), wv=_bf(wa_in[:, c_v:c_r]), wr=_bf(wa_in[:, c_r:c_g]),
        wmisc=_bf(wmisc), wg=_bf(wg), bg=row(b_gate[0]), glan=row(jnp.tile(gla_norm[0], GLA_H)),
        wo1=_bf(w_out[0][:GLA_VW]), wo2=_bf(w_out[0][GLA_VW:]), gpost=row(norm_mix_post[0]))
    wb_in = w_in_b[0]
    wq = wb_in[:, :SWA_QW].reshape(D, SWA_KVH, SWA_G, SWA_HD).transpose(0, 2, 1, 3).reshape(D, SWA_QW)
    wo1 = w_out[1][:SWA_QW].reshape(SWA_KVH, SWA_G, SWA_HD, D).transpose(1, 0, 2, 3).reshape(SWA_QW, D)
    rb = rel_bias.reshape(N_BUCKETS, SWA_KVH, SWA_G).transpose(0, 2, 1).reshape(N_BUCKETS, SWA_G * SWA_KVH)
    sk = sinks[0].reshape(SWA_KVH, SWA_G).T.reshape(SWA_G * SWA_KVH)
    wb = dict(
        gkv=row(norm_kv), wkv=_bf(w_kv), gpre=row(norm_mix_pre[1]), wq=_bf(wq), wqm=_bf(wb_in[:, SWA_QW:]),
        wo1=_bf(wo1), wo2=_bf(w_out[1][SWA_QW:]), gpost=row(norm_mix_post[1]), rb=rb, sinks=sk)
    ffn = [dict(gpre=row(norm_ffn_pre[l]), wup=_bf(w_ffn_up[l]), wdn=_bf(w_ffn_down[l]),
                gpost=row(norm_ffn_post[l])) for l in range(2)]
    return wa, wb, ffn


def kernel(x_prompt, x_sample, state_gla, cache_swa_k, cache_swa_v, cache_mem_k, cache_mem_v, mem_prompt,
           norm_mix_pre, norm_mix_post, norm_ffn_pre, norm_ffn_post, norm_mem, w_mem_kv, w_in_a, w_gate_up,
           b_gate, gla_norm, w_in_b, sinks, norm_kv, w_kv, rel_bias, w_out, w_ffn_up, w_ffn_down):
    wa, wb, ffn = _prep_weights(norm_mix_pre, norm_mix_post, norm_ffn_pre, norm_ffn_post, w_in_a, w_gate_up,
                                b_gate, gla_norm, w_in_b, sinks, norm_kv, w_kv, rel_bias, w_out, w_ffn_up,
                                w_ffn_down)
    nb, seq, _ = x_prompt.shape
    ns = x_sample.shape[0]

    mkt, mvt, kbd, vbd = _memkv_call(mem_prompt, norm_mem, w_mem_kv)
    x1, st = _mixer_a_call(x_prompt, wa, kbd[0], vbd[0])
    x2 = _ffn_call(x1.reshape(nb * seq, D), ffn[0], TM_PROMPT).reshape(nb, seq, D)
    x3, kc, vc = _mixer_b_call(x2, wb, kbd[1], vbd[1])
    y_prompt = _ffn_call(x3.reshape(nb * seq, D), ffn[1], TM_PROMPT).reshape(nb, seq, D)
    st4 = st.reshape(nb, GLA_H, GLA_DV, GLA_H, GLA_DK)
    state_prompt = jnp.stack([st4[:, h, :, h, :] for h in range(GLA_H)], axis=1).transpose(0, 1, 3, 2)[None]
    to_mem = lambda t: t.reshape(2, nb, MEM_H, MEM_HD, N_MEM).transpose(0, 1, 4, 2, 3)
    swa_shape = (nb, WINDOW, SWA_KVH, SWA_HD)

    xs = x_sample.reshape(ns, D)
    state5 = jnp.transpose(state_gla, (0, 2, 3, 4, 1))
    mk5 = jnp.transpose(cache_mem_k, (0, 1, 3, 4, 2))
    mv5 = jnp.transpose(cache_mem_v, (0, 1, 3, 4, 2))
    kc4 = jnp.transpose(cache_swa_k, (0, 2, 3, 1))
    vc4 = jnp.transpose(cache_swa_v, (0, 2, 3, 1))

    q, k, g, v, r, qm = _pre_a_call(xs, wa)
    state5_new, ot = _gla_step_call(q.T, k.T, g.T, v.T, state5)
    om = _mem_step_call(qm, mk5, mv5, 0)
    xs1 = _post_a_call(xs, ot.T, r, om, wa)
    xs2 = _ffn_call(xs1, ffn[0], ns)
    ks, vs, qb, qmb = _pre_b_call(xs2, wb)
    kn4, vn4, ob = _swa_step_call(kc4, vc4, ks, vs, qb, wb)
    omb = _mem_step_call(qmb, mk5, mv5, 1)
    xs3 = _post_b_call(xs2, ob, omb, wb)
    y_sample = _ffn_call(xs3, ffn[1], ns).reshape(ns, 1, D)

    return (y_prompt, y_sample, state_prompt,
            jnp.transpose(state5_new, (0, 4, 1, 2, 3)),
            kc.reshape(swa_shape), vc.reshape(swa_shape),
            jnp.transpose(kn4, (0, 3, 1, 2)), jnp.transpose(vn4, (0, 3, 1, 2)),
            to_mem(mkt), to_mem(mvt))
```

```python
import functools
import math

import numpy as np
import jax
import jax.numpy as jnp
from jax import lax
from jax.experimental import pallas as pl
from jax.experimental.pallas import tpu as pltpu

F32 = jnp.float32
BF16 = jnp.bfloat16

D = 1024
D_FF = 4 * D
N_MEM = 256
MEM_H = 4
MEM_HD = 64
MEM_W = MEM_H * MEM_HD
GLA_H = 4
GLA_DK = 96
GLA_DV = 192
GLA_KW = GLA_H * GLA_DK
GLA_VW = GLA_H * GLA_DV
GATE_RANK = 16
GATE_NORM = 16.0
SWA_HD = 64
SWA_KVH = 4
SWA_G = 3
SWA_QW = SWA_KVH * SWA_G * SWA_HD
SWA_KVW = SWA_KVH * SWA_HD
WINDOW = 128
N_BUCKETS = 32
MAX_DISTANCE = 128
EPS = 1e-6

GLA_CHUNK = 64
GLA_SUB = 16
TM_PROMPT = 512
FF_CHUNK = 512
GLA_DK_BLOCK = 32
MEM_ROWS_PER_STEP = 16
SWA_ROWS_PER_STEP = 16
V7X_VMEM_LIMIT = 56 * 1024 * 1024
NEG_INF = float("-inf")


def _bf(x):
    return x.astype(BF16)


def _dot(a, b):
    return jnp.dot(a, b, preferred_element_type=F32)


def _dot_nt(a, b):
    return lax.dot_general(a, b, (((1,), (1,)), ((), ())), preferred_element_type=F32)


def _dot_tn(a, b):
    return lax.dot_general(a, b, (((0,), (0,)), ((), ())), preferred_element_type=F32)


def _rms(x, g):
    return x * lax.rsqrt(jnp.mean(x * x, axis=-1, keepdims=True) + EPS) * g


def _split3(x):
    x1 = _bf(x)
    r1 = x - x1.astype(F32)
    x2 = _bf(r1)
    x3 = _bf(r1 - x2.astype(F32))
    return x1, x2, x3


def _exact_dot(sel, x):
    x1, x2, x3 = _split3(x)
    return _dot(sel, x1) + _dot(sel, x2) + _dot(sel, x3)


def _log_sigmoid(z):
    return jnp.minimum(z, 0.0) - jnp.log1p(jnp.exp(-jnp.abs(z)))


def _silu(z):
    return z * (1.0 / (1.0 + jnp.exp(-z)))


def _iota(shape, dim):
    return lax.broadcasted_iota(jnp.int32, shape, dim)


def _gla_k_head(lane):
    one = jnp.int32(1)
    zero = jnp.int32(0)
    return (jnp.where(lane >= GLA_DK, one, zero) + jnp.where(lane >= 2 * GLA_DK, one, zero)
            + jnp.where(lane >= 3 * GLA_DK, one, zero))


def _gla_v_head(lane):
    one = jnp.int32(1)
    zero = jnp.int32(0)
    return (jnp.where(lane >= GLA_DV, one, zero) + jnp.where(lane >= 2 * GLA_DV, one, zero)
            + jnp.where(lane >= 3 * GLA_DV, one, zero))


def _full_spec(a):
    nd = a.ndim
    return pl.BlockSpec(a.shape, lambda *_: (0,) * nd)


def _params(sem):
    return pltpu.CompilerParams(dimension_semantics=sem, vmem_limit_bytes=V7X_VMEM_LIMIT)


def _mem_softmax_pv(qm, kbd_ref, vbd_ref):
    s = _dot(_bf(qm), kbd_ref[...]) * (MEM_HD ** -0.5)
    out = None
    for h in range(MEM_H):
        sh = s[:, h * N_MEM:(h + 1) * N_MEM]
        mx = jnp.max(sh, axis=-1, keepdims=True)
        e = jnp.exp(sh - mx)
        p = e / jnp.sum(e, axis=-1, keepdims=True)
        t = _dot(_bf(p), vbd_ref[h * N_MEM:(h + 1) * N_MEM, :])
        out = t if out is None else out + t
    return out


def _memkv_kernel(mem_ref, g_ref, w_ref, k_ref, v_ref, kbd_ref, vbd_ref):
    h = _bf(_rms(mem_ref[0], g_ref[0]))
    kv = _dot(h, w_ref[0])
    k = kv[:, :MEM_W]
    v = kv[:, MEM_W:]
    kt = k.T
    k_ref[0, 0] = kt
    v_ref[0, 0] = v.T
    kt4 = jnp.concatenate([kt, kt, kt, kt], axis=1)
    keep_k = (_iota((MEM_W, MEM_H * N_MEM), 0) >> 6) == (_iota((MEM_W, MEM_H * N_MEM), 1) >> 8)
    kbd_ref[0, 0] = _bf(jnp.where(keep_k, kt4, 0.0))
    v4 = jnp.concatenate([v, v, v, v], axis=0)
    keep_v = (_iota((MEM_H * N_MEM, MEM_W), 0) >> 8) == (_iota((MEM_H * N_MEM, MEM_W), 1) >> 6)
    vbd_ref[0, 0] = _bf(jnp.where(keep_v, v4, 0.0))


def _memkv_call(mem, norm_mem, w_mem_kv):
    nb = mem.shape[0]
    nl = w_mem_kv.shape[0]
    g = norm_mem.reshape(nl, 1, D)
    w = _bf(w_mem_kv)
    return pl.pallas_call(
        _memkv_kernel,
        grid=(nl, nb),
        in_specs=[
            pl.BlockSpec((1, N_MEM, D), lambda l, b: (b, 0, 0)),
            pl.BlockSpec((1, 1, D), lambda l, b: (l, 0, 0)),
            pl.BlockSpec((1, D, 2 * MEM_W), lambda l, b: (l, 0, 0)),
        ],
        out_specs=[
            pl.BlockSpec((1, 1, MEM_W, N_MEM), lambda l, b: (l, b, 0, 0)),
            pl.BlockSpec((1, 1, MEM_W, N_MEM), lambda l, b: (l, b, 0, 0)),
            pl.BlockSpec((1, 1, MEM_W, MEM_H * N_MEM), lambda l, b: (l, b, 0, 0)),
            pl.BlockSpec((1, 1, MEM_H * N_MEM, MEM_W), lambda l, b: (l, b, 0, 0)),
        ],
        out_shape=[
            jax.ShapeDtypeStruct((nl, nb, MEM_W, N_MEM), F32),
            jax.ShapeDtypeStruct((nl, nb, MEM_W, N_MEM), F32),
            jax.ShapeDtypeStruct((nl, nb, MEM_W, MEM_H * N_MEM), BF16),
            jax.ShapeDtypeStruct((nl, nb, MEM_H * N_MEM, MEM_W), BF16),
        ],
        compiler_params=_params(("arbitrary", "arbitrary")),
        name="mem_kv",
    )(mem, g, w)


def _proj_a(x, gpre_ref, wqk_ref, wv_ref, wr_ref, wmisc_ref, wg_ref, bg_ref):
    h = _bf(_rms(x, gpre_ref[...]))
    qk = _dot(h, wqk_ref[...])
    q = qk[:, :GLA_KW] * (GLA_DK ** -0.5)
    k = qk[:, GLA_KW:]
    v = _dot(h, wv_ref[...])
    r = _dot(h, wr_ref[...])
    misc = _dot(h, wmisc_ref[...])
    glr = misc[:, :128]
    qm = misc[:, 128:]
    g = _log_sigmoid(_dot(_bf(glr), wg_ref[...]) + bg_ref[...]) * (1.0 / GATE_NORM)
    return q, k, g, v, r, qm


def _gla_out_gate(o, r, glan_ref):
    vh = _gla_v_head(_iota(o.shape, 1))
    o2 = o * o
    scale = jnp.zeros_like(o)
    for h in range(GLA_H):
        ss = jnp.sum(jnp.where(vh == h, o2, 0.0), axis=-1, keepdims=True) * (1.0 / GLA_DV)
        scale = jnp.where(vh == h, lax.rsqrt(ss + EPS), scale)
    return o * scale * glan_ref[...] * _silu(r)


def _mix_residual(x, o_main, o_mem, wo1_ref, wo2_ref, gpost_ref):
    mix = _dot(_bf(o_main), wo1_ref[...]) + _dot(_bf(o_mem), wo2_ref[...])
    return x + _rms(mix, gpost_ref[...])


def _make_gla_chunk(q_s, k_s, g_s, v_s, o_s, state_s, p_s, filler=None):
    C = GLA_CHUNK
    fill = filler if filler is not None else (lambda c, i: None)

    ri = _iota((C, GLA_KW), 0)
    kh = _gla_k_head(_iota((C, GLA_KW), 1))
    khcat = jnp.concatenate([kh, kh, kh], axis=1)
    vh = _gla_v_head(_iota((C, GLA_VW), 1))
    tri = _bf(jnp.where(_iota((C, C), 0) >= _iota((C, C), 1), 1.0, 0.0))
    d_rs = _iota((C, 4 * C), 0) - (_iota((C, 4 * C), 1) & (C - 1))
    band = jnp.where((d_rs >= 0) & (d_rs <= (_iota((C, 4 * C), 0) & (GLA_SUB - 1))), d_rs, -1)
    ebc = _bf(jnp.where(_gla_k_head(_iota((GLA_KW, 4 * C), 0)) == (_iota((GLA_KW, 4 * C), 1) >> 6),
                        1.0, 0.0))
    blockmask = _gla_v_head(_iota((GLA_VW, GLA_KW), 0)) == _gla_k_head(_iota((GLA_VW, GLA_KW), 1))

    def chunk(c, carry):
        r0 = pl.multiple_of(c * C, C)
        qc = q_s[pl.ds(r0, C), :]
        kc = k_s[pl.ds(r0, C), :]
        gc = g_s[pl.ds(r0, C), :]
        vc = v_s[pl.ds(r0, C), :]
        b = _exact_dot(tri, gc)
        fill(c, 0)

        ref1 = jnp.broadcast_to(b[31:32, :], b.shape)
        ref2 = jnp.where(ri < 32, jnp.broadcast_to(b[15:16, :], b.shape),
                         jnp.broadcast_to(b[47:48, :], b.shape))
        q1 = qc * jnp.exp(jnp.minimum(b - ref1, 0.0))
        k1 = kc * jnp.exp(jnp.minimum(ref1 - b, 0.0))
        q2 = qc * jnp.exp(jnp.minimum(b - ref2, 0.0))
        k2 = kc * jnp.exp(jnp.minimum(ref2 - b, 0.0))
        qcat = jnp.concatenate([
            jnp.where(ri >= 32, q1, 0.0),
            jnp.where((ri >= 16) & (ri < 32), q2, 0.0),
            jnp.where(ri >= 48, q2, 0.0)], axis=1)
        kcat = jnp.concatenate([
            jnp.where(ri < 32, k1, 0.0),
            jnp.where(ri < 16, k2, 0.0),
            jnp.where((ri >= 32) & (ri < 48), k2, 0.0)], axis=1)
        kst = _bf(jnp.concatenate([jnp.where(khcat == h, kcat, 0.0) for h in range(GLA_H)], axis=0))
        a_off = _dot_nt(_bf(qcat), kst)
        fill(c, 1)

        for dlt in range(GLA_SUB):
            if dlt == 0:
                pr = qc * kc
            else:
                kd = pltpu.roll(kc, dlt, 0)
                bd = pltpu.roll(b, dlt, 0)
                pr = qc * kd * jnp.exp(jnp.minimum(b - bd, 0.0))
            p_s[dlt * C:(dlt + 1) * C, :] = _bf(pr)
        rsum = _dot(p_s[...], ebc)
        fill(c, 2)
        a_diag = jnp.zeros((C, 4 * C), F32)
        for dlt in range(GLA_SUB):
            a_diag = jnp.where(band == dlt, rsum[dlt * C:(dlt + 1) * C, :], a_diag)

        vst = _bf(jnp.concatenate([jnp.where(vh == h, vc, 0.0) for h in range(GLA_H)], axis=0))
        o_intra = _dot(_bf(a_off + a_diag), vst)
        st = state_s[...]
        o_inter = _dot_nt(_bf(qc * jnp.exp(b)), _bf(st))
        o_s[pl.ds(r0, C), :] = o_intra + o_inter

        blast = jnp.broadcast_to(b[C - 1:C, :], b.shape)
        kdec = kc * jnp.exp(blast - b)
        kvt = _dot_tn(_bf(vc), _bf(kdec))
        fill(c, 3)
        decay = jnp.exp(b[C - 1:C, :])
        state_s[...] = st * decay + jnp.where(blockmask, kvt, 0.0)
        return carry

    return chunk


def _mlp_chunk(c, hf_s, wup_ref, wdn_ref, acc_s, rows=None):
    rs = slice(None) if rows is None else rows
    u = jnp.maximum(_dot(hf_s[rs, :], wup_ref[c]), 0.0)
    acc_s[rs, :] += _dot(_bf(u * u), wdn_ref[c])


def _layer_a_kernel(x_ref, gpre_ref, wqk_ref, wv_ref, wr_ref, wmisc_ref, wg_ref, bg_ref, glan_ref,
                    kbd_ref, vbd_ref, wo1_ref, wo2_ref, gpost_ref,
                    fgpre_ref, wup_ref, wdn_ref, fgpost_ref,
                    y_ref, st_ref,
                    q_s, k_s, g_s, v_s, o_s, state_s, p_s, r_s, qm_s, hf_s, x1_s, acc_s, u_s):
    tm = x_ref.shape[1]
    nchunk = tm // GLA_CHUNK
    t = pl.program_id(1)
    last = pl.num_programs(1) - 1

    @pl.when(t == 0)
    def _():
        state_s[...] = jnp.zeros_like(state_s)
        hf_s[...] = jnp.zeros_like(hf_s)
        x1_s[...] = jnp.zeros_like(x1_s)

    acc_s[...] = jnp.zeros_like(acc_s)

    @pl.when(t < last)
    def _():
        q, k, g, v, r, qm = _proj_a(x_ref[0], gpre_ref, wqk_ref, wv_ref, wr_ref, wmisc_ref, wg_ref, bg_ref)
        q_s[...] = q
        k_s[...] = k
        g_s[...] = g
        v_s[...] = v
        r_s[...] = r
        qm_s[...] = qm
        half = tm // 2

        def mlp_piece(c, i):
            rows = slice((i // 2) * half, (i // 2 + 1) * half)
            if i % 2 == 0:
                u = jnp.maximum(_dot(hf_s[rows, :], wup_ref[c]), 0.0)
                u_s[rows, :] = _bf(u * u)
            else:
                acc_s[rows, :] += _dot(u_s[rows, :], wdn_ref[c])

        lax.fori_loop(0, nchunk, _make_gla_chunk(q_s, k_s, g_s, v_s, o_s, state_s, p_s, mlp_piece), 0)
        st_ref[0] = state_s[...]

    @pl.when(t == last)
    def _():
        def body(c, carry):
            _mlp_chunk(c, hf_s, wup_ref, wdn_ref, acc_s)
            return carry

        lax.fori_loop(0, nchunk, body, 0)

    y_ref[0] = x1_s[...] + _rms(acc_s[...], fgpost_ref[...])

    @pl.when(t < last)
    def _():
        o_main = _gla_out_gate(o_s[...], r_s[...], glan_ref)
        o_mem = _mem_softmax_pv(qm_s[...], kbd_ref.at[0], vbd_ref.at[0])
        x1 = _mix_residual(x_ref[0], o_main, o_mem, wo1_ref, wo2_ref, gpost_ref)
        x1_s[...] = x1
        hf_s[...] = _bf(_rms(x1, fgpre_ref[...]))


def _const_spec(a):
    nd = a.ndim
    return pl.BlockSpec(a.shape, lambda *_: (0,) * nd, pipeline_mode=pl.Buffered(1))


def _layer_a_call(x, wa, wf, kbd, vbd):
    nb, seq, _ = x.shape
    tm = TM_PROMPT
    nt = seq // tm
    assert tm // GLA_CHUNK == D_FF // FF_CHUNK
    weights = [wa["gpre"], wa["wqk"], wa["wv"], wa["wr"], wa["wmisc"], wa["wg"], wa["bg"], wa["glan"]]
    tail = [wa["wo1"], wa["wo2"], wa["gpost"], wf["gpre"], wf["wup3"], wf["wdn3"], wf["gpost"]]
    return pl.pallas_call(
        _layer_a_kernel,
        grid=(nb, nt + 1),
        in_specs=([pl.BlockSpec((1, tm, D), lambda b, t: (b, jnp.minimum(t, nt - 1), 0))]
                  + [_const_spec(w) for w in weights]
                  + [pl.BlockSpec((1, MEM_W, MEM_H * N_MEM), lambda b, t: (b, 0, 0)),
                     pl.BlockSpec((1, MEM_H * N_MEM, MEM_W), lambda b, t: (b, 0, 0))]
                  + [_const_spec(w) for w in tail]),
        out_specs=[pl.BlockSpec((1, tm, D), lambda b, t: (b, jnp.maximum(t - 1, 0), 0)),
                   pl.BlockSpec((1, GLA_VW, GLA_KW), lambda b, t: (b, 0, 0))],
        out_shape=[jax.ShapeDtypeStruct((nb, seq, D), F32),
                   jax.ShapeDtypeStruct((nb, GLA_VW, GLA_KW), F32)],
        scratch_shapes=[
            pltpu.VMEM((tm, GLA_KW), F32), pltpu.VMEM((tm, GLA_KW), F32), pltpu.VMEM((tm, GLA_KW), F32),
            pltpu.VMEM((tm, GLA_VW), F32), pltpu.VMEM((tm, GLA_VW), F32),
            pltpu.VMEM((GLA_VW, GLA_KW), F32),
            pltpu.VMEM((GLA_SUB * GLA_CHUNK, GLA_KW), BF16),
            pltpu.VMEM((tm, GLA_VW), F32), pltpu.VMEM((tm, MEM_W), F32),
            pltpu.VMEM((tm, D), BF16), pltpu.VMEM((tm, D), F32), pltpu.VMEM((tm, D), F32),
            pltpu.VMEM((tm, FF_CHUNK), BF16),
        ],
        compiler_params=_params(("arbitrary", "arbitrary")),
        name="layer_a_prompt",
    )(x, *weights, kbd, vbd, *tail)


def _ffn_kernel(x_ref, gpre_ref, wup_ref, wdn_ref, gpost_ref, o_ref, acc_s):
    x = x_ref[...]
    h = _bf(_rms(x, gpre_ref[...]))
    for j in range(D_FF // FF_CHUNK):
        u = _dot(h, wup_ref[:, j * FF_CHUNK:(j + 1) * FF_CHUNK])
        u = jnp.maximum(u, 0.0)
        d = _dot(_bf(u * u), wdn_ref[j * FF_CHUNK:(j + 1) * FF_CHUNK, :])
        if j == 0:
            acc_s[...] = d
        else:
            acc_s[...] += d
    o_ref[...] = x + _rms(acc_s[...], gpost_ref[...])


def _ffn_call(x2d, wf, tm):
    n = x2d.shape[0]
    ws = [wf["gpre"], wf["wup"], wf["wdn"], wf["gpost"]]
    return pl.pallas_call(
        _ffn_kernel,
        grid=(n // tm,),
        in_specs=[pl.BlockSpec((tm, D), lambda i: (i, 0))] + [_full_spec(w) for w in ws],
        out_specs=pl.BlockSpec((tm, D), lambda i: (i, 0)),
        out_shape=jax.ShapeDtypeStruct((n, D), F32),
        scratch_shapes=[pltpu.VMEM((tm, D), F32)],
        compiler_params=_params(("arbitrary",)),
        name="ffn",
    )(x2d, *ws)


def _t5_bucket(dist):
    max_exact = N_BUCKETS // 2
    n = np.maximum(dist, 0)
    nf = np.maximum(n, 1).astype(np.float32)
    large = max_exact + (np.log(nf / np.float32(max_exact)) / np.float32(math.log(MAX_DISTANCE / max_exact))
                         * np.float32(N_BUCKETS - max_exact)).astype(np.int32)
    large = np.minimum(large, N_BUCKETS - 1)
    return np.where(n < max_exact, n, large)


def _swa_bias_tables():
    qi = np.arange(WINDOW)[:, None] + WINDOW
    kj = np.arange(2 * WINDOW)[None, :]
    dist = qi - kj
    valid = (dist >= 0) & (dist < WINDOW)
    return np.where(valid, _t5_bucket(dist), -1).astype(np.int32)


def _mixer_b_kernel(rb_ref, sink_ref,
                    x_ref, bkt_ref, gkv_ref, wkv_ref, gpre_ref, wq_ref, wqm_ref,
                    kbd_ref, vbd_ref, wo1_ref, wo2_ref, gpost_ref,
                    xo_ref, kc_ref, vc_ref,
                    kbuf, vbuf, q_s, o_s, bias_s):
    W = WINDOW
    tm = x_ref.shape[1]
    bb = pl.program_id(0)
    t = pl.program_id(1)
    nheads = SWA_G * SWA_KVH

    @pl.when((bb == 0) & (t == 0))
    def _():
        bkt = bkt_ref[...]
        own = _iota((W, 2 * W), 1) >= W
        for i in range(nheads):
            def add_bucket(n, acc):
                return jnp.where(bkt == n, rb_ref[n, i], acc)
            tab = lax.fori_loop(0, N_BUCKETS, add_bucket, jnp.zeros((W, 2 * W), F32))
            tab = jnp.where(bkt < 0, NEG_INF, tab)
            bias_s[0, i] = tab
            bias_s[1, i] = jnp.where(own, tab, NEG_INF)

    @pl.when(t == 0)
    def _():
        kbuf[0:W, :] = jnp.zeros((W, SWA_KVW), F32)
        vbuf[0:W, :] = jnp.zeros((W, SWA_KVW), F32)

    x = x_ref[0]
    xn = x * lax.rsqrt(jnp.mean(x * x, axis=-1, keepdims=True) + EPS)
    kv = _dot(_bf(xn * gkv_ref[...]), wkv_ref[...])
    kbuf[W:W + tm, :] = kv[:, :SWA_KVW]
    vbuf[W:W + tm, :] = kv[:, SWA_KVW:]
    kc_ref[0] = kv[tm - W:, :SWA_KVW]
    vc_ref[0] = kv[tm - W:, SWA_KVW:]

    h = _bf(xn * gpre_ref[...])
    q_s[...] = _dot(h, wq_ref[...]) * (SWA_HD ** -0.5)
    qm = _dot(h, wqm_ref[...])

    lane_head = _iota((W, SWA_KVW), 1) >> 6
    key_head = _iota((2 * W, SWA_KVW), 1) >> 6

    def block(j, carry):
        r0 = pl.multiple_of(j * W, W)
        kb = kbuf[pl.ds(r0, 2 * W), :]
        vb = vbuf[pl.ds(r0, 2 * W), :]
        tab = jnp.where((j == 0) & (t == 0), 1, 0)
        qrows = []
        for gi in range(SWA_G):
            qg = q_s[pl.ds(r0, W), gi * SWA_KVW:(gi + 1) * SWA_KVW]
            for hh in range(SWA_KVH):
                qrows.append(jnp.where(lane_head == hh, qg, 0.0))
        s_all = _dot_nt(_bf(jnp.concatenate(qrows, axis=0)), _bf(kb))
        vst = _bf(jnp.concatenate([jnp.where(key_head == hh, vb, 0.0) for hh in range(SWA_KVH)], axis=0))
        for gi in range(SWA_G):
            ps = []
            for hh in range(SWA_KVH):
                i = gi * SWA_KVH + hh
                s = s_all[i * W:(i + 1) * W, :] + bias_s[tab, i]
                sink = sink_ref[i]
                mx = jnp.maximum(jnp.max(s, axis=-1, keepdims=True), sink)
                p = jnp.exp(s - mx)
                p = p / (jnp.sum(p, axis=-1, keepdims=True) + jnp.exp(sink - mx))
                ps.append(_bf(p))
            o_s[pl.ds(r0, W), gi * SWA_KVW:(gi + 1) * SWA_KVW] = _dot(jnp.concatenate(ps, axis=1), vst)
        return carry

    lax.fori_loop(0, tm // W, block, 0, unroll=2)

    kbuf[0:W, :] = kbuf[tm:tm + W, :]
    vbuf[0:W, :] = vbuf[tm:tm + W, :]

    o_mem = _mem_softmax_pv(qm, kbd_ref.at[0], vbd_ref.at[0])
    xo_ref[0] = _mix_residual(x, o_s[...], o_mem, wo1_ref, wo2_ref, gpost_ref)


def _mixer_b_call(x, wb, kbd, vbd):
    nb, seq, _ = x.shape
    tm = TM_PROMPT
    bkt = jnp.asarray(_swa_bias_tables())
    head = [bkt, wb["gkv"], wb["wkv"], wb["gpre"], wb["wq"], wb["wqm"]]
    tail = [wb["wo1"], wb["wo2"], wb["gpost"]]
    smem = pl.BlockSpec(memory_space=pltpu.SMEM)
    return pl.pallas_call(
        _mixer_b_kernel,
        grid=(nb, seq // tm),
        in_specs=([smem, smem, pl.BlockSpec((1, tm, D), lambda b, t: (b, t, 0))]
                  + [_full_spec(w) for w in head]
                  + [pl.BlockSpec((1, MEM_W, MEM_H * N_MEM), lambda b, t: (b, 0, 0)),
                     pl.BlockSpec((1, MEM_H * N_MEM, MEM_W), lambda b, t: (b, 0, 0))]
                  + [_full_spec(w) for w in tail]),
        out_specs=[pl.BlockSpec((1, tm, D), lambda b, t: (b, t, 0)),
                   pl.BlockSpec((1, WINDOW, SWA_KVW), lambda b, t: (b, 0, 0)),
                   pl.BlockSpec((1, WINDOW, SWA_KVW), lambda b, t: (b, 0, 0))],
        out_shape=[jax.ShapeDtypeStruct((nb, seq, D), F32),
                   jax.ShapeDtypeStruct((nb, WINDOW, SWA_KVW), F32),
                   jax.ShapeDtypeStruct((nb, WINDOW, SWA_KVW), F32)],
        scratch_shapes=[
            pltpu.VMEM((tm + WINDOW, SWA_KVW), F32), pltpu.VMEM((tm + WINDOW, SWA_KVW), F32),
            pltpu.VMEM((tm, SWA_QW), F32), pltpu.VMEM((tm, SWA_QW), F32),
            pltpu.VMEM((2, SWA_G * SWA_KVH, WINDOW, 2 * WINDOW), F32),
        ],
        compiler_params=_params(("arbitrary", "arbitrary")),
        name="mixer_b_prompt",
    )(wb["rb"], wb["sinks"], x, *head, kbd, vbd, *tail)


def _row_to_col(row, eye):
    return jnp.sum(jnp.where(eye, jnp.broadcast_to(row, eye.shape), 0.0), axis=1, keepdims=True)


def _eye(n):
    return _iota((n, n), 0) == _iota((n, n), 1)


def _pre_a_kernel(x_ref, gpre_ref, wqk_ref, wv_ref, wr_ref, wmisc_ref, wg_ref, bg_ref,
                  q_ref, k_ref, g_ref, v_ref, r_ref, qm_ref):
    q, k, g, v, r, qm = _proj_a(x_ref[...], gpre_ref, wqk_ref, wv_ref, wr_ref, wmisc_ref, wg_ref, bg_ref)
    q_ref[...] = q
    k_ref[...] = k
    g_ref[...] = g
    v_ref[...] = v
    r_ref[...] = r
    qm_ref[...] = qm


def _pre_a_call(x2d, wa):
    n = x2d.shape[0]
    ws = [wa["gpre"], wa["wqk"], wa["wv"], wa["wr"], wa["wmisc"], wa["wg"], wa["bg"]]
    shapes = [(n, GLA_KW), (n, GLA_KW), (n, GLA_KW), (n, GLA_VW), (n, GLA_VW), (n, MEM_W)]
    return pl.pallas_call(
        _pre_a_kernel,
        grid=(1,),
        in_specs=[_full_spec(x2d)] + [_full_spec(w) for w in ws],
        out_specs=[pl.BlockSpec(s, lambda i: (0, 0)) for s in shapes],
        out_shape=[jax.ShapeDtypeStruct(s, F32) for s in shapes],
        compiler_params=_params(("arbitrary",)),
        name="pre_a_sample",
    )(x2d, *ws)


def _gla_step_kernel(qt_ref, kt_ref, gt_ref, vt_ref, s_ref, so_ref, ot_ref):
    c = pl.program_id(1)
    vt = vt_ref[...]
    acc = jnp.zeros_like(vt)
    for d in range(GLA_DK_BLOCK):
        s_new = jnp.exp(gt_ref[d:d + 1, :]) * s_ref[0, 0, d] + kt_ref[d:d + 1, :] * vt
        so_ref[0, 0, d] = s_new
        acc = acc + qt_ref[d:d + 1, :] * s_new

    @pl.when(c == 0)
    def _():
        ot_ref[...] = acc

    @pl.when(c > 0)
    def _():
        ot_ref[...] += acc


def _gla_step_call(qt, kt, gt, vt, state5):
    n = qt.shape[1]
    nblk = GLA_DK // GLA_DK_BLOCK
    rows = pl.BlockSpec((GLA_DK_BLOCK, n), lambda h, c: (h * nblk + c, 0))
    head = pl.BlockSpec((GLA_DV, n), lambda h, c: (h, 0))
    st = pl.BlockSpec((1, 1, GLA_DK_BLOCK, GLA_DV, n), lambda h, c: (0, h, c, 0, 0))
    return pl.pallas_call(
        _gla_step_kernel,
        grid=(GLA_H, nblk),
        in_specs=[rows, rows, rows, head, st],
        out_specs=[st, head],
        out_shape=[jax.ShapeDtypeStruct(state5.shape, F32), jax.ShapeDtypeStruct((GLA_VW, n), F32)],
        compiler_params=_params(("arbitrary", "arbitrary")),
        name="gla_step_sample",
    )(qt, kt, gt, vt, state5)


def _mem_step_kernel(qm_ref, mk_ref, mv_ref, o_ref):
    own = (_iota((8, MEM_W), 1) >> 6) == _iota((8, MEM_W), 0)
    for i in range(qm_ref.shape[0]):
        q8 = jnp.where(own, jnp.broadcast_to(qm_ref[i], (8, MEM_W)), 0.0)
        s = _dot(_bf(q8), _bf(mk_ref[0, i].reshape(MEM_W, N_MEM))) * (MEM_HD ** -0.5)
        e = jnp.exp(s - jnp.max(s, axis=1, keepdims=True))
        p = e / jnp.sum(e, axis=1, keepdims=True)
        res = _dot_nt(_bf(p), _bf(mv_ref[0, i].reshape(MEM_W, N_MEM)))
        o_ref[i] = jnp.sum(jnp.where(own, res, 0.0), axis=0, keepdims=True)


def _mem_step_call(qm, mk5, mv5, layer):
    n = qm.shape[0]
    rb = MEM_ROWS_PER_STEP
    blk = pl.BlockSpec((1, rb, MEM_H, MEM_HD, N_MEM), lambda i: (layer, i, 0, 0, 0))
    rows = pl.BlockSpec((rb, 1, MEM_W), lambda i: (i, 0, 0))
    return pl.pallas_call(
        _mem_step_kernel,
        grid=(n // rb,),
        in_specs=[rows, blk, blk],
        out_specs=rows,
        out_shape=jax.ShapeDtypeStruct((n, 1, MEM_W), F32),
        compiler_params=_params(("arbitrary",)),
        name="mem_step_sample",
    )(qm.reshape(n, 1, MEM_W), mk5, mv5).reshape(n, MEM_W)


def _post_a_kernel(x_ref, o_ref, r_ref, om_ref, glan_ref, wo1_ref, wo2_ref, gpost_ref, xo_ref):
    o_main = _gla_out_gate(o_ref[...], r_ref[...], glan_ref)
    xo_ref[...] = _mix_residual(x_ref[...], o_main, om_ref[...], wo1_ref, wo2_ref, gpost_ref)


def _post_a_call(x2d, o, r, om, wa):
    args = [x2d, o, r, om, wa["glan"], wa["wo1"], wa["wo2"], wa["gpost"]]
    return pl.pallas_call(
        _post_a_kernel,
        grid=(1,),
        in_specs=[_full_spec(a) for a in args],
        out_specs=_full_spec(x2d),
        out_shape=jax.ShapeDtypeStruct(x2d.shape, F32),
        compiler_params=_params(("arbitrary",)),
        name="post_a_sample",
    )(*args)


def _pre_b_kernel(x_ref, gkv_ref, wkv_ref, gpre_ref, wq_ref, wqm_ref, ks_ref, vs_ref, q_ref, qm_ref):
    x = x_ref[...]
    kv = _dot(_bf(_rms(x, gkv_ref[...])), wkv_ref[...])
    ks_ref[...] = kv[:, :SWA_KVW]
    vs_ref[...] = kv[:, SWA_KVW:]
    h = _bf(_rms(x, gpre_ref[...]))
    q_ref[...] = _dot(h, wq_ref[...])
    qm_ref[...] = _dot(h, wqm_ref[...])


def _pre_b_call(x2d, wb):
    n = x2d.shape[0]
    ws = [wb["gkv"], wb["wkv"], wb["gpre"], wb["wq"], wb["wqm"]]
    widths = [SWA_KVW, SWA_KVW, SWA_QW, MEM_W]
    return pl.pallas_call(
        _pre_b_kernel,
        grid=(1,),
        in_specs=[_full_spec(x2d)] + [_full_spec(w) for w in ws],
        out_specs=[pl.BlockSpec((n, w), lambda i: (0, 0)) for w in widths],
        out_shape=[jax.ShapeDtypeStruct((n, w), F32) for w in widths],
        compiler_params=_params(("arbitrary",)),
        name="pre_b_sample",
    )(x2d, *ws)


def _sample_buckets():
    dist = (WINDOW - 1) - np.arange(WINDOW)
    return _t5_bucket(dist).astype(np.int32).reshape(1, WINDOW)


def _swa_step_kernel(sink_ref, rb_ref, bkt_ref, kc_ref, vc_ref, kst_ref, vst_ref, q_ref,
                     kn_ref, vn_ref, o_ref, bias_s, sink_s, s_s, p_s):
    W = WINDOW
    R = SWA_KVH * 8

    @pl.when(pl.program_id(0) == 0)
    def _():
        bkt = bkt_ref[...]
        rid = _iota((R, W), 0)
        bias = jnp.zeros((R, W), F32)
        sink = jnp.zeros((R, W), F32)
        for h in range(SWA_KVH):
            for g in range(SWA_G):
                idx = g * SWA_KVH + h
                def add_bucket(n, acc):
                    return jnp.where(bkt == n, rb_ref[n, idx], acc)
                brow = lax.fori_loop(0, N_BUCKETS, add_bucket, jnp.zeros((1, W), F32))
                bias = jnp.where(rid == h * 8 + g, brow, bias)
                sink = jnp.where(rid == h * 8 + g, sink_ref[idx], sink)
        bias_s[...] = bias
        sink_s[...] = sink

    rb = q_ref.shape[0]
    base = pl.program_id(0) * rb
    last = _iota((SWA_KVW, W), 1) == W - 1
    own = (_iota((R, SWA_KVW), 1) >> 6) == (_iota((R, SWA_KVW), 0) >> 3)
    kst = kst_ref[...]
    vst = vst_ref[...]
    for i in range(rb):
        shift = W - 1 - (base + i)
        kn = jnp.where(last, pltpu.roll(kst, shift, 1), pltpu.roll(kc_ref[i].reshape(SWA_KVW, W), W - 1, 1))
        vn = jnp.where(last, pltpu.roll(vst, shift, 1), pltpu.roll(vc_ref[i].reshape(SWA_KVW, W), W - 1, 1))
        kn_ref[i] = kn.reshape(SWA_KVH, SWA_HD, W)
        vn_ref[i] = vn.reshape(SWA_KVH, SWA_HD, W)
        q32 = jnp.where(own, jnp.concatenate([q_ref[i]] * SWA_KVH, axis=0), 0.0)
        s_s[i * R:(i + 1) * R, :] = _dot(_bf(q32), _bf(kn))
    s = s_s[...] * (SWA_HD ** -0.5) + jnp.concatenate([bias_s[...]] * rb, axis=0)
    sink = jnp.concatenate([sink_s[...]] * rb, axis=0)
    mx = jnp.maximum(jnp.max(s, axis=1, keepdims=True), sink)
    p = jnp.exp(s - mx)
    p_s[...] = _bf(p / (jnp.sum(p, axis=1, keepdims=True) + jnp.exp(sink - mx)))
    for i in range(rb):
        vn = vn_ref[i].reshape(SWA_KVW, W)
        res = jnp.where(own, _dot_nt(p_s[i * R:(i + 1) * R, :], _bf(vn)), 0.0)
        o_ref[i] = res[0:8] + res[8:16] + res[16:24] + res[24:32]


def _swa_step_call(kc4, vc4, ks, vs, q, wb):
    n = q.shape[0]
    rb = SWA_ROWS_PER_STEP
    bkt = jnp.asarray(_sample_buckets())
    q8 = jnp.pad(q.reshape(n, SWA_G, SWA_KVW), ((0, 0), (0, 8 - SWA_G), (0, 0)))
    row3 = lambda r, w: pl.BlockSpec((rb, r, w), lambda i: (i, 0, 0))
    cache = pl.BlockSpec((rb, SWA_KVH, SWA_HD, WINDOW), lambda i: (i, 0, 0, 0))
    smem = pl.BlockSpec(memory_space=pltpu.SMEM)
    kn4, vn4, o8 = pl.pallas_call(
        _swa_step_kernel,
        grid=(n // rb,),
        in_specs=[smem, smem, _full_spec(bkt),
                  cache, cache, pl.BlockSpec((SWA_KVW, n), lambda i: (0, 0)),
                  pl.BlockSpec((SWA_KVW, n), lambda i: (0, 0)), row3(8, SWA_KVW)],
        out_specs=[cache, cache, row3(8, SWA_KVW)],
        out_shape=[jax.ShapeDtypeStruct(kc4.shape, F32), jax.ShapeDtypeStruct(vc4.shape, F32),
                   jax.ShapeDtypeStruct((n, 8, SWA_KVW), F32)],
        scratch_shapes=[pltpu.VMEM((SWA_KVH * 8, WINDOW), F32), pltpu.VMEM((SWA_KVH * 8, WINDOW), F32),
                        pltpu.VMEM((rb * SWA_KVH * 8, WINDOW), F32), pltpu.VMEM((rb * SWA_KVH * 8, WINDOW), BF16)],
        compiler_params=_params(("arbitrary",)),
        name="swa_step_sample",
    )(wb["sinks"], wb["rb"], bkt, kc4, vc4, ks.T, vs.T, q8)
    return kn4, vn4, o8[:, :SWA_G].reshape(n, SWA_QW)


def _post_b_kernel(x_ref, o_ref, om_ref, wo1_ref, wo2_ref, gpost_ref, xo_ref):
    xo_ref[...] = _mix_residual(x_ref[...], o_ref[...], om_ref[...], wo1_ref, wo2_ref, gpost_ref)


def _post_b_call(x2d, o, om, wb):
    args = [x2d, o, om, wb["wo1"], wb["wo2"], wb["gpost"]]
    return pl.pallas_call(
        _post_b_kernel,
        grid=(1,),
        in_specs=[_full_spec(a) for a in args],
        out_specs=_full_spec(x2d),
        out_shape=jax.ShapeDtypeStruct(x2d.shape, F32),
        compiler_params=_params(("arbitrary",)),
        name="post_b_sample",
    )(*args)


def _prep_weights(norm_mix_pre, norm_mix_post, norm_ffn_pre, norm_ffn_post, w_in_a, w_gate_up, b_gate,
                  gla_norm, w_in_b, sinks, norm_kv, w_kv, rel_bias, w_out, w_ffn_up, w_ffn_down):
    row = lambda g: g.reshape(1, -1)
    wa_in = w_in_a[0]
    c_v = 2 * GLA_KW
    c_r = c_v + GLA_VW
    c_g = c_r + GLA_VW
    c_m = c_g + GATE_RANK
    wmisc = jnp.concatenate([wa_in[:, c_g:c_m], jnp.zeros((D, 128 - GATE_RANK), F32), wa_in[:, c_m:]], axis=1)
    wg = jnp.zeros((128, GLA_KW), F32).at[:GATE_RANK].set(w_gate_up[0])
    wa = dict(
        gpre=row(norm_mix_pre[0]), wqk=_bf(wa_in[:, :c_v]), wv=_bf(wa_in[:, c_v:c_r]), wr=_bf(wa_in[:, c_r:c_g]),
        wmisc=_bf(wmisc), wg=_bf(wg), bg=row(b_gate[0]), glan=row(jnp.tile(gla_norm[0], GLA_H)),
        wo1=_bf(w_out[0][:GLA_VW]), wo2=_bf(w_out[0][GLA_VW:]), gpost=row(norm_mix_post[0]))
    wb_in = w_in_b[0]
    wq = wb_in[:, :SWA_QW].reshape(D, SWA_KVH, SWA_G, SWA_HD).transpose(0, 2, 1, 3).reshape(D, SWA_QW)
    wo1 = w_out[1][:SWA_QW].reshape(SWA_KVH, SWA_G, SWA_HD, D).transpose(1, 0, 2, 3).reshape(SWA_QW, D)
    rb = rel_bias.reshape(N_BUCKETS, SWA_KVH, SWA_G).transpose(0, 2, 1).reshape(N_BUCKETS, SWA_G * SWA_KVH)
    sk = sinks[0].reshape(SWA_KVH, SWA_G).T.reshape(SWA_G * SWA_KVH)
    wb = dict(
        gkv=row(norm_kv), wkv=_bf(w_kv), gpre=row(norm_mix_pre[1]), wq=_bf(wq), wqm=_bf(wb_in[:, SWA_QW:]),
        wo1=_bf(wo1), wo2=_bf(w_out[1][SWA_QW:]), gpost=row(norm_mix_post[1]), rb=rb, sinks=sk)
    nff = D_FF // FF_CHUNK
    ffn = [dict(gpre=row(norm_ffn_pre[l]), wup=_bf(w_ffn_up[l]), wdn=_bf(w_ffn_down[l]),
                wup3=_bf(w_ffn_up[l]).reshape(D, nff, FF_CHUNK).transpose(1, 0, 2),
                wdn3=_bf(w_ffn_down[l]).reshape(nff, FF_CHUNK, D),
                gpost=row(norm_ffn_post[l])) for l in range(2)]
    return wa, wb, ffn


def kernel(x_prompt, x_sample, state_gla, cache_swa_k, cache_swa_v, cache_mem_k, cache_mem_v, mem_prompt,
           norm_mix_pre, norm_mix_post, norm_ffn_pre, norm_ffn_post, norm_mem, w_mem_kv, w_in_a, w_gate_up,
           b_gate, gla_norm, w_in_b, sinks, norm_kv, w_kv, rel_bias, w_out, w_ffn_up, w_ffn_down):
    wa, wb, ffn = _prep_weights(norm_mix_pre, norm_mix_post, norm_ffn_pre, norm_ffn_post, w_in_a, w_gate_up,
                                b_gate, gla_norm, w_in_b, sinks, norm_kv, w_kv, rel_bias, w_out, w_ffn_up,
                                w_ffn_down)
    nb, seq, _ = x_prompt.shape
    ns = x_sample.shape[0]

    mkt, mvt, kbd, vbd = _memkv_call(mem_prompt, norm_mem, w_mem_kv)
    x2, st = _layer_a_call(x_prompt, wa, ffn[0], kbd[0], vbd[0])
    x3, kc, vc = _mixer_b_call(x2, wb, kbd[1], vbd[1])
    y_prompt = _ffn_call(x3.reshape(nb * seq, D), ffn[1], TM_PROMPT).reshape(nb, seq, D)
    st4 = st.reshape(nb, GLA_H, GLA_DV, GLA_H, GLA_DK)
    state_prompt = jnp.stack([st4[:, h, :, h, :] for h in range(GLA_H)], axis=1).transpose(0, 1, 3, 2)[None]
    to_mem = lambda t: t.reshape(2, nb, MEM_H, MEM_HD, N_MEM).transpose(0, 1, 4, 2, 3)
    swa_shape = (nb, WINDOW, SWA_KVH, SWA_HD)

    xs = x_sample.reshape(ns, D)
    state5 = jnp.transpose(state_gla, (0, 2, 3, 4, 1))
    mk5 = jnp.transpose(cache_mem_k, (0, 1, 3, 4, 2))
    mv5 = jnp.transpose(cache_mem_v, (0, 1, 3, 4, 2))
    kc4 = jnp.transpose(cache_swa_k, (0, 2, 3, 1))
    vc4 = jnp.transpose(cache_swa_v, (0, 2, 3, 1))

    q, k, g, v, r, qm = _pre_a_call(xs, wa)
    state5_new, ot = _gla_step_call(q.T, k.T, g.T, v.T, state5)
    om = _mem_step_call(qm, mk5, mv5, 0)
    xs1 = _post_a_call(xs, ot.T, r, om, wa)
    xs2 = _ffn_call(xs1, ffn[0], ns)
    ks, vs, qb, qmb = _pre_b_call(xs2, wb)
    kn4, vn4, ob = _swa_step_call(kc4, vc4, ks, vs, qb, wb)
    omb = _mem_step_call(qmb, mk5, mv5, 1)
    xs3 = _post_b_call(xs2, ob, omb, wb)
    y_sample = _ffn_call(xs3, ffn[1], ns).reshape(ns, 1, D)

    return (y_prompt, y_sample, state_prompt,
            jnp.transpose(state5_new, (0, 4, 1, 2, 3)),
            kc.reshape(swa_shape), vc.reshape(swa_shape),
            jnp.transpose(kn4, (0, 3, 1, 2)), jnp.transpose(vn4, (0, 3, 1, 2)),
            to_mem(mkt), to_mem(mvt))
```

```python
import functools
import math

import numpy as np
import jax
import jax.numpy as jnp
from jax import lax
from jax.experimental import pallas as pl
from jax.experimental.pallas import tpu as pltpu

F32 = jnp.float32
BF16 = jnp.bfloat16

D = 1024
D_FF = 4 * D
N_MEM = 256
MEM_H = 4
MEM_HD = 64
MEM_W = MEM_H * MEM_HD
GLA_H = 4
GLA_DK = 96
GLA_DV = 192
GLA_KW = GLA_H * GLA_DK
GLA_VW = GLA_H * GLA_DV
GATE_RANK = 16
GATE_NORM = 16.0
SWA_HD = 64
SWA_KVH = 4
SWA_G = 3
SWA_QW = SWA_KVH * SWA_G * SWA_HD
SWA_KVW = SWA_KVH * SWA_HD
WINDOW = 128
N_BUCKETS = 32
MAX_DISTANCE = 128
EPS = 1e-6

GLA_CHUNK = 64
GLA_SUB = 16
GLA_SAFE_DECAY = 60.0
TM_PROMPT = 512
FF_CHUNK = 512
GLA_DK_BLOCK = 32
MEM_ROWS_PER_STEP = 16
SWA_ROWS_PER_STEP = 16
V7X_VMEM_LIMIT = 56 * 1024 * 1024
NEG_INF = float("-inf")


def _bf(x):
    return x.astype(BF16)


def _dot(a, b):
    return jnp.dot(a, b, preferred_element_type=F32)


def _dot_nt(a, b):
    return lax.dot_general(a, b, (((1,), (1,)), ((), ())), preferred_element_type=F32)


def _dot_tn(a, b):
    return lax.dot_general(a, b, (((0,), (0,)), ((), ())), preferred_element_type=F32)


def _rms(x, g):
    return x * lax.rsqrt(jnp.mean(x * x, axis=-1, keepdims=True) + EPS) * g


def _split3(x):
    x1 = _bf(x)
    r1 = x - x1.astype(F32)
    x2 = _bf(r1)
    x3 = _bf(r1 - x2.astype(F32))
    return x1, x2, x3


def _exact_dot(sel, x):
    x1, x2, x3 = _split3(x)
    return _dot(sel, x1) + _dot(sel, x2) + _dot(sel, x3)


def _log_sigmoid(z):
    return jnp.minimum(z, 0.0) - jnp.log1p(jnp.exp(-jnp.abs(z)))


def _silu(z):
    return z * (1.0 / (1.0 + jnp.exp(-z)))


def _iota(shape, dim):
    return lax.broadcasted_iota(jnp.int32, shape, dim)


def _gla_k_head(lane):
    one = jnp.int32(1)
    zero = jnp.int32(0)
    return (jnp.where(lane >= GLA_DK, one, zero) + jnp.where(lane >= 2 * GLA_DK, one, zero)
            + jnp.where(lane >= 3 * GLA_DK, one, zero))


def _gla_v_head(lane):
    one = jnp.int32(1)
    zero = jnp.int32(0)
    return (jnp.where(lane >= GLA_DV, one, zero) + jnp.where(lane >= 2 * GLA_DV, one, zero)
            + jnp.where(lane >= 3 * GLA_DV, one, zero))


def _full_spec(a):
    nd = a.ndim
    return pl.BlockSpec(a.shape, lambda *_: (0,) * nd)


def _params(sem):
    return pltpu.CompilerParams(dimension_semantics=sem, vmem_limit_bytes=V7X_VMEM_LIMIT)


def _mem_softmax_pv(qm, kbd_ref, vbd_ref):
    s = _dot(_bf(qm), kbd_ref[...]) * (MEM_HD ** -0.5)
    out = None
    for h in range(MEM_H):
        sh = s[:, h * N_MEM:(h + 1) * N_MEM]
        mx = jnp.max(sh, axis=-1, keepdims=True)
        e = jnp.exp(sh - mx)
        p = e / jnp.sum(e, axis=-1, keepdims=True)
        t = _dot(_bf(p), vbd_ref[h * N_MEM:(h + 1) * N_MEM, :])
        out = t if out is None else out + t
    return out


def _memkv_kernel(mem_ref, g_ref, w_ref, k_ref, v_ref, kbd_ref, vbd_ref):
    h = _bf(_rms(mem_ref[0], g_ref[0]))
    kv = _dot(h, w_ref[0])
    k = kv[:, :MEM_W]
    v = kv[:, MEM_W:]
    kt = k.T
    k_ref[0, 0] = kt
    v_ref[0, 0] = v.T
    kt4 = jnp.concatenate([kt, kt, kt, kt], axis=1)
    keep_k = (_iota((MEM_W, MEM_H * N_MEM), 0) >> 6) == (_iota((MEM_W, MEM_H * N_MEM), 1) >> 8)
    kbd_ref[0, 0] = _bf(jnp.where(keep_k, kt4, 0.0))
    v4 = jnp.concatenate([v, v, v, v], axis=0)
    keep_v = (_iota((MEM_H * N_MEM, MEM_W), 0) >> 8) == (_iota((MEM_H * N_MEM, MEM_W), 1) >> 6)
    vbd_ref[0, 0] = _bf(jnp.where(keep_v, v4, 0.0))


def _memkv_call(mem, norm_mem, w_mem_kv):
    nb = mem.shape[0]
    nl = w_mem_kv.shape[0]
    g = norm_mem.reshape(nl, 1, D)
    w = _bf(w_mem_kv)
    return pl.pallas_call(
        _memkv_kernel,
        grid=(nl, nb),
        in_specs=[
            pl.BlockSpec((1, N_MEM, D), lambda l, b: (b, 0, 0)),
            pl.BlockSpec((1, 1, D), lambda l, b: (l, 0, 0)),
            pl.BlockSpec((1, D, 2 * MEM_W), lambda l, b: (l, 0, 0)),
        ],
        out_specs=[
            pl.BlockSpec((1, 1, MEM_W, N_MEM), lambda l, b: (l, b, 0, 0)),
            pl.BlockSpec((1, 1, MEM_W, N_MEM), lambda l, b: (l, b, 0, 0)),
            pl.BlockSpec((1, 1, MEM_W, MEM_H * N_MEM), lambda l, b: (l, b, 0, 0)),
            pl.BlockSpec((1, 1, MEM_H * N_MEM, MEM_W), lambda l, b: (l, b, 0, 0)),
        ],
        out_shape=[
            jax.ShapeDtypeStruct((nl, nb, MEM_W, N_MEM), F32),
            jax.ShapeDtypeStruct((nl, nb, MEM_W, N_MEM), F32),
            jax.ShapeDtypeStruct((nl, nb, MEM_W, MEM_H * N_MEM), BF16),
            jax.ShapeDtypeStruct((nl, nb, MEM_H * N_MEM, MEM_W), BF16),
        ],
        compiler_params=_params(("arbitrary", "arbitrary")),
        name="mem_kv",
    )(mem, g, w)


def _proj_a(x, gpre_ref, wqk_ref, wv_ref, wr_ref, wmisc_ref, wg_ref, bg_ref):
    h = _bf(_rms(x, gpre_ref[...]))
    qk = _dot(h, wqk_ref[...])
    q = qk[:, :GLA_KW] * (GLA_DK ** -0.5)
    k = qk[:, GLA_KW:]
    v = _dot(h, wv_ref[...])
    r = _dot(h, wr_ref[...])
    misc = _dot(h, wmisc_ref[...])
    glr = misc[:, :128]
    qm = misc[:, 128:]
    g = _log_sigmoid(_dot(_bf(glr), wg_ref[...]) + bg_ref[...]) * (1.0 / GATE_NORM)
    return q, k, g, v, r, qm


def _gla_out_gate(o, r, glan_ref):
    vh = _gla_v_head(_iota(o.shape, 1))
    o2 = o * o
    scale = jnp.zeros_like(o)
    for h in range(GLA_H):
        ss = jnp.sum(jnp.where(vh == h, o2, 0.0), axis=-1, keepdims=True) * (1.0 / GLA_DV)
        scale = jnp.where(vh == h, lax.rsqrt(ss + EPS), scale)
    return o * scale * glan_ref[...] * _silu(r)


def _mix_residual(x, o_main, o_mem, wo1_ref, wo2_ref, gpost_ref):
    mix = _dot(_bf(o_main), wo1_ref[...]) + _dot(_bf(o_mem), wo2_ref[...])
    return x + _rms(mix, gpost_ref[...])


def _make_gla_chunk(q_s, k_s, g_s, v_s, o_s, state_s, p_s):
    C = GLA_CHUNK

    ri = _iota((C, GLA_KW), 0)
    kh = _gla_k_head(_iota((C, GLA_KW), 1))
    khcat = jnp.concatenate([kh, kh, kh], axis=1)
    vh = _gla_v_head(_iota((C, GLA_VW), 1))
    tri = _bf(jnp.where(_iota((C, C), 0) >= _iota((C, C), 1), 1.0, 0.0))
    d_rs = _iota((C, 4 * C), 0) - (_iota((C, 4 * C), 1) & (C - 1))
    band = jnp.where((d_rs >= 0) & (d_rs <= (_iota((C, 4 * C), 0) & (GLA_SUB - 1))), d_rs, -1)
    ebc = _bf(jnp.where(_gla_k_head(_iota((GLA_KW, 4 * C), 0)) == (_iota((GLA_KW, 4 * C), 1) >> 6),
                        1.0, 0.0))
    blockmask = _gla_v_head(_iota((GLA_VW, GLA_KW), 0)) == _gla_k_head(_iota((GLA_VW, GLA_KW), 1))

    def chunk(c, carry):
        r0 = pl.multiple_of(c * C, C)
        qc = q_s[pl.ds(r0, C), :]
        kc = k_s[pl.ds(r0, C), :]
        gc = g_s[pl.ds(r0, C), :]
        vc = v_s[pl.ds(r0, C), :]
        b = _exact_dot(tri, gc)

        ref1 = jnp.broadcast_to(b[31:32, :], b.shape)
        ref2 = jnp.where(ri < 32, jnp.broadcast_to(b[15:16, :], b.shape),
                         jnp.broadcast_to(b[47:48, :], b.shape))
        q1 = qc * jnp.exp(jnp.minimum(b - ref1, 0.0))
        k1 = kc * jnp.exp(jnp.minimum(ref1 - b, 0.0))
        q2 = qc * jnp.exp(jnp.minimum(b - ref2, 0.0))
        k2 = kc * jnp.exp(jnp.minimum(ref2 - b, 0.0))
        qcat = jnp.concatenate([
            jnp.where(ri >= 32, q1, 0.0),
            jnp.where((ri >= 16) & (ri < 32), q2, 0.0),
            jnp.where(ri >= 48, q2, 0.0)], axis=1)
        kcat = jnp.concatenate([
            jnp.where(ri < 32, k1, 0.0),
            jnp.where(ri < 16, k2, 0.0),
            jnp.where((ri >= 32) & (ri < 48), k2, 0.0)], axis=1)
        kst = _bf(jnp.concatenate([jnp.where(khcat == h, kcat, 0.0) for h in range(GLA_H)], axis=0))
        a_off = _dot_nt(_bf(qcat), kst)

        for dlt in range(GLA_SUB):
            if dlt == 0:
                pr = qc * kc
            else:
                kd = pltpu.roll(kc, dlt, 0)
                bd = pltpu.roll(b, dlt, 0)
                pr = qc * kd * jnp.exp(jnp.minimum(b - bd, 0.0))
            p_s[dlt * C:(dlt + 1) * C, :] = _bf(pr)
        rsum = _dot(p_s[...], ebc)
        a_diag = jnp.zeros((C, 4 * C), F32)
        for dlt in range(GLA_SUB):
            a_diag = jnp.where(band == dlt, rsum[dlt * C:(dlt + 1) * C, :], a_diag)

        vst = _bf(jnp.concatenate([jnp.where(vh == h, vc, 0.0) for h in range(GLA_H)], axis=0))
        o_intra = _dot(_bf(a_off + a_diag), vst)
        st = state_s[...]
        o_inter = _dot_nt(_bf(qc * jnp.exp(b)), _bf(st))
        o_s[pl.ds(r0, C), :] = o_intra + o_inter

        blast = jnp.broadcast_to(b[C - 1:C, :], b.shape)
        kdec = kc * jnp.exp(blast - b)
        kvt = _dot_tn(_bf(vc), _bf(kdec))
        decay = jnp.exp(b[C - 1:C, :])
        state_s[...] = st * decay + jnp.where(blockmask, kvt, 0.0)
        return carry

    return chunk


def _make_gla_chunk_bounded(q_s, k_s, g_s, v_s, o_s, state_s):
    C = GLA_CHUNK
    kh = _gla_k_head(_iota((C, GLA_KW), 1))
    vh = _gla_v_head(_iota((C, GLA_VW), 1))
    tri = _bf(jnp.where(_iota((C, C), 0) >= _iota((C, C), 1), 1.0, 0.0))
    causal = _iota((C, 4 * C), 0) >= (_iota((C, 4 * C), 1) & (C - 1))
    blockmask = _gla_v_head(_iota((GLA_VW, GLA_KW), 0)) == _gla_k_head(_iota((GLA_VW, GLA_KW), 1))

    def chunk(c, carry):
        r0 = pl.multiple_of(c * C, C)
        qc = q_s[pl.ds(r0, C), :]
        kc = k_s[pl.ds(r0, C), :]
        vc = v_s[pl.ds(r0, C), :]
        b = _exact_dot(tri, g_s[pl.ds(r0, C), :])
        qe = _bf(qc * jnp.exp(b))
        ke = kc * jnp.exp(-b)
        kst = _bf(jnp.concatenate([jnp.where(kh == h, ke, 0.0) for h in range(GLA_H)], axis=0))
        a = jnp.where(causal, _dot_nt(qe, kst), 0.0)
        vst = _bf(jnp.concatenate([jnp.where(vh == h, vc, 0.0) for h in range(GLA_H)], axis=0))
        st = state_s[...]
        o_s[pl.ds(r0, C), :] = _dot(_bf(a), vst) + _dot_nt(qe, _bf(st))
        decay = jnp.exp(b[C - 1:C, :])
        kvt = _dot_tn(_bf(vc), _bf(ke * decay))
        state_s[...] = st * decay + jnp.where(blockmask, kvt, 0.0)
        return carry

    return chunk


def _mixer_a_kernel(x_ref, gpre_ref, wqk_ref, wv_ref, wr_ref, wmisc_ref, wg_ref, bg_ref, glan_ref,
                    kbd_ref, vbd_ref, wo1_ref, wo2_ref, gpost_ref,
                    xo_ref, st_ref,
                    q_s, k_s, g_s, v_s, o_s, state_s, p_s, r_s, qm_s):
    C = GLA_CHUNK
    tm = x_ref.shape[1]
    nchunk = tm // C

    @pl.when(pl.program_id(1) == 0)
    def _():
        state_s[...] = jnp.zeros_like(state_s)

    q, k, g, v, r, qm = _proj_a(x_ref[0], gpre_ref, wqk_ref, wv_ref, wr_ref, wmisc_ref, wg_ref, bg_ref)
    q_s[...] = q
    k_s[...] = k
    g_s[...] = g
    v_s[...] = v
    r_s[...] = r
    qm_s[...] = qm

    total = jnp.sum(g.reshape(nchunk, C, GLA_KW), axis=1)
    bounded = jnp.min(total) > -GLA_SAFE_DECAY

    @pl.when(bounded)
    def _():
        lax.fori_loop(0, nchunk, _make_gla_chunk_bounded(q_s, k_s, g_s, v_s, o_s, state_s), 0, unroll=2)

    @pl.when(jnp.logical_not(bounded))
    def _():
        lax.fori_loop(0, nchunk, _make_gla_chunk(q_s, k_s, g_s, v_s, o_s, state_s, p_s), 0)

    st_ref[0] = state_s[...]
    o_main = _gla_out_gate(o_s[...], r_s[...], glan_ref)
    o_mem = _mem_softmax_pv(qm_s[...], kbd_ref.at[0], vbd_ref.at[0])
    xo_ref[0] = _mix_residual(x_ref[0], o_main, o_mem, wo1_ref, wo2_ref, gpost_ref)


def _const_spec(a):
    nd = a.ndim
    return pl.BlockSpec(a.shape, lambda *_: (0,) * nd, pipeline_mode=pl.Buffered(1))


def _mixer_a_call(x, wa, kbd, vbd):
    nb, seq, _ = x.shape
    tm = TM_PROMPT
    weights = [wa["gpre"], wa["wqk"], wa["wv"], wa["wr"], wa["wmisc"], wa["wg"], wa["bg"], wa["glan"]]
    tail = [wa["wo1"], wa["wo2"], wa["gpost"]]
    return pl.pallas_call(
        _mixer_a_kernel,
        grid=(nb, seq // tm),
        in_specs=([pl.BlockSpec((1, tm, D), lambda b, t: (b, t, 0))]
                  + [_const_spec(w) for w in weights]
                  + [pl.BlockSpec((1, MEM_W, MEM_H * N_MEM), lambda b, t: (b, 0, 0)),
                     pl.BlockSpec((1, MEM_H * N_MEM, MEM_W), lambda b, t: (b, 0, 0))]
                  + [_const_spec(w) for w in tail]),
        out_specs=[pl.BlockSpec((1, tm, D), lambda b, t: (b, t, 0)),
                   pl.BlockSpec((1, GLA_VW, GLA_KW), lambda b, t: (b, 0, 0))],
        out_shape=[jax.ShapeDtypeStruct((nb, seq, D), F32),
                   jax.ShapeDtypeStruct((nb, GLA_VW, GLA_KW), F32)],
        scratch_shapes=[
            pltpu.VMEM((tm, GLA_KW), F32), pltpu.VMEM((tm, GLA_KW), F32), pltpu.VMEM((tm, GLA_KW), F32),
            pltpu.VMEM((tm, GLA_VW), F32), pltpu.VMEM((tm, GLA_VW), F32),
            pltpu.VMEM((GLA_VW, GLA_KW), F32),
            pltpu.VMEM((GLA_SUB * GLA_CHUNK, GLA_KW), BF16),
            pltpu.VMEM((tm, GLA_VW), F32), pltpu.VMEM((tm, MEM_W), F32),
        ],
        compiler_params=_params(("arbitrary", "arbitrary")),
        name="mixer_a_prompt",
    )(x, *weights, kbd, vbd, *tail)


def _ffn_kernel(x_ref, gpre_ref, wup_ref, wdn_ref, gpost_ref, o_ref, acc_s):
    x = x_ref[...]
    h = _bf(_rms(x, gpre_ref[...]))
    for j in range(D_FF // FF_CHUNK):
        u = _dot(h, wup_ref[:, j * FF_CHUNK:(j + 1) * FF_CHUNK])
        u = jnp.maximum(u, 0.0)
        d = _dot(_bf(u * u), wdn_ref[j * FF_CHUNK:(j + 1) * FF_CHUNK, :])
        if j == 0:
            acc_s[...] = d
        else:
            acc_s[...] += d
    o_ref[...] = x + _rms(acc_s[...], gpost_ref[...])


def _ffn_call(x2d, wf, tm):
    n = x2d.shape[0]
    ws = [wf["gpre"], wf["wup"], wf["wdn"], wf["gpost"]]
    return pl.pallas_call(
        _ffn_kernel,
        grid=(n // tm,),
        in_specs=[pl.BlockSpec((tm, D), lambda i: (i, 0))] + [_full_spec(w) for w in ws],
        out_specs=pl.BlockSpec((tm, D), lambda i: (i, 0)),
        out_shape=jax.ShapeDtypeStruct((n, D), F32),
        scratch_shapes=[pltpu.VMEM((tm, D), F32)],
        compiler_params=_params(("arbitrary",)),
        name="ffn",
    )(x2d, *ws)


def _t5_bucket(dist):
    max_exact = N_BUCKETS // 2
    n = np.maximum(dist, 0)
    nf = np.maximum(n, 1).astype(np.float32)
    large = max_exact + (np.log(nf / np.float32(max_exact)) / np.float32(math.log(MAX_DISTANCE / max_exact))
                         * np.float32(N_BUCKETS - max_exact)).astype(np.int32)
    large = np.minimum(large, N_BUCKETS - 1)
    return np.where(n < max_exact, n, large)


def _swa_bias_tables():
    qi = np.arange(WINDOW)[:, None] + WINDOW
    kj = np.arange(2 * WINDOW)[None, :]
    dist = qi - kj
    valid = (dist >= 0) & (dist < WINDOW)
    return np.where(valid, _t5_bucket(dist), -1).astype(np.int32)


def _mixer_b_kernel(rb_ref, sink_ref,
                    x_ref, bkt_ref, gkv_ref, wkv_ref, gpre_ref, wq_ref, wqm_ref,
                    kbd_ref, vbd_ref, wo1_ref, wo2_ref, gpost_ref,
                    xo_ref, kc_ref, vc_ref,
                    kbuf, vbuf, q_s, o_s, bias_s):
    W = WINDOW
    tm = x_ref.shape[1]
    bb = pl.program_id(0)
    t = pl.program_id(1)
    nheads = SWA_G * SWA_KVH

    @pl.when((bb == 0) & (t == 0))
    def _():
        bkt = bkt_ref[...]
        own = _iota((W, 2 * W), 1) >= W
        for i in range(nheads):
            def add_bucket(n, acc):
                return jnp.where(bkt == n, rb_ref[n, i], acc)
            tab = lax.fori_loop(0, N_BUCKETS, add_bucket, jnp.zeros((W, 2 * W), F32))
            tab = jnp.where(bkt < 0, NEG_INF, tab)
            bias_s[0, i] = tab
            bias_s[1, i] = jnp.where(own, tab, NEG_INF)

    @pl.when(t == 0)
    def _():
        kbuf[0:W, :] = jnp.zeros((W, SWA_KVW), F32)
        vbuf[0:W, :] = jnp.zeros((W, SWA_KVW), F32)

    x = x_ref[0]
    xn = x * lax.rsqrt(jnp.mean(x * x, axis=-1, keepdims=True) + EPS)
    kv = _dot(_bf(xn * gkv_ref[...]), wkv_ref[...])
    kbuf[W:W + tm, :] = kv[:, :SWA_KVW]
    vbuf[W:W + tm, :] = kv[:, SWA_KVW:]
    kc_ref[0] = kv[tm - W:, :SWA_KVW]
    vc_ref[0] = kv[tm - W:, SWA_KVW:]

    h = _bf(xn * gpre_ref[...])
    q_s[...] = _dot(h, wq_ref[...]) * (SWA_HD ** -0.5)
    qm = _dot(h, wqm_ref[...])

    lane_head = _iota((W, SWA_KVW), 1) >> 6
    key_head = _iota((2 * W, SWA_KVW), 1) >> 6

    def block(j, carry):
        r0 = pl.multiple_of(j * W, W)
        kb = kbuf[pl.ds(r0, 2 * W), :]
        vb = vbuf[pl.ds(r0, 2 * W), :]
        tab = jnp.where((j == 0) & (t == 0), 1, 0)
        qrows = []
        for gi in range(SWA_G):
            qg = q_s[pl.ds(r0, W), gi * SWA_KVW:(gi + 1) * SWA_KVW]
            for hh in range(SWA_KVH):
                qrows.append(jnp.where(lane_head == hh, qg, 0.0))
        s_all = _dot_nt(_bf(jnp.concatenate(qrows, axis=0)), _bf(kb))
        vst = _bf(jnp.concatenate([jnp.where(key_head == hh, vb, 0.0) for hh in range(SWA_KVH)], axis=0))
        for gi in range(SWA_G):
            ps = []
            for hh in range(SWA_KVH):
                i = gi * SWA_KVH + hh
                s = s_all[i * W:(i + 1) * W, :] + bias_s[tab, i]
                sink = sink_ref[i]
                mx = jnp.maximum(jnp.max(s, axis=-1, keepdims=True), sink)
                p = jnp.exp(s - mx)
                p = p / (jnp.sum(p, axis=-1, keepdims=True) + jnp.exp(sink - mx))
                ps.append(_bf(p))
            o_s[pl.ds(r0, W), gi * SWA_KVW:(gi + 1) * SWA_KVW] = _dot(jnp.concatenate(ps, axis=1), vst)
        return carry

    lax.fori_loop(0, tm // W, block, 0, unroll=2)

    kbuf[0:W, :] = kbuf[tm:tm + W, :]
    vbuf[0:W, :] = vbuf[tm:tm + W, :]

    o_mem = _mem_softmax_pv(qm, kbd_ref.at[0], vbd_ref.at[0])
    xo_ref[0] = _mix_residual(x, o_s[...], o_mem, wo1_ref, wo2_ref, gpost_ref)


def _mixer_b_call(x, wb, kbd, vbd):
    nb, seq, _ = x.shape
    tm = TM_PROMPT
    bkt = jnp.asarray(_swa_bias_tables())
    head = [bkt, wb["gkv"], wb["wkv"], wb["gpre"], wb["wq"], wb["wqm"]]
    tail = [wb["wo1"], wb["wo2"], wb["gpost"]]
    smem = pl.BlockSpec(memory_space=pltpu.SMEM)
    return pl.pallas_call(
        _mixer_b_kernel,
        grid=(nb, seq // tm),
        in_specs=([smem, smem, pl.BlockSpec((1, tm, D), lambda b, t: (b, t, 0))]
                  + [_full_spec(w) for w in head]
                  + [pl.BlockSpec((1, MEM_W, MEM_H * N_MEM), lambda b, t: (b, 0, 0)),
                     pl.BlockSpec((1, MEM_H * N_MEM, MEM_W), lambda b, t: (b, 0, 0))]
                  + [_full_spec(w) for w in tail]),
        out_specs=[pl.BlockSpec((1, tm, D), lambda b, t: (b, t, 0)),
                   pl.BlockSpec((1, WINDOW, SWA_KVW), lambda b, t: (b, 0, 0)),
                   pl.BlockSpec((1, WINDOW, SWA_KVW), lambda b, t: (b, 0, 0))],
        out_shape=[jax.ShapeDtypeStruct((nb, seq, D), F32),
                   jax.ShapeDtypeStruct((nb, WINDOW, SWA_KVW), F32),
                   jax.ShapeDtypeStruct((nb, WINDOW, SWA_KVW), F32)],
        scratch_shapes=[
            pltpu.VMEM((tm + WINDOW, SWA_KVW), F32), pltpu.VMEM((tm + WINDOW, SWA_KVW), F32),
            pltpu.VMEM((tm, SWA_QW), F32), pltpu.VMEM((tm, SWA_QW), F32),
            pltpu.VMEM((2, SWA_G * SWA_KVH, WINDOW, 2 * WINDOW), F32),
        ],
        compiler_params=_params(("arbitrary", "arbitrary")),
        name="mixer_b_prompt",
    )(wb["rb"], wb["sinks"], x, *head, kbd, vbd, *tail)


def _row_to_col(row, eye):
    return jnp.sum(jnp.where(eye, jnp.broadcast_to(row, eye.shape), 0.0), axis=1, keepdims=True)


def _eye(n):
    return _iota((n, n), 0) == _iota((n, n), 1)


def _pre_a_kernel(x_ref, gpre_ref, wqk_ref, wv_ref, wr_ref, wmisc_ref, wg_ref, bg_ref,
                  q_ref, k_ref, g_ref, v_ref, r_ref, qm_ref):
    q, k, g, v, r, qm = _proj_a(x_ref[...], gpre_ref, wqk_ref, wv_ref, wr_ref, wmisc_ref, wg_ref, bg_ref)
    q_ref[...] = q
    k_ref[...] = k
    g_ref[...] = g
    v_ref[...] = v
    r_ref[...] = r
    qm_ref[...] = qm


def _pre_a_call(x2d, wa):
    n = x2d.shape[0]
    ws = [wa["gpre"], wa["wqk"], wa["wv"], wa["wr"], wa["wmisc"], wa["wg"], wa["bg"]]
    shapes = [(n, GLA_KW), (n, GLA_KW), (n, GLA_KW), (n, GLA_VW), (n, GLA_VW), (n, MEM_W)]
    return pl.pallas_call(
        _pre_a_kernel,
        grid=(1,),
        in_specs=[_full_spec(x2d)] + [_full_spec(w) for w in ws],
        out_specs=[pl.BlockSpec(s, lambda i: (0, 0)) for s in shapes],
        out_shape=[jax.ShapeDtypeStruct(s, F32) for s in shapes],
        compiler_params=_params(("arbitrary",)),
        name="pre_a_sample",
    )(x2d, *ws)


def _gla_step_kernel(qt_ref, kt_ref, gt_ref, vt_ref, s_ref, so_ref, ot_ref):
    c = pl.program_id(1)
    vt = vt_ref[...]
    acc = jnp.zeros_like(vt)
    for d in range(GLA_DK_BLOCK):
        s_new = jnp.exp(gt_ref[d:d + 1, :]) * s_ref[0, 0, d] + kt_ref[d:d + 1, :] * vt
        so_ref[0, 0, d] = s_new
        acc = acc + qt_ref[d:d + 1, :] * s_new

    @pl.when(c == 0)
    def _():
        ot_ref[...] = acc

    @pl.when(c > 0)
    def _():
        ot_ref[...] += acc


def _gla_step_call(qt, kt, gt, vt, state5):
    n = qt.shape[1]
    nblk = GLA_DK // GLA_DK_BLOCK
    rows = pl.BlockSpec((GLA_DK_BLOCK, n), lambda h, c: (h * nblk + c, 0))
    head = pl.BlockSpec((GLA_DV, n), lambda h, c: (h, 0))
    st = pl.BlockSpec((1, 1, GLA_DK_BLOCK, GLA_DV, n), lambda h, c: (0, h, c, 0, 0))
    return pl.pallas_call(
        _gla_step_kernel,
        grid=(GLA_H, nblk),
        in_specs=[rows, rows, rows, head, st],
        out_specs=[st, head],
        out_shape=[jax.ShapeDtypeStruct(state5.shape, F32), jax.ShapeDtypeStruct((GLA_VW, n), F32)],
        compiler_params=_params(("arbitrary", "arbitrary")),
        name="gla_step_sample",
    )(qt, kt, gt, vt, state5)


def _mem_step_kernel(qm_ref, mk_ref, mv_ref, o_ref):
    own = (_iota((8, MEM_W), 1) >> 6) == _iota((8, MEM_W), 0)
    for i in range(qm_ref.shape[0]):
        q8 = jnp.where(own, jnp.broadcast_to(qm_ref[i], (8, MEM_W)), 0.0)
        s = _dot(_bf(q8), _bf(mk_ref[0, i].reshape(MEM_W, N_MEM))) * (MEM_HD ** -0.5)
        e = jnp.exp(s - jnp.max(s, axis=1, keepdims=True))
        p = e / jnp.sum(e, axis=1, keepdims=True)
        res = _dot_nt(_bf(p), _bf(mv_ref[0, i].reshape(MEM_W, N_MEM)))
        o_ref[i] = jnp.sum(jnp.where(own, res, 0.0), axis=0, keepdims=True)


def _mem_step_call(qm, mk5, mv5, layer):
    n = qm.shape[0]
    rb = MEM_ROWS_PER_STEP
    blk = pl.BlockSpec((1, rb, MEM_H, MEM_HD, N_MEM), lambda i: (layer, i, 0, 0, 0))
    rows = pl.BlockSpec((rb, 1, MEM_W), lambda i: (i, 0, 0))
    return pl.pallas_call(
        _mem_step_kernel,
        grid=(n // rb,),
        in_specs=[rows, blk, blk],
        out_specs=rows,
        out_shape=jax.ShapeDtypeStruct((n, 1, MEM_W), F32),
        compiler_params=_params(("arbitrary",)),
        name="mem_step_sample",
    )(qm.reshape(n, 1, MEM_W), mk5, mv5).reshape(n, MEM_W)


def _post_a_kernel(x_ref, o_ref, r_ref, om_ref, glan_ref, wo1_ref, wo2_ref, gpost_ref, xo_ref):
    o_main = _gla_out_gate(o_ref[...], r_ref[...], glan_ref)
    xo_ref[...] = _mix_residual(x_ref[...], o_main, om_ref[...], wo1_ref, wo2_ref, gpost_ref)


def _post_a_call(x2d, o, r, om, wa):
    args = [x2d, o, r, om, wa["glan"], wa["wo1"], wa["wo2"], wa["gpost"]]
    return pl.pallas_call(
        _post_a_kernel,
        grid=(1,),
        in_specs=[_full_spec(a) for a in args],
        out_specs=_full_spec(x2d),
        out_shape=jax.ShapeDtypeStruct(x2d.shape, F32),
        compiler_params=_params(("arbitrary",)),
        name="post_a_sample",
    )(*args)


def _pre_b_kernel(x_ref, gkv_ref, wkv_ref, gpre_ref, wq_ref, wqm_ref, ks_ref, vs_ref, q_ref, qm_ref):
    x = x_ref[...]
    kv = _dot(_bf(_rms(x, gkv_ref[...])), wkv_ref[...])
    ks_ref[...] = kv[:, :SWA_KVW]
    vs_ref[...] = kv[:, SWA_KVW:]
    h = _bf(_rms(x, gpre_ref[...]))
    q_ref[...] = _dot(h, wq_ref[...])
    qm_ref[...] = _dot(h, wqm_ref[...])


def _pre_b_call(x2d, wb):
    n = x2d.shape[0]
    ws = [wb["gkv"], wb["wkv"], wb["gpre"], wb["wq"], wb["wqm"]]
    widths = [SWA_KVW, SWA_KVW, SWA_QW, MEM_W]
    return pl.pallas_call(
        _pre_b_kernel,
        grid=(1,),
        in_specs=[_full_spec(x2d)] + [_full_spec(w) for w in ws],
        out_specs=[pl.BlockSpec((n, w), lambda i: (0, 0)) for w in widths],
        out_shape=[jax.ShapeDtypeStruct((n, w), F32) for w in widths],
        compiler_params=_params(("arbitrary",)),
        name="pre_b_sample",
    )(x2d, *ws)


def _sample_buckets():
    dist = (WINDOW - 1) - np.arange(WINDOW)
    return _t5_bucket(dist).astype(np.int32).reshape(1, WINDOW)


def _swa_step_kernel(sink_ref, rb_ref, bkt_ref, kc_ref, vc_ref, kst_ref, vst_ref, q_ref,
                     kn_ref, vn_ref, o_ref, bias_s, sink_s, s_s, p_s):
    W = WINDOW
    R = SWA_KVH * 8

    @pl.when(pl.program_id(0) == 0)
    def _():
        bkt = bkt_ref[...]
        rid = _iota((R, W), 0)
        bias = jnp.zeros((R, W), F32)
        sink = jnp.zeros((R, W), F32)
        for h in range(SWA_KVH):
            for g in range(SWA_G):
                idx = g * SWA_KVH + h
                def add_bucket(n, acc):
                    return jnp.where(bkt == n, rb_ref[n, idx], acc)
                brow = lax.fori_loop(0, N_BUCKETS, add_bucket, jnp.zeros((1, W), F32))
                bias = jnp.where(rid == h * 8 + g, brow, bias)
                sink = jnp.where(rid == h * 8 + g, sink_ref[idx], sink)
        bias_s[...] = bias
        sink_s[...] = sink

    rb = q_ref.shape[0]
    base = pl.program_id(0) * rb
    last = _iota((SWA_KVW, W), 1) == W - 1
    own = (_iota((R, SWA_KVW), 1) >> 6) == (_iota((R, SWA_KVW), 0) >> 3)
    kst = kst_ref[...]
    vst = vst_ref[...]
    for i in range(rb):
        shift = W - 1 - (base + i)
        kn = jnp.where(last, pltpu.roll(kst, shift, 1), pltpu.roll(kc_ref[i].reshape(SWA_KVW, W), W - 1, 1))
        vn = jnp.where(last, pltpu.roll(vst, shift, 1), pltpu.roll(vc_ref[i].reshape(SWA_KVW, W), W - 1, 1))
        kn_ref[i] = kn.reshape(SWA_KVH, SWA_HD, W)
        vn_ref[i] = vn.reshape(SWA_KVH, SWA_HD, W)
        q32 = jnp.where(own, jnp.concatenate([q_ref[i]] * SWA_KVH, axis=0), 0.0)
        s_s[i * R:(i + 1) * R, :] = _dot(_bf(q32), _bf(kn))
    s = s_s[...] * (SWA_HD ** -0.5) + jnp.concatenate([bias_s[...]] * rb, axis=0)
    sink = jnp.concatenate([sink_s[...]] * rb, axis=0)
    mx = jnp.maximum(jnp.max(s, axis=1, keepdims=True), sink)
    p = jnp.exp(s - mx)
    p_s[...] = _bf(p / (jnp.sum(p, axis=1, keepdims=True) + jnp.exp(sink - mx)))
    for i in range(rb):
        vn = vn_ref[i].reshape(SWA_KVW, W)
        res = jnp.where(own, _dot_nt(p_s[i * R:(i + 1) * R, :], _bf(vn)), 0.0)
        o_ref[i] = res[0:8] + res[8:16] + res[16:24] + res[24:32]


def _swa_step_call(kc4, vc4, ks, vs, q, wb):
    n = q.shape[0]
    rb = SWA_ROWS_PER_STEP
    bkt = jnp.asarray(_sample_buckets())
    q8 = jnp.pad(q.reshape(n, SWA_G, SWA_KVW), ((0, 0), (0, 8 - SWA_G), (0, 0)))
    row3 = lambda r, w: pl.BlockSpec((rb, r, w), lambda i: (i, 0, 0))
    cache = pl.BlockSpec((rb, SWA_KVH, SWA_HD, WINDOW), lambda i: (i, 0, 0, 0))
    smem = pl.BlockSpec(memory_space=pltpu.SMEM)
    kn4, vn4, o8 = pl.pallas_call(
        _swa_step_kernel,
        grid=(n // rb,),
        in_specs=[smem, smem, _full_spec(bkt),
                  cache, cache, pl.BlockSpec((SWA_KVW, n), lambda i: (0, 0)),
                  pl.BlockSpec((SWA_KVW, n), lambda i: (0, 0)), row3(8, SWA_KVW)],
        out_specs=[cache, cache, row3(8, SWA_KVW)],
        out_shape=[jax.ShapeDtypeStruct(kc4.shape, F32), jax.ShapeDtypeStruct(vc4.shape, F32),
                   jax.ShapeDtypeStruct((n, 8, SWA_KVW), F32)],
        scratch_shapes=[pltpu.VMEM((SWA_KVH * 8, WINDOW), F32), pltpu.VMEM((SWA_KVH * 8, WINDOW), F32),
                        pltpu.VMEM((rb * SWA_KVH * 8, WINDOW), F32), pltpu.VMEM((rb * SWA_KVH * 8, WINDOW), BF16)],
        compiler_params=_params(("arbitrary",)),
        name="swa_step_sample",
    )(wb["sinks"], wb["rb"], bkt, kc4, vc4, ks.T, vs.T, q8)
    return kn4, vn4, o8[:, :SWA_G].reshape(n, SWA_QW)


def _post_b_kernel(x_ref, o_ref, om_ref, wo1_ref, wo2_ref, gpost_ref, xo_ref):
    xo_ref[...] = _mix_residual(x_ref[...], o_ref[...], om_ref[...], wo1_ref, wo2_ref, gpost_ref)


def _post_b_call(x2d, o, om, wb):
    args = [x2d, o, om, wb["wo1"], wb["wo2"], wb["gpost"]]
    return pl.pallas_call(
        _post_b_kernel,
        grid=(1,),
        in_specs=[_full_spec(a) for a in args],
        out_specs=_full_spec(x2d),
        out_shape=jax.ShapeDtypeStruct(x2d.shape, F32),
        compiler_params=_params(("arbitrary",)),
        name="post_b_sample",
    )(*args)


def _prep_weights(norm_mix_pre, norm_mix_post, norm_ffn_pre, norm_ffn_post, w_in_a, w_gate_up, b_gate,
                  gla_norm, w_in_b, sinks, norm_kv, w_kv, rel_bias, w_out, w_ffn_up, w_ffn_down):
    row = lambda g: g.reshape(1, -1)
    wa_in = w_in_a[0]
    c_v = 2 * GLA_KW
    c_r = c_v + GLA_VW
    c_g = c_r + GLA_VW
    c_m = c_g + GATE_RANK
    wmisc = jnp.concatenate([wa_in[:, c_g:c_m], jnp.zeros((D, 128 - GATE_RANK), F32), wa_in[:, c_m:]], axis=1)
    wg = jnp.zeros((128, GLA_KW), F32).at[:GATE_RANK].set(w_gate_up[0])
    wa = dict(
        gpre=row(norm_mix_pre[0]), wqk=_bf(wa_in[:, :c_v]), wv=_bf(wa_in[:, c_v:c_r]), wr=_bf(wa_in[:, c_r:c_g]),
        wmisc=_bf(wmisc), wg=_bf(wg), bg=row(b_gate[0]), glan=row(jnp.tile(gla_norm[0], GLA_H)),
        wo1=_bf(w_out[0][:GLA_VW]), wo2=_bf(w_out[0][GLA_VW:]), gpost=row(norm_mix_post[0]))
    wb_in = w_in_b[0]
    wq = wb_in[:, :SWA_QW].reshape(D, SWA_KVH, SWA_G, SWA_HD).transpose(0, 2, 1, 3).reshape(D, SWA_QW)
    wo1 = w_out[1][:SWA_QW].reshape(SWA_KVH, SWA_G, SWA_HD, D).transpose(1, 0, 2, 3).reshape(SWA_QW, D)
    rb = rel_bias.reshape(N_BUCKETS, SWA_KVH, SWA_G).transpose(0, 2, 1).reshape(N_BUCKETS, SWA_G * SWA_KVH)
    sk = sinks[0].reshape(SWA_KVH, SWA_G).T.reshape(SWA_G * SWA_KVH)
    wb = dict(
        gkv=row(norm_kv), wkv=_bf(w_kv), gpre=row(norm_mix_pre[1]), wq=_bf(wq), wqm=_bf(wb_in[:, SWA_QW:]),
        wo1=_bf(wo1), wo2=_bf(w_out[1][SWA_QW:]), gpost=row(norm_mix_post[1]), rb=rb, sinks=sk)
    ffn = [dict(gpre=row(norm_ffn_pre[l]), wup=_bf(w_ffn_up[l]), wdn=_bf(w_ffn_down[l]),
                gpost=row(norm_ffn_post[l])) for l in range(2)]
    return wa, wb, ffn


def kernel(x_prompt, x_sample, state_gla, cache_swa_k, cache_swa_v, cache_mem_k, cache_mem_v, mem_prompt,
           norm_mix_pre, norm_mix_post, norm_ffn_pre, norm_ffn_post, norm_mem, w_mem_kv, w_in_a, w_gate_up,
           b_gate, gla_norm, w_in_b, sinks, norm_kv, w_kv, rel_bias, w_out, w_ffn_up, w_ffn_down):
    wa, wb, ffn = _prep_weights(norm_mix_pre, norm_mix_post, norm_ffn_pre, norm_ffn_post, w_in_a, w_gate_up,
                                b_gate, gla_norm, w_in_b, sinks, norm_kv, w_kv, rel_bias, w_out, w_ffn_up,
                                w_ffn_down)
    nb, seq, _ = x_prompt.shape
    ns = x_sample.shape[0]

    mkt, mvt, kbd, vbd = _memkv_call(mem_prompt, norm_mem, w_mem_kv)
    x1, st = _mixer_a_call(x_prompt, wa, kbd[0], vbd[0])
    x2 = _ffn_call(x1.reshape(nb * seq, D), ffn[0], TM_PROMPT).reshape(nb, seq, D)
    x3, kc, vc = _mixer_b_call(x2, wb, kbd[1], vbd[1])
    y_prompt = _ffn_call(x3.reshape(nb * seq, D), ffn[1], TM_PROMPT).reshape(nb, seq, D)
    st4 = st.reshape(nb, GLA_H, GLA_DV, GLA_H, GLA_DK)
    state_prompt = jnp.stack([st4[:, h, :, h, :] for h in range(GLA_H)], axis=1).transpose(0, 1, 3, 2)[None]
    to_mem = lambda t: t.reshape(2, nb, MEM_H, MEM_HD, N_MEM).transpose(0, 1, 4, 2, 3)
    swa_shape = (nb, WINDOW, SWA_KVH, SWA_HD)

    xs = x_sample.reshape(ns, D)
    state5 = jnp.transpose(state_gla, (0, 2, 3, 4, 1))
    mk5 = jnp.transpose(cache_mem_k, (0, 1, 3, 4, 2))
    mv5 = jnp.transpose(cache_mem_v, (0, 1, 3, 4, 2))
    kc4 = jnp.transpose(cache_swa_k, (0, 2, 3, 1))
    vc4 = jnp.transpose(cache_swa_v, (0, 2, 3, 1))

    q, k, g, v, r, qm = _pre_a_call(xs, wa)
    state5_new, ot = _gla_step_call(q.T, k.T, g.T, v.T, state5)
    om = _mem_step_call(qm, mk5, mv5, 0)
    xs1 = _post_a_call(xs, ot.T, r, om, wa)
    xs2 = _ffn_call(xs1, ffn[0], ns)
    ks, vs, qb, qmb = _pre_b_call(xs2, wb)
    kn4, vn4, ob = _swa_step_call(kc4, vc4, ks, vs, qb, wb)
    omb = _mem_step_call(qmb, mk5, mv5, 1)
    xs3 = _post_b_call(xs2, ob, omb, wb)
    y_sample = _ffn_call(xs3, ffn[1], ns).reshape(ns, 1, D)

    return (y_prompt, y_sample, state_prompt,
            jnp.transpose(state5_new, (0, 4, 1, 2, 3)),
            kc.reshape(swa_shape), vc.reshape(swa_shape),
            jnp.transpose(kn4, (0, 3, 1, 2)), jnp.transpose(vn4, (0, 3, 1, 2)),
            to_mem(mkt), to_mem(mvt))
```

```python
import functools
import math

import numpy as np
import jax
import jax.numpy as jnp
from jax import lax
from jax.experimental import pallas as pl
from jax.experimental.pallas import tpu as pltpu

F32 = jnp.float32
BF16 = jnp.bfloat16

D = 1024
D_FF = 4 * D
N_MEM = 256
MEM_H = 4
MEM_HD = 64
MEM_W = MEM_H * MEM_HD
GLA_H = 4
GLA_DK = 96
GLA_DV = 192
GLA_KW = GLA_H * GLA_DK
GLA_VW = GLA_H * GLA_DV
GATE_RANK = 16
GATE_NORM = 16.0
SWA_HD = 64
SWA_KVH = 4
SWA_G = 3
SWA_QW = SWA_KVH * SWA_G * SWA_HD
SWA_KVW = SWA_KVH * SWA_HD
WINDOW = 128
N_BUCKETS = 32
MAX_DISTANCE = 128
EPS = 1e-6

GLA_CHUNK = 64
GLA_SUB = 16
GLA_SAFE_DECAY = 60.0
TM_PROMPT = 512
FF_CHUNK = 512
GLA_DK_BLOCK = 32
MEM_ROWS_PER_STEP = 16
SWA_ROWS_PER_STEP = 16
V7X_VMEM_LIMIT = 56 * 1024 * 1024
NEG_INF = float("-inf")


def _bf(x):
    return x.astype(BF16)


def _dot(a, b):
    return jnp.dot(a, b, preferred_element_type=F32)


def _dot_nt(a, b):
    return lax.dot_general(a, b, (((1,), (1,)), ((), ())), preferred_element_type=F32)


def _dot_tn(a, b):
    return lax.dot_general(a, b, (((0,), (0,)), ((), ())), preferred_element_type=F32)


def _rms(x, g):
    return x * lax.rsqrt(jnp.mean(x * x, axis=-1, keepdims=True) + EPS) * g


def _split3(x):
    x1 = _bf(x)
    r1 = x - x1.astype(F32)
    x2 = _bf(r1)
    x3 = _bf(r1 - x2.astype(F32))
    return x1, x2, x3


def _exact_dot(sel, x):
    x1, x2, x3 = _split3(x)
    return _dot(sel, x1) + _dot(sel, x2) + _dot(sel, x3)


def _log_sigmoid(z):
    return jnp.minimum(z, 0.0) - jnp.log1p(jnp.exp(-jnp.abs(z)))


def _silu(z):
    return z * (1.0 / (1.0 + jnp.exp(-z)))


def _iota(shape, dim):
    return lax.broadcasted_iota(jnp.int32, shape, dim)


def _gla_k_head(lane):
    one = jnp.int32(1)
    zero = jnp.int32(0)
    return (jnp.where(lane >= GLA_DK, one, zero) + jnp.where(lane >= 2 * GLA_DK, one, zero)
            + jnp.where(lane >= 3 * GLA_DK, one, zero))


def _gla_v_head(lane):
    one = jnp.int32(1)
    zero = jnp.int32(0)
    return (jnp.where(lane >= GLA_DV, one, zero) + jnp.where(lane >= 2 * GLA_DV, one, zero)
            + jnp.where(lane >= 3 * GLA_DV, one, zero))


def _full_spec(a):
    nd = a.ndim
    return pl.BlockSpec(a.shape, lambda *_: (0,) * nd)


def _params(sem):
    return pltpu.CompilerParams(dimension_semantics=sem, vmem_limit_bytes=V7X_VMEM_LIMIT)


def _mem_probs(qm, kbd_ref):
    s = _dot(_bf(qm), kbd_ref[...]) * (MEM_HD ** -0.5)
    ps = []
    for h in range(MEM_H):
        sh = s[:, h * N_MEM:(h + 1) * N_MEM]
        e = jnp.exp(sh - jnp.max(sh, axis=-1, keepdims=True))
        ps.append(_bf(e / jnp.sum(e, axis=-1, keepdims=True)))
    return ps


def _mem_pv(ps, vbd_ref):
    out = None
    for h in range(MEM_H):
        t = _dot(ps[h], vbd_ref[h * N_MEM:(h + 1) * N_MEM, :])
        out = t if out is None else out + t
    return out


def _memkv_kernel(mem_ref, g_ref, w_ref, k_ref, v_ref, kbd_ref, vbd_ref):
    h = _bf(_rms(mem_ref[0], g_ref[0]))
    kv = _dot(h, w_ref[0])
    k = kv[:, :MEM_W]
    v = kv[:, MEM_W:]
    kt = k.T
    k_ref[0, 0] = kt
    v_ref[0, 0] = v.T
    kt4 = jnp.concatenate([kt, kt, kt, kt], axis=1)
    keep_k = (_iota((MEM_W, MEM_H * N_MEM), 0) >> 6) == (_iota((MEM_W, MEM_H * N_MEM), 1) >> 8)
    kbd_ref[0, 0] = _bf(jnp.where(keep_k, kt4, 0.0))
    v4 = jnp.concatenate([v, v, v, v], axis=0)
    keep_v = (_iota((MEM_H * N_MEM, MEM_W), 0) >> 8) == (_iota((MEM_H * N_MEM, MEM_W), 1) >> 6)
    vbd_ref[0, 0] = _bf(jnp.where(keep_v, v4, 0.0))


def _memkv_call(mem, norm_mem, w_mem_kv):
    nb = mem.shape[0]
    nl = w_mem_kv.shape[0]
    g = norm_mem.reshape(nl, 1, D)
    w = _bf(w_mem_kv)
    return pl.pallas_call(
        _memkv_kernel,
        grid=(nl, nb),
        in_specs=[
            pl.BlockSpec((1, N_MEM, D), lambda l, b: (b, 0, 0)),
            pl.BlockSpec((1, 1, D), lambda l, b: (l, 0, 0)),
            pl.BlockSpec((1, D, 2 * MEM_W), lambda l, b: (l, 0, 0)),
        ],
        out_specs=[
            pl.BlockSpec((1, 1, MEM_W, N_MEM), lambda l, b: (l, b, 0, 0)),
            pl.BlockSpec((1, 1, MEM_W, N_MEM), lambda l, b: (l, b, 0, 0)),
            pl.BlockSpec((1, 1, MEM_W, MEM_H * N_MEM), lambda l, b: (l, b, 0, 0)),
            pl.BlockSpec((1, 1, MEM_H * N_MEM, MEM_W), lambda l, b: (l, b, 0, 0)),
        ],
        out_shape=[
            jax.ShapeDtypeStruct((nl, nb, MEM_W, N_MEM), F32),
            jax.ShapeDtypeStruct((nl, nb, MEM_W, N_MEM), F32),
            jax.ShapeDtypeStruct((nl, nb, MEM_W, MEM_H * N_MEM), BF16),
            jax.ShapeDtypeStruct((nl, nb, MEM_H * N_MEM, MEM_W), BF16),
        ],
        compiler_params=_params(("arbitrary", "arbitrary")),
        name="mem_kv",
    )(mem, g, w)


def _proj_a(x, gpre_ref, wqk_ref, wv_ref, wr_ref, wmisc_ref, wg_ref, bg_ref, glan_ref, mem_refs=None):
    h = _bf(_rms(x, gpre_ref[...]))
    misc = _dot(h, wmisc_ref[...])
    glr = misc[:, :128]
    qm = misc[:, 128:]
    ps = None if mem_refs is None else _mem_probs(qm, mem_refs[0])
    g = _log_sigmoid(_dot(_bf(glr), wg_ref[...]) + bg_ref[...]) * (1.0 / GATE_NORM)
    gate = glan_ref[...] * _silu(_dot(h, wr_ref[...]))
    qk = _dot(h, wqk_ref[...])
    q = qk[:, :GLA_KW] * (GLA_DK ** -0.5)
    k = qk[:, GLA_KW:]
    v = _dot(h, wv_ref[...])
    return q, k, g, v, gate, (qm if mem_refs is None else _mem_pv(ps, mem_refs[1]))


def _gla_out_gate(o, gate):
    vh = _gla_v_head(_iota(o.shape, 1))
    o2 = o * o
    scale = jnp.zeros_like(o)
    for h in range(GLA_H):
        ss = jnp.sum(jnp.where(vh == h, o2, 0.0), axis=-1, keepdims=True) * (1.0 / GLA_DV)
        scale = jnp.where(vh == h, lax.rsqrt(ss + EPS), scale)
    return o * scale * gate


def _mix_residual(x, o_main, o_mem, wo1_ref, wo2_ref, gpost_ref):
    mix = _dot(_bf(o_main), wo1_ref[...]) + _dot(_bf(o_mem), wo2_ref[...])
    return x + _rms(mix, gpost_ref[...])


def _make_gla_chunk(q_s, k_s, g_s, v_s, o_s, state_s, p_s):
    C = GLA_CHUNK

    ri = _iota((C, GLA_KW), 0)
    kh = _gla_k_head(_iota((C, GLA_KW), 1))
    khcat = jnp.concatenate([kh, kh, kh], axis=1)
    vh = _gla_v_head(_iota((C, GLA_VW), 1))
    tri = _bf(jnp.where(_iota((C, C), 0) >= _iota((C, C), 1), 1.0, 0.0))
    d_rs = _iota((C, 4 * C), 0) - (_iota((C, 4 * C), 1) & (C - 1))
    band = jnp.where((d_rs >= 0) & (d_rs <= (_iota((C, 4 * C), 0) & (GLA_SUB - 1))), d_rs, -1)
    ebc = _bf(jnp.where(_gla_k_head(_iota((GLA_KW, 4 * C), 0)) == (_iota((GLA_KW, 4 * C), 1) >> 6),
                        1.0, 0.0))
    blockmask = _gla_v_head(_iota((GLA_VW, GLA_KW), 0)) == _gla_k_head(_iota((GLA_VW, GLA_KW), 1))

    def chunk(c, carry):
        r0 = pl.multiple_of(c * C, C)
        qc = q_s[pl.ds(r0, C), :]
        kc = k_s[pl.ds(r0, C), :]
        gc = g_s[pl.ds(r0, C), :]
        vc = v_s[pl.ds(r0, C), :]
        b = _exact_dot(tri, gc)

        ref1 = jnp.broadcast_to(b[31:32, :], b.shape)
        ref2 = jnp.where(ri < 32, jnp.broadcast_to(b[15:16, :], b.shape),
                         jnp.broadcast_to(b[47:48, :], b.shape))
        q1 = qc * jnp.exp(jnp.minimum(b - ref1, 0.0))
        k1 = kc * jnp.exp(jnp.minimum(ref1 - b, 0.0))
        q2 = qc * jnp.exp(jnp.minimum(b - ref2, 0.0))
        k2 = kc * jnp.exp(jnp.minimum(ref2 - b, 0.0))
        qcat = jnp.concatenate([
            jnp.where(ri >= 32, q1, 0.0),
            jnp.where((ri >= 16) & (ri < 32), q2, 0.0),
            jnp.where(ri >= 48, q2, 0.0)], axis=1)
        kcat = jnp.concatenate([
            jnp.where(ri < 32, k1, 0.0),
            jnp.where(ri < 16, k2, 0.0),
            jnp.where((ri >= 32) & (ri < 48), k2, 0.0)], axis=1)
        kst = _bf(jnp.concatenate([jnp.where(khcat == h, kcat, 0.0) for h in range(GLA_H)], axis=0))
        a_off = _dot_nt(_bf(qcat), kst)

        for dlt in range(GLA_SUB):
            if dlt == 0:
                pr = qc * kc
            else:
                kd = pltpu.roll(kc, dlt, 0)
                bd = pltpu.roll(b, dlt, 0)
                pr = qc * kd * jnp.exp(jnp.minimum(b - bd, 0.0))
            p_s[dlt * C:(dlt + 1) * C, :] = _bf(pr)
        rsum = _dot(p_s[...], ebc)
        a_diag = jnp.zeros((C, 4 * C), F32)
        for dlt in range(GLA_SUB):
            a_diag = jnp.where(band == dlt, rsum[dlt * C:(dlt + 1) * C, :], a_diag)

        vst = _bf(jnp.concatenate([jnp.where(vh == h, vc, 0.0) for h in range(GLA_H)], axis=0))
        o_intra = _dot(_bf(a_off + a_diag), vst)
        st = state_s[...]
        o_inter = _dot_nt(_bf(qc * jnp.exp(b)), _bf(st))
        o_s[pl.ds(r0, C), :] = o_intra + o_inter

        blast = jnp.broadcast_to(b[C - 1:C, :], b.shape)
        kdec = kc * jnp.exp(blast - b)
        kvt = _dot_tn(_bf(vc), _bf(kdec))
        decay = jnp.exp(b[C - 1:C, :])
        state_s[...] = st * decay + jnp.where(blockmask, kvt, 0.0)
        return carry

    return chunk


def _make_gla_chunk_bounded(q_s, k_s, g_s, v_s, o_s, state_s):
    C = GLA_CHUNK
    kh = _gla_k_head(_iota((C, GLA_KW), 1))
    vh = _gla_v_head(_iota((C, GLA_VW), 1))
    tri = _bf(jnp.where(_iota((C, C), 0) >= _iota((C, C), 1), 1.0, 0.0))
    causal = _iota((C, 4 * C), 0) >= (_iota((C, 4 * C), 1) & (C - 1))
    blockmask = _gla_v_head(_iota((GLA_VW, GLA_KW), 0)) == _gla_k_head(_iota((GLA_VW, GLA_KW), 1))

    def chunk(c, carry):
        r0 = pl.multiple_of(c * C, C)
        qc = q_s[pl.ds(r0, C), :]
        kc = k_s[pl.ds(r0, C), :]
        vc = v_s[pl.ds(r0, C), :]
        b = _exact_dot(tri, g_s[pl.ds(r0, C), :])
        qe = _bf(qc * jnp.exp(b))
        ke = kc * jnp.exp(-b)
        kst = _bf(jnp.concatenate([jnp.where(kh == h, ke, 0.0) for h in range(GLA_H)], axis=0))
        a = jnp.where(causal, _dot_nt(qe, kst), 0.0)
        vst = _bf(jnp.concatenate([jnp.where(vh == h, vc, 0.0) for h in range(GLA_H)], axis=0))
        st = state_s[...]
        o_s[pl.ds(r0, C), :] = _dot(_bf(a), vst) + _dot_nt(qe, _bf(st))
        decay = jnp.exp(b[C - 1:C, :])
        kvt = _dot_tn(_bf(vc), _bf(ke * decay))
        state_s[...] = st * decay + jnp.where(blockmask, kvt, 0.0)
        return carry

    return chunk


def _mixer_a_kernel(x_ref, gpre_ref, wqk_ref, wv_ref, wr_ref, wmisc_ref, wg_ref, bg_ref, glan_ref,
                    kbd_ref, vbd_ref, wo1_ref, wo2_ref, gpost_ref,
                    xo_ref, st_ref,
                    q_s, k_s, g_s, v_s, o_s, state_s, p_s, gate_s, om_s):
    C = GLA_CHUNK
    tm = x_ref.shape[1]
    nchunk = tm // C

    @pl.when(pl.program_id(1) == 0)
    def _():
        state_s[...] = jnp.zeros_like(state_s)

    q, k, g, v, gate, o_mem = _proj_a(x_ref[0], gpre_ref, wqk_ref, wv_ref, wr_ref, wmisc_ref, wg_ref, bg_ref,
                                      glan_ref, (kbd_ref.at[0], vbd_ref.at[0]))
    q_s[...] = q
    k_s[...] = k
    g_s[...] = g
    v_s[...] = v
    gate_s[...] = gate
    om_s[...] = _bf(o_mem)

    total = jnp.sum(g.reshape(nchunk, C, GLA_KW), axis=1)
    bounded = jnp.min(total) > -GLA_SAFE_DECAY

    @pl.when(bounded)
    def _():
        lax.fori_loop(0, nchunk, _make_gla_chunk_bounded(q_s, k_s, g_s, v_s, o_s, state_s), 0, unroll=4)

    @pl.when(jnp.logical_not(bounded))
    def _():
        lax.fori_loop(0, nchunk, _make_gla_chunk(q_s, k_s, g_s, v_s, o_s, state_s, p_s), 0)

    st_ref[0] = state_s[...]
    o_main = _gla_out_gate(o_s[...], gate_s[...])
    xo_ref[0] = _mix_residual(x_ref[0], o_main, om_s[...], wo1_ref, wo2_ref, gpost_ref)


def _const_spec(a):
    nd = a.ndim
    return pl.BlockSpec(a.shape, lambda *_: (0,) * nd, pipeline_mode=pl.Buffered(1))


def _mixer_a_call(x, wa, kbd, vbd):
    nb, seq, _ = x.shape
    tm = TM_PROMPT
    weights = [wa["gpre"], wa["wqk"], wa["wv"], wa["wr"], wa["wmisc"], wa["wg"], wa["bg"], wa["glan"]]
    tail = [wa["wo1"], wa["wo2"], wa["gpost"]]
    return pl.pallas_call(
        _mixer_a_kernel,
        grid=(nb, seq // tm),
        in_specs=([pl.BlockSpec((1, tm, D), lambda b, t: (b, t, 0))]
                  + [_const_spec(w) for w in weights]
                  + [pl.BlockSpec((1, MEM_W, MEM_H * N_MEM), lambda b, t: (b, 0, 0)),
                     pl.BlockSpec((1, MEM_H * N_MEM, MEM_W), lambda b, t: (b, 0, 0))]
                  + [_const_spec(w) for w in tail]),
        out_specs=[pl.BlockSpec((1, tm, D), lambda b, t: (b, t, 0)),
                   pl.BlockSpec((1, GLA_VW, GLA_KW), lambda b, t: (b, 0, 0))],
        out_shape=[jax.ShapeDtypeStruct((nb, seq, D), F32),
                   jax.ShapeDtypeStruct((nb, GLA_VW, GLA_KW), F32)],
        scratch_shapes=[
            pltpu.VMEM((tm, GLA_KW), F32), pltpu.VMEM((tm, GLA_KW), F32), pltpu.VMEM((tm, GLA_KW), F32),
            pltpu.VMEM((tm, GLA_VW), F32), pltpu.VMEM((tm, GLA_VW), F32),
            pltpu.VMEM((GLA_VW, GLA_KW), F32),
            pltpu.VMEM((GLA_SUB * GLA_CHUNK, GLA_KW), BF16),
            pltpu.VMEM((tm, GLA_VW), F32), pltpu.VMEM((tm, MEM_W), BF16),
        ],
        compiler_params=_params(("arbitrary", "arbitrary")),
        name="mixer_a_prompt",
    )(x, *weights, kbd, vbd, *tail)


def _ffn_kernel(x_ref, gpre_ref, wup_ref, wdn_ref, gpost_ref, o_ref, acc_s):
    x = x_ref[...]
    h = _bf(_rms(x, gpre_ref[...]))
    for j in range(D_FF // FF_CHUNK):
        u = _dot(h, wup_ref[:, j * FF_CHUNK:(j + 1) * FF_CHUNK])
        u = jnp.maximum(u, 0.0)
        d = _dot(_bf(u * u), wdn_ref[j * FF_CHUNK:(j + 1) * FF_CHUNK, :])
        if j == 0:
            acc_s[...] = d
        else:
            acc_s[...] += d
    o_ref[...] = x + _rms(acc_s[...], gpost_ref[...])


def _ffn_call(x2d, wf, tm):
    n = x2d.shape[0]
    ws = [wf["gpre"], wf["wup"], wf["wdn"], wf["gpost"]]
    return pl.pallas_call(
        _ffn_kernel,
        grid=(n // tm,),
        in_specs=[pl.BlockSpec((tm, D), lambda i: (i, 0))] + [_full_spec(w) for w in ws],
        out_specs=pl.BlockSpec((tm, D), lambda i: (i, 0)),
        out_shape=jax.ShapeDtypeStruct((n, D), F32),
        scratch_shapes=[pltpu.VMEM((tm, D), F32)],
        compiler_params=_params(("arbitrary",)),
        name="ffn",
    )(x2d, *ws)


def _t5_bucket(dist):
    max_exact = N_BUCKETS // 2
    n = np.maximum(dist, 0)
    nf = np.maximum(n, 1).astype(np.float32)
    large = max_exact + (np.log(nf / np.float32(max_exact)) / np.float32(math.log(MAX_DISTANCE / max_exact))
                         * np.float32(N_BUCKETS - max_exact)).astype(np.int32)
    large = np.minimum(large, N_BUCKETS - 1)
    return np.where(n < max_exact, n, large)


def _swa_bias_tables():
    qi = np.arange(WINDOW)[:, None] + WINDOW
    kj = np.arange(2 * WINDOW)[None, :]
    dist = qi - kj
    valid = (dist >= 0) & (dist < WINDOW)
    return np.where(valid, _t5_bucket(dist), -1).astype(np.int32)


def _mixer_b_kernel(rb_ref, sink_ref,
                    x_ref, bkt_ref, gkv_ref, wkv_ref, gpre_ref, wq_ref, wqm_ref,
                    kbd_ref, vbd_ref, wo1_ref, wo2_ref, gpost_ref,
                    xo_ref, kc_ref, vc_ref,
                    kbuf, vbuf, q_s, o_s, bias_s, qm_s):
    W = WINDOW
    tm = x_ref.shape[1]
    bb = pl.program_id(0)
    t = pl.program_id(1)
    nheads = SWA_G * SWA_KVH

    @pl.when((bb == 0) & (t == 0))
    def _():
        bkt = bkt_ref[...]
        own = _iota((W, 2 * W), 1) >= W
        for i in range(nheads):
            def add_bucket(n, acc):
                return jnp.where(bkt == n, rb_ref[n, i], acc)
            tab = lax.fori_loop(0, N_BUCKETS, add_bucket, jnp.zeros((W, 2 * W), F32))
            tab = jnp.where(bkt < 0, NEG_INF, tab)
            bias_s[0, i] = tab
            bias_s[1, i] = jnp.where(own, tab, NEG_INF)

    @pl.when(t == 0)
    def _():
        kbuf[0:W, :] = jnp.zeros((W, SWA_KVW), F32)
        vbuf[0:W, :] = jnp.zeros((W, SWA_KVW), F32)

    x = x_ref[0]
    xn = x * lax.rsqrt(jnp.mean(x * x, axis=-1, keepdims=True) + EPS)
    kv = _dot(_bf(xn * gkv_ref[...]), wkv_ref[...])
    kbuf[W:W + tm, :] = kv[:, :SWA_KVW]
    vbuf[W:W + tm, :] = kv[:, SWA_KVW:]
    kc_ref[0] = kv[tm - W:, :SWA_KVW]
    vc_ref[0] = kv[tm - W:, SWA_KVW:]

    h = _bf(xn * gpre_ref[...])
    q_s[...] = _dot(h, wq_ref[...]) * (SWA_HD ** -0.5)
    qm_s[...] = _dot(h, wqm_ref[...])

    lane_head = _iota((W, SWA_KVW), 1) >> 6
    key_head = _iota((2 * W, SWA_KVW), 1) >> 6

    def block(j, carry):
        r0 = pl.multiple_of(j * W, W)
        kb = kbuf[pl.ds(r0, 2 * W), :]
        vb = vbuf[pl.ds(r0, 2 * W), :]
        tab = jnp.where((j == 0) & (t == 0), 1, 0)
        qrows = []
        for gi in range(SWA_G):
            qg = q_s[pl.ds(r0, W), gi * SWA_KVW:(gi + 1) * SWA_KVW]
            for hh in range(SWA_KVH):
                qrows.append(jnp.where(lane_head == hh, qg, 0.0))
        s_all = _dot_nt(_bf(jnp.concatenate(qrows, axis=0)), _bf(kb))
        vst = _bf(jnp.concatenate([jnp.where(key_head == hh, vb, 0.0) for hh in range(SWA_KVH)], axis=0))
        for gi in range(SWA_G):
            ps = []
            for hh in range(SWA_KVH):
                i = gi * SWA_KVH + hh
                s = s_all[i * W:(i + 1) * W, :] + bias_s[tab, i]
                sink = sink_ref[i]
                mx = jnp.maximum(jnp.max(s, axis=-1, keepdims=True), sink)
                p = jnp.exp(s - mx)
                p = p / (jnp.sum(p, axis=-1, keepdims=True) + jnp.exp(sink - mx))
                ps.append(_bf(p))
            o_s[pl.ds(r0, W), gi * SWA_KVW:(gi + 1) * SWA_KVW] = _dot(jnp.concatenate(ps, axis=1), vst)
        return carry

    lax.fori_loop(0, tm // W, block, 0, unroll=2)

    kbuf[0:W, :] = kbuf[tm:tm + W, :]
    vbuf[0:W, :] = vbuf[tm:tm + W, :]

    o_mem = _mem_pv(_mem_probs(qm_s[...], kbd_ref.at[0]), vbd_ref.at[0])
    xo_ref[0] = _mix_residual(x_ref[0], o_s[...], o_mem, wo1_ref, wo2_ref, gpost_ref)


def _mixer_b_call(x, wb, kbd, vbd):
    nb, seq, _ = x.shape
    tm = TM_PROMPT
    bkt = jnp.asarray(_swa_bias_tables())
    head = [bkt, wb["gkv"], wb["wkv"], wb["gpre"], wb["wq"], wb["wqm"]]
    tail = [wb["wo1"], wb["wo2"], wb["gpost"]]
    smem = pl.BlockSpec(memory_space=pltpu.SMEM)
    return pl.pallas_call(
        _mixer_b_kernel,
        grid=(nb, seq // tm),
        in_specs=([smem, smem, pl.BlockSpec((1, tm, D), lambda b, t: (b, t, 0))]
                  + [_full_spec(w) for w in head]
                  + [pl.BlockSpec((1, MEM_W, MEM_H * N_MEM), lambda b, t: (b, 0, 0)),
                     pl.BlockSpec((1, MEM_H * N_MEM, MEM_W), lambda b, t: (b, 0, 0))]
                  + [_full_spec(w) for w in tail]),
        out_specs=[pl.BlockSpec((1, tm, D), lambda b, t: (b, t, 0)),
                   pl.BlockSpec((1, WINDOW, SWA_KVW), lambda b, t: (b, 0, 0)),
                   pl.BlockSpec((1, WINDOW, SWA_KVW), lambda b, t: (b, 0, 0))],
        out_shape=[jax.ShapeDtypeStruct((nb, seq, D), F32),
                   jax.ShapeDtypeStruct((nb, WINDOW, SWA_KVW), F32),
                   jax.ShapeDtypeStruct((nb, WINDOW, SWA_KVW), F32)],
        scratch_shapes=[
            pltpu.VMEM((tm + WINDOW, SWA_KVW), F32), pltpu.VMEM((tm + WINDOW, SWA_KVW), F32),
            pltpu.VMEM((tm, SWA_QW), F32), pltpu.VMEM((tm, SWA_QW), F32),
            pltpu.VMEM((2, SWA_G * SWA_KVH, WINDOW, 2 * WINDOW), F32),
            pltpu.VMEM((tm, MEM_W), F32),
        ],
        compiler_params=_params(("arbitrary", "arbitrary")),
        name="mixer_b_prompt",
    )(wb["rb"], wb["sinks"], x, *head, kbd, vbd, *tail)


def _row_to_col(row, eye):
    return jnp.sum(jnp.where(eye, jnp.broadcast_to(row, eye.shape), 0.0), axis=1, keepdims=True)


def _eye(n):
    return _iota((n, n), 0) == _iota((n, n), 1)


def _pre_a_kernel(x_ref, gpre_ref, wqk_ref, wv_ref, wr_ref, wmisc_ref, wg_ref, bg_ref, glan_ref,
                  q_ref, k_ref, g_ref, v_ref, gate_ref, qm_ref):
    q, k, g, v, gate, qm = _proj_a(x_ref[...], gpre_ref, wqk_ref, wv_ref, wr_ref, wmisc_ref, wg_ref, bg_ref,
                                   glan_ref)
    q_ref[...] = q
    k_ref[...] = k
    g_ref[...] = g
    v_ref[...] = v
    gate_ref[...] = gate
    qm_ref[...] = qm


def _pre_a_call(x2d, wa):
    n = x2d.shape[0]
    ws = [wa["gpre"], wa["wqk"], wa["wv"], wa["wr"], wa["wmisc"], wa["wg"], wa["bg"], wa["glan"]]
    shapes = [(n, GLA_KW), (n, GLA_KW), (n, GLA_KW), (n, GLA_VW), (n, GLA_VW), (n, MEM_W)]
    return pl.pallas_call(
        _pre_a_kernel,
        grid=(1,),
        in_specs=[_full_spec(x2d)] + [_full_spec(w) for w in ws],
        out_specs=[pl.BlockSpec(s, lambda i: (0, 0)) for s in shapes],
        out_shape=[jax.ShapeDtypeStruct(s, F32) for s in shapes],
        compiler_params=_params(("arbitrary",)),
        name="pre_a_sample",
    )(x2d, *ws)


def _gla_step_kernel(qt_ref, kt_ref, gt_ref, vt_ref, s_ref, so_ref, ot_ref):
    c = pl.program_id(1)
    vt = vt_ref[...]
    acc = jnp.zeros_like(vt)
    for d in range(GLA_DK_BLOCK):
        s_new = jnp.exp(gt_ref[d:d + 1, :]) * s_ref[0, 0, d] + kt_ref[d:d + 1, :] * vt
        so_ref[0, 0, d] = s_new
        acc = acc + qt_ref[d:d + 1, :] * s_new

    @pl.when(c == 0)
    def _():
        ot_ref[...] = acc

    @pl.when(c > 0)
    def _():
        ot_ref[...] += acc


def _gla_step_call(qt, kt, gt, vt, state5):
    n = qt.shape[1]
    nblk = GLA_DK // GLA_DK_BLOCK
    rows = pl.BlockSpec((GLA_DK_BLOCK, n), lambda h, c: (h * nblk + c, 0))
    head = pl.BlockSpec((GLA_DV, n), lambda h, c: (h, 0))
    st = pl.BlockSpec((1, 1, GLA_DK_BLOCK, GLA_DV, n), lambda h, c: (0, h, c, 0, 0))
    return pl.pallas_call(
        _gla_step_kernel,
        grid=(GLA_H, nblk),
        in_specs=[rows, rows, rows, head, st],
        out_specs=[st, head],
        out_shape=[jax.ShapeDtypeStruct(state5.shape, F32), jax.ShapeDtypeStruct((GLA_VW, n), F32)],
        compiler_params=_params(("arbitrary", "arbitrary")),
        name="gla_step_sample",
    )(qt, kt, gt, vt, state5)


def _mem_step_kernel(qm_ref, mk_ref, mv_ref, o_ref):
    own = (_iota((8, MEM_W), 1) >> 6) == _iota((8, MEM_W), 0)
    for i in range(qm_ref.shape[0]):
        q8 = jnp.where(own, jnp.broadcast_to(qm_ref[i], (8, MEM_W)), 0.0)
        s = _dot(_bf(q8), _bf(mk_ref[0, i].reshape(MEM_W, N_MEM))) * (MEM_HD ** -0.5)
        e = jnp.exp(s - jnp.max(s, axis=1, keepdims=True))
        p = e / jnp.sum(e, axis=1, keepdims=True)
        res = _dot_nt(_bf(p), _bf(mv_ref[0, i].reshape(MEM_W, N_MEM)))
        o_ref[i] = jnp.sum(jnp.where(own, res, 0.0), axis=0, keepdims=True)


def _mem_step_call(qm, mk5, mv5, layer):
    n = qm.shape[0]
    rb = MEM_ROWS_PER_STEP
    blk = pl.BlockSpec((1, rb, MEM_H, MEM_HD, N_MEM), lambda i: (layer, i, 0, 0, 0))
    rows = pl.BlockSpec((rb, 1, MEM_W), lambda i: (i, 0, 0))
    return pl.pallas_call(
        _mem_step_kernel,
        grid=(n // rb,),
        in_specs=[rows, blk, blk],
        out_specs=rows,
        out_shape=jax.ShapeDtypeStruct((n, 1, MEM_W), F32),
        compiler_params=_params(("arbitrary",)),
        name="mem_step_sample",
    )(qm.reshape(n, 1, MEM_W), mk5, mv5).reshape(n, MEM_W)


def _post_a_kernel(x_ref, o_ref, gate_ref, om_ref, wo1_ref, wo2_ref, gpost_ref, xo_ref):
    o_main = _gla_out_gate(o_ref[...], gate_ref[...])
    xo_ref[...] = _mix_residual(x_ref[...], o_main, om_ref[...], wo1_ref, wo2_ref, gpost_ref)


def _post_a_call(x2d, o, gate, om, wa):
    args = [x2d, o, gate, om, wa["wo1"], wa["wo2"], wa["gpost"]]
    return pl.pallas_call(
        _post_a_kernel,
        grid=(1,),
        in_specs=[_full_spec(a) for a in args],
        out_specs=_full_spec(x2d),
        out_shape=jax.ShapeDtypeStruct(x2d.shape, F32),
        compiler_params=_params(("arbitrary",)),
        name="post_a_sample",
    )(*args)


def _pre_b_kernel(x_ref, gkv_ref, wkv_ref, gpre_ref, wq_ref, wqm_ref, ks_ref, vs_ref, q_ref, qm_ref):
    x = x_ref[...]
    kv = _dot(_bf(_rms(x, gkv_ref[...])), wkv_ref[...])
    ks_ref[...] = kv[:, :SWA_KVW]
    vs_ref[...] = kv[:, SWA_KVW:]
    h = _bf(_rms(x, gpre_ref[...]))
    q_ref[...] = _dot(h, wq_ref[...])
    qm_ref[...] = _dot(h, wqm_ref[...])


def _pre_b_call(x2d, wb):
    n = x2d.shape[0]
    ws = [wb["gkv"], wb["wkv"], wb["gpre"], wb["wq"], wb["wqm"]]
    widths = [SWA_KVW, SWA_KVW, SWA_QW, MEM_W]
    return pl.pallas_call(
        _pre_b_kernel,
        grid=(1,),
        in_specs=[_full_spec(x2d)] + [_full_spec(w) for w in ws],
        out_specs=[pl.BlockSpec((n, w), lambda i: (0, 0)) for w in widths],
        out_shape=[jax.ShapeDtypeStruct((n, w), F32) for w in widths],
        compiler_params=_params(("arbitrary",)),
        name="pre_b_sample",
    )(x2d, *ws)


def _sample_buckets():
    dist = (WINDOW - 1) - np.arange(WINDOW)
    return _t5_bucket(dist).astype(np.int32).reshape(1, WINDOW)


def _swa_step_kernel(sink_ref, rb_ref, bkt_ref, kc_ref, vc_ref, kst_ref, vst_ref, q_ref,
                     kn_ref, vn_ref, o_ref, bias_s, sink_s, s_s, p_s):
    W = WINDOW
    R = SWA_KVH * 8

    @pl.when(pl.program_id(0) == 0)
    def _():
        bkt = bkt_ref[...]
        rid = _iota((R, W), 0)
        bias = jnp.zeros((R, W), F32)
        sink = jnp.zeros((R, W), F32)
        for h in range(SWA_KVH):
            for g in range(SWA_G):
                idx = g * SWA_KVH + h
                def add_bucket(n, acc):
                    return jnp.where(bkt == n, rb_ref[n, idx], acc)
                brow = lax.fori_loop(0, N_BUCKETS, add_bucket, jnp.zeros((1, W), F32))
                bias = jnp.where(rid == h * 8 + g, brow, bias)
                sink = jnp.where(rid == h * 8 + g, sink_ref[idx], sink)
        bias_s[...] = bias
        sink_s[...] = sink

    rb = q_ref.shape[0]
    base = pl.program_id(0) * rb
    last = _iota((SWA_KVW, W), 1) == W - 1
    own = (_iota((R, SWA_KVW), 1) >> 6) == (_iota((R, SWA_KVW), 0) >> 3)
    kst = kst_ref[...]
    vst = vst_ref[...]
    for i in range(rb):
        shift = W - 1 - (base + i)
        kn = jnp.where(last, pltpu.roll(kst, shift, 1), pltpu.roll(kc_ref[i].reshape(SWA_KVW, W), W - 1, 1))
        vn = jnp.where(last, pltpu.roll(vst, shift, 1), pltpu.roll(vc_ref[i].reshape(SWA_KVW, W), W - 1, 1))
        kn_ref[i] = kn.reshape(SWA_KVH, SWA_HD, W)
        vn_ref[i] = vn.reshape(SWA_KVH, SWA_HD, W)
        q32 = jnp.where(own, jnp.concatenate([q_ref[i]] * SWA_KVH, axis=0), 0.0)
        s_s[i * R:(i + 1) * R, :] = _dot(_bf(q32), _bf(kn))
    s = s_s[...] * (SWA_HD ** -0.5) + jnp.concatenate([bias_s[...]] * rb, axis=0)
    sink = jnp.concatenate([sink_s[...]] * rb, axis=0)
    mx = jnp.maximum(jnp.max(s, axis=1, keepdims=True), sink)
    p = jnp.exp(s - mx)
    p_s[...] = _bf(p / (jnp.sum(p, axis=1, keepdims=True) + jnp.exp(sink - mx)))
    for i in range(rb):
        vn = vn_ref[i].reshape(SWA_KVW, W)
        res = jnp.where(own, _dot_nt(p_s[i * R:(i + 1) * R, :], _bf(vn)), 0.0)
        o_ref[i] = res[0:8] + res[8:16] + res[16:24] + res[24:32]


def _swa_step_call(kc4, vc4, ks, vs, q, wb):
    n = q.shape[0]
    rb = SWA_ROWS_PER_STEP
    bkt = jnp.asarray(_sample_buckets())
    q8 = jnp.pad(q.reshape(n, SWA_G, SWA_KVW), ((0, 0), (0, 8 - SWA_G), (0, 0)))
    row3 = lambda r, w: pl.BlockSpec((rb, r, w), lambda i: (i, 0, 0))
    cache = pl.BlockSpec((rb, SWA_KVH, SWA_HD, WINDOW), lambda i: (i, 0, 0, 0))
    smem = pl.BlockSpec(memory_space=pltpu.SMEM)
    kn4, vn4, o8 = pl.pallas_call(
        _swa_step_kernel,
        grid=(n // rb,),
        in_specs=[smem, smem, _full_spec(bkt),
                  cache, cache, pl.BlockSpec((SWA_KVW, n), lambda i: (0, 0)),
                  pl.BlockSpec((SWA_KVW, n), lambda i: (0, 0)), row3(8, SWA_KVW)],
        out_specs=[cache, cache, row3(8, SWA_KVW)],
        out_shape=[jax.ShapeDtypeStruct(kc4.shape, F32), jax.ShapeDtypeStruct(vc4.shape, F32),
                   jax.ShapeDtypeStruct((n, 8, SWA_KVW), F32)],
        scratch_shapes=[pltpu.VMEM((SWA_KVH * 8, WINDOW), F32), pltpu.VMEM((SWA_KVH * 8, WINDOW), F32),
                        pltpu.VMEM((rb * SWA_KVH * 8, WINDOW), F32), pltpu.VMEM((rb * SWA_KVH * 8, WINDOW), BF16)],
        compiler_params=_params(("arbitrary",)),
        name="swa_step_sample",
    )(wb["sinks"], wb["rb"], bkt, kc4, vc4, ks.T, vs.T, q8)
    return kn4, vn4, o8[:, :SWA_G].reshape(n, SWA_QW)


def _post_b_kernel(x_ref, o_ref, om_ref, wo1_ref, wo2_ref, gpost_ref, xo_ref):
    xo_ref[...] = _mix_residual(x_ref[...], o_ref[...], om_ref[...], wo1_ref, wo2_ref, gpost_ref)


def _post_b_call(x2d, o, om, wb):
    args = [x2d, o, om, wb["wo1"], wb["wo2"], wb["gpost"]]
    return pl.pallas_call(
        _post_b_kernel,
        grid=(1,),
        in_specs=[_full_spec(a) for a in args],
        out_specs=_full_spec(x2d),
        out_shape=jax.ShapeDtypeStruct(x2d.shape, F32),
        compiler_params=_params(("arbitrary",)),
        name="post_b_sample",
    )(*args)


def _prep_weights(norm_mix_pre, norm_mix_post, norm_ffn_pre, norm_ffn_post, w_in_a, w_gate_up, b_gate,
                  gla_norm, w_in_b, sinks, norm_kv, w_kv, rel_bias, w_out, w_ffn_up, w_ffn_down):
    row = lambda g: g.reshape(1, -1)
    wa_in = w_in_a[0]
    c_v = 2 * GLA_KW
    c_r = c_v + GLA_VW
    c_g = c_r + GLA_VW
    c_m = c_g + GATE_RANK
    wmisc = jnp.concatenate([wa_in[:, c_g:c_m], jnp.zeros((D, 128 - GATE_RANK), F32), wa_in[:, c_m:]], axis=1)
    wg = jnp.zeros((128, GLA_KW), F32).at[:GATE_RANK].set(w_gate_up[0])
    wa = dict(
        gpre=row(norm_mix_pre[0]), wqk=_bf(wa_in[:, :c_v]), wv=_bf(wa_in[:, c_v:c_r]), wr=_bf(wa_in[:, c_r:c_g]),
        wmisc=_bf(wmisc), wg=_bf(wg), bg=row(b_gate[0]), glan=row(jnp.tile(gla_norm[0], GLA_H)),
        wo1=_bf(w_out[0][:GLA_VW]), wo2=_bf(w_out[0][GLA_VW:]), gpost=row(norm_mix_post[0]))
    wb_in = w_in_b[0]
    wq = wb_in[:, :SWA_QW].reshape(D, SWA_KVH, SWA_G, SWA_HD).transpose(0, 2, 1, 3).reshape(D, SWA_QW)
    wo1 = w_out[1][:SWA_QW].reshape(SWA_KVH, SWA_G, SWA_HD, D).transpose(1, 0, 2, 3).reshape(SWA_QW, D)
    rb = rel_bias.reshape(N_BUCKETS, SWA_KVH, SWA_G).transpose(0, 2, 1).reshape(N_BUCKETS, SWA_G * SWA_KVH)
    sk = sinks[0].reshape(SWA_KVH, SWA_G).T.reshape(SWA_G * SWA_KVH)
    wb = dict(
        gkv=row(norm_kv), wkv=_bf(w_kv), gpre=row(norm_mix_pre[1]), wq=_bf(wq), wqm=_bf(wb_in[:, SWA_QW:]),
        wo1=_bf(wo1), wo2=_bf(w_out[1][SWA_QW:]), gpost=row(norm_mix_post[1]), rb=rb, sinks=sk)
    ffn = [dict(gpre=row(norm_ffn_pre[l]), wup=_bf(w_ffn_up[l]), wdn=_bf(w_ffn_down[l]),
                gpost=row(norm_ffn_post[l])) for l in range(2)]
    return wa, wb, ffn


def kernel(x_prompt, x_sample, state_gla, cache_swa_k, cache_swa_v, cache_mem_k, cache_mem_v, mem_prompt,
           norm_mix_pre, norm_mix_post, norm_ffn_pre, norm_ffn_post, norm_mem, w_mem_kv, w_in_a, w_gate_up,
           b_gate, gla_norm, w_in_b, sinks, norm_kv, w_kv, rel_bias, w_out, w_ffn_up, w_ffn_down):
    wa, wb, ffn = _prep_weights(norm_mix_pre, norm_mix_post, norm_ffn_pre, norm_ffn_post, w_in_a, w_gate_up,
                                b_gate, gla_norm, w_in_b, sinks, norm_kv, w_kv, rel_bias, w_out, w_ffn_up,
                                w_ffn_down)
    nb, seq, _ = x_prompt.shape
    ns = x_sample.shape[0]

    mkt, mvt, kbd, vbd = _memkv_call(mem_prompt, norm_mem, w_mem_kv)
    x1, st = _mixer_a_call(x_prompt, wa, kbd[0], vbd[0])
    x2 = _ffn_call(x1.reshape(nb * seq, D), ffn[0], TM_PROMPT).reshape(nb, seq, D)
    x3, kc, vc = _mixer_b_call(x2, wb, kbd[1], vbd[1])
    y_prompt = _ffn_call(x3.reshape(nb * seq, D), ffn[1], TM_PROMPT).reshape(nb, seq, D)
    st4 = st.reshape(nb, GLA_H, GLA_DV, GLA_H, GLA_DK)
    state_prompt = jnp.stack([st4[:, h, :, h, :] for h in range(GLA_H)], axis=1).transpose(0, 1, 3, 2)[None]
    to_mem = lambda t: t.reshape(2, nb, MEM_H, MEM_HD, N_MEM).transpose(0, 1, 4, 2, 3)
    swa_shape = (nb, WINDOW, SWA_KVH, SWA_HD)

    xs = x_sample.reshape(ns, D)
    state5 = jnp.transpose(state_gla, (0, 2, 3, 4, 1))
    mk5 = jnp.transpose(cache_mem_k, (0, 1, 3, 4, 2))
    mv5 = jnp.transpose(cache_mem_v, (0, 1, 3, 4, 2))
    kc4 = jnp.transpose(cache_swa_k, (0, 2, 3, 1))
    vc4 = jnp.transpose(cache_swa_v, (0, 2, 3, 1))

    q, k, g, v, r, qm = _pre_a_call(xs, wa)
    state5_new, ot = _gla_step_call(q.T, k.T, g.T, v.T, state5)
    om = _mem_step_call(qm, mk5, mv5, 0)
    xs1 = _post_a_call(xs, ot.T, r, om, wa)
    xs2 = _ffn_call(xs1, ffn[0], ns)
    ks, vs, qb, qmb = _pre_b_call(xs2, wb)
    kn4, vn4, ob = _swa_step_call(kc4, vc4, ks, vs, qb, wb)
    omb = _mem_step_call(qmb, mk5, mv5, 1)
    xs3 = _post_b_call(xs2, ob, omb, wb)
    y_sample = _ffn_call(xs3, ffn[1], ns).reshape(ns, 1, D)

    return (y_prompt, y_sample, state_prompt,
            jnp.transpose(state5_new, (0, 4, 1, 2, 3)),
            kc.reshape(swa_shape), vc.reshape(swa_shape),
            jnp.transpose(kn4, (0, 3, 1, 2)), jnp.transpose(vn4, (0, 3, 1, 2)),
            to_mem(mkt), to_mem(mvt))
```

```python
import functools
import math

import numpy as np
import jax
import jax.numpy as jnp
from jax import lax
from jax.experimental import pallas as pl
from jax.experimental.pallas import tpu as pltpu

F32 = jnp.float32
BF16 = jnp.bfloat16

D = 1024
D_FF = 4 * D
N_MEM = 256
MEM_H = 4
MEM_HD = 64
MEM_W = MEM_H * MEM_HD
GLA_H = 4
GLA_DK = 96
GLA_DV = 192
GLA_KW = GLA_H * GLA_DK
GLA_VW = GLA_H * GLA_DV
GATE_RANK = 16
GATE_NORM = 16.0
SWA_HD = 64
SWA_KVH = 4
SWA_G = 3
SWA_QW = SWA_KVH * SWA_G * SWA_HD
SWA_KVW = SWA_KVH * SWA_HD
WINDOW = 128
N_BUCKETS = 32
MAX_DISTANCE = 128
EPS = 1e-6

GLA_CHUNK = 64
GLA_SUB = 16
GLA_SAFE_DECAY = 60.0
TM_PROMPT = 512
FF_CHUNK = 512
GLA_DK_BLOCK = 16
V7X_VMEM_LIMIT = 56 * 1024 * 1024
NEG_INF = float("-inf")


def _bf(x):
    return x.astype(BF16)


def _dot(a, b):
    return jnp.dot(a, b, preferred_element_type=F32)


def _dot_nt(a, b):
    return lax.dot_general(a, b, (((1,), (1,)), ((), ())), preferred_element_type=F32)


def _dot_tn(a, b):
    return lax.dot_general(a, b, (((0,), (0,)), ((), ())), preferred_element_type=F32)


def _rms(x, g):
    return x * lax.rsqrt(jnp.mean(x * x, axis=-1, keepdims=True) + EPS) * g


def _split3(x):
    x1 = _bf(x)
    r1 = x - x1.astype(F32)
    x2 = _bf(r1)
    x3 = _bf(r1 - x2.astype(F32))
    return x1, x2, x3


def _exact_dot(sel, x):
    x1, x2, x3 = _split3(x)
    return _dot(sel, x1) + _dot(sel, x2) + _dot(sel, x3)


def _log_sigmoid(z):
    return jnp.minimum(z, 0.0) - jnp.log1p(jnp.exp(-jnp.abs(z)))


def _silu(z):
    return z * (1.0 / (1.0 + jnp.exp(-z)))


def _iota(shape, dim):
    return lax.broadcasted_iota(jnp.int32, shape, dim)


def _gla_k_head(lane):
    one = jnp.int32(1)
    zero = jnp.int32(0)
    return (jnp.where(lane >= GLA_DK, one, zero) + jnp.where(lane >= 2 * GLA_DK, one, zero)
            + jnp.where(lane >= 3 * GLA_DK, one, zero))


def _gla_v_head(lane):
    one = jnp.int32(1)
    zero = jnp.int32(0)
    return (jnp.where(lane >= GLA_DV, one, zero) + jnp.where(lane >= 2 * GLA_DV, one, zero)
            + jnp.where(lane >= 3 * GLA_DV, one, zero))


def _full_spec(a):
    nd = a.ndim
    return pl.BlockSpec(a.shape, lambda *_: (0,) * nd)


def _params(sem):
    return pltpu.CompilerParams(dimension_semantics=sem, vmem_limit_bytes=V7X_VMEM_LIMIT)


def _call_with_guests(host_kernel, grid, step_of, inputs, in_specs, out_shape, out_specs, scratch, guests, name):
    def to_spec(s):
        if isinstance(s, pl.BlockSpec):
            return s
        block, fn = s
        return pl.BlockSpec(block, lambda *ids: fn(step_of(*ids)))

    counts = [(len(inputs), len(out_shape), len(scratch))]
    counts += [(len(g["inputs"]), len(g["out_shape"]), len(g["scratch"])) for g in guests]

    def kern(*refs):
        refs = list(refs)
        parts = [[refs.pop(0) for _ in range(c[k])] for k in range(3) for c in counts]
        n = len(counts)
        ins, outs, scrs = parts[:n], parts[n:2 * n], parts[2 * n:]
        step = step_of(*[pl.program_id(a) for a in range(len(grid))])
        grefs = [(*a, *b, *c) for a, b, c in zip(ins[1:], outs[1:], scrs[1:])]
        states = [g["pre"](step, *r) for g, r in zip(guests, grefs)]

        def hook():
            for g, st, r in zip(guests, states, grefs):
                g["post"](step, st, *r)

        host_kernel(*ins[0], *outs[0], *scrs[0], hook=hook)

    res = pl.pallas_call(
        kern,
        grid=grid,
        in_specs=list(in_specs) + [to_spec(s) for g in guests for s in g["in_specs"]],
        out_specs=list(out_specs) + [to_spec(s) for g in guests for s in g["out_specs"]],
        out_shape=list(out_shape) + [s for g in guests for s in g["out_shape"]],
        scratch_shapes=list(scratch) + [s for g in guests for s in g["scratch"]],
        compiler_params=_params(("arbitrary",) * len(grid)),
        name=name,
    )(*inputs, *[a for g in guests for a in g["inputs"]])
    res = list(res)
    split = []
    for c in counts:
        split.append([res.pop(0) for _ in range(c[1])])
    return split[0], split[1:]


def _mem_probs(qm, kbd_ref):
    s = _dot(_bf(qm), kbd_ref[...]) * (MEM_HD ** -0.5)
    ps = []
    for h in range(MEM_H):
        sh = s[:, h * N_MEM:(h + 1) * N_MEM]
        e = jnp.exp(sh - jnp.max(sh, axis=-1, keepdims=True))
        ps.append(_bf(e / jnp.sum(e, axis=-1, keepdims=True)))
    return ps


def _mem_pv(ps, vbd_ref):
    out = None
    for h in range(MEM_H):
        t = _dot(ps[h], vbd_ref[h * N_MEM:(h + 1) * N_MEM, :])
        out = t if out is None else out + t
    return out


def _memkv_kernel(mem_ref, g_ref, w_ref, k_ref, v_ref, kbd_ref, vbd_ref):
    h = _bf(_rms(mem_ref[0], g_ref[0]))
    kv = _dot(h, w_ref[0])
    k = kv[:, :MEM_W]
    v = kv[:, MEM_W:]
    kt = k.T
    k_ref[0, 0] = kt
    v_ref[0, 0] = v.T
    kt4 = jnp.concatenate([kt, kt, kt, kt], axis=1)
    keep_k = (_iota((MEM_W, MEM_H * N_MEM), 0) >> 6) == (_iota((MEM_W, MEM_H * N_MEM), 1) >> 8)
    kbd_ref[0, 0] = _bf(jnp.where(keep_k, kt4, 0.0))
    v4 = jnp.concatenate([v, v, v, v], axis=0)
    keep_v = (_iota((MEM_H * N_MEM, MEM_W), 0) >> 8) == (_iota((MEM_H * N_MEM, MEM_W), 1) >> 6)
    vbd_ref[0, 0] = _bf(jnp.where(keep_v, v4, 0.0))


def _memkv_call(mem, norm_mem, w_mem_kv):
    nb = mem.shape[0]
    nl = w_mem_kv.shape[0]
    g = norm_mem.reshape(nl, 1, D)
    w = _bf(w_mem_kv)
    return pl.pallas_call(
        _memkv_kernel,
        grid=(nl, nb),
        in_specs=[
            pl.BlockSpec((1, N_MEM, D), lambda l, b: (b, 0, 0)),
            pl.BlockSpec((1, 1, D), lambda l, b: (l, 0, 0)),
            pl.BlockSpec((1, D, 2 * MEM_W), lambda l, b: (l, 0, 0)),
        ],
        out_specs=[
            pl.BlockSpec((1, 1, MEM_W, N_MEM), lambda l, b: (l, b, 0, 0)),
            pl.BlockSpec((1, 1, MEM_W, N_MEM), lambda l, b: (l, b, 0, 0)),
            pl.BlockSpec((1, 1, MEM_W, MEM_H * N_MEM), lambda l, b: (l, b, 0, 0)),
            pl.BlockSpec((1, 1, MEM_H * N_MEM, MEM_W), lambda l, b: (l, b, 0, 0)),
        ],
        out_shape=[
            jax.ShapeDtypeStruct((nl, nb, MEM_W, N_MEM), F32),
            jax.ShapeDtypeStruct((nl, nb, MEM_W, N_MEM), F32),
            jax.ShapeDtypeStruct((nl, nb, MEM_W, MEM_H * N_MEM), BF16),
            jax.ShapeDtypeStruct((nl, nb, MEM_H * N_MEM, MEM_W), BF16),
        ],
        compiler_params=_params(("arbitrary", "arbitrary")),
        name="mem_kv",
    )(mem, g, w)


def _proj_a(x, gpre_ref, wqk_ref, wv_ref, wr_ref, wmisc_ref, wg_ref, bg_ref, glan_ref, mem_refs=None):
    h = _bf(_rms(x, gpre_ref[...]))
    misc = _dot(h, wmisc_ref[...])
    glr = misc[:, :128]
    qm = misc[:, 128:]
    ps = None if mem_refs is None else _mem_probs(qm, mem_refs[0])
    g = _log_sigmoid(_dot(_bf(glr), wg_ref[...]) + bg_ref[...]) * (1.0 / GATE_NORM)
    gate = glan_ref[...] * _silu(_dot(h, wr_ref[...]))
    qk = _dot(h, wqk_ref[...])
    q = qk[:, :GLA_KW] * (GLA_DK ** -0.5)
    k = qk[:, GLA_KW:]
    v = _dot(h, wv_ref[...])
    return q, k, g, v, gate, (qm if mem_refs is None else _mem_pv(ps, mem_refs[1]))


def _gla_out_gate(o, gate):
    vh = _gla_v_head(_iota(o.shape, 1))
    o2 = o * o
    scale = jnp.zeros_like(o)
    for h in range(GLA_H):
        ss = jnp.sum(jnp.where(vh == h, o2, 0.0), axis=-1, keepdims=True) * (1.0 / GLA_DV)
        scale = jnp.where(vh == h, lax.rsqrt(ss + EPS), scale)
    return o * scale * gate


def _mix_residual(x, o_main, o_mem, wo1_ref, wo2_ref, gpost_ref):
    mix = _dot(_bf(o_main), wo1_ref[...]) + _dot(_bf(o_mem), wo2_ref[...])
    return x + _rms(mix, gpost_ref[...])


def _make_gla_chunk(q_s, k_s, g_s, v_s, o_s, state_s, p_s):
    C = GLA_CHUNK

    ri = _iota((C, GLA_KW), 0)
    kh = _gla_k_head(_iota((C, GLA_KW), 1))
    khcat = jnp.concatenate([kh, kh, kh], axis=1)
    vh = _gla_v_head(_iota((C, GLA_VW), 1))
    tri = _bf(jnp.where(_iota((C, C), 0) >= _iota((C, C), 1), 1.0, 0.0))
    d_rs = _iota((C, 4 * C), 0) - (_iota((C, 4 * C), 1) & (C - 1))
    band = jnp.where((d_rs >= 0) & (d_rs <= (_iota((C, 4 * C), 0) & (GLA_SUB - 1))), d_rs, -1)
    ebc = _bf(jnp.where(_gla_k_head(_iota((GLA_KW, 4 * C), 0)) == (_iota((GLA_KW, 4 * C), 1) >> 6),
                        1.0, 0.0))
    blockmask = _gla_v_head(_iota((GLA_VW, GLA_KW), 0)) == _gla_k_head(_iota((GLA_VW, GLA_KW), 1))

    def chunk(c, carry):
        r0 = pl.multiple_of(c * C, C)
        qc = q_s[pl.ds(r0, C), :]
        kc = k_s[pl.ds(r0, C), :]
        gc = g_s[pl.ds(r0, C), :]
        vc = v_s[pl.ds(r0, C), :]
        b = _exact_dot(tri, gc)

        ref1 = jnp.broadcast_to(b[31:32, :], b.shape)
        ref2 = jnp.where(ri < 32, jnp.broadcast_to(b[15:16, :], b.shape),
                         jnp.broadcast_to(b[47:48, :], b.shape))
        q1 = qc * jnp.exp(jnp.minimum(b - ref1, 0.0))
        k1 = kc * jnp.exp(jnp.minimum(ref1 - b, 0.0))
        q2 = qc * jnp.exp(jnp.minimum(b - ref2, 0.0))
        k2 = kc * jnp.exp(jnp.minimum(ref2 - b, 0.0))
        qcat = jnp.concatenate([
            jnp.where(ri >= 32, q1, 0.0),
            jnp.where((ri >= 16) & (ri < 32), q2, 0.0),
            jnp.where(ri >= 48, q2, 0.0)], axis=1)
        kcat = jnp.concatenate([
            jnp.where(ri < 32, k1, 0.0),
            jnp.where(ri < 16, k2, 0.0),
            jnp.where((ri >= 32) & (ri < 48), k2, 0.0)], axis=1)
        kst = _bf(jnp.concatenate([jnp.where(khcat == h, kcat, 0.0) for h in range(GLA_H)], axis=0))
        a_off = _dot_nt(_bf(qcat), kst)

        for dlt in range(GLA_SUB):
            if dlt == 0:
                pr = qc * kc
            else:
                kd = pltpu.roll(kc, dlt, 0)
                bd = pltpu.roll(b, dlt, 0)
                pr = qc * kd * jnp.exp(jnp.minimum(b - bd, 0.0))
            p_s[dlt * C:(dlt + 1) * C, :] = _bf(pr)
        rsum = _dot(p_s[...], ebc)
        a_diag = jnp.zeros((C, 4 * C), F32)
        for dlt in range(GLA_SUB):
            a_diag = jnp.where(band == dlt, rsum[dlt * C:(dlt + 1) * C, :], a_diag)

        vst = _bf(jnp.concatenate([jnp.where(vh == h, vc, 0.0) for h in range(GLA_H)], axis=0))
        o_intra = _dot(_bf(a_off + a_diag), vst)
        st = state_s[...]
        o_inter = _dot_nt(_bf(qc * jnp.exp(b)), _bf(st))
        o_s[pl.ds(r0, C), :] = o_intra + o_inter

        blast = jnp.broadcast_to(b[C - 1:C, :], b.shape)
        kdec = kc * jnp.exp(blast - b)
        kvt = _dot_tn(_bf(vc), _bf(kdec))
        decay = jnp.exp(b[C - 1:C, :])
        state_s[...] = st * decay + jnp.where(blockmask, kvt, 0.0)
        return carry

    return chunk


def _make_gla_chunk_bounded(q_s, k_s, g_s, v_s, o_s, state_s):
    C = GLA_CHUNK
    kh = _gla_k_head(_iota((C, GLA_KW), 1))
    vh = _gla_v_head(_iota((C, GLA_VW), 1))
    tri = _bf(jnp.where(_iota((C, C), 0) >= _iota((C, C), 1), 1.0, 0.0))
    causal = _iota((C, 4 * C), 0) >= (_iota((C, 4 * C), 1) & (C - 1))
    blockmask = _gla_v_head(_iota((GLA_VW, GLA_KW), 0)) == _gla_k_head(_iota((GLA_VW, GLA_KW), 1))

    def chunk(c, carry):
        r0 = pl.multiple_of(c * C, C)
        qc = q_s[pl.ds(r0, C), :]
        kc = k_s[pl.ds(r0, C), :]
        vc = v_s[pl.ds(r0, C), :]
        b = _exact_dot(tri, g_s[pl.ds(r0, C), :])
        qe = _bf(qc * jnp.exp(b))
        ke = kc * jnp.exp(-b)
        kst = _bf(jnp.concatenate([jnp.where(kh == h, ke, 0.0) for h in range(GLA_H)], axis=0))
        a = jnp.where(causal, _dot_nt(qe, kst), 0.0)
        vst = _bf(jnp.concatenate([jnp.where(vh == h, vc, 0.0) for h in range(GLA_H)], axis=0))
        st = state_s[...]
        o_s[pl.ds(r0, C), :] = _dot(_bf(a), vst) + _dot_nt(qe, _bf(st))
        decay = jnp.exp(b[C - 1:C, :])
        kvt = _dot_tn(_bf(vc), _bf(ke * decay))
        state_s[...] = st * decay + jnp.where(blockmask, kvt, 0.0)
        return carry

    return chunk


def _mixer_a_kernel(x_ref, gpre_ref, wqk_ref, wv_ref, wr_ref, wmisc_ref, wg_ref, bg_ref, glan_ref,
                    kbd_ref, vbd_ref, wo1_ref, wo2_ref, gpost_ref,
                    xo_ref, st_ref,
                    q_s, k_s, g_s, v_s, o_s, state_s, p_s, gate_s, om_s, hook=None):
    C = GLA_CHUNK
    tm = x_ref.shape[1]
    nchunk = tm // C

    @pl.when(pl.program_id(1) == 0)
    def _():
        state_s[...] = jnp.zeros_like(state_s)

    q, k, g, v, gate, o_mem = _proj_a(x_ref[0], gpre_ref, wqk_ref, wv_ref, wr_ref, wmisc_ref, wg_ref, bg_ref,
                                      glan_ref, (kbd_ref.at[0], vbd_ref.at[0]))
    q_s[...] = q
    k_s[...] = k
    g_s[...] = g
    v_s[...] = v
    gate_s[...] = gate
    om_s[...] = _bf(o_mem)
    if hook is not None:
        hook()

    total = jnp.sum(g.reshape(nchunk, C, GLA_KW), axis=1)
    bounded = jnp.min(total) > -GLA_SAFE_DECAY

    @pl.when(bounded)
    def _():
        lax.fori_loop(0, nchunk, _make_gla_chunk_bounded(q_s, k_s, g_s, v_s, o_s, state_s), 0, unroll=4)

    @pl.when(jnp.logical_not(bounded))
    def _():
        lax.fori_loop(0, nchunk, _make_gla_chunk(q_s, k_s, g_s, v_s, o_s, state_s, p_s), 0)

    st_ref[0] = state_s[...]
    o_main = _gla_out_gate(o_s[...], gate_s[...])
    xo_ref[0] = _mix_residual(x_ref[0], o_main, om_s[...], wo1_ref, wo2_ref, gpost_ref)


def _const_spec(a):
    nd = a.ndim
    return pl.BlockSpec(a.shape, lambda *_: (0,) * nd, pipeline_mode=pl.Buffered(1))


def _mixer_a_call(x, wa, kbd, vbd, make_guests):
    nb, seq, _ = x.shape
    tm = TM_PROMPT
    nt = seq // tm
    weights = [wa["gpre"], wa["wqk"], wa["wv"], wa["wr"], wa["wmisc"], wa["wg"], wa["bg"], wa["glan"]]
    tail = [wa["wo1"], wa["wo2"], wa["gpost"]]
    return _call_with_guests(
        _mixer_a_kernel,
        grid=(nb, nt),
        step_of=lambda b, t: b * nt + t,
        inputs=[x, *weights, kbd, vbd, *tail],
        in_specs=([pl.BlockSpec((1, tm, D), lambda b, t: (b, t, 0))]
                  + [_const_spec(w) for w in weights]
                  + [pl.BlockSpec((1, MEM_W, MEM_H * N_MEM), lambda b, t: (b, 0, 0)),
                     pl.BlockSpec((1, MEM_H * N_MEM, MEM_W), lambda b, t: (b, 0, 0))]
                  + [_const_spec(w) for w in tail]),
        out_specs=[pl.BlockSpec((1, tm, D), lambda b, t: (b, t, 0)),
                   pl.BlockSpec((1, GLA_VW, GLA_KW), lambda b, t: (b, 0, 0))],
        out_shape=[jax.ShapeDtypeStruct((nb, seq, D), F32),
                   jax.ShapeDtypeStruct((nb, GLA_VW, GLA_KW), F32)],
        scratch=[
            pltpu.VMEM((tm, GLA_KW), F32), pltpu.VMEM((tm, GLA_KW), F32), pltpu.VMEM((tm, GLA_KW), F32),
            pltpu.VMEM((tm, GLA_VW), F32), pltpu.VMEM((tm, GLA_VW), F32),
            pltpu.VMEM((GLA_VW, GLA_KW), F32),
            pltpu.VMEM((GLA_SUB * GLA_CHUNK, GLA_KW), BF16),
            pltpu.VMEM((tm, GLA_VW), F32), pltpu.VMEM((tm, MEM_W), BF16),
        ],
        guests=make_guests(nb * nt),
        name="mixer_a_prompt",
    )


def _ffn_kernel(x_ref, gpre_ref, wup_ref, wdn_ref, gpost_ref, o_ref, acc_s, hook=None):
    x = x_ref[...]
    h = _bf(_rms(x, gpre_ref[...]))
    nff = D_FF // FF_CHUNK
    for j in range(nff):
        if hook is not None and j == nff // 2:
            hook()
        u = _dot(h, wup_ref[:, j * FF_CHUNK:(j + 1) * FF_CHUNK])
        u = jnp.maximum(u, 0.0)
        d = _dot(_bf(u * u), wdn_ref[j * FF_CHUNK:(j + 1) * FF_CHUNK, :])
        if j == 0:
            acc_s[...] = d
        else:
            acc_s[...] += d
    o_ref[...] = x + _rms(acc_s[...], gpost_ref[...])


def _ffn_call(x2d, wf, tm, make_guests=None):
    n = x2d.shape[0]
    ws = [wf["gpre"], wf["wup"], wf["wdn"], wf["gpost"]]
    (y,), guest_outs = _call_with_guests(
        _ffn_kernel,
        grid=(n // tm,),
        step_of=lambda i: i,
        inputs=[x2d, *ws],
        in_specs=[pl.BlockSpec((tm, D), lambda i: (i, 0))] + [_const_spec(w) for w in ws],
        out_specs=[pl.BlockSpec((tm, D), lambda i: (i, 0))],
        out_shape=[jax.ShapeDtypeStruct((n, D), F32)],
        scratch=[pltpu.VMEM((tm, D), F32)],
        guests=[] if make_guests is None else make_guests(n // tm),
        name="ffn",
    )
    return y if make_guests is None else (y, guest_outs)


def _t5_bucket(dist):
    max_exact = N_BUCKETS // 2
    n = np.maximum(dist, 0)
    nf = np.maximum(n, 1).astype(np.float32)
    large = max_exact + (np.log(nf / np.float32(max_exact)) / np.float32(math.log(MAX_DISTANCE / max_exact))
                         * np.float32(N_BUCKETS - max_exact)).astype(np.int32)
    large = np.minimum(large, N_BUCKETS - 1)
    return np.where(n < max_exact, n, large)


def _swa_bias_tables():
    qi = np.arange(WINDOW)[:, None] + WINDOW
    kj = np.arange(2 * WINDOW)[None, :]
    dist = qi - kj
    valid = (dist >= 0) & (dist < WINDOW)
    return np.where(valid, _t5_bucket(dist), -1).astype(np.int32)


def _mixer_b_kernel(rb_ref, sink_ref,
                    x_ref, bkt_ref, gkv_ref, wkv_ref, gpre_ref, wq_ref, wqm_ref,
                    kbd_ref, vbd_ref, wo1_ref, wo2_ref, gpost_ref,
                    xo_ref, kc_ref, vc_ref,
                    kbuf, vbuf, q_s, o_s, bias_s, qm_s):
    W = WINDOW
    tm = x_ref.shape[1]
    bb = pl.program_id(0)
    t = pl.program_id(1)
    nheads = SWA_G * SWA_KVH

    @pl.when((bb == 0) & (t == 0))
    def _():
        bkt = bkt_ref[...]
        own = _iota((W, 2 * W), 1) >= W
        for i in range(nheads):
            def add_bucket(n, acc):
                return jnp.where(bkt == n, rb_ref[n, i], acc)
            tab = lax.fori_loop(0, N_BUCKETS, add_bucket, jnp.zeros((W, 2 * W), F32))
            tab = jnp.where(bkt < 0, NEG_INF, tab)
            bias_s[0, i] = tab
            bias_s[1, i] = jnp.where(own, tab, NEG_INF)

    @pl.when(t == 0)
    def _():
        kbuf[0:W, :] = jnp.zeros((W, SWA_KVW), F32)
        vbuf[0:W, :] = jnp.zeros((W, SWA_KVW), F32)

    x = x_ref[0]
    xn = x * lax.rsqrt(jnp.mean(x * x, axis=-1, keepdims=True) + EPS)
    kv = _dot(_bf(xn * gkv_ref[...]), wkv_ref[...])
    kbuf[W:W + tm, :] = kv[:, :SWA_KVW]
    vbuf[W:W + tm, :] = kv[:, SWA_KVW:]
    kc_ref[0] = kv[tm - W:, :SWA_KVW]
    vc_ref[0] = kv[tm - W:, SWA_KVW:]

    h = _bf(xn * gpre_ref[...])
    q_s[...] = _dot(h, wq_ref[...]) * (SWA_HD ** -0.5)
    qm_s[...] = _dot(h, wqm_ref[...])

    lane_head = _iota((W, SWA_KVW), 1) >> 6
    key_head = _iota((2 * W, SWA_KVW), 1) >> 6

    def block(j, carry):
        r0 = pl.multiple_of(j * W, W)
        kb = kbuf[pl.ds(r0, 2 * W), :]
        vb = vbuf[pl.ds(r0, 2 * W), :]
        tab = jnp.where((j == 0) & (t == 0), 1, 0)
        qrows = []
        for gi in range(SWA_G):
            qg = q_s[pl.ds(r0, W), gi * SWA_KVW:(gi + 1) * SWA_KVW]
            for hh in range(SWA_KVH):
                qrows.append(jnp.where(lane_head == hh, qg, 0.0))
        s_all = _dot_nt(_bf(jnp.concatenate(qrows, axis=0)), _bf(kb))
        vst = _bf(jnp.concatenate([jnp.where(key_head == hh, vb, 0.0) for hh in range(SWA_KVH)], axis=0))
        for gi in range(SWA_G):
            ps = []
            for hh in range(SWA_KVH):
                i = gi * SWA_KVH + hh
                s = s_all[i * W:(i + 1) * W, :] + bias_s[tab, i]
                sink = sink_ref[i]
                mx = jnp.maximum(jnp.max(s, axis=-1, keepdims=True), sink)
                p = jnp.exp(s - mx)
                p = p / (jnp.sum(p, axis=-1, keepdims=True) + jnp.exp(sink - mx))
                ps.append(_bf(p))
            o_s[pl.ds(r0, W), gi * SWA_KVW:(gi + 1) * SWA_KVW] = _dot(jnp.concatenate(ps, axis=1), vst)
        return carry

    lax.fori_loop(0, tm // W, block, 0, unroll=2)

    kbuf[0:W, :] = kbuf[tm:tm + W, :]
    vbuf[0:W, :] = vbuf[tm:tm + W, :]

    o_mem = _mem_pv(_mem_probs(qm_s[...], kbd_ref.at[0]), vbd_ref.at[0])
    xo_ref[0] = _mix_residual(x_ref[0], o_s[...], o_mem, wo1_ref, wo2_ref, gpost_ref)


def _mixer_b_call(x, wb, kbd, vbd):
    nb, seq, _ = x.shape
    tm = TM_PROMPT
    bkt = jnp.asarray(_swa_bias_tables())
    head = [bkt, wb["gkv"], wb["wkv"], wb["gpre"], wb["wq"], wb["wqm"]]
    tail = [wb["wo1"], wb["wo2"], wb["gpost"]]
    smem = pl.BlockSpec(memory_space=pltpu.SMEM)
    return pl.pallas_call(
        _mixer_b_kernel,
        grid=(nb, seq // tm),
        in_specs=([smem, smem, pl.BlockSpec((1, tm, D), lambda b, t: (b, t, 0))]
                  + [_full_spec(w) for w in head]
                  + [pl.BlockSpec((1, MEM_W, MEM_H * N_MEM), lambda b, t: (b, 0, 0)),
                     pl.BlockSpec((1, MEM_H * N_MEM, MEM_W), lambda b, t: (b, 0, 0))]
                  + [_full_spec(w) for w in tail]),
        out_specs=[pl.BlockSpec((1, tm, D), lambda b, t: (b, t, 0)),
                   pl.BlockSpec((1, WINDOW, SWA_KVW), lambda b, t: (b, 0, 0)),
                   pl.BlockSpec((1, WINDOW, SWA_KVW), lambda b, t: (b, 0, 0))],
        out_shape=[jax.ShapeDtypeStruct((nb, seq, D), F32),
                   jax.ShapeDtypeStruct((nb, WINDOW, SWA_KVW), F32),
                   jax.ShapeDtypeStruct((nb, WINDOW, SWA_KVW), F32)],
        scratch_shapes=[
            pltpu.VMEM((tm + WINDOW, SWA_KVW), F32), pltpu.VMEM((tm + WINDOW, SWA_KVW), F32),
            pltpu.VMEM((tm, SWA_QW), F32), pltpu.VMEM((tm, SWA_QW), F32),
            pltpu.VMEM((2, SWA_G * SWA_KVH, WINDOW, 2 * WINDOW), F32),
            pltpu.VMEM((tm, MEM_W), F32),
        ],
        compiler_params=_params(("arbitrary", "arbitrary")),
        name="mixer_b_prompt",
    )(wb["rb"], wb["sinks"], x, *head, kbd, vbd, *tail)


def _row_to_col(row, eye):
    return jnp.sum(jnp.where(eye, jnp.broadcast_to(row, eye.shape), 0.0), axis=1, keepdims=True)


def _eye(n):
    return _iota((n, n), 0) == _iota((n, n), 1)


def _pre_a_kernel(x_ref, gpre_ref, wqk_ref, wv_ref, wr_ref, wmisc_ref, wg_ref, bg_ref, glan_ref,
                  q_ref, k_ref, g_ref, v_ref, gate_ref, qm_ref):
    q, k, g, v, gate, qm = _proj_a(x_ref[...], gpre_ref, wqk_ref, wv_ref, wr_ref, wmisc_ref, wg_ref, bg_ref,
                                   glan_ref)
    q_ref[...] = q
    k_ref[...] = k
    g_ref[...] = g
    v_ref[...] = v
    gate_ref[...] = gate
    qm_ref[...] = qm


def _pre_a_call(x2d, wa):
    n = x2d.shape[0]
    ws = [wa["gpre"], wa["wqk"], wa["wv"], wa["wr"], wa["wmisc"], wa["wg"], wa["bg"], wa["glan"]]
    shapes = [(n, GLA_KW), (n, GLA_KW), (n, GLA_KW), (n, GLA_VW), (n, GLA_VW), (n, MEM_W)]
    return pl.pallas_call(
        _pre_a_kernel,
        grid=(1,),
        in_specs=[_full_spec(x2d)] + [_full_spec(w) for w in ws],
        out_specs=[pl.BlockSpec(s, lambda i: (0, 0)) for s in shapes],
        out_shape=[jax.ShapeDtypeStruct(s, F32) for s in shapes],
        compiler_params=_params(("arbitrary",)),
        name="pre_a_sample",
    )(x2d, *ws)


def _gla_step_guest(qt, kt, gt, vt, state5, nsteps):
    n = qt.shape[1]
    nblk = GLA_DK // GLA_DK_BLOCK
    nact = GLA_H * nblk
    assert nact <= nsteps
    act = lambda s: jnp.minimum(s, nact - 1)

    def pre(step, qt_ref, kt_ref, gt_ref, vt_ref, s_ref, so_ref, ot_ref):
        @pl.when(step < nact)
        def _():
            vt_blk = vt_ref[...]
            acc = jnp.zeros_like(vt_blk)
            for d in range(GLA_DK_BLOCK):
                s_new = jnp.exp(gt_ref[d:d + 1, :]) * s_ref[0, 0, d] + kt_ref[d:d + 1, :] * vt_blk
                so_ref[0, 0, d] = s_new
                acc = acc + qt_ref[d:d + 1, :] * s_new
            first = step % nblk == 0

            @pl.when(first)
            def _():
                ot_ref[...] = acc

            @pl.when(jnp.logical_not(first))
            def _():
                ot_ref[...] += acc

    rows = ((GLA_DK_BLOCK, n), lambda s: (act(s), 0))
    head = ((GLA_DV, n), lambda s: (act(s) // nblk, 0))
    st = ((1, 1, GLA_DK_BLOCK, GLA_DV, n), lambda s: (0, act(s) // nblk, act(s) % nblk, 0, 0))
    return dict(inputs=[qt, kt, gt, vt, state5], in_specs=[rows, rows, rows, head, st],
                out_shape=[jax.ShapeDtypeStruct(state5.shape, F32), jax.ShapeDtypeStruct((GLA_VW, n), F32)],
                out_specs=[st, head], scratch=[], pre=pre, post=lambda step, state, *refs: None)


def _mem_step_pre(step, qm_ref, mk_ref, mv_ref, o_ref):
    own = (_iota((8, MEM_W), 1) >> 6) == _iota((8, MEM_W), 0)
    ps = []
    for i in range(qm_ref.shape[0]):
        q8 = jnp.where(own, jnp.broadcast_to(qm_ref[i], (8, MEM_W)), 0.0)
        s = _dot(_bf(q8), _bf(mk_ref[0, i].reshape(MEM_W, N_MEM))) * (MEM_HD ** -0.5)
        e = jnp.exp(s - jnp.max(s, axis=1, keepdims=True))
        ps.append(_bf(e / jnp.sum(e, axis=1, keepdims=True)))
    return ps


def _mem_step_post(step, ps, qm_ref, mk_ref, mv_ref, o_ref):
    own = (_iota((8, MEM_W), 1) >> 6) == _iota((8, MEM_W), 0)
    for i, p in enumerate(ps):
        res = _dot_nt(p, _bf(mv_ref[0, i].reshape(MEM_W, N_MEM)))
        o_ref[i] = jnp.sum(jnp.where(own, res, 0.0), axis=0, keepdims=True)


def _mem_step_guest(qm, mk5, mv5, layer, nsteps):
    n = qm.shape[0]
    rb = n // nsteps
    blk = ((1, rb, MEM_H, MEM_HD, N_MEM), lambda s: (layer, s, 0, 0, 0))
    rows = ((rb, 1, MEM_W), lambda s: (s, 0, 0))
    return dict(inputs=[qm.reshape(n, 1, MEM_W), mk5, mv5], in_specs=[rows, blk, blk],
                out_shape=[jax.ShapeDtypeStruct((n, 1, MEM_W), F32)], out_specs=[rows], scratch=[],
                pre=_mem_step_pre, post=_mem_step_post)


def _post_a_kernel(x_ref, o_ref, gate_ref, om_ref, wo1_ref, wo2_ref, gpost_ref, xo_ref):
    o_main = _gla_out_gate(o_ref[...], gate_ref[...])
    xo_ref[...] = _mix_residual(x_ref[...], o_main, om_ref[...], wo1_ref, wo2_ref, gpost_ref)


def _post_a_call(x2d, o, gate, om, wa):
    args = [x2d, o, gate, om, wa["wo1"], wa["wo2"], wa["gpost"]]
    return pl.pallas_call(
        _post_a_kernel,
        grid=(1,),
        in_specs=[_full_spec(a) for a in args],
        out_specs=_full_spec(x2d),
        out_shape=jax.ShapeDtypeStruct(x2d.shape, F32),
        compiler_params=_params(("arbitrary",)),
        name="post_a_sample",
    )(*args)


def _pre_b_kernel(x_ref, gkv_ref, wkv_ref, gpre_ref, wq_ref, wqm_ref, ks_ref, vs_ref, q_ref, qm_ref):
    x = x_ref[...]
    kv = _dot(_bf(_rms(x, gkv_ref[...])), wkv_ref[...])
    ks_ref[...] = kv[:, :SWA_KVW]
    vs_ref[...] = kv[:, SWA_KVW:]
    h = _bf(_rms(x, gpre_ref[...]))
    q_ref[...] = _dot(h, wq_ref[...])
    qm_ref[...] = _dot(h, wqm_ref[...])


def _pre_b_call(x2d, wb):
    n = x2d.shape[0]
    ws = [wb["gkv"], wb["wkv"], wb["gpre"], wb["wq"], wb["wqm"]]
    widths = [SWA_KVW, SWA_KVW, SWA_QW, MEM_W]
    return pl.pallas_call(
        _pre_b_kernel,
        grid=(1,),
        in_specs=[_full_spec(x2d)] + [_full_spec(w) for w in ws],
        out_specs=[pl.BlockSpec((n, w), lambda i: (0, 0)) for w in widths],
        out_shape=[jax.ShapeDtypeStruct((n, w), F32) for w in widths],
        compiler_params=_params(("arbitrary",)),
        name="pre_b_sample",
    )(x2d, *ws)


def _sample_buckets():
    dist = (WINDOW - 1) - np.arange(WINDOW)
    return _t5_bucket(dist).astype(np.int32).reshape(1, WINDOW)


def _swa_step_pre(step, sink_ref, rb_ref, bkt_ref, kc_ref, vc_ref, kst_ref, vst_ref, q_ref,
                  kn_ref, vn_ref, o_ref, bias_s, sink_s, s_s, p_s):
    W = WINDOW
    R = SWA_KVH * 8

    @pl.when(step == 0)
    def _():
        bkt = bkt_ref[...]
        rid = _iota((R, W), 0)
        bias = jnp.zeros((R, W), F32)
        sink = jnp.zeros((R, W), F32)
        for h in range(SWA_KVH):
            for g in range(SWA_G):
                idx = g * SWA_KVH + h
                def add_bucket(n, acc):
                    return jnp.where(bkt == n, rb_ref[n, idx], acc)
                brow = lax.fori_loop(0, N_BUCKETS, add_bucket, jnp.zeros((1, W), F32))
                bias = jnp.where(rid == h * 8 + g, brow, bias)
                sink = jnp.where(rid == h * 8 + g, sink_ref[idx], sink)
        bias_s[...] = bias
        sink_s[...] = sink

    rb = q_ref.shape[0]
    base = step * rb
    last = _iota((SWA_KVW, W), 1) == W - 1
    own = (_iota((R, SWA_KVW), 1) >> 6) == (_iota((R, SWA_KVW), 0) >> 3)
    kst = kst_ref[...]
    vst = vst_ref[...]
    for i in range(rb):
        shift = W - 1 - (base + i)
        kn = jnp.where(last, pltpu.roll(kst, shift, 1), pltpu.roll(kc_ref[i].reshape(SWA_KVW, W), W - 1, 1))
        vn = jnp.where(last, pltpu.roll(vst, shift, 1), pltpu.roll(vc_ref[i].reshape(SWA_KVW, W), W - 1, 1))
        kn_ref[i] = kn.reshape(SWA_KVH, SWA_HD, W)
        vn_ref[i] = vn.reshape(SWA_KVH, SWA_HD, W)
        q32 = jnp.where(own, jnp.concatenate([q_ref[i]] * SWA_KVH, axis=0), 0.0)
        s_s[i * R:(i + 1) * R, :] = _dot(_bf(q32), _bf(kn))
    s = s_s[...] * (SWA_HD ** -0.5) + jnp.concatenate([bias_s[...]] * rb, axis=0)
    sink = jnp.concatenate([sink_s[...]] * rb, axis=0)
    mx = jnp.maximum(jnp.max(s, axis=1, keepdims=True), sink)
    p = jnp.exp(s - mx)
    p_s[...] = _bf(p / (jnp.sum(p, axis=1, keepdims=True) + jnp.exp(sink - mx)))


def _swa_step_post(step, _, sink_ref, rb_ref, bkt_ref, kc_ref, vc_ref, kst_ref, vst_ref, q_ref,
                   kn_ref, vn_ref, o_ref, bias_s, sink_s, s_s, p_s):
    W = WINDOW
    R = SWA_KVH * 8
    own = (_iota((R, SWA_KVW), 1) >> 6) == (_iota((R, SWA_KVW), 0) >> 3)
    for i in range(q_ref.shape[0]):
        vn = vn_ref[i].reshape(SWA_KVW, W)
        res = jnp.where(own, _dot_nt(p_s[i * R:(i + 1) * R, :], _bf(vn)), 0.0)
        o_ref[i] = res[0:8] + res[8:16] + res[16:24] + res[24:32]


def _swa_step_guest(kc4, vc4, ks, vs, q, wb, nsteps):
    n = q.shape[0]
    rb = n // nsteps
    bkt = jnp.asarray(_sample_buckets())
    q8 = jnp.pad(q.reshape(n, SWA_G, SWA_KVW), ((0, 0), (0, 8 - SWA_G), (0, 0)))
    row3 = lambda r, w: ((rb, r, w), lambda s: (s, 0, 0))
    cache = ((rb, SWA_KVH, SWA_HD, WINDOW), lambda s: (s, 0, 0, 0))
    whole = ((SWA_KVW, n), lambda s: (0, 0))
    smem = pl.BlockSpec(memory_space=pltpu.SMEM)
    return dict(inputs=[wb["sinks"], wb["rb"], bkt, kc4, vc4, ks.T, vs.T, q8],
                in_specs=[smem, smem, _full_spec(bkt), cache, cache, whole, whole, row3(8, SWA_KVW)],
                out_shape=[jax.ShapeDtypeStruct(kc4.shape, F32), jax.ShapeDtypeStruct(vc4.shape, F32),
                           jax.ShapeDtypeStruct((n, 8, SWA_KVW), F32)],
                out_specs=[cache, cache, row3(8, SWA_KVW)],
                scratch=[pltpu.VMEM((SWA_KVH * 8, WINDOW), F32), pltpu.VMEM((SWA_KVH * 8, WINDOW), F32),
                         pltpu.VMEM((rb * SWA_KVH * 8, WINDOW), F32),
                         pltpu.VMEM((rb * SWA_KVH * 8, WINDOW), BF16)],
                pre=_swa_step_pre, post=_swa_step_post)


def _post_b_kernel(x_ref, o_ref, om_ref, wo1_ref, wo2_ref, gpost_ref, xo_ref):
    xo_ref[...] = _mix_residual(x_ref[...], o_ref[...], om_ref[...], wo1_ref, wo2_ref, gpost_ref)


def _post_b_call(x2d, o, om, wb):
    args = [x2d, o, om, wb["wo1"], wb["wo2"], wb["gpost"]]
    return pl.pallas_call(
        _post_b_kernel,
        grid=(1,),
        in_specs=[_full_spec(a) for a in args],
        out_specs=_full_spec(x2d),
        out_shape=jax.ShapeDtypeStruct(x2d.shape, F32),
        compiler_params=_params(("arbitrary",)),
        name="post_b_sample",
    )(*args)


def _prep_weights(norm_mix_pre, norm_mix_post, norm_ffn_pre, norm_ffn_post, w_in_a, w_gate_up, b_gate,
                  gla_norm, w_in_b, sinks, norm_kv, w_kv, rel_bias, w_out, w_ffn_up, w_ffn_down):
    row = lambda g: g.reshape(1, -1)
    wa_in = w_in_a[0]
    c_v = 2 * GLA_KW
    c_r = c_v + GLA_VW
    c_g = c_r + GLA_VW
    c_m = c_g + GATE_RANK
    wmisc = jnp.concatenate([wa_in[:, c_g:c_m], jnp.zeros((D, 128 - GATE_RANK), F32), wa_in[:, c_m:]], axis=1)
    wg = jnp.zeros((128, GLA_KW), F32).at[:GATE_RANK].set(w_gate_up[0])
    wa = dict(
        gpre=row(norm_mix_pre[0]), wqk=_bf(wa_in[:, :c_v]), wv=_bf(wa_in[:, c_v:c_r]), wr=_bf(wa_in[:, c_r:c_g]),
        wmisc=_bf(wmisc), wg=_bf(wg), bg=row(b_gate[0]), glan=row(jnp.tile(gla_norm[0], GLA_H)),
        wo1=_bf(w_out[0][:GLA_VW]), wo2=_bf(w_out[0][GLA_VW:]), gpost=row(norm_mix_post[0]))
    wb_in = w_in_b[0]
    wq = wb_in[:, :SWA_QW].reshape(D, SWA_KVH, SWA_G, SWA_HD).transpose(0, 2, 1, 3).reshape(D, SWA_QW)
    wo1 = w_out[1][:SWA_QW].reshape(SWA_KVH, SWA_G, SWA_HD, D).transpose(1, 0, 2, 3).reshape(SWA_QW, D)
    rb = rel_bias.reshape(N_BUCKETS, SWA_KVH, SWA_G).transpose(0, 2, 1).reshape(N_BUCKETS, SWA_G * SWA_KVH)
    sk = sinks[0].reshape(SWA_KVH, SWA_G).T.reshape(SWA_G * SWA_KVH)
    wb = dict(
        gkv=row(norm_kv), wkv=_bf(w_kv), gpre=row(norm_mix_pre[1]), wq=_bf(wq), wqm=_bf(wb_in[:, SWA_QW:]),
        wo1=_bf(wo1), wo2=_bf(w_out[1][SWA_QW:]), gpost=row(norm_mix_post[1]), rb=rb, sinks=sk)
    ffn = [dict(gpre=row(norm_ffn_pre[l]), wup=_bf(w_ffn_up[l]), wdn=_bf(w_ffn_down[l]),
                gpost=row(norm_ffn_post[l])) for l in range(2)]
    return wa, wb, ffn


def kernel(x_prompt, x_sample, state_gla, cache_swa_k, cache_swa_v, cache_mem_k, cache_mem_v, mem_prompt,
           norm_mix_pre, norm_mix_post, norm_ffn_pre, norm_ffn_post, norm_mem, w_mem_kv, w_in_a, w_gate_up,
           b_gate, gla_norm, w_in_b, sinks, norm_kv, w_kv, rel_bias, w_out, w_ffn_up, w_ffn_down):
    wa, wb, ffn = _prep_weights(norm_mix_pre, norm_mix_post, norm_ffn_pre, norm_ffn_post, w_in_a, w_gate_up,
                                b_gate, gla_norm, w_in_b, sinks, norm_kv, w_kv, rel_bias, w_out, w_ffn_up,
                                w_ffn_down)
    nb, seq, _ = x_prompt.shape
    ns = x_sample.shape[0]

    xs = x_sample.reshape(ns, D)
    state5 = jnp.transpose(state_gla, (0, 2, 3, 4, 1))
    mk5 = jnp.transpose(cache_mem_k, (0, 1, 3, 4, 2))
    mv5 = jnp.transpose(cache_mem_v, (0, 1, 3, 4, 2))
    kc4 = jnp.transpose(cache_swa_k, (0, 2, 3, 1))
    vc4 = jnp.transpose(cache_swa_v, (0, 2, 3, 1))

    mkt, mvt, kbd, vbd = _memkv_call(mem_prompt, norm_mem, w_mem_kv)
    q, k, g, v, gate, qm = _pre_a_call(xs, wa)
    (x1, st), ((state5_new, ot), (om,)) = _mixer_a_call(
        x_prompt, wa, kbd[0], vbd[0],
        lambda nsteps: [_gla_step_guest(q.T, k.T, g.T, v.T, state5, nsteps),
                        _mem_step_guest(qm, mk5, mv5, 0, nsteps)])
    xs1 = _post_a_call(xs, ot.T, gate, om.reshape(ns, MEM_W), wa)
    xs2 = _ffn_call(xs1, ffn[0], ns)
    ks, vs, qb, qmb = _pre_b_call(xs2, wb)
    x2, ((kn4, vn4, o8), (omb,)) = _ffn_call(
        x1.reshape(nb * seq, D), ffn[0], TM_PROMPT,
        lambda nsteps: [_swa_step_guest(kc4, vc4, ks, vs, qb, wb, nsteps),
                        _mem_step_guest(qmb, mk5, mv5, 1, nsteps)])
    xs3 = _post_b_call(xs2, o8[:, :SWA_G].reshape(ns, SWA_QW), omb.reshape(ns, MEM_W), wb)
    y_sample = _ffn_call(xs3, ffn[1], ns).reshape(ns, 1, D)

    x3, kc, vc = _mixer_b_call(x2.reshape(nb, seq, D), wb, kbd[1], vbd[1])
    y_prompt = _ffn_call(x3.reshape(nb * seq, D), ffn[1], TM_PROMPT).reshape(nb, seq, D)
    st4 = st.reshape(nb, GLA_H, GLA_DV, GLA_H, GLA_DK)
    state_prompt = jnp.stack([st4[:, h, :, h, :] for h in range(GLA_H)], axis=1).transpose(0, 1, 3, 2)[None]
    to_mem = lambda t: t.reshape(2, nb, MEM_H, MEM_HD, N_MEM).transpose(0, 1, 4, 2, 3)
    swa_shape = (nb, WINDOW, SWA_KVH, SWA_HD)

    return (y_prompt, y_sample, state_prompt,
            jnp.transpose(state5_new, (0, 4, 1, 2, 3)),
            kc.reshape(swa_shape), vc.reshape(swa_shape),
            jnp.transpose(kn4, (0, 3, 1, 2)), jnp.transpose(vn4, (0, 3, 1, 2)),
            to_mem(mkt), to_mem(mvt))
```

```python
import functools
import math

import numpy as np
import jax
import jax.numpy as jnp
from jax import lax
from jax.experimental import pallas as pl
from jax.experimental.pallas import tpu as pltpu

F32 = jnp.float32
BF16 = jnp.bfloat16

D = 1024
D_FF = 4 * D
N_MEM = 256
MEM_H = 4
MEM_HD = 64
MEM_W = MEM_H * MEM_HD
GLA_H = 4
GLA_DK = 96
GLA_DV = 192
GLA_KW = GLA_H * GLA_DK
GLA_VW = GLA_H * GLA_DV
GATE_RANK = 16
GATE_NORM = 16.0
SWA_HD = 64
SWA_KVH = 4
SWA_G = 3
SWA_QW = SWA_KVH * SWA_G * SWA_HD
SWA_KVW = SWA_KVH * SWA_HD
WINDOW = 128
N_BUCKETS = 32
MAX_DISTANCE = 128
EPS = 1e-6

GLA_CHUNK = 64
GLA_SUB = 16
GLA_SAFE_DECAY = 60.0
TM_PROMPT = 512
FF_CHUNK = 512
GLA_DK_BLOCK = 16
V7X_VMEM_LIMIT = 56 * 1024 * 1024
NEG_INF = float("-inf")


def _bf(x):
    return x.astype(BF16)


def _dot(a, b):
    return jnp.dot(a, b, preferred_element_type=F32)


def _dot_nt(a, b):
    return lax.dot_general(a, b, (((1,), (1,)), ((), ())), preferred_element_type=F32)


def _dot_tn(a, b):
    return lax.dot_general(a, b, (((0,), (0,)), ((), ())), preferred_element_type=F32)


def _rms(x, g):
    return x * lax.rsqrt(jnp.mean(x * x, axis=-1, keepdims=True) + EPS) * g


def _split3(x):
    x1 = _bf(x)
    r1 = x - x1.astype(F32)
    x2 = _bf(r1)
    x3 = _bf(r1 - x2.astype(F32))
    return x1, x2, x3


def _exact_dot(sel, x):
    x1, x2, x3 = _split3(x)
    return _dot(sel, x1) + _dot(sel, x2) + _dot(sel, x3)


def _log_sigmoid(z):
    return jnp.minimum(z, 0.0) - jnp.log1p(jnp.exp(-jnp.abs(z)))


def _silu(z):
    return z * (1.0 / (1.0 + jnp.exp(-z)))


def _iota(shape, dim):
    return lax.broadcasted_iota(jnp.int32, shape, dim)


def _gla_k_head(lane):
    one = jnp.int32(1)
    zero = jnp.int32(0)
    return (jnp.where(lane >= GLA_DK, one, zero) + jnp.where(lane >= 2 * GLA_DK, one, zero)
            + jnp.where(lane >= 3 * GLA_DK, one, zero))


def _gla_v_head(lane):
    one = jnp.int32(1)
    zero = jnp.int32(0)
    return (jnp.where(lane >= GLA_DV, one, zero) + jnp.where(lane >= 2 * GLA_DV, one, zero)
            + jnp.where(lane >= 3 * GLA_DV, one, zero))


def _full_spec(a):
    nd = a.ndim
    return pl.BlockSpec(a.shape, lambda *_: (0,) * nd)


def _params(sem):
    return pltpu.CompilerParams(dimension_semantics=sem, vmem_limit_bytes=V7X_VMEM_LIMIT)


def _call_with_guests(host_kernel, grid, step_of, inputs, in_specs, out_shape, out_specs, scratch, guests, name):
    def to_spec(s):
        if isinstance(s, pl.BlockSpec):
            return s
        block, fn = s
        return pl.BlockSpec(block, lambda *ids: fn(step_of(*ids)))

    counts = [(len(inputs), len(out_shape), len(scratch))]
    counts += [(len(g["inputs"]), len(g["out_shape"]), len(g["scratch"])) for g in guests]

    def kern(*refs):
        refs = list(refs)
        parts = [[refs.pop(0) for _ in range(c[k])] for k in range(3) for c in counts]
        n = len(counts)
        ins, outs, scrs = parts[:n], parts[n:2 * n], parts[2 * n:]
        step = step_of(*[pl.program_id(a) for a in range(len(grid))])
        grefs = [(*a, *b, *c) for a, b, c in zip(ins[1:], outs[1:], scrs[1:])]
        states = [g["pre"](step, *r) for g, r in zip(guests, grefs)]

        def hook():
            for g, st, r in zip(guests, states, grefs):
                g["post"](step, st, *r)

        host_kernel(*ins[0], *outs[0], *scrs[0], hook=hook)

    res = pl.pallas_call(
        kern,
        grid=grid,
        in_specs=list(in_specs) + [to_spec(s) for g in guests for s in g["in_specs"]],
        out_specs=list(out_specs) + [to_spec(s) for g in guests for s in g["out_specs"]],
        out_shape=list(out_shape) + [s for g in guests for s in g["out_shape"]],
        scratch_shapes=list(scratch) + [s for g in guests for s in g["scratch"]],
        compiler_params=_params(("arbitrary",) * len(grid)),
        name=name,
    )(*inputs, *[a for g in guests for a in g["inputs"]])
    res = list(res)
    split = []
    for c in counts:
        split.append([res.pop(0) for _ in range(c[1])])
    return split[0], split[1:]


def _mem_probs(qm, kbd_ref):
    s = _dot(_bf(qm), kbd_ref[...]) * (MEM_HD ** -0.5)
    ps = []
    for h in range(MEM_H):
        sh = s[:, h * N_MEM:(h + 1) * N_MEM]
        e = jnp.exp(sh - jnp.max(sh, axis=-1, keepdims=True))
        ps.append(_bf(e / jnp.sum(e, axis=-1, keepdims=True)))
    return ps


def _mem_pv(ps, vbd_ref):
    out = None
    for h in range(MEM_H):
        t = _dot(ps[h], vbd_ref[h * N_MEM:(h + 1) * N_MEM, :])
        out = t if out is None else out + t
    return out


def _memkv_kernel(mem_ref, g_ref, w_ref, k_ref, v_ref, kbd_ref, vbd_ref):
    h = _bf(_rms(mem_ref[0], g_ref[0]))
    kv = _dot(h, w_ref[0])
    k = kv[:, :MEM_W]
    v = kv[:, MEM_W:]
    kt = k.T
    k_ref[0, 0] = kt
    v_ref[0, 0] = v.T
    kt4 = jnp.concatenate([kt, kt, kt, kt], axis=1)
    keep_k = (_iota((MEM_W, MEM_H * N_MEM), 0) >> 6) == (_iota((MEM_W, MEM_H * N_MEM), 1) >> 8)
    kbd_ref[0, 0] = _bf(jnp.where(keep_k, kt4, 0.0))
    v4 = jnp.concatenate([v, v, v, v], axis=0)
    keep_v = (_iota((MEM_H * N_MEM, MEM_W), 0) >> 8) == (_iota((MEM_H * N_MEM, MEM_W), 1) >> 6)
    vbd_ref[0, 0] = _bf(jnp.where(keep_v, v4, 0.0))


def _memkv_call(mem, norm_mem, w_mem_kv):
    nb = mem.shape[0]
    nl = w_mem_kv.shape[0]
    g = norm_mem.reshape(nl, 1, D)
    w = _bf(w_mem_kv)
    return pl.pallas_call(
        _memkv_kernel,
        grid=(nl, nb),
        in_specs=[
            pl.BlockSpec((1, N_MEM, D), lambda l, b: (b, 0, 0)),
            pl.BlockSpec((1, 1, D), lambda l, b: (l, 0, 0)),
            pl.BlockSpec((1, D, 2 * MEM_W), lambda l, b: (l, 0, 0)),
        ],
        out_specs=[
            pl.BlockSpec((1, 1, MEM_W, N_MEM), lambda l, b: (l, b, 0, 0)),
            pl.BlockSpec((1, 1, MEM_W, N_MEM), lambda l, b: (l, b, 0, 0)),
            pl.BlockSpec((1, 1, MEM_W, MEM_H * N_MEM), lambda l, b: (l, b, 0, 0)),
            pl.BlockSpec((1, 1, MEM_H * N_MEM, MEM_W), lambda l, b: (l, b, 0, 0)),
        ],
        out_shape=[
            jax.ShapeDtypeStruct((nl, nb, MEM_W, N_MEM), F32),
            jax.ShapeDtypeStruct((nl, nb, MEM_W, N_MEM), F32),
            jax.ShapeDtypeStruct((nl, nb, MEM_W, MEM_H * N_MEM), BF16),
            jax.ShapeDtypeStruct((nl, nb, MEM_H * N_MEM, MEM_W), BF16),
        ],
        compiler_params=_params(("arbitrary", "arbitrary")),
        name="mem_kv",
    )(mem, g, w)


def _proj_a(x, gpre_ref, wqk_ref, wv_ref, wr_ref, wmisc_ref, wg_ref, bg_ref, glan_ref, mem_refs=None):
    h = _bf(_rms(x, gpre_ref[...]))
    misc = _dot(h, wmisc_ref[...])
    glr = misc[:, :128]
    qm = misc[:, 128:]
    ps = None if mem_refs is None else _mem_probs(qm, mem_refs[0])
    g = _log_sigmoid(_dot(_bf(glr), wg_ref[...]) + bg_ref[...]) * (1.0 / GATE_NORM)
    gate = glan_ref[...] * _silu(_dot(h, wr_ref[...]))
    qk = _dot(h, wqk_ref[...])
    q = qk[:, :GLA_KW] * (GLA_DK ** -0.5)
    k = qk[:, GLA_KW:]
    v = _dot(h, wv_ref[...])
    return q, k, g, v, gate, (qm if mem_refs is None else _mem_pv(ps, mem_refs[1]))


def _gla_out_gate(o, gate):
    vh = _gla_v_head(_iota(o.shape, 1))
    o2 = o * o
    scale = jnp.zeros_like(o)
    for h in range(GLA_H):
        ss = jnp.sum(jnp.where(vh == h, o2, 0.0), axis=-1, keepdims=True) * (1.0 / GLA_DV)
        scale = jnp.where(vh == h, lax.rsqrt(ss + EPS), scale)
    return o * scale * gate


def _mix_residual(x, o_main, o_mem, wo1_ref, wo2_ref, gpost_ref):
    mix = _dot(_bf(o_main), wo1_ref[...]) + _dot(_bf(o_mem), wo2_ref[...])
    return x + _rms(mix, gpost_ref[...])


def _make_gla_chunk(q_s, k_s, g_s, v_s, o_s, state_s, p_s):
    C = GLA_CHUNK

    ri = _iota((C, GLA_KW), 0)
    kh = _gla_k_head(_iota((C, GLA_KW), 1))
    khcat = jnp.concatenate([kh, kh, kh], axis=1)
    vh = _gla_v_head(_iota((C, GLA_VW), 1))
    tri = _bf(jnp.where(_iota((C, C), 0) >= _iota((C, C), 1), 1.0, 0.0))
    d_rs = _iota((C, 4 * C), 0) - (_iota((C, 4 * C), 1) & (C - 1))
    band = jnp.where((d_rs >= 0) & (d_rs <= (_iota((C, 4 * C), 0) & (GLA_SUB - 1))), d_rs, -1)
    ebc = _bf(jnp.where(_gla_k_head(_iota((GLA_KW, 4 * C), 0)) == (_iota((GLA_KW, 4 * C), 1) >> 6),
                        1.0, 0.0))
    blockmask = _gla_v_head(_iota((GLA_VW, GLA_KW), 0)) == _gla_k_head(_iota((GLA_VW, GLA_KW), 1))

    def chunk(c, carry):
        r0 = pl.multiple_of(c * C, C)
        qc = q_s[pl.ds(r0, C), :]
        kc = k_s[pl.ds(r0, C), :]
        gc = g_s[pl.ds(r0, C), :]
        vc = v_s[pl.ds(r0, C), :]
        b = _exact_dot(tri, gc)

        ref1 = jnp.broadcast_to(b[31:32, :], b.shape)
        ref2 = jnp.where(ri < 32, jnp.broadcast_to(b[15:16, :], b.shape),
                         jnp.broadcast_to(b[47:48, :], b.shape))
        q1 = qc * jnp.exp(jnp.minimum(b - ref1, 0.0))
        k1 = kc * jnp.exp(jnp.minimum(ref1 - b, 0.0))
        q2 = qc * jnp.exp(jnp.minimum(b - ref2, 0.0))
        k2 = kc * jnp.exp(jnp.minimum(ref2 - b, 0.0))
        qcat = jnp.concatenate([
            jnp.where(ri >= 32, q1, 0.0),
            jnp.where((ri >= 16) & (ri < 32), q2, 0.0),
            jnp.where(ri >= 48, q2, 0.0)], axis=1)
        kcat = jnp.concatenate([
            jnp.where(ri < 32, k1, 0.0),
            jnp.where(ri < 16, k2, 0.0),
            jnp.where((ri >= 32) & (ri < 48), k2, 0.0)], axis=1)
        kst = _bf(jnp.concatenate([jnp.where(khcat == h, kcat, 0.0) for h in range(GLA_H)], axis=0))
        a_off = _dot_nt(_bf(qcat), kst)

        for dlt in range(GLA_SUB):
            if dlt == 0:
                pr = qc * kc
            else:
                kd = pltpu.roll(kc, dlt, 0)
                bd = pltpu.roll(b, dlt, 0)
                pr = qc * kd * jnp.exp(jnp.minimum(b - bd, 0.0))
            p_s[dlt * C:(dlt + 1) * C, :] = _bf(pr)
        rsum = _dot(p_s[...], ebc)
        a_diag = jnp.zeros((C, 4 * C), F32)
        for dlt in range(GLA_SUB):
            a_diag = jnp.where(band == dlt, rsum[dlt * C:(dlt + 1) * C, :], a_diag)

        vst = _bf(jnp.concatenate([jnp.where(vh == h, vc, 0.0) for h in range(GLA_H)], axis=0))
        o_intra = _dot(_bf(a_off + a_diag), vst)
        st = state_s[...]
        o_inter = _dot_nt(_bf(qc * jnp.exp(b)), _bf(st))
        o_s[pl.ds(r0, C), :] = o_intra + o_inter

        blast = jnp.broadcast_to(b[C - 1:C, :], b.shape)
        kdec = kc * jnp.exp(blast - b)
        kvt = _dot_tn(_bf(vc), _bf(kdec))
        decay = jnp.exp(b[C - 1:C, :])
        state_s[...] = st * decay + jnp.where(blockmask, kvt, 0.0)
        return carry

    return chunk


def _make_gla_chunk_bounded(q_s, k_s, g_s, v_s, o_s, state_s):
    C = GLA_CHUNK
    kh = _gla_k_head(_iota((C, GLA_KW), 1))
    vh = _gla_v_head(_iota((C, GLA_VW), 1))
    tri = _bf(jnp.where(_iota((C, C), 0) >= _iota((C, C), 1), 1.0, 0.0))
    causal = _iota((C, 4 * C), 0) >= (_iota((C, 4 * C), 1) & (C - 1))
    blockmask = _gla_v_head(_iota((GLA_VW, GLA_KW), 0)) == _gla_k_head(_iota((GLA_VW, GLA_KW), 1))

    def chunk(c, carry):
        r0 = pl.multiple_of(c * C, C)
        qc = q_s[pl.ds(r0, C), :]
        kc = k_s[pl.ds(r0, C), :]
        vc = v_s[pl.ds(r0, C), :]
        b = _exact_dot(tri, g_s[pl.ds(r0, C), :])
        qe = _bf(qc * jnp.exp(b))
        ke = kc * jnp.exp(-b)
        kst = _bf(jnp.concatenate([jnp.where(kh == h, ke, 0.0) for h in range(GLA_H)], axis=0))
        a = jnp.where(causal, _dot_nt(qe, kst), 0.0)
        vst = _bf(jnp.concatenate([jnp.where(vh == h, vc, 0.0) for h in range(GLA_H)], axis=0))
        st = state_s[...]
        o_s[pl.ds(r0, C), :] = _dot(_bf(a), vst) + _dot_nt(qe, _bf(st))
        decay = jnp.exp(b[C - 1:C, :])
        kvt = _dot_tn(_bf(vc), _bf(ke * decay))
        state_s[...] = st * decay + jnp.where(blockmask, kvt, 0.0)
        return carry

    return chunk


def _mixer_a_kernel(x_ref, gpre_ref, wqk_ref, wv_ref, wr_ref, wmisc_ref, wg_ref, bg_ref, glan_ref,
                    kbd_ref, vbd_ref, wo1_ref, wo2_ref, gpost_ref,
                    xo_ref, st_ref,
                    q_s, k_s, g_s, v_s, o_s, state_s, p_s, gate_s, om_s, hook=None):
    C = GLA_CHUNK
    tm = x_ref.shape[1]
    nchunk = tm // C

    @pl.when(pl.program_id(1) == 0)
    def _():
        state_s[...] = jnp.zeros_like(state_s)

    q, k, g, v, gate, o_mem = _proj_a(x_ref[0], gpre_ref, wqk_ref, wv_ref, wr_ref, wmisc_ref, wg_ref, bg_ref,
                                      glan_ref, (kbd_ref.at[0], vbd_ref.at[0]))
    q_s[...] = q
    k_s[...] = k
    g_s[...] = g
    v_s[...] = v
    gate_s[...] = gate
    om_s[...] = _bf(o_mem)
    if hook is not None:
        hook()

    total = jnp.sum(g.reshape(nchunk, C, GLA_KW), axis=1)
    bounded = jnp.min(total) > -GLA_SAFE_DECAY

    @pl.when(bounded)
    def _():
        lax.fori_loop(0, nchunk, _make_gla_chunk_bounded(q_s, k_s, g_s, v_s, o_s, state_s), 0, unroll=4)

    @pl.when(jnp.logical_not(bounded))
    def _():
        lax.fori_loop(0, nchunk, _make_gla_chunk(q_s, k_s, g_s, v_s, o_s, state_s, p_s), 0)

    st_ref[0] = state_s[...]
    o_main = _gla_out_gate(o_s[...], gate_s[...])
    xo_ref[0] = _mix_residual(x_ref[0], o_main, om_s[...], wo1_ref, wo2_ref, gpost_ref)


def _const_spec(a):
    nd = a.ndim
    return pl.BlockSpec(a.shape, lambda *_: (0,) * nd, pipeline_mode=pl.Buffered(1))


def _mixer_a_call(x, wa, kbd, vbd, make_guests):
    nb, seq, _ = x.shape
    tm = TM_PROMPT
    nt = seq // tm
    weights = [wa["gpre"], wa["wqk"], wa["wv"], wa["wr"], wa["wmisc"], wa["wg"], wa["bg"], wa["glan"]]
    tail = [wa["wo1"], wa["wo2"], wa["gpost"]]
    return _call_with_guests(
        _mixer_a_kernel,
        grid=(nb, nt),
        step_of=lambda b, t: b * nt + t,
        inputs=[x, *weights, kbd, vbd, *tail],
        in_specs=([pl.BlockSpec((1, tm, D), lambda b, t: (b, t, 0))]
                  + [_const_spec(w) for w in weights]
                  + [pl.BlockSpec((1, MEM_W, MEM_H * N_MEM), lambda b, t: (b, 0, 0)),
                     pl.BlockSpec((1, MEM_H * N_MEM, MEM_W), lambda b, t: (b, 0, 0))]
                  + [_const_spec(w) for w in tail]),
        out_specs=[pl.BlockSpec((1, tm, D), lambda b, t: (b, t, 0)),
                   pl.BlockSpec((1, GLA_VW, GLA_KW), lambda b, t: (b, 0, 0))],
        out_shape=[jax.ShapeDtypeStruct((nb, seq, D), F32),
                   jax.ShapeDtypeStruct((nb, GLA_VW, GLA_KW), F32)],
        scratch=[
            pltpu.VMEM((tm, GLA_KW), F32), pltpu.VMEM((tm, GLA_KW), F32), pltpu.VMEM((tm, GLA_KW), F32),
            pltpu.VMEM((tm, GLA_VW), F32), pltpu.VMEM((tm, GLA_VW), F32),
            pltpu.VMEM((GLA_VW, GLA_KW), F32),
            pltpu.VMEM((GLA_SUB * GLA_CHUNK, GLA_KW), BF16),
            pltpu.VMEM((tm, GLA_VW), F32), pltpu.VMEM((tm, MEM_W), BF16),
        ],
        guests=make_guests(nb * nt),
        name="mixer_a_prompt",
    )


def _ffn_rows(x, gpre_ref, wup_ref, wdn_ref, gpost_ref, acc_ref):
    h = _bf(_rms(x, gpre_ref[0]))
    for j in range(D_FF // FF_CHUNK):
        cols = slice(j * FF_CHUNK, (j + 1) * FF_CHUNK)
        u = jnp.maximum(_dot(h, _bf(wup_ref[0, :, cols])), 0.0)
        d = _dot(_bf(u * u), _bf(wdn_ref[0, cols, :]))
        if j == 0:
            acc_ref[...] = d
        else:
            acc_ref[...] += d
    return x + _rms(acc_ref[...], gpost_ref[0])


def _ffn_kernel(xp_ref, xs_ref, gpre_ref, wup_ref, wdn_ref, gpost_ref, yp_ref, ys_ref, acc_s):
    i = pl.program_id(0)
    last = pl.num_programs(0) - 1

    @pl.when(i < last)
    def _():
        yp_ref[...] = _ffn_rows(xp_ref[...], gpre_ref, wup_ref, wdn_ref, gpost_ref, acc_s)

    @pl.when(i == last)
    def _():
        ns = xs_ref.shape[0]
        ys_ref[...] = _ffn_rows(xs_ref[...], gpre_ref, wup_ref, wdn_ref, gpost_ref, acc_s.at[0:ns, :])


def _ffn_call(xp, xs, norm_pre, w_up, w_down, norm_post, layer):
    n, ns = xp.shape[0], xs.shape[0]
    tm = TM_PROMPT
    nt = n // tm
    gains = [a.reshape(a.shape[0], 1, D) for a in (norm_pre, norm_post)]
    gspec = pl.BlockSpec((1, 1, D), lambda i: (layer, 0, 0))
    wspec = lambda w: pl.BlockSpec((1,) + w.shape[1:], lambda i: (layer, 0, 0), pipeline_mode=pl.Buffered(1))
    tile = pl.BlockSpec((tm, D), lambda i: (jnp.minimum(i, nt - 1), 0))
    return pl.pallas_call(
        _ffn_kernel,
        grid=(nt + 1,),
        in_specs=[tile, _full_spec(xs), gspec, wspec(w_up), wspec(w_down), gspec],
        out_specs=[tile, _full_spec(xs)],
        out_shape=[jax.ShapeDtypeStruct((n, D), F32), jax.ShapeDtypeStruct((ns, D), F32)],
        scratch_shapes=[pltpu.VMEM((tm, D), F32)],
        compiler_params=_params(("arbitrary",)),
        name="ffn",
    )(xp, xs, gains[0], w_up, w_down, gains[1])


def _t5_bucket(dist):
    max_exact = N_BUCKETS // 2
    n = np.maximum(dist, 0)
    nf = np.maximum(n, 1).astype(np.float32)
    large = max_exact + (np.log(nf / np.float32(max_exact)) / np.float32(math.log(MAX_DISTANCE / max_exact))
                         * np.float32(N_BUCKETS - max_exact)).astype(np.int32)
    large = np.minimum(large, N_BUCKETS - 1)
    return np.where(n < max_exact, n, large)


def _swa_bias_tables():
    qi = np.arange(WINDOW)[:, None] + WINDOW
    kj = np.arange(2 * WINDOW)[None, :]
    dist = qi - kj
    valid = (dist >= 0) & (dist < WINDOW)
    return np.where(valid, _t5_bucket(dist), -1).astype(np.int32)


def _mixer_b_kernel(rb_ref, sink_ref,
                    x_ref, bkt_ref, gkv_ref, wkv_ref, gpre_ref, wq_ref, wqm_ref,
                    kbd_ref, vbd_ref, wo1_ref, wo2_ref, gpost_ref,
                    xo_ref, kc_ref, vc_ref,
                    kbuf, vbuf, q_s, o_s, bias_s, qm_s, hook=None):
    W = WINDOW
    tm = x_ref.shape[1]
    bb = pl.program_id(0)
    t = pl.program_id(1)
    nheads = SWA_G * SWA_KVH

    @pl.when((bb == 0) & (t == 0))
    def _():
        bkt = bkt_ref[...]
        own = _iota((W, 2 * W), 1) >= W
        for i in range(nheads):
            def add_bucket(n, acc):
                return jnp.where(bkt == n, rb_ref[n, i], acc)
            tab = lax.fori_loop(0, N_BUCKETS, add_bucket, jnp.zeros((W, 2 * W), F32))
            tab = jnp.where(bkt < 0, NEG_INF, tab)
            bias_s[0, i] = tab
            bias_s[1, i] = jnp.where(own, tab, NEG_INF)

    @pl.when(t == 0)
    def _():
        kbuf[0:W, :] = jnp.zeros((W, SWA_KVW), F32)
        vbuf[0:W, :] = jnp.zeros((W, SWA_KVW), F32)

    x = x_ref[0]
    xn = x * lax.rsqrt(jnp.mean(x * x, axis=-1, keepdims=True) + EPS)
    kv = _dot(_bf(xn * gkv_ref[...]), wkv_ref[...])
    kbuf[W:W + tm, :] = kv[:, :SWA_KVW]
    vbuf[W:W + tm, :] = kv[:, SWA_KVW:]
    kc_ref[0] = kv[tm - W:, :SWA_KVW]
    vc_ref[0] = kv[tm - W:, SWA_KVW:]

    h = _bf(xn * gpre_ref[...])
    q_s[...] = _dot(h, wq_ref[...]) * (SWA_HD ** -0.5)
    qm_s[...] = _dot(h, wqm_ref[...])
    if hook is not None:
        hook()

    lane_head = _iota((W, SWA_KVW), 1) >> 6
    key_head = _iota((2 * W, SWA_KVW), 1) >> 6

    def block(j, carry):
        r0 = pl.multiple_of(j * W, W)
        kb = kbuf[pl.ds(r0, 2 * W), :]
        vb = vbuf[pl.ds(r0, 2 * W), :]
        tab = jnp.where((j == 0) & (t == 0), 1, 0)
        qrows = []
        for gi in range(SWA_G):
            qg = q_s[pl.ds(r0, W), gi * SWA_KVW:(gi + 1) * SWA_KVW]
            for hh in range(SWA_KVH):
                qrows.append(jnp.where(lane_head == hh, qg, 0.0))
        s_all = _dot_nt(_bf(jnp.concatenate(qrows, axis=0)), _bf(kb))
        vst = _bf(jnp.concatenate([jnp.where(key_head == hh, vb, 0.0) for hh in range(SWA_KVH)], axis=0))
        for gi in range(SWA_G):
            ps = []
            for hh in range(SWA_KVH):
                i = gi * SWA_KVH + hh
                s = s_all[i * W:(i + 1) * W, :] + bias_s[tab, i]
                sink = sink_ref[i]
                mx = jnp.maximum(jnp.max(s, axis=-1, keepdims=True), sink)
                p = jnp.exp(s - mx)
                p = p / (jnp.sum(p, axis=-1, keepdims=True) + jnp.exp(sink - mx))
                ps.append(_bf(p))
            o_s[pl.ds(r0, W), gi * SWA_KVW:(gi + 1) * SWA_KVW] = _dot(jnp.concatenate(ps, axis=1), vst)
        return carry

    lax.fori_loop(0, tm // W, block, 0, unroll=2)

    kbuf[0:W, :] = kbuf[tm:tm + W, :]
    vbuf[0:W, :] = vbuf[tm:tm + W, :]

    o_mem = _mem_pv(_mem_probs(qm_s[...], kbd_ref.at[0]), vbd_ref.at[0])
    xo_ref[0] = _mix_residual(x_ref[0], o_s[...], o_mem, wo1_ref, wo2_ref, gpost_ref)


def _mixer_b_call(x, wb, kbd, vbd, make_guests):
    nb, seq, _ = x.shape
    tm = TM_PROMPT
    nt = seq // tm
    bkt = jnp.asarray(_swa_bias_tables())
    head = [bkt, wb["gkv"], wb["wkv"], wb["gpre"], wb["wq"], wb["wqm"]]
    tail = [wb["wo1"], wb["wo2"], wb["gpost"]]
    smem = pl.BlockSpec(memory_space=pltpu.SMEM)
    return _call_with_guests(
        _mixer_b_kernel,
        grid=(nb, nt),
        step_of=lambda b, t: b * nt + t,
        inputs=[wb["rb"], wb["sinks"], x, *head, kbd, vbd, *tail],
        in_specs=([smem, smem, pl.BlockSpec((1, tm, D), lambda b, t: (b, t, 0))]
                  + [_const_spec(w) for w in head]
                  + [pl.BlockSpec((1, MEM_W, MEM_H * N_MEM), lambda b, t: (b, 0, 0)),
                     pl.BlockSpec((1, MEM_H * N_MEM, MEM_W), lambda b, t: (b, 0, 0))]
                  + [_const_spec(w) for w in tail]),
        out_specs=[pl.BlockSpec((1, tm, D), lambda b, t: (b, t, 0)),
                   pl.BlockSpec((1, WINDOW, SWA_KVW), lambda b, t: (b, 0, 0)),
                   pl.BlockSpec((1, WINDOW, SWA_KVW), lambda b, t: (b, 0, 0))],
        out_shape=[jax.ShapeDtypeStruct((nb, seq, D), F32),
                   jax.ShapeDtypeStruct((nb, WINDOW, SWA_KVW), F32),
                   jax.ShapeDtypeStruct((nb, WINDOW, SWA_KVW), F32)],
        scratch=[
            pltpu.VMEM((tm + WINDOW, SWA_KVW), F32), pltpu.VMEM((tm + WINDOW, SWA_KVW), F32),
            pltpu.VMEM((tm, SWA_QW), F32), pltpu.VMEM((tm, SWA_QW), F32),
            pltpu.VMEM((2, SWA_G * SWA_KVH, WINDOW, 2 * WINDOW), F32),
            pltpu.VMEM((tm, MEM_W), F32),
        ],
        guests=make_guests(nb * nt),
        name="mixer_b_prompt",
    )


def _row_to_col(row, eye):
    return jnp.sum(jnp.where(eye, jnp.broadcast_to(row, eye.shape), 0.0), axis=1, keepdims=True)


def _eye(n):
    return _iota((n, n), 0) == _iota((n, n), 1)


def _pre_a_kernel(x_ref, gpre_ref, wqk_ref, wv_ref, wr_ref, wmisc_ref, wg_ref, bg_ref, glan_ref,
                  q_ref, k_ref, g_ref, v_ref, gate_ref, qm_ref):
    q, k, g, v, gate, qm = _proj_a(x_ref[...], gpre_ref, wqk_ref, wv_ref, wr_ref, wmisc_ref, wg_ref, bg_ref,
                                   glan_ref)
    q_ref[...] = q
    k_ref[...] = k
    g_ref[...] = g
    v_ref[...] = v
    gate_ref[...] = gate
    qm_ref[...] = qm


def _pre_a_call(x2d, wa):
    n = x2d.shape[0]
    ws = [wa["gpre"], wa["wqk"], wa["wv"], wa["wr"], wa["wmisc"], wa["wg"], wa["bg"], wa["glan"]]
    shapes = [(n, GLA_KW), (n, GLA_KW), (n, GLA_KW), (n, GLA_VW), (n, GLA_VW), (n, MEM_W)]
    return pl.pallas_call(
        _pre_a_kernel,
        grid=(1,),
        in_specs=[_full_spec(x2d)] + [_full_spec(w) for w in ws],
        out_specs=[pl.BlockSpec(s, lambda i: (0, 0)) for s in shapes],
        out_shape=[jax.ShapeDtypeStruct(s, F32) for s in shapes],
        compiler_params=_params(("arbitrary",)),
        name="pre_a_sample",
    )(x2d, *ws)


def _gla_step_guest(qt, kt, gt, vt, state5, nsteps):
    n = qt.shape[1]
    nblk = GLA_DK // GLA_DK_BLOCK
    nact = GLA_H * nblk
    assert nact <= nsteps
    act = lambda s: jnp.minimum(s, nact - 1)

    def pre(step, qt_ref, kt_ref, gt_ref, vt_ref, s_ref, so_ref, ot_ref):
        @pl.when(step < nact)
        def _():
            vt_blk = vt_ref[...]
            acc = jnp.zeros_like(vt_blk)
            for d in range(GLA_DK_BLOCK):
                s_new = jnp.exp(gt_ref[d:d + 1, :]) * s_ref[0, 0, d] + kt_ref[d:d + 1, :] * vt_blk
                so_ref[0, 0, d] = s_new
                acc = acc + qt_ref[d:d + 1, :] * s_new
            first = step % nblk == 0

            @pl.when(first)
            def _():
                ot_ref[...] = acc

            @pl.when(jnp.logical_not(first))
            def _():
                ot_ref[...] += acc

    rows = ((GLA_DK_BLOCK, n), lambda s: (act(s), 0))
    head = ((GLA_DV, n), lambda s: (act(s) // nblk, 0))
    st = ((1, 1, GLA_DK_BLOCK, GLA_DV, n), lambda s: (0, act(s) // nblk, act(s) % nblk, 0, 0))
    return dict(inputs=[qt, kt, gt, vt, state5], in_specs=[rows, rows, rows, head, st],
                out_shape=[jax.ShapeDtypeStruct(state5.shape, F32), jax.ShapeDtypeStruct((GLA_VW, n), F32)],
                out_specs=[st, head], scratch=[], pre=pre, post=lambda step, state, *refs: None)


def _mem_step_pre(step, qm_ref, mk_ref, mv_ref, o_ref):
    own = (_iota((8, MEM_W), 1) >> 6) == _iota((8, MEM_W), 0)
    ps = []
    for i in range(qm_ref.shape[0]):
        q8 = jnp.where(own, jnp.broadcast_to(qm_ref[i], (8, MEM_W)), 0.0)
        s = _dot(_bf(q8), _bf(mk_ref[0, i].reshape(MEM_W, N_MEM))) * (MEM_HD ** -0.5)
        e = jnp.exp(s - jnp.max(s, axis=1, keepdims=True))
        ps.append(_bf(e / jnp.sum(e, axis=1, keepdims=True)))
    return ps


def _mem_step_post(step, ps, qm_ref, mk_ref, mv_ref, o_ref):
    own = (_iota((8, MEM_W), 1) >> 6) == _iota((8, MEM_W), 0)
    for i, p in enumerate(ps):
        res = _dot_nt(p, _bf(mv_ref[0, i].reshape(MEM_W, N_MEM)))
        o_ref[i] = jnp.sum(jnp.where(own, res, 0.0), axis=0, keepdims=True)


def _mem_step_guest(qm, mk5, mv5, layer, nsteps):
    n = qm.shape[0]
    rb = n // nsteps
    blk = ((1, rb, MEM_H, MEM_HD, N_MEM), lambda s: (layer, s, 0, 0, 0))
    rows = ((rb, 1, MEM_W), lambda s: (s, 0, 0))
    return dict(inputs=[qm.reshape(n, 1, MEM_W), mk5, mv5], in_specs=[rows, blk, blk],
                out_shape=[jax.ShapeDtypeStruct((n, 1, MEM_W), F32)], out_specs=[rows], scratch=[],
                pre=_mem_step_pre, post=_mem_step_post)


def _post_a_kernel(x_ref, o_ref, gate_ref, om_ref, wo1_ref, wo2_ref, gpost_ref, xo_ref):
    o_main = _gla_out_gate(o_ref[...], gate_ref[...])
    xo_ref[...] = _mix_residual(x_ref[...], o_main, om_ref[...], wo1_ref, wo2_ref, gpost_ref)


def _post_a_call(x2d, o, gate, om, wa):
    args = [x2d, o, gate, om, wa["wo1"], wa["wo2"], wa["gpost"]]
    return pl.pallas_call(
        _post_a_kernel,
        grid=(1,),
        in_specs=[_full_spec(a) for a in args],
        out_specs=_full_spec(x2d),
        out_shape=jax.ShapeDtypeStruct(x2d.shape, F32),
        compiler_params=_params(("arbitrary",)),
        name="post_a_sample",
    )(*args)


def _pre_b_kernel(x_ref, gkv_ref, wkv_ref, gpre_ref, wq_ref, wqm_ref, ks_ref, vs_ref, q_ref, qm_ref):
    x = x_ref[...]
    kv = _dot(_bf(_rms(x, gkv_ref[...])), wkv_ref[...])
    ks_ref[...] = kv[:, :SWA_KVW]
    vs_ref[...] = kv[:, SWA_KVW:]
    h = _bf(_rms(x, gpre_ref[...]))
    q_ref[...] = _dot(h, wq_ref[...])
    qm_ref[...] = _dot(h, wqm_ref[...])


def _pre_b_call(x2d, wb):
    n = x2d.shape[0]
    ws = [wb["gkv"], wb["wkv"], wb["gpre"], wb["wq"], wb["wqm"]]
    widths = [SWA_KVW, SWA_KVW, SWA_QW, MEM_W]
    return pl.pallas_call(
        _pre_b_kernel,
        grid=(1,),
        in_specs=[_full_spec(x2d)] + [_full_spec(w) for w in ws],
        out_specs=[pl.BlockSpec((n, w), lambda i: (0, 0)) for w in widths],
        out_shape=[jax.ShapeDtypeStruct((n, w), F32) for w in widths],
        compiler_params=_params(("arbitrary",)),
        name="pre_b_sample",
    )(x2d, *ws)


def _sample_buckets():
    dist = (WINDOW - 1) - np.arange(WINDOW)
    return _t5_bucket(dist).astype(np.int32).reshape(1, WINDOW)


def _swa_step_pre(step, sink_ref, rb_ref, bkt_ref, kc_ref, vc_ref, kst_ref, vst_ref, q_ref,
                  kn_ref, vn_ref, o_ref, bias_s, sink_s, s_s, p_s):
    W = WINDOW
    R = SWA_KVH * 8

    @pl.when(step == 0)
    def _():
        bkt = bkt_ref[...]
        rid = _iota((R, W), 0)
        bias = jnp.zeros((R, W), F32)
        sink = jnp.zeros((R, W), F32)
        for h in range(SWA_KVH):
            for g in range(SWA_G):
                idx = g * SWA_KVH + h
                def add_bucket(n, acc):
                    return jnp.where(bkt == n, rb_ref[n, idx], acc)
                brow = lax.fori_loop(0, N_BUCKETS, add_bucket, jnp.zeros((1, W), F32))
                bias = jnp.where(rid == h * 8 + g, brow, bias)
                sink = jnp.where(rid == h * 8 + g, sink_ref[idx], sink)
        bias_s[...] = bias
        sink_s[...] = sink

    rb = q_ref.shape[0]
    base = step * rb
    last = _iota((SWA_KVW, W), 1) == W - 1
    own = (_iota((R, SWA_KVW), 1) >> 6) == (_iota((R, SWA_KVW), 0) >> 3)
    kst = kst_ref[...]
    vst = vst_ref[...]
    for i in range(rb):
        shift = W - 1 - (base + i)
        kn = jnp.where(last, pltpu.roll(kst, shift, 1), pltpu.roll(kc_ref[i].reshape(SWA_KVW, W), W - 1, 1))
        vn = jnp.where(last, pltpu.roll(vst, shift, 1), pltpu.roll(vc_ref[i].reshape(SWA_KVW, W), W - 1, 1))
        kn_ref[i] = kn.reshape(SWA_KVH, SWA_HD, W)
        vn_ref[i] = vn.reshape(SWA_KVH, SWA_HD, W)
        q32 = jnp.where(own, jnp.concatenate([q_ref[i]] * SWA_KVH, axis=0), 0.0)
        s_s[i * R:(i + 1) * R, :] = _dot(_bf(q32), _bf(kn))
    s = s_s[...] * (SWA_HD ** -0.5) + jnp.concatenate([bias_s[...]] * rb, axis=0)
    sink = jnp.concatenate([sink_s[...]] * rb, axis=0)
    mx = jnp.maximum(jnp.max(s, axis=1, keepdims=True), sink)
    p = jnp.exp(s - mx)
    p_s[...] = _bf(p / (jnp.sum(p, axis=1, keepdims=True) + jnp.exp(sink - mx)))


def _swa_step_post(step, _, sink_ref, rb_ref, bkt_ref, kc_ref, vc_ref, kst_ref, vst_ref, q_ref,
                   kn_ref, vn_ref, o_ref, bias_s, sink_s, s_s, p_s):
    W = WINDOW
    R = SWA_KVH * 8
    own = (_iota((R, SWA_KVW), 1) >> 6) == (_iota((R, SWA_KVW), 0) >> 3)
    for i in range(q_ref.shape[0]):
        vn = vn_ref[i].reshape(SWA_KVW, W)
        res = jnp.where(own, _dot_nt(p_s[i * R:(i + 1) * R, :], _bf(vn)), 0.0)
        o_ref[i] = res[0:8] + res[8:16] + res[16:24] + res[24:32]


def _swa_step_guest(kc4, vc4, ks, vs, q, wb, nsteps):
    n = q.shape[0]
    rb = n // nsteps
    bkt = jnp.asarray(_sample_buckets())
    q8 = jnp.pad(q.reshape(n, SWA_G, SWA_KVW), ((0, 0), (0, 8 - SWA_G), (0, 0)))
    row3 = lambda r, w: ((rb, r, w), lambda s: (s, 0, 0))
    cache = ((rb, SWA_KVH, SWA_HD, WINDOW), lambda s: (s, 0, 0, 0))
    whole = ((SWA_KVW, n), lambda s: (0, 0))
    smem = pl.BlockSpec(memory_space=pltpu.SMEM)
    return dict(inputs=[wb["sinks"], wb["rb"], bkt, kc4, vc4, ks.T, vs.T, q8],
                in_specs=[smem, smem, _full_spec(bkt), cache, cache, whole, whole, row3(8, SWA_KVW)],
                out_shape=[jax.ShapeDtypeStruct(kc4.shape, F32), jax.ShapeDtypeStruct(vc4.shape, F32),
                           jax.ShapeDtypeStruct((n, 8, SWA_KVW), F32)],
                out_specs=[cache, cache, row3(8, SWA_KVW)],
                scratch=[pltpu.VMEM((SWA_KVH * 8, WINDOW), F32), pltpu.VMEM((SWA_KVH * 8, WINDOW), F32),
                         pltpu.VMEM((rb * SWA_KVH * 8, WINDOW), F32),
                         pltpu.VMEM((rb * SWA_KVH * 8, WINDOW), BF16)],
                pre=_swa_step_pre, post=_swa_step_post)


def _post_b_kernel(x_ref, o_ref, om_ref, wo1_ref, wo2_ref, gpost_ref, xo_ref):
    xo_ref[...] = _mix_residual(x_ref[...], o_ref[...], om_ref[...], wo1_ref, wo2_ref, gpost_ref)


def _post_b_call(x2d, o, om, wb):
    args = [x2d, o, om, wb["wo1"], wb["wo2"], wb["gpost"]]
    return pl.pallas_call(
        _post_b_kernel,
        grid=(1,),
        in_specs=[_full_spec(a) for a in args],
        out_specs=_full_spec(x2d),
        out_shape=jax.ShapeDtypeStruct(x2d.shape, F32),
        compiler_params=_params(("arbitrary",)),
        name="post_b_sample",
    )(*args)


def _prep_weights(norm_mix_pre, norm_mix_post, w_in_a, w_gate_up, b_gate,
                  gla_norm, w_in_b, sinks, norm_kv, w_kv, rel_bias, w_out):
    row = lambda g: g.reshape(1, -1)
    wa_in = w_in_a[0]
    c_v = 2 * GLA_KW
    c_r = c_v + GLA_VW
    c_g = c_r + GLA_VW
    c_m = c_g + GATE_RANK
    wmisc = jnp.concatenate([wa_in[:, c_g:c_m], jnp.zeros((D, 128 - GATE_RANK), F32), wa_in[:, c_m:]], axis=1)
    wg = jnp.zeros((128, GLA_KW), F32).at[:GATE_RANK].set(w_gate_up[0])
    wa = dict(
        gpre=row(norm_mix_pre[0]), wqk=_bf(wa_in[:, :c_v]), wv=_bf(wa_in[:, c_v:c_r]), wr=_bf(wa_in[:, c_r:c_g]),
        wmisc=_bf(wmisc), wg=_bf(wg), bg=row(b_gate[0]), glan=row(jnp.tile(gla_norm[0], GLA_H)),
        wo1=_bf(w_out[0][:GLA_VW]), wo2=_bf(w_out[0][GLA_VW:]), gpost=row(norm_mix_post[0]))
    wb_in = w_in_b[0]
    wq = wb_in[:, :SWA_QW].reshape(D, SWA_KVH, SWA_G, SWA_HD).transpose(0, 2, 1, 3).reshape(D, SWA_QW)
    wo1 = w_out[1][:SWA_QW].reshape(SWA_KVH, SWA_G, SWA_HD, D).transpose(1, 0, 2, 3).reshape(SWA_QW, D)
    rb = rel_bias.reshape(N_BUCKETS, SWA_KVH, SWA_G).transpose(0, 2, 1).reshape(N_BUCKETS, SWA_G * SWA_KVH)
    sk = sinks[0].reshape(SWA_KVH, SWA_G).T.reshape(SWA_G * SWA_KVH)
    wb = dict(
        gkv=row(norm_kv), wkv=_bf(w_kv), gpre=row(norm_mix_pre[1]), wq=_bf(wq), wqm=_bf(wb_in[:, SWA_QW:]),
        wo1=_bf(wo1), wo2=_bf(w_out[1][SWA_QW:]), gpost=row(norm_mix_post[1]), rb=rb, sinks=sk)
    return wa, wb


def kernel(x_prompt, x_sample, state_gla, cache_swa_k, cache_swa_v, cache_mem_k, cache_mem_v, mem_prompt,
           norm_mix_pre, norm_mix_post, norm_ffn_pre, norm_ffn_post, norm_mem, w_mem_kv, w_in_a, w_gate_up,
           b_gate, gla_norm, w_in_b, sinks, norm_kv, w_kv, rel_bias, w_out, w_ffn_up, w_ffn_down):
    wa, wb = _prep_weights(norm_mix_pre, norm_mix_post, w_in_a, w_gate_up, b_gate, gla_norm, w_in_b, sinks,
                           norm_kv, w_kv, rel_bias, w_out)
    ffn = lambda xp, xs_, l: _ffn_call(xp, xs_, norm_ffn_pre, w_ffn_up, w_ffn_down, norm_ffn_post, l)
    nb, seq, _ = x_prompt.shape
    ns = x_sample.shape[0]

    xs = x_sample.reshape(ns, D)
    state5 = jnp.transpose(state_gla, (0, 2, 3, 4, 1))
    mk5 = jnp.transpose(cache_mem_k, (0, 1, 3, 4, 2))
    mv5 = jnp.transpose(cache_mem_v, (0, 1, 3, 4, 2))
    kc4 = jnp.transpose(cache_swa_k, (0, 2, 3, 1))
    vc4 = jnp.transpose(cache_swa_v, (0, 2, 3, 1))

    mkt, mvt, kbd, vbd = _memkv_call(mem_prompt, norm_mem, w_mem_kv)
    q, k, g, v, gate, qm = _pre_a_call(xs, wa)
    (x1, st), ((state5_new, ot), (om,)) = _mixer_a_call(
        x_prompt, wa, kbd[0], vbd[0],
        lambda nsteps: [_gla_step_guest(q.T, k.T, g.T, v.T, state5, nsteps),
                        _mem_step_guest(qm, mk5, mv5, 0, nsteps)])
    xs1 = _post_a_call(xs, ot.T, gate, om.reshape(ns, MEM_W), wa)
    x2, xs2 = ffn(x1.reshape(nb * seq, D), xs1, 0)
    ks, vs, qb, qmb = _pre_b_call(xs2, wb)
    (x3, kc, vc), ((kn4, vn4, o8), (omb,)) = _mixer_b_call(
        x2.reshape(nb, seq, D), wb, kbd[1], vbd[1],
        lambda nsteps: [_swa_step_guest(kc4, vc4, ks, vs, qb, wb, nsteps),
                        _mem_step_guest(qmb, mk5, mv5, 1, nsteps)])
    xs3 = _post_b_call(xs2, o8[:, :SWA_G].reshape(ns, SWA_QW), omb.reshape(ns, MEM_W), wb)
    y_prompt, y_sample = ffn(x3.reshape(nb * seq, D), xs3, 1)
    y_prompt = y_prompt.reshape(nb, seq, D)
    y_sample = y_sample.reshape(ns, 1, D)
    st4 = st.reshape(nb, GLA_H, GLA_DV, GLA_H, GLA_DK)
    state_prompt = jnp.stack([st4[:, h, :, h, :] for h in range(GLA_H)], axis=1).transpose(0, 1, 3, 2)[None]
    to_mem = lambda t: t.reshape(2, nb, MEM_H, MEM_HD, N_MEM).transpose(0, 1, 4, 2, 3)
    swa_shape = (nb, WINDOW, SWA_KVH, SWA_HD)

    return (y_prompt, y_sample, state_prompt,
            jnp.transpose(state5_new, (0, 4, 1, 2, 3)),
            kc.reshape(swa_shape), vc.reshape(swa_shape),
            jnp.transpose(kn4, (0, 3, 1, 2)), jnp.transpose(vn4, (0, 3, 1, 2)),
            to_mem(mkt), to_mem(mvt))
```

```python
import functools
import math

import numpy as np
import jax
import jax.numpy as jnp
from jax import lax
from jax.experimental import pallas as pl
from jax.experimental.pallas import tpu as pltpu

F32 = jnp.float32
BF16 = jnp.bfloat16

D = 1024
D_FF = 4 * D
N_MEM = 256
MEM_H = 4
MEM_HD = 64
MEM_W = MEM_H * MEM_HD
GLA_H = 4
GLA_DK = 96
GLA_DV = 192
GLA_KW = GLA_H * GLA_DK
GLA_VW = GLA_H * GLA_DV
GATE_RANK = 16
GATE_NORM = 16.0
SWA_HD = 64
SWA_KVH = 4
SWA_G = 3
SWA_QW = SWA_KVH * SWA_G * SWA_HD
SWA_KVW = SWA_KVH * SWA_HD
WINDOW = 128
N_BUCKETS = 32
MAX_DISTANCE = 128
EPS = 1e-6

GLA_CHUNK = 64
GLA_SUB = 16
GLA_SAFE_DECAY = 60.0
GLA_GROUP = 4
SWA_BLOCKS_PER_TRIP = 2
TM_PROMPT = 512
FF_CHUNK = 512
GLA_DK_BLOCK = 16
V7X_VMEM_LIMIT = 56 * 1024 * 1024
NEG_INF = float("-inf")


def _bf(x):
    return x.astype(BF16)


def _dot(a, b):
    return jnp.dot(a, b, preferred_element_type=F32)


def _dot_nt(a, b):
    return lax.dot_general(a, b, (((1,), (1,)), ((), ())), preferred_element_type=F32)


def _dot_tn(a, b):
    return lax.dot_general(a, b, (((0,), (0,)), ((), ())), preferred_element_type=F32)


def _rms(x, g):
    return x * lax.rsqrt(jnp.mean(x * x, axis=-1, keepdims=True) + EPS) * g


def _split3(x):
    x1 = _bf(x)
    r1 = x - x1.astype(F32)
    x2 = _bf(r1)
    x3 = _bf(r1 - x2.astype(F32))
    return x1, x2, x3


def _exact_dot(sel, x):
    x1, x2, x3 = _split3(x)
    return _dot(sel, x1) + _dot(sel, x2) + _dot(sel, x3)


def _log_sigmoid(z):
    return jnp.minimum(z, 0.0) - jnp.log1p(jnp.exp(-jnp.abs(z)))


def _silu(z):
    return z * (1.0 / (1.0 + jnp.exp(-z)))


def _iota(shape, dim):
    return lax.broadcasted_iota(jnp.int32, shape, dim)


def _gla_k_head(lane):
    one = jnp.int32(1)
    zero = jnp.int32(0)
    return (jnp.where(lane >= GLA_DK, one, zero) + jnp.where(lane >= 2 * GLA_DK, one, zero)
            + jnp.where(lane >= 3 * GLA_DK, one, zero))


def _gla_v_head(lane):
    one = jnp.int32(1)
    zero = jnp.int32(0)
    return (jnp.where(lane >= GLA_DV, one, zero) + jnp.where(lane >= 2 * GLA_DV, one, zero)
            + jnp.where(lane >= 3 * GLA_DV, one, zero))


def _full_spec(a):
    nd = a.ndim
    return pl.BlockSpec(a.shape, lambda *_: (0,) * nd)


def _params(sem):
    return pltpu.CompilerParams(dimension_semantics=sem, vmem_limit_bytes=V7X_VMEM_LIMIT)


def _call_with_guests(host_kernel, grid, step_of, inputs, in_specs, out_shape, out_specs, scratch, guests, name):
    def to_spec(s):
        if isinstance(s, pl.BlockSpec):
            return s
        block, fn = s
        return pl.BlockSpec(block, lambda *ids: fn(step_of(*ids)))

    counts = [(len(inputs), len(out_shape), len(scratch))]
    counts += [(len(g["inputs"]), len(g["out_shape"]), len(g["scratch"])) for g in guests]

    def kern(*refs):
        refs = list(refs)
        parts = [[refs.pop(0) for _ in range(c[k])] for k in range(3) for c in counts]
        n = len(counts)
        ins, outs, scrs = parts[:n], parts[n:2 * n], parts[2 * n:]
        step = step_of(*[pl.program_id(a) for a in range(len(grid))])
        grefs = [(*a, *b, *c) for a, b, c in zip(ins[1:], outs[1:], scrs[1:])]
        states = [g["pre"](step, *r) for g, r in zip(guests, grefs)]

        def hook():
            for g, st, r in zip(guests, states, grefs):
                g["post"](step, st, *r)

        host_kernel(*ins[0], *outs[0], *scrs[0], hook=hook)

    res = pl.pallas_call(
        kern,
        grid=grid,
        in_specs=list(in_specs) + [to_spec(s) for g in guests for s in g["in_specs"]],
        out_specs=list(out_specs) + [to_spec(s) for g in guests for s in g["out_specs"]],
        out_shape=list(out_shape) + [s for g in guests for s in g["out_shape"]],
        scratch_shapes=list(scratch) + [s for g in guests for s in g["scratch"]],
        compiler_params=_params(("arbitrary",) * len(grid)),
        name=name,
    )(*inputs, *[a for g in guests for a in g["inputs"]])
    res = list(res)
    split = []
    for c in counts:
        split.append([res.pop(0) for _ in range(c[1])])
    return split[0], split[1:]


def _mem_probs(qm, kbd_ref):
    s = _dot(_bf(qm), kbd_ref[...]) * (MEM_HD ** -0.5)
    ps = []
    for h in range(MEM_H):
        sh = s[:, h * N_MEM:(h + 1) * N_MEM]
        e = jnp.exp(sh - jnp.max(sh, axis=-1, keepdims=True))
        ps.append(_bf(e / jnp.sum(e, axis=-1, keepdims=True)))
    return ps


def _mem_pv(ps, vbd_ref):
    out = None
    for h in range(MEM_H):
        t = _dot(ps[h], vbd_ref[h * N_MEM:(h + 1) * N_MEM, :])
        out = t if out is None else out + t
    return out


def _memkv_kernel(mem_ref, g_ref, w_ref, k_ref, v_ref, kbd_ref, vbd_ref):
    h = _bf(_rms(mem_ref[0], g_ref[0]))
    kv = _dot(h, w_ref[0])
    k = kv[:, :MEM_W]
    v = kv[:, MEM_W:]
    kt = k.T
    k_ref[0, 0] = kt
    v_ref[0, 0] = v.T
    kt4 = jnp.concatenate([kt, kt, kt, kt], axis=1)
    keep_k = (_iota((MEM_W, MEM_H * N_MEM), 0) >> 6) == (_iota((MEM_W, MEM_H * N_MEM), 1) >> 8)
    kbd_ref[0, 0] = _bf(jnp.where(keep_k, kt4, 0.0))
    v4 = jnp.concatenate([v, v, v, v], axis=0)
    keep_v = (_iota((MEM_H * N_MEM, MEM_W), 0) >> 8) == (_iota((MEM_H * N_MEM, MEM_W), 1) >> 6)
    vbd_ref[0, 0] = _bf(jnp.where(keep_v, v4, 0.0))


def _memkv_call(mem, norm_mem, w_mem_kv):
    nb = mem.shape[0]
    nl = w_mem_kv.shape[0]
    g = norm_mem.reshape(nl, 1, D)
    w = _bf(w_mem_kv)
    return pl.pallas_call(
        _memkv_kernel,
        grid=(nl, nb),
        in_specs=[
            pl.BlockSpec((1, N_MEM, D), lambda l, b: (b, 0, 0)),
            pl.BlockSpec((1, 1, D), lambda l, b: (l, 0, 0)),
            pl.BlockSpec((1, D, 2 * MEM_W), lambda l, b: (l, 0, 0)),
        ],
        out_specs=[
            pl.BlockSpec((1, 1, MEM_W, N_MEM), lambda l, b: (l, b, 0, 0)),
            pl.BlockSpec((1, 1, MEM_W, N_MEM), lambda l, b: (l, b, 0, 0)),
            pl.BlockSpec((1, 1, MEM_W, MEM_H * N_MEM), lambda l, b: (l, b, 0, 0)),
            pl.BlockSpec((1, 1, MEM_H * N_MEM, MEM_W), lambda l, b: (l, b, 0, 0)),
        ],
        out_shape=[
            jax.ShapeDtypeStruct((nl, nb, MEM_W, N_MEM), F32),
            jax.ShapeDtypeStruct((nl, nb, MEM_W, N_MEM), F32),
            jax.ShapeDtypeStruct((nl, nb, MEM_W, MEM_H * N_MEM), BF16),
            jax.ShapeDtypeStruct((nl, nb, MEM_H * N_MEM, MEM_W), BF16),
        ],
        compiler_params=_params(("arbitrary", "arbitrary")),
        name="mem_kv",
    )(mem, g, w)


def _proj_a(x, gpre_ref, wqk_ref, wv_ref, wr_ref, wmisc_ref, wg_ref, bg_ref, glan_ref, mem_refs=None):
    h = _bf(_rms(x, gpre_ref[...]))
    misc = _dot(h, wmisc_ref[...])
    glr = misc[:, :128]
    qm = misc[:, 128:]
    ps = None if mem_refs is None else _mem_probs(qm, mem_refs[0])
    g = _log_sigmoid(_dot(_bf(glr), wg_ref[...]) + bg_ref[...]) * (1.0 / GATE_NORM)
    gate = glan_ref[...] * _silu(_dot(h, wr_ref[...]))
    qk = _dot(h, wqk_ref[...])
    q = qk[:, :GLA_KW] * (GLA_DK ** -0.5)
    k = qk[:, GLA_KW:]
    v = _dot(h, wv_ref[...])
    return q, k, g, v, gate, (qm if mem_refs is None else _mem_pv(ps, mem_refs[1]))


def _gla_out_gate(o, gate):
    vh = _gla_v_head(_iota(o.shape, 1))
    o2 = o * o
    scale = jnp.zeros_like(o)
    for h in range(GLA_H):
        ss = jnp.sum(jnp.where(vh == h, o2, 0.0), axis=-1, keepdims=True) * (1.0 / GLA_DV)
        scale = jnp.where(vh == h, lax.rsqrt(ss + EPS), scale)
    return o * scale * gate


def _mix_residual(x, o_main, o_mem, wo1_ref, wo2_ref, gpost_ref):
    mix = _dot(_bf(o_main), wo1_ref[...]) + _dot(_bf(o_mem), wo2_ref[...])
    return x + _rms(mix, gpost_ref[...])


def _make_gla_chunk(q_s, k_s, g_s, v_s, o_s, state_s, p_s):
    C = GLA_CHUNK

    ri = _iota((C, GLA_KW), 0)
    kh = _gla_k_head(_iota((C, GLA_KW), 1))
    khcat = jnp.concatenate([kh, kh, kh], axis=1)
    vh = _gla_v_head(_iota((C, GLA_VW), 1))
    tri = _bf(jnp.where(_iota((C, C), 0) >= _iota((C, C), 1), 1.0, 0.0))
    d_rs = _iota((C, 4 * C), 0) - (_iota((C, 4 * C), 1) & (C - 1))
    band = jnp.where((d_rs >= 0) & (d_rs <= (_iota((C, 4 * C), 0) & (GLA_SUB - 1))), d_rs, -1)
    ebc = _bf(jnp.where(_gla_k_head(_iota((GLA_KW, 4 * C), 0)) == (_iota((GLA_KW, 4 * C), 1) >> 6),
                        1.0, 0.0))
    blockmask = _gla_v_head(_iota((GLA_VW, GLA_KW), 0)) == _gla_k_head(_iota((GLA_VW, GLA_KW), 1))

    def chunk(c, carry):
        r0 = pl.multiple_of(c * C, C)
        qc = q_s[pl.ds(r0, C), :]
        kc = k_s[pl.ds(r0, C), :]
        gc = g_s[pl.ds(r0, C), :]
        vc = v_s[pl.ds(r0, C), :]
        b = _exact_dot(tri, gc)

        ref1 = jnp.broadcast_to(b[31:32, :], b.shape)
        ref2 = jnp.where(ri < 32, jnp.broadcast_to(b[15:16, :], b.shape),
                         jnp.broadcast_to(b[47:48, :], b.shape))
        q1 = qc * jnp.exp(jnp.minimum(b - ref1, 0.0))
        k1 = kc * jnp.exp(jnp.minimum(ref1 - b, 0.0))
        q2 = qc * jnp.exp(jnp.minimum(b - ref2, 0.0))
        k2 = kc * jnp.exp(jnp.minimum(ref2 - b, 0.0))
        qcat = jnp.concatenate([
            jnp.where(ri >= 32, q1, 0.0),
            jnp.where((ri >= 16) & (ri < 32), q2, 0.0),
            jnp.where(ri >= 48, q2, 0.0)], axis=1)
        kcat = jnp.concatenate([
            jnp.where(ri < 32, k1, 0.0),
            jnp.where(ri < 16, k2, 0.0),
            jnp.where((ri >= 32) & (ri < 48), k2, 0.0)], axis=1)
        kst = _bf(jnp.concatenate([jnp.where(khcat == h, kcat, 0.0) for h in range(GLA_H)], axis=0))
        a_off = _dot_nt(_bf(qcat), kst)

        for dlt in range(GLA_SUB):
            if dlt == 0:
                pr = qc * kc
            else:
                kd = pltpu.roll(kc, dlt, 0)
                bd = pltpu.roll(b, dlt, 0)
                pr = qc * kd * jnp.exp(jnp.minimum(b - bd, 0.0))
            p_s[dlt * C:(dlt + 1) * C, :] = _bf(pr)
        rsum = _dot(p_s[...], ebc)
        a_diag = jnp.zeros((C, 4 * C), F32)
        for dlt in range(GLA_SUB):
            a_diag = jnp.where(band == dlt, rsum[dlt * C:(dlt + 1) * C, :], a_diag)

        vst = _bf(jnp.concatenate([jnp.where(vh == h, vc, 0.0) for h in range(GLA_H)], axis=0))
        o_intra = _dot(_bf(a_off + a_diag), vst)
        st = state_s[...]
        o_inter = _dot_nt(_bf(qc * jnp.exp(b)), _bf(st))
        o_s[pl.ds(r0, C), :] = o_intra + o_inter

        blast = jnp.broadcast_to(b[C - 1:C, :], b.shape)
        kdec = kc * jnp.exp(blast - b)
        kvt = _dot_tn(_bf(vc), _bf(kdec))
        decay = jnp.exp(b[C - 1:C, :])
        state_s[...] = st * decay + jnp.where(blockmask, kvt, 0.0)
        return carry

    return chunk


def _make_gla_chunk_bounded(q_s, k_s, g_s, v_s, o_s, state_s):
    C = GLA_CHUNK
    kh = _gla_k_head(_iota((C, GLA_KW), 1))
    vh = _gla_v_head(_iota((C, GLA_VW), 1))
    tri = _bf(jnp.where(_iota((C, C), 0) >= _iota((C, C), 1), 1.0, 0.0))
    causal = _iota((C, 4 * C), 0) >= (_iota((C, 4 * C), 1) & (C - 1))
    blockmask = _gla_v_head(_iota((GLA_VW, GLA_KW), 0)) == _gla_k_head(_iota((GLA_VW, GLA_KW), 1))

    def group(gi, carry):
        base = gi * (GLA_GROUP * C)
        rows = [pl.ds(pl.multiple_of(base + i * C, C), C) for i in range(GLA_GROUP)]
        bs = [_exact_dot(tri, g_s[r, :]) for r in rows]
        qes = [_bf(q_s[r, :] * jnp.exp(b)) for r, b in zip(rows, bs)]
        kes = [k_s[r, :] * jnp.exp(-b) for r, b in zip(rows, bs)]
        ksts = [_bf(jnp.concatenate([jnp.where(kh == h, ke, 0.0) for h in range(GLA_H)], axis=0)) for ke in kes]
        attn = [_bf(jnp.where(causal, _dot_nt(qe, kst), 0.0)) for qe, kst in zip(qes, ksts)]
        vcs = [v_s[r, :] for r in rows]
        vsts = [_bf(jnp.concatenate([jnp.where(vh == h, vc, 0.0) for h in range(GLA_H)], axis=0)) for vc in vcs]
        o_intra = [_dot(a, vst) for a, vst in zip(attn, vsts)]
        decays = [jnp.exp(b[C - 1:C, :]) for b in bs]
        kvts = [_dot_tn(_bf(vc), _bf(ke * d)) for vc, ke, d in zip(vcs, kes, decays)]
        st = state_s[...]
        for i in range(GLA_GROUP):
            o_s[rows[i], :] = o_intra[i] + _dot_nt(qes[i], _bf(st))
            st = st * decays[i] + jnp.where(blockmask, kvts[i], 0.0)
        state_s[...] = st
        return carry

    return group


def _mixer_a_kernel(x_ref, gpre_ref, wqk_ref, wv_ref, wr_ref, wmisc_ref, wg_ref, bg_ref, glan_ref,
                    kbd_ref, vbd_ref, wo1_ref, wo2_ref, gpost_ref,
                    xo_ref, st_ref,
                    q_s, k_s, g_s, v_s, o_s, state_s, p_s, gate_s, om_s, hook=None):
    C = GLA_CHUNK
    tm = x_ref.shape[1]
    nchunk = tm // C

    @pl.when(pl.program_id(1) == 0)
    def _():
        state_s[...] = jnp.zeros_like(state_s)

    q, k, g, v, gate, o_mem = _proj_a(x_ref[0], gpre_ref, wqk_ref, wv_ref, wr_ref, wmisc_ref, wg_ref, bg_ref,
                                      glan_ref, (kbd_ref.at[0], vbd_ref.at[0]))
    q_s[...] = q
    k_s[...] = k
    g_s[...] = g
    v_s[...] = v
    gate_s[...] = gate
    om_s[...] = _bf(o_mem)
    if hook is not None:
        hook()

    total = jnp.sum(g.reshape(nchunk, C, GLA_KW), axis=1)
    bounded = jnp.min(total) > -GLA_SAFE_DECAY

    @pl.when(bounded)
    def _():
        lax.fori_loop(0, nchunk // GLA_GROUP, _make_gla_chunk_bounded(q_s, k_s, g_s, v_s, o_s, state_s), 0)

    @pl.when(jnp.logical_not(bounded))
    def _():
        lax.fori_loop(0, nchunk, _make_gla_chunk(q_s, k_s, g_s, v_s, o_s, state_s, p_s), 0)

    st_ref[0] = state_s[...]
    o_main = _gla_out_gate(o_s[...], gate_s[...])
    xo_ref[0] = _mix_residual(x_ref[0], o_main, om_s[...], wo1_ref, wo2_ref, gpost_ref)


def _const_spec(a):
    nd = a.ndim
    return pl.BlockSpec(a.shape, lambda *_: (0,) * nd, pipeline_mode=pl.Buffered(1))


def _mixer_a_call(x, wa, kbd, vbd, make_guests):
    nb, seq, _ = x.shape
    tm = TM_PROMPT
    nt = seq // tm
    weights = [wa["gpre"], wa["wqk"], wa["wv"], wa["wr"], wa["wmisc"], wa["wg"], wa["bg"], wa["glan"]]
    tail = [wa["wo1"], wa["wo2"], wa["gpost"]]
    return _call_with_guests(
        _mixer_a_kernel,
        grid=(nb, nt),
        step_of=lambda b, t: b * nt + t,
        inputs=[x, *weights, kbd, vbd, *tail],
        in_specs=([pl.BlockSpec((1, tm, D), lambda b, t: (b, t, 0))]
                  + [_const_spec(w) for w in weights]
                  + [pl.BlockSpec((1, MEM_W, MEM_H * N_MEM), lambda b, t: (b, 0, 0)),
                     pl.BlockSpec((1, MEM_H * N_MEM, MEM_W), lambda b, t: (b, 0, 0))]
                  + [_const_spec(w) for w in tail]),
        out_specs=[pl.BlockSpec((1, tm, D), lambda b, t: (b, t, 0)),
                   pl.BlockSpec((1, GLA_VW, GLA_KW), lambda b, t: (b, 0, 0))],
        out_shape=[jax.ShapeDtypeStruct((nb, seq, D), F32),
                   jax.ShapeDtypeStruct((nb, GLA_VW, GLA_KW), F32)],
        scratch=[
            pltpu.VMEM((tm, GLA_KW), F32), pltpu.VMEM((tm, GLA_KW), F32), pltpu.VMEM((tm, GLA_KW), F32),
            pltpu.VMEM((tm, GLA_VW), F32), pltpu.VMEM((tm, GLA_VW), F32),
            pltpu.VMEM((GLA_VW, GLA_KW), F32),
            pltpu.VMEM((GLA_SUB * GLA_CHUNK, GLA_KW), BF16),
            pltpu.VMEM((tm, GLA_VW), F32), pltpu.VMEM((tm, MEM_W), BF16),
        ],
        guests=make_guests(nb * nt),
        name="mixer_a_prompt",
    )


def _ffn_rows(x, gpre_ref, wup_ref, wdn_ref, gpost_ref, acc_ref):
    h = _bf(_rms(x, gpre_ref[0]))
    for j in range(D_FF // FF_CHUNK):
        cols = slice(j * FF_CHUNK, (j + 1) * FF_CHUNK)
        u = jnp.maximum(_dot(h, _bf(wup_ref[0, :, cols])), 0.0)
        d = _dot(_bf(u * u), _bf(wdn_ref[0, cols, :]))
        if j == 0:
            acc_ref[...] = d
        else:
            acc_ref[...] += d
    return x + _rms(acc_ref[...], gpost_ref[0])


def _ffn_kernel(xp_ref, xs_ref, gpre_ref, wup_ref, wdn_ref, gpost_ref, yp_ref, ys_ref, acc_s):
    i = pl.program_id(0)
    last = pl.num_programs(0) - 1

    @pl.when(i < last)
    def _():
        yp_ref[...] = _ffn_rows(xp_ref[...], gpre_ref, wup_ref, wdn_ref, gpost_ref, acc_s)

    @pl.when(i == last)
    def _():
        ns = xs_ref.shape[0]
        ys_ref[...] = _ffn_rows(xs_ref[...], gpre_ref, wup_ref, wdn_ref, gpost_ref, acc_s.at[0:ns, :])


def _ffn_call(xp, xs, norm_pre, w_up, w_down, norm_post, layer):
    n, ns = xp.shape[0], xs.shape[0]
    tm = TM_PROMPT
    nt = n // tm
    gains = [a.reshape(a.shape[0], 1, D) for a in (norm_pre, norm_post)]
    gspec = pl.BlockSpec((1, 1, D), lambda i: (layer, 0, 0))
    wspec = lambda w: pl.BlockSpec((1,) + w.shape[1:], lambda i: (layer, 0, 0), pipeline_mode=pl.Buffered(1))
    tile = pl.BlockSpec((tm, D), lambda i: (jnp.minimum(i, nt - 1), 0))
    return pl.pallas_call(
        _ffn_kernel,
        grid=(nt + 1,),
        in_specs=[tile, _full_spec(xs), gspec, wspec(w_up), wspec(w_down), gspec],
        out_specs=[tile, _full_spec(xs)],
        out_shape=[jax.ShapeDtypeStruct((n, D), F32), jax.ShapeDtypeStruct((ns, D), F32)],
        scratch_shapes=[pltpu.VMEM((tm, D), F32)],
        compiler_params=_params(("arbitrary",)),
        name="ffn",
    )(xp, xs, gains[0], w_up, w_down, gains[1])


def _t5_bucket(dist):
    max_exact = N_BUCKETS // 2
    n = np.maximum(dist, 0)
    nf = np.maximum(n, 1).astype(np.float32)
    large = max_exact + (np.log(nf / np.float32(max_exact)) / np.float32(math.log(MAX_DISTANCE / max_exact))
                         * np.float32(N_BUCKETS - max_exact)).astype(np.int32)
    large = np.minimum(large, N_BUCKETS - 1)
    return np.where(n < max_exact, n, large)


def _swa_bias_tables():
    qi = np.arange(WINDOW)[:, None] + WINDOW
    kj = np.arange(2 * WINDOW)[None, :]
    dist = qi - kj
    valid = (dist >= 0) & (dist < WINDOW)
    return np.where(valid, _t5_bucket(dist), -1).astype(np.int32)


def _mixer_b_kernel(rb_ref, sink_ref,
                    x_ref, bkt_ref, gkv_ref, wkv_ref, gpre_ref, wq_ref, wqm_ref,
                    kbd_ref, vbd_ref, wo1_ref, wo2_ref, gpost_ref,
                    xo_ref, kc_ref, vc_ref,
                    kbuf, vbuf, q_s, o_s, bias_s, qm_s, hook=None):
    W = WINDOW
    tm = x_ref.shape[1]
    bb = pl.program_id(0)
    t = pl.program_id(1)
    nheads = SWA_G * SWA_KVH

    @pl.when((bb == 0) & (t == 0))
    def _():
        bkt = bkt_ref[...]
        own = _iota((W, 2 * W), 1) >= W
        for i in range(nheads):
            def add_bucket(n, acc):
                return jnp.where(bkt == n, rb_ref[n, i], acc)
            tab = lax.fori_loop(0, N_BUCKETS, add_bucket, jnp.zeros((W, 2 * W), F32))
            tab = jnp.where(bkt < 0, NEG_INF, tab)
            bias_s[0, i] = tab
            bias_s[1, i] = jnp.where(own, tab, NEG_INF)

    @pl.when(t == 0)
    def _():
        kbuf[0:W, :] = jnp.zeros((W, SWA_KVW), F32)
        vbuf[0:W, :] = jnp.zeros((W, SWA_KVW), F32)

    x = x_ref[0]
    xn = x * lax.rsqrt(jnp.mean(x * x, axis=-1, keepdims=True) + EPS)
    kv = _dot(_bf(xn * gkv_ref[...]), wkv_ref[...])
    kbuf[W:W + tm, :] = kv[:, :SWA_KVW]
    vbuf[W:W + tm, :] = kv[:, SWA_KVW:]
    kc_ref[0] = kv[tm - W:, :SWA_KVW]
    vc_ref[0] = kv[tm - W:, SWA_KVW:]

    h = _bf(xn * gpre_ref[...])
    q_s[...] = _dot(h, wq_ref[...]) * (SWA_HD ** -0.5)
    qm_s[...] = _dot(h, wqm_ref[...])
    if hook is not None:
        hook()

    lane_head = _iota((W, SWA_KVW), 1) >> 6
    key_head = _iota((2 * W, SWA_KVW), 1) >> 6

    def scores(j):
        r0 = pl.multiple_of(j * W, W)
        qrows = []
        for gi in range(SWA_G):
            qg = q_s[pl.ds(r0, W), gi * SWA_KVW:(gi + 1) * SWA_KVW]
            for hh in range(SWA_KVH):
                qrows.append(jnp.where(lane_head == hh, qg, 0.0))
        return _dot_nt(_bf(jnp.concatenate(qrows, axis=0)), _bf(kbuf[pl.ds(r0, 2 * W), :]))

    def probs(j, s_all, gi):
        tab = jnp.where((j == 0) & (t == 0), 1, 0)
        ps = []
        for hh in range(SWA_KVH):
            i = gi * SWA_KVH + hh
            s = s_all[i * W:(i + 1) * W, :] + bias_s[tab, i]
            sink = sink_ref[i]
            mx = jnp.maximum(jnp.max(s, axis=-1, keepdims=True), sink)
            p = jnp.exp(s - mx)
            p = p / (jnp.sum(p, axis=-1, keepdims=True) + jnp.exp(sink - mx))
            ps.append(_bf(p))
        return jnp.concatenate(ps, axis=1)

    def pair(jj, carry):
        js = [jj * SWA_BLOCKS_PER_TRIP + i for i in range(SWA_BLOCKS_PER_TRIP)]
        s_alls = [scores(j) for j in js]
        for j, s_all in zip(js, s_alls):
            r0 = pl.multiple_of(j * W, W)
            vb = vbuf[pl.ds(r0, 2 * W), :]
            vst = _bf(jnp.concatenate([jnp.where(key_head == hh, vb, 0.0) for hh in range(SWA_KVH)], axis=0))
            for gi in range(SWA_G):
                o_s[pl.ds(r0, W), gi * SWA_KVW:(gi + 1) * SWA_KVW] = _dot(probs(j, s_all, gi), vst)
        return carry

    lax.fori_loop(0, tm // (W * SWA_BLOCKS_PER_TRIP), pair, 0)

    kbuf[0:W, :] = kbuf[tm:tm + W, :]
    vbuf[0:W, :] = vbuf[tm:tm + W, :]

    o_mem = _mem_pv(_mem_probs(qm_s[...], kbd_ref.at[0]), vbd_ref.at[0])
    xo_ref[0] = _mix_residual(x_ref[0], o_s[...], o_mem, wo1_ref, wo2_ref, gpost_ref)


def _mixer_b_call(x, wb, kbd, vbd, make_guests):
    nb, seq, _ = x.shape
    tm = TM_PROMPT
    nt = seq // tm
    bkt = jnp.asarray(_swa_bias_tables())
    head = [bkt, wb["gkv"], wb["wkv"], wb["gpre"], wb["wq"], wb["wqm"]]
    tail = [wb["wo1"], wb["wo2"], wb["gpost"]]
    smem = pl.BlockSpec(memory_space=pltpu.SMEM)
    return _call_with_guests(
        _mixer_b_kernel,
        grid=(nb, nt),
        step_of=lambda b, t: b * nt + t,
        inputs=[wb["rb"], wb["sinks"], x, *head, kbd, vbd, *tail],
        in_specs=([smem, smem, pl.BlockSpec((1, tm, D), lambda b, t: (b, t, 0))]
                  + [_const_spec(w) for w in head]
                  + [pl.BlockSpec((1, MEM_W, MEM_H * N_MEM), lambda b, t: (b, 0, 0)),
                     pl.BlockSpec((1, MEM_H * N_MEM, MEM_W), lambda b, t: (b, 0, 0))]
                  + [_const_spec(w) for w in tail]),
        out_specs=[pl.BlockSpec((1, tm, D), lambda b, t: (b, t, 0)),
                   pl.BlockSpec((1, WINDOW, SWA_KVW), lambda b, t: (b, 0, 0)),
                   pl.BlockSpec((1, WINDOW, SWA_KVW), lambda b, t: (b, 0, 0))],
        out_shape=[jax.ShapeDtypeStruct((nb, seq, D), F32),
                   jax.ShapeDtypeStruct((nb, WINDOW, SWA_KVW), F32),
                   jax.ShapeDtypeStruct((nb, WINDOW, SWA_KVW), F32)],
        scratch=[
            pltpu.VMEM((tm + WINDOW, SWA_KVW), F32), pltpu.VMEM((tm + WINDOW, SWA_KVW), F32),
            pltpu.VMEM((tm, SWA_QW), F32), pltpu.VMEM((tm, SWA_QW), F32),
            pltpu.VMEM((2, SWA_G * SWA_KVH, WINDOW, 2 * WINDOW), F32),
            pltpu.VMEM((tm, MEM_W), F32),
        ],
        guests=make_guests(nb * nt),
        name="mixer_b_prompt",
    )


def _row_to_col(row, eye):
    return jnp.sum(jnp.where(eye, jnp.broadcast_to(row, eye.shape), 0.0), axis=1, keepdims=True)


def _eye(n):
    return _iota((n, n), 0) == _iota((n, n), 1)


def _pre_a_kernel(x_ref, gpre_ref, wqk_ref, wv_ref, wr_ref, wmisc_ref, wg_ref, bg_ref, glan_ref,
                  q_ref, k_ref, g_ref, v_ref, gate_ref, qm_ref):
    q, k, g, v, gate, qm = _proj_a(x_ref[...], gpre_ref, wqk_ref, wv_ref, wr_ref, wmisc_ref, wg_ref, bg_ref,
                                   glan_ref)
    q_ref[...] = q
    k_ref[...] = k
    g_ref[...] = g
    v_ref[...] = v
    gate_ref[...] = gate
    qm_ref[...] = qm


def _pre_a_call(x2d, wa):
    n = x2d.shape[0]
    ws = [wa["gpre"], wa["wqk"], wa["wv"], wa["wr"], wa["wmisc"], wa["wg"], wa["bg"], wa["glan"]]
    shapes = [(n, GLA_KW), (n, GLA_KW), (n, GLA_KW), (n, GLA_VW), (n, GLA_VW), (n, MEM_W)]
    return pl.pallas_call(
        _pre_a_kernel,
        grid=(1,),
        in_specs=[_full_spec(x2d)] + [_full_spec(w) for w in ws],
        out_specs=[pl.BlockSpec(s, lambda i: (0, 0)) for s in shapes],
        out_shape=[jax.ShapeDtypeStruct(s, F32) for s in shapes],
        compiler_params=_params(("arbitrary",)),
        name="pre_a_sample",
    )(x2d, *ws)


def _gla_step_guest(qt, kt, gt, vt, state5, nsteps):
    n = qt.shape[1]
    nblk = GLA_DK // GLA_DK_BLOCK
    nact = GLA_H * nblk
    assert nact <= nsteps
    act = lambda s: jnp.minimum(s, nact - 1)

    def pre(step, qt_ref, kt_ref, gt_ref, vt_ref, s_ref, so_ref, ot_ref):
        @pl.when(step < nact)
        def _():
            vt_blk = vt_ref[...]
            acc = jnp.zeros_like(vt_blk)
            for d in range(GLA_DK_BLOCK):
                s_new = jnp.exp(gt_ref[d:d + 1, :]) * s_ref[0, 0, d] + kt_ref[d:d + 1, :] * vt_blk
                so_ref[0, 0, d] = s_new
                acc = acc + qt_ref[d:d + 1, :] * s_new
            first = step % nblk == 0

            @pl.when(first)
            def _():
                ot_ref[...] = acc

            @pl.when(jnp.logical_not(first))
            def _():
                ot_ref[...] += acc

    rows = ((GLA_DK_BLOCK, n), lambda s: (act(s), 0))
    head = ((GLA_DV, n), lambda s: (act(s) // nblk, 0))
    st = ((1, 1, GLA_DK_BLOCK, GLA_DV, n), lambda s: (0, act(s) // nblk, act(s) % nblk, 0, 0))
    return dict(inputs=[qt, kt, gt, vt, state5], in_specs=[rows, rows, rows, head, st],
                out_shape=[jax.ShapeDtypeStruct(state5.shape, F32), jax.ShapeDtypeStruct((GLA_VW, n), F32)],
                out_specs=[st, head], scratch=[], pre=pre, post=lambda step, state, *refs: None)


def _mem_step_pre(step, qm_ref, mk_ref, mv_ref, o_ref):
    own = (_iota((8, MEM_W), 1) >> 6) == _iota((8, MEM_W), 0)
    ps = []
    for i in range(qm_ref.shape[0]):
        q8 = jnp.where(own, jnp.broadcast_to(qm_ref[i], (8, MEM_W)), 0.0)
        s = _dot(_bf(q8), _bf(mk_ref[0, i].reshape(MEM_W, N_MEM))) * (MEM_HD ** -0.5)
        e = jnp.exp(s - jnp.max(s, axis=1, keepdims=True))
        ps.append(_bf(e / jnp.sum(e, axis=1, keepdims=True)))
    return ps


def _mem_step_post(step, ps, qm_ref, mk_ref, mv_ref, o_ref):
    own = (_iota((8, MEM_W), 1) >> 6) == _iota((8, MEM_W), 0)
    for i, p in enumerate(ps):
        res = _dot_nt(p, _bf(mv_ref[0, i].reshape(MEM_W, N_MEM)))
        o_ref[i] = jnp.sum(jnp.where(own, res, 0.0), axis=0, keepdims=True)


def _mem_step_guest(qm, mk5, mv5, layer, nsteps):
    n = qm.shape[0]
    rb = n // nsteps
    blk = ((1, rb, MEM_H, MEM_HD, N_MEM), lambda s: (layer, s, 0, 0, 0))
    rows = ((rb, 1, MEM_W), lambda s: (s, 0, 0))
    return dict(inputs=[qm.reshape(n, 1, MEM_W), mk5, mv5], in_specs=[rows, blk, blk],
                out_shape=[jax.ShapeDtypeStruct((n, 1, MEM_W), F32)], out_specs=[rows], scratch=[],
                pre=_mem_step_pre, post=_mem_step_post)


def _post_a_kernel(x_ref, o_ref, gate_ref, om_ref, wo1_ref, wo2_ref, gpost_ref, xo_ref):
    o_main = _gla_out_gate(o_ref[...], gate_ref[...])
    xo_ref[...] = _mix_residual(x_ref[...], o_main, om_ref[...], wo1_ref, wo2_ref, gpost_ref)


def _post_a_call(x2d, o, gate, om, wa):
    args = [x2d, o, gate, om, wa["wo1"], wa["wo2"], wa["gpost"]]
    return pl.pallas_call(
        _post_a_kernel,
        grid=(1,),
        in_specs=[_full_spec(a) for a in args],
        out_specs=_full_spec(x2d),
        out_shape=jax.ShapeDtypeStruct(x2d.shape, F32),
        compiler_params=_params(("arbitrary",)),
        name="post_a_sample",
    )(*args)


def _pre_b_kernel(x_ref, gkv_ref, wkv_ref, gpre_ref, wq_ref, wqm_ref, ks_ref, vs_ref, q_ref, qm_ref):
    x = x_ref[...]
    kv = _dot(_bf(_rms(x, gkv_ref[...])), wkv_ref[...])
    ks_ref[...] = kv[:, :SWA_KVW]
    vs_ref[...] = kv[:, SWA_KVW:]
    h = _bf(_rms(x, gpre_ref[...]))
    q_ref[...] = _dot(h, wq_ref[...])
    qm_ref[...] = _dot(h, wqm_ref[...])


def _pre_b_call(x2d, wb):
    n = x2d.shape[0]
    ws = [wb["gkv"], wb["wkv"], wb["gpre"], wb["wq"], wb["wqm"]]
    widths = [SWA_KVW, SWA_KVW, SWA_QW, MEM_W]
    return pl.pallas_call(
        _pre_b_kernel,
        grid=(1,),
        in_specs=[_full_spec(x2d)] + [_full_spec(w) for w in ws],
        out_specs=[pl.BlockSpec((n, w), lambda i: (0, 0)) for w in widths],
        out_shape=[jax.ShapeDtypeStruct((n, w), F32) for w in widths],
        compiler_params=_params(("arbitrary",)),
        name="pre_b_sample",
    )(x2d, *ws)


def _sample_buckets():
    dist = (WINDOW - 1) - np.arange(WINDOW)
    return _t5_bucket(dist).astype(np.int32).reshape(1, WINDOW)


def _swa_step_pre(step, sink_ref, rb_ref, bkt_ref, kc_ref, vc_ref, kst_ref, vst_ref, q_ref,
                  kn_ref, vn_ref, o_ref, bias_s, sink_s, s_s, p_s):
    W = WINDOW
    R = SWA_KVH * 8

    @pl.when(step == 0)
    def _():
        bkt = bkt_ref[...]
        rid = _iota((R, W), 0)
        bias = jnp.zeros((R, W), F32)
        sink = jnp.zeros((R, W), F32)
        for h in range(SWA_KVH):
            for g in range(SWA_G):
                idx = g * SWA_KVH + h
                def add_bucket(n, acc):
                    return jnp.where(bkt == n, rb_ref[n, idx], acc)
                brow = lax.fori_loop(0, N_BUCKETS, add_bucket, jnp.zeros((1, W), F32))
                bias = jnp.where(rid == h * 8 + g, brow, bias)
                sink = jnp.where(rid == h * 8 + g, sink_ref[idx], sink)
        bias_s[...] = bias
        sink_s[...] = sink

    rb = q_ref.shape[0]
    base = step * rb
    last = _iota((SWA_KVW, W), 1) == W - 1
    own = (_iota((R, SWA_KVW), 1) >> 6) == (_iota((R, SWA_KVW), 0) >> 3)
    kst = kst_ref[...]
    vst = vst_ref[...]
    for i in range(rb):
        shift = W - 1 - (base + i)
        kn = jnp.where(last, pltpu.roll(kst, shift, 1), pltpu.roll(kc_ref[i].reshape(SWA_KVW, W), W - 1, 1))
        vn = jnp.where(last, pltpu.roll(vst, shift, 1), pltpu.roll(vc_ref[i].reshape(SWA_KVW, W), W - 1, 1))
        kn_ref[i] = kn.reshape(SWA_KVH, SWA_HD, W)
        vn_ref[i] = vn.reshape(SWA_KVH, SWA_HD, W)
        q32 = jnp.where(own, jnp.concatenate([q_ref[i]] * SWA_KVH, axis=0), 0.0)
        s_s[i * R:(i + 1) * R, :] = _dot(_bf(q32), _bf(kn))
    s = s_s[...] * (SWA_HD ** -0.5) + jnp.concatenate([bias_s[...]] * rb, axis=0)
    sink = jnp.concatenate([sink_s[...]] * rb, axis=0)
    mx = jnp.maximum(jnp.max(s, axis=1, keepdims=True), sink)
    p = jnp.exp(s - mx)
    p_s[...] = _bf(p / (jnp.sum(p, axis=1, keepdims=True) + jnp.exp(sink - mx)))


def _swa_step_post(step, _, sink_ref, rb_ref, bkt_ref, kc_ref, vc_ref, kst_ref, vst_ref, q_ref,
                   kn_ref, vn_ref, o_ref, bias_s, sink_s, s_s, p_s):
    W = WINDOW
    R = SWA_KVH * 8
    own = (_iota((R, SWA_KVW), 1) >> 6) == (_iota((R, SWA_KVW), 0) >> 3)
    for i in range(q_ref.shape[0]):
        vn = vn_ref[i].reshape(SWA_KVW, W)
        res = jnp.where(own, _dot_nt(p_s[i * R:(i + 1) * R, :], _bf(vn)), 0.0)
        o_ref[i] = res[0:8] + res[8:16] + res[16:24] + res[24:32]


def _swa_step_guest(kc4, vc4, ks, vs, q, wb, nsteps):
    n = q.shape[0]
    rb = n // nsteps
    bkt = jnp.asarray(_sample_buckets())
    q8 = jnp.pad(q.reshape(n, SWA_G, SWA_KVW), ((0, 0), (0, 8 - SWA_G), (0, 0)))
    row3 = lambda r, w: ((rb, r, w), lambda s: (s, 0, 0))
    cache = ((rb, SWA_KVH, SWA_HD, WINDOW), lambda s: (s, 0, 0, 0))
    whole = ((SWA_KVW, n), lambda s: (0, 0))
    smem = pl.BlockSpec(memory_space=pltpu.SMEM)
    return dict(inputs=[wb["sinks"], wb["rb"], bkt, kc4, vc4, ks.T, vs.T, q8],
                in_specs=[smem, smem, _full_spec(bkt), cache, cache, whole, whole, row3(8, SWA_KVW)],
                out_shape=[jax.ShapeDtypeStruct(kc4.shape, F32), jax.ShapeDtypeStruct(vc4.shape, F32),
                           jax.ShapeDtypeStruct((n, 8, SWA_KVW), F32)],
                out_specs=[cache, cache, row3(8, SWA_KVW)],
                scratch=[pltpu.VMEM((SWA_KVH * 8, WINDOW), F32), pltpu.VMEM((SWA_KVH * 8, WINDOW), F32),
                         pltpu.VMEM((rb * SWA_KVH * 8, WINDOW), F32),
                         pltpu.VMEM((rb * SWA_KVH * 8, WINDOW), BF16)],
                pre=_swa_step_pre, post=_swa_step_post)


def _post_b_kernel(x_ref, o_ref, om_ref, wo1_ref, wo2_ref, gpost_ref, xo_ref):
    xo_ref[...] = _mix_residual(x_ref[...], o_ref[...], om_ref[...], wo1_ref, wo2_ref, gpost_ref)


def _post_b_call(x2d, o, om, wb):
    args = [x2d, o, om, wb["wo1"], wb["wo2"], wb["gpost"]]
    return pl.pallas_call(
        _post_b_kernel,
        grid=(1,),
        in_specs=[_full_spec(a) for a in args],
        out_specs=_full_spec(x2d),
        out_shape=jax.ShapeDtypeStruct(x2d.shape, F32),
        compiler_params=_params(("arbitrary",)),
        name="post_b_sample",
    )(*args)


def _prep_weights(norm_mix_pre, norm_mix_post, w_in_a, w_gate_up, b_gate,
                  gla_norm, w_in_b, sinks, norm_kv, w_kv, rel_bias, w_out):
    row = lambda g: g.reshape(1, -1)
    wa_in = w_in_a[0]
    c_v = 2 * GLA_KW
    c_r = c_v + GLA_VW
    c_g = c_r + GLA_VW
    c_m = c_g + GATE_RANK
    wmisc = jnp.concatenate([wa_in[:, c_g:c_m], jnp.zeros((D, 128 - GATE_RANK), F32), wa_in[:, c_m:]], axis=1)
    wg = jnp.zeros((128, GLA_KW), F32).at[:GATE_RANK].set(w_gate_up[0])
    wa = dict(
        gpre=row(norm_mix_pre[0]), wqk=_bf(wa_in[:, :c_v]), wv=_bf(wa_in[:, c_v:c_r]), wr=_bf(wa_in[:, c_r:c_g]),
        wmisc=_bf(wmisc), wg=_bf(wg), bg=row(b_gate[0]), glan=row(jnp.tile(gla_norm[0], GLA_H)),
        wo1=_bf(w_out[0][:GLA_VW]), wo2=_bf(w_out[0][GLA_VW:]), gpost=row(norm_mix_post[0]))
    wb_in = w_in_b[0]
    wq = wb_in[:, :SWA_QW].reshape(D, SWA_KVH, SWA_G, SWA_HD).transpose(0, 2, 1, 3).reshape(D, SWA_QW)
    wo1 = w_out[1][:SWA_QW].reshape(SWA_KVH, SWA_G, SWA_HD, D).transpose(1, 0, 2, 3).reshape(SWA_QW, D)
    rb = rel_bias.reshape(N_BUCKETS, SWA_KVH, SWA_G).transpose(0, 2, 1).reshape(N_BUCKETS, SWA_G * SWA_KVH)
    sk = sinks[0].reshape(SWA_KVH, SWA_G).T.reshape(SWA_G * SWA_KVH)
    wb = dict(
        gkv=row(norm_kv), wkv=_bf(w_kv), gpre=row(norm_mix_pre[1]), wq=_bf(wq), wqm=_bf(wb_in[:, SWA_QW:]),
        wo1=_bf(wo1), wo2=_bf(w_out[1][SWA_QW:]), gpost=row(norm_mix_post[1]), rb=rb, sinks=sk)
    return wa, wb


def kernel(x_prompt, x_sample, state_gla, cache_swa_k, cache_swa_v, cache_mem_k, cache_mem_v, mem_prompt,
           norm_mix_pre, norm_mix_post, norm_ffn_pre, norm_ffn_post, norm_mem, w_mem_kv, w_in_a, w_gate_up,
           b_gate, gla_norm, w_in_b, sinks, norm_kv, w_kv, rel_bias, w_out, w_ffn_up, w_ffn_down):
    wa, wb = _prep_weights(norm_mix_pre, norm_mix_post, w_in_a, w_gate_up, b_gate, gla_norm, w_in_b, sinks,
                           norm_kv, w_kv, rel_bias, w_out)
    ffn = lambda xp, xs_, l: _ffn_call(xp, xs_, norm_ffn_pre, w_ffn_up, w_ffn_down, norm_ffn_post, l)
    nb, seq, _ = x_prompt.shape
    ns = x_sample.shape[0]

    xs = x_sample.reshape(ns, D)
    state5 = jnp.transpose(state_gla, (0, 2, 3, 4, 1))
    mk5 = jnp.transpose(cache_mem_k, (0, 1, 3, 4, 2))
    mv5 = jnp.transpose(cache_mem_v, (0, 1, 3, 4, 2))
    kc4 = jnp.transpose(cache_swa_k, (0, 2, 3, 1))
    vc4 = jnp.transpose(cache_swa_v, (0, 2, 3, 1))

    mkt, mvt, kbd, vbd = _memkv_call(mem_prompt, norm_mem, w_mem_kv)
    q, k, g, v, gate, qm = _pre_a_call(xs, wa)
    (x1, st), ((state5_new, ot), (om,)) = _mixer_a_call(
        x_prompt, wa, kbd[0], vbd[0],
        lambda nsteps: [_gla_step_guest(q.T, k.T, g.T, v.T, state5, nsteps),
                        _mem_step_guest(qm, mk5, mv5, 0, nsteps)])
    xs1 = _post_a_call(xs, ot.T, gate, om.reshape(ns, MEM_W), wa)
    x2, xs2 = ffn(x1.reshape(nb * seq, D), xs1, 0)
    ks, vs, qb, qmb = _pre_b_call(xs2, wb)
    (x3, kc, vc), ((kn4, vn4, o8), (omb,)) = _mixer_b_call(
        x2.reshape(nb, seq, D), wb, kbd[1], vbd[1],
        lambda nsteps: [_swa_step_guest(kc4, vc4, ks, vs, qb, wb, nsteps),
                        _mem_step_guest(qmb, mk5, mv5, 1, nsteps)])
    xs3 = _post_b_call(xs2, o8[:, :SWA_G].reshape(ns, SWA_QW), omb.reshape(ns, MEM_W), wb)
    y_prompt, y_sample = ffn(x3.reshape(nb * seq, D), xs3, 1)
    y_prompt = y_prompt.reshape(nb, seq, D)
    y_sample = y_sample.reshape(ns, 1, D)
    st4 = st.reshape(nb, GLA_H, GLA_DV, GLA_H, GLA_DK)
    state_prompt = jnp.stack([st4[:, h, :, h, :] for h in range(GLA_H)], axis=1).transpose(0, 1, 3, 2)[None]
    to_mem = lambda t: t.reshape(2, nb, MEM_H, MEM_HD, N_MEM).transpose(0, 1, 4, 2, 3)
    swa_shape = (nb, WINDOW, SWA_KVH, SWA_HD)

    return (y_prompt, y_sample, state_prompt,
            jnp.transpose(state5_new, (0, 4, 1, 2, 3)),
            kc.reshape(swa_shape), vc.reshape(swa_shape),
            jnp.transpose(kn4, (0, 3, 1, 2)), jnp.transpose(vn4, (0, 3, 1, 2)),
            to_mem(mkt), to_mem(mvt))
```

```python
import functools
import math

import numpy as np
import jax
import jax.numpy as jnp
from jax import lax
from jax.experimental import pallas as pl
from jax.experimental.pallas import tpu as pltpu

F32 = jnp.float32
BF16 = jnp.bfloat16

D = 1024
D_FF = 4 * D
N_MEM = 256
MEM_H = 4
MEM_HD = 64
MEM_W = MEM_H * MEM_HD
GLA_H = 4
GLA_DK = 96
GLA_DV = 192
GLA_KW = GLA_H * GLA_DK
GLA_VW = GLA_H * GLA_DV
GATE_RANK = 16
GATE_NORM = 16.0
SWA_HD = 64
SWA_KVH = 4
SWA_G = 3
SWA_QW = SWA_KVH * SWA_G * SWA_HD
SWA_KVW = SWA_KVH * SWA_HD
WINDOW = 128
N_BUCKETS = 32
MAX_DISTANCE = 128
EPS = 1e-6

GLA_CHUNK = 64
GLA_SUB = 16
GLA_SAFE_DECAY = 60.0
GLA_GROUP = 4
SWA_BLOCKS_PER_TRIP = 2
TM_PROMPT = 512
FF_CHUNK = 512
GLA_DK_BLOCK = 16
V7X_VMEM_LIMIT = 56 * 1024 * 1024
NEG_INF = float("-inf")


def _bf(x):
    return x.astype(BF16)


def _dot(a, b):
    return jnp.dot(a, b, preferred_element_type=F32)


def _dot_nt(a, b):
    return lax.dot_general(a, b, (((1,), (1,)), ((), ())), preferred_element_type=F32)


def _dot_tn(a, b):
    return lax.dot_general(a, b, (((0,), (0,)), ((), ())), preferred_element_type=F32)


def _rms(x, g):
    return x * lax.rsqrt(jnp.mean(x * x, axis=-1, keepdims=True) + EPS) * g


def _split3(x):
    x1 = _bf(x)
    r1 = x - x1.astype(F32)
    x2 = _bf(r1)
    x3 = _bf(r1 - x2.astype(F32))
    return x1, x2, x3


def _exact_dot(sel, x):
    x1, x2, x3 = _split3(x)
    return _dot(sel, x1) + _dot(sel, x2) + _dot(sel, x3)


def _log_sigmoid(z):
    return jnp.minimum(z, 0.0) - jnp.log1p(jnp.exp(-jnp.abs(z)))


def _silu(z):
    return z * (1.0 / (1.0 + jnp.exp(-z)))


def _iota(shape, dim):
    return lax.broadcasted_iota(jnp.int32, shape, dim)


def _gla_k_head(lane):
    one = jnp.int32(1)
    zero = jnp.int32(0)
    return (jnp.where(lane >= GLA_DK, one, zero) + jnp.where(lane >= 2 * GLA_DK, one, zero)
            + jnp.where(lane >= 3 * GLA_DK, one, zero))


def _gla_v_head(lane):
    one = jnp.int32(1)
    zero = jnp.int32(0)
    return (jnp.where(lane >= GLA_DV, one, zero) + jnp.where(lane >= 2 * GLA_DV, one, zero)
            + jnp.where(lane >= 3 * GLA_DV, one, zero))


def _full_spec(a):
    nd = a.ndim
    return pl.BlockSpec(a.shape, lambda *_: (0,) * nd)


def _params(sem):
    return pltpu.CompilerParams(dimension_semantics=sem, vmem_limit_bytes=V7X_VMEM_LIMIT)


def _call_with_guests(host_kernel, grid, step_of, inputs, in_specs, out_shape, out_specs, scratch, guests, name):
    def to_spec(s):
        if isinstance(s, pl.BlockSpec):
            return s
        block, fn = s
        return pl.BlockSpec(block, lambda *ids: fn(step_of(*ids)))

    counts = [(len(inputs), len(out_shape), len(scratch))]
    counts += [(len(g["inputs"]), len(g["out_shape"]), len(g["scratch"])) for g in guests]

    def kern(*refs):
        refs = list(refs)
        parts = [[refs.pop(0) for _ in range(c[k])] for k in range(3) for c in counts]
        n = len(counts)
        ins, outs, scrs = parts[:n], parts[n:2 * n], parts[2 * n:]
        step = step_of(*[pl.program_id(a) for a in range(len(grid))])
        grefs = [(*a, *b, *c) for a, b, c in zip(ins[1:], outs[1:], scrs[1:])]
        states = []

        def pre_hook():
            for g, r in zip(guests, grefs):
                g["init"](step, *r)
            states.extend(g["pre"](step, *r) for g, r in zip(guests, grefs))

        def hook():
            for g, st, r in zip(guests, states, grefs):
                g["post"](step, st, *r)

        host_kernel(*ins[0], *outs[0], *scrs[0], pre_hook=pre_hook, hook=hook)

    res = pl.pallas_call(
        kern,
        grid=grid,
        in_specs=list(in_specs) + [to_spec(s) for g in guests for s in g["in_specs"]],
        out_specs=list(out_specs) + [to_spec(s) for g in guests for s in g["out_specs"]],
        out_shape=list(out_shape) + [s for g in guests for s in g["out_shape"]],
        scratch_shapes=list(scratch) + [s for g in guests for s in g["scratch"]],
        compiler_params=_params(("arbitrary",) * len(grid)),
        name=name,
    )(*inputs, *[a for g in guests for a in g["inputs"]])
    res = list(res)
    split = []
    for c in counts:
        split.append([res.pop(0) for _ in range(c[1])])
    return split[0], split[1:]


def _mem_probs(qm, kbd_ref):
    s = _dot(_bf(qm), kbd_ref[...]) * (MEM_HD ** -0.5)
    ps = []
    for h in range(MEM_H):
        sh = s[:, h * N_MEM:(h + 1) * N_MEM]
        e = jnp.exp(sh - jnp.max(sh, axis=-1, keepdims=True))
        ps.append(_bf(e / jnp.sum(e, axis=-1, keepdims=True)))
    return ps


def _mem_pv(ps, vbd_ref):
    out = None
    for h in range(MEM_H):
        t = _dot(ps[h], vbd_ref[h * N_MEM:(h + 1) * N_MEM, :])
        out = t if out is None else out + t
    return out


def _memkv_kernel(mem_ref, g_ref, w_ref, k_ref, v_ref, kbd_ref, vbd_ref):
    h = _bf(_rms(mem_ref[0], g_ref[0]))
    kv = _dot(h, w_ref[0])
    k = kv[:, :MEM_W]
    v = kv[:, MEM_W:]
    kt = k.T
    k_ref[0, 0] = kt
    v_ref[0, 0] = v.T
    kt4 = jnp.concatenate([kt, kt, kt, kt], axis=1)
    keep_k = (_iota((MEM_W, MEM_H * N_MEM), 0) >> 6) == (_iota((MEM_W, MEM_H * N_MEM), 1) >> 8)
    kbd_ref[0, 0] = _bf(jnp.where(keep_k, kt4, 0.0))
    v4 = jnp.concatenate([v, v, v, v], axis=0)
    keep_v = (_iota((MEM_H * N_MEM, MEM_W), 0) >> 8) == (_iota((MEM_H * N_MEM, MEM_W), 1) >> 6)
    vbd_ref[0, 0] = _bf(jnp.where(keep_v, v4, 0.0))


def _memkv_call(mem, norm_mem, w_mem_kv):
    nb = mem.shape[0]
    nl = w_mem_kv.shape[0]
    g = norm_mem.reshape(nl, 1, D)
    w = _bf(w_mem_kv)
    return pl.pallas_call(
        _memkv_kernel,
        grid=(nl, nb),
        in_specs=[
            pl.BlockSpec((1, N_MEM, D), lambda l, b: (b, 0, 0)),
            pl.BlockSpec((1, 1, D), lambda l, b: (l, 0, 0)),
            pl.BlockSpec((1, D, 2 * MEM_W), lambda l, b: (l, 0, 0)),
        ],
        out_specs=[
            pl.BlockSpec((1, 1, MEM_W, N_MEM), lambda l, b: (l, b, 0, 0)),
            pl.BlockSpec((1, 1, MEM_W, N_MEM), lambda l, b: (l, b, 0, 0)),
            pl.BlockSpec((1, 1, MEM_W, MEM_H * N_MEM), lambda l, b: (l, b, 0, 0)),
            pl.BlockSpec((1, 1, MEM_H * N_MEM, MEM_W), lambda l, b: (l, b, 0, 0)),
        ],
        out_shape=[
            jax.ShapeDtypeStruct((nl, nb, MEM_W, N_MEM), F32),
            jax.ShapeDtypeStruct((nl, nb, MEM_W, N_MEM), F32),
            jax.ShapeDtypeStruct((nl, nb, MEM_W, MEM_H * N_MEM), BF16),
            jax.ShapeDtypeStruct((nl, nb, MEM_H * N_MEM, MEM_W), BF16),
        ],
        compiler_params=_params(("arbitrary", "arbitrary")),
        name="mem_kv",
    )(mem, g, w)


def _proj_a(x, gpre_ref, wqk_ref, wv_ref, wr_ref, wmisc_ref, wg_ref, bg_ref, glan_ref, mem_refs=None):
    h = _bf(_rms(x, gpre_ref[...]))
    misc = _dot(h, wmisc_ref[...])
    glr = misc[:, :128]
    qm = misc[:, 128:]
    ps = None if mem_refs is None else _mem_probs(qm, mem_refs[0])
    g = _log_sigmoid(_dot(_bf(glr), wg_ref[...]) + bg_ref[...]) * (1.0 / GATE_NORM)
    gate = glan_ref[...] * _silu(_dot(h, wr_ref[...]))
    qk = _dot(h, wqk_ref[...])
    q = qk[:, :GLA_KW] * (GLA_DK ** -0.5)
    k = qk[:, GLA_KW:]
    v = _dot(h, wv_ref[...])
    return q, k, g, v, gate, (qm if mem_refs is None else _mem_pv(ps, mem_refs[1]))


def _gla_out_gate(o, gate):
    vh = _gla_v_head(_iota(o.shape, 1))
    o2 = o * o
    scale = jnp.zeros_like(o)
    for h in range(GLA_H):
        ss = jnp.sum(jnp.where(vh == h, o2, 0.0), axis=-1, keepdims=True) * (1.0 / GLA_DV)
        scale = jnp.where(vh == h, lax.rsqrt(ss + EPS), scale)
    return o * scale * gate


def _mix_residual(x, o_main, o_mem, wo1_ref, wo2_ref, gpost_ref):
    mix = _dot(_bf(o_main), wo1_ref[...]) + _dot(_bf(o_mem), wo2_ref[...])
    return x + _rms(mix, gpost_ref[...])


def _make_gla_chunk(q_s, k_s, g_s, v_s, o_s, state_s, p_s):
    C = GLA_CHUNK

    ri = _iota((C, GLA_KW), 0)
    kh = _gla_k_head(_iota((C, GLA_KW), 1))
    khcat = jnp.concatenate([kh, kh, kh], axis=1)
    vh = _gla_v_head(_iota((C, GLA_VW), 1))
    tri = _bf(jnp.where(_iota((C, C), 0) >= _iota((C, C), 1), 1.0, 0.0))
    d_rs = _iota((C, 4 * C), 0) - (_iota((C, 4 * C), 1) & (C - 1))
    band = jnp.where((d_rs >= 0) & (d_rs <= (_iota((C, 4 * C), 0) & (GLA_SUB - 1))), d_rs, -1)
    ebc = _bf(jnp.where(_gla_k_head(_iota((GLA_KW, 4 * C), 0)) == (_iota((GLA_KW, 4 * C), 1) >> 6),
                        1.0, 0.0))
    blockmask = _gla_v_head(_iota((GLA_VW, GLA_KW), 0)) == _gla_k_head(_iota((GLA_VW, GLA_KW), 1))

    def chunk(c, carry):
        r0 = pl.multiple_of(c * C, C)
        qc = q_s[pl.ds(r0, C), :]
        kc = k_s[pl.ds(r0, C), :]
        gc = g_s[pl.ds(r0, C), :]
        vc = v_s[pl.ds(r0, C), :]
        b = _exact_dot(tri, gc)

        ref1 = jnp.broadcast_to(b[31:32, :], b.shape)
        ref2 = jnp.where(ri < 32, jnp.broadcast_to(b[15:16, :], b.shape),
                         jnp.broadcast_to(b[47:48, :], b.shape))
        q1 = qc * jnp.exp(jnp.minimum(b - ref1, 0.0))
        k1 = kc * jnp.exp(jnp.minimum(ref1 - b, 0.0))
        q2 = qc * jnp.exp(jnp.minimum(b - ref2, 0.0))
        k2 = kc * jnp.exp(jnp.minimum(ref2 - b, 0.0))
        qcat = jnp.concatenate([
            jnp.where(ri >= 32, q1, 0.0),
            jnp.where((ri >= 16) & (ri < 32), q2, 0.0),
            jnp.where(ri >= 48, q2, 0.0)], axis=1)
        kcat = jnp.concatenate([
            jnp.where(ri < 32, k1, 0.0),
            jnp.where(ri < 16, k2, 0.0),
            jnp.where((ri >= 32) & (ri < 48), k2, 0.0)], axis=1)
        kst = _bf(jnp.concatenate([jnp.where(khcat == h, kcat, 0.0) for h in range(GLA_H)], axis=0))
        a_off = _dot_nt(_bf(qcat), kst)

        for dlt in range(GLA_SUB):
            if dlt == 0:
                pr = qc * kc
            else:
                kd = pltpu.roll(kc, dlt, 0)
                bd = pltpu.roll(b, dlt, 0)
                pr = qc * kd * jnp.exp(jnp.minimum(b - bd, 0.0))
            p_s[dlt * C:(dlt + 1) * C, :] = _bf(pr)
        rsum = _dot(p_s[...], ebc)
        a_diag = jnp.zeros((C, 4 * C), F32)
        for dlt in range(GLA_SUB):
            a_diag = jnp.where(band == dlt, rsum[dlt * C:(dlt + 1) * C, :], a_diag)

        vst = _bf(jnp.concatenate([jnp.where(vh == h, vc, 0.0) for h in range(GLA_H)], axis=0))
        o_intra = _dot(_bf(a_off + a_diag), vst)
        st = state_s[...]
        o_inter = _dot_nt(_bf(qc * jnp.exp(b)), _bf(st))
        o_s[pl.ds(r0, C), :] = o_intra + o_inter

        blast = jnp.broadcast_to(b[C - 1:C, :], b.shape)
        kdec = kc * jnp.exp(blast - b)
        kvt = _dot_tn(_bf(vc), _bf(kdec))
        decay = jnp.exp(b[C - 1:C, :])
        state_s[...] = st * decay + jnp.where(blockmask, kvt, 0.0)
        return carry

    return chunk


def _make_gla_chunk_bounded(q_s, k_s, g_s, v_s, o_s, state_s):
    C = GLA_CHUNK
    kh = _gla_k_head(_iota((C, GLA_KW), 1))
    vh = _gla_v_head(_iota((C, GLA_VW), 1))
    tri = _bf(jnp.where(_iota((C, C), 0) >= _iota((C, C), 1), 1.0, 0.0))
    causal = _iota((C, 4 * C), 0) >= (_iota((C, 4 * C), 1) & (C - 1))
    blockmask = _gla_v_head(_iota((GLA_VW, GLA_KW), 0)) == _gla_k_head(_iota((GLA_VW, GLA_KW), 1))

    def group(gi, carry):
        base = gi * (GLA_GROUP * C)
        rows = [pl.ds(pl.multiple_of(base + i * C, C), C) for i in range(GLA_GROUP)]
        bs = [_exact_dot(tri, g_s[r, :]) for r in rows]
        qes = [_bf(q_s[r, :] * jnp.exp(b)) for r, b in zip(rows, bs)]
        kes = [k_s[r, :] * jnp.exp(-b) for r, b in zip(rows, bs)]
        ksts = [_bf(jnp.concatenate([jnp.where(kh == h, ke, 0.0) for h in range(GLA_H)], axis=0)) for ke in kes]
        attn = [_bf(jnp.where(causal, _dot_nt(qe, kst), 0.0)) for qe, kst in zip(qes, ksts)]
        vcs = [v_s[r, :] for r in rows]
        vsts = [_bf(jnp.concatenate([jnp.where(vh == h, vc, 0.0) for h in range(GLA_H)], axis=0)) for vc in vcs]
        o_intra = [_dot(a, vst) for a, vst in zip(attn, vsts)]
        decays = [jnp.exp(b[C - 1:C, :]) for b in bs]
        kvts = [_dot_tn(_bf(vc), _bf(ke * d)) for vc, ke, d in zip(vcs, kes, decays)]
        st = state_s[...]
        for i in range(GLA_GROUP):
            o_s[rows[i], :] = o_intra[i] + _dot_nt(qes[i], _bf(st))
            st = st * decays[i] + jnp.where(blockmask, kvts[i], 0.0)
        state_s[...] = st
        return carry

    return group


def _mixer_a_kernel(x_ref, gpre_ref, wqk_ref, wv_ref, wr_ref, wmisc_ref, wg_ref, bg_ref, glan_ref,
                    kbd_ref, vbd_ref, wo1_ref, wo2_ref, gpost_ref,
                    xo_ref, st_ref,
                    q_s, k_s, g_s, v_s, o_s, state_s, p_s, gate_s, om_s, pre_hook, hook):
    C = GLA_CHUNK
    tm = x_ref.shape[1]
    nchunk = tm // C

    @pl.when(pl.program_id(1) == 0)
    def _():
        state_s[...] = jnp.zeros_like(state_s)

    pre_hook()

    q, k, g, v, gate, o_mem = _proj_a(x_ref[0], gpre_ref, wqk_ref, wv_ref, wr_ref, wmisc_ref, wg_ref, bg_ref,
                                      glan_ref, (kbd_ref.at[0], vbd_ref.at[0]))
    q_s[...] = q
    k_s[...] = k
    g_s[...] = g
    v_s[...] = v
    gate_s[...] = gate
    om_s[...] = _bf(o_mem)
    hook()

    total = jnp.sum(g.reshape(nchunk, C, GLA_KW), axis=1)
    bounded = jnp.min(total) > -GLA_SAFE_DECAY

    @pl.when(bounded)
    def _():
        lax.fori_loop(0, nchunk // GLA_GROUP, _make_gla_chunk_bounded(q_s, k_s, g_s, v_s, o_s, state_s), 0)

    @pl.when(jnp.logical_not(bounded))
    def _():
        lax.fori_loop(0, nchunk, _make_gla_chunk(q_s, k_s, g_s, v_s, o_s, state_s, p_s), 0)

    st_ref[0] = state_s[...]
    o_main = _gla_out_gate(o_s[...], gate_s[...])
    xo_ref[0] = _mix_residual(x_ref[0], o_main, om_s[...], wo1_ref, wo2_ref, gpost_ref)


def _const_spec(a):
    nd = a.ndim
    return pl.BlockSpec(a.shape, lambda *_: (0,) * nd, pipeline_mode=pl.Buffered(1))


def _mixer_a_call(x, wa, kbd, vbd, make_guests):
    nb, seq, _ = x.shape
    tm = TM_PROMPT
    nt = seq // tm
    weights = [wa["gpre"], wa["wqk"], wa["wv"], wa["wr"], wa["wmisc"], wa["wg"], wa["bg"], wa["glan"]]
    tail = [wa["wo1"], wa["wo2"], wa["gpost"]]
    return _call_with_guests(
        _mixer_a_kernel,
        grid=(nb, nt),
        step_of=lambda b, t: b * nt + t,
        inputs=[x, *weights, kbd, vbd, *tail],
        in_specs=([pl.BlockSpec((1, tm, D), lambda b, t: (b, t, 0))]
                  + [_const_spec(w) for w in weights]
                  + [pl.BlockSpec((1, MEM_W, MEM_H * N_MEM), lambda b, t: (b, 0, 0)),
                     pl.BlockSpec((1, MEM_H * N_MEM, MEM_W), lambda b, t: (b, 0, 0))]
                  + [_const_spec(w) for w in tail]),
        out_specs=[pl.BlockSpec((1, tm, D), lambda b, t: (b, t, 0)),
                   pl.BlockSpec((1, GLA_VW, GLA_KW), lambda b, t: (b, 0, 0))],
        out_shape=[jax.ShapeDtypeStruct((nb, seq, D), F32),
                   jax.ShapeDtypeStruct((nb, GLA_VW, GLA_KW), F32)],
        scratch=[
            pltpu.VMEM((tm, GLA_KW), F32), pltpu.VMEM((tm, GLA_KW), F32), pltpu.VMEM((tm, GLA_KW), F32),
            pltpu.VMEM((tm, GLA_VW), F32), pltpu.VMEM((tm, GLA_VW), F32),
            pltpu.VMEM((GLA_VW, GLA_KW), F32),
            pltpu.VMEM((GLA_SUB * GLA_CHUNK, GLA_KW), BF16),
            pltpu.VMEM((tm, GLA_VW), F32), pltpu.VMEM((tm, MEM_W), BF16),
        ],
        guests=make_guests(nb * nt),
        name="mixer_a_prompt",
    )


def _ffn_rows(x, gpre_ref, wup_ref, wdn_ref, gpost_ref, acc_ref):
    h = _bf(_rms(x, gpre_ref[0]))
    nff = D_FF // FF_CHUNK
    for j in range(nff - 1):
        cols = slice(j * FF_CHUNK, (j + 1) * FF_CHUNK)
        u = jnp.maximum(_dot(h, _bf(wup_ref[0, :, cols])), 0.0)
        d = _dot(_bf(u * u), _bf(wdn_ref[0, cols, :]))
        if j == 0:
            acc_ref[...] = d
        else:
            acc_ref[...] += d
    cols = slice((nff - 1) * FF_CHUNK, nff * FF_CHUNK)
    u = jnp.maximum(_dot(h, _bf(wup_ref[0, :, cols])), 0.0)
    uu = _bf(u * u)
    wd = _bf(wdn_ref[0, cols, :])
    m = x.shape[0]
    bounds = (0, m) if m < 256 else (0, m // 2, m)
    parts = []
    for lo, hi in zip(bounds[:-1], bounds[1:]):
        tot = acc_ref[lo:hi, :] + _dot(uu[lo:hi], wd)
        parts.append(x[lo:hi] + _rms(tot, gpost_ref[0]))
    return parts[0] if len(parts) == 1 else jnp.concatenate(parts, axis=0)


def _ffn_kernel(xp_ref, xs_ref, gpre_ref, wup_ref, wdn_ref, gpost_ref, yp_ref, ys_ref, acc_s):
    i = pl.program_id(0)
    last = pl.num_programs(0) - 1

    @pl.when(i < last)
    def _():
        yp_ref[...] = _ffn_rows(xp_ref[...], gpre_ref, wup_ref, wdn_ref, gpost_ref, acc_s)

    @pl.when(i == last)
    def _():
        ns = xs_ref.shape[0]
        ys_ref[...] = _ffn_rows(xs_ref[...], gpre_ref, wup_ref, wdn_ref, gpost_ref, acc_s.at[0:ns, :])


def _ffn_call(xp, xs, norm_pre, w_up, w_down, norm_post, layer):
    n, ns = xp.shape[0], xs.shape[0]
    tm = TM_PROMPT
    nt = n // tm
    gains = [a.reshape(a.shape[0], 1, D) for a in (norm_pre, norm_post)]
    gspec = pl.BlockSpec((1, 1, D), lambda i: (layer, 0, 0))
    wspec = lambda w: pl.BlockSpec((1,) + w.shape[1:], lambda i: (layer, 0, 0), pipeline_mode=pl.Buffered(1))
    tile = pl.BlockSpec((tm, D), lambda i: (jnp.minimum(i, nt - 1), 0))
    return pl.pallas_call(
        _ffn_kernel,
        grid=(nt + 1,),
        in_specs=[tile, _full_spec(xs), gspec, wspec(w_up), wspec(w_down), gspec],
        out_specs=[tile, _full_spec(xs)],
        out_shape=[jax.ShapeDtypeStruct((n, D), F32), jax.ShapeDtypeStruct((ns, D), F32)],
        scratch_shapes=[pltpu.VMEM((tm, D), F32)],
        compiler_params=_params(("arbitrary",)),
        name="ffn",
    )(xp, xs, gains[0], w_up, w_down, gains[1])


def _t5_bucket(dist):
    max_exact = N_BUCKETS // 2
    n = np.maximum(dist, 0)
    nf = np.maximum(n, 1).astype(np.float32)
    large = max_exact + (np.log(nf / np.float32(max_exact)) / np.float32(math.log(MAX_DISTANCE / max_exact))
                         * np.float32(N_BUCKETS - max_exact)).astype(np.int32)
    large = np.minimum(large, N_BUCKETS - 1)
    return np.where(n < max_exact, n, large)


def _swa_bias_tables():
    qi = np.arange(WINDOW)[:, None] + WINDOW
    kj = np.arange(2 * WINDOW)[None, :]
    dist = qi - kj
    valid = (dist >= 0) & (dist < WINDOW)
    return np.where(valid, _t5_bucket(dist), -1).astype(np.int32)


def _mixer_b_kernel(rb_ref, sink_ref,
                    x_ref, bkt_ref, gkv_ref, wkv_ref, gpre_ref, wq_ref, wqm_ref,
                    kbd_ref, vbd_ref, wo1_ref, wo2_ref, gpost_ref,
                    xo_ref, kc_ref, vc_ref,
                    kbuf, vbuf, q_s, o_s, bias_s, qm_s, pre_hook, hook):
    W = WINDOW
    tm = x_ref.shape[1]
    bb = pl.program_id(0)
    t = pl.program_id(1)
    nheads = SWA_G * SWA_KVH

    @pl.when((bb == 0) & (t == 0))
    def _():
        bkt = bkt_ref[...]
        own = _iota((W, 2 * W), 1) >= W
        for i in range(nheads):
            def add_bucket(n, acc):
                return jnp.where(bkt == n, rb_ref[n, i], acc)
            tab = lax.fori_loop(0, N_BUCKETS, add_bucket, jnp.zeros((W, 2 * W), F32))
            tab = jnp.where(bkt < 0, NEG_INF, tab)
            bias_s[0, i] = tab
            bias_s[1, i] = jnp.where(own, tab, NEG_INF)

    @pl.when(t == 0)
    def _():
        kbuf[0:W, :] = jnp.zeros((W, SWA_KVW), F32)
        vbuf[0:W, :] = jnp.zeros((W, SWA_KVW), F32)

    pre_hook()
    x = x_ref[0]
    xn = x * lax.rsqrt(jnp.mean(x * x, axis=-1, keepdims=True) + EPS)
    kv = _dot(_bf(xn * gkv_ref[...]), wkv_ref[...])
    kbuf[W:W + tm, :] = kv[:, :SWA_KVW]
    vbuf[W:W + tm, :] = kv[:, SWA_KVW:]
    kc_ref[0] = kv[tm - W:, :SWA_KVW]
    vc_ref[0] = kv[tm - W:, SWA_KVW:]

    h = _bf(xn * gpre_ref[...])
    q_s[...] = _dot(h, wq_ref[...]) * (SWA_HD ** -0.5)
    qm_s[...] = _dot(h, wqm_ref[...])
    hook()

    lane_head = _iota((W, SWA_KVW), 1) >> 6
    key_head = _iota((2 * W, SWA_KVW), 1) >> 6

    def scores(j):
        r0 = pl.multiple_of(j * W, W)
        qrows = []
        for gi in range(SWA_G):
            qg = q_s[pl.ds(r0, W), gi * SWA_KVW:(gi + 1) * SWA_KVW]
            for hh in range(SWA_KVH):
                qrows.append(jnp.where(lane_head == hh, qg, 0.0))
        return _dot_nt(_bf(jnp.concatenate(qrows, axis=0)), _bf(kbuf[pl.ds(r0, 2 * W), :]))

    def probs(j, s_all, gi):
        tab = jnp.where((j == 0) & (t == 0), 1, 0)
        ps = []
        for hh in range(SWA_KVH):
            i = gi * SWA_KVH + hh
            s = s_all[i * W:(i + 1) * W, :] + bias_s[tab, i]
            sink = sink_ref[i]
            mx = jnp.maximum(jnp.max(s, axis=-1, keepdims=True), sink)
            p = jnp.exp(s - mx)
            p = p / (jnp.sum(p, axis=-1, keepdims=True) + jnp.exp(sink - mx))
            ps.append(_bf(p))
        return jnp.concatenate(ps, axis=1)

    def pair(jj, carry):
        js = [jj * SWA_BLOCKS_PER_TRIP + i for i in range(SWA_BLOCKS_PER_TRIP)]
        s_alls = [scores(j) for j in js]
        for j, s_all in zip(js, s_alls):
            r0 = pl.multiple_of(j * W, W)
            vb = vbuf[pl.ds(r0, 2 * W), :]
            vst = _bf(jnp.concatenate([jnp.where(key_head == hh, vb, 0.0) for hh in range(SWA_KVH)], axis=0))
            for gi in range(SWA_G):
                o_s[pl.ds(r0, W), gi * SWA_KVW:(gi + 1) * SWA_KVW] = _dot(probs(j, s_all, gi), vst)
        return carry

    lax.fori_loop(0, tm // (W * SWA_BLOCKS_PER_TRIP), pair, 0)

    kbuf[0:W, :] = kbuf[tm:tm + W, :]
    vbuf[0:W, :] = vbuf[tm:tm + W, :]

    o_mem = _mem_pv(_mem_probs(qm_s[...], kbd_ref.at[0]), vbd_ref.at[0])
    xo_ref[0] = _mix_residual(x_ref[0], o_s[...], o_mem, wo1_ref, wo2_ref, gpost_ref)


def _mixer_b_call(x, wb, kbd, vbd, make_guests):
    nb, seq, _ = x.shape
    tm = TM_PROMPT
    nt = seq // tm
    bkt = jnp.asarray(_swa_bias_tables())
    head = [bkt, wb["gkv"], wb["wkv"], wb["gpre"], wb["wq"], wb["wqm"]]
    tail = [wb["wo1"], wb["wo2"], wb["gpost"]]
    smem = pl.BlockSpec(memory_space=pltpu.SMEM)
    return _call_with_guests(
        _mixer_b_kernel,
        grid=(nb, nt),
        step_of=lambda b, t: b * nt + t,
        inputs=[wb["rb"], wb["sinks"], x, *head, kbd, vbd, *tail],
        in_specs=([smem, smem, pl.BlockSpec((1, tm, D), lambda b, t: (b, t, 0))]
                  + [_const_spec(w) for w in head]
                  + [pl.BlockSpec((1, MEM_W, MEM_H * N_MEM), lambda b, t: (b, 0, 0)),
                     pl.BlockSpec((1, MEM_H * N_MEM, MEM_W), lambda b, t: (b, 0, 0))]
                  + [_const_spec(w) for w in tail]),
        out_specs=[pl.BlockSpec((1, tm, D), lambda b, t: (b, t, 0)),
                   pl.BlockSpec((1, WINDOW, SWA_KVW), lambda b, t: (b, 0, 0)),
                   pl.BlockSpec((1, WINDOW, SWA_KVW), lambda b, t: (b, 0, 0))],
        out_shape=[jax.ShapeDtypeStruct((nb, seq, D), F32),
                   jax.ShapeDtypeStruct((nb, WINDOW, SWA_KVW), F32),
                   jax.ShapeDtypeStruct((nb, WINDOW, SWA_KVW), F32)],
        scratch=[
            pltpu.VMEM((tm + WINDOW, SWA_KVW), F32), pltpu.VMEM((tm + WINDOW, SWA_KVW), F32),
            pltpu.VMEM((tm, SWA_QW), F32), pltpu.VMEM((tm, SWA_QW), F32),
            pltpu.VMEM((2, SWA_G * SWA_KVH, WINDOW, 2 * WINDOW), F32),
            pltpu.VMEM((tm, MEM_W), F32),
        ],
        guests=make_guests(nb * nt),
        name="mixer_b_prompt",
    )


def _row_to_col(row, eye):
    return jnp.sum(jnp.where(eye, jnp.broadcast_to(row, eye.shape), 0.0), axis=1, keepdims=True)


def _eye(n):
    return _iota((n, n), 0) == _iota((n, n), 1)


def _pre_a_kernel(x_ref, gpre_ref, wqk_ref, wv_ref, wr_ref, wmisc_ref, wg_ref, bg_ref, glan_ref,
                  q_ref, k_ref, g_ref, v_ref, gate_ref, qm_ref):
    q, k, g, v, gate, qm = _proj_a(x_ref[...], gpre_ref, wqk_ref, wv_ref, wr_ref, wmisc_ref, wg_ref, bg_ref,
                                   glan_ref)
    q_ref[...] = q
    k_ref[...] = k
    g_ref[...] = g
    v_ref[...] = v
    gate_ref[...] = gate
    qm_ref[...] = qm


def _pre_a_call(x2d, wa):
    n = x2d.shape[0]
    ws = [wa["gpre"], wa["wqk"], wa["wv"], wa["wr"], wa["wmisc"], wa["wg"], wa["bg"], wa["glan"]]
    shapes = [(n, GLA_KW), (n, GLA_KW), (n, GLA_KW), (n, GLA_VW), (n, GLA_VW), (n, MEM_W)]
    return pl.pallas_call(
        _pre_a_kernel,
        grid=(1,),
        in_specs=[_full_spec(x2d)] + [_full_spec(w) for w in ws],
        out_specs=[pl.BlockSpec(s, lambda i: (0, 0)) for s in shapes],
        out_shape=[jax.ShapeDtypeStruct(s, F32) for s in shapes],
        compiler_params=_params(("arbitrary",)),
        name="pre_a_sample",
    )(x2d, *ws)


def _gla_step_guest(qt, kt, gt, vt, state5, nsteps):
    n = qt.shape[1]
    nblk = GLA_DK // GLA_DK_BLOCK
    nact = GLA_H * nblk
    assert nact <= nsteps
    act = lambda s: jnp.minimum(s, nact - 1)

    def init(step, qt_ref, kt_ref, gt_ref, vt_ref, s_ref, so_ref, ot_ref):
        @pl.when((step < nact) & (step % nblk == 0))
        def _():
            ot_ref[...] = jnp.zeros_like(ot_ref)

    def pre(step, qt_ref, kt_ref, gt_ref, vt_ref, s_ref, so_ref, ot_ref):
        vt_blk = vt_ref[...]
        acc = jnp.zeros_like(vt_blk)
        for d in range(GLA_DK_BLOCK):
            s_new = jnp.exp(gt_ref[d:d + 1, :]) * s_ref[0, 0, d] + kt_ref[d:d + 1, :] * vt_blk
            so_ref[0, 0, d] = s_new
            acc = acc + qt_ref[d:d + 1, :] * s_new
        ot_ref[...] += jnp.where(step < nact, acc, 0.0)

    rows = ((GLA_DK_BLOCK, n), lambda s: (act(s), 0))
    head = ((GLA_DV, n), lambda s: (act(s) // nblk, 0))
    st = ((1, 1, GLA_DK_BLOCK, GLA_DV, n), lambda s: (0, act(s) // nblk, act(s) % nblk, 0, 0))
    return dict(inputs=[qt, kt, gt, vt, state5], in_specs=[rows, rows, rows, head, st],
                out_shape=[jax.ShapeDtypeStruct(state5.shape, F32), jax.ShapeDtypeStruct((GLA_VW, n), F32)],
                out_specs=[st, head], scratch=[], init=init, pre=pre, post=lambda step, state, *refs: None)


def _mem_step_pre(step, qm_ref, mk_ref, mv_ref, o_ref):
    own = (_iota((8, MEM_W), 1) >> 6) == _iota((8, MEM_W), 0)
    ps = []
    for i in range(qm_ref.shape[0]):
        q8 = jnp.where(own, jnp.broadcast_to(qm_ref[i], (8, MEM_W)), 0.0)
        s = _dot(_bf(q8), _bf(mk_ref[0, i].reshape(MEM_W, N_MEM))) * (MEM_HD ** -0.5)
        e = jnp.exp(s - jnp.max(s, axis=1, keepdims=True))
        ps.append(_bf(e / jnp.sum(e, axis=1, keepdims=True)))
    return ps


def _mem_step_post(step, ps, qm_ref, mk_ref, mv_ref, o_ref):
    own = (_iota((8, MEM_W), 1) >> 6) == _iota((8, MEM_W), 0)
    for i, p in enumerate(ps):
        res = _dot_nt(p, _bf(mv_ref[0, i].reshape(MEM_W, N_MEM)))
        o_ref[i] = jnp.sum(jnp.where(own, res, 0.0), axis=0, keepdims=True)


def _mem_step_guest(qm, mk5, mv5, layer, nsteps):
    n = qm.shape[0]
    rb = n // nsteps
    blk = ((1, rb, MEM_H, MEM_HD, N_MEM), lambda s: (layer, s, 0, 0, 0))
    rows = ((rb, 1, MEM_W), lambda s: (s, 0, 0))
    return dict(inputs=[qm.reshape(n, 1, MEM_W), mk5, mv5], in_specs=[rows, blk, blk],
                out_shape=[jax.ShapeDtypeStruct((n, 1, MEM_W), F32)], out_specs=[rows], scratch=[],
                init=lambda step, *refs: None, pre=_mem_step_pre, post=_mem_step_post)


def _post_a_kernel(x_ref, o_ref, gate_ref, om_ref, wo1_ref, wo2_ref, gpost_ref, xo_ref):
    o_main = _gla_out_gate(o_ref[...], gate_ref[...])
    xo_ref[...] = _mix_residual(x_ref[...], o_main, om_ref[...], wo1_ref, wo2_ref, gpost_ref)


def _post_a_call(x2d, o, gate, om, wa):
    args = [x2d, o, gate, om, wa["wo1"], wa["wo2"], wa["gpost"]]
    return pl.pallas_call(
        _post_a_kernel,
        grid=(1,),
        in_specs=[_full_spec(a) for a in args],
        out_specs=_full_spec(x2d),
        out_shape=jax.ShapeDtypeStruct(x2d.shape, F32),
        compiler_params=_params(("arbitrary",)),
        name="post_a_sample",
    )(*args)


def _pre_b_kernel(x_ref, gkv_ref, wkv_ref, gpre_ref, wq_ref, wqm_ref, ks_ref, vs_ref, q_ref, qm_ref):
    x = x_ref[...]
    kv = _dot(_bf(_rms(x, gkv_ref[...])), wkv_ref[...])
    ks_ref[...] = kv[:, :SWA_KVW]
    vs_ref[...] = kv[:, SWA_KVW:]
    h = _bf(_rms(x, gpre_ref[...]))
    q_ref[...] = _dot(h, wq_ref[...])
    qm_ref[...] = _dot(h, wqm_ref[...])


def _pre_b_call(x2d, wb):
    n = x2d.shape[0]
    ws = [wb["gkv"], wb["wkv"], wb["gpre"], wb["wq"], wb["wqm"]]
    widths = [SWA_KVW, SWA_KVW, SWA_QW, MEM_W]
    return pl.pallas_call(
        _pre_b_kernel,
        grid=(1,),
        in_specs=[_full_spec(x2d)] + [_full_spec(w) for w in ws],
        out_specs=[pl.BlockSpec((n, w), lambda i: (0, 0)) for w in widths],
        out_shape=[jax.ShapeDtypeStruct((n, w), F32) for w in widths],
        compiler_params=_params(("arbitrary",)),
        name="pre_b_sample",
    )(x2d, *ws)


def _sample_buckets():
    dist = (WINDOW - 1) - np.arange(WINDOW)
    return _t5_bucket(dist).astype(np.int32).reshape(1, WINDOW)


def _swa_step_init(step, sink_ref, rb_ref, bkt_ref, kc_ref, vc_ref, kst_ref, vst_ref, q_ref,
                   kn_ref, vn_ref, o_ref, bias_s, sink_s, s_s, p_s):
    W = WINDOW
    R = SWA_KVH * 8

    @pl.when(step == 0)
    def _():
        bkt = bkt_ref[...]
        rid = _iota((R, W), 0)
        bias = jnp.zeros((R, W), F32)
        sink = jnp.zeros((R, W), F32)
        for h in range(SWA_KVH):
            for g in range(SWA_G):
                idx = g * SWA_KVH + h
                def add_bucket(n, acc):
                    return jnp.where(bkt == n, rb_ref[n, idx], acc)
                brow = lax.fori_loop(0, N_BUCKETS, add_bucket, jnp.zeros((1, W), F32))
                bias = jnp.where(rid == h * 8 + g, brow, bias)
                sink = jnp.where(rid == h * 8 + g, sink_ref[idx], sink)
        bias_s[...] = bias
        sink_s[...] = sink


def _swa_step_pre(step, sink_ref, rb_ref, bkt_ref, kc_ref, vc_ref, kst_ref, vst_ref, q_ref,
                  kn_ref, vn_ref, o_ref, bias_s, sink_s, s_s, p_s):
    W = WINDOW
    R = SWA_KVH * 8
    rb = q_ref.shape[0]
    base = step * rb
    last = _iota((SWA_KVW, W), 1) == W - 1
    own = (_iota((R, SWA_KVW), 1) >> 6) == (_iota((R, SWA_KVW), 0) >> 3)
    kst = kst_ref[...]
    vst = vst_ref[...]
    for i in range(rb):
        shift = W - 1 - (base + i)
        kn = jnp.where(last, pltpu.roll(kst, shift, 1), pltpu.roll(kc_ref[i].reshape(SWA_KVW, W), W - 1, 1))
        vn = jnp.where(last, pltpu.roll(vst, shift, 1), pltpu.roll(vc_ref[i].reshape(SWA_KVW, W), W - 1, 1))
        kn_ref[i] = kn.reshape(SWA_KVH, SWA_HD, W)
        vn_ref[i] = vn.reshape(SWA_KVH, SWA_HD, W)
        q32 = jnp.where(own, jnp.concatenate([q_ref[i]] * SWA_KVH, axis=0), 0.0)
        s_s[i * R:(i + 1) * R, :] = _dot(_bf(q32), _bf(kn))
    s = s_s[...] * (SWA_HD ** -0.5) + jnp.concatenate([bias_s[...]] * rb, axis=0)
    sink = jnp.concatenate([sink_s[...]] * rb, axis=0)
    mx = jnp.maximum(jnp.max(s, axis=1, keepdims=True), sink)
    p = jnp.exp(s - mx)
    p_s[...] = _bf(p / (jnp.sum(p, axis=1, keepdims=True) + jnp.exp(sink - mx)))


def _swa_step_post(step, _, sink_ref, rb_ref, bkt_ref, kc_ref, vc_ref, kst_ref, vst_ref, q_ref,
                   kn_ref, vn_ref, o_ref, bias_s, sink_s, s_s, p_s):
    W = WINDOW
    R = SWA_KVH * 8
    own = (_iota((R, SWA_KVW), 1) >> 6) == (_iota((R, SWA_KVW), 0) >> 3)
    for i in range(q_ref.shape[0]):
        vn = vn_ref[i].reshape(SWA_KVW, W)
        res = jnp.where(own, _dot_nt(p_s[i * R:(i + 1) * R, :], _bf(vn)), 0.0)
        o_ref[i] = res[0:8] + res[8:16] + res[16:24] + res[24:32]


def _swa_step_guest(kc4, vc4, ks, vs, q, wb, nsteps):
    n = q.shape[0]
    rb = n // nsteps
    bkt = jnp.asarray(_sample_buckets())
    q8 = jnp.pad(q.reshape(n, SWA_G, SWA_KVW), ((0, 0), (0, 8 - SWA_G), (0, 0)))
    row3 = lambda r, w: ((rb, r, w), lambda s: (s, 0, 0))
    cache = ((rb, SWA_KVH, SWA_HD, WINDOW), lambda s: (s, 0, 0, 0))
    whole = ((SWA_KVW, n), lambda s: (0, 0))
    smem = pl.BlockSpec(memory_space=pltpu.SMEM)
    return dict(inputs=[wb["sinks"], wb["rb"], bkt, kc4, vc4, ks.T, vs.T, q8],
                in_specs=[smem, smem, _full_spec(bkt), cache, cache, whole, whole, row3(8, SWA_KVW)],
                out_shape=[jax.ShapeDtypeStruct(kc4.shape, F32), jax.ShapeDtypeStruct(vc4.shape, F32),
                           jax.ShapeDtypeStruct((n, 8, SWA_KVW), F32)],
                out_specs=[cache, cache, row3(8, SWA_KVW)],
                scratch=[pltpu.VMEM((SWA_KVH * 8, WINDOW), F32), pltpu.VMEM((SWA_KVH * 8, WINDOW), F32),
                         pltpu.VMEM((rb * SWA_KVH * 8, WINDOW), F32),
                         pltpu.VMEM((rb * SWA_KVH * 8, WINDOW), BF16)],
                init=_swa_step_init, pre=_swa_step_pre, post=_swa_step_post)


def _post_b_kernel(x_ref, o_ref, om_ref, wo1_ref, wo2_ref, gpost_ref, xo_ref):
    xo_ref[...] = _mix_residual(x_ref[...], o_ref[...], om_ref[...], wo1_ref, wo2_ref, gpost_ref)


def _post_b_call(x2d, o, om, wb):
    args = [x2d, o, om, wb["wo1"], wb["wo2"], wb["gpost"]]
    return pl.pallas_call(
        _post_b_kernel,
        grid=(1,),
        in_specs=[_full_spec(a) for a in args],
        out_specs=_full_spec(x2d),
        out_shape=jax.ShapeDtypeStruct(x2d.shape, F32),
        compiler_params=_params(("arbitrary",)),
        name="post_b_sample",
    )(*args)


def _prep_weights(norm_mix_pre, norm_mix_post, w_in_a, w_gate_up, b_gate,
                  gla_norm, w_in_b, sinks, norm_kv, w_kv, rel_bias, w_out):
    row = lambda g: g.reshape(1, -1)
    wa_in = w_in_a[0]
    c_v = 2 * GLA_KW
    c_r = c_v + GLA_VW
    c_g = c_r + GLA_VW
    c_m = c_g + GATE_RANK
    wmisc = jnp.concatenate([wa_in[:, c_g:c_m], jnp.zeros((D, 128 - GATE_RANK), F32), wa_in[:, c_m:]], axis=1)
    wg = jnp.zeros((128, GLA_KW), F32).at[:GATE_RANK].set(w_gate_up[0])
    wa = dict(
        gpre=row(norm_mix_pre[0]), wqk=_bf(wa_in[:, :c_v]), wv=_bf(wa_in[:, c_v:c_r]), wr=_bf(wa_in[:, c_r:c_g]),
        wmisc=_bf(wmisc), wg=_bf(wg), bg=row(b_gate[0]), glan=row(jnp.tile(gla_norm[0], GLA_H)),
        wo1=_bf(w_out[0][:GLA_VW]), wo2=_bf(w_out[0][GLA_VW:]), gpost=row(norm_mix_post[0]))
    wb_in = w_in_b[0]
    wq = wb_in[:, :SWA_QW].reshape(D, SWA_KVH, SWA_G, SWA_HD).transpose(0, 2, 1, 3).reshape(D, SWA_QW)
    wo1 = w_out[1][:SWA_QW].reshape(SWA_KVH, SWA_G, SWA_HD, D).transpose(1, 0, 2, 3).reshape(SWA_QW, D)
    rb = rel_bias.reshape(N_BUCKETS, SWA_KVH, SWA_G).transpose(0, 2, 1).reshape(N_BUCKETS, SWA_G * SWA_KVH)
    sk = sinks[0].reshape(SWA_KVH, SWA_G).T.reshape(SWA_G * SWA_KVH)
    wb = dict(
        gkv=row(norm_kv), wkv=_bf(w_kv), gpre=row(norm_mix_pre[1]), wq=_bf(wq), wqm=_bf(wb_in[:, SWA_QW:]),
        wo1=_bf(wo1), wo2=_bf(w_out[1][SWA_QW:]), gpost=row(norm_mix_post[1]), rb=rb, sinks=sk)
    return wa, wb


def kernel(x_prompt, x_sample, state_gla, cache_swa_k, cache_swa_v, cache_mem_k, cache_mem_v, mem_prompt,
           norm_mix_pre, norm_mix_post, norm_ffn_pre, norm_ffn_post, norm_mem, w_mem_kv, w_in_a, w_gate_up,
           b_gate, gla_norm, w_in_b, sinks, norm_kv, w_kv, rel_bias, w_out, w_ffn_up, w_ffn_down):
    wa, wb = _prep_weights(norm_mix_pre, norm_mix_post, w_in_a, w_gate_up, b_gate, gla_norm, w_in_b, sinks,
                           norm_kv, w_kv, rel_bias, w_out)
    ffn = lambda xp, xs_, l: _ffn_call(xp, xs_, norm_ffn_pre, w_ffn_up, w_ffn_down, norm_ffn_post, l)
    nb, seq, _ = x_prompt.shape
    ns = x_sample.shape[0]

    xs = x_sample.reshape(ns, D)
    state5 = jnp.transpose(state_gla, (0, 2, 3, 4, 1))
    mk5 = jnp.transpose(cache_mem_k, (0, 1, 3, 4, 2))
    mv5 = jnp.transpose(cache_mem_v, (0, 1, 3, 4, 2))
    kc4 = jnp.transpose(cache_swa_k, (0, 2, 3, 1))
    vc4 = jnp.transpose(cache_swa_v, (0, 2, 3, 1))

    mkt, mvt, kbd, vbd = _memkv_call(mem_prompt, norm_mem, w_mem_kv)
    q, k, g, v, gate, qm = _pre_a_call(xs, wa)
    (x1, st), ((state5_new, ot), (om,)) = _mixer_a_call(
        x_prompt, wa, kbd[0], vbd[0],
        lambda nsteps: [_gla_step_guest(q.T, k.T, g.T, v.T, state5, nsteps),
                        _mem_step_guest(qm, mk5, mv5, 0, nsteps)])
    xs1 = _post_a_call(xs, ot.T, gate, om.reshape(ns, MEM_W), wa)
    x2, xs2 = ffn(x1.reshape(nb * seq, D), xs1, 0)
    ks, vs, qb, qmb = _pre_b_call(xs2, wb)
    (x3, kc, vc), ((kn4, vn4, o8), (omb,)) = _mixer_b_call(
        x2.reshape(nb, seq, D), wb, kbd[1], vbd[1],
        lambda nsteps: [_swa_step_guest(kc4, vc4, ks, vs, qb, wb, nsteps),
                        _mem_step_guest(qmb, mk5, mv5, 1, nsteps)])
    xs3 = _post_b_call(xs2, o8[:, :SWA_G].reshape(ns, SWA_QW), omb.reshape(ns, MEM_W), wb)
    y_prompt, y_sample = ffn(x3.reshape(nb * seq, D), xs3, 1)
    y_prompt = y_prompt.reshape(nb, seq, D)
    y_sample = y_sample.reshape(ns, 1, D)
    st4 = st.reshape(nb, GLA_H, GLA_DV, GLA_H, GLA_DK)
    state_prompt = jnp.stack([st4[:, h, :, h, :] for h in range(GLA_H)], axis=1).transpose(0, 1, 3, 2)[None]
    to_mem = lambda t: t.reshape(2, nb, MEM_H, MEM_HD, N_MEM).transpose(0, 1, 4, 2, 3)
    swa_shape = (nb, WINDOW, SWA_KVH, SWA_HD)

    return (y_prompt, y_sample, state_prompt,
            jnp.transpose(state5_new, (0, 4, 1, 2, 3)),
            kc.reshape(swa_shape), vc.reshape(swa_shape),
            jnp.transpose(kn4, (0, 3, 1, 2)), jnp.transpose(vn4, (0, 3, 1, 2)),
            to_mem(mkt), to_mem(mvt))
```

```python
import functools
import math

import numpy as np
import jax
import jax.numpy as jnp
from jax import lax
from jax.experimental import pallas as pl
from jax.experimental.pallas import tpu as pltpu

F32 = jnp.float32
BF16 = jnp.bfloat16

D = 1024
D_FF = 4 * D
N_MEM = 256
MEM_H = 4
MEM_HD = 64
MEM_W = MEM_H * MEM_HD
GLA_H = 4
GLA_DK = 96
GLA_DV = 192
GLA_KW = GLA_H * GLA_DK
GLA_VW = GLA_H * GLA_DV
GATE_RANK = 16
GATE_NORM = 16.0
SWA_HD = 64
SWA_KVH = 4
SWA_G = 3
SWA_QW = SWA_KVH * SWA_G * SWA_HD
SWA_KVW = SWA_KVH * SWA_HD
WINDOW = 128
N_BUCKETS = 32
MAX_DISTANCE = 128
EPS = 1e-6

GLA_CHUNK = 64
GLA_SUB = 16
GLA_SAFE_DECAY = 60.0
GLA_GROUP = 4
SWA_BLOCKS_PER_TRIP = 2
TM_PROMPT = 512
FF_CHUNK = 512
GLA_DK_BLOCK = 16
V7X_VMEM_LIMIT = 56 * 1024 * 1024
NEG_INF = float("-inf")


def _bf(x):
    return x.astype(BF16)


def _dot(a, b):
    return jnp.dot(a, b, preferred_element_type=F32)


def _dot_nt(a, b):
    return lax.dot_general(a, b, (((1,), (1,)), ((), ())), preferred_element_type=F32)


def _dot_tn(a, b):
    return lax.dot_general(a, b, (((0,), (0,)), ((), ())), preferred_element_type=F32)


def _rms(x, g):
    return x * lax.rsqrt(jnp.mean(x * x, axis=-1, keepdims=True) + EPS) * g


def _split3(x):
    x1 = _bf(x)
    r1 = x - x1.astype(F32)
    x2 = _bf(r1)
    x3 = _bf(r1 - x2.astype(F32))
    return x1, x2, x3


def _exact_dot(sel, x):
    x1, x2, x3 = _split3(x)
    return _dot(sel, x1) + _dot(sel, x2) + _dot(sel, x3)


def _log_sigmoid(z):
    return jnp.minimum(z, 0.0) - jnp.log1p(jnp.exp(-jnp.abs(z)))


def _silu(z):
    return z * (1.0 / (1.0 + jnp.exp(-z)))


def _iota(shape, dim):
    return lax.broadcasted_iota(jnp.int32, shape, dim)


def _gla_k_head(lane):
    one = jnp.int32(1)
    zero = jnp.int32(0)
    return (jnp.where(lane >= GLA_DK, one, zero) + jnp.where(lane >= 2 * GLA_DK, one, zero)
            + jnp.where(lane >= 3 * GLA_DK, one, zero))


def _gla_v_head(lane):
    one = jnp.int32(1)
    zero = jnp.int32(0)
    return (jnp.where(lane >= GLA_DV, one, zero) + jnp.where(lane >= 2 * GLA_DV, one, zero)
            + jnp.where(lane >= 3 * GLA_DV, one, zero))


def _full_spec(a):
    nd = a.ndim
    return pl.BlockSpec(a.shape, lambda *_: (0,) * nd)


def _params(sem):
    return pltpu.CompilerParams(dimension_semantics=sem, vmem_limit_bytes=V7X_VMEM_LIMIT)


def _call_with_guests(host_kernel, grid, step_of, inputs, in_specs, out_shape, out_specs, scratch, guests, name):
    def to_spec(s):
        if isinstance(s, pl.BlockSpec):
            return s
        block, fn = s
        return pl.BlockSpec(block, lambda *ids: fn(step_of(*ids)))

    counts = [(len(inputs), len(out_shape), len(scratch))]
    counts += [(len(g["inputs"]), len(g["out_shape"]), len(g["scratch"])) for g in guests]

    def kern(*refs):
        refs = list(refs)
        parts = [[refs.pop(0) for _ in range(c[k])] for k in range(3) for c in counts]
        n = len(counts)
        ins, outs, scrs = parts[:n], parts[n:2 * n], parts[2 * n:]
        step = step_of(*[pl.program_id(a) for a in range(len(grid))])
        grefs = [(*a, *b, *c) for a, b, c in zip(ins[1:], outs[1:], scrs[1:])]
        states = []

        def pre_hook():
            for g, r in zip(guests, grefs):
                g["init"](step, *r)
            states.extend(g["pre"](step, *r) for g, r in zip(guests, grefs))

        def hook():
            for g, st, r in zip(guests, states, grefs):
                g["post"](step, st, *r)

        host_kernel(*ins[0], *outs[0], *scrs[0], pre_hook=pre_hook, hook=hook)

    res = pl.pallas_call(
        kern,
        grid=grid,
        in_specs=list(in_specs) + [to_spec(s) for g in guests for s in g["in_specs"]],
        out_specs=list(out_specs) + [to_spec(s) for g in guests for s in g["out_specs"]],
        out_shape=list(out_shape) + [s for g in guests for s in g["out_shape"]],
        scratch_shapes=list(scratch) + [s for g in guests for s in g["scratch"]],
        compiler_params=_params(("arbitrary",) * len(grid)),
        name=name,
    )(*inputs, *[a for g in guests for a in g["inputs"]])
    res = list(res)
    split = []
    for c in counts:
        split.append([res.pop(0) for _ in range(c[1])])
    return split[0], split[1:]


def _mem_probs(qm, kbd_ref):
    s = _dot(_bf(qm), kbd_ref[...]) * (MEM_HD ** -0.5)
    ps = []
    for h in range(MEM_H):
        sh = s[:, h * N_MEM:(h + 1) * N_MEM]
        e = jnp.exp(sh - jnp.max(sh, axis=-1, keepdims=True))
        ps.append(_bf(e / jnp.sum(e, axis=-1, keepdims=True)))
    return ps


def _mem_pv(ps, vbd_ref):
    out = None
    for h in range(MEM_H):
        t = _dot(ps[h], vbd_ref[h * N_MEM:(h + 1) * N_MEM, :])
        out = t if out is None else out + t
    return out


def _memkv_kernel(mem_ref, g_ref, w_ref, k_ref, v_ref, kbd_ref, vbd_ref):
    h = _bf(_rms(mem_ref[0], g_ref[0]))
    kv = _dot(h, w_ref[0])
    k = kv[:, :MEM_W]
    v = kv[:, MEM_W:]
    kt = k.T
    k_ref[0, 0] = kt
    v_ref[0, 0] = v.T
    kt4 = jnp.concatenate([kt, kt, kt, kt], axis=1)
    keep_k = (_iota((MEM_W, MEM_H * N_MEM), 0) >> 6) == (_iota((MEM_W, MEM_H * N_MEM), 1) >> 8)
    kbd_ref[0, 0] = _bf(jnp.where(keep_k, kt4, 0.0))
    v4 = jnp.concatenate([v, v, v, v], axis=0)
    keep_v = (_iota((MEM_H * N_MEM, MEM_W), 0) >> 8) == (_iota((MEM_H * N_MEM, MEM_W), 1) >> 6)
    vbd_ref[0, 0] = _bf(jnp.where(keep_v, v4, 0.0))


def _memkv_call(mem, norm_mem, w_mem_kv):
    nb = mem.shape[0]
    nl = w_mem_kv.shape[0]
    g = norm_mem.reshape(nl, 1, D)
    w = _bf(w_mem_kv)
    return pl.pallas_call(
        _memkv_kernel,
        grid=(nl, nb),
        in_specs=[
            pl.BlockSpec((1, N_MEM, D), lambda l, b: (b, 0, 0)),
            pl.BlockSpec((1, 1, D), lambda l, b: (l, 0, 0)),
            pl.BlockSpec((1, D, 2 * MEM_W), lambda l, b: (l, 0, 0)),
        ],
        out_specs=[
            pl.BlockSpec((1, 1, MEM_W, N_MEM), lambda l, b: (l, b, 0, 0)),
            pl.BlockSpec((1, 1, MEM_W, N_MEM), lambda l, b: (l, b, 0, 0)),
            pl.BlockSpec((1, 1, MEM_W, MEM_H * N_MEM), lambda l, b: (l, b, 0, 0)),
            pl.BlockSpec((1, 1, MEM_H * N_MEM, MEM_W), lambda l, b: (l, b, 0, 0)),
        ],
        out_shape=[
            jax.ShapeDtypeStruct((nl, nb, MEM_W, N_MEM), F32),
            jax.ShapeDtypeStruct((nl, nb, MEM_W, N_MEM), F32),
            jax.ShapeDtypeStruct((nl, nb, MEM_W, MEM_H * N_MEM), BF16),
            jax.ShapeDtypeStruct((nl, nb, MEM_H * N_MEM, MEM_W), BF16),
        ],
        compiler_params=_params(("arbitrary", "arbitrary")),
        name="mem_kv",
    )(mem, g, w)


def _proj_a(x, gpre_ref, wqk_ref, wv_ref, wr_ref, wmisc_ref, wg_ref, bg_ref, glan_ref, mem_refs=None):
    h = _bf(_rms(x, gpre_ref[...]))
    misc = _dot(h, wmisc_ref[...])
    glr = misc[:, :128]
    qm = misc[:, 128:]
    ps = None if mem_refs is None else _mem_probs(qm, mem_refs[0])
    g = _log_sigmoid(_dot(_bf(glr), wg_ref[...]) + bg_ref[...]) * (1.0 / GATE_NORM)
    gate = glan_ref[...] * _silu(_dot(h, wr_ref[...]))
    qk = _dot(h, wqk_ref[...])
    q = qk[:, :GLA_KW] * (GLA_DK ** -0.5)
    k = qk[:, GLA_KW:]
    v = _dot(h, wv_ref[...])
    return q, k, g, v, gate, (qm if mem_refs is None else _mem_pv(ps, mem_refs[1]))


def _gla_out_gate(o, gate):
    vh = _gla_v_head(_iota(o.shape, 1))
    o2 = o * o
    scale = jnp.zeros_like(o)
    for h in range(GLA_H):
        ss = jnp.sum(jnp.where(vh == h, o2, 0.0), axis=-1, keepdims=True) * (1.0 / GLA_DV)
        scale = jnp.where(vh == h, lax.rsqrt(ss + EPS), scale)
    return o * scale * gate


def _mix_residual(x, o_main, o_mem, wo1_ref, wo2_ref, gpost_ref):
    mix = _dot(_bf(o_main), wo1_ref[...]) + _dot(_bf(o_mem), wo2_ref[...])
    return x + _rms(mix, gpost_ref[...])


def _make_gla_chunk(q_s, k_s, g_s, v_s, o_s, state_s, p_s):
    C = GLA_CHUNK

    ri = _iota((C, GLA_KW), 0)
    kh = _gla_k_head(_iota((C, GLA_KW), 1))
    khcat = jnp.concatenate([kh, kh, kh], axis=1)
    vh = _gla_v_head(_iota((C, GLA_VW), 1))
    tri = _bf(jnp.where(_iota((C, C), 0) >= _iota((C, C), 1), 1.0, 0.0))
    d_rs = _iota((C, 4 * C), 0) - (_iota((C, 4 * C), 1) & (C - 1))
    band = jnp.where((d_rs >= 0) & (d_rs <= (_iota((C, 4 * C), 0) & (GLA_SUB - 1))), d_rs, -1)
    ebc = _bf(jnp.where(_gla_k_head(_iota((GLA_KW, 4 * C), 0)) == (_iota((GLA_KW, 4 * C), 1) >> 6),
                        1.0, 0.0))
    blockmask = _gla_k_head(_iota((GLA_KW, GLA_VW), 0)) == _gla_v_head(_iota((GLA_KW, GLA_VW), 1))
    eye = _iota((GLA_KW, GLA_KW), 0) == _iota((GLA_KW, GLA_KW), 1)

    def chunk(c, carry):
        r0 = pl.multiple_of(c * C, C)
        qc = q_s[pl.ds(r0, C), :]
        kc = k_s[pl.ds(r0, C), :]
        gc = g_s[pl.ds(r0, C), :]
        vc = v_s[pl.ds(r0, C), :]
        b = _exact_dot(tri, gc)

        ref1 = jnp.broadcast_to(b[31:32, :], b.shape)
        ref2 = jnp.where(ri < 32, jnp.broadcast_to(b[15:16, :], b.shape),
                         jnp.broadcast_to(b[47:48, :], b.shape))
        q1 = qc * jnp.exp(jnp.minimum(b - ref1, 0.0))
        k1 = kc * jnp.exp(jnp.minimum(ref1 - b, 0.0))
        q2 = qc * jnp.exp(jnp.minimum(b - ref2, 0.0))
        k2 = kc * jnp.exp(jnp.minimum(ref2 - b, 0.0))
        qcat = jnp.concatenate([
            jnp.where(ri >= 32, q1, 0.0),
            jnp.where((ri >= 16) & (ri < 32), q2, 0.0),
            jnp.where(ri >= 48, q2, 0.0)], axis=1)
        kcat = jnp.concatenate([
            jnp.where(ri < 32, k1, 0.0),
            jnp.where(ri < 16, k2, 0.0),
            jnp.where((ri >= 32) & (ri < 48), k2, 0.0)], axis=1)
        kst = _bf(jnp.concatenate([jnp.where(khcat == h, kcat, 0.0) for h in range(GLA_H)], axis=0))
        a_off = _dot_nt(_bf(qcat), kst)

        for dlt in range(GLA_SUB):
            if dlt == 0:
                pr = qc * kc
            else:
                kd = pltpu.roll(kc, dlt, 0)
                bd = pltpu.roll(b, dlt, 0)
                pr = qc * kd * jnp.exp(jnp.minimum(b - bd, 0.0))
            p_s[dlt * C:(dlt + 1) * C, :] = _bf(pr)
        rsum = _dot(p_s[...], ebc)
        a_diag = jnp.zeros((C, 4 * C), F32)
        for dlt in range(GLA_SUB):
            a_diag = jnp.where(band == dlt, rsum[dlt * C:(dlt + 1) * C, :], a_diag)

        vst = _bf(jnp.concatenate([jnp.where(vh == h, vc, 0.0) for h in range(GLA_H)], axis=0))
        o_intra = _dot(_bf(a_off + a_diag), vst)
        st = state_s[...]
        o_inter = _dot(_bf(qc * jnp.exp(b)), _bf(st))
        o_s[pl.ds(r0, C), :] = o_intra + o_inter

        blast = jnp.broadcast_to(b[C - 1:C, :], b.shape)
        kdec = kc * jnp.exp(blast - b)
        kv = _dot_tn(_bf(kdec), _bf(vc))
        decay = _row_to_col(jnp.exp(b[C - 1:C, :]), eye)
        state_s[...] = st * decay + jnp.where(blockmask, kv, 0.0)
        return carry

    return chunk


def _make_gla_chunk_bounded(q_s, k_s, g_s, v_s, o_s, state_s):
    C = GLA_CHUNK
    kh = _gla_k_head(_iota((C, GLA_KW), 1))
    vh = _gla_v_head(_iota((C, GLA_VW), 1))
    tri = _bf(jnp.where(_iota((C, C), 0) >= _iota((C, C), 1), 1.0, 0.0))
    causal = _iota((C, 4 * C), 0) >= (_iota((C, 4 * C), 1) & (C - 1))
    blockmask = _gla_k_head(_iota((GLA_KW, GLA_VW), 0)) == _gla_v_head(_iota((GLA_KW, GLA_VW), 1))
    eye = _iota((GLA_KW, GLA_KW), 0) == _iota((GLA_KW, GLA_KW), 1)

    def group(gi, carry):
        base = gi * (GLA_GROUP * C)
        rows = [pl.ds(pl.multiple_of(base + i * C, C), C) for i in range(GLA_GROUP)]
        bs = [_exact_dot(tri, g_s[r, :]) for r in rows]
        qes = [_bf(q_s[r, :] * jnp.exp(b)) for r, b in zip(rows, bs)]
        kes = [k_s[r, :] * jnp.exp(-b) for r, b in zip(rows, bs)]
        ksts = [_bf(jnp.concatenate([jnp.where(kh == h, ke, 0.0) for h in range(GLA_H)], axis=0)) for ke in kes]
        attn = [_bf(jnp.where(causal, _dot_nt(qe, kst), 0.0)) for qe, kst in zip(qes, ksts)]
        vcs = [v_s[r, :] for r in rows]
        vsts = [_bf(jnp.concatenate([jnp.where(vh == h, vc, 0.0) for h in range(GLA_H)], axis=0)) for vc in vcs]
        o_intra = [_dot(a, vst) for a, vst in zip(attn, vsts)]
        decays = [jnp.exp(b[C - 1:C, :]) for b in bs]
        kvs = [_dot_tn(_bf(ke * d), _bf(vc)) for vc, ke, d in zip(vcs, kes, decays)]
        dcols = [_row_to_col(d, eye) for d in decays]
        st = state_s[...]
        for i in range(GLA_GROUP):
            o_s[rows[i], :] = o_intra[i] + _dot(qes[i], _bf(st))
            st = st * dcols[i] + jnp.where(blockmask, kvs[i], 0.0)
        state_s[...] = st
        return carry

    return group


def _mixer_a_kernel(x_ref, gpre_ref, wqk_ref, wv_ref, wr_ref, wmisc_ref, wg_ref, bg_ref, glan_ref,
                    kbd_ref, vbd_ref, wo1_ref, wo2_ref, gpost_ref,
                    xo_ref, st_ref,
                    q_s, k_s, g_s, v_s, o_s, state_s, p_s, gate_s, om_s, pre_hook, hook):
    C = GLA_CHUNK
    tm = x_ref.shape[1]
    nchunk = tm // C

    @pl.when(pl.program_id(1) == 0)
    def _():
        state_s[...] = jnp.zeros_like(state_s)

    pre_hook()

    q, k, g, v, gate, o_mem = _proj_a(x_ref[0], gpre_ref, wqk_ref, wv_ref, wr_ref, wmisc_ref, wg_ref, bg_ref,
                                      glan_ref, (kbd_ref.at[0], vbd_ref.at[0]))
    q_s[...] = q
    k_s[...] = k
    g_s[...] = g
    v_s[...] = v
    gate_s[...] = gate
    om_s[...] = _bf(o_mem)
    hook()

    total = jnp.sum(g.reshape(nchunk, C, GLA_KW), axis=1)
    bounded = jnp.min(total) > -GLA_SAFE_DECAY

    @pl.when(bounded)
    def _():
        lax.fori_loop(0, nchunk // GLA_GROUP, _make_gla_chunk_bounded(q_s, k_s, g_s, v_s, o_s, state_s), 0)

    @pl.when(jnp.logical_not(bounded))
    def _():
        lax.fori_loop(0, nchunk, _make_gla_chunk(q_s, k_s, g_s, v_s, o_s, state_s, p_s), 0)

    st_ref[0] = state_s[...]
    o_main = _gla_out_gate(o_s[...], gate_s[...])
    xo_ref[0] = _mix_residual(x_ref[0], o_main, om_s[...], wo1_ref, wo2_ref, gpost_ref)


def _const_spec(a):
    nd = a.ndim
    return pl.BlockSpec(a.shape, lambda *_: (0,) * nd, pipeline_mode=pl.Buffered(1))


def _mixer_a_call(x, wa, kbd, vbd, make_guests):
    nb, seq, _ = x.shape
    tm = TM_PROMPT
    nt = seq // tm
    weights = [wa["gpre"], wa["wqk"], wa["wv"], wa["wr"], wa["wmisc"], wa["wg"], wa["bg"], wa["glan"]]
    tail = [wa["wo1"], wa["wo2"], wa["gpost"]]
    return _call_with_guests(
        _mixer_a_kernel,
        grid=(nb, nt),
        step_of=lambda b, t: b * nt + t,
        inputs=[x, *weights, kbd, vbd, *tail],
        in_specs=([pl.BlockSpec((1, tm, D), lambda b, t: (b, t, 0))]
                  + [_const_spec(w) for w in weights]
                  + [pl.BlockSpec((1, MEM_W, MEM_H * N_MEM), lambda b, t: (b, 0, 0)),
                     pl.BlockSpec((1, MEM_H * N_MEM, MEM_W), lambda b, t: (b, 0, 0))]
                  + [_const_spec(w) for w in tail]),
        out_specs=[pl.BlockSpec((1, tm, D), lambda b, t: (b, t, 0)),
                   pl.BlockSpec((1, GLA_KW, GLA_VW), lambda b, t: (b, 0, 0))],
        out_shape=[jax.ShapeDtypeStruct((nb, seq, D), F32),
                   jax.ShapeDtypeStruct((nb, GLA_KW, GLA_VW), F32)],
        scratch=[
            pltpu.VMEM((tm, GLA_KW), F32), pltpu.VMEM((tm, GLA_KW), F32), pltpu.VMEM((tm, GLA_KW), F32),
            pltpu.VMEM((tm, GLA_VW), F32), pltpu.VMEM((tm, GLA_VW), F32),
            pltpu.VMEM((GLA_KW, GLA_VW), F32),
            pltpu.VMEM((GLA_SUB * GLA_CHUNK, GLA_KW), BF16),
            pltpu.VMEM((tm, GLA_VW), F32), pltpu.VMEM((tm, MEM_W), BF16),
        ],
        guests=make_guests(nb * nt),
        name="mixer_a_prompt",
    )


def _ffn_rows(x, gpre_ref, wup_ref, wdn_ref, gpost_ref, acc_ref):
    h = _bf(_rms(x, gpre_ref[0]))
    nff = D_FF // FF_CHUNK
    for j in range(nff - 1):
        cols = slice(j * FF_CHUNK, (j + 1) * FF_CHUNK)
        u = jnp.maximum(_dot(h, _bf(wup_ref[0, :, cols])), 0.0)
        d = _dot(_bf(u * u), _bf(wdn_ref[0, cols, :]))
        if j == 0:
            acc_ref[...] = d
        else:
            acc_ref[...] += d
    cols = slice((nff - 1) * FF_CHUNK, nff * FF_CHUNK)
    u = jnp.maximum(_dot(h, _bf(wup_ref[0, :, cols])), 0.0)
    uu = _bf(u * u)
    wd = _bf(wdn_ref[0, cols, :])
    m = x.shape[0]
    bounds = (0, m) if m < 256 else (0, m // 2, m)
    parts = []
    for lo, hi in zip(bounds[:-1], bounds[1:]):
        tot = acc_ref[lo:hi, :] + _dot(uu[lo:hi], wd)
        parts.append(x[lo:hi] + _rms(tot, gpost_ref[0]))
    return parts[0] if len(parts) == 1 else jnp.concatenate(parts, axis=0)


def _ffn_kernel(xp_ref, xs_ref, gpre_ref, wup_ref, wdn_ref, gpost_ref, yp_ref, ys_ref, acc_s):
    i = pl.program_id(0)
    last = pl.num_programs(0) - 1

    @pl.when(i < last)
    def _():
        yp_ref[...] = _ffn_rows(xp_ref[...], gpre_ref, wup_ref, wdn_ref, gpost_ref, acc_s)

    @pl.when(i == last)
    def _():
        ns = xs_ref.shape[0]
        ys_ref[...] = _ffn_rows(xs_ref[...], gpre_ref, wup_ref, wdn_ref, gpost_ref, acc_s.at[0:ns, :])


def _ffn_call(xp, xs, norm_pre, w_up, w_down, norm_post, layer):
    n, ns = xp.shape[0], xs.shape[0]
    tm = TM_PROMPT
    nt = n // tm
    gains = [a.reshape(a.shape[0], 1, D) for a in (norm_pre, norm_post)]
    gspec = pl.BlockSpec((1, 1, D), lambda i: (layer, 0, 0))
    wspec = lambda w: pl.BlockSpec((1,) + w.shape[1:], lambda i: (layer, 0, 0), pipeline_mode=pl.Buffered(1))
    tile = pl.BlockSpec((tm, D), lambda i: (jnp.minimum(i, nt - 1), 0))
    return pl.pallas_call(
        _ffn_kernel,
        grid=(nt + 1,),
        in_specs=[tile, _full_spec(xs), gspec, wspec(w_up), wspec(w_down), gspec],
        out_specs=[tile, _full_spec(xs)],
        out_shape=[jax.ShapeDtypeStruct((n, D), F32), jax.ShapeDtypeStruct((ns, D), F32)],
        scratch_shapes=[pltpu.VMEM((tm, D), F32)],
        compiler_params=_params(("arbitrary",)),
        name="ffn",
    )(xp, xs, gains[0], w_up, w_down, gains[1])


def _t5_bucket(dist):
    max_exact = N_BUCKETS // 2
    n = np.maximum(dist, 0)
    nf = np.maximum(n, 1).astype(np.float32)
    large = max_exact + (np.log(nf / np.float32(max_exact)) / np.float32(math.log(MAX_DISTANCE / max_exact))
                         * np.float32(N_BUCKETS - max_exact)).astype(np.int32)
    large = np.minimum(large, N_BUCKETS - 1)
    return np.where(n < max_exact, n, large)


def _swa_bias_tables():
    qi = np.arange(WINDOW)[:, None] + WINDOW
    kj = np.arange(2 * WINDOW)[None, :]
    dist = qi - kj
    valid = (dist >= 0) & (dist < WINDOW)
    return np.where(valid, _t5_bucket(dist), -1).astype(np.int32)


def _mixer_b_kernel(rb_ref, sink_ref,
                    x_ref, bkt_ref, gkv_ref, wkv_ref, gpre_ref, wq_ref, wqm_ref,
                    kbd_ref, vbd_ref, wo1_ref, wo2_ref, gpost_ref,
                    xo_ref, kc_ref, vc_ref,
                    kbuf, vbuf, q_s, o_s, bias_s, qm_s, pre_hook, hook):
    W = WINDOW
    tm = x_ref.shape[1]
    bb = pl.program_id(0)
    t = pl.program_id(1)
    nheads = SWA_G * SWA_KVH

    @pl.when((bb == 0) & (t == 0))
    def _():
        bkt = bkt_ref[...]
        own = _iota((W, 2 * W), 1) >= W
        for i in range(nheads):
            def add_bucket(n, acc):
                return jnp.where(bkt == n, rb_ref[n, i], acc)
            tab = lax.fori_loop(0, N_BUCKETS, add_bucket, jnp.zeros((W, 2 * W), F32))
            tab = jnp.where(bkt < 0, NEG_INF, tab)
            bias_s[0, i] = tab
            bias_s[1, i] = jnp.where(own, tab, NEG_INF)

    @pl.when(t == 0)
    def _():
        kbuf[0:W, :] = jnp.zeros((W, SWA_KVW), F32)
        vbuf[0:W, :] = jnp.zeros((W, SWA_KVW), F32)

    pre_hook()
    x = x_ref[0]
    xn = x * lax.rsqrt(jnp.mean(x * x, axis=-1, keepdims=True) + EPS)
    kv = _dot(_bf(xn * gkv_ref[...]), wkv_ref[...])
    kbuf[W:W + tm, :] = kv[:, :SWA_KVW]
    vbuf[W:W + tm, :] = kv[:, SWA_KVW:]
    kc_ref[0] = kv[tm - W:, :SWA_KVW]
    vc_ref[0] = kv[tm - W:, SWA_KVW:]

    h = _bf(xn * gpre_ref[...])
    q_s[...] = _dot(h, wq_ref[...]) * (SWA_HD ** -0.5)
    qm_s[...] = _dot(h, wqm_ref[...])
    hook()

    lane_head = _iota((W, SWA_KVW), 1) >> 6
    key_head = _iota((2 * W, SWA_KVW), 1) >> 6

    def scores(j):
        r0 = pl.multiple_of(j * W, W)
        qrows = []
        for gi in range(SWA_G):
            qg = q_s[pl.ds(r0, W), gi * SWA_KVW:(gi + 1) * SWA_KVW]
            for hh in range(SWA_KVH):
                qrows.append(jnp.where(lane_head == hh, qg, 0.0))
        return _dot_nt(_bf(jnp.concatenate(qrows, axis=0)), _bf(kbuf[pl.ds(r0, 2 * W), :]))

    def probs(j, s_all, gi):
        tab = jnp.where((j == 0) & (t == 0), 1, 0)
        ps = []
        for hh in range(SWA_KVH):
            i = gi * SWA_KVH + hh
            s = s_all[i * W:(i + 1) * W, :] + bias_s[tab, i]
            sink = sink_ref[i]
            mx = jnp.maximum(jnp.max(s, axis=-1, keepdims=True), sink)
            p = jnp.exp(s - mx)
            p = p / (jnp.sum(p, axis=-1, keepdims=True) + jnp.exp(sink - mx))
            ps.append(_bf(p))
        return jnp.concatenate(ps, axis=1)

    def pair(jj, carry):
        js = [jj * SWA_BLOCKS_PER_TRIP + i for i in range(SWA_BLOCKS_PER_TRIP)]
        s_alls = [scores(j) for j in js]
        for j, s_all in zip(js, s_alls):
            r0 = pl.multiple_of(j * W, W)
            vb = vbuf[pl.ds(r0, 2 * W), :]
            vst = _bf(jnp.concatenate([jnp.where(key_head == hh, vb, 0.0) for hh in range(SWA_KVH)], axis=0))
            for gi in range(SWA_G):
                o_s[pl.ds(r0, W), gi * SWA_KVW:(gi + 1) * SWA_KVW] = _dot(probs(j, s_all, gi), vst)
        return carry

    lax.fori_loop(0, tm // (W * SWA_BLOCKS_PER_TRIP), pair, 0)

    kbuf[0:W, :] = kbuf[tm:tm + W, :]
    vbuf[0:W, :] = vbuf[tm:tm + W, :]

    o_mem = _mem_pv(_mem_probs(qm_s[...], kbd_ref.at[0]), vbd_ref.at[0])
    xo_ref[0] = _mix_residual(x_ref[0], o_s[...], o_mem, wo1_ref, wo2_ref, gpost_ref)


def _mixer_b_call(x, wb, kbd, vbd, make_guests):
    nb, seq, _ = x.shape
    tm = TM_PROMPT
    nt = seq // tm
    bkt = jnp.asarray(_swa_bias_tables())
    head = [bkt, wb["gkv"], wb["wkv"], wb["gpre"], wb["wq"], wb["wqm"]]
    tail = [wb["wo1"], wb["wo2"], wb["gpost"]]
    smem = pl.BlockSpec(memory_space=pltpu.SMEM)
    return _call_with_guests(
        _mixer_b_kernel,
        grid=(nb, nt),
        step_of=lambda b, t: b * nt + t,
        inputs=[wb["rb"], wb["sinks"], x, *head, kbd, vbd, *tail],
        in_specs=([smem, smem, pl.BlockSpec((1, tm, D), lambda b, t: (b, t, 0))]
                  + [_const_spec(w) for w in head]
                  + [pl.BlockSpec((1, MEM_W, MEM_H * N_MEM), lambda b, t: (b, 0, 0)),
                     pl.BlockSpec((1, MEM_H * N_MEM, MEM_W), lambda b, t: (b, 0, 0))]
                  + [_const_spec(w) for w in tail]),
        out_specs=[pl.BlockSpec((1, tm, D), lambda b, t: (b, t, 0)),
                   pl.BlockSpec((1, WINDOW, SWA_KVW), lambda b, t: (b, 0, 0)),
                   pl.BlockSpec((1, WINDOW, SWA_KVW), lambda b, t: (b, 0, 0))],
        out_shape=[jax.ShapeDtypeStruct((nb, seq, D), F32),
                   jax.ShapeDtypeStruct((nb, WINDOW, SWA_KVW), F32),
                   jax.ShapeDtypeStruct((nb, WINDOW, SWA_KVW), F32)],
        scratch=[
            pltpu.VMEM((tm + WINDOW, SWA_KVW), F32), pltpu.VMEM((tm + WINDOW, SWA_KVW), F32),
            pltpu.VMEM((tm, SWA_QW), F32), pltpu.VMEM((tm, SWA_QW), F32),
            pltpu.VMEM((2, SWA_G * SWA_KVH, WINDOW, 2 * WINDOW), F32),
            pltpu.VMEM((tm, MEM_W), F32),
        ],
        guests=make_guests(nb * nt),
        name="mixer_b_prompt",
    )


def _row_to_col(row, eye):
    return jnp.sum(jnp.where(eye, jnp.broadcast_to(row, eye.shape), 0.0), axis=1, keepdims=True)


def _eye(n):
    return _iota((n, n), 0) == _iota((n, n), 1)


def _pre_a_kernel(x_ref, gpre_ref, wqk_ref, wv_ref, wr_ref, wmisc_ref, wg_ref, bg_ref, glan_ref,
                  q_ref, k_ref, g_ref, v_ref, gate_ref, qm_ref):
    q, k, g, v, gate, qm = _proj_a(x_ref[...], gpre_ref, wqk_ref, wv_ref, wr_ref, wmisc_ref, wg_ref, bg_ref,
                                   glan_ref)
    q_ref[...] = q
    k_ref[...] = k
    g_ref[...] = g
    v_ref[...] = v
    gate_ref[...] = gate
    qm_ref[...] = qm


def _pre_a_call(x2d, wa):
    n = x2d.shape[0]
    ws = [wa["gpre"], wa["wqk"], wa["wv"], wa["wr"], wa["wmisc"], wa["wg"], wa["bg"], wa["glan"]]
    shapes = [(n, GLA_KW), (n, GLA_KW), (n, GLA_KW), (n, GLA_VW), (n, GLA_VW), (n, MEM_W)]
    return pl.pallas_call(
        _pre_a_kernel,
        grid=(1,),
        in_specs=[_full_spec(x2d)] + [_full_spec(w) for w in ws],
        out_specs=[pl.BlockSpec(s, lambda i: (0, 0)) for s in shapes],
        out_shape=[jax.ShapeDtypeStruct(s, F32) for s in shapes],
        compiler_params=_params(("arbitrary",)),
        name="pre_a_sample",
    )(x2d, *ws)


def _gla_step_guest(qt, kt, gt, vt, state5, nsteps):
    n = qt.shape[1]
    nblk = GLA_DK // GLA_DK_BLOCK
    nact = GLA_H * nblk
    assert nact <= nsteps
    act = lambda s: jnp.minimum(s, nact - 1)

    def init(step, qt_ref, kt_ref, gt_ref, vt_ref, s_ref, so_ref, ot_ref):
        @pl.when((step < nact) & (step % nblk == 0))
        def _():
            ot_ref[...] = jnp.zeros_like(ot_ref)

    def pre(step, qt_ref, kt_ref, gt_ref, vt_ref, s_ref, so_ref, ot_ref):
        vt_blk = vt_ref[...]
        acc = jnp.zeros_like(vt_blk)
        for d in range(GLA_DK_BLOCK):
            s_new = jnp.exp(gt_ref[d:d + 1, :]) * s_ref[0, 0, d] + kt_ref[d:d + 1, :] * vt_blk
            so_ref[0, 0, d] = s_new
            acc = acc + qt_ref[d:d + 1, :] * s_new
        ot_ref[...] += jnp.where(step < nact, acc, 0.0)

    rows = ((GLA_DK_BLOCK, n), lambda s: (act(s), 0))
    head = ((GLA_DV, n), lambda s: (act(s) // nblk, 0))
    st = ((1, 1, GLA_DK_BLOCK, GLA_DV, n), lambda s: (0, act(s) // nblk, act(s) % nblk, 0, 0))
    return dict(inputs=[qt, kt, gt, vt, state5], in_specs=[rows, rows, rows, head, st],
                out_shape=[jax.ShapeDtypeStruct(state5.shape, F32), jax.ShapeDtypeStruct((GLA_VW, n), F32)],
                out_specs=[st, head], scratch=[], init=init, pre=pre, post=lambda step, state, *refs: None)


def _mem_step_pre(step, qm_ref, mk_ref, mv_ref, o_ref):
    own = (_iota((8, MEM_W), 1) >> 6) == _iota((8, MEM_W), 0)
    ps = []
    for i in range(qm_ref.shape[0]):
        q8 = jnp.where(own, jnp.broadcast_to(qm_ref[i], (8, MEM_W)), 0.0)
        s = _dot(_bf(q8), _bf(mk_ref[0, i].reshape(MEM_W, N_MEM))) * (MEM_HD ** -0.5)
        e = jnp.exp(s - jnp.max(s, axis=1, keepdims=True))
        ps.append(_bf(e / jnp.sum(e, axis=1, keepdims=True)))
    return ps


def _mem_step_post(step, ps, qm_ref, mk_ref, mv_ref, o_ref):
    own = (_iota((8, MEM_W), 1) >> 6) == _iota((8, MEM_W), 0)
    for i, p in enumerate(ps):
        res = _dot_nt(p, _bf(mv_ref[0, i].reshape(MEM_W, N_MEM)))
        o_ref[i] = jnp.sum(jnp.where(own, res, 0.0), axis=0, keepdims=True)


def _mem_step_guest(qm, mk5, mv5, layer, nsteps):
    n = qm.shape[0]
    rb = n // nsteps
    blk = ((1, rb, MEM_H, MEM_HD, N_MEM), lambda s: (layer, s, 0, 0, 0))
    rows = ((rb, 1, MEM_W), lambda s: (s, 0, 0))
    return dict(inputs=[qm.reshape(n, 1, MEM_W), mk5, mv5], in_specs=[rows, blk, blk],
                out_shape=[jax.ShapeDtypeStruct((n, 1, MEM_W), F32)], out_specs=[rows], scratch=[],
                init=lambda step, *refs: None, pre=_mem_step_pre, post=_mem_step_post)


def _post_a_kernel(x_ref, o_ref, gate_ref, om_ref, wo1_ref, wo2_ref, gpost_ref, xo_ref):
    o_main = _gla_out_gate(o_ref[...], gate_ref[...])
    xo_ref[...] = _mix_residual(x_ref[...], o_main, om_ref[...], wo1_ref, wo2_ref, gpost_ref)


def _post_a_call(x2d, o, gate, om, wa):
    args = [x2d, o, gate, om, wa["wo1"], wa["wo2"], wa["gpost"]]
    return pl.pallas_call(
        _post_a_kernel,
        grid=(1,),
        in_specs=[_full_spec(a) for a in args],
        out_specs=_full_spec(x2d),
        out_shape=jax.ShapeDtypeStruct(x2d.shape, F32),
        compiler_params=_params(("arbitrary",)),
        name="post_a_sample",
    )(*args)


def _pre_b_kernel(x_ref, gkv_ref, wkv_ref, gpre_ref, wq_ref, wqm_ref, ks_ref, vs_ref, q_ref, qm_ref):
    x = x_ref[...]
    kv = _dot(_bf(_rms(x, gkv_ref[...])), wkv_ref[...])
    ks_ref[...] = kv[:, :SWA_KVW]
    vs_ref[...] = kv[:, SWA_KVW:]
    h = _bf(_rms(x, gpre_ref[...]))
    q_ref[...] = _dot(h, wq_ref[...])
    qm_ref[...] = _dot(h, wqm_ref[...])


def _pre_b_call(x2d, wb):
    n = x2d.shape[0]
    ws = [wb["gkv"], wb["wkv"], wb["gpre"], wb["wq"], wb["wqm"]]
    widths = [SWA_KVW, SWA_KVW, SWA_QW, MEM_W]
    return pl.pallas_call(
        _pre_b_kernel,
        grid=(1,),
        in_specs=[_full_spec(x2d)] + [_full_spec(w) for w in ws],
        out_specs=[pl.BlockSpec((n, w), lambda i: (0, 0)) for w in widths],
        out_shape=[jax.ShapeDtypeStruct((n, w), F32) for w in widths],
        compiler_params=_params(("arbitrary",)),
        name="pre_b_sample",
    )(x2d, *ws)


def _sample_buckets():
    dist = (WINDOW - 1) - np.arange(WINDOW)
    return _t5_bucket(dist).astype(np.int32).reshape(1, WINDOW)


def _swa_step_init(step, sink_ref, rb_ref, bkt_ref, kc_ref, vc_ref, kst_ref, vst_ref, q_ref,
                   kn_ref, vn_ref, o_ref, bias_s, sink_s, s_s, p_s):
    W = WINDOW
    R = SWA_KVH * 8

    @pl.when(step == 0)
    def _():
        bkt = bkt_ref[...]
        rid = _iota((R, W), 0)
        bias = jnp.zeros((R, W), F32)
        sink = jnp.zeros((R, W), F32)
        for h in range(SWA_KVH):
            for g in range(SWA_G):
                idx = g * SWA_KVH + h
                def add_bucket(n, acc):
                    return jnp.where(bkt == n, rb_ref[n, idx], acc)
                brow = lax.fori_loop(0, N_BUCKETS, add_bucket, jnp.zeros((1, W), F32))
                bias = jnp.where(rid == h * 8 + g, brow, bias)
                sink = jnp.where(rid == h * 8 + g, sink_ref[idx], sink)
        bias_s[...] = bias
        sink_s[...] = sink


def _swa_step_pre(step, sink_ref, rb_ref, bkt_ref, kc_ref, vc_ref, kst_ref, vst_ref, q_ref,
                  kn_ref, vn_ref, o_ref, bias_s, sink_s, s_s, p_s):
    W = WINDOW
    R = SWA_KVH * 8
    rb = q_ref.shape[0]
    base = step * rb
    last = _iota((SWA_KVW, W), 1) == W - 1
    own = (_iota((R, SWA_KVW), 1) >> 6) == (_iota((R, SWA_KVW), 0) >> 3)
    kst = kst_ref[...]
    vst = vst_ref[...]
    for i in range(rb):
        shift = W - 1 - (base + i)
        kn = jnp.where(last, pltpu.roll(kst, shift, 1), pltpu.roll(kc_ref[i].reshape(SWA_KVW, W), W - 1, 1))
        vn = jnp.where(last, pltpu.roll(vst, shift, 1), pltpu.roll(vc_ref[i].reshape(SWA_KVW, W), W - 1, 1))
        kn_ref[i] = kn.reshape(SWA_KVH, SWA_HD, W)
        vn_ref[i] = vn.reshape(SWA_KVH, SWA_HD, W)
        q32 = jnp.where(own, jnp.concatenate([q_ref[i]] * SWA_KVH, axis=0), 0.0)
        s_s[i * R:(i + 1) * R, :] = _dot(_bf(q32), _bf(kn))
    s = s_s[...] * (SWA_HD ** -0.5) + jnp.concatenate([bias_s[...]] * rb, axis=0)
    sink = jnp.concatenate([sink_s[...]] * rb, axis=0)
    mx = jnp.maximum(jnp.max(s, axis=1, keepdims=True), sink)
    p = jnp.exp(s - mx)
    p_s[...] = _bf(p / (jnp.sum(p, axis=1, keepdims=True) + jnp.exp(sink - mx)))


def _swa_step_post(step, _, sink_ref, rb_ref, bkt_ref, kc_ref, vc_ref, kst_ref, vst_ref, q_ref,
                   kn_ref, vn_ref, o_ref, bias_s, sink_s, s_s, p_s):
    W = WINDOW
    R = SWA_KVH * 8
    own = (_iota((R, SWA_KVW), 1) >> 6) == (_iota((R, SWA_KVW), 0) >> 3)
    for i in range(q_ref.shape[0]):
        vn = vn_ref[i].reshape(SWA_KVW, W)
        res = jnp.where(own, _dot_nt(p_s[i * R:(i + 1) * R, :], _bf(vn)), 0.0)
        o_ref[i] = res[0:8] + res[8:16] + res[16:24] + res[24:32]


def _swa_step_guest(kc4, vc4, ks, vs, q, wb, nsteps):
    n = q.shape[0]
    rb = n // nsteps
    bkt = jnp.asarray(_sample_buckets())
    q8 = jnp.pad(q.reshape(n, SWA_G, SWA_KVW), ((0, 0), (0, 8 - SWA_G), (0, 0)))
    row3 = lambda r, w: ((rb, r, w), lambda s: (s, 0, 0))
    cache = ((rb, SWA_KVH, SWA_HD, WINDOW), lambda s: (s, 0, 0, 0))
    whole = ((SWA_KVW, n), lambda s: (0, 0))
    smem = pl.BlockSpec(memory_space=pltpu.SMEM)
    return dict(inputs=[wb["sinks"], wb["rb"], bkt, kc4, vc4, ks.T, vs.T, q8],
                in_specs=[smem, smem, _full_spec(bkt), cache, cache, whole, whole, row3(8, SWA_KVW)],
                out_shape=[jax.ShapeDtypeStruct(kc4.shape, F32), jax.ShapeDtypeStruct(vc4.shape, F32),
                           jax.ShapeDtypeStruct((n, 8, SWA_KVW), F32)],
                out_specs=[cache, cache, row3(8, SWA_KVW)],
                scratch=[pltpu.VMEM((SWA_KVH * 8, WINDOW), F32), pltpu.VMEM((SWA_KVH * 8, WINDOW), F32),
                         pltpu.VMEM((rb * SWA_KVH * 8, WINDOW), F32),
                         pltpu.VMEM((rb * SWA_KVH * 8, WINDOW), BF16)],
                init=_swa_step_init, pre=_swa_step_pre, post=_swa_step_post)


def _post_b_kernel(x_ref, o_ref, om_ref, wo1_ref, wo2_ref, gpost_ref, xo_ref):
    xo_ref[...] = _mix_residual(x_ref[...], o_ref[...], om_ref[...], wo1_ref, wo2_ref, gpost_ref)


def _post_b_call(x2d, o, om, wb):
    args = [x2d, o, om, wb["wo1"], wb["wo2"], wb["gpost"]]
    return pl.pallas_call(
        _post_b_kernel,
        grid=(1,),
        in_specs=[_full_spec(a) for a in args],
        out_specs=_full_spec(x2d),
        out_shape=jax.ShapeDtypeStruct(x2d.shape, F32),
        compiler_params=_params(("arbitrary",)),
        name="post_b_sample",
    )(*args)


def _prep_weights(norm_mix_pre, norm_mix_post, w_in_a, w_gate_up, b_gate,
                  gla_norm, w_in_b, sinks, norm_kv, w_kv, rel_bias, w_out):
    row = lambda g: g.reshape(1, -1)
    wa_in = w_in_a[0]
    c_v = 2 * GLA_KW
    c_r = c_v + GLA_VW
    c_g = c_r + GLA_VW
    c_m = c_g + GATE_RANK
    wmisc = jnp.concatenate([wa_in[:, c_g:c_m], jnp.zeros((D, 128 - GATE_RANK), F32), wa_in[:, c_m:]], axis=1)
    wg = jnp.zeros((128, GLA_KW), F32).at[:GATE_RANK].set(w_gate_up[0])
    wa = dict(
        gpre=row(norm_mix_pre[0]), wqk=_bf(wa_in[:, :c_v]), wv=_bf(wa_in[:, c_v:c_r]), wr=_bf(wa_in[:, c_r:c_g]),
        wmisc=_bf(wmisc), wg=_bf(wg), bg=row(b_gate[0]), glan=row(jnp.tile(gla_norm[0], GLA_H)),
        wo1=_bf(w_out[0][:GLA_VW]), wo2=_bf(w_out[0][GLA_VW:]), gpost=row(norm_mix_post[0]))
    wb_in = w_in_b[0]
    wq = wb_in[:, :SWA_QW].reshape(D, SWA_KVH, SWA_G, SWA_HD).transpose(0, 2, 1, 3).reshape(D, SWA_QW)
    wo1 = w_out[1][:SWA_QW].reshape(SWA_KVH, SWA_G, SWA_HD, D).transpose(1, 0, 2, 3).reshape(SWA_QW, D)
    rb = rel_bias.reshape(N_BUCKETS, SWA_KVH, SWA_G).transpose(0, 2, 1).reshape(N_BUCKETS, SWA_G * SWA_KVH)
    sk = sinks[0].reshape(SWA_KVH, SWA_G).T.reshape(SWA_G * SWA_KVH)
    wb = dict(
        gkv=row(norm_kv), wkv=_bf(w_kv), gpre=row(norm_mix_pre[1]), wq=_bf(wq), wqm=_bf(wb_in[:, SWA_QW:]),
        wo1=_bf(wo1), wo2=_bf(w_out[1][SWA_QW:]), gpost=row(norm_mix_post[1]), rb=rb, sinks=sk)
    return wa, wb


def kernel(x_prompt, x_sample, state_gla, cache_swa_k, cache_swa_v, cache_mem_k, cache_mem_v, mem_prompt,
           norm_mix_pre, norm_mix_post, norm_ffn_pre, norm_ffn_post, norm_mem, w_mem_kv, w_in_a, w_gate_up,
           b_gate, gla_norm, w_in_b, sinks, norm_kv, w_kv, rel_bias, w_out, w_ffn_up, w_ffn_down):
    wa, wb = _prep_weights(norm_mix_pre, norm_mix_post, w_in_a, w_gate_up, b_gate, gla_norm, w_in_b, sinks,
                           norm_kv, w_kv, rel_bias, w_out)
    ffn = lambda xp, xs_, l: _ffn_call(xp, xs_, norm_ffn_pre, w_ffn_up, w_ffn_down, norm_ffn_post, l)
    nb, seq, _ = x_prompt.shape
    ns = x_sample.shape[0]

    xs = x_sample.reshape(ns, D)
    state5 = jnp.transpose(state_gla, (0, 2, 3, 4, 1))
    mk5 = jnp.transpose(cache_mem_k, (0, 1, 3, 4, 2))
    mv5 = jnp.transpose(cache_mem_v, (0, 1, 3, 4, 2))
    kc4 = jnp.transpose(cache_swa_k, (0, 2, 3, 1))
    vc4 = jnp.transpose(cache_swa_v, (0, 2, 3, 1))

    mkt, mvt, kbd, vbd = _memkv_call(mem_prompt, norm_mem, w_mem_kv)
    q, k, g, v, gate, qm = _pre_a_call(xs, wa)
    (x1, st), ((state5_new, ot), (om,)) = _mixer_a_call(
        x_prompt, wa, kbd[0], vbd[0],
        lambda nsteps: [_gla_step_guest(q.T, k.T, g.T, v.T, state5, nsteps),
                        _mem_step_guest(qm, mk5, mv5, 0, nsteps)])
    xs1 = _post_a_call(xs, ot.T, gate, om.reshape(ns, MEM_W), wa)
    x2, xs2 = ffn(x1.reshape(nb * seq, D), xs1, 0)
    ks, vs, qb, qmb = _pre_b_call(xs2, wb)
    (x3, kc, vc), ((kn4, vn4, o8), (omb,)) = _mixer_b_call(
        x2.reshape(nb, seq, D), wb, kbd[1], vbd[1],
        lambda nsteps: [_swa_step_guest(kc4, vc4, ks, vs, qb, wb, nsteps),
                        _mem_step_guest(qmb, mk5, mv5, 1, nsteps)])
    xs3 = _post_b_call(xs2, o8[:, :SWA_G].reshape(ns, SWA_QW), omb.reshape(ns, MEM_W), wb)
    y_prompt, y_sample = ffn(x3.reshape(nb * seq, D), xs3, 1)
    y_prompt = y_prompt.reshape(nb, seq, D)
    y_sample = y_sample.reshape(ns, 1, D)
    st4 = st.reshape(nb, GLA_H, GLA_DK, GLA_H, GLA_DV)
    state_prompt = jnp.stack([st4[:, h, :, h, :] for h in range(GLA_H)], axis=1)[None]
    to_mem = lambda t: t.reshape(2, nb, MEM_H, MEM_HD, N_MEM).transpose(0, 1, 4, 2, 3)
    swa_shape = (nb, WINDOW, SWA_KVH, SWA_HD)

    return (y_prompt, y_sample, state_prompt,
            jnp.transpose(state5_new, (0, 4, 1, 2, 3)),
            kc.reshape(swa_shape), vc.reshape(swa_shape),
            jnp.transpose(kn4, (0, 3, 1, 2)), jnp.transpose(vn4, (0, 3, 1, 2)),
            to_mem(mkt), to_mem(mvt))
```

```python
import functools
import math

import numpy as np
import jax
import jax.numpy as jnp
from jax import lax
from jax.experimental import pallas as pl
from jax.experimental.pallas import tpu as pltpu

F32 = jnp.float32
BF16 = jnp.bfloat16

D = 1024
D_FF = 4 * D
N_MEM = 256
MEM_H = 4
MEM_HD = 64
MEM_W = MEM_H * MEM_HD
GLA_H = 4
GLA_DK = 96
GLA_DV = 192
GLA_KW = GLA_H * GLA_DK
GLA_VW = GLA_H * GLA_DV
GATE_RANK = 16
GATE_NORM = 16.0
SWA_HD = 64
SWA_KVH = 4
SWA_G = 3
SWA_QW = SWA_KVH * SWA_G * SWA_HD
SWA_KVW = SWA_KVH * SWA_HD
WINDOW = 128
N_BUCKETS = 32
MAX_DISTANCE = 128
EPS = 1e-6

GLA_CHUNK = 64
GLA_SUB = 16
GLA_SAFE_DECAY = 60.0
GLA_GROUP = 4
SWA_BLOCKS_PER_TRIP = 2
TM_PROMPT = 512
FF_CHUNK = 512
GLA_DK_BLOCK = 16
V7X_VMEM_LIMIT = 56 * 1024 * 1024
NEG_INF = float("-inf")


def _bf(x):
    return x.astype(BF16)


def _dot(a, b):
    return jnp.dot(a, b, preferred_element_type=F32)


def _dot_nt(a, b):
    return lax.dot_general(a, b, (((1,), (1,)), ((), ())), preferred_element_type=F32)


def _dot_tn(a, b):
    return lax.dot_general(a, b, (((0,), (0,)), ((), ())), preferred_element_type=F32)


def _rms(x, g):
    return x * lax.rsqrt(jnp.mean(x * x, axis=-1, keepdims=True) + EPS) * g


def _split3(x):
    x1 = _bf(x)
    r1 = x - x1.astype(F32)
    x2 = _bf(r1)
    x3 = _bf(r1 - x2.astype(F32))
    return x1, x2, x3


def _exact_dot(sel, x):
    x1, x2, x3 = _split3(x)
    return _dot(sel, x1) + _dot(sel, x2) + _dot(sel, x3)


def _log_sigmoid(z):
    return jnp.minimum(z, 0.0) - jnp.log1p(jnp.exp(-jnp.abs(z)))


def _silu(z):
    return z * (1.0 / (1.0 + jnp.exp(-z)))


def _iota(shape, dim):
    return lax.broadcasted_iota(jnp.int32, shape, dim)


def _gla_k_head(lane):
    one = jnp.int32(1)
    zero = jnp.int32(0)
    return (jnp.where(lane >= GLA_DK, one, zero) + jnp.where(lane >= 2 * GLA_DK, one, zero)
            + jnp.where(lane >= 3 * GLA_DK, one, zero))


def _gla_v_head(lane):
    one = jnp.int32(1)
    zero = jnp.int32(0)
    return (jnp.where(lane >= GLA_DV, one, zero) + jnp.where(lane >= 2 * GLA_DV, one, zero)
            + jnp.where(lane >= 3 * GLA_DV, one, zero))


def _full_spec(a):
    nd = a.ndim
    return pl.BlockSpec(a.shape, lambda *_: (0,) * nd)


def _params(sem):
    return pltpu.CompilerParams(dimension_semantics=sem, vmem_limit_bytes=V7X_VMEM_LIMIT)


def _call_with_guests(host_kernel, grid, step_of, inputs, in_specs, out_shape, out_specs, scratch, guests, name):
    def to_spec(s):
        if isinstance(s, pl.BlockSpec):
            return s
        block, fn = s
        return pl.BlockSpec(block, lambda *ids: fn(step_of(*ids)))

    counts = [(len(inputs), len(out_shape), len(scratch))]
    counts += [(len(g["inputs"]), len(g["out_shape"]), len(g["scratch"])) for g in guests]

    def kern(*refs):
        refs = list(refs)
        parts = [[refs.pop(0) for _ in range(c[k])] for k in range(3) for c in counts]
        n = len(counts)
        ins, outs, scrs = parts[:n], parts[n:2 * n], parts[2 * n:]
        step = step_of(*[pl.program_id(a) for a in range(len(grid))])
        grefs = [(*a, *b, *c) for a, b, c in zip(ins[1:], outs[1:], scrs[1:])]
        states = []

        def pre_hook():
            for g, r in zip(guests, grefs):
                g["init"](step, *r)
            states.extend(g["pre"](step, *r) for g, r in zip(guests, grefs))

        def hook():
            for g, st, r in zip(guests, states, grefs):
                g["post"](step, st, *r)

        host_kernel(*ins[0], *outs[0], *scrs[0], pre_hook=pre_hook, hook=hook)

    res = pl.pallas_call(
        kern,
        grid=grid,
        in_specs=list(in_specs) + [to_spec(s) for g in guests for s in g["in_specs"]],
        out_specs=list(out_specs) + [to_spec(s) for g in guests for s in g["out_specs"]],
        out_shape=list(out_shape) + [s for g in guests for s in g["out_shape"]],
        scratch_shapes=list(scratch) + [s for g in guests for s in g["scratch"]],
        compiler_params=_params(("arbitrary",) * len(grid)),
        name=name,
    )(*inputs, *[a for g in guests for a in g["inputs"]])
    res = list(res)
    split = []
    for c in counts:
        split.append([res.pop(0) for _ in range(c[1])])
    return split[0], split[1:]


def _mem_probs(qm, kbd_ref):
    s = _dot(_bf(qm), kbd_ref[...]) * (MEM_HD ** -0.5)
    ps = []
    for h in range(MEM_H):
        sh = s[:, h * N_MEM:(h + 1) * N_MEM]
        e = jnp.exp(sh - jnp.max(sh, axis=-1, keepdims=True))
        ps.append(_bf(e / jnp.sum(e, axis=-1, keepdims=True)))
    return ps


def _mem_pv(ps, vbd_ref):
    out = None
    for h in range(MEM_H):
        t = _dot(ps[h], vbd_ref[h * N_MEM:(h + 1) * N_MEM, :])
        out = t if out is None else out + t
    return out


def _memkv_kernel(mem_ref, g_ref, w_ref, k_ref, v_ref, kbd_ref, vbd_ref):
    h = _bf(_rms(mem_ref[0], g_ref[0]))
    kv = _dot(h, w_ref[0])
    k = kv[:, :MEM_W]
    v = kv[:, MEM_W:]
    kt = k.T
    k_ref[0, 0] = kt
    v_ref[0, 0] = v.T
    kt4 = jnp.concatenate([kt, kt, kt, kt], axis=1)
    keep_k = (_iota((MEM_W, MEM_H * N_MEM), 0) >> 6) == (_iota((MEM_W, MEM_H * N_MEM), 1) >> 8)
    kbd_ref[0, 0] = _bf(jnp.where(keep_k, kt4, 0.0))
    v4 = jnp.concatenate([v, v, v, v], axis=0)
    keep_v = (_iota((MEM_H * N_MEM, MEM_W), 0) >> 8) == (_iota((MEM_H * N_MEM, MEM_W), 1) >> 6)
    vbd_ref[0, 0] = _bf(jnp.where(keep_v, v4, 0.0))


def _memkv_call(mem, norm_mem, w_mem_kv):
    nb = mem.shape[0]
    nl = w_mem_kv.shape[0]
    g = norm_mem.reshape(nl, 1, D)
    w = _bf(w_mem_kv)
    return pl.pallas_call(
        _memkv_kernel,
        grid=(nl, nb),
        in_specs=[
            pl.BlockSpec((1, N_MEM, D), lambda l, b: (b, 0, 0)),
            pl.BlockSpec((1, 1, D), lambda l, b: (l, 0, 0)),
            pl.BlockSpec((1, D, 2 * MEM_W), lambda l, b: (l, 0, 0)),
        ],
        out_specs=[
            pl.BlockSpec((1, 1, MEM_W, N_MEM), lambda l, b: (l, b, 0, 0)),
            pl.BlockSpec((1, 1, MEM_W, N_MEM), lambda l, b: (l, b, 0, 0)),
            pl.BlockSpec((1, 1, MEM_W, MEM_H * N_MEM), lambda l, b: (l, b, 0, 0)),
            pl.BlockSpec((1, 1, MEM_H * N_MEM, MEM_W), lambda l, b: (l, b, 0, 0)),
        ],
        out_shape=[
            jax.ShapeDtypeStruct((nl, nb, MEM_W, N_MEM), F32),
            jax.ShapeDtypeStruct((nl, nb, MEM_W, N_MEM), F32),
            jax.ShapeDtypeStruct((nl, nb, MEM_W, MEM_H * N_MEM), BF16),
            jax.ShapeDtypeStruct((nl, nb, MEM_H * N_MEM, MEM_W), BF16),
        ],
        compiler_params=_params(("arbitrary", "arbitrary")),
        name="mem_kv",
    )(mem, g, w)


def _proj_a(x, gpre_ref, wqk_ref, wv_ref, wr_ref, wmisc_ref, wg_ref, bg_ref, glan_ref, mem_refs=None):
    h = _bf(_rms(x, gpre_ref[...]))
    misc = _dot(h, wmisc_ref[...])
    glr = misc[:, :128]
    qm = misc[:, 128:]
    ps = None if mem_refs is None else _mem_probs(qm, mem_refs[0])
    g = _log_sigmoid(_dot(_bf(glr), wg_ref[...]) + bg_ref[...]) * (1.0 / GATE_NORM)
    gate = glan_ref[...] * _silu(_dot(h, wr_ref[...]))
    qk = _dot(h, wqk_ref[...])
    q = qk[:, :GLA_KW] * (GLA_DK ** -0.5)
    k = qk[:, GLA_KW:]
    v = _dot(h, wv_ref[...])
    return q, k, g, v, gate, (qm if mem_refs is None else _mem_pv(ps, mem_refs[1]))


def _gla_out_gate(o, gate):
    vh = _gla_v_head(_iota(o.shape, 1))
    o2 = o * o
    scale = jnp.zeros_like(o)
    for h in range(GLA_H):
        ss = jnp.sum(jnp.where(vh == h, o2, 0.0), axis=-1, keepdims=True) * (1.0 / GLA_DV)
        scale = jnp.where(vh == h, lax.rsqrt(ss + EPS), scale)
    return o * scale * gate


def _mix_residual(x, o_main, o_mem, wo1_ref, wo2_ref, gpost_ref):
    mix = _dot(_bf(o_main), wo1_ref[...]) + _dot(_bf(o_mem), wo2_ref[...])
    return x + _rms(mix, gpost_ref[...])


def _make_gla_chunk(q_s, k_s, g_s, v_s, o_s, state_s, p_s):
    C = GLA_CHUNK

    ri = _iota((C, GLA_KW), 0)
    kh = _gla_k_head(_iota((C, GLA_KW), 1))
    khcat = jnp.concatenate([kh, kh, kh], axis=1)
    vh = _gla_v_head(_iota((C, GLA_VW), 1))
    tri = _bf(jnp.where(_iota((C, C), 0) >= _iota((C, C), 1), 1.0, 0.0))
    d_rs = _iota((C, 4 * C), 0) - (_iota((C, 4 * C), 1) & (C - 1))
    band = jnp.where((d_rs >= 0) & (d_rs <= (_iota((C, 4 * C), 0) & (GLA_SUB - 1))), d_rs, -1)
    ebc = _bf(jnp.where(_gla_k_head(_iota((GLA_KW, 4 * C), 0)) == (_iota((GLA_KW, 4 * C), 1) >> 6),
                        1.0, 0.0))
    blockmask = _gla_k_head(_iota((GLA_KW, GLA_VW), 0)) == _gla_v_head(_iota((GLA_KW, GLA_VW), 1))
    eye = _iota((GLA_KW, GLA_KW), 0) == _iota((GLA_KW, GLA_KW), 1)

    def chunk(c, carry):
        r0 = pl.multiple_of(c * C, C)
        qc = q_s[pl.ds(r0, C), :]
        kc = k_s[pl.ds(r0, C), :]
        gc = g_s[pl.ds(r0, C), :]
        vc = v_s[pl.ds(r0, C), :]
        b = _exact_dot(tri, gc)

        ref1 = jnp.broadcast_to(b[31:32, :], b.shape)
        ref2 = jnp.where(ri < 32, jnp.broadcast_to(b[15:16, :], b.shape),
                         jnp.broadcast_to(b[47:48, :], b.shape))
        q1 = qc * jnp.exp(jnp.minimum(b - ref1, 0.0))
        k1 = kc * jnp.exp(jnp.minimum(ref1 - b, 0.0))
        q2 = qc * jnp.exp(jnp.minimum(b - ref2, 0.0))
        k2 = kc * jnp.exp(jnp.minimum(ref2 - b, 0.0))
        qcat = jnp.concatenate([
            jnp.where(ri >= 32, q1, 0.0),
            jnp.where((ri >= 16) & (ri < 32), q2, 0.0),
            jnp.where(ri >= 48, q2, 0.0)], axis=1)
        kcat = jnp.concatenate([
            jnp.where(ri < 32, k1, 0.0),
            jnp.where(ri < 16, k2, 0.0),
            jnp.where((ri >= 32) & (ri < 48), k2, 0.0)], axis=1)
        kst = _bf(jnp.concatenate([jnp.where(khcat == h, kcat, 0.0) for h in range(GLA_H)], axis=0))
        a_off = _dot_nt(_bf(qcat), kst)

        for dlt in range(GLA_SUB):
            if dlt == 0:
                pr = qc * kc
            else:
                kd = pltpu.roll(kc, dlt, 0)
                bd = pltpu.roll(b, dlt, 0)
                pr = qc * kd * jnp.exp(jnp.minimum(b - bd, 0.0))
            p_s[dlt * C:(dlt + 1) * C, :] = _bf(pr)
        rsum = _dot(p_s[...], ebc)
        a_diag = jnp.zeros((C, 4 * C), F32)
        for dlt in range(GLA_SUB):
            a_diag = jnp.where(band == dlt, rsum[dlt * C:(dlt + 1) * C, :], a_diag)

        vst = _bf(jnp.concatenate([jnp.where(vh == h, vc, 0.0) for h in range(GLA_H)], axis=0))
        o_intra = _dot(_bf(a_off + a_diag), vst)
        st = state_s[...]
        o_inter = _dot(_bf(qc * jnp.exp(b)), _bf(st))
        o_s[pl.ds(r0, C), :] = o_intra + o_inter

        blast = jnp.broadcast_to(b[C - 1:C, :], b.shape)
        kdec = kc * jnp.exp(blast - b)
        kv = _dot_tn(_bf(kdec), _bf(vc))
        decay = _row_to_col(jnp.exp(b[C - 1:C, :]), eye)
        state_s[...] = st * decay + jnp.where(blockmask, kv, 0.0)
        return carry

    return chunk


def _make_gla_chunk_bounded(q_s, k_s, g_s, v_s, o_s, state_s):
    C = GLA_CHUNK
    kh = _gla_k_head(_iota((C, GLA_KW), 1))
    vh = _gla_v_head(_iota((C, GLA_VW), 1))
    tri = _bf(jnp.where(_iota((C, C), 0) >= _iota((C, C), 1), 1.0, 0.0))
    causal = _iota((C, 4 * C), 0) >= (_iota((C, 4 * C), 1) & (C - 1))
    blockmask = _gla_k_head(_iota((GLA_KW, GLA_VW), 0)) == _gla_v_head(_iota((GLA_KW, GLA_VW), 1))
    eye = _iota((GLA_KW, GLA_KW), 0) == _iota((GLA_KW, GLA_KW), 1)

    def group(gi, carry):
        base = gi * (GLA_GROUP * C)
        rows = [pl.ds(pl.multiple_of(base + i * C, C), C) for i in range(GLA_GROUP)]
        bs = [_exact_dot(tri, g_s[r, :]) for r in rows]
        qes = [_bf(q_s[r, :] * jnp.exp(b)) for r, b in zip(rows, bs)]
        kes = [k_s[r, :] * jnp.exp(-b) for r, b in zip(rows, bs)]
        ksts = [_bf(jnp.concatenate([jnp.where(kh == h, ke, 0.0) for h in range(GLA_H)], axis=0)) for ke in kes]
        attn = [_bf(jnp.where(causal, _dot_nt(qe, kst), 0.0)) for qe, kst in zip(qes, ksts)]
        vcs = [v_s[r, :] for r in rows]
        vsts = [_bf(jnp.concatenate([jnp.where(vh == h, vc, 0.0) for h in range(GLA_H)], axis=0)) for vc in vcs]
        o_intra = [_dot(a, vst) for a, vst in zip(attn, vsts)]
        decays = [jnp.exp(b[C - 1:C, :]) for b in bs]
        kvs = [_dot_tn(_bf(ke * d), _bf(vc)) for vc, ke, d in zip(vcs, kes, decays)]
        dcols = [_row_to_col(d, eye) for d in decays]
        st = state_s[...]
        for i in range(GLA_GROUP):
            o_s[rows[i], :] = o_intra[i] + _dot(qes[i], _bf(st))
            st = st * dcols[i] + jnp.where(blockmask, kvs[i], 0.0)
        state_s[...] = st
        return carry

    return group


def _mixer_a_kernel(x_ref, gpre_ref, wqk_ref, wv_ref, wr_ref, wmisc_ref, wg_ref, bg_ref, glan_ref,
                    kbd_ref, vbd_ref, wo1_ref, wo2_ref, gpost_ref,
                    xo_ref, st_ref,
                    q_s, k_s, g_s, v_s, o_s, state_s, p_s, gate_s, om_s, pre_hook, hook):
    C = GLA_CHUNK
    tm = x_ref.shape[1]
    nchunk = tm // C

    @pl.when(pl.program_id(1) == 0)
    def _():
        state_s[...] = jnp.zeros_like(state_s)

    pre_hook()

    q, k, g, v, gate, o_mem = _proj_a(x_ref[0], gpre_ref, wqk_ref, wv_ref, wr_ref, wmisc_ref, wg_ref, bg_ref,
                                      glan_ref, (kbd_ref.at[0], vbd_ref.at[0]))
    q_s[...] = q
    k_s[...] = k
    g_s[...] = g
    v_s[...] = v
    gate_s[...] = gate
    om_s[...] = _bf(o_mem)
    hook()

    total = jnp.sum(g.reshape(nchunk, C, GLA_KW), axis=1)
    bounded = jnp.min(total) > -GLA_SAFE_DECAY

    @pl.when(bounded)
    def _():
        lax.fori_loop(0, nchunk // GLA_GROUP, _make_gla_chunk_bounded(q_s, k_s, g_s, v_s, o_s, state_s), 0)

    @pl.when(jnp.logical_not(bounded))
    def _():
        lax.fori_loop(0, nchunk, _make_gla_chunk(q_s, k_s, g_s, v_s, o_s, state_s, p_s), 0)

    st_ref[0] = state_s[...]
    o_main = _gla_out_gate(o_s[...], gate_s[...])
    xo_ref[0] = _mix_residual(x_ref[0], o_main, om_s[...], wo1_ref, wo2_ref, gpost_ref)


def _const_spec(a):
    nd = a.ndim
    return pl.BlockSpec(a.shape, lambda *_: (0,) * nd, pipeline_mode=pl.Buffered(1))


def _mixer_a_call(x, wa, kbd, vbd, make_guests):
    nb, seq, _ = x.shape
    tm = TM_PROMPT
    nt = seq // tm
    weights = [wa["gpre"], wa["wqk"], wa["wv"], wa["wr"], wa["wmisc"], wa["wg"], wa["bg"], wa["glan"]]
    tail = [wa["wo1"], wa["wo2"], wa["gpost"]]
    return _call_with_guests(
        _mixer_a_kernel,
        grid=(nb, nt),
        step_of=lambda b, t: b * nt + t,
        inputs=[x, *weights, kbd, vbd, *tail],
        in_specs=([pl.BlockSpec((1, tm, D), lambda b, t: (b, t, 0))]
                  + [_const_spec(w) for w in weights]
                  + [pl.BlockSpec((1, MEM_W, MEM_H * N_MEM), lambda b, t: (b, 0, 0)),
                     pl.BlockSpec((1, MEM_H * N_MEM, MEM_W), lambda b, t: (b, 0, 0))]
                  + [_const_spec(w) for w in tail]),
        out_specs=[pl.BlockSpec((1, tm, D), lambda b, t: (b, t, 0)),
                   pl.BlockSpec((1, GLA_KW, GLA_VW), lambda b, t: (b, 0, 0))],
        out_shape=[jax.ShapeDtypeStruct((nb, seq, D), F32),
                   jax.ShapeDtypeStruct((nb, GLA_KW, GLA_VW), F32)],
        scratch=[
            pltpu.VMEM((tm, GLA_KW), F32), pltpu.VMEM((tm, GLA_KW), F32), pltpu.VMEM((tm, GLA_KW), F32),
            pltpu.VMEM((tm, GLA_VW), F32), pltpu.VMEM((tm, GLA_VW), F32),
            pltpu.VMEM((GLA_KW, GLA_VW), F32),
            pltpu.VMEM((GLA_SUB * GLA_CHUNK, GLA_KW), BF16),
            pltpu.VMEM((tm, GLA_VW), F32), pltpu.VMEM((tm, MEM_W), BF16),
        ],
        guests=make_guests(nb * nt),
        name="mixer_a_prompt",
    )


def _ffn_rows(x, gpre_ref, wup_ref, wdn_ref, gpost_ref, acc_ref):
    h = _bf(_rms(x, gpre_ref[0]))
    nff = D_FF // FF_CHUNK
    for j in range(nff - 1):
        cols = slice(j * FF_CHUNK, (j + 1) * FF_CHUNK)
        u = jnp.maximum(_dot(h, _bf(wup_ref[0, :, cols])), 0.0)
        d = _dot(_bf(u * u), _bf(wdn_ref[0, cols, :]))
        if j == 0:
            acc_ref[...] = d
        else:
            acc_ref[...] += d
    cols = slice((nff - 1) * FF_CHUNK, nff * FF_CHUNK)
    u = jnp.maximum(_dot(h, _bf(wup_ref[0, :, cols])), 0.0)
    uu = _bf(u * u)
    wd = _bf(wdn_ref[0, cols, :])
    m = x.shape[0]
    bounds = (0, m) if m < 256 else (0, m // 2, m)
    parts = []
    for lo, hi in zip(bounds[:-1], bounds[1:]):
        tot = acc_ref[lo:hi, :] + _dot(uu[lo:hi], wd)
        parts.append(x[lo:hi] + _rms(tot, gpost_ref[0]))
    return parts[0] if len(parts) == 1 else jnp.concatenate(parts, axis=0)


def _ffn_kernel(xp_ref, xs_ref, gpre_ref, wup_ref, wdn_ref, gpost_ref, yp_ref, ys_ref, acc_s):
    i = pl.program_id(0)
    last = pl.num_programs(0) - 1

    @pl.when(i < last)
    def _():
        yp_ref[...] = _ffn_rows(xp_ref[...], gpre_ref, wup_ref, wdn_ref, gpost_ref, acc_s)

    @pl.when(i == last)
    def _():
        ns = xs_ref.shape[0]
        ys_ref[...] = _ffn_rows(xs_ref[...], gpre_ref, wup_ref, wdn_ref, gpost_ref, acc_s.at[0:ns, :])


def _ffn_call(xp, xs, norm_pre, w_up, w_down, norm_post, layer):
    n, ns = xp.shape[0], xs.shape[0]
    tm = TM_PROMPT
    nt = n // tm
    gains = [a.reshape(a.shape[0], 1, D) for a in (norm_pre, norm_post)]
    gspec = pl.BlockSpec((1, 1, D), lambda i: (layer, 0, 0))
    wspec = lambda w: pl.BlockSpec((1,) + w.shape[1:], lambda i: (layer, 0, 0), pipeline_mode=pl.Buffered(1))
    tile = pl.BlockSpec((tm, D), lambda i: (jnp.minimum(i, nt - 1), 0))
    return pl.pallas_call(
        _ffn_kernel,
        grid=(nt + 1,),
        in_specs=[tile, _full_spec(xs), gspec, wspec(w_up), wspec(w_down), gspec],
        out_specs=[tile, _full_spec(xs)],
        out_shape=[jax.ShapeDtypeStruct((n, D), F32), jax.ShapeDtypeStruct((ns, D), F32)],
        scratch_shapes=[pltpu.VMEM((tm, D), F32)],
        compiler_params=_params(("arbitrary",)),
        name="ffn",
    )(xp, xs, gains[0], w_up, w_down, gains[1])


def _t5_bucket(dist):
    max_exact = N_BUCKETS // 2
    n = np.maximum(dist, 0)
    nf = np.maximum(n, 1).astype(np.float32)
    large = max_exact + (np.log(nf / np.float32(max_exact)) / np.float32(math.log(MAX_DISTANCE / max_exact))
                         * np.float32(N_BUCKETS - max_exact)).astype(np.int32)
    large = np.minimum(large, N_BUCKETS - 1)
    return np.where(n < max_exact, n, large)


def _swa_bias_tables():
    qi = np.arange(WINDOW)[:, None] + WINDOW
    kj = np.arange(2 * WINDOW)[None, :]
    dist = qi - kj
    valid = (dist >= 0) & (dist < WINDOW)
    return np.where(valid, _t5_bucket(dist), -1).astype(np.int32)


def _mixer_b_kernel(rb_ref, sink_ref,
                    x_ref, bkt_ref, gkv_ref, wkv_ref, gpre_ref, wq_ref, wqm_ref,
                    kbd_ref, vbd_ref, wo1_ref, wo2_ref, gpost_ref,
                    xo_ref, kc_ref, vc_ref,
                    kbuf, vbuf, q_s, o_s, bias_s, qm_s, pre_hook, hook):
    W = WINDOW
    tm = x_ref.shape[1]
    bb = pl.program_id(0)
    t = pl.program_id(1)
    nheads = SWA_G * SWA_KVH

    @pl.when((bb == 0) & (t == 0))
    def _():
        bkt = bkt_ref[...]
        own = _iota((W, 2 * W), 1) >= W
        for i in range(nheads):
            def add_bucket(n, acc):
                return jnp.where(bkt == n, rb_ref[n, i], acc)
            tab = lax.fori_loop(0, N_BUCKETS, add_bucket, jnp.zeros((W, 2 * W), F32))
            tab = jnp.where(bkt < 0, NEG_INF, tab)
            bias_s[0, i] = tab
            bias_s[1, i] = jnp.where(own, tab, NEG_INF)

    @pl.when(t == 0)
    def _():
        kbuf[0:W, :] = jnp.zeros((W, SWA_KVW), F32)
        vbuf[0:W, :] = jnp.zeros((W, SWA_KVW), F32)

    pre_hook()
    x = x_ref[0]
    xn = x * lax.rsqrt(jnp.mean(x * x, axis=-1, keepdims=True) + EPS)
    kv = _dot(_bf(xn * gkv_ref[...]), wkv_ref[...])
    kbuf[W:W + tm, :] = kv[:, :SWA_KVW]
    vbuf[W:W + tm, :] = kv[:, SWA_KVW:]
    kc_ref[0] = kv[tm - W:, :SWA_KVW]
    vc_ref[0] = kv[tm - W:, SWA_KVW:]

    h = _bf(xn * gpre_ref[...])
    q_s[...] = _dot(h, wq_ref[...]) * (SWA_HD ** -0.5)
    qm_s[...] = _dot(h, wqm_ref[...])
    hook()

    lane_head = _iota((W, SWA_KVW), 1) >> 6
    key_head = _iota((2 * W, SWA_KVW), 1) >> 6

    def scores(j):
        r0 = pl.multiple_of(j * W, W)
        qrows = []
        for gi in range(SWA_G):
            qg = q_s[pl.ds(r0, W), gi * SWA_KVW:(gi + 1) * SWA_KVW]
            for hh in range(SWA_KVH):
                qrows.append(jnp.where(lane_head == hh, qg, 0.0))
        return _dot_nt(_bf(jnp.concatenate(qrows, axis=0)), _bf(kbuf[pl.ds(r0, 2 * W), :]))

    def probs(j, s_all, gi):
        tab = jnp.where((j == 0) & (t == 0), 1, 0)
        ps = []
        for hh in range(SWA_KVH):
            i = gi * SWA_KVH + hh
            s = s_all[i * W:(i + 1) * W, :] + bias_s[tab, i]
            sink = sink_ref[i]
            mx = jnp.maximum(jnp.max(s, axis=-1, keepdims=True), sink)
            p = jnp.exp(s - mx)
            p = p / (jnp.sum(p, axis=-1, keepdims=True) + jnp.exp(sink - mx))
            ps.append(_bf(p))
        return jnp.concatenate(ps, axis=1)

    def pair(jj, carry):
        js = [jj * SWA_BLOCKS_PER_TRIP + i for i in range(SWA_BLOCKS_PER_TRIP)]
        s_alls = [scores(j) for j in js]
        for j, s_all in zip(js, s_alls):
            r0 = pl.multiple_of(j * W, W)
            vb = vbuf[pl.ds(r0, 2 * W), :]
            vst = _bf(jnp.concatenate([jnp.where(key_head == hh, vb, 0.0) for hh in range(SWA_KVH)], axis=0))
            for gi in range(SWA_G):
                o_s[pl.ds(r0, W), gi * SWA_KVW:(gi + 1) * SWA_KVW] = _dot(probs(j, s_all, gi), vst)
        return carry

    lax.fori_loop(0, tm // (W * SWA_BLOCKS_PER_TRIP), pair, 0)

    kbuf[0:W, :] = kbuf[tm:tm + W, :]
    vbuf[0:W, :] = vbuf[tm:tm + W, :]

    o_mem = _mem_pv(_mem_probs(qm_s[...], kbd_ref.at[0]), vbd_ref.at[0])
    xo_ref[0] = _mix_residual(x_ref[0], o_s[...], o_mem, wo1_ref, wo2_ref, gpost_ref)


def _mixer_b_call(x, wb, kbd, vbd, make_guests):
    nb, seq, _ = x.shape
    tm = TM_PROMPT
    nt = seq // tm
    bkt = jnp.asarray(_swa_bias_tables())
    head = [bkt, wb["gkv"], wb["wkv"], wb["gpre"], wb["wq"], wb["wqm"]]
    tail = [wb["wo1"], wb["wo2"], wb["gpost"]]
    smem = pl.BlockSpec(memory_space=pltpu.SMEM)
    return _call_with_guests(
        _mixer_b_kernel,
        grid=(nb, nt),
        step_of=lambda b, t: b * nt + t,
        inputs=[wb["rb"], wb["sinks"], x, *head, kbd, vbd, *tail],
        in_specs=([smem, smem, pl.BlockSpec((1, tm, D), lambda b, t: (b, t, 0))]
                  + [_const_spec(w) for w in head]
                  + [pl.BlockSpec((1, MEM_W, MEM_H * N_MEM), lambda b, t: (b, 0, 0)),
                     pl.BlockSpec((1, MEM_H * N_MEM, MEM_W), lambda b, t: (b, 0, 0))]
                  + [_const_spec(w) for w in tail]),
        out_specs=[pl.BlockSpec((1, tm, D), lambda b, t: (b, t, 0)),
                   pl.BlockSpec((1, WINDOW, SWA_KVW), lambda b, t: (b, 0, 0)),
                   pl.BlockSpec((1, WINDOW, SWA_KVW), lambda b, t: (b, 0, 0))],
        out_shape=[jax.ShapeDtypeStruct((nb, seq, D), F32),
                   jax.ShapeDtypeStruct((nb, WINDOW, SWA_KVW), F32),
                   jax.ShapeDtypeStruct((nb, WINDOW, SWA_KVW), F32)],
        scratch=[
            pltpu.VMEM((tm + WINDOW, SWA_KVW), F32), pltpu.VMEM((tm + WINDOW, SWA_KVW), F32),
            pltpu.VMEM((tm, SWA_QW), F32), pltpu.VMEM((tm, SWA_QW), F32),
            pltpu.VMEM((2, SWA_G * SWA_KVH, WINDOW, 2 * WINDOW), F32),
            pltpu.VMEM((tm, MEM_W), F32),
        ],
        guests=make_guests(nb * nt),
        name="mixer_b_prompt",
    )


def _row_to_col(row, eye):
    return jnp.sum(jnp.where(eye, jnp.broadcast_to(row, eye.shape), 0.0), axis=1, keepdims=True)


def _eye(n):
    return _iota((n, n), 0) == _iota((n, n), 1)


def _pre_a_kernel(x_ref, gpre_ref, wqk_ref, wv_ref, wr_ref, wmisc_ref, wg_ref, bg_ref, glan_ref,
                  q_ref, k_ref, g_ref, v_ref, gate_ref, qm_ref):
    q, k, g, v, gate, qm = _proj_a(x_ref[...], gpre_ref, wqk_ref, wv_ref, wr_ref, wmisc_ref, wg_ref, bg_ref,
                                   glan_ref)
    q_ref[...] = q
    k_ref[...] = k
    g_ref[...] = g
    v_ref[...] = v
    gate_ref[...] = gate
    qm_ref[...] = qm


def _pre_a_call(x2d, wa):
    n = x2d.shape[0]
    ws = [wa["gpre"], wa["wqk"], wa["wv"], wa["wr"], wa["wmisc"], wa["wg"], wa["bg"], wa["glan"]]
    shapes = [(n, GLA_KW), (n, GLA_KW), (n, GLA_KW), (n, GLA_VW), (n, GLA_VW), (n, MEM_W)]
    return pl.pallas_call(
        _pre_a_kernel,
        grid=(1,),
        in_specs=[_full_spec(x2d)] + [_full_spec(w) for w in ws],
        out_specs=[pl.BlockSpec(s, lambda i: (0, 0)) for s in shapes],
        out_shape=[jax.ShapeDtypeStruct(s, F32) for s in shapes],
        compiler_params=_params(("arbitrary",)),
        name="pre_a_sample",
    )(x2d, *ws)


def _gla_step_guest(qt, kt, gt, vt, state5, nsteps):
    n = qt.shape[1]
    nblk = GLA_DK // GLA_DK_BLOCK
    nact = GLA_H * nblk
    assert nact <= nsteps
    act = lambda s: jnp.minimum(s, nact - 1)

    def init(step, qt_ref, kt_ref, gt_ref, vt_ref, s_ref, so_ref, ot_ref):
        @pl.when((step < nact) & (step % nblk == 0))
        def _():
            ot_ref[...] = jnp.zeros_like(ot_ref)

    def post(step, _, qt_ref, kt_ref, gt_ref, vt_ref, s_ref, so_ref, ot_ref):
        vt_blk = vt_ref[...]
        acc = jnp.zeros_like(vt_blk)
        for d in range(GLA_DK_BLOCK):
            s_new = jnp.exp(gt_ref[d:d + 1, :]) * s_ref[0, 0, d] + kt_ref[d:d + 1, :] * vt_blk
            so_ref[0, 0, d] = s_new
            acc = acc + qt_ref[d:d + 1, :] * s_new
        ot_ref[...] += jnp.where(step < nact, acc, 0.0)

    rows = ((GLA_DK_BLOCK, n), lambda s: (act(s), 0))
    head = ((GLA_DV, n), lambda s: (act(s) // nblk, 0))
    st = ((1, 1, GLA_DK_BLOCK, GLA_DV, n), lambda s: (0, act(s) // nblk, act(s) % nblk, 0, 0))
    return dict(inputs=[qt, kt, gt, vt, state5], in_specs=[rows, rows, rows, head, st],
                out_shape=[jax.ShapeDtypeStruct(state5.shape, F32), jax.ShapeDtypeStruct((GLA_VW, n), F32)],
                out_specs=[st, head], scratch=[], init=init, pre=lambda step, *refs: None, post=post)


def _mem_step_pre(step, qm_ref, mk_ref, mv_ref, o_ref):
    own = (_iota((8, MEM_W), 1) >> 6) == _iota((8, MEM_W), 0)
    ps = []
    for i in range(qm_ref.shape[0]):
        q8 = jnp.where(own, jnp.broadcast_to(qm_ref[i], (8, MEM_W)), 0.0)
        s = _dot(_bf(q8), _bf(mk_ref[0, i].reshape(MEM_W, N_MEM))) * (MEM_HD ** -0.5)
        e = jnp.exp(s - jnp.max(s, axis=1, keepdims=True))
        ps.append(_bf(e / jnp.sum(e, axis=1, keepdims=True)))
    return ps


def _mem_step_post(step, ps, qm_ref, mk_ref, mv_ref, o_ref):
    own = (_iota((8, MEM_W), 1) >> 6) == _iota((8, MEM_W), 0)
    for i, p in enumerate(ps):
        res = _dot_nt(p, _bf(mv_ref[0, i].reshape(MEM_W, N_MEM)))
        o_ref[i] = jnp.sum(jnp.where(own, res, 0.0), axis=0, keepdims=True)


def _mem_step_guest(qm, mk5, mv5, layer, nsteps):
    n = qm.shape[0]
    rb = n // nsteps
    blk = ((1, rb, MEM_H, MEM_HD, N_MEM), lambda s: (layer, s, 0, 0, 0))
    rows = ((rb, 1, MEM_W), lambda s: (s, 0, 0))
    return dict(inputs=[qm.reshape(n, 1, MEM_W), mk5, mv5], in_specs=[rows, blk, blk],
                out_shape=[jax.ShapeDtypeStruct((n, 1, MEM_W), F32)], out_specs=[rows], scratch=[],
                init=lambda step, *refs: None, pre=_mem_step_pre, post=_mem_step_post)


def _post_a_kernel(x_ref, o_ref, gate_ref, om_ref, wo1_ref, wo2_ref, gpost_ref, xo_ref):
    o_main = _gla_out_gate(o_ref[...], gate_ref[...])
    xo_ref[...] = _mix_residual(x_ref[...], o_main, om_ref[...], wo1_ref, wo2_ref, gpost_ref)


def _post_a_call(x2d, o, gate, om, wa):
    args = [x2d, o, gate, om, wa["wo1"], wa["wo2"], wa["gpost"]]
    return pl.pallas_call(
        _post_a_kernel,
        grid=(1,),
        in_specs=[_full_spec(a) for a in args],
        out_specs=_full_spec(x2d),
        out_shape=jax.ShapeDtypeStruct(x2d.shape, F32),
        compiler_params=_params(("arbitrary",)),
        name="post_a_sample",
    )(*args)


def _pre_b_kernel(x_ref, gkv_ref, wkv_ref, gpre_ref, wq_ref, wqm_ref, ks_ref, vs_ref, q_ref, qm_ref):
    x = x_ref[...]
    kv = _dot(_bf(_rms(x, gkv_ref[...])), wkv_ref[...])
    ks_ref[...] = kv[:, :SWA_KVW]
    vs_ref[...] = kv[:, SWA_KVW:]
    h = _bf(_rms(x, gpre_ref[...]))
    q_ref[...] = _dot(h, wq_ref[...])
    qm_ref[...] = _dot(h, wqm_ref[...])


def _pre_b_call(x2d, wb):
    n = x2d.shape[0]
    ws = [wb["gkv"], wb["wkv"], wb["gpre"], wb["wq"], wb["wqm"]]
    widths = [SWA_KVW, SWA_KVW, SWA_QW, MEM_W]
    return pl.pallas_call(
        _pre_b_kernel,
        grid=(1,),
        in_specs=[_full_spec(x2d)] + [_full_spec(w) for w in ws],
        out_specs=[pl.BlockSpec((n, w), lambda i: (0, 0)) for w in widths],
        out_shape=[jax.ShapeDtypeStruct((n, w), F32) for w in widths],
        compiler_params=_params(("arbitrary",)),
        name="pre_b_sample",
    )(x2d, *ws)


def _sample_buckets():
    dist = (WINDOW - 1) - np.arange(WINDOW)
    return _t5_bucket(dist).astype(np.int32).reshape(1, WINDOW)


def _swa_step_init(step, sink_ref, rb_ref, bkt_ref, kc_ref, vc_ref, kst_ref, vst_ref, q_ref,
                   kn_ref, vn_ref, o_ref, bias_s, sink_s, s_s, p_s):
    W = WINDOW
    R = SWA_KVH * 8

    @pl.when(step == 0)
    def _():
        bkt = bkt_ref[...]
        rid = _iota((R, W), 0)
        bias = jnp.zeros((R, W), F32)
        sink = jnp.zeros((R, W), F32)
        for h in range(SWA_KVH):
            for g in range(SWA_G):
                idx = g * SWA_KVH + h
                def add_bucket(n, acc):
                    return jnp.where(bkt == n, rb_ref[n, idx], acc)
                brow = lax.fori_loop(0, N_BUCKETS, add_bucket, jnp.zeros((1, W), F32))
                bias = jnp.where(rid == h * 8 + g, brow, bias)
                sink = jnp.where(rid == h * 8 + g, sink_ref[idx], sink)
        bias_s[...] = bias
        sink_s[...] = sink


def _swa_step_pre(step, sink_ref, rb_ref, bkt_ref, kc_ref, vc_ref, kst_ref, vst_ref, q_ref,
                  kn_ref, vn_ref, o_ref, bias_s, sink_s, s_s, p_s):
    W = WINDOW
    R = SWA_KVH * 8
    rb = q_ref.shape[0]
    base = step * rb
    last = _iota((SWA_KVW, W), 1) == W - 1
    own = (_iota((R, SWA_KVW), 1) >> 6) == (_iota((R, SWA_KVW), 0) >> 3)
    kst = kst_ref[...]
    vst = vst_ref[...]
    for i in range(rb):
        shift = W - 1 - (base + i)
        kn = jnp.where(last, pltpu.roll(kst, shift, 1), pltpu.roll(kc_ref[i].reshape(SWA_KVW, W), W - 1, 1))
        vn = jnp.where(last, pltpu.roll(vst, shift, 1), pltpu.roll(vc_ref[i].reshape(SWA_KVW, W), W - 1, 1))
        kn_ref[i] = kn.reshape(SWA_KVH, SWA_HD, W)
        vn_ref[i] = vn.reshape(SWA_KVH, SWA_HD, W)
        q32 = jnp.where(own, jnp.concatenate([q_ref[i]] * SWA_KVH, axis=0), 0.0)
        s_s[i * R:(i + 1) * R, :] = _dot(_bf(q32), _bf(kn))
    s = s_s[...] * (SWA_HD ** -0.5) + jnp.concatenate([bias_s[...]] * rb, axis=0)
    sink = jnp.concatenate([sink_s[...]] * rb, axis=0)
    mx = jnp.maximum(jnp.max(s, axis=1, keepdims=True), sink)
    p = jnp.exp(s - mx)
    p_s[...] = _bf(p / (jnp.sum(p, axis=1, keepdims=True) + jnp.exp(sink - mx)))


def _swa_step_post(step, _, sink_ref, rb_ref, bkt_ref, kc_ref, vc_ref, kst_ref, vst_ref, q_ref,
                   kn_ref, vn_ref, o_ref, bias_s, sink_s, s_s, p_s):
    W = WINDOW
    R = SWA_KVH * 8
    own = (_iota((R, SWA_KVW), 1) >> 6) == (_iota((R, SWA_KVW), 0) >> 3)
    for i in range(q_ref.shape[0]):
        vn = vn_ref[i].reshape(SWA_KVW, W)
        res = jnp.where(own, _dot_nt(p_s[i * R:(i + 1) * R, :], _bf(vn)), 0.0)
        o_ref[i] = res[0:8] + res[8:16] + res[16:24] + res[24:32]


def _swa_step_guest(kc4, vc4, ks, vs, q, wb, nsteps):
    n = q.shape[0]
    rb = n // nsteps
    bkt = jnp.asarray(_sample_buckets())
    q8 = jnp.pad(q.reshape(n, SWA_G, SWA_KVW), ((0, 0), (0, 8 - SWA_G), (0, 0)))
    row3 = lambda r, w: ((rb, r, w), lambda s: (s, 0, 0))
    cache = ((rb, SWA_KVH, SWA_HD, WINDOW), lambda s: (s, 0, 0, 0))
    whole = ((SWA_KVW, n), lambda s: (0, 0))
    smem = pl.BlockSpec(memory_space=pltpu.SMEM)
    return dict(inputs=[wb["sinks"], wb["rb"], bkt, kc4, vc4, ks.T, vs.T, q8],
                in_specs=[smem, smem, _full_spec(bkt), cache, cache, whole, whole, row3(8, SWA_KVW)],
                out_shape=[jax.ShapeDtypeStruct(kc4.shape, F32), jax.ShapeDtypeStruct(vc4.shape, F32),
                           jax.ShapeDtypeStruct((n, 8, SWA_KVW), F32)],
                out_specs=[cache, cache, row3(8, SWA_KVW)],
                scratch=[pltpu.VMEM((SWA_KVH * 8, WINDOW), F32), pltpu.VMEM((SWA_KVH * 8, WINDOW), F32),
                         pltpu.VMEM((rb * SWA_KVH * 8, WINDOW), F32),
                         pltpu.VMEM((rb * SWA_KVH * 8, WINDOW), BF16)],
                init=_swa_step_init, pre=_swa_step_pre, post=_swa_step_post)


def _post_b_kernel(x_ref, o_ref, om_ref, wo1_ref, wo2_ref, gpost_ref, xo_ref):
    xo_ref[...] = _mix_residual(x_ref[...], o_ref[...], om_ref[...], wo1_ref, wo2_ref, gpost_ref)


def _post_b_call(x2d, o, om, wb):
    args = [x2d, o, om, wb["wo1"], wb["wo2"], wb["gpost"]]
    return pl.pallas_call(
        _post_b_kernel,
        grid=(1,),
        in_specs=[_full_spec(a) for a in args],
        out_specs=_full_spec(x2d),
        out_shape=jax.ShapeDtypeStruct(x2d.shape, F32),
        compiler_params=_params(("arbitrary",)),
        name="post_b_sample",
    )(*args)


def _prep_weights(norm_mix_pre, norm_mix_post, w_in_a, w_gate_up, b_gate,
                  gla_norm, w_in_b, sinks, norm_kv, w_kv, rel_bias, w_out):
    row = lambda g: g.reshape(1, -1)
    wa_in = w_in_a[0]
    c_v = 2 * GLA_KW
    c_r = c_v + GLA_VW
    c_g = c_r + GLA_VW
    c_m = c_g + GATE_RANK
    wmisc = jnp.concatenate([wa_in[:, c_g:c_m], jnp.zeros((D, 128 - GATE_RANK), F32), wa_in[:, c_m:]], axis=1)
    wg = jnp.zeros((128, GLA_KW), F32).at[:GATE_RANK].set(w_gate_up[0])
    wa = dict(
        gpre=row(norm_mix_pre[0]), wqk=_bf(wa_in[:, :c_v]), wv=_bf(wa_in[:, c_v:c_r]), wr=_bf(wa_in[:, c_r:c_g]),
        wmisc=_bf(wmisc), wg=_bf(wg), bg=row(b_gate[0]), glan=row(jnp.tile(gla_norm[0], GLA_H)),
        wo1=_bf(w_out[0][:GLA_VW]), wo2=_bf(w_out[0][GLA_VW:]), gpost=row(norm_mix_post[0]))
    wb_in = w_in_b[0]
    wq = wb_in[:, :SWA_QW].reshape(D, SWA_KVH, SWA_G, SWA_HD).transpose(0, 2, 1, 3).reshape(D, SWA_QW)
    wo1 = w_out[1][:SWA_QW].reshape(SWA_KVH, SWA_G, SWA_HD, D).transpose(1, 0, 2, 3).reshape(SWA_QW, D)
    rb = rel_bias.reshape(N_BUCKETS, SWA_KVH, SWA_G).transpose(0, 2, 1).reshape(N_BUCKETS, SWA_G * SWA_KVH)
    sk = sinks[0].reshape(SWA_KVH, SWA_G).T.reshape(SWA_G * SWA_KVH)
    wb = dict(
        gkv=row(norm_kv), wkv=_bf(w_kv), gpre=row(norm_mix_pre[1]), wq=_bf(wq), wqm=_bf(wb_in[:, SWA_QW:]),
        wo1=_bf(wo1), wo2=_bf(w_out[1][SWA_QW:]), gpost=row(norm_mix_post[1]), rb=rb, sinks=sk)
    return wa, wb


def kernel(x_prompt, x_sample, state_gla, cache_swa_k, cache_swa_v, cache_mem_k, cache_mem_v, mem_prompt,
           norm_mix_pre, norm_mix_post, norm_ffn_pre, norm_ffn_post, norm_mem, w_mem_kv, w_in_a, w_gate_up,
           b_gate, gla_norm, w_in_b, sinks, norm_kv, w_kv, rel_bias, w_out, w_ffn_up, w_ffn_down):
    wa, wb = _prep_weights(norm_mix_pre, norm_mix_post, w_in_a, w_gate_up, b_gate, gla_norm, w_in_b, sinks,
                           norm_kv, w_kv, rel_bias, w_out)
    ffn = lambda xp, xs_, l: _ffn_call(xp, xs_, norm_ffn_pre, w_ffn_up, w_ffn_down, norm_ffn_post, l)
    nb, seq, _ = x_prompt.shape
    ns = x_sample.shape[0]

    xs = x_sample.reshape(ns, D)
    state5 = jnp.transpose(state_gla, (0, 2, 3, 4, 1))
    mk5 = jnp.transpose(cache_mem_k, (0, 1, 3, 4, 2))
    mv5 = jnp.transpose(cache_mem_v, (0, 1, 3, 4, 2))
    kc4 = jnp.transpose(cache_swa_k, (0, 2, 3, 1))
    vc4 = jnp.transpose(cache_swa_v, (0, 2, 3, 1))

    mkt, mvt, kbd, vbd = _memkv_call(mem_prompt, norm_mem, w_mem_kv)
    q, k, g, v, gate, qm = _pre_a_call(xs, wa)
    (x1, st), ((state5_new, ot), (om,)) = _mixer_a_call(
        x_prompt, wa, kbd[0], vbd[0],
        lambda nsteps: [_gla_step_guest(q.T, k.T, g.T, v.T, state5, nsteps),
                        _mem_step_guest(qm, mk5, mv5, 0, nsteps)])
    xs1 = _post_a_call(xs, ot.T, gate, om.reshape(ns, MEM_W), wa)
    x2, xs2 = ffn(x1.reshape(nb * seq, D), xs1, 0)
    ks, vs, qb, qmb = _pre_b_call(xs2, wb)
    (x3, kc, vc), ((kn4, vn4, o8), (omb,)) = _mixer_b_call(
        x2.reshape(nb, seq, D), wb, kbd[1], vbd[1],
        lambda nsteps: [_swa_step_guest(kc4, vc4, ks, vs, qb, wb, nsteps),
                        _mem_step_guest(qmb, mk5, mv5, 1, nsteps)])
    xs3 = _post_b_call(xs2, o8[:, :SWA_G].reshape(ns, SWA_QW), omb.reshape(ns, MEM_W), wb)
    y_prompt, y_sample = ffn(x3.reshape(nb * seq, D), xs3, 1)
    y_prompt = y_prompt.reshape(nb, seq, D)
    y_sample = y_sample.reshape(ns, 1, D)
    st4 = st.reshape(nb, GLA_H, GLA_DK, GLA_H, GLA_DV)
    state_prompt = jnp.stack([st4[:, h, :, h, :] for h in range(GLA_H)], axis=1)[None]
    to_mem = lambda t: t.reshape(2, nb, MEM_H, MEM_HD, N_MEM).transpose(0, 1, 4, 2, 3)
    swa_shape = (nb, WINDOW, SWA_KVH, SWA_HD)

    return (y_prompt, y_sample, state_prompt,
            jnp.transpose(state5_new, (0, 4, 1, 2, 3)),
            kc.reshape(swa_shape), vc.reshape(swa_shape),
            jnp.transpose(kn4, (0, 3, 1, 2)), jnp.transpose(vn4, (0, 3, 1, 2)),
            to_mem(mkt), to_mem(mvt))
```

```python
import functools
import math

import numpy as np
import jax
import jax.numpy as jnp
from jax import lax
from jax.experimental import pallas as pl
from jax.experimental.pallas import tpu as pltpu

F32 = jnp.float32
BF16 = jnp.bfloat16

D = 1024
D_FF = 4 * D
N_MEM = 256
MEM_H = 4
MEM_HD = 64
MEM_W = MEM_H * MEM_HD
GLA_H = 4
GLA_DK = 96
GLA_DV = 192
GLA_KW = GLA_H * GLA_DK
GLA_VW = GLA_H * GLA_DV
GATE_RANK = 16
GATE_NORM = 16.0
SWA_HD = 64
SWA_KVH = 4
SWA_G = 3
SWA_QW = SWA_KVH * SWA_G * SWA_HD
SWA_KVW = SWA_KVH * SWA_HD
WINDOW = 128
N_BUCKETS = 32
MAX_DISTANCE = 128
EPS = 1e-6

GLA_CHUNK = 64
GLA_SUB = 16
GLA_SAFE_DECAY = 60.0
GLA_GROUP = 4
SWA_BLOCKS_PER_TRIP = 2
TM_PROMPT = 512
FF_CHUNK = 512
GLA_DK_BLOCK = 16
V7X_VMEM_LIMIT = 56 * 1024 * 1024
NEG_INF = float("-inf")


def _bf(x):
    return x.astype(BF16)


def _dot(a, b):
    return jnp.dot(a, b, preferred_element_type=F32)


def _dot_nt(a, b):
    return lax.dot_general(a, b, (((1,), (1,)), ((), ())), preferred_element_type=F32)


def _dot_tn(a, b):
    return lax.dot_general(a, b, (((0,), (0,)), ((), ())), preferred_element_type=F32)


def _rms(x, g):
    return x * lax.rsqrt(jnp.mean(x * x, axis=-1, keepdims=True) + EPS) * g


def _split3(x):
    x1 = _bf(x)
    r1 = x - x1.astype(F32)
    x2 = _bf(r1)
    x3 = _bf(r1 - x2.astype(F32))
    return x1, x2, x3


def _exact_dot(sel, x):
    x1, x2, x3 = _split3(x)
    return _dot(sel, x1) + _dot(sel, x2) + _dot(sel, x3)


def _log_sigmoid(z):
    return jnp.minimum(z, 0.0) - jnp.log1p(jnp.exp(-jnp.abs(z)))


def _silu(z):
    return z * (1.0 / (1.0 + jnp.exp(-z)))


def _iota(shape, dim):
    return lax.broadcasted_iota(jnp.int32, shape, dim)


def _gla_k_head(lane):
    one = jnp.int32(1)
    zero = jnp.int32(0)
    return (jnp.where(lane >= GLA_DK, one, zero) + jnp.where(lane >= 2 * GLA_DK, one, zero)
            + jnp.where(lane >= 3 * GLA_DK, one, zero))


def _gla_v_head(lane):
    one = jnp.int32(1)
    zero = jnp.int32(0)
    return (jnp.where(lane >= GLA_DV, one, zero) + jnp.where(lane >= 2 * GLA_DV, one, zero)
            + jnp.where(lane >= 3 * GLA_DV, one, zero))


def _full_spec(a):
    nd = a.ndim
    return pl.BlockSpec(a.shape, lambda *_: (0,) * nd)


def _params(sem):
    return pltpu.CompilerParams(dimension_semantics=sem, vmem_limit_bytes=V7X_VMEM_LIMIT)


def _call_with_guests(host_kernel, grid, step_of, inputs, in_specs, out_shape, out_specs, scratch, guests, name):
    def to_spec(s):
        if isinstance(s, pl.BlockSpec):
            return s
        block, fn = s
        return pl.BlockSpec(block, lambda *ids: fn(step_of(*ids)))

    counts = [(len(inputs), len(out_shape), len(scratch))]
    counts += [(len(g["inputs"]), len(g["out_shape"]), len(g["scratch"])) for g in guests]

    def kern(*refs):
        refs = list(refs)
        parts = [[refs.pop(0) for _ in range(c[k])] for k in range(3) for c in counts]
        n = len(counts)
        ins, outs, scrs = parts[:n], parts[n:2 * n], parts[2 * n:]
        step = step_of(*[pl.program_id(a) for a in range(len(grid))])
        grefs = [(*a, *b, *c) for a, b, c in zip(ins[1:], outs[1:], scrs[1:])]
        states = []

        def pre_hook():
            for g, r in zip(guests, grefs):
                g["init"](step, *r)
            states.extend(g["pre"](step, *r) for g, r in zip(guests, grefs))

        def hook():
            for g, st, r in zip(guests, states, grefs):
                g["post"](step, st, *r)

        host_kernel(*ins[0], *outs[0], *scrs[0], pre_hook=pre_hook, hook=hook)

    res = pl.pallas_call(
        kern,
        grid=grid,
        in_specs=list(in_specs) + [to_spec(s) for g in guests for s in g["in_specs"]],
        out_specs=list(out_specs) + [to_spec(s) for g in guests for s in g["out_specs"]],
        out_shape=list(out_shape) + [s for g in guests for s in g["out_shape"]],
        scratch_shapes=list(scratch) + [s for g in guests for s in g["scratch"]],
        compiler_params=_params(("arbitrary",) * len(grid)),
        name=name,
    )(*inputs, *[a for g in guests for a in g["inputs"]])
    res = list(res)
    split = []
    for c in counts:
        split.append([res.pop(0) for _ in range(c[1])])
    return split[0], split[1:]


def _mem_probs(qm, kbd_ref):
    s = _dot(_bf(qm), kbd_ref[...]) * (MEM_HD ** -0.5)
    ps = []
    for h in range(MEM_H):
        sh = s[:, h * N_MEM:(h + 1) * N_MEM]
        e = jnp.exp(sh - jnp.max(sh, axis=-1, keepdims=True))
        ps.append(_bf(e / jnp.sum(e, axis=-1, keepdims=True)))
    return ps


def _mem_pv(ps, vbd_ref):
    out = None
    for h in range(MEM_H):
        t = _dot(ps[h], vbd_ref[h * N_MEM:(h + 1) * N_MEM, :])
        out = t if out is None else out + t
    return out


def _memkv_kernel(mem_ref, g_ref, w_ref, k_ref, v_ref, kbd_ref, vbd_ref):
    h = _bf(_rms(mem_ref[0], g_ref[0]))
    kv = _dot(h, w_ref[0])
    k = kv[:, :MEM_W]
    v = kv[:, MEM_W:]
    kt = k.T
    k_ref[0, 0] = kt
    v_ref[0, 0] = v.T
    kt4 = jnp.concatenate([kt, kt, kt, kt], axis=1)
    keep_k = (_iota((MEM_W, MEM_H * N_MEM), 0) >> 6) == (_iota((MEM_W, MEM_H * N_MEM), 1) >> 8)
    kbd_ref[0, 0] = _bf(jnp.where(keep_k, kt4, 0.0))
    v4 = jnp.concatenate([v, v, v, v], axis=0)
    keep_v = (_iota((MEM_H * N_MEM, MEM_W), 0) >> 8) == (_iota((MEM_H * N_MEM, MEM_W), 1) >> 6)
    vbd_ref[0, 0] = _bf(jnp.where(keep_v, v4, 0.0))


def _memkv_call(mem, norm_mem, w_mem_kv):
    nb = mem.shape[0]
    nl = w_mem_kv.shape[0]
    g = norm_mem.reshape(nl, 1, D)
    w = _bf(w_mem_kv)
    return pl.pallas_call(
        _memkv_kernel,
        grid=(nl, nb),
        in_specs=[
            pl.BlockSpec((1, N_MEM, D), lambda l, b: (b, 0, 0)),
            pl.BlockSpec((1, 1, D), lambda l, b: (l, 0, 0)),
            pl.BlockSpec((1, D, 2 * MEM_W), lambda l, b: (l, 0, 0)),
        ],
        out_specs=[
            pl.BlockSpec((1, 1, MEM_W, N_MEM), lambda l, b: (l, b, 0, 0)),
            pl.BlockSpec((1, 1, MEM_W, N_MEM), lambda l, b: (l, b, 0, 0)),
            pl.BlockSpec((1, 1, MEM_W, MEM_H * N_MEM), lambda l, b: (l, b, 0, 0)),
            pl.BlockSpec((1, 1, MEM_H * N_MEM, MEM_W), lambda l, b: (l, b, 0, 0)),
        ],
        out_shape=[
            jax.ShapeDtypeStruct((nl, nb, MEM_W, N_MEM), F32),
            jax.ShapeDtypeStruct((nl, nb, MEM_W, N_MEM), F32),
            jax.ShapeDtypeStruct((nl, nb, MEM_W, MEM_H * N_MEM), BF16),
            jax.ShapeDtypeStruct((nl, nb, MEM_H * N_MEM, MEM_W), BF16),
        ],
        compiler_params=_params(("arbitrary", "arbitrary")),
        name="mem_kv",
    )(mem, g, w)


def _proj_a(x, gpre_ref, wqk_ref, wv_ref, wr_ref, wmisc_ref, wg_ref, bg_ref, glan_ref, mem_refs=None):
    h = _bf(_rms(x, gpre_ref[...]))
    misc = _dot(h, wmisc_ref[...])
    glr = misc[:, :128]
    qm = misc[:, 128:]
    ps = None if mem_refs is None else _mem_probs(qm, mem_refs[0])
    g = _log_sigmoid(_dot(_bf(glr), wg_ref[...]) + bg_ref[...]) * (1.0 / GATE_NORM)
    gate = glan_ref[...] * _silu(_dot(h, wr_ref[...]))
    qk = _dot(h, wqk_ref[...])
    q = qk[:, :GLA_KW] * (GLA_DK ** -0.5)
    k = qk[:, GLA_KW:]
    v = _dot(h, wv_ref[...])
    return q, k, g, v, gate, (qm if mem_refs is None else _mem_pv(ps, mem_refs[1]))


def _gla_out_gate(o, gate):
    vh = _gla_v_head(_iota(o.shape, 1))
    o2 = o * o
    scale = jnp.zeros_like(o)
    for h in range(GLA_H):
        ss = jnp.sum(jnp.where(vh == h, o2, 0.0), axis=-1, keepdims=True) * (1.0 / GLA_DV)
        scale = jnp.where(vh == h, lax.rsqrt(ss + EPS), scale)
    return o * scale * gate


def _mix_residual(x, o_main, o_mem, wo1_ref, wo2_ref, gpost_ref):
    mix = _dot(_bf(o_main), wo1_ref[...]) + _dot(_bf(o_mem), wo2_ref[...])
    return x + _rms(mix, gpost_ref[...])


def _make_gla_chunk(q_s, k_s, g_s, v_s, o_s, state_s, p_s):
    C = GLA_CHUNK

    ri = _iota((C, GLA_KW), 0)
    kh = _gla_k_head(_iota((C, GLA_KW), 1))
    khcat = jnp.concatenate([kh, kh, kh], axis=1)
    vh = _gla_v_head(_iota((C, GLA_VW), 1))
    tri = _bf(jnp.where(_iota((C, C), 0) >= _iota((C, C), 1), 1.0, 0.0))
    d_rs = _iota((C, 4 * C), 0) - (_iota((C, 4 * C), 1) & (C - 1))
    band = jnp.where((d_rs >= 0) & (d_rs <= (_iota((C, 4 * C), 0) & (GLA_SUB - 1))), d_rs, -1)
    ebc = _bf(jnp.where(_gla_k_head(_iota((GLA_KW, 4 * C), 0)) == (_iota((GLA_KW, 4 * C), 1) >> 6),
                        1.0, 0.0))
    blockmask = _gla_k_head(_iota((GLA_KW, GLA_VW), 0)) == _gla_v_head(_iota((GLA_KW, GLA_VW), 1))
    eye = _iota((GLA_KW, GLA_KW), 0) == _iota((GLA_KW, GLA_KW), 1)

    def chunk(c, carry):
        r0 = pl.multiple_of(c * C, C)
        qc = q_s[pl.ds(r0, C), :]
        kc = k_s[pl.ds(r0, C), :]
        gc = g_s[pl.ds(r0, C), :]
        vc = v_s[pl.ds(r0, C), :]
        b = _exact_dot(tri, gc)

        ref1 = jnp.broadcast_to(b[31:32, :], b.shape)
        ref2 = jnp.where(ri < 32, jnp.broadcast_to(b[15:16, :], b.shape),
                         jnp.broadcast_to(b[47:48, :], b.shape))
        q1 = qc * jnp.exp(jnp.minimum(b - ref1, 0.0))
        k1 = kc * jnp.exp(jnp.minimum(ref1 - b, 0.0))
        q2 = qc * jnp.exp(jnp.minimum(b - ref2, 0.0))
        k2 = kc * jnp.exp(jnp.minimum(ref2 - b, 0.0))
        qcat = jnp.concatenate([
            jnp.where(ri >= 32, q1, 0.0),
            jnp.where((ri >= 16) & (ri < 32), q2, 0.0),
            jnp.where(ri >= 48, q2, 0.0)], axis=1)
        kcat = jnp.concatenate([
            jnp.where(ri < 32, k1, 0.0),
            jnp.where(ri < 16, k2, 0.0),
            jnp.where((ri >= 32) & (ri < 48), k2, 0.0)], axis=1)
        kst = _bf(jnp.concatenate([jnp.where(khcat == h, kcat, 0.0) for h in range(GLA_H)], axis=0))
        a_off = _dot_nt(_bf(qcat), kst)

        for dlt in range(GLA_SUB):
            if dlt == 0:
                pr = qc * kc
            else:
                kd = pltpu.roll(kc, dlt, 0)
                bd = pltpu.roll(b, dlt, 0)
                pr = qc * kd * jnp.exp(jnp.minimum(b - bd, 0.0))
            p_s[dlt * C:(dlt + 1) * C, :] = _bf(pr)
        rsum = _dot(p_s[...], ebc)
        a_diag = jnp.zeros((C, 4 * C), F32)
        for dlt in range(GLA_SUB):
            a_diag = jnp.where(band == dlt, rsum[dlt * C:(dlt + 1) * C, :], a_diag)

        vst = _bf(jnp.concatenate([jnp.where(vh == h, vc, 0.0) for h in range(GLA_H)], axis=0))
        o_intra = _dot(_bf(a_off + a_diag), vst)
        st = state_s[...]
        o_inter = _dot(_bf(qc * jnp.exp(b)), _bf(st))
        o_s[pl.ds(r0, C), :] = o_intra + o_inter

        blast = jnp.broadcast_to(b[C - 1:C, :], b.shape)
        kdec = kc * jnp.exp(blast - b)
        kv = _dot_tn(_bf(kdec), _bf(vc))
        decay = _row_to_col(jnp.exp(b[C - 1:C, :]), eye)
        state_s[...] = st * decay + jnp.where(blockmask, kv, 0.0)
        return carry

    return chunk


def _make_gla_chunk_bounded(q_s, k_s, g_s, v_s, o_s, state_s):
    C = GLA_CHUNK
    kh = _gla_k_head(_iota((C, GLA_KW), 1))
    vh = _gla_v_head(_iota((C, GLA_VW), 1))
    tri = _bf(jnp.where(_iota((C, C), 0) >= _iota((C, C), 1), 1.0, 0.0))
    causal = _iota((C, 4 * C), 0) >= (_iota((C, 4 * C), 1) & (C - 1))
    blockmask = _gla_k_head(_iota((GLA_KW, GLA_VW), 0)) == _gla_v_head(_iota((GLA_KW, GLA_VW), 1))
    eye = _iota((GLA_KW, GLA_KW), 0) == _iota((GLA_KW, GLA_KW), 1)

    def group(gi, carry):
        base = gi * (GLA_GROUP * C)
        rows = [pl.ds(pl.multiple_of(base + i * C, C), C) for i in range(GLA_GROUP)]
        bs = [_exact_dot(tri, g_s[r, :]) for r in rows]
        qes = [_bf(q_s[r, :] * jnp.exp(b)) for r, b in zip(rows, bs)]
        kes = [k_s[r, :] * jnp.exp(-b) for r, b in zip(rows, bs)]
        ksts = [_bf(jnp.concatenate([jnp.where(kh == h, ke, 0.0) for h in range(GLA_H)], axis=0)) for ke in kes]
        attn = [_bf(jnp.where(causal, _dot_nt(qe, kst), 0.0)) for qe, kst in zip(qes, ksts)]
        vcs = [v_s[r, :] for r in rows]
        vsts = [_bf(jnp.concatenate([jnp.where(vh == h, vc, 0.0) for h in range(GLA_H)], axis=0)) for vc in vcs]
        o_intra = [_dot(a, vst) for a, vst in zip(attn, vsts)]
        decays = [jnp.exp(b[C - 1:C, :]) for b in bs]
        kvs = [_dot_tn(_bf(ke * d), _bf(vc)) for vc, ke, d in zip(vcs, kes, decays)]
        dcols = [_row_to_col(d, eye) for d in decays]
        st = state_s[...]
        for i in range(GLA_GROUP):
            o_s[rows[i], :] = o_intra[i] + _dot(qes[i], _bf(st))
            st = st * dcols[i] + jnp.where(blockmask, kvs[i], 0.0)
        state_s[...] = st
        return carry

    return group


def _mixer_a_kernel(x_ref, gpre_ref, wqk_ref, wv_ref, wr_ref, wmisc_ref, wg_ref, bg_ref, glan_ref,
                    kbd_ref, vbd_ref, wo1_ref, wo2_ref, gpost_ref,
                    xo_ref, st_ref,
                    q_s, k_s, g_s, v_s, o_s, state_s, p_s, gate_s, om_s, pre_hook, hook):
    C = GLA_CHUNK
    tm = x_ref.shape[1]
    nchunk = tm // C

    @pl.when(pl.program_id(1) == 0)
    def _():
        state_s[...] = jnp.zeros_like(state_s)

    pre_hook()

    q, k, g, v, gate, o_mem = _proj_a(x_ref[0], gpre_ref, wqk_ref, wv_ref, wr_ref, wmisc_ref, wg_ref, bg_ref,
                                      glan_ref, (kbd_ref.at[0], vbd_ref.at[0]))
    q_s[...] = q
    k_s[...] = k
    g_s[...] = g
    v_s[...] = v
    gate_s[...] = gate
    om_s[...] = _bf(o_mem)
    hook()

    total = jnp.sum(g.reshape(nchunk, C, GLA_KW), axis=1)
    bounded = jnp.min(total) > -GLA_SAFE_DECAY

    @pl.when(bounded)
    def _():
        lax.fori_loop(0, nchunk // GLA_GROUP, _make_gla_chunk_bounded(q_s, k_s, g_s, v_s, o_s, state_s), 0)

    @pl.when(jnp.logical_not(bounded))
    def _():
        lax.fori_loop(0, nchunk, _make_gla_chunk(q_s, k_s, g_s, v_s, o_s, state_s, p_s), 0)

    st_ref[0] = state_s[...]
    o_main = _gla_out_gate(o_s[...], gate_s[...])
    xo_ref[0] = _mix_residual(x_ref[0], o_main, om_s[...], wo1_ref, wo2_ref, gpost_ref)


def _const_spec(a):
    nd = a.ndim
    return pl.BlockSpec(a.shape, lambda *_: (0,) * nd, pipeline_mode=pl.Buffered(1))


def _mixer_a_call(x, wa, kbd, vbd, make_guests):
    nb, seq, _ = x.shape
    tm = TM_PROMPT
    nt = seq // tm
    weights = [wa["gpre"], wa["wqk"], wa["wv"], wa["wr"], wa["wmisc"], wa["wg"], wa["bg"], wa["glan"]]
    tail = [wa["wo1"], wa["wo2"], wa["gpost"]]
    return _call_with_guests(
        _mixer_a_kernel,
        grid=(nb, nt),
        step_of=lambda b, t: b * nt + t,
        inputs=[x, *weights, kbd, vbd, *tail],
        in_specs=([pl.BlockSpec((1, tm, D), lambda b, t: (b, t, 0))]
                  + [_const_spec(w) for w in weights]
                  + [pl.BlockSpec((1, MEM_W, MEM_H * N_MEM), lambda b, t: (b, 0, 0)),
                     pl.BlockSpec((1, MEM_H * N_MEM, MEM_W), lambda b, t: (b, 0, 0))]
                  + [_const_spec(w) for w in tail]),
        out_specs=[pl.BlockSpec((1, tm, D), lambda b, t: (b, t, 0)),
                   pl.BlockSpec((1, GLA_KW, GLA_VW), lambda b, t: (b, 0, 0))],
        out_shape=[jax.ShapeDtypeStruct((nb, seq, D), F32),
                   jax.ShapeDtypeStruct((nb, GLA_KW, GLA_VW), F32)],
        scratch=[
            pltpu.VMEM((tm, GLA_KW), F32), pltpu.VMEM((tm, GLA_KW), F32), pltpu.VMEM((tm, GLA_KW), F32),
            pltpu.VMEM((tm, GLA_VW), F32), pltpu.VMEM((tm, GLA_VW), F32),
            pltpu.VMEM((GLA_KW, GLA_VW), F32),
            pltpu.VMEM((GLA_SUB * GLA_CHUNK, GLA_KW), BF16),
            pltpu.VMEM((tm, GLA_VW), F32), pltpu.VMEM((tm, MEM_W), BF16),
        ],
        guests=make_guests(nb * nt),
        name="mixer_a_prompt",
    )


def _ffn_rows(x, gpre_ref, wup_ref, wdn_ref, gpost_ref, acc_ref, arrive=None):
    arrive = arrive if arrive is not None else (lambda which, j: None)
    h = _bf(_rms(x, gpre_ref[0]))
    nff = D_FF // FF_CHUNK
    for j in range(nff - 1):
        cols = slice(j * FF_CHUNK, (j + 1) * FF_CHUNK)
        arrive(0, j)
        u = jnp.maximum(_dot(h, _bf(wup_ref[:, cols])), 0.0)
        arrive(1, j)
        d = _dot(_bf(u * u), _bf(wdn_ref[cols, :]))
        if j == 0:
            acc_ref[...] = d
        else:
            acc_ref[...] += d
    cols = slice((nff - 1) * FF_CHUNK, nff * FF_CHUNK)
    arrive(0, nff - 1)
    u = jnp.maximum(_dot(h, _bf(wup_ref[:, cols])), 0.0)
    uu = _bf(u * u)
    arrive(1, nff - 1)
    wd = _bf(wdn_ref[cols, :])
    m = x.shape[0]
    bounds = (0, m) if m < 256 else (0, m // 2, m)
    parts = []
    for lo, hi in zip(bounds[:-1], bounds[1:]):
        tot = acc_ref[lo:hi, :] + _dot(uu[lo:hi], wd)
        parts.append(x[lo:hi] + _rms(tot, gpost_ref[0]))
    return parts[0] if len(parts) == 1 else jnp.concatenate(parts, axis=0)


def _ffn_kernel(xp_ref, xs_ref, gpre_ref, wup_hbm, wdn_hbm, gpost_ref, yp_ref, ys_ref, acc_s, wup_s, wdn_s, sems,
                *, layer):
    i = pl.program_id(0)
    last = pl.num_programs(0) - 1
    nff = D_FF // FF_CHUNK

    def copy(which, j):
        cols = slice(j * FF_CHUNK, (j + 1) * FF_CHUNK)
        if which == 0:
            return pltpu.make_async_copy(wup_hbm.at[layer, :, cols], wup_s.at[:, cols], sems.at[0, j])
        return pltpu.make_async_copy(wdn_hbm.at[layer, cols, :], wdn_s.at[cols, :], sems.at[1, j])

    @pl.when(i == 0)
    def _():
        for j in range(nff):
            copy(0, j).start()
            copy(1, j).start()
        yp_ref[...] = _ffn_rows(xp_ref[...], gpre_ref, wup_s, wdn_s, gpost_ref, acc_s,
                                arrive=lambda which, j: copy(which, j).wait())

    @pl.when((i > 0) & (i < last))
    def _():
        yp_ref[...] = _ffn_rows(xp_ref[...], gpre_ref, wup_s, wdn_s, gpost_ref, acc_s)

    @pl.when(i == last)
    def _():
        ns = xs_ref.shape[0]
        ys_ref[...] = _ffn_rows(xs_ref[...], gpre_ref, wup_s, wdn_s, gpost_ref, acc_s.at[0:ns, :])


def _ffn_call(xp, xs, norm_pre, w_up, w_down, norm_post, layer):
    n, ns = xp.shape[0], xs.shape[0]
    tm = TM_PROMPT
    nt = n // tm
    gains = [a.reshape(a.shape[0], 1, D) for a in (norm_pre, norm_post)]
    gspec = pl.BlockSpec((1, 1, D), lambda i: (layer, 0, 0))
    hbm = pl.BlockSpec(memory_space=pl.ANY)
    tile = pl.BlockSpec((tm, D), lambda i: (jnp.minimum(i, nt - 1), 0))
    return pl.pallas_call(
        functools.partial(_ffn_kernel, layer=layer),
        grid=(nt + 1,),
        in_specs=[tile, _full_spec(xs), gspec, hbm, hbm, gspec],
        out_specs=[tile, _full_spec(xs)],
        out_shape=[jax.ShapeDtypeStruct((n, D), F32), jax.ShapeDtypeStruct((ns, D), F32)],
        scratch_shapes=[pltpu.VMEM((tm, D), F32), pltpu.VMEM((D, D_FF), F32), pltpu.VMEM((D_FF, D), F32),
                        pltpu.SemaphoreType.DMA((2, D_FF // FF_CHUNK))],
        compiler_params=_params(("arbitrary",)),
        name="ffn",
    )(xp, xs, gains[0], w_up, w_down, gains[1])


def _t5_bucket(dist):
    max_exact = N_BUCKETS // 2
    n = np.maximum(dist, 0)
    nf = np.maximum(n, 1).astype(np.float32)
    large = max_exact + (np.log(nf / np.float32(max_exact)) / np.float32(math.log(MAX_DISTANCE / max_exact))
                         * np.float32(N_BUCKETS - max_exact)).astype(np.int32)
    large = np.minimum(large, N_BUCKETS - 1)
    return np.where(n < max_exact, n, large)


def _swa_bias_tables():
    qi = np.arange(WINDOW)[:, None] + WINDOW
    kj = np.arange(2 * WINDOW)[None, :]
    dist = qi - kj
    valid = (dist >= 0) & (dist < WINDOW)
    return np.where(valid, _t5_bucket(dist), -1).astype(np.int32)


def _mixer_b_kernel(rb_ref, sink_ref,
                    x_ref, bkt_ref, gkv_ref, wkv_ref, gpre_ref, wq_ref, wqm_ref,
                    kbd_ref, vbd_ref, wo1_ref, wo2_ref, gpost_ref,
                    xo_ref, kc_ref, vc_ref,
                    kbuf, vbuf, q_s, o_s, bias_s, qm_s, pre_hook, hook):
    W = WINDOW
    tm = x_ref.shape[1]
    bb = pl.program_id(0)
    t = pl.program_id(1)
    nheads = SWA_G * SWA_KVH

    @pl.when((bb == 0) & (t == 0))
    def _():
        bkt = bkt_ref[...]
        own = _iota((W, 2 * W), 1) >= W
        for i in range(nheads):
            def add_bucket(n, acc):
                return jnp.where(bkt == n, rb_ref[n, i], acc)
            tab = lax.fori_loop(0, N_BUCKETS, add_bucket, jnp.zeros((W, 2 * W), F32))
            tab = jnp.where(bkt < 0, NEG_INF, tab)
            bias_s[0, i] = tab
            bias_s[1, i] = jnp.where(own, tab, NEG_INF)

    @pl.when(t == 0)
    def _():
        kbuf[0:W, :] = jnp.zeros((W, SWA_KVW), F32)
        vbuf[0:W, :] = jnp.zeros((W, SWA_KVW), F32)

    pre_hook()
    x = x_ref[0]
    xn = x * lax.rsqrt(jnp.mean(x * x, axis=-1, keepdims=True) + EPS)
    kv = _dot(_bf(xn * gkv_ref[...]), wkv_ref[...])
    kbuf[W:W + tm, :] = kv[:, :SWA_KVW]
    vbuf[W:W + tm, :] = kv[:, SWA_KVW:]
    kc_ref[0] = kv[tm - W:, :SWA_KVW]
    vc_ref[0] = kv[tm - W:, SWA_KVW:]

    h = _bf(xn * gpre_ref[...])
    q_s[...] = _dot(h, wq_ref[...]) * (SWA_HD ** -0.5)
    qm_s[...] = _dot(h, wqm_ref[...])
    hook()

    lane_head = _iota((W, SWA_KVW), 1) >> 6
    key_head = _iota((2 * W, SWA_KVW), 1) >> 6

    def scores(j):
        r0 = pl.multiple_of(j * W, W)
        qrows = []
        for gi in range(SWA_G):
            qg = q_s[pl.ds(r0, W), gi * SWA_KVW:(gi + 1) * SWA_KVW]
            for hh in range(SWA_KVH):
                qrows.append(jnp.where(lane_head == hh, qg, 0.0))
        return _dot_nt(_bf(jnp.concatenate(qrows, axis=0)), _bf(kbuf[pl.ds(r0, 2 * W), :]))

    def probs(j, s_all, gi):
        tab = jnp.where((j == 0) & (t == 0), 1, 0)
        ps = []
        for hh in range(SWA_KVH):
            i = gi * SWA_KVH + hh
            s = s_all[i * W:(i + 1) * W, :] + bias_s[tab, i]
            sink = sink_ref[i]
            mx = jnp.maximum(jnp.max(s, axis=-1, keepdims=True), sink)
            p = jnp.exp(s - mx)
            p = p / (jnp.sum(p, axis=-1, keepdims=True) + jnp.exp(sink - mx))
            ps.append(_bf(p))
        return jnp.concatenate(ps, axis=1)

    def pair(jj, carry):
        js = [jj * SWA_BLOCKS_PER_TRIP + i for i in range(SWA_BLOCKS_PER_TRIP)]
        s_alls = [scores(j) for j in js]
        for j, s_all in zip(js, s_alls):
            r0 = pl.multiple_of(j * W, W)
            vb = vbuf[pl.ds(r0, 2 * W), :]
            vst = _bf(jnp.concatenate([jnp.where(key_head == hh, vb, 0.0) for hh in range(SWA_KVH)], axis=0))
            for gi in range(SWA_G):
                o_s[pl.ds(r0, W), gi * SWA_KVW:(gi + 1) * SWA_KVW] = _dot(probs(j, s_all, gi), vst)
        return carry

    lax.fori_loop(0, tm // (W * SWA_BLOCKS_PER_TRIP), pair, 0)

    kbuf[0:W, :] = kbuf[tm:tm + W, :]
    vbuf[0:W, :] = vbuf[tm:tm + W, :]

    o_mem = _mem_pv(_mem_probs(qm_s[...], kbd_ref.at[0]), vbd_ref.at[0])
    xo_ref[0] = _mix_residual(x_ref[0], o_s[...], o_mem, wo1_ref, wo2_ref, gpost_ref)


def _mixer_b_call(x, wb, kbd, vbd, make_guests):
    nb, seq, _ = x.shape
    tm = TM_PROMPT
    nt = seq // tm
    bkt = jnp.asarray(_swa_bias_tables())
    head = [bkt, wb["gkv"], wb["wkv"], wb["gpre"], wb["wq"], wb["wqm"]]
    tail = [wb["wo1"], wb["wo2"], wb["gpost"]]
    smem = pl.BlockSpec(memory_space=pltpu.SMEM)
    return _call_with_guests(
        _mixer_b_kernel,
        grid=(nb, nt),
        step_of=lambda b, t: b * nt + t,
        inputs=[wb["rb"], wb["sinks"], x, *head, kbd, vbd, *tail],
        in_specs=([smem, smem, pl.BlockSpec((1, tm, D), lambda b, t: (b, t, 0))]
                  + [_const_spec(w) for w in head]
                  + [pl.BlockSpec((1, MEM_W, MEM_H * N_MEM), lambda b, t: (b, 0, 0)),
                     pl.BlockSpec((1, MEM_H * N_MEM, MEM_W), lambda b, t: (b, 0, 0))]
                  + [_const_spec(w) for w in tail]),
        out_specs=[pl.BlockSpec((1, tm, D), lambda b, t: (b, t, 0)),
                   pl.BlockSpec((1, WINDOW, SWA_KVW), lambda b, t: (b, 0, 0)),
                   pl.BlockSpec((1, WINDOW, SWA_KVW), lambda b, t: (b, 0, 0))],
        out_shape=[jax.ShapeDtypeStruct((nb, seq, D), F32),
                   jax.ShapeDtypeStruct((nb, WINDOW, SWA_KVW), F32),
                   jax.ShapeDtypeStruct((nb, WINDOW, SWA_KVW), F32)],
        scratch=[
            pltpu.VMEM((tm + WINDOW, SWA_KVW), F32), pltpu.VMEM((tm + WINDOW, SWA_KVW), F32),
            pltpu.VMEM((tm, SWA_QW), F32), pltpu.VMEM((tm, SWA_QW), F32),
            pltpu.VMEM((2, SWA_G * SWA_KVH, WINDOW, 2 * WINDOW), F32),
            pltpu.VMEM((tm, MEM_W), F32),
        ],
        guests=make_guests(nb * nt),
        name="mixer_b_prompt",
    )


def _row_to_col(row, eye):
    return jnp.sum(jnp.where(eye, jnp.broadcast_to(row, eye.shape), 0.0), axis=1, keepdims=True)


def _eye(n):
    return _iota((n, n), 0) == _iota((n, n), 1)


def _pre_a_kernel(x_ref, gpre_ref, wqk_ref, wv_ref, wr_ref, wmisc_ref, wg_ref, bg_ref, glan_ref,
                  q_ref, k_ref, g_ref, v_ref, gate_ref, qm_ref):
    q, k, g, v, gate, qm = _proj_a(x_ref[...], gpre_ref, wqk_ref, wv_ref, wr_ref, wmisc_ref, wg_ref, bg_ref,
                                   glan_ref)
    q_ref[...] = q
    k_ref[...] = k
    g_ref[...] = g
    v_ref[...] = v
    gate_ref[...] = gate
    qm_ref[...] = qm


def _pre_a_call(x2d, wa):
    n = x2d.shape[0]
    ws = [wa["gpre"], wa["wqk"], wa["wv"], wa["wr"], wa["wmisc"], wa["wg"], wa["bg"], wa["glan"]]
    shapes = [(n, GLA_KW), (n, GLA_KW), (n, GLA_KW), (n, GLA_VW), (n, GLA_VW), (n, MEM_W)]
    return pl.pallas_call(
        _pre_a_kernel,
        grid=(1,),
        in_specs=[_full_spec(x2d)] + [_full_spec(w) for w in ws],
        out_specs=[pl.BlockSpec(s, lambda i: (0, 0)) for s in shapes],
        out_shape=[jax.ShapeDtypeStruct(s, F32) for s in shapes],
        compiler_params=_params(("arbitrary",)),
        name="pre_a_sample",
    )(x2d, *ws)


def _gla_step_guest(qt, kt, gt, vt, state5, nsteps):
    n = qt.shape[1]
    nblk = GLA_DK // GLA_DK_BLOCK
    nact = GLA_H * nblk
    assert nact <= nsteps
    act = lambda s: jnp.minimum(s, nact - 1)

    def init(step, qt_ref, kt_ref, gt_ref, vt_ref, s_ref, so_ref, ot_ref):
        @pl.when((step < nact) & (step % nblk == 0))
        def _():
            ot_ref[...] = jnp.zeros_like(ot_ref)

    def post(step, _, qt_ref, kt_ref, gt_ref, vt_ref, s_ref, so_ref, ot_ref):
        vt_blk = vt_ref[...]
        acc = jnp.zeros_like(vt_blk)
        for d in range(GLA_DK_BLOCK):
            s_new = jnp.exp(gt_ref[d:d + 1, :]) * s_ref[0, 0, d] + kt_ref[d:d + 1, :] * vt_blk
            so_ref[0, 0, d] = s_new
            acc = acc + qt_ref[d:d + 1, :] * s_new
        ot_ref[...] += jnp.where(step < nact, acc, 0.0)

    rows = ((GLA_DK_BLOCK, n), lambda s: (act(s), 0))
    head = ((GLA_DV, n), lambda s: (act(s) // nblk, 0))
    st = ((1, 1, GLA_DK_BLOCK, GLA_DV, n), lambda s: (0, act(s) // nblk, act(s) % nblk, 0, 0))
    return dict(inputs=[qt, kt, gt, vt, state5], in_specs=[rows, rows, rows, head, st],
                out_shape=[jax.ShapeDtypeStruct(state5.shape, F32), jax.ShapeDtypeStruct((GLA_VW, n), F32)],
                out_specs=[st, head], scratch=[], init=init, pre=lambda step, *refs: None, post=post)


def _mem_step_pre(step, qm_ref, mk_ref, mv_ref, o_ref):
    own = (_iota((8, MEM_W), 1) >> 6) == _iota((8, MEM_W), 0)
    ps = []
    for i in range(qm_ref.shape[0]):
        q8 = jnp.where(own, jnp.broadcast_to(qm_ref[i], (8, MEM_W)), 0.0)
        s = _dot(_bf(q8), _bf(mk_ref[0, i].reshape(MEM_W, N_MEM))) * (MEM_HD ** -0.5)
        e = jnp.exp(s - jnp.max(s, axis=1, keepdims=True))
        ps.append(_bf(e / jnp.sum(e, axis=1, keepdims=True)))
    return ps


def _mem_step_post(step, ps, qm_ref, mk_ref, mv_ref, o_ref):
    own = (_iota((8, MEM_W), 1) >> 6) == _iota((8, MEM_W), 0)
    for i, p in enumerate(ps):
        res = _dot_nt(p, _bf(mv_ref[0, i].reshape(MEM_W, N_MEM)))
        o_ref[i] = jnp.sum(jnp.where(own, res, 0.0), axis=0, keepdims=True)


def _mem_step_guest(qm, mk5, mv5, layer, nsteps):
    n = qm.shape[0]
    rb = n // nsteps
    blk = ((1, rb, MEM_H, MEM_HD, N_MEM), lambda s: (layer, s, 0, 0, 0))
    rows = ((rb, 1, MEM_W), lambda s: (s, 0, 0))
    return dict(inputs=[qm.reshape(n, 1, MEM_W), mk5, mv5], in_specs=[rows, blk, blk],
                out_shape=[jax.ShapeDtypeStruct((n, 1, MEM_W), F32)], out_specs=[rows], scratch=[],
                init=lambda step, *refs: None, pre=_mem_step_pre, post=_mem_step_post)


def _post_a_kernel(x_ref, o_ref, gate_ref, om_ref, wo1_ref, wo2_ref, gpost_ref, xo_ref):
    o_main = _gla_out_gate(o_ref[...], gate_ref[...])
    xo_ref[...] = _mix_residual(x_ref[...], o_main, om_ref[...], wo1_ref, wo2_ref, gpost_ref)


def _post_a_call(x2d, o, gate, om, wa):
    args = [x2d, o, gate, om, wa["wo1"], wa["wo2"], wa["gpost"]]
    return pl.pallas_call(
        _post_a_kernel,
        grid=(1,),
        in_specs=[_full_spec(a) for a in args],
        out_specs=_full_spec(x2d),
        out_shape=jax.ShapeDtypeStruct(x2d.shape, F32),
        compiler_params=_params(("arbitrary",)),
        name="post_a_sample",
    )(*args)


def _pre_b_kernel(x_ref, gkv_ref, wkv_ref, gpre_ref, wq_ref, wqm_ref, ks_ref, vs_ref, q_ref, qm_ref):
    x = x_ref[...]
    kv = _dot(_bf(_rms(x, gkv_ref[...])), wkv_ref[...])
    ks_ref[...] = kv[:, :SWA_KVW]
    vs_ref[...] = kv[:, SWA_KVW:]
    h = _bf(_rms(x, gpre_ref[...]))
    q_ref[...] = _dot(h, wq_ref[...])
    qm_ref[...] = _dot(h, wqm_ref[...])


def _pre_b_call(x2d, wb):
    n = x2d.shape[0]
    ws = [wb["gkv"], wb["wkv"], wb["gpre"], wb["wq"], wb["wqm"]]
    widths = [SWA_KVW, SWA_KVW, SWA_QW, MEM_W]
    return pl.pallas_call(
        _pre_b_kernel,
        grid=(1,),
        in_specs=[_full_spec(x2d)] + [_full_spec(w) for w in ws],
        out_specs=[pl.BlockSpec((n, w), lambda i: (0, 0)) for w in widths],
        out_shape=[jax.ShapeDtypeStruct((n, w), F32) for w in widths],
        compiler_params=_params(("arbitrary",)),
        name="pre_b_sample",
    )(x2d, *ws)


def _sample_buckets():
    dist = (WINDOW - 1) - np.arange(WINDOW)
    return _t5_bucket(dist).astype(np.int32).reshape(1, WINDOW)


def _swa_step_init(step, sink_ref, rb_ref, bkt_ref, kc_ref, vc_ref, kst_ref, vst_ref, q_ref,
                   kn_ref, vn_ref, o_ref, bias_s, sink_s, s_s, p_s):
    W = WINDOW
    R = SWA_KVH * 8

    @pl.when(step == 0)
    def _():
        bkt = bkt_ref[...]
        rid = _iota((R, W), 0)
        bias = jnp.zeros((R, W), F32)
        sink = jnp.zeros((R, W), F32)
        for h in range(SWA_KVH):
            for g in range(SWA_G):
                idx = g * SWA_KVH + h
                def add_bucket(n, acc):
                    return jnp.where(bkt == n, rb_ref[n, idx], acc)
                brow = lax.fori_loop(0, N_BUCKETS, add_bucket, jnp.zeros((1, W), F32))
                bias = jnp.where(rid == h * 8 + g, brow, bias)
                sink = jnp.where(rid == h * 8 + g, sink_ref[idx], sink)
        bias_s[...] = bias
        sink_s[...] = sink


def _swa_step_pre(step, sink_ref, rb_ref, bkt_ref, kc_ref, vc_ref, kst_ref, vst_ref, q_ref,
                  kn_ref, vn_ref, o_ref, bias_s, sink_s, s_s, p_s):
    W = WINDOW
    R = SWA_KVH * 8
    rb = q_ref.shape[0]
    base = step * rb
    last = _iota((SWA_KVW, W), 1) == W - 1
    own = (_iota((R, SWA_KVW), 1) >> 6) == (_iota((R, SWA_KVW), 0) >> 3)
    kst = kst_ref[...]
    vst = vst_ref[...]
    for i in range(rb):
        shift = W - 1 - (base + i)
        kn = jnp.where(last, pltpu.roll(kst, shift, 1), pltpu.roll(kc_ref[i].reshape(SWA_KVW, W), W - 1, 1))
        vn = jnp.where(last, pltpu.roll(vst, shift, 1), pltpu.roll(vc_ref[i].reshape(SWA_KVW, W), W - 1, 1))
        kn_ref[i] = kn.reshape(SWA_KVH, SWA_HD, W)
        vn_ref[i] = vn.reshape(SWA_KVH, SWA_HD, W)
        q32 = jnp.where(own, jnp.concatenate([q_ref[i]] * SWA_KVH, axis=0), 0.0)
        s_s[i * R:(i + 1) * R, :] = _dot(_bf(q32), _bf(kn))
    s = s_s[...] * (SWA_HD ** -0.5) + jnp.concatenate([bias_s[...]] * rb, axis=0)
    sink = jnp.concatenate([sink_s[...]] * rb, axis=0)
    mx = jnp.maximum(jnp.max(s, axis=1, keepdims=True), sink)
    p = jnp.exp(s - mx)
    p_s[...] = _bf(p / (jnp.sum(p, axis=1, keepdims=True) + jnp.exp(sink - mx)))


def _swa_step_post(step, _, sink_ref, rb_ref, bkt_ref, kc_ref, vc_ref, kst_ref, vst_ref, q_ref,
                   kn_ref, vn_ref, o_ref, bias_s, sink_s, s_s, p_s):
    W = WINDOW
    R = SWA_KVH * 8
    own = (_iota((R, SWA_KVW), 1) >> 6) == (_iota((R, SWA_KVW), 0) >> 3)
    for i in range(q_ref.shape[0]):
        vn = vn_ref[i].reshape(SWA_KVW, W)
        res = jnp.where(own, _dot_nt(p_s[i * R:(i + 1) * R, :], _bf(vn)), 0.0)
        o_ref[i] = res[0:8] + res[8:16] + res[16:24] + res[24:32]


def _swa_step_guest(kc4, vc4, ks, vs, q, wb, nsteps):
    n = q.shape[0]
    rb = n // nsteps
    bkt = jnp.asarray(_sample_buckets())
    q8 = jnp.pad(q.reshape(n, SWA_G, SWA_KVW), ((0, 0), (0, 8 - SWA_G), (0, 0)))
    row3 = lambda r, w: ((rb, r, w), lambda s: (s, 0, 0))
    cache = ((rb, SWA_KVH, SWA_HD, WINDOW), lambda s: (s, 0, 0, 0))
    whole = ((SWA_KVW, n), lambda s: (0, 0))
    smem = pl.BlockSpec(memory_space=pltpu.SMEM)
    return dict(inputs=[wb["sinks"], wb["rb"], bkt, kc4, vc4, ks.T, vs.T, q8],
                in_specs=[smem, smem, _full_spec(bkt), cache, cache, whole, whole, row3(8, SWA_KVW)],
                out_shape=[jax.ShapeDtypeStruct(kc4.shape, F32), jax.ShapeDtypeStruct(vc4.shape, F32),
                           jax.ShapeDtypeStruct((n, 8, SWA_KVW), F32)],
                out_specs=[cache, cache, row3(8, SWA_KVW)],
                scratch=[pltpu.VMEM((SWA_KVH * 8, WINDOW), F32), pltpu.VMEM((SWA_KVH * 8, WINDOW), F32),
                         pltpu.VMEM((rb * SWA_KVH * 8, WINDOW), F32),
                         pltpu.VMEM((rb * SWA_KVH * 8, WINDOW), BF16)],
                init=_swa_step_init, pre=_swa_step_pre, post=_swa_step_post)


def _post_b_kernel(x_ref, o_ref, om_ref, wo1_ref, wo2_ref, gpost_ref, xo_ref):
    xo_ref[...] = _mix_residual(x_ref[...], o_ref[...], om_ref[...], wo1_ref, wo2_ref, gpost_ref)


def _post_b_call(x2d, o, om, wb):
    args = [x2d, o, om, wb["wo1"], wb["wo2"], wb["gpost"]]
    return pl.pallas_call(
        _post_b_kernel,
        grid=(1,),
        in_specs=[_full_spec(a) for a in args],
        out_specs=_full_spec(x2d),
        out_shape=jax.ShapeDtypeStruct(x2d.shape, F32),
        compiler_params=_params(("arbitrary",)),
        name="post_b_sample",
    )(*args)


def _prep_weights(norm_mix_pre, norm_mix_post, w_in_a, w_gate_up, b_gate,
                  gla_norm, w_in_b, sinks, norm_kv, w_kv, rel_bias, w_out):
    row = lambda g: g.reshape(1, -1)
    wa_in = w_in_a[0]
    c_v = 2 * GLA_KW
    c_r = c_v + GLA_VW
    c_g = c_r + GLA_VW
    c_m = c_g + GATE_RANK
    wmisc = jnp.concatenate([wa_in[:, c_g:c_m], jnp.zeros((D, 128 - GATE_RANK), F32), wa_in[:, c_m:]], axis=1)
    wg = jnp.zeros((128, GLA_KW), F32).at[:GATE_RANK].set(w_gate_up[0])
    wa = dict(
        gpre=row(norm_mix_pre[0]), wqk=_bf(wa_in[:, :c_v]), wv=_bf(wa_in[:, c_v:c_r]), wr=_bf(wa_in[:, c_r:c_g]),
        wmisc=_bf(wmisc), wg=_bf(wg), bg=row(b_gate[0]), glan=row(jnp.tile(gla_norm[0], GLA_H)),
        wo1=_bf(w_out[0][:GLA_VW]), wo2=_bf(w_out[0][GLA_VW:]), gpost=row(norm_mix_post[0]))
    wb_in = w_in_b[0]
    wq = wb_in[:, :SWA_QW].reshape(D, SWA_KVH, SWA_G, SWA_HD).transpose(0, 2, 1, 3).reshape(D, SWA_QW)
    wo1 = w_out[1][:SWA_QW].reshape(SWA_KVH, SWA_G, SWA_HD, D).transpose(1, 0, 2, 3).reshape(SWA_QW, D)
    rb = rel_bias.reshape(N_BUCKETS, SWA_KVH, SWA_G).transpose(0, 2, 1).reshape(N_BUCKETS, SWA_G * SWA_KVH)
    sk = sinks[0].reshape(SWA_KVH, SWA_G).T.reshape(SWA_G * SWA_KVH)
    wb = dict(
        gkv=row(norm_kv), wkv=_bf(w_kv), gpre=row(norm_mix_pre[1]), wq=_bf(wq), wqm=_bf(wb_in[:, SWA_QW:]),
        wo1=_bf(wo1), wo2=_bf(w_out[1][SWA_QW:]), gpost=row(norm_mix_post[1]), rb=rb, sinks=sk)
    return wa, wb


def kernel(x_prompt, x_sample, state_gla, cache_swa_k, cache_swa_v, cache_mem_k, cache_mem_v, mem_prompt,
           norm_mix_pre, norm_mix_post, norm_ffn_pre, norm_ffn_post, norm_mem, w_mem_kv, w_in_a, w_gate_up,
           b_gate, gla_norm, w_in_b, sinks, norm_kv, w_kv, rel_bias, w_out, w_ffn_up, w_ffn_down):
    wa, wb = _prep_weights(norm_mix_pre, norm_mix_post, w_in_a, w_gate_up, b_gate, gla_norm, w_in_b, sinks,
                           norm_kv, w_kv, rel_bias, w_out)
    ffn = lambda xp, xs_, l: _ffn_call(xp, xs_, norm_ffn_pre, w_ffn_up, w_ffn_down, norm_ffn_post, l)
    nb, seq, _ = x_prompt.shape
    ns = x_sample.shape[0]

    xs = x_sample.reshape(ns, D)
    state5 = jnp.transpose(state_gla, (0, 2, 3, 4, 1))
    mk5 = jnp.transpose(cache_mem_k, (0, 1, 3, 4, 2))
    mv5 = jnp.transpose(cache_mem_v, (0, 1, 3, 4, 2))
    kc4 = jnp.transpose(cache_swa_k, (0, 2, 3, 1))
    vc4 = jnp.transpose(cache_swa_v, (0, 2, 3, 1))

    mkt, mvt, kbd, vbd = _memkv_call(mem_prompt, norm_mem, w_mem_kv)
    q, k, g, v, gate, qm = _pre_a_call(xs, wa)
    (x1, st), ((state5_new, ot), (om,)) = _mixer_a_call(
        x_prompt, wa, kbd[0], vbd[0],
        lambda nsteps: [_gla_step_guest(q.T, k.T, g.T, v.T, state5, nsteps),
                        _mem_step_guest(qm, mk5, mv5, 0, nsteps)])
    xs1 = _post_a_call(xs, ot.T, gate, om.reshape(ns, MEM_W), wa)
    x2, xs2 = ffn(x1.reshape(nb * seq, D), xs1, 0)
    ks, vs, qb, qmb = _pre_b_call(xs2, wb)
    (x3, kc, vc), ((kn4, vn4, o8), (omb,)) = _mixer_b_call(
        x2.reshape(nb, seq, D), wb, kbd[1], vbd[1],
        lambda nsteps: [_swa_step_guest(kc4, vc4, ks, vs, qb, wb, nsteps),
                        _mem_step_guest(qmb, mk5, mv5, 1, nsteps)])
    xs3 = _post_b_call(xs2, o8[:, :SWA_G].reshape(ns, SWA_QW), omb.reshape(ns, MEM_W), wb)
    y_prompt, y_sample = ffn(x3.reshape(nb * seq, D), xs3, 1)
    y_prompt = y_prompt.reshape(nb, seq, D)
    y_sample = y_sample.reshape(ns, 1, D)
    st4 = st.reshape(nb, GLA_H, GLA_DK, GLA_H, GLA_DV)
    state_prompt = jnp.stack([st4[:, h, :, h, :] for h in range(GLA_H)], axis=1)[None]
    to_mem = lambda t: t.reshape(2, nb, MEM_H, MEM_HD, N_MEM).transpose(0, 1, 4, 2, 3)
    swa_shape = (nb, WINDOW, SWA_KVH, SWA_HD)

    return (y_prompt, y_sample, state_prompt,
            jnp.transpose(state5_new, (0, 4, 1, 2, 3)),
            kc.reshape(swa_shape), vc.reshape(swa_shape),
            jnp.transpose(kn4, (0, 3, 1, 2)), jnp.transpose(vn4, (0, 3, 1, 2)),
            to_mem(mkt), to_mem(mvt))
```

```python
import functools
import math

import numpy as np
import jax
import jax.numpy as jnp
from jax import lax
from jax.experimental import pallas as pl
from jax.experimental.pallas import tpu as pltpu

F32 = jnp.float32
BF16 = jnp.bfloat16

D = 1024
D_FF = 4 * D
N_MEM = 256
MEM_H = 4
MEM_HD = 64
MEM_W = MEM_H * MEM_HD
GLA_H = 4
GLA_DK = 96
GLA_DV = 192
GLA_KW = GLA_H * GLA_DK
GLA_VW = GLA_H * GLA_DV
GATE_RANK = 16
GATE_NORM = 16.0
SWA_HD = 64
SWA_KVH = 4
SWA_G = 3
SWA_QW = SWA_KVH * SWA_G * SWA_HD
SWA_KVW = SWA_KVH * SWA_HD
WINDOW = 128
N_BUCKETS = 32
MAX_DISTANCE = 128
EPS = 1e-6

GLA_CHUNK = 64
GLA_SUB = 16
GLA_SAFE_DECAY = 60.0
GLA_GROUP = 4
SWA_BLOCKS_PER_TRIP = 2
TM_PROMPT = 512
FF_CHUNK = 512
GLA_DK_BLOCK = 16
V7X_VMEM_LIMIT = 56 * 1024 * 1024
NEG_INF = float("-inf")


def _bf(x):
    return x.astype(BF16)


def _dot(a, b):
    return jnp.dot(a, b, preferred_element_type=F32)


def _dot_nt(a, b):
    return lax.dot_general(a, b, (((1,), (1,)), ((), ())), preferred_element_type=F32)


def _dot_tn(a, b):
    return lax.dot_general(a, b, (((0,), (0,)), ((), ())), preferred_element_type=F32)


def _rms(x, g):
    return x * lax.rsqrt(jnp.mean(x * x, axis=-1, keepdims=True) + EPS) * g


def _split3(x):
    x1 = _bf(x)
    r1 = x - x1.astype(F32)
    x2 = _bf(r1)
    x3 = _bf(r1 - x2.astype(F32))
    return x1, x2, x3


def _exact_dot(sel, x):
    x1, x2, x3 = _split3(x)
    return _dot(sel, x1) + _dot(sel, x2) + _dot(sel, x3)


def _log_sigmoid(z):
    return jnp.minimum(z, 0.0) - jnp.log1p(jnp.exp(-jnp.abs(z)))


def _silu(z):
    return z * (1.0 / (1.0 + jnp.exp(-z)))


def _iota(shape, dim):
    return lax.broadcasted_iota(jnp.int32, shape, dim)


def _gla_k_head(lane):
    one = jnp.int32(1)
    zero = jnp.int32(0)
    return (jnp.where(lane >= GLA_DK, one, zero) + jnp.where(lane >= 2 * GLA_DK, one, zero)
            + jnp.where(lane >= 3 * GLA_DK, one, zero))


def _gla_v_head(lane):
    one = jnp.int32(1)
    zero = jnp.int32(0)
    return (jnp.where(lane >= GLA_DV, one, zero) + jnp.where(lane >= 2 * GLA_DV, one, zero)
            + jnp.where(lane >= 3 * GLA_DV, one, zero))


def _full_spec(a):
    nd = a.ndim
    return pl.BlockSpec(a.shape, lambda *_: (0,) * nd)


def _params(sem):
    return pltpu.CompilerParams(dimension_semantics=sem, vmem_limit_bytes=V7X_VMEM_LIMIT)


def _call_with_guests(host_kernel, grid, step_of, inputs, in_specs, out_shape, out_specs, scratch, guests, name):
    def to_spec(s):
        if isinstance(s, pl.BlockSpec):
            return s
        block, fn = s
        return pl.BlockSpec(block, lambda *ids: fn(step_of(*ids)))

    counts = [(len(inputs), len(out_shape), len(scratch))]
    counts += [(len(g["inputs"]), len(g["out_shape"]), len(g["scratch"])) for g in guests]

    def kern(*refs):
        refs = list(refs)
        parts = [[refs.pop(0) for _ in range(c[k])] for k in range(3) for c in counts]
        n = len(counts)
        ins, outs, scrs = parts[:n], parts[n:2 * n], parts[2 * n:]
        step = step_of(*[pl.program_id(a) for a in range(len(grid))])
        grefs = [(*a, *b, *c) for a, b, c in zip(ins[1:], outs[1:], scrs[1:])]
        states = []

        def pre_hook():
            for g, r in zip(guests, grefs):
                g["init"](step, *r)
            states.extend(g["pre"](step, *r) for g, r in zip(guests, grefs))

        def hook():
            for g, st, r in zip(guests, states, grefs):
                g["post"](step, st, *r)

        host_kernel(*ins[0], *outs[0], *scrs[0], pre_hook=pre_hook, hook=hook)

    res = pl.pallas_call(
        kern,
        grid=grid,
        in_specs=list(in_specs) + [to_spec(s) for g in guests for s in g["in_specs"]],
        out_specs=list(out_specs) + [to_spec(s) for g in guests for s in g["out_specs"]],
        out_shape=list(out_shape) + [s for g in guests for s in g["out_shape"]],
        scratch_shapes=list(scratch) + [s for g in guests for s in g["scratch"]],
        compiler_params=_params(("arbitrary",) * len(grid)),
        name=name,
    )(*inputs, *[a for g in guests for a in g["inputs"]])
    res = list(res)
    split = []
    for c in counts:
        split.append([res.pop(0) for _ in range(c[1])])
    return split[0], split[1:]


def _mem_probs(qm, kbd_ref):
    s = _dot(_bf(qm), kbd_ref[...]) * (MEM_HD ** -0.5)
    ps = []
    for h in range(MEM_H):
        sh = s[:, h * N_MEM:(h + 1) * N_MEM]
        e = jnp.exp(sh - jnp.max(sh, axis=-1, keepdims=True))
        ps.append(_bf(e / jnp.sum(e, axis=-1, keepdims=True)))
    return ps


def _mem_pv(ps, vbd_ref):
    out = None
    for h in range(MEM_H):
        t = _dot(ps[h], vbd_ref[h * N_MEM:(h + 1) * N_MEM, :])
        out = t if out is None else out + t
    return out


def _memkv_kernel(mem_ref, g_ref, w_ref, k_ref, v_ref, kbd_ref, vbd_ref):
    h = _bf(_rms(mem_ref[0], g_ref[0]))
    kv = _dot(h, w_ref[0])
    k = kv[:, :MEM_W]
    v = kv[:, MEM_W:]
    kt = k.T
    k_ref[0, 0] = kt
    v_ref[0, 0] = v.T
    kt4 = jnp.concatenate([kt, kt, kt, kt], axis=1)
    keep_k = (_iota((MEM_W, MEM_H * N_MEM), 0) >> 6) == (_iota((MEM_W, MEM_H * N_MEM), 1) >> 8)
    kbd_ref[0, 0] = _bf(jnp.where(keep_k, kt4, 0.0))
    v4 = jnp.concatenate([v, v, v, v], axis=0)
    keep_v = (_iota((MEM_H * N_MEM, MEM_W), 0) >> 8) == (_iota((MEM_H * N_MEM, MEM_W), 1) >> 6)
    vbd_ref[0, 0] = _bf(jnp.where(keep_v, v4, 0.0))


def _memkv_call(mem, norm_mem, w_mem_kv):
    nb = mem.shape[0]
    nl = w_mem_kv.shape[0]
    g = norm_mem.reshape(nl, 1, D)
    w = _bf(w_mem_kv)
    return pl.pallas_call(
        _memkv_kernel,
        grid=(nl, nb),
        in_specs=[
            pl.BlockSpec((1, N_MEM, D), lambda l, b: (b, 0, 0)),
            pl.BlockSpec((1, 1, D), lambda l, b: (l, 0, 0)),
            pl.BlockSpec((1, D, 2 * MEM_W), lambda l, b: (l, 0, 0)),
        ],
        out_specs=[
            pl.BlockSpec((1, 1, MEM_W, N_MEM), lambda l, b: (l, b, 0, 0)),
            pl.BlockSpec((1, 1, MEM_W, N_MEM), lambda l, b: (l, b, 0, 0)),
            pl.BlockSpec((1, 1, MEM_W, MEM_H * N_MEM), lambda l, b: (l, b, 0, 0)),
            pl.BlockSpec((1, 1, MEM_H * N_MEM, MEM_W), lambda l, b: (l, b, 0, 0)),
        ],
        out_shape=[
            jax.ShapeDtypeStruct((nl, nb, MEM_W, N_MEM), F32),
            jax.ShapeDtypeStruct((nl, nb, MEM_W, N_MEM), F32),
            jax.ShapeDtypeStruct((nl, nb, MEM_W, MEM_H * N_MEM), BF16),
            jax.ShapeDtypeStruct((nl, nb, MEM_H * N_MEM, MEM_W), BF16),
        ],
        compiler_params=_params(("arbitrary", "arbitrary")),
        name="mem_kv",
    )(mem, g, w)


def _proj_a(x, gpre_ref, wqk_ref, wv_ref, wr_ref, wmisc_ref, wg_ref, bg_ref, glan_ref, mem_refs=None):
    h = _bf(_rms(x, gpre_ref[...]))
    misc = _dot(h, wmisc_ref[...])
    glr = misc[:, :128]
    qm = misc[:, 128:]
    ps = None if mem_refs is None else _mem_probs(qm, mem_refs[0])
    g = _log_sigmoid(_dot(_bf(glr), wg_ref[...]) + bg_ref[...]) * (1.0 / GATE_NORM)
    gate = glan_ref[...] * _silu(_dot(h, wr_ref[...]))
    qk = _dot(h, wqk_ref[...])
    q = qk[:, :GLA_KW] * (GLA_DK ** -0.5)
    k = qk[:, GLA_KW:]
    v = _dot(h, wv_ref[...])
    return q, k, g, v, gate, (qm if mem_refs is None else _mem_pv(ps, mem_refs[1]))


def _gla_out_gate(o, gate):
    vh = _gla_v_head(_iota(o.shape, 1))
    o2 = o * o
    scale = jnp.zeros_like(o)
    for h in range(GLA_H):
        ss = jnp.sum(jnp.where(vh == h, o2, 0.0), axis=-1, keepdims=True) * (1.0 / GLA_DV)
        scale = jnp.where(vh == h, lax.rsqrt(ss + EPS), scale)
    return o * scale * gate


def _mix_residual(x, o_main, o_mem, wo1_ref, wo2_ref, gpost_ref):
    mix = _dot(_bf(o_main), wo1_ref[...]) + _dot(_bf(o_mem), wo2_ref[...])
    return x + _rms(mix, gpost_ref[...])


def _make_gla_chunk(q_s, k_s, g_s, v_s, o_s, state_s, p_s):
    C = GLA_CHUNK

    ri = _iota((C, GLA_KW), 0)
    kh = _gla_k_head(_iota((C, GLA_KW), 1))
    khcat = jnp.concatenate([kh, kh, kh], axis=1)
    vh = _gla_v_head(_iota((C, GLA_VW), 1))
    tri = _bf(jnp.where(_iota((C, C), 0) >= _iota((C, C), 1), 1.0, 0.0))
    d_rs = _iota((C, 4 * C), 0) - (_iota((C, 4 * C), 1) & (C - 1))
    band = jnp.where((d_rs >= 0) & (d_rs <= (_iota((C, 4 * C), 0) & (GLA_SUB - 1))), d_rs, -1)
    ebc = _bf(jnp.where(_gla_k_head(_iota((GLA_KW, 4 * C), 0)) == (_iota((GLA_KW, 4 * C), 1) >> 6),
                        1.0, 0.0))
    blockmask = _gla_k_head(_iota((GLA_KW, GLA_VW), 0)) == _gla_v_head(_iota((GLA_KW, GLA_VW), 1))
    eye = _iota((GLA_KW, GLA_KW), 0) == _iota((GLA_KW, GLA_KW), 1)

    def chunk(c, carry):
        r0 = pl.multiple_of(c * C, C)
        qc = q_s[pl.ds(r0, C), :]
        kc = k_s[pl.ds(r0, C), :]
        gc = g_s[pl.ds(r0, C), :]
        vc = v_s[pl.ds(r0, C), :]
        b = _exact_dot(tri, gc)

        ref1 = jnp.broadcast_to(b[31:32, :], b.shape)
        ref2 = jnp.where(ri < 32, jnp.broadcast_to(b[15:16, :], b.shape),
                         jnp.broadcast_to(b[47:48, :], b.shape))
        q1 = qc * jnp.exp(jnp.minimum(b - ref1, 0.0))
        k1 = kc * jnp.exp(jnp.minimum(ref1 - b, 0.0))
        q2 = qc * jnp.exp(jnp.minimum(b - ref2, 0.0))
        k2 = kc * jnp.exp(jnp.minimum(ref2 - b, 0.0))
        qcat = jnp.concatenate([
            jnp.where(ri >= 32, q1, 0.0),
            jnp.where((ri >= 16) & (ri < 32), q2, 0.0),
            jnp.where(ri >= 48, q2, 0.0)], axis=1)
        kcat = jnp.concatenate([
            jnp.where(ri < 32, k1, 0.0),
            jnp.where(ri < 16, k2, 0.0),
            jnp.where((ri >= 32) & (ri < 48), k2, 0.0)], axis=1)
        kst = _bf(jnp.concatenate([jnp.where(khcat == h, kcat, 0.0) for h in range(GLA_H)], axis=0))
        a_off = _dot_nt(_bf(qcat), kst)

        for dlt in range(GLA_SUB):
            if dlt == 0:
                pr = qc * kc
            else:
                kd = pltpu.roll(kc, dlt, 0)
                bd = pltpu.roll(b, dlt, 0)
                pr = qc * kd * jnp.exp(jnp.minimum(b - bd, 0.0))
            p_s[dlt * C:(dlt + 1) * C, :] = _bf(pr)
        rsum = _dot(p_s[...], ebc)
        a_diag = jnp.zeros((C, 4 * C), F32)
        for dlt in range(GLA_SUB):
            a_diag = jnp.where(band == dlt, rsum[dlt * C:(dlt + 1) * C, :], a_diag)

        vst = _bf(jnp.concatenate([jnp.where(vh == h, vc, 0.0) for h in range(GLA_H)], axis=0))
        o_intra = _dot(_bf(a_off + a_diag), vst)
        st = state_s[...]
        o_inter = _dot(_bf(qc * jnp.exp(b)), _bf(st))
        o_s[pl.ds(r0, C), :] = o_intra + o_inter

        blast = jnp.broadcast_to(b[C - 1:C, :], b.shape)
        kdec = kc * jnp.exp(blast - b)
        kv = _dot_tn(_bf(kdec), _bf(vc))
        decay = _row_to_col(jnp.exp(b[C - 1:C, :]), eye)
        state_s[...] = st * decay + jnp.where(blockmask, kv, 0.0)
        return carry

    return chunk


def _make_gla_chunk_bounded(q_s, k_s, g_s, v_s, o_s, state_s):
    C = GLA_CHUNK
    kh = _gla_k_head(_iota((C, GLA_KW), 1))
    vh = _gla_v_head(_iota((C, GLA_VW), 1))
    tri = _bf(jnp.where(_iota((C, C), 0) >= _iota((C, C), 1), 1.0, 0.0))
    causal = _iota((C, 4 * C), 0) >= (_iota((C, 4 * C), 1) & (C - 1))
    blockmask = _gla_k_head(_iota((GLA_KW, GLA_VW), 0)) == _gla_v_head(_iota((GLA_KW, GLA_VW), 1))
    eye = _iota((GLA_KW, GLA_KW), 0) == _iota((GLA_KW, GLA_KW), 1)

    def group(gi, carry):
        base = gi * (GLA_GROUP * C)
        rows = [pl.ds(pl.multiple_of(base + i * C, C), C) for i in range(GLA_GROUP)]
        bs = [_exact_dot(tri, g_s[r, :]) for r in rows]
        qes = [_bf(q_s[r, :] * jnp.exp(b)) for r, b in zip(rows, bs)]
        kes = [k_s[r, :] * jnp.exp(-b) for r, b in zip(rows, bs)]
        ksts = [_bf(jnp.concatenate([jnp.where(kh == h, ke, 0.0) for h in range(GLA_H)], axis=0)) for ke in kes]
        attn = [_bf(jnp.where(causal, _dot_nt(qe, kst), 0.0)) for qe, kst in zip(qes, ksts)]
        vcs = [v_s[r, :] for r in rows]
        vsts = [_bf(jnp.concatenate([jnp.where(vh == h, vc, 0.0) for h in range(GLA_H)], axis=0)) for vc in vcs]
        o_intra = [_dot(a, vst) for a, vst in zip(attn, vsts)]
        decays = [jnp.exp(b[C - 1:C, :]) for b in bs]
        kvs = [_dot_tn(_bf(ke * d), _bf(vc)) for vc, ke, d in zip(vcs, kes, decays)]
        dcols = [_row_to_col(d, eye) for d in decays]
        st = state_s[...]
        for i in range(GLA_GROUP):
            o_s[rows[i], :] = o_intra[i] + _dot(qes[i], _bf(st))
            st = st * dcols[i] + jnp.where(blockmask, kvs[i], 0.0)
        state_s[...] = st
        return carry

    return group


def _mixer_a_kernel(x_ref, gpre_ref, wqk_ref, wv_ref, wr_ref, wmisc_ref, wg_ref, bg_ref, glan_ref,
                    kbd_ref, vbd_ref, wo1_ref, wo2_ref, gpost_ref,
                    xo_ref, st_ref,
                    q_s, k_s, g_s, v_s, o_s, state_s, p_s, gate_s, om_s, pre_hook, hook):
    C = GLA_CHUNK
    tm = x_ref.shape[1]
    nchunk = tm // C

    @pl.when(pl.program_id(1) == 0)
    def _():
        state_s[...] = jnp.zeros_like(state_s)

    pre_hook()

    q, k, g, v, gate, o_mem = _proj_a(x_ref[0], gpre_ref, wqk_ref, wv_ref, wr_ref, wmisc_ref, wg_ref, bg_ref,
                                      glan_ref, (kbd_ref.at[0], vbd_ref.at[0]))
    q_s[...] = q
    k_s[...] = k
    g_s[...] = g
    v_s[...] = v
    gate_s[...] = gate
    om_s[...] = _bf(o_mem)
    hook()

    total = jnp.sum(g.reshape(nchunk, C, GLA_KW), axis=1)
    bounded = jnp.min(total) > -GLA_SAFE_DECAY

    @pl.when(bounded)
    def _():
        lax.fori_loop(0, nchunk // GLA_GROUP, _make_gla_chunk_bounded(q_s, k_s, g_s, v_s, o_s, state_s), 0)

    @pl.when(jnp.logical_not(bounded))
    def _():
        lax.fori_loop(0, nchunk, _make_gla_chunk(q_s, k_s, g_s, v_s, o_s, state_s, p_s), 0)

    st_ref[0] = state_s[...]
    o_main = _gla_out_gate(o_s[...], gate_s[...])
    xo_ref[0] = _mix_residual(x_ref[0], o_main, om_s[...], wo1_ref, wo2_ref, gpost_ref)


def _const_spec(a):
    nd = a.ndim
    return pl.BlockSpec(a.shape, lambda *_: (0,) * nd, pipeline_mode=pl.Buffered(1))


def _mixer_a_call(x, wa, kbd, vbd, make_guests):
    nb, seq, _ = x.shape
    tm = TM_PROMPT
    nt = seq // tm
    weights = [wa["gpre"], wa["wqk"], wa["wv"], wa["wr"], wa["wmisc"], wa["wg"], wa["bg"], wa["glan"]]
    tail = [wa["wo1"], wa["wo2"], wa["gpost"]]
    return _call_with_guests(
        _mixer_a_kernel,
        grid=(nb, nt),
        step_of=lambda b, t: b * nt + t,
        inputs=[x, *weights, kbd, vbd, *tail],
        in_specs=([pl.BlockSpec((1, tm, D), lambda b, t: (b, t, 0))]
                  + [_const_spec(w) for w in weights]
                  + [pl.BlockSpec((1, MEM_W, MEM_H * N_MEM), lambda b, t: (b, 0, 0)),
                     pl.BlockSpec((1, MEM_H * N_MEM, MEM_W), lambda b, t: (b, 0, 0))]
                  + [_const_spec(w) for w in tail]),
        out_specs=[pl.BlockSpec((1, tm, D), lambda b, t: (b, t, 0)),
                   pl.BlockSpec((1, GLA_KW, GLA_VW), lambda b, t: (b, 0, 0))],
        out_shape=[jax.ShapeDtypeStruct((nb, seq, D), F32),
                   jax.ShapeDtypeStruct((nb, GLA_KW, GLA_VW), F32)],
        scratch=[
            pltpu.VMEM((tm, GLA_KW), F32), pltpu.VMEM((tm, GLA_KW), F32), pltpu.VMEM((tm, GLA_KW), F32),
            pltpu.VMEM((tm, GLA_VW), F32), pltpu.VMEM((tm, GLA_VW), F32),
            pltpu.VMEM((GLA_KW, GLA_VW), F32),
            pltpu.VMEM((GLA_SUB * GLA_CHUNK, GLA_KW), BF16),
            pltpu.VMEM((tm, GLA_VW), F32), pltpu.VMEM((tm, MEM_W), BF16),
        ],
        guests=make_guests(nb * nt),
        name="mixer_a_prompt",
    )


def _ffn_rows(x, gpre_ref, wup_ref, wdn_ref, gpost_ref, acc_ref, arrive=None):
    arrive = arrive if arrive is not None else (lambda which, j: None)
    h = _bf(_rms(x, gpre_ref[0]))
    nff = D_FF // FF_CHUNK
    for j in range(nff - 1):
        cols = slice(j * FF_CHUNK, (j + 1) * FF_CHUNK)
        arrive(0, j)
        u = jnp.maximum(_dot(h, _bf(wup_ref[:, cols])), 0.0)
        arrive(1, j)
        d = _dot(_bf(u * u), _bf(wdn_ref[cols, :]))
        if j == 0:
            acc_ref[...] = d
        else:
            acc_ref[...] += d
    cols = slice((nff - 1) * FF_CHUNK, nff * FF_CHUNK)
    arrive(0, nff - 1)
    u = jnp.maximum(_dot(h, _bf(wup_ref[:, cols])), 0.0)
    uu = _bf(u * u)
    arrive(1, nff - 1)
    wd = _bf(wdn_ref[cols, :])
    m = x.shape[0]
    bounds = (0, m) if m < 256 else (0, m // 2, m)
    parts = []
    for lo, hi in zip(bounds[:-1], bounds[1:]):
        tot = acc_ref[lo:hi, :] + _dot(uu[lo:hi], wd)
        parts.append(x[lo:hi] + _rms(tot, gpost_ref[0]))
    return parts[0] if len(parts) == 1 else jnp.concatenate(parts, axis=0)


def _ffn_kernel(xp_ref, xs_ref, gpre_ref, wup_hbm, wdn_hbm, gpost_ref, yp_ref, ys_ref, acc_s, wup_s, wdn_s, sems,
                *, layer):
    i = pl.program_id(0)
    last = pl.num_programs(0) - 1
    nff = D_FF // FF_CHUNK

    def copy(which, j):
        cols = slice(j * FF_CHUNK, (j + 1) * FF_CHUNK)
        if which == 0:
            return pltpu.make_async_copy(wup_hbm.at[layer, :, cols], wup_s.at[:, cols], sems.at[0, j])
        return pltpu.make_async_copy(wdn_hbm.at[layer, cols, :], wdn_s.at[cols, :], sems.at[1, j])

    @pl.when(i == 0)
    def _():
        for j in range(nff):
            copy(0, j).start()
            copy(1, j).start()
        yp_ref[...] = _ffn_rows(xp_ref[...], gpre_ref, wup_s, wdn_s, gpost_ref, acc_s,
                                arrive=lambda which, j: copy(which, j).wait())

    @pl.when((i > 0) & (i < last))
    def _():
        yp_ref[...] = _ffn_rows(xp_ref[...], gpre_ref, wup_s, wdn_s, gpost_ref, acc_s)

    @pl.when(i == last)
    def _():
        ns = xs_ref.shape[0]
        ys_ref[...] = _ffn_rows(xs_ref[...], gpre_ref, wup_s, wdn_s, gpost_ref, acc_s.at[0:ns, :])


def _ffn_call(xp, xs, norm_pre, w_up, w_down, norm_post, layer):
    n, ns = xp.shape[0], xs.shape[0]
    tm = TM_PROMPT
    nt = n // tm
    gains = [a.reshape(a.shape[0], 1, D) for a in (norm_pre, norm_post)]
    gspec = pl.BlockSpec((1, 1, D), lambda i: (layer, 0, 0))
    hbm = pl.BlockSpec(memory_space=pl.ANY)
    tile = pl.BlockSpec((tm, D), lambda i: (jnp.minimum(i, nt - 1), 0))
    return pl.pallas_call(
        functools.partial(_ffn_kernel, layer=layer),
        grid=(nt + 1,),
        in_specs=[tile, _full_spec(xs), gspec, hbm, hbm, gspec],
        out_specs=[tile, _full_spec(xs)],
        out_shape=[jax.ShapeDtypeStruct((n, D), F32), jax.ShapeDtypeStruct((ns, D), F32)],
        scratch_shapes=[pltpu.VMEM((tm, D), F32), pltpu.VMEM((D, D_FF), F32), pltpu.VMEM((D_FF, D), F32),
                        pltpu.SemaphoreType.DMA((2, D_FF // FF_CHUNK))],
        compiler_params=_params(("arbitrary",)),
        name="ffn",
    )(xp, xs, gains[0], w_up, w_down, gains[1])


def _t5_bucket(dist):
    max_exact = N_BUCKETS // 2
    n = np.maximum(dist, 0)
    nf = np.maximum(n, 1).astype(np.float32)
    large = max_exact + (np.log(nf / np.float32(max_exact)) / np.float32(math.log(MAX_DISTANCE / max_exact))
                         * np.float32(N_BUCKETS - max_exact)).astype(np.int32)
    large = np.minimum(large, N_BUCKETS - 1)
    return np.where(n < max_exact, n, large)


def _swa_bias_tables():
    qi = np.arange(WINDOW)[:, None] + WINDOW
    kj = np.arange(2 * WINDOW)[None, :]
    dist = qi - kj
    valid = (dist >= 0) & (dist < WINDOW)
    return np.where(valid, _t5_bucket(dist), -1).astype(np.int32)


def _mixer_b_kernel(rb_ref, sink_ref,
                    x_ref, bkt_ref, gkv_ref, wkv_ref, gpre_ref, wq_ref, wqm_ref,
                    kbd_ref, vbd_ref, wo1_ref, wo2_ref, gpost_ref,
                    xo_ref, kc_ref, vc_ref,
                    kbuf, vbuf, q_s, o_s, bias_s, qm_s, pre_hook, hook):
    W = WINDOW
    tm = x_ref.shape[1]
    bb = pl.program_id(0)
    t = pl.program_id(1)
    nheads = SWA_G * SWA_KVH

    @pl.when((bb == 0) & (t == 0))
    def _():
        bkt = bkt_ref[...]
        own = _iota((W, 2 * W), 1) >= W
        for i in range(nheads):
            def add_bucket(n, acc):
                return jnp.where(bkt == n, rb_ref[n, i], acc)
            tab = lax.fori_loop(0, N_BUCKETS, add_bucket, jnp.zeros((W, 2 * W), F32))
            tab = jnp.where(bkt < 0, NEG_INF, tab)
            bias_s[0, i] = tab
            bias_s[1, i] = jnp.where(own, tab, NEG_INF)

    @pl.when(t == 0)
    def _():
        kbuf[0:W, :] = jnp.zeros((W, SWA_KVW), F32)
        vbuf[0:W, :] = jnp.zeros((W, SWA_KVW), F32)

    pre_hook()
    x = x_ref[0]
    xn = x * lax.rsqrt(jnp.mean(x * x, axis=-1, keepdims=True) + EPS)
    kv = _dot(_bf(xn * gkv_ref[...]), wkv_ref[...])
    kbuf[W:W + tm, :] = kv[:, :SWA_KVW]
    vbuf[W:W + tm, :] = kv[:, SWA_KVW:]
    kc_ref[0] = kv[tm - W:, :SWA_KVW]
    vc_ref[0] = kv[tm - W:, SWA_KVW:]

    h = _bf(xn * gpre_ref[...])
    q_s[...] = _dot(h, wq_ref[...]) * (SWA_HD ** -0.5)
    qm_s[...] = _dot(h, wqm_ref[...])
    hook()

    lane_head = _iota((W, SWA_KVW), 1) >> 6
    key_head = _iota((2 * W, SWA_KVW), 1) >> 6

    def scores(j):
        r0 = pl.multiple_of(j * W, W)
        qrows = []
        for gi in range(SWA_G):
            qg = q_s[pl.ds(r0, W), gi * SWA_KVW:(gi + 1) * SWA_KVW]
            for hh in range(SWA_KVH):
                qrows.append(jnp.where(lane_head == hh, qg, 0.0))
        return _dot_nt(_bf(jnp.concatenate(qrows, axis=0)), _bf(kbuf[pl.ds(r0, 2 * W), :]))

    def probs(j, s_all, gi):
        tab = jnp.where((j == 0) & (t == 0), 1, 0)
        ps = []
        for hh in range(SWA_KVH):
            i = gi * SWA_KVH + hh
            s = s_all[i * W:(i + 1) * W, :] + bias_s[tab, i]
            sink = sink_ref[i]
            mx = jnp.maximum(jnp.max(s, axis=-1, keepdims=True), sink)
            p = jnp.exp(s - mx)
            p = p / (jnp.sum(p, axis=-1, keepdims=True) + jnp.exp(sink - mx))
            ps.append(_bf(p))
        return jnp.concatenate(ps, axis=1)

    def pair(jj, carry):
        js = [jj * SWA_BLOCKS_PER_TRIP + i for i in range(SWA_BLOCKS_PER_TRIP)]
        s_alls = [scores(j) for j in js]
        for j, s_all in zip(js, s_alls):
            r0 = pl.multiple_of(j * W, W)
            vb = vbuf[pl.ds(r0, 2 * W), :]
            vst = _bf(jnp.concatenate([jnp.where(key_head == hh, vb, 0.0) for hh in range(SWA_KVH)], axis=0))
            for gi in range(SWA_G):
                o_s[pl.ds(r0, W), gi * SWA_KVW:(gi + 1) * SWA_KVW] = _dot(probs(j, s_all, gi), vst)
        return carry

    lax.fori_loop(0, tm // (W * SWA_BLOCKS_PER_TRIP), pair, 0)

    kbuf[0:W, :] = kbuf[tm:tm + W, :]
    vbuf[0:W, :] = vbuf[tm:tm + W, :]

    o_mem = _mem_pv(_mem_probs(qm_s[...], kbd_ref.at[0]), vbd_ref.at[0])
    xo_ref[0] = _mix_residual(x_ref[0], o_s[...], o_mem, wo1_ref, wo2_ref, gpost_ref)


def _mixer_b_call(x, wb, kbd, vbd, make_guests):
    nb, seq, _ = x.shape
    tm = TM_PROMPT
    nt = seq // tm
    bkt = jnp.asarray(_swa_bias_tables())
    head = [bkt, wb["gkv"], wb["wkv"], wb["gpre"], wb["wq"], wb["wqm"]]
    tail = [wb["wo1"], wb["wo2"], wb["gpost"]]
    smem = pl.BlockSpec(memory_space=pltpu.SMEM)
    return _call_with_guests(
        _mixer_b_kernel,
        grid=(nb, nt),
        step_of=lambda b, t: b * nt + t,
        inputs=[wb["rb"], wb["sinks"], x, *head, kbd, vbd, *tail],
        in_specs=([smem, smem, pl.BlockSpec((1, tm, D), lambda b, t: (b, t, 0))]
                  + [_const_spec(w) for w in head]
                  + [pl.BlockSpec((1, MEM_W, MEM_H * N_MEM), lambda b, t: (b, 0, 0)),
                     pl.BlockSpec((1, MEM_H * N_MEM, MEM_W), lambda b, t: (b, 0, 0))]
                  + [_const_spec(w) for w in tail]),
        out_specs=[pl.BlockSpec((1, tm, D), lambda b, t: (b, t, 0)),
                   pl.BlockSpec((1, WINDOW, SWA_KVW), lambda b, t: (b, 0, 0)),
                   pl.BlockSpec((1, WINDOW, SWA_KVW), lambda b, t: (b, 0, 0))],
        out_shape=[jax.ShapeDtypeStruct((nb, seq, D), F32),
                   jax.ShapeDtypeStruct((nb, WINDOW, SWA_KVW), F32),
                   jax.ShapeDtypeStruct((nb, WINDOW, SWA_KVW), F32)],
        scratch=[
            pltpu.VMEM((tm + WINDOW, SWA_KVW), F32), pltpu.VMEM((tm + WINDOW, SWA_KVW), F32),
            pltpu.VMEM((tm, SWA_QW), F32), pltpu.VMEM((tm, SWA_QW), F32),
            pltpu.VMEM((2, SWA_G * SWA_KVH, WINDOW, 2 * WINDOW), F32),
            pltpu.VMEM((tm, MEM_W), F32),
        ],
        guests=make_guests(nb * nt),
        name="mixer_b_prompt",
    )


def _row_to_col(row, eye):
    return jnp.sum(jnp.where(eye, jnp.broadcast_to(row, eye.shape), 0.0), axis=1, keepdims=True)


def _eye(n):
    return _iota((n, n), 0) == _iota((n, n), 1)


def _pre_a_kernel(x_ref, gpre_ref, wqk_ref, wv_ref, wr_ref, wmisc_ref, wg_ref, bg_ref, glan_ref,
                  qkgv_ref, gate_ref, qm_ref):
    q, k, g, v, gate, qm = _proj_a(x_ref[...], gpre_ref, wqk_ref, wv_ref, wr_ref, wmisc_ref, wg_ref, bg_ref,
                                   glan_ref)
    qkgv_ref[:, 0:GLA_KW] = q
    qkgv_ref[:, GLA_KW:2 * GLA_KW] = k
    qkgv_ref[:, 2 * GLA_KW:3 * GLA_KW] = g
    qkgv_ref[:, 3 * GLA_KW:] = v
    gate_ref[...] = gate
    qm_ref[...] = qm


def _pre_a_call(x2d, wa):
    n = x2d.shape[0]
    ws = [wa["gpre"], wa["wqk"], wa["wv"], wa["wr"], wa["wmisc"], wa["wg"], wa["bg"], wa["glan"]]
    shapes = [(n, 3 * GLA_KW + GLA_VW), (n, GLA_VW), (n, MEM_W)]
    return pl.pallas_call(
        _pre_a_kernel,
        grid=(1,),
        in_specs=[_full_spec(x2d)] + [_full_spec(w) for w in ws],
        out_specs=[pl.BlockSpec(s, lambda i: (0, 0)) for s in shapes],
        out_shape=[jax.ShapeDtypeStruct(s, F32) for s in shapes],
        compiler_params=_params(("arbitrary",)),
        name="pre_a_sample",
    )(x2d, *ws)


def _gla_step_guest(qkgv_t, state5, nsteps):
    n = qkgv_t.shape[1]
    nblk = GLA_DK // GLA_DK_BLOCK
    nact = GLA_H * nblk
    assert nact <= nsteps
    act = lambda s: jnp.minimum(s, nact - 1)

    def init(step, qt_ref, kt_ref, gt_ref, vt_ref, s_ref, so_ref, ot_ref):
        @pl.when((step < nact) & (step % nblk == 0))
        def _():
            ot_ref[...] = jnp.zeros_like(ot_ref)

    def post(step, _, qt_ref, kt_ref, gt_ref, vt_ref, s_ref, so_ref, ot_ref):
        vt_blk = vt_ref[...]
        acc = jnp.zeros_like(vt_blk)
        for d in range(GLA_DK_BLOCK):
            s_new = jnp.exp(gt_ref[d:d + 1, :]) * s_ref[0, 0, d] + kt_ref[d:d + 1, :] * vt_blk
            so_ref[0, 0, d] = s_new
            acc = acc + qt_ref[d:d + 1, :] * s_new
        ot_ref[...] += jnp.where(step < nact, acc, 0.0)

    rows = lambda part: ((GLA_DK_BLOCK, n), lambda s: (part * nact + act(s), 0))
    vrow = ((GLA_DV, n), lambda s: (3 * GLA_KW // GLA_DV + act(s) // nblk, 0))
    head = ((GLA_DV, n), lambda s: (act(s) // nblk, 0))
    st = ((1, 1, GLA_DK_BLOCK, GLA_DV, n), lambda s: (0, act(s) // nblk, act(s) % nblk, 0, 0))
    return dict(inputs=[qkgv_t, qkgv_t, qkgv_t, qkgv_t, state5], in_specs=[rows(0), rows(1), rows(2), vrow, st],
                out_shape=[jax.ShapeDtypeStruct(state5.shape, F32), jax.ShapeDtypeStruct((GLA_VW, n), F32)],
                out_specs=[st, head], scratch=[], init=init, pre=lambda step, *refs: None, post=post)


def _mem_step_pre(step, qm_ref, mk_ref, mv_ref, o_ref):
    own = (_iota((8, MEM_W), 1) >> 6) == _iota((8, MEM_W), 0)
    ps = []
    for i in range(qm_ref.shape[0]):
        q8 = jnp.where(own, jnp.broadcast_to(qm_ref[i], (8, MEM_W)), 0.0)
        s = _dot(_bf(q8), _bf(mk_ref[0, i].reshape(MEM_W, N_MEM))) * (MEM_HD ** -0.5)
        e = jnp.exp(s - jnp.max(s, axis=1, keepdims=True))
        ps.append(_bf(e / jnp.sum(e, axis=1, keepdims=True)))
    return ps


def _mem_step_post(step, ps, qm_ref, mk_ref, mv_ref, o_ref):
    own = (_iota((8, MEM_W), 1) >> 6) == _iota((8, MEM_W), 0)
    for i, p in enumerate(ps):
        res = _dot_nt(p, _bf(mv_ref[0, i].reshape(MEM_W, N_MEM)))
        o_ref[i] = jnp.sum(jnp.where(own, res, 0.0), axis=0, keepdims=True)


def _mem_step_guest(qm, mk5, mv5, layer, nsteps):
    n = qm.shape[0]
    rb = n // nsteps
    blk = ((1, rb, MEM_H, MEM_HD, N_MEM), lambda s: (layer, s, 0, 0, 0))
    rows = ((rb, 1, MEM_W), lambda s: (s, 0, 0))
    return dict(inputs=[qm.reshape(n, 1, MEM_W), mk5, mv5], in_specs=[rows, blk, blk],
                out_shape=[jax.ShapeDtypeStruct((n, 1, MEM_W), F32)], out_specs=[rows], scratch=[],
                init=lambda step, *refs: None, pre=_mem_step_pre, post=_mem_step_post)


def _post_a_kernel(x_ref, o_ref, gate_ref, om_ref, wo1_ref, wo2_ref, gpost_ref, xo_ref):
    o_main = _gla_out_gate(o_ref[...], gate_ref[...])
    xo_ref[...] = _mix_residual(x_ref[...], o_main, om_ref[...], wo1_ref, wo2_ref, gpost_ref)


def _post_a_call(x2d, o, gate, om, wa):
    args = [x2d, o, gate, om, wa["wo1"], wa["wo2"], wa["gpost"]]
    return pl.pallas_call(
        _post_a_kernel,
        grid=(1,),
        in_specs=[_full_spec(a) for a in args],
        out_specs=_full_spec(x2d),
        out_shape=jax.ShapeDtypeStruct(x2d.shape, F32),
        compiler_params=_params(("arbitrary",)),
        name="post_a_sample",
    )(*args)


def _pre_b_kernel(x_ref, gkv_ref, wkv_ref, gpre_ref, wq_ref, wqm_ref, kv_ref, q_ref, qm_ref):
    x = x_ref[...]
    kv_ref[...] = _dot(_bf(_rms(x, gkv_ref[...])), wkv_ref[...])
    h = _bf(_rms(x, gpre_ref[...]))
    q_ref[...] = _dot(h, wq_ref[...])
    qm_ref[...] = _dot(h, wqm_ref[...])


def _pre_b_call(x2d, wb):
    n = x2d.shape[0]
    ws = [wb["gkv"], wb["wkv"], wb["gpre"], wb["wq"], wb["wqm"]]
    widths = [2 * SWA_KVW, SWA_QW, MEM_W]
    return pl.pallas_call(
        _pre_b_kernel,
        grid=(1,),
        in_specs=[_full_spec(x2d)] + [_full_spec(w) for w in ws],
        out_specs=[pl.BlockSpec((n, w), lambda i: (0, 0)) for w in widths],
        out_shape=[jax.ShapeDtypeStruct((n, w), F32) for w in widths],
        compiler_params=_params(("arbitrary",)),
        name="pre_b_sample",
    )(x2d, *ws)


def _sample_buckets():
    dist = (WINDOW - 1) - np.arange(WINDOW)
    return _t5_bucket(dist).astype(np.int32).reshape(1, WINDOW)


def _swa_step_init(step, sink_ref, rb_ref, bkt_ref, kc_ref, vc_ref, kst_ref, vst_ref, q_ref,
                   kn_ref, vn_ref, o_ref, bias_s, sink_s, s_s, p_s):
    W = WINDOW
    R = SWA_KVH * 8

    @pl.when(step == 0)
    def _():
        bkt = bkt_ref[...]
        rid = _iota((R, W), 0)
        bias = jnp.zeros((R, W), F32)
        sink = jnp.zeros((R, W), F32)
        for h in range(SWA_KVH):
            for g in range(SWA_G):
                idx = g * SWA_KVH + h
                def add_bucket(n, acc):
                    return jnp.where(bkt == n, rb_ref[n, idx], acc)
                brow = lax.fori_loop(0, N_BUCKETS, add_bucket, jnp.zeros((1, W), F32))
                bias = jnp.where(rid == h * 8 + g, brow, bias)
                sink = jnp.where(rid == h * 8 + g, sink_ref[idx], sink)
        bias_s[...] = bias
        sink_s[...] = sink


def _swa_step_pre(step, sink_ref, rb_ref, bkt_ref, kc_ref, vc_ref, kst_ref, vst_ref, q_ref,
                  kn_ref, vn_ref, o_ref, bias_s, sink_s, s_s, p_s):
    W = WINDOW
    R = SWA_KVH * 8
    rb = q_ref.shape[0]
    base = step * rb
    last = _iota((SWA_KVW, W), 1) == W - 1
    own = (_iota((R, SWA_KVW), 1) >> 6) == (_iota((R, SWA_KVW), 0) >> 3)
    kst = kst_ref[...]
    vst = vst_ref[...]
    for i in range(rb):
        shift = W - 1 - (base + i)
        kn = jnp.where(last, pltpu.roll(kst, shift, 1), pltpu.roll(kc_ref[i].reshape(SWA_KVW, W), W - 1, 1))
        vn = jnp.where(last, pltpu.roll(vst, shift, 1), pltpu.roll(vc_ref[i].reshape(SWA_KVW, W), W - 1, 1))
        kn_ref[i] = kn.reshape(SWA_KVH, SWA_HD, W)
        vn_ref[i] = vn.reshape(SWA_KVH, SWA_HD, W)
        q32 = jnp.where(own, jnp.concatenate([q_ref[i]] * SWA_KVH, axis=0), 0.0)
        s_s[i * R:(i + 1) * R, :] = _dot(_bf(q32), _bf(kn))
    s = s_s[...] * (SWA_HD ** -0.5) + jnp.concatenate([bias_s[...]] * rb, axis=0)
    sink = jnp.concatenate([sink_s[...]] * rb, axis=0)
    mx = jnp.maximum(jnp.max(s, axis=1, keepdims=True), sink)
    p = jnp.exp(s - mx)
    p_s[...] = _bf(p / (jnp.sum(p, axis=1, keepdims=True) + jnp.exp(sink - mx)))


def _swa_step_post(step, _, sink_ref, rb_ref, bkt_ref, kc_ref, vc_ref, kst_ref, vst_ref, q_ref,
                   kn_ref, vn_ref, o_ref, bias_s, sink_s, s_s, p_s):
    W = WINDOW
    R = SWA_KVH * 8
    own = (_iota((R, SWA_KVW), 1) >> 6) == (_iota((R, SWA_KVW), 0) >> 3)
    for i in range(q_ref.shape[0]):
        vn = vn_ref[i].reshape(SWA_KVW, W)
        res = jnp.where(own, _dot_nt(p_s[i * R:(i + 1) * R, :], _bf(vn)), 0.0)
        o_ref[i] = res[0:8] + res[8:16] + res[16:24] + res[24:32]


def _swa_step_guest(kc4, vc4, kv_t, q, wb, nsteps):
    n = q.shape[0]
    rb = n // nsteps
    bkt = jnp.asarray(_sample_buckets())
    q8 = jnp.pad(q.reshape(n, SWA_G, SWA_KVW), ((0, 0), (0, 8 - SWA_G), (0, 0)))
    row3 = lambda r, w: ((rb, r, w), lambda s: (s, 0, 0))
    cache = ((rb, SWA_KVH, SWA_HD, WINDOW), lambda s: (s, 0, 0, 0))
    half = lambda i: ((SWA_KVW, n), lambda s: (i, 0))
    smem = pl.BlockSpec(memory_space=pltpu.SMEM)
    return dict(inputs=[wb["sinks"], wb["rb"], bkt, kc4, vc4, kv_t, kv_t, q8],
                in_specs=[smem, smem, _full_spec(bkt), cache, cache, half(0), half(1), row3(8, SWA_KVW)],
                out_shape=[jax.ShapeDtypeStruct(kc4.shape, F32), jax.ShapeDtypeStruct(vc4.shape, F32),
                           jax.ShapeDtypeStruct((n, 8, SWA_KVW), F32)],
                out_specs=[cache, cache, row3(8, SWA_KVW)],
                scratch=[pltpu.VMEM((SWA_KVH * 8, WINDOW), F32), pltpu.VMEM((SWA_KVH * 8, WINDOW), F32),
                         pltpu.VMEM((rb * SWA_KVH * 8, WINDOW), F32),
                         pltpu.VMEM((rb * SWA_KVH * 8, WINDOW), BF16)],
                init=_swa_step_init, pre=_swa_step_pre, post=_swa_step_post)


def _post_b_kernel(x_ref, o_ref, om_ref, wo1_ref, wo2_ref, gpost_ref, xo_ref):
    xo_ref[...] = _mix_residual(x_ref[...], o_ref[...], om_ref[...], wo1_ref, wo2_ref, gpost_ref)


def _post_b_call(x2d, o, om, wb):
    args = [x2d, o, om, wb["wo1"], wb["wo2"], wb["gpost"]]
    return pl.pallas_call(
        _post_b_kernel,
        grid=(1,),
        in_specs=[_full_spec(a) for a in args],
        out_specs=_full_spec(x2d),
        out_shape=jax.ShapeDtypeStruct(x2d.shape, F32),
        compiler_params=_params(("arbitrary",)),
        name="post_b_sample",
    )(*args)


def _prep_weights(norm_mix_pre, norm_mix_post, w_in_a, w_gate_up, b_gate,
                  gla_norm, w_in_b, sinks, norm_kv, w_kv, rel_bias, w_out):
    row = lambda g: g.reshape(1, -1)
    wa_in = w_in_a[0]
    c_v = 2 * GLA_KW
    c_r = c_v + GLA_VW
    c_g = c_r + GLA_VW
    c_m = c_g + GATE_RANK
    wmisc = jnp.concatenate([wa_in[:, c_g:c_m], jnp.zeros((D, 128 - GATE_RANK), F32), wa_in[:, c_m:]], axis=1)
    wg = jnp.zeros((128, GLA_KW), F32).at[:GATE_RANK].set(w_gate_up[0])
    wa = dict(
        gpre=row(norm_mix_pre[0]), wqk=_bf(wa_in[:, :c_v]), wv=_bf(wa_in[:, c_v:c_r]), wr=_bf(wa_in[:, c_r:c_g]),
        wmisc=_bf(wmisc), wg=_bf(wg), bg=row(b_gate[0]), glan=row(jnp.tile(gla_norm[0], GLA_H)),
        wo1=_bf(w_out[0][:GLA_VW]), wo2=_bf(w_out[0][GLA_VW:]), gpost=row(norm_mix_post[0]))
    wb_in = w_in_b[0]
    wq = wb_in[:, :SWA_QW].reshape(D, SWA_KVH, SWA_G, SWA_HD).transpose(0, 2, 1, 3).reshape(D, SWA_QW)
    wo1 = w_out[1][:SWA_QW].reshape(SWA_KVH, SWA_G, SWA_HD, D).transpose(1, 0, 2, 3).reshape(SWA_QW, D)
    rb = rel_bias.reshape(N_BUCKETS, SWA_KVH, SWA_G).transpose(0, 2, 1).reshape(N_BUCKETS, SWA_G * SWA_KVH)
    sk = sinks[0].reshape(SWA_KVH, SWA_G).T.reshape(SWA_G * SWA_KVH)
    wb = dict(
        gkv=row(norm_kv), wkv=_bf(w_kv), gpre=row(norm_mix_pre[1]), wq=_bf(wq), wqm=_bf(wb_in[:, SWA_QW:]),
        wo1=_bf(wo1), wo2=_bf(w_out[1][SWA_QW:]), gpost=row(norm_mix_post[1]), rb=rb, sinks=sk)
    return wa, wb


def kernel(x_prompt, x_sample, state_gla, cache_swa_k, cache_swa_v, cache_mem_k, cache_mem_v, mem_prompt,
           norm_mix_pre, norm_mix_post, norm_ffn_pre, norm_ffn_post, norm_mem, w_mem_kv, w_in_a, w_gate_up,
           b_gate, gla_norm, w_in_b, sinks, norm_kv, w_kv, rel_bias, w_out, w_ffn_up, w_ffn_down):
    wa, wb = _prep_weights(norm_mix_pre, norm_mix_post, w_in_a, w_gate_up, b_gate, gla_norm, w_in_b, sinks,
                           norm_kv, w_kv, rel_bias, w_out)
    ffn = lambda xp, xs_, l: _ffn_call(xp, xs_, norm_ffn_pre, w_ffn_up, w_ffn_down, norm_ffn_post, l)
    nb, seq, _ = x_prompt.shape
    ns = x_sample.shape[0]

    xs = x_sample.reshape(ns, D)
    state5 = jnp.transpose(state_gla, (0, 2, 3, 4, 1))
    mk5 = jnp.transpose(cache_mem_k, (0, 1, 3, 4, 2))
    mv5 = jnp.transpose(cache_mem_v, (0, 1, 3, 4, 2))
    kc4 = jnp.transpose(cache_swa_k, (0, 2, 3, 1))
    vc4 = jnp.transpose(cache_swa_v, (0, 2, 3, 1))

    mkt, mvt, kbd, vbd = _memkv_call(mem_prompt, norm_mem, w_mem_kv)
    qkgv, gate, qm = _pre_a_call(xs, wa)
    qkgv_t = qkgv.T
    (x1, st), ((state5_new, ot), (om,)) = _mixer_a_call(
        x_prompt, wa, kbd[0], vbd[0],
        lambda nsteps: [_gla_step_guest(qkgv_t, state5, nsteps),
                        _mem_step_guest(qm, mk5, mv5, 0, nsteps)])
    xs1 = _post_a_call(xs, ot.T, gate, om.reshape(ns, MEM_W), wa)
    x2, xs2 = ffn(x1.reshape(nb * seq, D), xs1, 0)
    kv, qb, qmb = _pre_b_call(xs2, wb)
    kv_t = kv.T
    (x3, kc, vc), ((kn4, vn4, o8), (omb,)) = _mixer_b_call(
        x2.reshape(nb, seq, D), wb, kbd[1], vbd[1],
        lambda nsteps: [_swa_step_guest(kc4, vc4, kv_t, qb, wb, nsteps),
                        _mem_step_guest(qmb, mk5, mv5, 1, nsteps)])
    xs3 = _post_b_call(xs2, o8[:, :SWA_G].reshape(ns, SWA_QW), omb.reshape(ns, MEM_W), wb)
    y_prompt, y_sample = ffn(x3.reshape(nb * seq, D), xs3, 1)
    y_prompt = y_prompt.reshape(nb, seq, D)
    y_sample = y_sample.reshape(ns, 1, D)
    st4 = st.reshape(nb, GLA_H, GLA_DK, GLA_H, GLA_DV)
    state_prompt = jnp.stack([st4[:, h, :, h, :] for h in range(GLA_H)], axis=1)[None]
    to_mem = lambda t: t.reshape(2, nb, MEM_H, MEM_HD, N_MEM).transpose(0, 1, 4, 2, 3)
    swa_shape = (nb, WINDOW, SWA_KVH, SWA_HD)

    return (y_prompt, y_sample, state_prompt,
            jnp.transpose(state5_new, (0, 4, 1, 2, 3)),
            kc.reshape(swa_shape), vc.reshape(swa_shape),
            jnp.transpose(kn4, (0, 3, 1, 2)), jnp.transpose(vn4, (0, 3, 1, 2)),
            to_mem(mkt), to_mem(mvt))
```

```python
import functools
import math

import numpy as np
import jax
import jax.numpy as jnp
from jax import lax
from jax.experimental import pallas as pl
from jax.experimental.pallas import tpu as pltpu

F32 = jnp.float32
BF16 = jnp.bfloat16

D = 1024
D_FF = 4 * D
N_MEM = 256
MEM_H = 4
MEM_HD = 64
MEM_W = MEM_H * MEM_HD
GLA_H = 4
GLA_DK = 96
GLA_DV = 192
GLA_KW = GLA_H * GLA_DK
GLA_VW = GLA_H * GLA_DV
GATE_RANK = 16
GLA_COL_V = 2 * GLA_KW
GLA_COL_R = GLA_COL_V + GLA_VW
GLA_COL_GATE = GLA_COL_R + GLA_VW
GATE_NORM = 16.0
SWA_HD = 64
SWA_KVH = 4
SWA_G = 3
SWA_QW = SWA_KVH * SWA_G * SWA_HD
SWA_KVW = SWA_KVH * SWA_HD
WINDOW = 128
N_BUCKETS = 32
MAX_DISTANCE = 128
EPS = 1e-6

GLA_CHUNK = 64
GLA_SUB = 16
GLA_SAFE_DECAY = 60.0
GLA_GROUP = 4
SWA_BLOCKS_PER_TRIP = 2
TM_PROMPT = 512
FF_CHUNK = 512
GLA_DK_BLOCK = 16
V7X_VMEM_LIMIT = 56 * 1024 * 1024
NEG_INF = float("-inf")


def _bf(x):
    return x.astype(BF16)


def _dot(a, b):
    return jnp.dot(a, b, preferred_element_type=F32)


def _dot_nt(a, b):
    return lax.dot_general(a, b, (((1,), (1,)), ((), ())), preferred_element_type=F32)


def _dot_tn(a, b):
    return lax.dot_general(a, b, (((0,), (0,)), ((), ())), preferred_element_type=F32)


def _rms(x, g):
    return x * lax.rsqrt(jnp.mean(x * x, axis=-1, keepdims=True) + EPS) * g


def _split3(x):
    x1 = _bf(x)
    r1 = x - x1.astype(F32)
    x2 = _bf(r1)
    x3 = _bf(r1 - x2.astype(F32))
    return x1, x2, x3


def _exact_dot(sel, x):
    x1, x2, x3 = _split3(x)
    return _dot(sel, x1) + _dot(sel, x2) + _dot(sel, x3)


def _log_sigmoid(z):
    return jnp.minimum(z, 0.0) - jnp.log1p(jnp.exp(-jnp.abs(z)))


def _silu(z):
    return z * (1.0 / (1.0 + jnp.exp(-z)))


def _iota(shape, dim):
    return lax.broadcasted_iota(jnp.int32, shape, dim)


def _gla_k_head(lane):
    one = jnp.int32(1)
    zero = jnp.int32(0)
    return (jnp.where(lane >= GLA_DK, one, zero) + jnp.where(lane >= 2 * GLA_DK, one, zero)
            + jnp.where(lane >= 3 * GLA_DK, one, zero))


def _gla_v_head(lane):
    one = jnp.int32(1)
    zero = jnp.int32(0)
    return (jnp.where(lane >= GLA_DV, one, zero) + jnp.where(lane >= 2 * GLA_DV, one, zero)
            + jnp.where(lane >= 3 * GLA_DV, one, zero))


def _full_spec(a):
    nd = a.ndim
    return pl.BlockSpec(a.shape, lambda *_: (0,) * nd)


def _mem_bd_specs(layer):
    return [pl.BlockSpec((None, 1, MEM_W, MEM_H * N_MEM), lambda b, t: (layer, b, 0, 0)),
            pl.BlockSpec((None, 1, MEM_H * N_MEM, MEM_W), lambda b, t: (layer, b, 0, 0))]


def _params(sem):
    return pltpu.CompilerParams(dimension_semantics=sem, vmem_limit_bytes=V7X_VMEM_LIMIT)


def _call_with_guests(host_kernel, grid, step_of, inputs, in_specs, out_shape, out_specs, scratch, guests, name):
    def to_spec(s):
        if isinstance(s, pl.BlockSpec):
            return s
        block, fn = s
        return pl.BlockSpec(block, lambda *ids: fn(step_of(*ids)))

    counts = [(len(inputs), len(out_shape), len(scratch))]
    counts += [(len(g["inputs"]), len(g["out_shape"]), len(g["scratch"])) for g in guests]

    def kern(*refs):
        refs = list(refs)
        parts = [[refs.pop(0) for _ in range(c[k])] for k in range(3) for c in counts]
        n = len(counts)
        ins, outs, scrs = parts[:n], parts[n:2 * n], parts[2 * n:]
        step = step_of(*[pl.program_id(a) for a in range(len(grid))])
        grefs = [(*a, *b, *c) for a, b, c in zip(ins[1:], outs[1:], scrs[1:])]
        states = []

        def pre_hook():
            for g, r in zip(guests, grefs):
                g["init"](step, *r)
            states.extend(g["pre"](step, *r) for g, r in zip(guests, grefs))

        def hook():
            for g, st, r in zip(guests, states, grefs):
                g["post"](step, st, *r)

        host_kernel(*ins[0], *outs[0], *scrs[0], pre_hook=pre_hook, hook=hook)

    res = pl.pallas_call(
        kern,
        grid=grid,
        in_specs=list(in_specs) + [to_spec(s) for g in guests for s in g["in_specs"]],
        out_specs=list(out_specs) + [to_spec(s) for g in guests for s in g["out_specs"]],
        out_shape=list(out_shape) + [s for g in guests for s in g["out_shape"]],
        scratch_shapes=list(scratch) + [s for g in guests for s in g["scratch"]],
        compiler_params=_params(("arbitrary",) * len(grid)),
        name=name,
    )(*inputs, *[a for g in guests for a in g["inputs"]])
    res = list(res)
    split = []
    for c in counts:
        split.append([res.pop(0) for _ in range(c[1])])
    return split[0], split[1:]


def _mem_probs(qm, kbd_ref):
    s = _dot(_bf(qm), kbd_ref[...]) * (MEM_HD ** -0.5)
    ps = []
    for h in range(MEM_H):
        sh = s[:, h * N_MEM:(h + 1) * N_MEM]
        e = jnp.exp(sh - jnp.max(sh, axis=-1, keepdims=True))
        ps.append(_bf(e / jnp.sum(e, axis=-1, keepdims=True)))
    return ps


def _mem_pv(ps, vbd_ref):
    out = None
    for h in range(MEM_H):
        t = _dot(ps[h], vbd_ref[h * N_MEM:(h + 1) * N_MEM, :])
        out = t if out is None else out + t
    return out


def _memkv_kernel(mem_ref, g_ref, w_ref, k_ref, v_ref, kbd_ref, vbd_ref):
    h = _bf(_rms(mem_ref[0], g_ref[0]))
    kv = _dot(h, _bf(w_ref[0]))
    k = kv[:, :MEM_W]
    v = kv[:, MEM_W:]
    kt = k.T
    k_ref[0, 0] = kt
    v_ref[0, 0] = v.T
    kt4 = jnp.concatenate([kt, kt, kt, kt], axis=1)
    keep_k = (_iota((MEM_W, MEM_H * N_MEM), 0) >> 6) == (_iota((MEM_W, MEM_H * N_MEM), 1) >> 8)
    kbd_ref[0, 0] = _bf(jnp.where(keep_k, kt4, 0.0))
    v4 = jnp.concatenate([v, v, v, v], axis=0)
    keep_v = (_iota((MEM_H * N_MEM, MEM_W), 0) >> 8) == (_iota((MEM_H * N_MEM, MEM_W), 1) >> 6)
    vbd_ref[0, 0] = _bf(jnp.where(keep_v, v4, 0.0))


def _memkv_call(mem, norm_mem, w_mem_kv):
    nb = mem.shape[0]
    nl = w_mem_kv.shape[0]
    g = norm_mem.reshape(nl, 1, D)
    return pl.pallas_call(
        _memkv_kernel,
        grid=(nl, nb),
        in_specs=[
            pl.BlockSpec((1, N_MEM, D), lambda l, b: (b, 0, 0)),
            pl.BlockSpec((1, 1, D), lambda l, b: (l, 0, 0)),
            pl.BlockSpec((1, D, 2 * MEM_W), lambda l, b: (l, 0, 0)),
        ],
        out_specs=[
            pl.BlockSpec((1, 1, MEM_W, N_MEM), lambda l, b: (l, b, 0, 0)),
            pl.BlockSpec((1, 1, MEM_W, N_MEM), lambda l, b: (l, b, 0, 0)),
            pl.BlockSpec((1, 1, MEM_W, MEM_H * N_MEM), lambda l, b: (l, b, 0, 0)),
            pl.BlockSpec((1, 1, MEM_H * N_MEM, MEM_W), lambda l, b: (l, b, 0, 0)),
        ],
        out_shape=[
            jax.ShapeDtypeStruct((nl, nb, MEM_W, N_MEM), F32),
            jax.ShapeDtypeStruct((nl, nb, MEM_W, N_MEM), F32),
            jax.ShapeDtypeStruct((nl, nb, MEM_W, MEM_H * N_MEM), BF16),
            jax.ShapeDtypeStruct((nl, nb, MEM_H * N_MEM, MEM_W), BF16),
        ],
        compiler_params=_params(("arbitrary", "arbitrary")),
        name="mem_kv",
    )(mem, g, w_mem_kv)


def _proj_a(x, gpre_ref, win_ref, wmisc_ref, wg_ref, bg_ref, glan_ref, mem_refs=None):
    h = _bf(_rms(x, gpre_ref[...]))
    misc = _dot(h, wmisc_ref[...])
    glr = misc[:, :128]
    qm = misc[:, 128:]
    ps = None if mem_refs is None else _mem_probs(qm, mem_refs[0])
    g = _log_sigmoid(_dot(_bf(glr), wg_ref[...]) + bg_ref[...]) * (1.0 / GATE_NORM)
    r = _dot(h, win_ref[:, GLA_COL_R:GLA_COL_GATE])
    gate = glan_ref[...] * _silu(r)
    qk = _dot(h, win_ref[:, :GLA_COL_V])
    q = qk[:, :GLA_KW] * (GLA_DK ** -0.5)
    k = qk[:, GLA_KW:]
    v = _dot(h, win_ref[:, GLA_COL_V:GLA_COL_R])
    return q, k, g, v, gate, (qm if mem_refs is None else _mem_pv(ps, mem_refs[1]))


def _gla_out_gate(o, gate):
    vh = _gla_v_head(_iota(o.shape, 1))
    o2 = o * o
    scale = jnp.zeros_like(o)
    for h in range(GLA_H):
        ss = jnp.sum(jnp.where(vh == h, o2, 0.0), axis=-1, keepdims=True) * (1.0 / GLA_DV)
        scale = jnp.where(vh == h, lax.rsqrt(ss + EPS), scale)
    return o * scale * gate


def _mix_residual(x, o_main, o_mem, wo1_ref, wo2_ref, gpost_ref):
    mix = _dot(_bf(o_main), wo1_ref[...]) + _dot(_bf(o_mem), wo2_ref[...])
    return x + _rms(mix, gpost_ref[...])


def _make_gla_chunk(q_s, k_s, g_s, v_s, o_s, state_s, p_s):
    C = GLA_CHUNK

    ri = _iota((C, GLA_KW), 0)
    kh = _gla_k_head(_iota((C, GLA_KW), 1))
    khcat = jnp.concatenate([kh, kh, kh], axis=1)
    vh = _gla_v_head(_iota((C, GLA_VW), 1))
    tri = _bf(jnp.where(_iota((C, C), 0) >= _iota((C, C), 1), 1.0, 0.0))
    d_rs = _iota((C, 4 * C), 0) - (_iota((C, 4 * C), 1) & (C - 1))
    band = jnp.where((d_rs >= 0) & (d_rs <= (_iota((C, 4 * C), 0) & (GLA_SUB - 1))), d_rs, -1)
    ebc = _bf(jnp.where(_gla_k_head(_iota((GLA_KW, 4 * C), 0)) == (_iota((GLA_KW, 4 * C), 1) >> 6),
                        1.0, 0.0))
    blockmask = _gla_k_head(_iota((GLA_KW, GLA_VW), 0)) == _gla_v_head(_iota((GLA_KW, GLA_VW), 1))
    eye = _iota((GLA_KW, GLA_KW), 0) == _iota((GLA_KW, GLA_KW), 1)

    def chunk(c, carry):
        r0 = pl.multiple_of(c * C, C)
        qc = q_s[pl.ds(r0, C), :]
        kc = k_s[pl.ds(r0, C), :]
        gc = g_s[pl.ds(r0, C), :]
        vc = v_s[pl.ds(r0, C), :]
        b = _exact_dot(tri, gc)

        ref1 = jnp.broadcast_to(b[31:32, :], b.shape)
        ref2 = jnp.where(ri < 32, jnp.broadcast_to(b[15:16, :], b.shape),
                         jnp.broadcast_to(b[47:48, :], b.shape))
        q1 = qc * jnp.exp(jnp.minimum(b - ref1, 0.0))
        k1 = kc * jnp.exp(jnp.minimum(ref1 - b, 0.0))
        q2 = qc * jnp.exp(jnp.minimum(b - ref2, 0.0))
        k2 = kc * jnp.exp(jnp.minimum(ref2 - b, 0.0))
        qcat = jnp.concatenate([
            jnp.where(ri >= 32, q1, 0.0),
            jnp.where((ri >= 16) & (ri < 32), q2, 0.0),
            jnp.where(ri >= 48, q2, 0.0)], axis=1)
        kcat = jnp.concatenate([
            jnp.where(ri < 32, k1, 0.0),
            jnp.where(ri < 16, k2, 0.0),
            jnp.where((ri >= 32) & (ri < 48), k2, 0.0)], axis=1)
        kst = _bf(jnp.concatenate([jnp.where(khcat == h, kcat, 0.0) for h in range(GLA_H)], axis=0))
        a_off = _dot_nt(_bf(qcat), kst)

        for dlt in range(GLA_SUB):
            if dlt == 0:
                pr = qc * kc
            else:
                kd = pltpu.roll(kc, dlt, 0)
                bd = pltpu.roll(b, dlt, 0)
                pr = qc * kd * jnp.exp(jnp.minimum(b - bd, 0.0))
            p_s[dlt * C:(dlt + 1) * C, :] = _bf(pr)
        rsum = _dot(p_s[...], ebc)
        a_diag = jnp.zeros((C, 4 * C), F32)
        for dlt in range(GLA_SUB):
            a_diag = jnp.where(band == dlt, rsum[dlt * C:(dlt + 1) * C, :], a_diag)

        vst = _bf(jnp.concatenate([jnp.where(vh == h, vc, 0.0) for h in range(GLA_H)], axis=0))
        o_intra = _dot(_bf(a_off + a_diag), vst)
        st = state_s[...]
        o_inter = _dot(_bf(qc * jnp.exp(b)), _bf(st))
        o_s[pl.ds(r0, C), :] = o_intra + o_inter

        blast = jnp.broadcast_to(b[C - 1:C, :], b.shape)
        kdec = kc * jnp.exp(blast - b)
        kv = _dot_tn(_bf(kdec), _bf(vc))
        decay = _row_to_col(jnp.exp(b[C - 1:C, :]), eye)
        state_s[...] = st * decay + jnp.where(blockmask, kv, 0.0)
        return carry

    return chunk


def _make_gla_chunk_bounded(q_s, k_s, g_s, v_s, o_s, state_s):
    C = GLA_CHUNK
    kh = _gla_k_head(_iota((C, GLA_KW), 1))
    vh = _gla_v_head(_iota((C, GLA_VW), 1))
    tri = _bf(jnp.where(_iota((C, C), 0) >= _iota((C, C), 1), 1.0, 0.0))
    causal = _iota((C, 4 * C), 0) >= (_iota((C, 4 * C), 1) & (C - 1))
    blockmask = _gla_k_head(_iota((GLA_KW, GLA_VW), 0)) == _gla_v_head(_iota((GLA_KW, GLA_VW), 1))
    eye = _iota((GLA_KW, GLA_KW), 0) == _iota((GLA_KW, GLA_KW), 1)

    def group(gi, carry):
        base = gi * (GLA_GROUP * C)
        rows = [pl.ds(pl.multiple_of(base + i * C, C), C) for i in range(GLA_GROUP)]
        bs = [_exact_dot(tri, g_s[r, :]) for r in rows]
        qes = [_bf(q_s[r, :] * jnp.exp(b)) for r, b in zip(rows, bs)]
        kes = [k_s[r, :] * jnp.exp(-b) for r, b in zip(rows, bs)]
        ksts = [_bf(jnp.concatenate([jnp.where(kh == h, ke, 0.0) for h in range(GLA_H)], axis=0)) for ke in kes]
        attn = [_bf(jnp.where(causal, _dot_nt(qe, kst), 0.0)) for qe, kst in zip(qes, ksts)]
        vcs = [v_s[r, :] for r in rows]
        vsts = [_bf(jnp.concatenate([jnp.where(vh == h, vc, 0.0) for h in range(GLA_H)], axis=0)) for vc in vcs]
        o_intra = [_dot(a, vst) for a, vst in zip(attn, vsts)]
        decays = [jnp.exp(b[C - 1:C, :]) for b in bs]
        kvs = [_dot_tn(_bf(ke * d), _bf(vc)) for vc, ke, d in zip(vcs, kes, decays)]
        dcols = [_row_to_col(d, eye) for d in decays]
        st = state_s[...]
        for i in range(GLA_GROUP):
            o_s[rows[i], :] = o_intra[i] + _dot(qes[i], _bf(st))
            st = st * dcols[i] + jnp.where(blockmask, kvs[i], 0.0)
        state_s[...] = st
        return carry

    return group


def _mixer_a_kernel(x_ref, gpre_ref, win_ref, wmisc_ref, wg_ref, bg_ref, glan_ref,
                    kbd_ref, vbd_ref, wo1_ref, wo2_ref, gpost_ref,
                    xo_ref, st_ref,
                    q_s, k_s, g_s, v_s, o_s, state_s, p_s, gate_s, om_s, pre_hook, hook):
    C = GLA_CHUNK
    tm = x_ref.shape[1]
    nchunk = tm // C

    @pl.when(pl.program_id(1) == 0)
    def _():
        state_s[...] = jnp.zeros_like(state_s)

    pre_hook()

    q, k, g, v, gate, o_mem = _proj_a(x_ref[0], gpre_ref, win_ref, wmisc_ref, wg_ref, bg_ref,
                                      glan_ref, (kbd_ref.at[0], vbd_ref.at[0]))
    q_s[...] = q
    k_s[...] = k
    g_s[...] = g
    v_s[...] = v
    gate_s[...] = gate
    om_s[...] = _bf(o_mem)
    hook()

    total = jnp.sum(g.reshape(nchunk, C, GLA_KW), axis=1)
    bounded = jnp.min(total) > -GLA_SAFE_DECAY

    @pl.when(bounded)
    def _():
        lax.fori_loop(0, nchunk // GLA_GROUP, _make_gla_chunk_bounded(q_s, k_s, g_s, v_s, o_s, state_s), 0)

    @pl.when(jnp.logical_not(bounded))
    def _():
        lax.fori_loop(0, nchunk, _make_gla_chunk(q_s, k_s, g_s, v_s, o_s, state_s, p_s), 0)

    st_ref[0] = state_s[...]
    o_main = _gla_out_gate(o_s[...], gate_s[...])
    xo_ref[0] = _mix_residual(x_ref[0], o_main, om_s[...], wo1_ref, wo2_ref, gpost_ref)


def _const_spec(a):
    nd = a.ndim
    return pl.BlockSpec(a.shape, lambda *_: (0,) * nd, pipeline_mode=pl.Buffered(1))


def _mixer_a_call(x, wa, kbd, vbd, make_guests):
    nb, seq, _ = x.shape
    tm = TM_PROMPT
    nt = seq // tm
    weights = [wa["gpre"], wa["win"], wa["wmisc"], wa["wg"], wa["bg"], wa["glan"]]
    tail = [wa["wo1"], wa["wo2"], wa["gpost"]]
    return _call_with_guests(
        _mixer_a_kernel,
        grid=(nb, nt),
        step_of=lambda b, t: b * nt + t,
        inputs=[x, *weights, kbd, vbd, *tail],
        in_specs=([pl.BlockSpec((1, tm, D), lambda b, t: (b, t, 0))]
                  + [_const_spec(w) for w in weights]
                  + _mem_bd_specs(0)
                  + [_const_spec(w) for w in tail]),
        out_specs=[pl.BlockSpec((1, tm, D), lambda b, t: (b, t, 0)),
                   pl.BlockSpec((1, GLA_KW, GLA_VW), lambda b, t: (b, 0, 0))],
        out_shape=[jax.ShapeDtypeStruct((nb, seq, D), F32),
                   jax.ShapeDtypeStruct((nb, GLA_KW, GLA_VW), F32)],
        scratch=[
            pltpu.VMEM((tm, GLA_KW), F32), pltpu.VMEM((tm, GLA_KW), F32), pltpu.VMEM((tm, GLA_KW), F32),
            pltpu.VMEM((tm, GLA_VW), F32), pltpu.VMEM((tm, GLA_VW), F32),
            pltpu.VMEM((GLA_KW, GLA_VW), F32),
            pltpu.VMEM((GLA_SUB * GLA_CHUNK, GLA_KW), BF16),
            pltpu.VMEM((tm, GLA_VW), F32), pltpu.VMEM((tm, MEM_W), BF16),
        ],
        guests=make_guests(nb * nt),
        name="mixer_a_prompt",
    )


def _ffn_rows(x, gpre_ref, wup_ref, wdn_ref, gpost_ref, acc_ref, arrive=None):
    arrive = arrive if arrive is not None else (lambda which, j: None)
    h = _bf(_rms(x, gpre_ref[0]))
    nff = D_FF // FF_CHUNK
    for j in range(nff - 1):
        cols = slice(j * FF_CHUNK, (j + 1) * FF_CHUNK)
        arrive(0, j)
        u = jnp.maximum(_dot(h, _bf(wup_ref[:, cols])), 0.0)
        arrive(1, j)
        d = _dot(_bf(u * u), _bf(wdn_ref[cols, :]))
        if j == 0:
            acc_ref[...] = d
        else:
            acc_ref[...] += d
    cols = slice((nff - 1) * FF_CHUNK, nff * FF_CHUNK)
    arrive(0, nff - 1)
    u = jnp.maximum(_dot(h, _bf(wup_ref[:, cols])), 0.0)
    uu = _bf(u * u)
    arrive(1, nff - 1)
    wd = _bf(wdn_ref[cols, :])
    m = x.shape[0]
    bounds = (0, m) if m < 256 else (0, m // 2, m)
    parts = []
    for lo, hi in zip(bounds[:-1], bounds[1:]):
        tot = acc_ref[lo:hi, :] + _dot(uu[lo:hi], wd)
        parts.append(x[lo:hi] + _rms(tot, gpost_ref[0]))
    return parts[0] if len(parts) == 1 else jnp.concatenate(parts, axis=0)


def _ffn_kernel(xp_ref, xs_ref, gpre_ref, wup_hbm, wdn_hbm, gpost_ref, yp_ref, ys_ref, acc_s, wup_s, wdn_s, sems,
                *, layer):
    i = pl.program_id(0)
    last = pl.num_programs(0) - 1
    nff = D_FF // FF_CHUNK

    def copy(which, j):
        cols = slice(j * FF_CHUNK, (j + 1) * FF_CHUNK)
        if which == 0:
            return pltpu.make_async_copy(wup_hbm.at[layer, :, cols], wup_s.at[:, cols], sems.at[0, j])
        return pltpu.make_async_copy(wdn_hbm.at[layer, cols, :], wdn_s.at[cols, :], sems.at[1, j])

    @pl.when(i == 0)
    def _():
        for j in range(nff):
            copy(0, j).start()
            copy(1, j).start()
        yp_ref[...] = _ffn_rows(xp_ref[...], gpre_ref, wup_s, wdn_s, gpost_ref, acc_s,
                                arrive=lambda which, j: copy(which, j).wait())

    @pl.when((i > 0) & (i < last))
    def _():
        yp_ref[...] = _ffn_rows(xp_ref[...], gpre_ref, wup_s, wdn_s, gpost_ref, acc_s)

    @pl.when(i == last)
    def _():
        ns = xs_ref.shape[0]
        ys_ref[...] = _ffn_rows(xs_ref[...], gpre_ref, wup_s, wdn_s, gpost_ref, acc_s.at[0:ns, :])


def _ffn_call(xp, xs, norm_pre, w_up, w_down, norm_post, layer):
    n, ns = xp.shape[0], xs.shape[0]
    tm = TM_PROMPT
    nt = n // tm
    gains = [a.reshape(a.shape[0], 1, D) for a in (norm_pre, norm_post)]
    gspec = pl.BlockSpec((1, 1, D), lambda i: (layer, 0, 0))
    hbm = pl.BlockSpec(memory_space=pl.ANY)
    tile = pl.BlockSpec((tm, D), lambda i: (jnp.minimum(i, nt - 1), 0))
    return pl.pallas_call(
        functools.partial(_ffn_kernel, layer=layer),
        grid=(nt + 1,),
        in_specs=[tile, _full_spec(xs), gspec, hbm, hbm, gspec],
        out_specs=[tile, _full_spec(xs)],
        out_shape=[jax.ShapeDtypeStruct((n, D), F32), jax.ShapeDtypeStruct((ns, D), F32)],
        scratch_shapes=[pltpu.VMEM((tm, D), F32), pltpu.VMEM((D, D_FF), F32), pltpu.VMEM((D_FF, D), F32),
                        pltpu.SemaphoreType.DMA((2, D_FF // FF_CHUNK))],
        compiler_params=_params(("arbitrary",)),
        name="ffn",
    )(xp, xs, gains[0], w_up, w_down, gains[1])


def _t5_bucket(dist):
    max_exact = N_BUCKETS // 2
    n = np.maximum(dist, 0)
    nf = np.maximum(n, 1).astype(np.float32)
    large = max_exact + (np.log(nf / np.float32(max_exact)) / np.float32(math.log(MAX_DISTANCE / max_exact))
                         * np.float32(N_BUCKETS - max_exact)).astype(np.int32)
    large = np.minimum(large, N_BUCKETS - 1)
    return np.where(n < max_exact, n, large)


def _swa_bias_tables():
    qi = np.arange(WINDOW)[:, None] + WINDOW
    kj = np.arange(2 * WINDOW)[None, :]
    dist = qi - kj
    valid = (dist >= 0) & (dist < WINDOW)
    return np.where(valid, _t5_bucket(dist), -1).astype(np.int32)


def _mixer_b_kernel(rb_ref, sink_ref,
                    x_ref, bkt_ref, gkv_ref, wkv_ref, gpre_ref, wq_ref, wqm_ref,
                    kbd_ref, vbd_ref, wo1_ref, wo2_ref, gpost_ref,
                    xo_ref, kc_ref, vc_ref,
                    kbuf, vbuf, q_s, o_s, bias_s, qm_s, pre_hook, hook):
    W = WINDOW
    tm = x_ref.shape[1]
    bb = pl.program_id(0)
    t = pl.program_id(1)
    nheads = SWA_G * SWA_KVH

    @pl.when((bb == 0) & (t == 0))
    def _():
        bkt = bkt_ref[...]
        own = _iota((W, 2 * W), 1) >= W
        for i in range(nheads):
            def add_bucket(n, acc):
                return jnp.where(bkt == n, rb_ref[n, i], acc)
            tab = lax.fori_loop(0, N_BUCKETS, add_bucket, jnp.zeros((W, 2 * W), F32))
            tab = jnp.where(bkt < 0, NEG_INF, tab)
            bias_s[0, i] = tab
            bias_s[1, i] = jnp.where(own, tab, NEG_INF)

    @pl.when(t == 0)
    def _():
        kbuf[0:W, :] = jnp.zeros((W, SWA_KVW), F32)
        vbuf[0:W, :] = jnp.zeros((W, SWA_KVW), F32)

    pre_hook()
    x = x_ref[0]
    xn = x * lax.rsqrt(jnp.mean(x * x, axis=-1, keepdims=True) + EPS)
    kv = _dot(_bf(xn * gkv_ref[...]), wkv_ref[...])
    kbuf[W:W + tm, :] = kv[:, :SWA_KVW]
    vbuf[W:W + tm, :] = kv[:, SWA_KVW:]
    kc_ref[0] = kv[tm - W:, :SWA_KVW]
    vc_ref[0] = kv[tm - W:, SWA_KVW:]

    h = _bf(xn * gpre_ref[...])
    q_s[...] = _dot(h, wq_ref[...]) * (SWA_HD ** -0.5)
    qm_s[...] = _dot(h, wqm_ref[...])
    hook()

    lane_head = _iota((W, SWA_KVW), 1) >> 6
    key_head = _iota((2 * W, SWA_KVW), 1) >> 6

    def scores(j):
        r0 = pl.multiple_of(j * W, W)
        qrows = []
        for gi in range(SWA_G):
            qg = q_s[pl.ds(r0, W), gi * SWA_KVW:(gi + 1) * SWA_KVW]
            for hh in range(SWA_KVH):
                qrows.append(jnp.where(lane_head == hh, qg, 0.0))
        return _dot_nt(_bf(jnp.concatenate(qrows, axis=0)), _bf(kbuf[pl.ds(r0, 2 * W), :]))

    def probs(j, s_all, gi):
        tab = jnp.where((j == 0) & (t == 0), 1, 0)
        ps = []
        for hh in range(SWA_KVH):
            i = gi * SWA_KVH + hh
            s = s_all[i * W:(i + 1) * W, :] + bias_s[tab, i]
            sink = sink_ref[i]
            mx = jnp.maximum(jnp.max(s, axis=-1, keepdims=True), sink)
            p = jnp.exp(s - mx)
            p = p / (jnp.sum(p, axis=-1, keepdims=True) + jnp.exp(sink - mx))
            ps.append(_bf(p))
        return jnp.concatenate(ps, axis=1)

    def pair(jj, carry):
        js = [jj * SWA_BLOCKS_PER_TRIP + i for i in range(SWA_BLOCKS_PER_TRIP)]
        s_alls = [scores(j) for j in js]
        for j, s_all in zip(js, s_alls):
            r0 = pl.multiple_of(j * W, W)
            vb = vbuf[pl.ds(r0, 2 * W), :]
            vst = _bf(jnp.concatenate([jnp.where(key_head == hh, vb, 0.0) for hh in range(SWA_KVH)], axis=0))
            for gi in range(SWA_G):
                o_s[pl.ds(r0, W), gi * SWA_KVW:(gi + 1) * SWA_KVW] = _dot(probs(j, s_all, gi), vst)
        return carry

    lax.fori_loop(0, tm // (W * SWA_BLOCKS_PER_TRIP), pair, 0)

    kbuf[0:W, :] = kbuf[tm:tm + W, :]
    vbuf[0:W, :] = vbuf[tm:tm + W, :]

    o_mem = _mem_pv(_mem_probs(qm_s[...], kbd_ref.at[0]), vbd_ref.at[0])
    xo_ref[0] = _mix_residual(x_ref[0], o_s[...], o_mem, wo1_ref, wo2_ref, gpost_ref)


def _mixer_b_call(x, wb, kbd, vbd, make_guests):
    nb, seq, _ = x.shape
    tm = TM_PROMPT
    nt = seq // tm
    bkt = jnp.asarray(_swa_bias_tables())
    head = [bkt, wb["gkv"], wb["wkv"], wb["gpre"], wb["wq"], wb["wqm"]]
    tail = [wb["wo1"], wb["wo2"], wb["gpost"]]
    smem = pl.BlockSpec(memory_space=pltpu.SMEM)
    return _call_with_guests(
        _mixer_b_kernel,
        grid=(nb, nt),
        step_of=lambda b, t: b * nt + t,
        inputs=[wb["rb"], wb["sinks"], x, *head, kbd, vbd, *tail],
        in_specs=([smem, smem, pl.BlockSpec((1, tm, D), lambda b, t: (b, t, 0))]
                  + [_const_spec(w) for w in head]
                  + _mem_bd_specs(1)
                  + [_const_spec(w) for w in tail]),
        out_specs=[pl.BlockSpec((1, tm, D), lambda b, t: (b, t, 0)),
                   pl.BlockSpec((1, WINDOW, SWA_KVW), lambda b, t: (b, 0, 0)),
                   pl.BlockSpec((1, WINDOW, SWA_KVW), lambda b, t: (b, 0, 0))],
        out_shape=[jax.ShapeDtypeStruct((nb, seq, D), F32),
                   jax.ShapeDtypeStruct((nb, WINDOW, SWA_KVW), F32),
                   jax.ShapeDtypeStruct((nb, WINDOW, SWA_KVW), F32)],
        scratch=[
            pltpu.VMEM((tm + WINDOW, SWA_KVW), F32), pltpu.VMEM((tm + WINDOW, SWA_KVW), F32),
            pltpu.VMEM((tm, SWA_QW), F32), pltpu.VMEM((tm, SWA_QW), F32),
            pltpu.VMEM((2, SWA_G * SWA_KVH, WINDOW, 2 * WINDOW), F32),
            pltpu.VMEM((tm, MEM_W), F32),
        ],
        guests=make_guests(nb * nt),
        name="mixer_b_prompt",
    )


def _row_to_col(row, eye):
    return jnp.sum(jnp.where(eye, jnp.broadcast_to(row, eye.shape), 0.0), axis=1, keepdims=True)


def _eye(n):
    return _iota((n, n), 0) == _iota((n, n), 1)


def _pre_a_kernel(x_ref, gpre_ref, win_ref, wmisc_ref, wg_ref, bg_ref, glan_ref,
                  qkgv_ref, gate_ref, qm_ref):
    q, k, g, v, gate, qm = _proj_a(x_ref[...], gpre_ref, win_ref, wmisc_ref, wg_ref, bg_ref,
                                   glan_ref)
    qkgv_ref[:, 0:GLA_KW] = q
    qkgv_ref[:, GLA_KW:2 * GLA_KW] = k
    qkgv_ref[:, 2 * GLA_KW:3 * GLA_KW] = g
    qkgv_ref[:, 3 * GLA_KW:] = v
    gate_ref[...] = gate
    qm_ref[...] = qm


def _pre_a_call(x2d, wa):
    n = x2d.shape[0]
    ws = [wa["gpre"], wa["win"], wa["wmisc"], wa["wg"], wa["bg"], wa["glan"]]
    shapes = [(n, 3 * GLA_KW + GLA_VW), (n, GLA_VW), (n, MEM_W)]
    return pl.pallas_call(
        _pre_a_kernel,
        grid=(1,),
        in_specs=[_full_spec(x2d)] + [_full_spec(w) for w in ws],
        out_specs=[pl.BlockSpec(s, lambda i: (0, 0)) for s in shapes],
        out_shape=[jax.ShapeDtypeStruct(s, F32) for s in shapes],
        compiler_params=_params(("arbitrary",)),
        name="pre_a_sample",
    )(x2d, *ws)


def _gla_step_guest(qkgv_t, state5, nsteps):
    n = qkgv_t.shape[1]
    nblk = GLA_DK // GLA_DK_BLOCK
    nact = GLA_H * nblk
    assert nact <= nsteps
    act = lambda s: jnp.minimum(s, nact - 1)

    def init(step, qt_ref, kt_ref, gt_ref, vt_ref, s_ref, so_ref, ot_ref):
        @pl.when((step < nact) & (step % nblk == 0))
        def _():
            ot_ref[...] = jnp.zeros_like(ot_ref)

    def post(step, _, qt_ref, kt_ref, gt_ref, vt_ref, s_ref, so_ref, ot_ref):
        vt_blk = vt_ref[...]
        acc = jnp.zeros_like(vt_blk)
        for d in range(GLA_DK_BLOCK):
            s_new = jnp.exp(gt_ref[d:d + 1, :]) * s_ref[0, 0, d] + kt_ref[d:d + 1, :] * vt_blk
            so_ref[0, 0, d] = s_new
            acc = acc + qt_ref[d:d + 1, :] * s_new
        ot_ref[...] += jnp.where(step < nact, acc, 0.0)

    rows = lambda part: ((GLA_DK_BLOCK, n), lambda s: (part * nact + act(s), 0))
    vrow = ((GLA_DV, n), lambda s: (3 * GLA_KW // GLA_DV + act(s) // nblk, 0))
    head = ((GLA_DV, n), lambda s: (act(s) // nblk, 0))
    st = ((1, 1, GLA_DK_BLOCK, GLA_DV, n), lambda s: (0, act(s) // nblk, act(s) % nblk, 0, 0))
    return dict(inputs=[qkgv_t, qkgv_t, qkgv_t, qkgv_t, state5], in_specs=[rows(0), rows(1), rows(2), vrow, st],
                out_shape=[jax.ShapeDtypeStruct(state5.shape, F32), jax.ShapeDtypeStruct((GLA_VW, n), F32)],
                out_specs=[st, head], scratch=[], init=init, pre=lambda step, *refs: None, post=post)


def _mem_step_pre(step, qm_ref, mk_ref, mv_ref, o_ref):
    own = (_iota((8, MEM_W), 1) >> 6) == _iota((8, MEM_W), 0)
    ps = []
    for i in range(qm_ref.shape[0]):
        q8 = jnp.where(own, jnp.broadcast_to(qm_ref[i], (8, MEM_W)), 0.0)
        s = _dot(_bf(q8), _bf(mk_ref[0, i].reshape(MEM_W, N_MEM))) * (MEM_HD ** -0.5)
        e = jnp.exp(s - jnp.max(s, axis=1, keepdims=True))
        ps.append(_bf(e / jnp.sum(e, axis=1, keepdims=True)))
    return ps


def _mem_step_post(step, ps, qm_ref, mk_ref, mv_ref, o_ref):
    own = (_iota((8, MEM_W), 1) >> 6) == _iota((8, MEM_W), 0)
    for i, p in enumerate(ps):
        res = _dot_nt(p, _bf(mv_ref[0, i].reshape(MEM_W, N_MEM)))
        o_ref[i] = jnp.sum(jnp.where(own, res, 0.0), axis=0, keepdims=True)


def _mem_step_guest(qm, mk5, mv5, layer, nsteps):
    n = qm.shape[0]
    rb = n // nsteps
    blk = ((1, rb, MEM_H, MEM_HD, N_MEM), lambda s: (layer, s, 0, 0, 0))
    rows = ((rb, 1, MEM_W), lambda s: (s, 0, 0))
    return dict(inputs=[qm.reshape(n, 1, MEM_W), mk5, mv5], in_specs=[rows, blk, blk],
                out_shape=[jax.ShapeDtypeStruct((n, 1, MEM_W), F32)], out_specs=[rows], scratch=[],
                init=lambda step, *refs: None, pre=_mem_step_pre, post=_mem_step_post)


def _post_a_kernel(x_ref, o_ref, gate_ref, om_ref, wo1_ref, wo2_ref, gpost_ref, xo_ref):
    o_main = _gla_out_gate(o_ref[...], gate_ref[...])
    xo_ref[...] = _mix_residual(x_ref[...], o_main, om_ref[...], wo1_ref, wo2_ref, gpost_ref)


def _post_a_call(x2d, o, gate, om, wa):
    args = [x2d, o, gate, om, wa["wo1"], wa["wo2"], wa["gpost"]]
    return pl.pallas_call(
        _post_a_kernel,
        grid=(1,),
        in_specs=[_full_spec(a) for a in args],
        out_specs=_full_spec(x2d),
        out_shape=jax.ShapeDtypeStruct(x2d.shape, F32),
        compiler_params=_params(("arbitrary",)),
        name="post_a_sample",
    )(*args)


def _pre_b_kernel(x_ref, gkv_ref, wkv_ref, gpre_ref, wq_ref, wqm_ref, kv_ref, q_ref, qm_ref):
    x = x_ref[...]
    kv_ref[...] = _dot(_bf(_rms(x, gkv_ref[...])), wkv_ref[...])
    h = _bf(_rms(x, gpre_ref[...]))
    q_ref[...] = _dot(h, wq_ref[...])
    qm_ref[...] = _dot(h, wqm_ref[...])


def _pre_b_call(x2d, wb):
    n = x2d.shape[0]
    ws = [wb["gkv"], wb["wkv"], wb["gpre"], wb["wq"], wb["wqm"]]
    widths = [2 * SWA_KVW, SWA_QW, MEM_W]
    return pl.pallas_call(
        _pre_b_kernel,
        grid=(1,),
        in_specs=[_full_spec(x2d)] + [_full_spec(w) for w in ws],
        out_specs=[pl.BlockSpec((n, w), lambda i: (0, 0)) for w in widths],
        out_shape=[jax.ShapeDtypeStruct((n, w), F32) for w in widths],
        compiler_params=_params(("arbitrary",)),
        name="pre_b_sample",
    )(x2d, *ws)


def _sample_buckets():
    dist = (WINDOW - 1) - np.arange(WINDOW)
    return _t5_bucket(dist).astype(np.int32).reshape(1, WINDOW)


def _swa_step_init(step, sink_ref, rb_ref, bkt_ref, kc_ref, vc_ref, kst_ref, vst_ref, q_ref,
                   kn_ref, vn_ref, o_ref, bias_s, sink_s, s_s, p_s):
    W = WINDOW
    R = SWA_KVH * 8

    @pl.when(step == 0)
    def _():
        bkt = bkt_ref[...]
        rid = _iota((R, W), 0)
        bias = jnp.zeros((R, W), F32)
        sink = jnp.zeros((R, W), F32)
        for h in range(SWA_KVH):
            for g in range(SWA_G):
                idx = g * SWA_KVH + h
                def add_bucket(n, acc):
                    return jnp.where(bkt == n, rb_ref[n, idx], acc)
                brow = lax.fori_loop(0, N_BUCKETS, add_bucket, jnp.zeros((1, W), F32))
                bias = jnp.where(rid == h * 8 + g, brow, bias)
                sink = jnp.where(rid == h * 8 + g, sink_ref[idx], sink)
        bias_s[...] = bias
        sink_s[...] = sink


def _swa_step_pre(step, sink_ref, rb_ref, bkt_ref, kc_ref, vc_ref, kst_ref, vst_ref, q_ref,
                  kn_ref, vn_ref, o_ref, bias_s, sink_s, s_s, p_s):
    W = WINDOW
    R = SWA_KVH * 8
    rb = q_ref.shape[0]
    base = step * rb
    last = _iota((SWA_KVW, W), 1) == W - 1
    own = (_iota((R, SWA_KVW), 1) >> 6) == (_iota((R, SWA_KVW), 0) >> 3)
    kst = kst_ref[...]
    vst = vst_ref[...]
    for i in range(rb):
        shift = W - 1 - (base + i)
        kn = jnp.where(last, pltpu.roll(kst, shift, 1), pltpu.roll(kc_ref[i].reshape(SWA_KVW, W), W - 1, 1))
        vn = jnp.where(last, pltpu.roll(vst, shift, 1), pltpu.roll(vc_ref[i].reshape(SWA_KVW, W), W - 1, 1))
        kn_ref[i] = kn.reshape(SWA_KVH, SWA_HD, W)
        vn_ref[i] = vn.reshape(SWA_KVH, SWA_HD, W)
        q32 = jnp.where(own, jnp.concatenate([q_ref[i]] * SWA_KVH, axis=0), 0.0)
        s_s[i * R:(i + 1) * R, :] = _dot(_bf(q32), _bf(kn))
    s = s_s[...] * (SWA_HD ** -0.5) + jnp.concatenate([bias_s[...]] * rb, axis=0)
    sink = jnp.concatenate([sink_s[...]] * rb, axis=0)
    mx = jnp.maximum(jnp.max(s, axis=1, keepdims=True), sink)
    p = jnp.exp(s - mx)
    p_s[...] = _bf(p / (jnp.sum(p, axis=1, keepdims=True) + jnp.exp(sink - mx)))


def _swa_step_post(step, _, sink_ref, rb_ref, bkt_ref, kc_ref, vc_ref, kst_ref, vst_ref, q_ref,
                   kn_ref, vn_ref, o_ref, bias_s, sink_s, s_s, p_s):
    W = WINDOW
    R = SWA_KVH * 8
    own = (_iota((R, SWA_KVW), 1) >> 6) == (_iota((R, SWA_KVW), 0) >> 3)
    for i in range(q_ref.shape[0]):
        vn = vn_ref[i].reshape(SWA_KVW, W)
        res = jnp.where(own, _dot_nt(p_s[i * R:(i + 1) * R, :], _bf(vn)), 0.0)
        o_ref[i] = res[0:8] + res[8:16] + res[16:24] + res[24:32]


def _swa_step_guest(kc4, vc4, kv_t, q, wb, nsteps):
    n = q.shape[0]
    rb = n // nsteps
    bkt = jnp.asarray(_sample_buckets())
    q8 = jnp.pad(q.reshape(n, SWA_G, SWA_KVW), ((0, 0), (0, 8 - SWA_G), (0, 0)))
    row3 = lambda r, w: ((rb, r, w), lambda s: (s, 0, 0))
    cache = ((rb, SWA_KVH, SWA_HD, WINDOW), lambda s: (s, 0, 0, 0))
    half = lambda i: ((SWA_KVW, n), lambda s: (i, 0))
    smem = pl.BlockSpec(memory_space=pltpu.SMEM)
    return dict(inputs=[wb["sinks"], wb["rb"], bkt, kc4, vc4, kv_t, kv_t, q8],
                in_specs=[smem, smem, _full_spec(bkt), cache, cache, half(0), half(1), row3(8, SWA_KVW)],
                out_shape=[jax.ShapeDtypeStruct(kc4.shape, F32), jax.ShapeDtypeStruct(vc4.shape, F32),
                           jax.ShapeDtypeStruct((n, 8, SWA_KVW), F32)],
                out_specs=[cache, cache, row3(8, SWA_KVW)],
                scratch=[pltpu.VMEM((SWA_KVH * 8, WINDOW), F32), pltpu.VMEM((SWA_KVH * 8, WINDOW), F32),
                         pltpu.VMEM((rb * SWA_KVH * 8, WINDOW), F32),
                         pltpu.VMEM((rb * SWA_KVH * 8, WINDOW), BF16)],
                init=_swa_step_init, pre=_swa_step_pre, post=_swa_step_post)


def _post_b_kernel(x_ref, o_ref, om_ref, wo1_ref, wo2_ref, gpost_ref, xo_ref):
    xo_ref[...] = _mix_residual(x_ref[...], o_ref[...], om_ref[...], wo1_ref, wo2_ref, gpost_ref)


def _post_b_call(x2d, o, om, wb):
    args = [x2d, o, om, wb["wo1"], wb["wo2"], wb["gpost"]]
    return pl.pallas_call(
        _post_b_kernel,
        grid=(1,),
        in_specs=[_full_spec(a) for a in args],
        out_specs=_full_spec(x2d),
        out_shape=jax.ShapeDtypeStruct(x2d.shape, F32),
        compiler_params=_params(("arbitrary",)),
        name="post_b_sample",
    )(*args)


def _prep_weights(norm_mix_pre, norm_mix_post, w_in_a, w_gate_up, b_gate,
                  gla_norm, w_in_b, sinks, norm_kv, w_kv, rel_bias, w_out):
    row = lambda g: g.reshape(1, -1)
    wa_in = w_in_a[0]
    c_g = GLA_COL_GATE
    c_m = c_g + GATE_RANK
    wmisc = jnp.concatenate([wa_in[:, c_g:c_m], jnp.zeros((D, 128 - GATE_RANK), F32), wa_in[:, c_m:]], axis=1)
    wg = jnp.zeros((128, GLA_KW), F32).at[:GATE_RANK].set(w_gate_up[0])
    wa = dict(
        gpre=row(norm_mix_pre[0]), win=_bf(wa_in),
        wmisc=_bf(wmisc), wg=_bf(wg), bg=row(b_gate[0]), glan=row(jnp.tile(gla_norm[0], GLA_H)),
        wo1=_bf(w_out[0][:GLA_VW]), wo2=_bf(w_out[0][GLA_VW:]), gpost=row(norm_mix_post[0]))
    wb_in = w_in_b[0]
    wq = wb_in[:, :SWA_QW].reshape(D, SWA_KVH, SWA_G, SWA_HD).transpose(0, 2, 1, 3).reshape(D, SWA_QW)
    wo1 = w_out[1][:SWA_QW].reshape(SWA_KVH, SWA_G, SWA_HD, D).transpose(1, 0, 2, 3).reshape(SWA_QW, D)
    rb = rel_bias.reshape(N_BUCKETS, SWA_KVH, SWA_G).transpose(0, 2, 1).reshape(N_BUCKETS, SWA_G * SWA_KVH)
    sk = sinks[0].reshape(SWA_KVH, SWA_G).T.reshape(SWA_G * SWA_KVH)
    wb = dict(
        gkv=row(norm_kv), wkv=_bf(w_kv), gpre=row(norm_mix_pre[1]), wq=_bf(wq), wqm=_bf(wb_in[:, SWA_QW:]),
        wo1=_bf(wo1), wo2=_bf(w_out[1][SWA_QW:]), gpost=row(norm_mix_post[1]), rb=rb, sinks=sk)
    return wa, wb


def kernel(x_prompt, x_sample, state_gla, cache_swa_k, cache_swa_v, cache_mem_k, cache_mem_v, mem_prompt,
           norm_mix_pre, norm_mix_post, norm_ffn_pre, norm_ffn_post, norm_mem, w_mem_kv, w_in_a, w_gate_up,
           b_gate, gla_norm, w_in_b, sinks, norm_kv, w_kv, rel_bias, w_out, w_ffn_up, w_ffn_down):
    wa, wb = _prep_weights(norm_mix_pre, norm_mix_post, w_in_a, w_gate_up, b_gate, gla_norm, w_in_b, sinks,
                           norm_kv, w_kv, rel_bias, w_out)
    ffn = lambda xp, xs_, l: _ffn_call(xp, xs_, norm_ffn_pre, w_ffn_up, w_ffn_down, norm_ffn_post, l)
    nb, seq, _ = x_prompt.shape
    ns = x_sample.shape[0]

    xs = x_sample.reshape(ns, D)
    state5 = jnp.transpose(state_gla, (0, 2, 3, 4, 1))
    mk5 = jnp.transpose(cache_mem_k, (0, 1, 3, 4, 2))
    mv5 = jnp.transpose(cache_mem_v, (0, 1, 3, 4, 2))
    kc4 = jnp.transpose(cache_swa_k, (0, 2, 3, 1))
    vc4 = jnp.transpose(cache_swa_v, (0, 2, 3, 1))

    mkt, mvt, kbd, vbd = _memkv_call(mem_prompt, norm_mem, w_mem_kv)
    qkgv, gate, qm = _pre_a_call(xs, wa)
    qkgv_t = qkgv.T
    (x1, st), ((state5_new, ot), (om,)) = _mixer_a_call(
        x_prompt, wa, kbd, vbd,
        lambda nsteps: [_gla_step_guest(qkgv_t, state5, nsteps),
                        _mem_step_guest(qm, mk5, mv5, 0, nsteps)])
    xs1 = _post_a_call(xs, ot.T, gate, om.reshape(ns, MEM_W), wa)
    x2, xs2 = ffn(x1.reshape(nb * seq, D), xs1, 0)
    kv, qb, qmb = _pre_b_call(xs2, wb)
    kv_t = kv.T
    (x3, kc, vc), ((kn4, vn4, o8), (omb,)) = _mixer_b_call(
        x2.reshape(nb, seq, D), wb, kbd, vbd,
        lambda nsteps: [_swa_step_guest(kc4, vc4, kv_t, qb, wb, nsteps),
                        _mem_step_guest(qmb, mk5, mv5, 1, nsteps)])
    xs3 = _post_b_call(xs2, o8[:, :SWA_G].reshape(ns, SWA_QW), omb.reshape(ns, MEM_W), wb)
    y_prompt, y_sample = ffn(x3.reshape(nb * seq, D), xs3, 1)
    y_prompt = y_prompt.reshape(nb, seq, D)
    y_sample = y_sample.reshape(ns, 1, D)
    st4 = st.reshape(nb, GLA_H, GLA_DK, GLA_H, GLA_DV)
    state_prompt = jnp.stack([st4[:, h, :, h, :] for h in range(GLA_H)], axis=1)[None]
    to_mem = lambda t: t.reshape(2, nb, MEM_H, MEM_HD, N_MEM).transpose(0, 1, 4, 2, 3)
    swa_shape = (nb, WINDOW, SWA_KVH, SWA_HD)

    return (y_prompt, y_sample, state_prompt,
            jnp.transpose(state5_new, (0, 4, 1, 2, 3)),
            kc.reshape(swa_shape), vc.reshape(swa_shape),
            jnp.transpose(kn4, (0, 3, 1, 2)), jnp.transpose(vn4, (0, 3, 1, 2)),
            to_mem(mkt), to_mem(mvt))
```

```python
import functools
import math

import numpy as np
import jax
import jax.numpy as jnp
from jax import lax
from jax.experimental import pallas as pl
from jax.experimental.pallas import tpu as pltpu

F32 = jnp.float32
BF16 = jnp.bfloat16

D = 1024
D_FF = 4 * D
N_MEM = 256
MEM_H = 4
MEM_HD = 64
MEM_W = MEM_H * MEM_HD
GLA_H = 4
GLA_DK = 96
GLA_DV = 192
GLA_KW = GLA_H * GLA_DK
GLA_VW = GLA_H * GLA_DV
GATE_RANK = 16
GLA_COL_V = 2 * GLA_KW
GLA_COL_R = GLA_COL_V + GLA_VW
GLA_COL_GATE = GLA_COL_R + GLA_VW
GATE_NORM = 16.0
SWA_HD = 64
SWA_KVH = 4
SWA_G = 3
SWA_QW = SWA_KVH * SWA_G * SWA_HD
SWA_KVW = SWA_KVH * SWA_HD
WINDOW = 128
N_BUCKETS = 32
MAX_DISTANCE = 128
EPS = 1e-6

GLA_CHUNK = 64
GLA_SUB = 16
GLA_SAFE_DECAY = 60.0
GLA_GROUP = 4
SWA_BLOCKS_PER_TRIP = 2
TM_PROMPT = 512
FF_CHUNK = 512
GLA_DK_BLOCK = 16
V7X_VMEM_LIMIT = 56 * 1024 * 1024
NEG_INF = float("-inf")


def _bf(x):
    return x.astype(BF16)


def _dot(a, b):
    return jnp.dot(a, b, preferred_element_type=F32)


def _dot_nt(a, b):
    return lax.dot_general(a, b, (((1,), (1,)), ((), ())), preferred_element_type=F32)


def _dot_tn(a, b):
    return lax.dot_general(a, b, (((0,), (0,)), ((), ())), preferred_element_type=F32)


def _rms(x, g):
    return x * lax.rsqrt(jnp.mean(x * x, axis=-1, keepdims=True) + EPS) * g


def _split3(x):
    x1 = _bf(x)
    r1 = x - x1.astype(F32)
    x2 = _bf(r1)
    x3 = _bf(r1 - x2.astype(F32))
    return x1, x2, x3


def _exact_dot(sel, x):
    x1, x2, x3 = _split3(x)
    return _dot(sel, x1) + _dot(sel, x2) + _dot(sel, x3)


def _log_sigmoid(z):
    return jnp.minimum(z, 0.0) - jnp.log1p(jnp.exp(-jnp.abs(z)))


def _silu(z):
    return z * (1.0 / (1.0 + jnp.exp(-z)))


def _iota(shape, dim):
    return lax.broadcasted_iota(jnp.int32, shape, dim)


def _gla_k_head(lane):
    one = jnp.int32(1)
    zero = jnp.int32(0)
    return (jnp.where(lane >= GLA_DK, one, zero) + jnp.where(lane >= 2 * GLA_DK, one, zero)
            + jnp.where(lane >= 3 * GLA_DK, one, zero))


def _gla_v_head(lane):
    one = jnp.int32(1)
    zero = jnp.int32(0)
    return (jnp.where(lane >= GLA_DV, one, zero) + jnp.where(lane >= 2 * GLA_DV, one, zero)
            + jnp.where(lane >= 3 * GLA_DV, one, zero))


def _full_spec(a):
    nd = a.ndim
    return pl.BlockSpec(a.shape, lambda *_: (0,) * nd)


def _mem_bd_specs(layer):
    return [pl.BlockSpec((None, 1, MEM_W, MEM_H * N_MEM), lambda b, t: (layer, b, 0, 0)),
            pl.BlockSpec((None, 1, MEM_H * N_MEM, MEM_W), lambda b, t: (layer, b, 0, 0))]


def _params(sem):
    return pltpu.CompilerParams(dimension_semantics=sem, vmem_limit_bytes=V7X_VMEM_LIMIT)


def _call_with_guests(host_kernel, grid, step_of, inputs, in_specs, out_shape, out_specs, scratch, guests, name):
    def to_spec(s):
        if isinstance(s, pl.BlockSpec):
            return s
        block, fn = s
        return pl.BlockSpec(block, lambda *ids: fn(step_of(*ids)))

    counts = [(len(inputs), len(out_shape), len(scratch))]
    counts += [(len(g["inputs"]), len(g["out_shape"]), len(g["scratch"])) for g in guests]

    def kern(*refs):
        refs = list(refs)
        parts = [[refs.pop(0) for _ in range(c[k])] for k in range(3) for c in counts]
        n = len(counts)
        ins, outs, scrs = parts[:n], parts[n:2 * n], parts[2 * n:]
        step = step_of(*[pl.program_id(a) for a in range(len(grid))])
        grefs = [(*a, *b, *c) for a, b, c in zip(ins[1:], outs[1:], scrs[1:])]
        states = []

        def pre_hook():
            for g, r in zip(guests, grefs):
                g["init"](step, *r)
            states.extend(g["pre"](step, *r) for g, r in zip(guests, grefs))

        def hook():
            for g, st, r in zip(guests, states, grefs):
                g["post"](step, st, *r)

        host_kernel(*ins[0], *outs[0], *scrs[0], pre_hook=pre_hook, hook=hook)

    res = pl.pallas_call(
        kern,
        grid=grid,
        in_specs=list(in_specs) + [to_spec(s) for g in guests for s in g["in_specs"]],
        out_specs=list(out_specs) + [to_spec(s) for g in guests for s in g["out_specs"]],
        out_shape=list(out_shape) + [s for g in guests for s in g["out_shape"]],
        scratch_shapes=list(scratch) + [s for g in guests for s in g["scratch"]],
        compiler_params=_params(("arbitrary",) * len(grid)),
        name=name,
    )(*inputs, *[a for g in guests for a in g["inputs"]])
    res = list(res)
    split = []
    for c in counts:
        split.append([res.pop(0) for _ in range(c[1])])
    return split[0], split[1:]


def _mem_probs(qm, kbd_ref):
    s = _dot(_bf(qm), kbd_ref[...]) * (MEM_HD ** -0.5)
    ps = []
    for h in range(MEM_H):
        sh = s[:, h * N_MEM:(h + 1) * N_MEM]
        e = jnp.exp(sh - jnp.max(sh, axis=-1, keepdims=True))
        ps.append(_bf(e / jnp.sum(e, axis=-1, keepdims=True)))
    return ps


def _mem_pv(ps, vbd_ref):
    out = None
    for h in range(MEM_H):
        t = _dot(ps[h], vbd_ref[h * N_MEM:(h + 1) * N_MEM, :])
        out = t if out is None else out + t
    return out


def _memkv_kernel(mem_ref, g_ref, w_ref, k_ref, v_ref, kbd_ref, vbd_ref):
    h = _bf(_rms(mem_ref[0], g_ref[0]))
    kv = _dot(h, _bf(w_ref[0]))
    k = kv[:, :MEM_W]
    v = kv[:, MEM_W:]
    kt = k.T
    k_ref[0, 0] = kt
    v_ref[0, 0] = v.T
    kt4 = jnp.concatenate([kt, kt, kt, kt], axis=1)
    keep_k = (_iota((MEM_W, MEM_H * N_MEM), 0) >> 6) == (_iota((MEM_W, MEM_H * N_MEM), 1) >> 8)
    kbd_ref[0, 0] = _bf(jnp.where(keep_k, kt4, 0.0))
    v4 = jnp.concatenate([v, v, v, v], axis=0)
    keep_v = (_iota((MEM_H * N_MEM, MEM_W), 0) >> 8) == (_iota((MEM_H * N_MEM, MEM_W), 1) >> 6)
    vbd_ref[0, 0] = _bf(jnp.where(keep_v, v4, 0.0))


def _memkv_call(mem, norm_mem, w_mem_kv):
    nb = mem.shape[0]
    nl = w_mem_kv.shape[0]
    g = norm_mem.reshape(nl, 1, D)
    return pl.pallas_call(
        _memkv_kernel,
        grid=(nl, nb),
        in_specs=[
            pl.BlockSpec((1, N_MEM, D), lambda l, b: (b, 0, 0)),
            pl.BlockSpec((1, 1, D), lambda l, b: (l, 0, 0)),
            pl.BlockSpec((1, D, 2 * MEM_W), lambda l, b: (l, 0, 0)),
        ],
        out_specs=[
            pl.BlockSpec((1, 1, MEM_W, N_MEM), lambda l, b: (l, b, 0, 0)),
            pl.BlockSpec((1, 1, MEM_W, N_MEM), lambda l, b: (l, b, 0, 0)),
            pl.BlockSpec((1, 1, MEM_W, MEM_H * N_MEM), lambda l, b: (l, b, 0, 0)),
            pl.BlockSpec((1, 1, MEM_H * N_MEM, MEM_W), lambda l, b: (l, b, 0, 0)),
        ],
        out_shape=[
            jax.ShapeDtypeStruct((nl, nb, MEM_W, N_MEM), F32),
            jax.ShapeDtypeStruct((nl, nb, MEM_W, N_MEM), F32),
            jax.ShapeDtypeStruct((nl, nb, MEM_W, MEM_H * N_MEM), BF16),
            jax.ShapeDtypeStruct((nl, nb, MEM_H * N_MEM, MEM_W), BF16),
        ],
        compiler_params=_params(("arbitrary", "arbitrary")),
        name="mem_kv",
    )(mem, g, w_mem_kv)


def _proj_a(x, gpre_ref, win_ref, wmisc_ref, wg_ref, bg_ref, glan_ref, mem_refs=None):
    h = _bf(_rms(x, gpre_ref[...]))
    misc = _dot(h, wmisc_ref[...])
    glr = misc[:, :128]
    qm = misc[:, 128:]
    ps = None if mem_refs is None else _mem_probs(qm, mem_refs[0])
    g = _log_sigmoid(_dot(_bf(glr), wg_ref[...]) + bg_ref[...]) * (1.0 / GATE_NORM)
    r = _dot(h, win_ref[:, GLA_COL_R:GLA_COL_GATE])
    gate = glan_ref[...] * _silu(r)
    qk = _dot(h, win_ref[:, :GLA_COL_V])
    q = qk[:, :GLA_KW] * (GLA_DK ** -0.5)
    k = qk[:, GLA_KW:]
    v = _dot(h, win_ref[:, GLA_COL_V:GLA_COL_R])
    return q, k, g, v, gate, (qm if mem_refs is None else _mem_pv(ps, mem_refs[1]))


def _gla_out_gate(o, gate):
    vh = _gla_v_head(_iota(o.shape, 1))
    o2 = o * o
    scale = jnp.zeros_like(o)
    for h in range(GLA_H):
        ss = jnp.sum(jnp.where(vh == h, o2, 0.0), axis=-1, keepdims=True) * (1.0 / GLA_DV)
        scale = jnp.where(vh == h, lax.rsqrt(ss + EPS), scale)
    return o * scale * gate


def _mix_residual(x, o_main, o_mem, wo1_ref, wo2_ref, gpost_ref):
    mix = _dot(_bf(o_main), wo1_ref[...]) + _dot(_bf(o_mem), wo2_ref[...])
    return x + _rms(mix, gpost_ref[...])


def _make_gla_chunk(q_s, k_s, g_s, v_s, o_s, state_s, p_s):
    C = GLA_CHUNK

    ri = _iota((C, GLA_KW), 0)
    kh = _gla_k_head(_iota((C, GLA_KW), 1))
    khcat = jnp.concatenate([kh, kh, kh], axis=1)
    vh = _gla_v_head(_iota((C, GLA_VW), 1))
    tri = _bf(jnp.where(_iota((C, C), 0) >= _iota((C, C), 1), 1.0, 0.0))
    d_rs = _iota((C, 4 * C), 0) - (_iota((C, 4 * C), 1) & (C - 1))
    band = jnp.where((d_rs >= 0) & (d_rs <= (_iota((C, 4 * C), 0) & (GLA_SUB - 1))), d_rs, -1)
    ebc = _bf(jnp.where(_gla_k_head(_iota((GLA_KW, 4 * C), 0)) == (_iota((GLA_KW, 4 * C), 1) >> 6),
                        1.0, 0.0))
    blockmask = _gla_k_head(_iota((GLA_KW, GLA_VW), 0)) == _gla_v_head(_iota((GLA_KW, GLA_VW), 1))
    eye = _iota((GLA_KW, GLA_KW), 0) == _iota((GLA_KW, GLA_KW), 1)

    def chunk(c, carry):
        r0 = pl.multiple_of(c * C, C)
        qc = q_s[pl.ds(r0, C), :]
        kc = k_s[pl.ds(r0, C), :]
        gc = g_s[pl.ds(r0, C), :]
        vc = v_s[pl.ds(r0, C), :]
        b = _exact_dot(tri, gc)

        ref1 = jnp.broadcast_to(b[31:32, :], b.shape)
        ref2 = jnp.where(ri < 32, jnp.broadcast_to(b[15:16, :], b.shape),
                         jnp.broadcast_to(b[47:48, :], b.shape))
        q1 = qc * jnp.exp(jnp.minimum(b - ref1, 0.0))
        k1 = kc * jnp.exp(jnp.minimum(ref1 - b, 0.0))
        q2 = qc * jnp.exp(jnp.minimum(b - ref2, 0.0))
        k2 = kc * jnp.exp(jnp.minimum(ref2 - b, 0.0))
        qcat = jnp.concatenate([
            jnp.where(ri >= 32, q1, 0.0),
            jnp.where((ri >= 16) & (ri < 32), q2, 0.0),
            jnp.where(ri >= 48, q2, 0.0)], axis=1)
        kcat = jnp.concatenate([
            jnp.where(ri < 32, k1, 0.0),
            jnp.where(ri < 16, k2, 0.0),
            jnp.where((ri >= 32) & (ri < 48), k2, 0.0)], axis=1)
        kst = _bf(jnp.concatenate([jnp.where(khcat == h, kcat, 0.0) for h in range(GLA_H)], axis=0))
        a_off = _dot_nt(_bf(qcat), kst)

        for dlt in range(GLA_SUB):
            if dlt == 0:
                pr = qc * kc
            else:
                kd = pltpu.roll(kc, dlt, 0)
                bd = pltpu.roll(b, dlt, 0)
                pr = qc * kd * jnp.exp(jnp.minimum(b - bd, 0.0))
            p_s[dlt * C:(dlt + 1) * C, :] = _bf(pr)
        rsum = _dot(p_s[...], ebc)
        a_diag = jnp.zeros((C, 4 * C), F32)
        for dlt in range(GLA_SUB):
            a_diag = jnp.where(band == dlt, rsum[dlt * C:(dlt + 1) * C, :], a_diag)

        vst = _bf(jnp.concatenate([jnp.where(vh == h, vc, 0.0) for h in range(GLA_H)], axis=0))
        o_intra = _dot(_bf(a_off + a_diag), vst)
        st = state_s[...]
        o_inter = _dot(_bf(qc * jnp.exp(b)), _bf(st))
        o_s[pl.ds(r0, C), :] = o_intra + o_inter

        blast = jnp.broadcast_to(b[C - 1:C, :], b.shape)
        kdec = kc * jnp.exp(blast - b)
        kv = _dot_tn(_bf(kdec), _bf(vc))
        decay = _row_to_col(jnp.exp(b[C - 1:C, :]), eye)
        state_s[...] = st * decay + jnp.where(blockmask, kv, 0.0)
        return carry

    return chunk


def _make_gla_chunk_bounded(q_s, k_s, g_s, v_s, o_s, state_s):
    C = GLA_CHUNK
    kh = _gla_k_head(_iota((C, GLA_KW), 1))
    vh = _gla_v_head(_iota((C, GLA_VW), 1))
    tri = _bf(jnp.where(_iota((C, C), 0) >= _iota((C, C), 1), 1.0, 0.0))
    causal = _iota((C, 4 * C), 0) >= (_iota((C, 4 * C), 1) & (C - 1))
    blockmask = _gla_k_head(_iota((GLA_KW, GLA_VW), 0)) == _gla_v_head(_iota((GLA_KW, GLA_VW), 1))
    eye = _iota((GLA_KW, GLA_KW), 0) == _iota((GLA_KW, GLA_KW), 1)

    def group(gi, carry):
        base = gi * (GLA_GROUP * C)
        rows = [pl.ds(pl.multiple_of(base + i * C, C), C) for i in range(GLA_GROUP)]
        bs = [_exact_dot(tri, g_s[r, :]) for r in rows]
        qes = [_bf(q_s[r, :] * jnp.exp(b)) for r, b in zip(rows, bs)]
        kes = [k_s[r, :] * jnp.exp(-b) for r, b in zip(rows, bs)]
        ksts = [_bf(jnp.concatenate([jnp.where(kh == h, ke, 0.0) for h in range(GLA_H)], axis=0)) for ke in kes]
        attn = [_bf(jnp.where(causal, _dot_nt(qe, kst), 0.0)) for qe, kst in zip(qes, ksts)]
        vcs = [v_s[r, :] for r in rows]
        vsts = [_bf(jnp.concatenate([jnp.where(vh == h, vc, 0.0) for h in range(GLA_H)], axis=0)) for vc in vcs]
        o_intra = [_dot(a, vst) for a, vst in zip(attn, vsts)]
        decays = [jnp.exp(b[C - 1:C, :]) for b in bs]
        kvs = [_dot_tn(_bf(ke * d), _bf(vc)) for vc, ke, d in zip(vcs, kes, decays)]
        dcols = [_row_to_col(d, eye) for d in decays]
        st = state_s[...]
        for i in range(GLA_GROUP):
            o_s[rows[i], :] = o_intra[i] + _dot(qes[i], _bf(st))
            st = st * dcols[i] + jnp.where(blockmask, kvs[i], 0.0)
        state_s[...] = st
        return carry

    return group


def _mixer_a_kernel(x_ref, gpre_ref, win_ref, wmisc_ref, wg_ref, bg_ref, glan_ref,
                    kbd_ref, vbd_ref, wo1_ref, wo2_ref, gpost_ref,
                    xo_ref, st_ref,
                    q_s, k_s, g_s, v_s, o_s, state_s, p_s, gate_s, om_s, pre_hook, hook):
    C = GLA_CHUNK
    tm = x_ref.shape[1]
    nchunk = tm // C

    @pl.when(pl.program_id(1) == 0)
    def _():
        state_s[...] = jnp.zeros_like(state_s)

    pre_hook()

    q, k, g, v, gate, o_mem = _proj_a(x_ref[0], gpre_ref, win_ref, wmisc_ref, wg_ref, bg_ref,
                                      glan_ref, (kbd_ref.at[0], vbd_ref.at[0]))
    q_s[...] = q
    k_s[...] = k
    g_s[...] = g
    v_s[...] = v
    gate_s[...] = gate
    om_s[...] = _bf(o_mem)
    hook()

    total = jnp.sum(g.reshape(nchunk, C, GLA_KW), axis=1)
    bounded = jnp.min(total) > -GLA_SAFE_DECAY

    @pl.when(bounded)
    def _():
        lax.fori_loop(0, nchunk // GLA_GROUP, _make_gla_chunk_bounded(q_s, k_s, g_s, v_s, o_s, state_s), 0)

    @pl.when(jnp.logical_not(bounded))
    def _():
        lax.fori_loop(0, nchunk, _make_gla_chunk(q_s, k_s, g_s, v_s, o_s, state_s, p_s), 0)

    st_ref[0] = state_s[...]
    o_main = _gla_out_gate(o_s[...], gate_s[...])
    xo_ref[0] = _mix_residual(x_ref[0], o_main, om_s[...], wo1_ref, wo2_ref, gpost_ref)


def _const_spec(a):
    nd = a.ndim
    return pl.BlockSpec(a.shape, lambda *_: (0,) * nd, pipeline_mode=pl.Buffered(1))


def _mixer_a_call(x, wa, kbd, vbd, make_guests):
    nb, seq, _ = x.shape
    tm = TM_PROMPT
    nt = seq // tm
    weights = [wa["gpre"], wa["win"], wa["wmisc"], wa["wg"], wa["bg"], wa["glan"]]
    tail = [wa["wo1"], wa["wo2"], wa["gpost"]]
    return _call_with_guests(
        _mixer_a_kernel,
        grid=(nb, nt),
        step_of=lambda b, t: b * nt + t,
        inputs=[x, *weights, kbd, vbd, *tail],
        in_specs=([pl.BlockSpec((1, tm, D), lambda b, t: (b, t, 0))]
                  + [_const_spec(w) for w in weights]
                  + _mem_bd_specs(0)
                  + [_const_spec(w) for w in tail]),
        out_specs=[pl.BlockSpec((1, tm, D), lambda b, t: (b, t, 0)),
                   pl.BlockSpec((1, GLA_KW, GLA_VW), lambda b, t: (b, 0, 0))],
        out_shape=[jax.ShapeDtypeStruct((nb, seq, D), F32),
                   jax.ShapeDtypeStruct((nb, GLA_KW, GLA_VW), F32)],
        scratch=[
            pltpu.VMEM((tm, GLA_KW), F32), pltpu.VMEM((tm, GLA_KW), F32), pltpu.VMEM((tm, GLA_KW), F32),
            pltpu.VMEM((tm, GLA_VW), F32), pltpu.VMEM((tm, GLA_VW), F32),
            pltpu.VMEM((GLA_KW, GLA_VW), F32),
            pltpu.VMEM((GLA_SUB * GLA_CHUNK, GLA_KW), BF16),
            pltpu.VMEM((tm, GLA_VW), F32), pltpu.VMEM((tm, MEM_W), BF16),
        ],
        guests=make_guests(nb * nt),
        name="mixer_a_prompt",
    )


def _ffn_rows(x, gpre_ref, wup_ref, wdn_ref, gpost_ref, acc_ref, arrive=None):
    arrive = arrive if arrive is not None else (lambda which, j: None)
    h = _bf(_rms(x, gpre_ref[0]))
    nff = D_FF // FF_CHUNK
    for j in range(nff - 1):
        cols = slice(j * FF_CHUNK, (j + 1) * FF_CHUNK)
        arrive(0, j)
        u = jnp.maximum(_dot(h, _bf(wup_ref[:, cols])), 0.0)
        arrive(1, j)
        d = _dot(_bf(u * u), _bf(wdn_ref[cols, :]))
        if j == 0:
            acc_ref[...] = d
        else:
            acc_ref[...] += d
    cols = slice((nff - 1) * FF_CHUNK, nff * FF_CHUNK)
    arrive(0, nff - 1)
    u = jnp.maximum(_dot(h, _bf(wup_ref[:, cols])), 0.0)
    uu = _bf(u * u)
    arrive(1, nff - 1)
    wd = _bf(wdn_ref[cols, :])
    m = x.shape[0]
    bounds = (0, m) if m < 256 else (0, m // 2, m)
    parts = []
    for lo, hi in zip(bounds[:-1], bounds[1:]):
        tot = acc_ref[lo:hi, :] + _dot(uu[lo:hi], wd)
        parts.append(x[lo:hi] + _rms(tot, gpost_ref[0]))
    return parts[0] if len(parts) == 1 else jnp.concatenate(parts, axis=0)


def _ffn_kernel(xp_ref, xs_ref, gpre_ref, wup_hbm, wdn_hbm, gpost_ref, yp_ref, ys_ref, acc_s, wup_s, wdn_s, sems,
                *, layer):
    i = pl.program_id(0)
    last = pl.num_programs(0) - 1
    nff = D_FF // FF_CHUNK

    def copy(which, j):
        cols = slice(j * FF_CHUNK, (j + 1) * FF_CHUNK)
        if which == 0:
            return pltpu.make_async_copy(wup_hbm.at[layer, :, cols], wup_s.at[:, cols], sems.at[0, j])
        return pltpu.make_async_copy(wdn_hbm.at[layer, cols, :], wdn_s.at[cols, :], sems.at[1, j])

    @pl.when(i == 0)
    def _():
        for j in range(nff):
            copy(0, j).start()
            copy(1, j).start()
        yp_ref[...] = _ffn_rows(xp_ref[...], gpre_ref, wup_s, wdn_s, gpost_ref, acc_s,
                                arrive=lambda which, j: copy(which, j).wait())

    @pl.when((i > 0) & (i < last))
    def _():
        yp_ref[...] = _ffn_rows(xp_ref[...], gpre_ref, wup_s, wdn_s, gpost_ref, acc_s)

    @pl.when(i == last)
    def _():
        ns = xs_ref.shape[0]
        ys_ref[...] = _ffn_rows(xs_ref[...], gpre_ref, wup_s, wdn_s, gpost_ref, acc_s.at[0:ns, :])


def _ffn_call(xp, xs, norm_pre, w_up, w_down, norm_post, layer):
    n, ns = xp.shape[0], xs.shape[0]
    tm = TM_PROMPT
    nt = n // tm
    gains = [a.reshape(a.shape[0], 1, D) for a in (norm_pre, norm_post)]
    gspec = pl.BlockSpec((1, 1, D), lambda i: (layer, 0, 0))
    hbm = pl.BlockSpec(memory_space=pl.ANY)
    tile = pl.BlockSpec((tm, D), lambda i: (jnp.minimum(i, nt - 1), 0))
    return pl.pallas_call(
        functools.partial(_ffn_kernel, layer=layer),
        grid=(nt + 1,),
        in_specs=[tile, _full_spec(xs), gspec, hbm, hbm, gspec],
        out_specs=[tile, _full_spec(xs)],
        out_shape=[jax.ShapeDtypeStruct((n, D), F32), jax.ShapeDtypeStruct((ns, D), F32)],
        scratch_shapes=[pltpu.VMEM((tm, D), F32), pltpu.VMEM((D, D_FF), F32), pltpu.VMEM((D_FF, D), F32),
                        pltpu.SemaphoreType.DMA((2, D_FF // FF_CHUNK))],
        compiler_params=_params(("arbitrary",)),
        name="ffn",
    )(xp, xs, gains[0], w_up, w_down, gains[1])


def _t5_bucket(dist):
    max_exact = N_BUCKETS // 2
    n = np.maximum(dist, 0)
    nf = np.maximum(n, 1).astype(np.float32)
    large = max_exact + (np.log(nf / np.float32(max_exact)) / np.float32(math.log(MAX_DISTANCE / max_exact))
                         * np.float32(N_BUCKETS - max_exact)).astype(np.int32)
    large = np.minimum(large, N_BUCKETS - 1)
    return np.where(n < max_exact, n, large)


def _swa_bias_tables():
    qi = np.arange(WINDOW)[:, None] + WINDOW
    kj = np.arange(2 * WINDOW)[None, :]
    dist = qi - kj
    valid = (dist >= 0) & (dist < WINDOW)
    return np.where(valid, _t5_bucket(dist), -1).astype(np.int32)


def _mixer_b_kernel(rb_ref, sink_ref,
                    x_ref, bkt_ref, gkv_ref, wkv_ref, gpre_ref, wq_ref, wqm_ref,
                    kbd_ref, vbd_ref, wo1_ref, wo2_ref, gpost_ref,
                    xo_ref, kc_ref, vc_ref,
                    kbuf, vbuf, q_s, o_s, bias_s, qm_s, pre_hook, hook):
    W = WINDOW
    tm = x_ref.shape[1]
    bb = pl.program_id(0)
    t = pl.program_id(1)
    nheads = SWA_G * SWA_KVH

    @pl.when((bb == 0) & (t == 0))
    def _():
        bkt = bkt_ref[...]
        own = _iota((W, 2 * W), 1) >= W
        for i in range(nheads):
            def add_bucket(n, acc):
                return jnp.where(bkt == n, rb_ref[n, i], acc)
            tab = lax.fori_loop(0, N_BUCKETS, add_bucket, jnp.zeros((W, 2 * W), F32))
            tab = jnp.where(bkt < 0, NEG_INF, tab)
            bias_s[0, i] = tab
            bias_s[1, i] = jnp.where(own, tab, NEG_INF)

    @pl.when(t == 0)
    def _():
        kbuf[0:W, :] = jnp.zeros((W, SWA_KVW), F32)
        vbuf[0:W, :] = jnp.zeros((W, SWA_KVW), F32)

    pre_hook()
    x = x_ref[0]
    xn = x * lax.rsqrt(jnp.mean(x * x, axis=-1, keepdims=True) + EPS)
    kv = _dot(_bf(xn * gkv_ref[...]), wkv_ref[...])
    kbuf[W:W + tm, :] = kv[:, :SWA_KVW]
    vbuf[W:W + tm, :] = kv[:, SWA_KVW:]
    kc_ref[0] = kv[tm - W:, :SWA_KVW].T
    vc_ref[0] = kv[tm - W:, SWA_KVW:].T

    h = _bf(xn * gpre_ref[...])
    q_s[...] = _dot(h, wq_ref[...]) * (SWA_HD ** -0.5)
    qm_s[...] = _dot(h, wqm_ref[...])
    hook()

    lane_head = _iota((W, SWA_KVW), 1) >> 6
    key_head = _iota((2 * W, SWA_KVW), 1) >> 6

    def scores(j):
        r0 = pl.multiple_of(j * W, W)
        qrows = []
        for gi in range(SWA_G):
            qg = q_s[pl.ds(r0, W), gi * SWA_KVW:(gi + 1) * SWA_KVW]
            for hh in range(SWA_KVH):
                qrows.append(jnp.where(lane_head == hh, qg, 0.0))
        return _dot_nt(_bf(jnp.concatenate(qrows, axis=0)), _bf(kbuf[pl.ds(r0, 2 * W), :]))

    def probs(j, s_all, gi):
        tab = jnp.where((j == 0) & (t == 0), 1, 0)
        ps = []
        for hh in range(SWA_KVH):
            i = gi * SWA_KVH + hh
            s = s_all[i * W:(i + 1) * W, :] + bias_s[tab, i]
            sink = sink_ref[i]
            mx = jnp.maximum(jnp.max(s, axis=-1, keepdims=True), sink)
            p = jnp.exp(s - mx)
            p = p / (jnp.sum(p, axis=-1, keepdims=True) + jnp.exp(sink - mx))
            ps.append(_bf(p))
        return jnp.concatenate(ps, axis=1)

    def pair(jj, carry):
        js = [jj * SWA_BLOCKS_PER_TRIP + i for i in range(SWA_BLOCKS_PER_TRIP)]
        s_alls = [scores(j) for j in js]
        for j, s_all in zip(js, s_alls):
            r0 = pl.multiple_of(j * W, W)
            vb = vbuf[pl.ds(r0, 2 * W), :]
            vst = _bf(jnp.concatenate([jnp.where(key_head == hh, vb, 0.0) for hh in range(SWA_KVH)], axis=0))
            for gi in range(SWA_G):
                o_s[pl.ds(r0, W), gi * SWA_KVW:(gi + 1) * SWA_KVW] = _dot(probs(j, s_all, gi), vst)
        return carry

    lax.fori_loop(0, tm // (W * SWA_BLOCKS_PER_TRIP), pair, 0)

    kbuf[0:W, :] = kbuf[tm:tm + W, :]
    vbuf[0:W, :] = vbuf[tm:tm + W, :]

    o_mem = _mem_pv(_mem_probs(qm_s[...], kbd_ref.at[0]), vbd_ref.at[0])
    xo_ref[0] = _mix_residual(x_ref[0], o_s[...], o_mem, wo1_ref, wo2_ref, gpost_ref)


def _mixer_b_call(x, wb, kbd, vbd, make_guests):
    nb, seq, _ = x.shape
    tm = TM_PROMPT
    nt = seq // tm
    bkt = jnp.asarray(_swa_bias_tables())
    head = [bkt, wb["gkv"], wb["wkv"], wb["gpre"], wb["wq"], wb["wqm"]]
    tail = [wb["wo1"], wb["wo2"], wb["gpost"]]
    smem = pl.BlockSpec(memory_space=pltpu.SMEM)
    return _call_with_guests(
        _mixer_b_kernel,
        grid=(nb, nt),
        step_of=lambda b, t: b * nt + t,
        inputs=[wb["rb"], wb["sinks"], x, *head, kbd, vbd, *tail],
        in_specs=([smem, smem, pl.BlockSpec((1, tm, D), lambda b, t: (b, t, 0))]
                  + [_const_spec(w) for w in head]
                  + _mem_bd_specs(1)
                  + [_const_spec(w) for w in tail]),
        out_specs=[pl.BlockSpec((1, tm, D), lambda b, t: (b, t, 0)),
                   pl.BlockSpec((1, SWA_KVW, WINDOW), lambda b, t: (b, 0, 0)),
                   pl.BlockSpec((1, SWA_KVW, WINDOW), lambda b, t: (b, 0, 0))],
        out_shape=[jax.ShapeDtypeStruct((nb, seq, D), F32),
                   jax.ShapeDtypeStruct((nb, SWA_KVW, WINDOW), F32),
                   jax.ShapeDtypeStruct((nb, SWA_KVW, WINDOW), F32)],
        scratch=[
            pltpu.VMEM((tm + WINDOW, SWA_KVW), F32), pltpu.VMEM((tm + WINDOW, SWA_KVW), F32),
            pltpu.VMEM((tm, SWA_QW), F32), pltpu.VMEM((tm, SWA_QW), F32),
            pltpu.VMEM((2, SWA_G * SWA_KVH, WINDOW, 2 * WINDOW), F32),
            pltpu.VMEM((tm, MEM_W), F32),
        ],
        guests=make_guests(nb * nt),
        name="mixer_b_prompt",
    )


def _row_to_col(row, eye):
    return jnp.sum(jnp.where(eye, jnp.broadcast_to(row, eye.shape), 0.0), axis=1, keepdims=True)


def _eye(n):
    return _iota((n, n), 0) == _iota((n, n), 1)


def _pre_a_kernel(x_ref, gpre_ref, win_ref, wmisc_ref, wg_ref, bg_ref, glan_ref,
                  qkgv_ref, gate_ref, qm_ref):
    q, k, g, v, gate, qm = _proj_a(x_ref[...], gpre_ref, win_ref, wmisc_ref, wg_ref, bg_ref,
                                   glan_ref)
    qkgv_ref[...] = jnp.concatenate([q, k, g, v], axis=1).T
    gate_ref[...] = gate
    qm_ref[...] = qm


def _pre_a_call(x2d, wa):
    n = x2d.shape[0]
    ws = [wa["gpre"], wa["win"], wa["wmisc"], wa["wg"], wa["bg"], wa["glan"]]
    shapes = [(3 * GLA_KW + GLA_VW, n), (n, GLA_VW), (n, MEM_W)]
    return pl.pallas_call(
        _pre_a_kernel,
        grid=(1,),
        in_specs=[_full_spec(x2d)] + [_full_spec(w) for w in ws],
        out_specs=[pl.BlockSpec(s, lambda i: (0, 0)) for s in shapes],
        out_shape=[jax.ShapeDtypeStruct(s, F32) for s in shapes],
        compiler_params=_params(("arbitrary",)),
        name="pre_a_sample",
    )(x2d, *ws)


def _gla_step_guest(qkgv_t, state5, nsteps):
    n = qkgv_t.shape[1]
    nblk = GLA_DK // GLA_DK_BLOCK
    nact = GLA_H * nblk
    assert nact <= nsteps
    act = lambda s: jnp.minimum(s, nact - 1)

    def init(step, qt_ref, kt_ref, gt_ref, vt_ref, s_ref, so_ref, ot_ref):
        @pl.when((step < nact) & (step % nblk == 0))
        def _():
            ot_ref[...] = jnp.zeros_like(ot_ref)

    def post(step, _, qt_ref, kt_ref, gt_ref, vt_ref, s_ref, so_ref, ot_ref):
        vt_blk = vt_ref[...]
        acc = jnp.zeros_like(vt_blk)
        for d in range(GLA_DK_BLOCK):
            s_new = jnp.exp(gt_ref[d:d + 1, :]) * s_ref[0, 0, d] + kt_ref[d:d + 1, :] * vt_blk
            so_ref[0, 0, d] = s_new
            acc = acc + qt_ref[d:d + 1, :] * s_new
        ot_ref[...] += jnp.where(step < nact, acc, 0.0)

    rows = lambda part: ((GLA_DK_BLOCK, n), lambda s: (part * nact + act(s), 0))
    vrow = ((GLA_DV, n), lambda s: (3 * GLA_KW // GLA_DV + act(s) // nblk, 0))
    head = ((GLA_DV, n), lambda s: (act(s) // nblk, 0))
    st = ((1, 1, GLA_DK_BLOCK, GLA_DV, n), lambda s: (0, act(s) // nblk, act(s) % nblk, 0, 0))
    return dict(inputs=[qkgv_t, qkgv_t, qkgv_t, qkgv_t, state5], in_specs=[rows(0), rows(1), rows(2), vrow, st],
                out_shape=[jax.ShapeDtypeStruct(state5.shape, F32), jax.ShapeDtypeStruct((GLA_VW, n), F32)],
                out_specs=[st, head], scratch=[], init=init, pre=lambda step, *refs: None, post=post)


def _mem_step_pre(step, qm_ref, mk_ref, mv_ref, o_ref):
    own = (_iota((8, MEM_W), 1) >> 6) == _iota((8, MEM_W), 0)
    ps = []
    for i in range(qm_ref.shape[0]):
        q8 = jnp.where(own, jnp.broadcast_to(qm_ref[i], (8, MEM_W)), 0.0)
        s = _dot(_bf(q8), _bf(mk_ref[0, i].reshape(MEM_W, N_MEM))) * (MEM_HD ** -0.5)
        e = jnp.exp(s - jnp.max(s, axis=1, keepdims=True))
        ps.append(_bf(e / jnp.sum(e, axis=1, keepdims=True)))
    return ps


def _mem_step_post(step, ps, qm_ref, mk_ref, mv_ref, o_ref):
    own = (_iota((8, MEM_W), 1) >> 6) == _iota((8, MEM_W), 0)
    for i, p in enumerate(ps):
        res = _dot_nt(p, _bf(mv_ref[0, i].reshape(MEM_W, N_MEM)))
        o_ref[i] = jnp.sum(jnp.where(own, res, 0.0), axis=0, keepdims=True)


def _mem_step_guest(qm, mk5, mv5, layer, nsteps):
    n = qm.shape[0]
    rb = n // nsteps
    blk = ((1, rb, MEM_H, MEM_HD, N_MEM), lambda s: (layer, s, 0, 0, 0))
    rows = ((rb, 1, MEM_W), lambda s: (s, 0, 0))
    return dict(inputs=[qm.reshape(n, 1, MEM_W), mk5, mv5], in_specs=[rows, blk, blk],
                out_shape=[jax.ShapeDtypeStruct((n, 1, MEM_W), F32)], out_specs=[rows], scratch=[],
                init=lambda step, *refs: None, pre=_mem_step_pre, post=_mem_step_post)


def _post_a_kernel(x_ref, ot_ref, gate_ref, om_ref, wo1_ref, wo2_ref, gpost_ref, xo_ref):
    o_main = _gla_out_gate(ot_ref[...].T, gate_ref[...])
    xo_ref[...] = _mix_residual(x_ref[...], o_main, om_ref[...], wo1_ref, wo2_ref, gpost_ref)


def _post_a_call(x2d, ot, gate, om, wa):
    args = [x2d, ot, gate, om, wa["wo1"], wa["wo2"], wa["gpost"]]
    return pl.pallas_call(
        _post_a_kernel,
        grid=(1,),
        in_specs=[_full_spec(a) for a in args],
        out_specs=_full_spec(x2d),
        out_shape=jax.ShapeDtypeStruct(x2d.shape, F32),
        compiler_params=_params(("arbitrary",)),
        name="post_a_sample",
    )(*args)


def _pre_b_kernel(x_ref, gkv_ref, wkv_ref, gpre_ref, wq_ref, wqm_ref, kv_ref, q_ref, qm_ref):
    x = x_ref[...]
    kv = _dot(_bf(_rms(x, gkv_ref[...])), wkv_ref[...])
    kv_ref[...] = jnp.concatenate([kv[:, :SWA_KVW].T, kv[:, SWA_KVW:].T], axis=0)
    h = _bf(_rms(x, gpre_ref[...]))
    q_ref[...] = _dot(h, wq_ref[...])
    qm_ref[...] = _dot(h, wqm_ref[...])


def _pre_b_call(x2d, wb):
    n = x2d.shape[0]
    ws = [wb["gkv"], wb["wkv"], wb["gpre"], wb["wq"], wb["wqm"]]
    shapes = [(2 * SWA_KVW, n), (n, SWA_QW), (n, MEM_W)]
    return pl.pallas_call(
        _pre_b_kernel,
        grid=(1,),
        in_specs=[_full_spec(x2d)] + [_full_spec(w) for w in ws],
        out_specs=[pl.BlockSpec(s, lambda i: (0, 0)) for s in shapes],
        out_shape=[jax.ShapeDtypeStruct(s, F32) for s in shapes],
        compiler_params=_params(("arbitrary",)),
        name="pre_b_sample",
    )(x2d, *ws)


def _sample_buckets():
    dist = (WINDOW - 1) - np.arange(WINDOW)
    return _t5_bucket(dist).astype(np.int32).reshape(1, WINDOW)


def _swa_step_init(step, sink_ref, rb_ref, bkt_ref, kc_ref, vc_ref, kst_ref, vst_ref, q_ref,
                   kn_ref, vn_ref, o_ref, bias_s, sink_s, s_s, p_s):
    W = WINDOW
    R = SWA_KVH * 8

    @pl.when(step == 0)
    def _():
        bkt = bkt_ref[...]
        rid = _iota((R, W), 0)
        bias = jnp.zeros((R, W), F32)
        sink = jnp.zeros((R, W), F32)
        for h in range(SWA_KVH):
            for g in range(SWA_G):
                idx = g * SWA_KVH + h
                def add_bucket(n, acc):
                    return jnp.where(bkt == n, rb_ref[n, idx], acc)
                brow = lax.fori_loop(0, N_BUCKETS, add_bucket, jnp.zeros((1, W), F32))
                bias = jnp.where(rid == h * 8 + g, brow, bias)
                sink = jnp.where(rid == h * 8 + g, sink_ref[idx], sink)
        bias_s[...] = bias
        sink_s[...] = sink


def _swa_step_pre(step, sink_ref, rb_ref, bkt_ref, kc_ref, vc_ref, kst_ref, vst_ref, q_ref,
                  kn_ref, vn_ref, o_ref, bias_s, sink_s, s_s, p_s):
    W = WINDOW
    R = SWA_KVH * 8
    rb = q_ref.shape[0]
    base = step * rb
    last = _iota((SWA_KVW, W), 1) == W - 1
    own = (_iota((R, SWA_KVW), 1) >> 6) == (_iota((R, SWA_KVW), 0) >> 3)
    kst = kst_ref[...]
    vst = vst_ref[...]
    for i in range(rb):
        shift = W - 1 - (base + i)
        kn = jnp.where(last, pltpu.roll(kst, shift, 1), pltpu.roll(kc_ref[i].reshape(SWA_KVW, W), W - 1, 1))
        vn = jnp.where(last, pltpu.roll(vst, shift, 1), pltpu.roll(vc_ref[i].reshape(SWA_KVW, W), W - 1, 1))
        kn_ref[i] = kn.reshape(SWA_KVH, SWA_HD, W)
        vn_ref[i] = vn.reshape(SWA_KVH, SWA_HD, W)
        q32 = jnp.where(own, jnp.concatenate([q_ref[i]] * SWA_KVH, axis=0), 0.0)
        s_s[i * R:(i + 1) * R, :] = _dot(_bf(q32), _bf(kn))
    s = s_s[...] * (SWA_HD ** -0.5) + jnp.concatenate([bias_s[...]] * rb, axis=0)
    sink = jnp.concatenate([sink_s[...]] * rb, axis=0)
    mx = jnp.maximum(jnp.max(s, axis=1, keepdims=True), sink)
    p = jnp.exp(s - mx)
    p_s[...] = _bf(p / (jnp.sum(p, axis=1, keepdims=True) + jnp.exp(sink - mx)))


def _swa_step_post(step, _, sink_ref, rb_ref, bkt_ref, kc_ref, vc_ref, kst_ref, vst_ref, q_ref,
                   kn_ref, vn_ref, o_ref, bias_s, sink_s, s_s, p_s):
    W = WINDOW
    R = SWA_KVH * 8
    own = (_iota((R, SWA_KVW), 1) >> 6) == (_iota((R, SWA_KVW), 0) >> 3)
    for i in range(q_ref.shape[0]):
        vn = vn_ref[i].reshape(SWA_KVW, W)
        res = jnp.where(own, _dot_nt(p_s[i * R:(i + 1) * R, :], _bf(vn)), 0.0)
        o_ref[i] = res[0:8] + res[8:16] + res[16:24] + res[24:32]


def _swa_step_guest(kc4, vc4, kv_t, q, wb, nsteps):
    n = q.shape[0]
    rb = n // nsteps
    bkt = jnp.asarray(_sample_buckets())
    q8 = jnp.pad(q.reshape(n, SWA_G, SWA_KVW), ((0, 0), (0, 8 - SWA_G), (0, 0)))
    row3 = lambda r, w: ((rb, r, w), lambda s: (s, 0, 0))
    cache = ((rb, SWA_KVH, SWA_HD, WINDOW), lambda s: (s, 0, 0, 0))
    half = lambda i: ((SWA_KVW, n), lambda s: (i, 0))
    smem = pl.BlockSpec(memory_space=pltpu.SMEM)
    return dict(inputs=[wb["sinks"], wb["rb"], bkt, kc4, vc4, kv_t, kv_t, q8],
                in_specs=[smem, smem, _full_spec(bkt), cache, cache, half(0), half(1), row3(8, SWA_KVW)],
                out_shape=[jax.ShapeDtypeStruct(kc4.shape, F32), jax.ShapeDtypeStruct(vc4.shape, F32),
                           jax.ShapeDtypeStruct((n, 8, SWA_KVW), F32)],
                out_specs=[cache, cache, row3(8, SWA_KVW)],
                scratch=[pltpu.VMEM((SWA_KVH * 8, WINDOW), F32), pltpu.VMEM((SWA_KVH * 8, WINDOW), F32),
                         pltpu.VMEM((rb * SWA_KVH * 8, WINDOW), F32),
                         pltpu.VMEM((rb * SWA_KVH * 8, WINDOW), BF16)],
                init=_swa_step_init, pre=_swa_step_pre, post=_swa_step_post)


def _post_b_kernel(x_ref, o_ref, om_ref, wo1_ref, wo2_ref, gpost_ref, xo_ref):
    xo_ref[...] = _mix_residual(x_ref[...], o_ref[...], om_ref[...], wo1_ref, wo2_ref, gpost_ref)


def _post_b_call(x2d, o, om, wb):
    args = [x2d, o, om, wb["wo1"], wb["wo2"], wb["gpost"]]
    return pl.pallas_call(
        _post_b_kernel,
        grid=(1,),
        in_specs=[_full_spec(a) for a in args],
        out_specs=_full_spec(x2d),
        out_shape=jax.ShapeDtypeStruct(x2d.shape, F32),
        compiler_params=_params(("arbitrary",)),
        name="post_b_sample",
    )(*args)


def _prep_weights(norm_mix_pre, norm_mix_post, w_in_a, w_gate_up, b_gate,
                  gla_norm, w_in_b, sinks, norm_kv, w_kv, rel_bias, w_out):
    row = lambda g: g.reshape(1, -1)
    wa_in = w_in_a[0]
    c_g = GLA_COL_GATE
    c_m = c_g + GATE_RANK
    wmisc = jnp.concatenate([wa_in[:, c_g:c_m], jnp.zeros((D, 128 - GATE_RANK), F32), wa_in[:, c_m:]], axis=1)
    wg = jnp.zeros((128, GLA_KW), F32).at[:GATE_RANK].set(w_gate_up[0])
    wa = dict(
        gpre=row(norm_mix_pre[0]), win=_bf(wa_in),
        wmisc=_bf(wmisc), wg=_bf(wg), bg=row(b_gate[0]), glan=row(jnp.tile(gla_norm[0], GLA_H)),
        wo1=_bf(w_out[0][:GLA_VW]), wo2=_bf(w_out[0][GLA_VW:]), gpost=row(norm_mix_post[0]))
    wb_in = w_in_b[0]
    wq = wb_in[:, :SWA_QW].reshape(D, SWA_KVH, SWA_G, SWA_HD).transpose(0, 2, 1, 3).reshape(D, SWA_QW)
    wo1 = w_out[1][:SWA_QW].reshape(SWA_KVH, SWA_G, SWA_HD, D).transpose(1, 0, 2, 3).reshape(SWA_QW, D)
    rb = rel_bias.reshape(N_BUCKETS, SWA_KVH, SWA_G).transpose(0, 2, 1).reshape(N_BUCKETS, SWA_G * SWA_KVH)
    sk = sinks[0].reshape(SWA_KVH, SWA_G).T.reshape(SWA_G * SWA_KVH)
    wb = dict(
        gkv=row(norm_kv), wkv=_bf(w_kv), gpre=row(norm_mix_pre[1]), wq=_bf(wq), wqm=_bf(wb_in[:, SWA_QW:]),
        wo1=_bf(wo1), wo2=_bf(w_out[1][SWA_QW:]), gpost=row(norm_mix_post[1]), rb=rb, sinks=sk)
    return wa, wb


def kernel(x_prompt, x_sample, state_gla, cache_swa_k, cache_swa_v, cache_mem_k, cache_mem_v, mem_prompt,
           norm_mix_pre, norm_mix_post, norm_ffn_pre, norm_ffn_post, norm_mem, w_mem_kv, w_in_a, w_gate_up,
           b_gate, gla_norm, w_in_b, sinks, norm_kv, w_kv, rel_bias, w_out, w_ffn_up, w_ffn_down):
    wa, wb = _prep_weights(norm_mix_pre, norm_mix_post, w_in_a, w_gate_up, b_gate, gla_norm, w_in_b, sinks,
                           norm_kv, w_kv, rel_bias, w_out)
    ffn = lambda xp, xs_, l: _ffn_call(xp, xs_, norm_ffn_pre, w_ffn_up, w_ffn_down, norm_ffn_post, l)
    nb, seq, _ = x_prompt.shape
    ns = x_sample.shape[0]

    xs = x_sample.reshape(ns, D)
    state5 = jnp.transpose(state_gla, (0, 2, 3, 4, 1))
    mk5 = jnp.transpose(cache_mem_k, (0, 1, 3, 4, 2))
    mv5 = jnp.transpose(cache_mem_v, (0, 1, 3, 4, 2))
    kc4 = jnp.transpose(cache_swa_k, (0, 2, 3, 1))
    vc4 = jnp.transpose(cache_swa_v, (0, 2, 3, 1))

    mkt, mvt, kbd, vbd = _memkv_call(mem_prompt, norm_mem, w_mem_kv)
    qkgv_t, gate, qm = _pre_a_call(xs, wa)
    (x1, st), ((state5_new, ot), (om,)) = _mixer_a_call(
        x_prompt, wa, kbd, vbd,
        lambda nsteps: [_gla_step_guest(qkgv_t, state5, nsteps),
                        _mem_step_guest(qm, mk5, mv5, 0, nsteps)])
    xs1 = _post_a_call(xs, ot, gate, om.reshape(ns, MEM_W), wa)
    x2, xs2 = ffn(x1.reshape(nb * seq, D), xs1, 0)
    kv_t, qb, qmb = _pre_b_call(xs2, wb)
    (x3, kc, vc), ((kn4, vn4, o8), (omb,)) = _mixer_b_call(
        x2.reshape(nb, seq, D), wb, kbd, vbd,
        lambda nsteps: [_swa_step_guest(kc4, vc4, kv_t, qb, wb, nsteps),
                        _mem_step_guest(qmb, mk5, mv5, 1, nsteps)])
    xs3 = _post_b_call(xs2, o8[:, :SWA_G].reshape(ns, SWA_QW), omb.reshape(ns, MEM_W), wb)
    y_prompt, y_sample = ffn(x3.reshape(nb * seq, D), xs3, 1)
    y_prompt = y_prompt.reshape(nb, seq, D)
    y_sample = y_sample.reshape(ns, 1, D)
    st4 = st.reshape(nb, GLA_H, GLA_DK, GLA_H, GLA_DV)
    state_prompt = jnp.stack([st4[:, h, :, h, :] for h in range(GLA_H)], axis=1)[None]
    to_mem = lambda t: t.reshape(2, nb, MEM_H, MEM_HD, N_MEM).transpose(0, 1, 4, 2, 3)
    to_swa = lambda t: t.reshape(nb, SWA_KVH, SWA_HD, WINDOW).transpose(0, 3, 1, 2)

    return (y_prompt, y_sample, state_prompt,
            jnp.transpose(state5_new, (0, 4, 1, 2, 3)),
            to_swa(kc), to_swa(vc),
            jnp.transpose(kn4, (0, 3, 1, 2)), jnp.transpose(vn4, (0, 3, 1, 2)),
            to_mem(mkt), to_mem(mvt))
```

```python
import functools
import math

import numpy as np
import jax
import jax.numpy as jnp
from jax import lax
from jax.experimental import pallas as pl
from jax.experimental.pallas import tpu as pltpu

F32 = jnp.float32
BF16 = jnp.bfloat16

D = 1024
D_FF = 4 * D
N_MEM = 256
MEM_H = 4
MEM_HD = 64
MEM_W = MEM_H * MEM_HD
GLA_H = 4
GLA_DK = 96
GLA_DV = 192
GLA_KW = GLA_H * GLA_DK
GLA_VW = GLA_H * GLA_DV
GATE_RANK = 16
GLA_COL_V = 2 * GLA_KW
GLA_COL_R = GLA_COL_V + GLA_VW
GLA_COL_GATE = GLA_COL_R + GLA_VW
GATE_NORM = 16.0
SWA_HD = 64
SWA_KVH = 4
SWA_G = 3
SWA_QW = SWA_KVH * SWA_G * SWA_HD
SWA_KVW = SWA_KVH * SWA_HD
WINDOW = 128
N_BUCKETS = 32
MAX_DISTANCE = 128
EPS = 1e-6

GLA_CHUNK = 64
GLA_SUB = 16
GLA_SAFE_DECAY = 60.0
GLA_GROUP = 4
SWA_BLOCKS_PER_TRIP = 2
TM_PROMPT = 512
FF_CHUNK = 512
FF_STAGE_SLOTS = 3
GLA_DK_BLOCK = 16
V7X_VMEM_LIMIT = 56 * 1024 * 1024
NEG_INF = float("-inf")


def _bf(x):
    return x.astype(BF16)


def _dot(a, b):
    return jnp.dot(a, b, preferred_element_type=F32)


def _dot_nt(a, b):
    return lax.dot_general(a, b, (((1,), (1,)), ((), ())), preferred_element_type=F32)


def _dot_tn(a, b):
    return lax.dot_general(a, b, (((0,), (0,)), ((), ())), preferred_element_type=F32)


def _rms(x, g):
    return x * lax.rsqrt(jnp.mean(x * x, axis=-1, keepdims=True) + EPS) * g


def _split3(x):
    x1 = _bf(x)
    r1 = x - x1.astype(F32)
    x2 = _bf(r1)
    x3 = _bf(r1 - x2.astype(F32))
    return x1, x2, x3


def _exact_dot(sel, x):
    x1, x2, x3 = _split3(x)
    return _dot(sel, x1) + _dot(sel, x2) + _dot(sel, x3)


def _log_sigmoid(z):
    return jnp.minimum(z, 0.0) - jnp.log1p(jnp.exp(-jnp.abs(z)))


def _silu(z):
    return z * (1.0 / (1.0 + jnp.exp(-z)))


def _iota(shape, dim):
    return lax.broadcasted_iota(jnp.int32, shape, dim)


def _gla_k_head(lane):
    one = jnp.int32(1)
    zero = jnp.int32(0)
    return (jnp.where(lane >= GLA_DK, one, zero) + jnp.where(lane >= 2 * GLA_DK, one, zero)
            + jnp.where(lane >= 3 * GLA_DK, one, zero))


def _gla_v_head(lane):
    one = jnp.int32(1)
    zero = jnp.int32(0)
    return (jnp.where(lane >= GLA_DV, one, zero) + jnp.where(lane >= 2 * GLA_DV, one, zero)
            + jnp.where(lane >= 3 * GLA_DV, one, zero))


def _full_spec(a):
    nd = a.ndim
    return pl.BlockSpec(a.shape, lambda *_: (0,) * nd)


def _mem_bd_specs(layer):
    return [pl.BlockSpec((None, 1, MEM_W, MEM_H * N_MEM), lambda b, t: (layer, b, 0, 0)),
            pl.BlockSpec((None, 1, MEM_H * N_MEM, MEM_W), lambda b, t: (layer, b, 0, 0))]


def _params(sem):
    return pltpu.CompilerParams(dimension_semantics=sem, vmem_limit_bytes=V7X_VMEM_LIMIT)


def _call_with_guests(host_kernel, grid, step_of, inputs, in_specs, out_shape, out_specs, scratch, guests, name):
    def to_spec(s):
        if isinstance(s, pl.BlockSpec):
            return s
        block, fn = s
        return pl.BlockSpec(block, lambda *ids: fn(step_of(*ids)))

    counts = [(len(inputs), len(out_shape), len(scratch))]
    counts += [(len(g["inputs"]), len(g["out_shape"]), len(g["scratch"])) for g in guests]

    def kern(*refs):
        refs = list(refs)
        parts = [[refs.pop(0) for _ in range(c[k])] for k in range(3) for c in counts]
        n = len(counts)
        ins, outs, scrs = parts[:n], parts[n:2 * n], parts[2 * n:]
        step = step_of(*[pl.program_id(a) for a in range(len(grid))])
        grefs = [(*a, *b, *c) for a, b, c in zip(ins[1:], outs[1:], scrs[1:])]
        states = []

        def pre_hook():
            for g, r in zip(guests, grefs):
                g["init"](step, *r)
            states.extend(g["pre"](step, *r) for g, r in zip(guests, grefs))

        def hook():
            for g, st, r in zip(guests, states, grefs):
                g["post"](step, st, *r)

        host_kernel(*ins[0], *outs[0], *scrs[0], pre_hook=pre_hook, hook=hook)

    res = pl.pallas_call(
        kern,
        grid=grid,
        in_specs=list(in_specs) + [to_spec(s) for g in guests for s in g["in_specs"]],
        out_specs=list(out_specs) + [to_spec(s) for g in guests for s in g["out_specs"]],
        out_shape=list(out_shape) + [s for g in guests for s in g["out_shape"]],
        scratch_shapes=list(scratch) + [s for g in guests for s in g["scratch"]],
        compiler_params=_params(("arbitrary",) * len(grid)),
        name=name,
    )(*inputs, *[a for g in guests for a in g["inputs"]])
    res = list(res)
    split = []
    for c in counts:
        split.append([res.pop(0) for _ in range(c[1])])
    return split[0], split[1:]


def _mem_probs(qm, kbd_ref):
    s = _dot(_bf(qm), kbd_ref[...]) * (MEM_HD ** -0.5)
    ps = []
    for h in range(MEM_H):
        sh = s[:, h * N_MEM:(h + 1) * N_MEM]
        e = jnp.exp(sh - jnp.max(sh, axis=-1, keepdims=True))
        ps.append(_bf(e / jnp.sum(e, axis=-1, keepdims=True)))
    return ps


def _mem_pv(ps, vbd_ref):
    out = None
    for h in range(MEM_H):
        t = _dot(ps[h], vbd_ref[h * N_MEM:(h + 1) * N_MEM, :])
        out = t if out is None else out + t
    return out


def _memkv_kernel(mem_ref, g_ref, w_ref, k_ref, v_ref, kbd_ref, vbd_ref):
    h = _bf(_rms(mem_ref[0], g_ref[0]))
    kv = _dot(h, _bf(w_ref[0]))
    k = kv[:, :MEM_W]
    v = kv[:, MEM_W:]
    kt = k.T
    k_ref[0, 0] = kt
    v_ref[0, 0] = v.T
    kt4 = jnp.concatenate([kt, kt, kt, kt], axis=1)
    keep_k = (_iota((MEM_W, MEM_H * N_MEM), 0) >> 6) == (_iota((MEM_W, MEM_H * N_MEM), 1) >> 8)
    kbd_ref[0, 0] = _bf(jnp.where(keep_k, kt4, 0.0))
    v4 = jnp.concatenate([v, v, v, v], axis=0)
    keep_v = (_iota((MEM_H * N_MEM, MEM_W), 0) >> 8) == (_iota((MEM_H * N_MEM, MEM_W), 1) >> 6)
    vbd_ref[0, 0] = _bf(jnp.where(keep_v, v4, 0.0))


def _memkv_call(mem, norm_mem, w_mem_kv):
    nb = mem.shape[0]
    nl = w_mem_kv.shape[0]
    g = norm_mem.reshape(nl, 1, D)
    return pl.pallas_call(
        _memkv_kernel,
        grid=(nl, nb),
        in_specs=[
            pl.BlockSpec((1, N_MEM, D), lambda l, b: (b, 0, 0)),
            pl.BlockSpec((1, 1, D), lambda l, b: (l, 0, 0)),
            pl.BlockSpec((1, D, 2 * MEM_W), lambda l, b: (l, 0, 0)),
        ],
        out_specs=[
            pl.BlockSpec((1, 1, MEM_W, N_MEM), lambda l, b: (l, b, 0, 0)),
            pl.BlockSpec((1, 1, MEM_W, N_MEM), lambda l, b: (l, b, 0, 0)),
            pl.BlockSpec((1, 1, MEM_W, MEM_H * N_MEM), lambda l, b: (l, b, 0, 0)),
            pl.BlockSpec((1, 1, MEM_H * N_MEM, MEM_W), lambda l, b: (l, b, 0, 0)),
        ],
        out_shape=[
            jax.ShapeDtypeStruct((nl, nb, MEM_W, N_MEM), F32),
            jax.ShapeDtypeStruct((nl, nb, MEM_W, N_MEM), F32),
            jax.ShapeDtypeStruct((nl, nb, MEM_W, MEM_H * N_MEM), BF16),
            jax.ShapeDtypeStruct((nl, nb, MEM_H * N_MEM, MEM_W), BF16),
        ],
        compiler_params=_params(("arbitrary", "arbitrary")),
        name="mem_kv",
    )(mem, g, w_mem_kv)


def _proj_a(x, gpre_ref, win_ref, wmisc_ref, wg_ref, bg_ref, glan_ref, mem_refs=None):
    h = _bf(_rms(x, gpre_ref[...]))
    misc = _dot(h, wmisc_ref[...])
    glr = misc[:, :128]
    qm = misc[:, 128:]
    ps = None if mem_refs is None else _mem_probs(qm, mem_refs[0])
    g = _log_sigmoid(_dot(_bf(glr), wg_ref[...]) + bg_ref[...]) * (1.0 / GATE_NORM)
    r = _dot(h, win_ref[:, GLA_COL_R:GLA_COL_GATE])
    gate = glan_ref[...] * _silu(r)
    qk = _dot(h, win_ref[:, :GLA_COL_V])
    q = qk[:, :GLA_KW] * (GLA_DK ** -0.5)
    k = qk[:, GLA_KW:]
    v = _dot(h, win_ref[:, GLA_COL_V:GLA_COL_R])
    return q, k, g, v, gate, (qm if mem_refs is None else _mem_pv(ps, mem_refs[1]))


def _gla_out_gate(o, gate):
    vh = _gla_v_head(_iota(o.shape, 1))
    o2 = o * o
    scale = jnp.zeros_like(o)
    for h in range(GLA_H):
        ss = jnp.sum(jnp.where(vh == h, o2, 0.0), axis=-1, keepdims=True) * (1.0 / GLA_DV)
        scale = jnp.where(vh == h, lax.rsqrt(ss + EPS), scale)
    return o * scale * gate


def _mix_residual(x, o_main, o_mem, wo1_ref, wo2_ref, gpost_ref):
    mix = _dot(_bf(o_main), wo1_ref[...]) + _dot(_bf(o_mem), wo2_ref[...])
    return x + _rms(mix, gpost_ref[...])


def _make_gla_chunk(q_s, k_s, g_s, v_s, o_s, state_s, p_s):
    C = GLA_CHUNK

    ri = _iota((C, GLA_KW), 0)
    kh = _gla_k_head(_iota((C, GLA_KW), 1))
    khcat = jnp.concatenate([kh, kh, kh], axis=1)
    vh = _gla_v_head(_iota((C, GLA_VW), 1))
    tri = _bf(jnp.where(_iota((C, C), 0) >= _iota((C, C), 1), 1.0, 0.0))
    d_rs = _iota((C, 4 * C), 0) - (_iota((C, 4 * C), 1) & (C - 1))
    band = jnp.where((d_rs >= 0) & (d_rs <= (_iota((C, 4 * C), 0) & (GLA_SUB - 1))), d_rs, -1)
    ebc = _bf(jnp.where(_gla_k_head(_iota((GLA_KW, 4 * C), 0)) == (_iota((GLA_KW, 4 * C), 1) >> 6),
                        1.0, 0.0))
    blockmask = _gla_k_head(_iota((GLA_KW, GLA_VW), 0)) == _gla_v_head(_iota((GLA_KW, GLA_VW), 1))
    eye = _iota((GLA_KW, GLA_KW), 0) == _iota((GLA_KW, GLA_KW), 1)

    def chunk(c, carry):
        r0 = pl.multiple_of(c * C, C)
        qc = q_s[pl.ds(r0, C), :]
        kc = k_s[pl.ds(r0, C), :]
        gc = g_s[pl.ds(r0, C), :]
        vc = v_s[pl.ds(r0, C), :]
        b = _exact_dot(tri, gc)

        ref1 = jnp.broadcast_to(b[31:32, :], b.shape)
        ref2 = jnp.where(ri < 32, jnp.broadcast_to(b[15:16, :], b.shape),
                         jnp.broadcast_to(b[47:48, :], b.shape))
        q1 = qc * jnp.exp(jnp.minimum(b - ref1, 0.0))
        k1 = kc * jnp.exp(jnp.minimum(ref1 - b, 0.0))
        q2 = qc * jnp.exp(jnp.minimum(b - ref2, 0.0))
        k2 = kc * jnp.exp(jnp.minimum(ref2 - b, 0.0))
        qcat = jnp.concatenate([
            jnp.where(ri >= 32, q1, 0.0),
            jnp.where((ri >= 16) & (ri < 32), q2, 0.0),
            jnp.where(ri >= 48, q2, 0.0)], axis=1)
        kcat = jnp.concatenate([
            jnp.where(ri < 32, k1, 0.0),
            jnp.where(ri < 16, k2, 0.0),
            jnp.where((ri >= 32) & (ri < 48), k2, 0.0)], axis=1)
        kst = _bf(jnp.concatenate([jnp.where(khcat == h, kcat, 0.0) for h in range(GLA_H)], axis=0))
        a_off = _dot_nt(_bf(qcat), kst)

        for dlt in range(GLA_SUB):
            if dlt == 0:
                pr = qc * kc
            else:
                kd = pltpu.roll(kc, dlt, 0)
                bd = pltpu.roll(b, dlt, 0)
                pr = qc * kd * jnp.exp(jnp.minimum(b - bd, 0.0))
            p_s[dlt * C:(dlt + 1) * C, :] = _bf(pr)
        rsum = _dot(p_s[...], ebc)
        a_diag = jnp.zeros((C, 4 * C), F32)
        for dlt in range(GLA_SUB):
            a_diag = jnp.where(band == dlt, rsum[dlt * C:(dlt + 1) * C, :], a_diag)

        vst = _bf(jnp.concatenate([jnp.where(vh == h, vc, 0.0) for h in range(GLA_H)], axis=0))
        o_intra = _dot(_bf(a_off + a_diag), vst)
        st = state_s[...]
        o_inter = _dot(_bf(qc * jnp.exp(b)), _bf(st))
        o_s[pl.ds(r0, C), :] = o_intra + o_inter

        blast = jnp.broadcast_to(b[C - 1:C, :], b.shape)
        kdec = kc * jnp.exp(blast - b)
        kv = _dot_tn(_bf(kdec), _bf(vc))
        decay = _row_to_col(jnp.exp(b[C - 1:C, :]), eye)
        state_s[...] = st * decay + jnp.where(blockmask, kv, 0.0)
        return carry

    return chunk


def _make_gla_chunk_bounded(q_s, k_s, g_s, v_s, o_s, state_s):
    C = GLA_CHUNK
    kh = _gla_k_head(_iota((C, GLA_KW), 1))
    vh = _gla_v_head(_iota((C, GLA_VW), 1))
    tri = _bf(jnp.where(_iota((C, C), 0) >= _iota((C, C), 1), 1.0, 0.0))
    causal = _iota((C, 4 * C), 0) >= (_iota((C, 4 * C), 1) & (C - 1))
    blockmask = _gla_k_head(_iota((GLA_KW, GLA_VW), 0)) == _gla_v_head(_iota((GLA_KW, GLA_VW), 1))
    eye = _iota((GLA_KW, GLA_KW), 0) == _iota((GLA_KW, GLA_KW), 1)

    def group(gi, carry):
        base = gi * (GLA_GROUP * C)
        rows = [pl.ds(pl.multiple_of(base + i * C, C), C) for i in range(GLA_GROUP)]
        bs = [_exact_dot(tri, g_s[r, :]) for r in rows]
        qes = [_bf(q_s[r, :] * jnp.exp(b)) for r, b in zip(rows, bs)]
        kes = [k_s[r, :] * jnp.exp(-b) for r, b in zip(rows, bs)]
        ksts = [_bf(jnp.concatenate([jnp.where(kh == h, ke, 0.0) for h in range(GLA_H)], axis=0)) for ke in kes]
        attn = [_bf(jnp.where(causal, _dot_nt(qe, kst), 0.0)) for qe, kst in zip(qes, ksts)]
        vcs = [v_s[r, :] for r in rows]
        vsts = [_bf(jnp.concatenate([jnp.where(vh == h, vc, 0.0) for h in range(GLA_H)], axis=0)) for vc in vcs]
        o_intra = [_dot(a, vst) for a, vst in zip(attn, vsts)]
        decays = [jnp.exp(b[C - 1:C, :]) for b in bs]
        kvs = [_dot_tn(_bf(ke * d), _bf(vc)) for vc, ke, d in zip(vcs, kes, decays)]
        dcols = [_row_to_col(d, eye) for d in decays]
        st = state_s[...]
        for i in range(GLA_GROUP):
            o_s[rows[i], :] = o_intra[i] + _dot(qes[i], _bf(st))
            st = st * dcols[i] + jnp.where(blockmask, kvs[i], 0.0)
        state_s[...] = st
        return carry

    return group


def _mixer_a_kernel(x_ref, gpre_ref, win_ref, wmisc_ref, wg_ref, bg_ref, glan_ref,
                    kbd_ref, vbd_ref, wo1_ref, wo2_ref, gpost_ref,
                    xo_ref, st_ref,
                    q_s, k_s, g_s, v_s, o_s, state_s, p_s, gate_s, om_s, pre_hook, hook):
    C = GLA_CHUNK
    tm = x_ref.shape[1]
    nchunk = tm // C

    @pl.when(pl.program_id(1) == 0)
    def _():
        state_s[...] = jnp.zeros_like(state_s)

    pre_hook()

    q, k, g, v, gate, o_mem = _proj_a(x_ref[0], gpre_ref, win_ref, wmisc_ref, wg_ref, bg_ref,
                                      glan_ref, (kbd_ref.at[0], vbd_ref.at[0]))
    q_s[...] = q
    k_s[...] = k
    g_s[...] = g
    v_s[...] = v
    gate_s[...] = gate
    om_s[...] = _bf(o_mem)
    hook()

    total = jnp.sum(g.reshape(nchunk, C, GLA_KW), axis=1)
    bounded = jnp.min(total) > -GLA_SAFE_DECAY

    @pl.when(bounded)
    def _():
        lax.fori_loop(0, nchunk // GLA_GROUP, _make_gla_chunk_bounded(q_s, k_s, g_s, v_s, o_s, state_s), 0)

    @pl.when(jnp.logical_not(bounded))
    def _():
        lax.fori_loop(0, nchunk, _make_gla_chunk(q_s, k_s, g_s, v_s, o_s, state_s, p_s), 0)

    st_ref[0] = state_s[...]
    o_main = _gla_out_gate(o_s[...], gate_s[...])
    xo_ref[0] = _mix_residual(x_ref[0], o_main, om_s[...], wo1_ref, wo2_ref, gpost_ref)


def _const_spec(a):
    nd = a.ndim
    return pl.BlockSpec(a.shape, lambda *_: (0,) * nd, pipeline_mode=pl.Buffered(1))


def _mixer_a_call(x, wa, kbd, vbd, make_guests):
    nb, seq, _ = x.shape
    tm = TM_PROMPT
    nt = seq // tm
    weights = [wa["gpre"], wa["win"], wa["wmisc"], wa["wg"], wa["bg"], wa["glan"]]
    tail = [wa["wo1"], wa["wo2"], wa["gpost"]]
    return _call_with_guests(
        _mixer_a_kernel,
        grid=(nb, nt),
        step_of=lambda b, t: b * nt + t,
        inputs=[x, *weights, kbd, vbd, *tail],
        in_specs=([pl.BlockSpec((1, tm, D), lambda b, t: (b, t, 0))]
                  + [_const_spec(w) for w in weights]
                  + _mem_bd_specs(0)
                  + [_const_spec(w) for w in tail]),
        out_specs=[pl.BlockSpec((1, tm, D), lambda b, t: (b, t, 0)),
                   pl.BlockSpec((1, GLA_KW, GLA_VW), lambda b, t: (b, 0, 0))],
        out_shape=[jax.ShapeDtypeStruct((nb, seq, D), F32),
                   jax.ShapeDtypeStruct((nb, GLA_KW, GLA_VW), F32)],
        scratch=[
            pltpu.VMEM((tm, GLA_KW), F32), pltpu.VMEM((tm, GLA_KW), F32), pltpu.VMEM((tm, GLA_KW), F32),
            pltpu.VMEM((tm, GLA_VW), F32), pltpu.VMEM((tm, GLA_VW), F32),
            pltpu.VMEM((GLA_KW, GLA_VW), F32),
            pltpu.VMEM((GLA_SUB * GLA_CHUNK, GLA_KW), BF16),
            pltpu.VMEM((tm, GLA_VW), F32), pltpu.VMEM((tm, MEM_W), BF16),
        ],
        guests=make_guests(nb * nt),
        name="mixer_a_prompt",
    )


def _ffn_rows(x, gpre_ref, wup_ref, wdn_ref, gpost_ref, acc_ref, arrive=None):
    arrive = arrive if arrive is not None else (lambda which, j: None)
    h = _bf(_rms(x, gpre_ref[0]))
    nff = D_FF // FF_CHUNK
    for j in range(nff - 1):
        cols = slice(j * FF_CHUNK, (j + 1) * FF_CHUNK)
        arrive(0, j)
        u = jnp.maximum(_dot(h, wup_ref[:, cols]), 0.0)
        arrive(1, j)
        d = _dot(_bf(u * u), wdn_ref[cols, :])
        if j == 0:
            acc_ref[...] = d
        else:
            acc_ref[...] += d
    cols = slice((nff - 1) * FF_CHUNK, nff * FF_CHUNK)
    arrive(0, nff - 1)
    u = jnp.maximum(_dot(h, wup_ref[:, cols]), 0.0)
    uu = _bf(u * u)
    arrive(1, nff - 1)
    wd = wdn_ref[cols, :]
    m = x.shape[0]
    bounds = (0, m) if m < 256 else (0, m // 2, m)
    parts = []
    for lo, hi in zip(bounds[:-1], bounds[1:]):
        tot = acc_ref[lo:hi, :] + _dot(uu[lo:hi], wd)
        parts.append(x[lo:hi] + _rms(tot, gpost_ref[0]))
    return parts[0] if len(parts) == 1 else jnp.concatenate(parts, axis=0)


def _ffn_kernel(xp_ref, xs_ref, gpre_ref, wup_hbm, wdn_hbm, gpost_ref, yp_ref, ys_ref,
                acc_s, wup_s, wdn_s, upstage_s, dnstage_s, sems, *, layer):
    i = pl.program_id(0)
    last = pl.num_programs(0) - 1
    nff = D_FF // FF_CHUNK

    def copy(which, j):
        cols = slice(j * FF_CHUNK, (j + 1) * FF_CHUNK)
        slot = j % FF_STAGE_SLOTS
        if which == 0:
            return pltpu.make_async_copy(wup_hbm.at[layer, :, cols], upstage_s.at[slot], sems.at[0, slot])
        return pltpu.make_async_copy(wdn_hbm.at[layer, cols, :], dnstage_s.at[slot], sems.at[1, slot])

    def arrive(which, j):
        cols = slice(j * FF_CHUNK, (j + 1) * FF_CHUNK)
        slot = j % FF_STAGE_SLOTS
        copy(which, j).wait()
        if which == 0:
            wup_s[:, cols] = _bf(upstage_s[slot])
        else:
            wdn_s[cols, :] = _bf(dnstage_s[slot])
        if j + FF_STAGE_SLOTS < nff:
            copy(which, j + FF_STAGE_SLOTS).start()

    @pl.when(i == 0)
    def _():
        for j in range(FF_STAGE_SLOTS):
            copy(0, j).start()
            copy(1, j).start()
        yp_ref[...] = _ffn_rows(xp_ref[...], gpre_ref, wup_s, wdn_s, gpost_ref, acc_s, arrive=arrive)

    @pl.when((i > 0) & (i < last))
    def _():
        yp_ref[...] = _ffn_rows(xp_ref[...], gpre_ref, wup_s, wdn_s, gpost_ref, acc_s)

    @pl.when(i == last)
    def _():
        ns = xs_ref.shape[0]
        ys_ref[...] = _ffn_rows(xs_ref[...], gpre_ref, wup_s, wdn_s, gpost_ref, acc_s.at[0:ns, :])


def _ffn_call(xp, xs, norm_pre, w_up, w_down, norm_post, layer):
    n, ns = xp.shape[0], xs.shape[0]
    tm = TM_PROMPT
    nt = n // tm
    gains = [a.reshape(a.shape[0], 1, D) for a in (norm_pre, norm_post)]
    gspec = pl.BlockSpec((1, 1, D), lambda i: (layer, 0, 0))
    hbm = pl.BlockSpec(memory_space=pl.ANY)
    tile = pl.BlockSpec((tm, D), lambda i: (jnp.minimum(i, nt - 1), 0))
    return pl.pallas_call(
        functools.partial(_ffn_kernel, layer=layer),
        grid=(nt + 1,),
        in_specs=[tile, _full_spec(xs), gspec, hbm, hbm, gspec],
        out_specs=[tile, _full_spec(xs)],
        out_shape=[jax.ShapeDtypeStruct((n, D), F32), jax.ShapeDtypeStruct((ns, D), F32)],
        scratch_shapes=[pltpu.VMEM((tm, D), F32), pltpu.VMEM((D, D_FF), BF16), pltpu.VMEM((D_FF, D), BF16),
                        pltpu.VMEM((FF_STAGE_SLOTS, D, FF_CHUNK), F32),
                        pltpu.VMEM((FF_STAGE_SLOTS, FF_CHUNK, D), F32),
                        pltpu.SemaphoreType.DMA((2, FF_STAGE_SLOTS))],
        compiler_params=_params(("arbitrary",)),
        name="ffn",
    )(xp, xs, gains[0], w_up, w_down, gains[1])


def _t5_bucket(dist):
    max_exact = N_BUCKETS // 2
    n = np.maximum(dist, 0)
    nf = np.maximum(n, 1).astype(np.float32)
    large = max_exact + (np.log(nf / np.float32(max_exact)) / np.float32(math.log(MAX_DISTANCE / max_exact))
                         * np.float32(N_BUCKETS - max_exact)).astype(np.int32)
    large = np.minimum(large, N_BUCKETS - 1)
    return np.where(n < max_exact, n, large)


def _swa_bias_tables():
    qi = np.arange(WINDOW)[:, None] + WINDOW
    kj = np.arange(2 * WINDOW)[None, :]
    dist = qi - kj
    valid = (dist >= 0) & (dist < WINDOW)
    return np.where(valid, _t5_bucket(dist), -1).astype(np.int32)


def _mixer_b_kernel(rb_ref, sink_ref,
                    x_ref, bkt_ref, gkv_ref, wkv_ref, gpre_ref, wq_ref, wqm_ref,
                    kbd_ref, vbd_ref, wo1_ref, wo2_ref, gpost_ref,
                    xo_ref, kc_ref, vc_ref,
                    kbuf, vbuf, q_s, o_s, bias_s, qm_s, pre_hook, hook):
    W = WINDOW
    tm = x_ref.shape[1]
    bb = pl.program_id(0)
    t = pl.program_id(1)
    nheads = SWA_G * SWA_KVH

    @pl.when((bb == 0) & (t == 0))
    def _():
        bkt = bkt_ref[...]
        own = _iota((W, 2 * W), 1) >= W
        for i in range(nheads):
            def add_bucket(n, acc):
                return jnp.where(bkt == n, rb_ref[n, i], acc)
            tab = lax.fori_loop(0, N_BUCKETS, add_bucket, jnp.zeros((W, 2 * W), F32))
            tab = jnp.where(bkt < 0, NEG_INF, tab)
            bias_s[0, i] = tab
            bias_s[1, i] = jnp.where(own, tab, NEG_INF)

    @pl.when(t == 0)
    def _():
        kbuf[0:W, :] = jnp.zeros((W, SWA_KVW), F32)
        vbuf[0:W, :] = jnp.zeros((W, SWA_KVW), F32)

    pre_hook()
    x = x_ref[0]
    xn = x * lax.rsqrt(jnp.mean(x * x, axis=-1, keepdims=True) + EPS)
    kv = _dot(_bf(xn * gkv_ref[...]), wkv_ref[...])
    kbuf[W:W + tm, :] = kv[:, :SWA_KVW]
    vbuf[W:W + tm, :] = kv[:, SWA_KVW:]
    kc_ref[0] = kv[tm - W:, :SWA_KVW].T
    vc_ref[0] = kv[tm - W:, SWA_KVW:].T

    h = _bf(xn * gpre_ref[...])
    q_s[...] = _dot(h, wq_ref[...]) * (SWA_HD ** -0.5)
    qm_s[...] = _dot(h, wqm_ref[...])
    hook()

    lane_head = _iota((W, SWA_KVW), 1) >> 6
    key_head = _iota((2 * W, SWA_KVW), 1) >> 6

    def scores(j):
        r0 = pl.multiple_of(j * W, W)
        qrows = []
        for gi in range(SWA_G):
            qg = q_s[pl.ds(r0, W), gi * SWA_KVW:(gi + 1) * SWA_KVW]
            for hh in range(SWA_KVH):
                qrows.append(jnp.where(lane_head == hh, qg, 0.0))
        return _dot_nt(_bf(jnp.concatenate(qrows, axis=0)), _bf(kbuf[pl.ds(r0, 2 * W), :]))

    def probs(j, s_all, gi):
        tab = jnp.where((j == 0) & (t == 0), 1, 0)
        ps = []
        for hh in range(SWA_KVH):
            i = gi * SWA_KVH + hh
            s = s_all[i * W:(i + 1) * W, :] + bias_s[tab, i]
            sink = sink_ref[i]
            mx = jnp.maximum(jnp.max(s, axis=-1, keepdims=True), sink)
            p = jnp.exp(s - mx)
            p = p / (jnp.sum(p, axis=-1, keepdims=True) + jnp.exp(sink - mx))
            ps.append(_bf(p))
        return jnp.concatenate(ps, axis=1)

    def pair(jj, carry):
        js = [jj * SWA_BLOCKS_PER_TRIP + i for i in range(SWA_BLOCKS_PER_TRIP)]
        s_alls = [scores(j) for j in js]
        for j, s_all in zip(js, s_alls):
            r0 = pl.multiple_of(j * W, W)
            vb = vbuf[pl.ds(r0, 2 * W), :]
            vst = _bf(jnp.concatenate([jnp.where(key_head == hh, vb, 0.0) for hh in range(SWA_KVH)], axis=0))
            for gi in range(SWA_G):
                o_s[pl.ds(r0, W), gi * SWA_KVW:(gi + 1) * SWA_KVW] = _dot(probs(j, s_all, gi), vst)
        return carry

    lax.fori_loop(0, tm // (W * SWA_BLOCKS_PER_TRIP), pair, 0)

    kbuf[0:W, :] = kbuf[tm:tm + W, :]
    vbuf[0:W, :] = vbuf[tm:tm + W, :]

    o_mem = _mem_pv(_mem_probs(qm_s[...], kbd_ref.at[0]), vbd_ref.at[0])
    xo_ref[0] = _mix_residual(x_ref[0], o_s[...], o_mem, wo1_ref, wo2_ref, gpost_ref)


def _mixer_b_call(x, wb, kbd, vbd, make_guests):
    nb, seq, _ = x.shape
    tm = TM_PROMPT
    nt = seq // tm
    bkt = jnp.asarray(_swa_bias_tables())
    head = [bkt, wb["gkv"], wb["wkv"], wb["gpre"], wb["wq"], wb["wqm"]]
    tail = [wb["wo1"], wb["wo2"], wb["gpost"]]
    smem = pl.BlockSpec(memory_space=pltpu.SMEM)
    return _call_with_guests(
        _mixer_b_kernel,
        grid=(nb, nt),
        step_of=lambda b, t: b * nt + t,
        inputs=[wb["rb"], wb["sinks"], x, *head, kbd, vbd, *tail],
        in_specs=([smem, smem, pl.BlockSpec((1, tm, D), lambda b, t: (b, t, 0))]
                  + [_const_spec(w) for w in head]
                  + _mem_bd_specs(1)
                  + [_const_spec(w) for w in tail]),
        out_specs=[pl.BlockSpec((1, tm, D), lambda b, t: (b, t, 0)),
                   pl.BlockSpec((1, SWA_KVW, WINDOW), lambda b, t: (b, 0, 0)),
                   pl.BlockSpec((1, SWA_KVW, WINDOW), lambda b, t: (b, 0, 0))],
        out_shape=[jax.ShapeDtypeStruct((nb, seq, D), F32),
                   jax.ShapeDtypeStruct((nb, SWA_KVW, WINDOW), F32),
                   jax.ShapeDtypeStruct((nb, SWA_KVW, WINDOW), F32)],
        scratch=[
            pltpu.VMEM((tm + WINDOW, SWA_KVW), F32), pltpu.VMEM((tm + WINDOW, SWA_KVW), F32),
            pltpu.VMEM((tm, SWA_QW), F32), pltpu.VMEM((tm, SWA_QW), F32),
            pltpu.VMEM((2, SWA_G * SWA_KVH, WINDOW, 2 * WINDOW), F32),
            pltpu.VMEM((tm, MEM_W), F32),
        ],
        guests=make_guests(nb * nt),
        name="mixer_b_prompt",
    )


def _row_to_col(row, eye):
    return jnp.sum(jnp.where(eye, jnp.broadcast_to(row, eye.shape), 0.0), axis=1, keepdims=True)


def _eye(n):
    return _iota((n, n), 0) == _iota((n, n), 1)


def _pre_a_kernel(x_ref, gpre_ref, win_ref, wmisc_ref, wg_ref, bg_ref, glan_ref,
                  qkgv_ref, gate_ref, qm_ref):
    q, k, g, v, gate, qm = _proj_a(x_ref[...], gpre_ref, win_ref, wmisc_ref, wg_ref, bg_ref,
                                   glan_ref)
    qkgv_ref[...] = jnp.concatenate([q, k, g, v], axis=1).T
    gate_ref[...] = gate
    qm_ref[...] = qm


def _pre_a_call(x2d, wa):
    n = x2d.shape[0]
    ws = [wa["gpre"], wa["win"], wa["wmisc"], wa["wg"], wa["bg"], wa["glan"]]
    shapes = [(3 * GLA_KW + GLA_VW, n), (n, GLA_VW), (n, MEM_W)]
    return pl.pallas_call(
        _pre_a_kernel,
        grid=(1,),
        in_specs=[_full_spec(x2d)] + [_full_spec(w) for w in ws],
        out_specs=[pl.BlockSpec(s, lambda i: (0, 0)) for s in shapes],
        out_shape=[jax.ShapeDtypeStruct(s, F32) for s in shapes],
        compiler_params=_params(("arbitrary",)),
        name="pre_a_sample",
    )(x2d, *ws)


def _gla_step_guest(qkgv_t, state5, nsteps):
    n = qkgv_t.shape[1]
    nblk = GLA_DK // GLA_DK_BLOCK
    nact = GLA_H * nblk
    assert nact <= nsteps
    act = lambda s: jnp.minimum(s, nact - 1)

    def init(step, qt_ref, kt_ref, gt_ref, vt_ref, s_ref, so_ref, ot_ref):
        @pl.when((step < nact) & (step % nblk == 0))
        def _():
            ot_ref[...] = jnp.zeros_like(ot_ref)

    def post(step, _, qt_ref, kt_ref, gt_ref, vt_ref, s_ref, so_ref, ot_ref):
        vt_blk = vt_ref[...]
        acc = jnp.zeros_like(vt_blk)
        for d in range(GLA_DK_BLOCK):
            s_new = jnp.exp(gt_ref[d:d + 1, :]) * s_ref[0, 0, d] + kt_ref[d:d + 1, :] * vt_blk
            so_ref[0, 0, d] = s_new
            acc = acc + qt_ref[d:d + 1, :] * s_new
        ot_ref[...] += jnp.where(step < nact, acc, 0.0)

    rows = lambda part: ((GLA_DK_BLOCK, n), lambda s: (part * nact + act(s), 0))
    vrow = ((GLA_DV, n), lambda s: (3 * GLA_KW // GLA_DV + act(s) // nblk, 0))
    head = ((GLA_DV, n), lambda s: (act(s) // nblk, 0))
    st = ((1, 1, GLA_DK_BLOCK, GLA_DV, n), lambda s: (0, act(s) // nblk, act(s) % nblk, 0, 0))
    return dict(inputs=[qkgv_t, qkgv_t, qkgv_t, qkgv_t, state5], in_specs=[rows(0), rows(1), rows(2), vrow, st],
                out_shape=[jax.ShapeDtypeStruct(state5.shape, F32), jax.ShapeDtypeStruct((GLA_VW, n), F32)],
                out_specs=[st, head], scratch=[], init=init, pre=lambda step, *refs: None, post=post)


def _mem_step_pre(step, qm_ref, mk_ref, mv_ref, o_ref):
    own = (_iota((8, MEM_W), 1) >> 6) == _iota((8, MEM_W), 0)
    ps = []
    for i in range(qm_ref.shape[0]):
        q8 = jnp.where(own, jnp.broadcast_to(qm_ref[i], (8, MEM_W)), 0.0)
        s = _dot(_bf(q8), _bf(mk_ref[0, i].reshape(MEM_W, N_MEM))) * (MEM_HD ** -0.5)
        e = jnp.exp(s - jnp.max(s, axis=1, keepdims=True))
        ps.append(_bf(e / jnp.sum(e, axis=1, keepdims=True)))
    return ps


def _mem_step_post(step, ps, qm_ref, mk_ref, mv_ref, o_ref):
    own = (_iota((8, MEM_W), 1) >> 6) == _iota((8, MEM_W), 0)
    for i, p in enumerate(ps):
        res = _dot_nt(p, _bf(mv_ref[0, i].reshape(MEM_W, N_MEM)))
        o_ref[i] = jnp.sum(jnp.where(own, res, 0.0), axis=0, keepdims=True)


def _mem_step_guest(qm, mk5, mv5, layer, nsteps):
    n = qm.shape[0]
    rb = n // nsteps
    blk = ((1, rb, MEM_H, MEM_HD, N_MEM), lambda s: (layer, s, 0, 0, 0))
    rows = ((rb, 1, MEM_W), lambda s: (s, 0, 0))
    return dict(inputs=[qm.reshape(n, 1, MEM_W), mk5, mv5], in_specs=[rows, blk, blk],
                out_shape=[jax.ShapeDtypeStruct((n, 1, MEM_W), F32)], out_specs=[rows], scratch=[],
                init=lambda step, *refs: None, pre=_mem_step_pre, post=_mem_step_post)


def _post_a_kernel(x_ref, ot_ref, gate_ref, om_ref, wo1_ref, wo2_ref, gpost_ref, xo_ref):
    o_main = _gla_out_gate(ot_ref[...].T, gate_ref[...])
    xo_ref[...] = _mix_residual(x_ref[...], o_main, om_ref[...], wo1_ref, wo2_ref, gpost_ref)


def _post_a_call(x2d, ot, gate, om, wa):
    args = [x2d, ot, gate, om, wa["wo1"], wa["wo2"], wa["gpost"]]
    return pl.pallas_call(
        _post_a_kernel,
        grid=(1,),
        in_specs=[_full_spec(a) for a in args],
        out_specs=_full_spec(x2d),
        out_shape=jax.ShapeDtypeStruct(x2d.shape, F32),
        compiler_params=_params(("arbitrary",)),
        name="post_a_sample",
    )(*args)


def _pre_b_kernel(x_ref, gkv_ref, wkv_ref, gpre_ref, wq_ref, wqm_ref, kv_ref, q_ref, qm_ref):
    x = x_ref[...]
    kv = _dot(_bf(_rms(x, gkv_ref[...])), wkv_ref[...])
    kv_ref[...] = jnp.concatenate([kv[:, :SWA_KVW].T, kv[:, SWA_KVW:].T], axis=0)
    h = _bf(_rms(x, gpre_ref[...]))
    q_ref[...] = _dot(h, wq_ref[...])
    qm_ref[...] = _dot(h, wqm_ref[...])


def _pre_b_call(x2d, wb):
    n = x2d.shape[0]
    ws = [wb["gkv"], wb["wkv"], wb["gpre"], wb["wq"], wb["wqm"]]
    shapes = [(2 * SWA_KVW, n), (n, SWA_QW), (n, MEM_W)]
    return pl.pallas_call(
        _pre_b_kernel,
        grid=(1,),
        in_specs=[_full_spec(x2d)] + [_full_spec(w) for w in ws],
        out_specs=[pl.BlockSpec(s, lambda i: (0, 0)) for s in shapes],
        out_shape=[jax.ShapeDtypeStruct(s, F32) for s in shapes],
        compiler_params=_params(("arbitrary",)),
        name="pre_b_sample",
    )(x2d, *ws)


def _sample_buckets():
    dist = (WINDOW - 1) - np.arange(WINDOW)
    return _t5_bucket(dist).astype(np.int32).reshape(1, WINDOW)


def _swa_step_init(step, sink_ref, rb_ref, bkt_ref, kc_ref, vc_ref, kst_ref, vst_ref, q_ref,
                   kn_ref, vn_ref, o_ref, bias_s, sink_s, s_s, p_s):
    W = WINDOW
    R = SWA_KVH * 8

    @pl.when(step == 0)
    def _():
        bkt = bkt_ref[...]
        rid = _iota((R, W), 0)
        bias = jnp.zeros((R, W), F32)
        sink = jnp.zeros((R, W), F32)
        for h in range(SWA_KVH):
            for g in range(SWA_G):
                idx = g * SWA_KVH + h
                def add_bucket(n, acc):
                    return jnp.where(bkt == n, rb_ref[n, idx], acc)
                brow = lax.fori_loop(0, N_BUCKETS, add_bucket, jnp.zeros((1, W), F32))
                bias = jnp.where(rid == h * 8 + g, brow, bias)
                sink = jnp.where(rid == h * 8 + g, sink_ref[idx], sink)
        bias_s[...] = bias
        sink_s[...] = sink


def _swa_step_pre(step, sink_ref, rb_ref, bkt_ref, kc_ref, vc_ref, kst_ref, vst_ref, q_ref,
                  kn_ref, vn_ref, o_ref, bias_s, sink_s, s_s, p_s):
    W = WINDOW
    R = SWA_KVH * 8
    rb = q_ref.shape[0]
    base = step * rb
    last = _iota((SWA_KVW, W), 1) == W - 1
    own = (_iota((R, SWA_KVW), 1) >> 6) == (_iota((R, SWA_KVW), 0) >> 3)
    kst = kst_ref[...]
    vst = vst_ref[...]
    for i in range(rb):
        shift = W - 1 - (base + i)
        kn = jnp.where(last, pltpu.roll(kst, shift, 1), pltpu.roll(kc_ref[i].reshape(SWA_KVW, W), W - 1, 1))
        vn = jnp.where(last, pltpu.roll(vst, shift, 1), pltpu.roll(vc_ref[i].reshape(SWA_KVW, W), W - 1, 1))
        kn_ref[i] = kn.reshape(SWA_KVH, SWA_HD, W)
        vn_ref[i] = vn.reshape(SWA_KVH, SWA_HD, W)
        q32 = jnp.where(own, jnp.concatenate([q_ref[i]] * SWA_KVH, axis=0), 0.0)
        s_s[i * R:(i + 1) * R, :] = _dot(_bf(q32), _bf(kn))
    s = s_s[...] * (SWA_HD ** -0.5) + jnp.concatenate([bias_s[...]] * rb, axis=0)
    sink = jnp.concatenate([sink_s[...]] * rb, axis=0)
    mx = jnp.maximum(jnp.max(s, axis=1, keepdims=True), sink)
    p = jnp.exp(s - mx)
    p_s[...] = _bf(p / (jnp.sum(p, axis=1, keepdims=True) + jnp.exp(sink - mx)))


def _swa_step_post(step, _, sink_ref, rb_ref, bkt_ref, kc_ref, vc_ref, kst_ref, vst_ref, q_ref,
                   kn_ref, vn_ref, o_ref, bias_s, sink_s, s_s, p_s):
    W = WINDOW
    R = SWA_KVH * 8
    own = (_iota((R, SWA_KVW), 1) >> 6) == (_iota((R, SWA_KVW), 0) >> 3)
    for i in range(q_ref.shape[0]):
        vn = vn_ref[i].reshape(SWA_KVW, W)
        res = jnp.where(own, _dot_nt(p_s[i * R:(i + 1) * R, :], _bf(vn)), 0.0)
        o_ref[i] = res[0:8] + res[8:16] + res[16:24] + res[24:32]


def _swa_step_guest(kc4, vc4, kv_t, q, wb, nsteps):
    n = q.shape[0]
    rb = n // nsteps
    bkt = jnp.asarray(_sample_buckets())
    q8 = jnp.pad(q.reshape(n, SWA_G, SWA_KVW), ((0, 0), (0, 8 - SWA_G), (0, 0)))
    row3 = lambda r, w: ((rb, r, w), lambda s: (s, 0, 0))
    cache = ((rb, SWA_KVH, SWA_HD, WINDOW), lambda s: (s, 0, 0, 0))
    half = lambda i: ((SWA_KVW, n), lambda s: (i, 0))
    smem = pl.BlockSpec(memory_space=pltpu.SMEM)
    return dict(inputs=[wb["sinks"], wb["rb"], bkt, kc4, vc4, kv_t, kv_t, q8],
                in_specs=[smem, smem, _full_spec(bkt), cache, cache, half(0), half(1), row3(8, SWA_KVW)],
                out_shape=[jax.ShapeDtypeStruct(kc4.shape, F32), jax.ShapeDtypeStruct(vc4.shape, F32),
                           jax.ShapeDtypeStruct((n, 8, SWA_KVW), F32)],
                out_specs=[cache, cache, row3(8, SWA_KVW)],
                scratch=[pltpu.VMEM((SWA_KVH * 8, WINDOW), F32), pltpu.VMEM((SWA_KVH * 8, WINDOW), F32),
                         pltpu.VMEM((rb * SWA_KVH * 8, WINDOW), F32),
                         pltpu.VMEM((rb * SWA_KVH * 8, WINDOW), BF16)],
                init=_swa_step_init, pre=_swa_step_pre, post=_swa_step_post)


def _post_b_kernel(x_ref, o_ref, om_ref, wo1_ref, wo2_ref, gpost_ref, xo_ref):
    xo_ref[...] = _mix_residual(x_ref[...], o_ref[...], om_ref[...], wo1_ref, wo2_ref, gpost_ref)


def _post_b_call(x2d, o, om, wb):
    args = [x2d, o, om, wb["wo1"], wb["wo2"], wb["gpost"]]
    return pl.pallas_call(
        _post_b_kernel,
        grid=(1,),
        in_specs=[_full_spec(a) for a in args],
        out_specs=_full_spec(x2d),
        out_shape=jax.ShapeDtypeStruct(x2d.shape, F32),
        compiler_params=_params(("arbitrary",)),
        name="post_b_sample",
    )(*args)


def _prep_weights(norm_mix_pre, norm_mix_post, w_in_a, w_gate_up, b_gate,
                  gla_norm, w_in_b, sinks, norm_kv, w_kv, rel_bias, w_out):
    row = lambda g: g.reshape(1, -1)
    wa_in = w_in_a[0]
    c_g = GLA_COL_GATE
    c_m = c_g + GATE_RANK
    wmisc = jnp.concatenate([wa_in[:, c_g:c_m], jnp.zeros((D, 128 - GATE_RANK), F32), wa_in[:, c_m:]], axis=1)
    wg = jnp.zeros((128, GLA_KW), F32).at[:GATE_RANK].set(w_gate_up[0])
    wa = dict(
        gpre=row(norm_mix_pre[0]), win=_bf(wa_in),
        wmisc=_bf(wmisc), wg=_bf(wg), bg=row(b_gate[0]), glan=row(jnp.tile(gla_norm[0], GLA_H)),
        wo1=_bf(w_out[0][:GLA_VW]), wo2=_bf(w_out[0][GLA_VW:]), gpost=row(norm_mix_post[0]))
    wb_in = w_in_b[0]
    wq = wb_in[:, :SWA_QW].reshape(D, SWA_KVH, SWA_G, SWA_HD).transpose(0, 2, 1, 3).reshape(D, SWA_QW)
    wo1 = w_out[1][:SWA_QW].reshape(SWA_KVH, SWA_G, SWA_HD, D).transpose(1, 0, 2, 3).reshape(SWA_QW, D)
    rb = rel_bias.reshape(N_BUCKETS, SWA_KVH, SWA_G).transpose(0, 2, 1).reshape(N_BUCKETS, SWA_G * SWA_KVH)
    sk = sinks[0].reshape(SWA_KVH, SWA_G).T.reshape(SWA_G * SWA_KVH)
    wb = dict(
        gkv=row(norm_kv), wkv=_bf(w_kv), gpre=row(norm_mix_pre[1]), wq=_bf(wq), wqm=_bf(wb_in[:, SWA_QW:]),
        wo1=_bf(wo1), wo2=_bf(w_out[1][SWA_QW:]), gpost=row(norm_mix_post[1]), rb=rb, sinks=sk)
    return wa, wb


def kernel(x_prompt, x_sample, state_gla, cache_swa_k, cache_swa_v, cache_mem_k, cache_mem_v, mem_prompt,
           norm_mix_pre, norm_mix_post, norm_ffn_pre, norm_ffn_post, norm_mem, w_mem_kv, w_in_a, w_gate_up,
           b_gate, gla_norm, w_in_b, sinks, norm_kv, w_kv, rel_bias, w_out, w_ffn_up, w_ffn_down):
    wa, wb = _prep_weights(norm_mix_pre, norm_mix_post, w_in_a, w_gate_up, b_gate, gla_norm, w_in_b, sinks,
                           norm_kv, w_kv, rel_bias, w_out)
    ffn = lambda xp, xs_, l: _ffn_call(xp, xs_, norm_ffn_pre, w_ffn_up, w_ffn_down, norm_ffn_post, l)
    nb, seq, _ = x_prompt.shape
    ns = x_sample.shape[0]

    xs = x_sample.reshape(ns, D)
    state5 = jnp.transpose(state_gla, (0, 2, 3, 4, 1))
    mk5 = jnp.transpose(cache_mem_k, (0, 1, 3, 4, 2))
    mv5 = jnp.transpose(cache_mem_v, (0, 1, 3, 4, 2))
    kc4 = jnp.transpose(cache_swa_k, (0, 2, 3, 1))
    vc4 = jnp.transpose(cache_swa_v, (0, 2, 3, 1))

    mkt, mvt, kbd, vbd = _memkv_call(mem_prompt, norm_mem, w_mem_kv)
    qkgv_t, gate, qm = _pre_a_call(xs, wa)
    (x1, st), ((state5_new, ot), (om,)) = _mixer_a_call(
        x_prompt, wa, kbd, vbd,
        lambda nsteps: [_gla_step_guest(qkgv_t, state5, nsteps),
                        _mem_step_guest(qm, mk5, mv5, 0, nsteps)])
    xs1 = _post_a_call(xs, ot, gate, om.reshape(ns, MEM_W), wa)
    x2, xs2 = ffn(x1.reshape(nb * seq, D), xs1, 0)
    kv_t, qb, qmb = _pre_b_call(xs2, wb)
    (x3, kc, vc), ((kn4, vn4, o8), (omb,)) = _mixer_b_call(
        x2.reshape(nb, seq, D), wb, kbd, vbd,
        lambda nsteps: [_swa_step_guest(kc4, vc4, kv_t, qb, wb, nsteps),
                        _mem_step_guest(qmb, mk5, mv5, 1, nsteps)])
    xs3 = _post_b_call(xs2, o8[:, :SWA_G].reshape(ns, SWA_QW), omb.reshape(ns, MEM_W), wb)
    y_prompt, y_sample = ffn(x3.reshape(nb * seq, D), xs3, 1)
    y_prompt = y_prompt.reshape(nb, seq, D)
    y_sample = y_sample.reshape(ns, 1, D)
    st4 = st.reshape(nb, GLA_H, GLA_DK, GLA_H, GLA_DV)
    state_prompt = jnp.stack([st4[:, h, :, h, :] for h in range(GLA_H)], axis=1)[None]
    to_mem = lambda t: t.reshape(2, nb, MEM_H, MEM_HD, N_MEM).transpose(0, 1, 4, 2, 3)
    to_swa = lambda t: t.reshape(nb, SWA_KVH, SWA_HD, WINDOW).transpose(0, 3, 1, 2)

    return (y_prompt, y_sample, state_prompt,
            jnp.transpose(state5_new, (0, 4, 1, 2, 3)),
            to_swa(kc), to_swa(vc),
            jnp.transpose(kn4, (0, 3, 1, 2)), jnp.transpose(vn4, (0, 3, 1, 2)),
            to_mem(mkt), to_mem(mvt))
```
